```python
import math
import jax
import jax.numpy as jnp
from jax import lax
import numpy as np

D_MODEL = 1024
BATCH = 8
SEQ = 2048
DEPTH = 2
DEC_BATCH = 128
DEC_SEQ = 1
PAST_LEN = 2048
PAGE_SIZE = 128

HEAD_DIM = 64
N_ATT_HEADS = 8
N_KV_HEADS = 2
GQA = N_ATT_HEADS // N_KV_HEADS
IDX_HEADS = 4
IDX_DIM = 64
TOPK_MAX = 256
CONV_WIDTH = 3
CONV_CH = 512
RWKV_HEADS = 8
RWKV_DIM = RWKV_HEADS * HEAD_DIM
LORA_W = 64
LORA_A = 64
LORA_G = 128
GN_EPS = 64e-5
CMP_BLOCK = 32
SEL_BLOCK = 64
CMP_PER_SEL = SEL_BLOCK // CMP_BLOCK
N_SEL_BLOCKS = 8
WINDOW = 512
N_BUCKETS = 32
MAX_DISTANCE = 128
N_EXPERTS = 16
N_GROUPS = 4
EXPERTS_PER_GROUP = N_EXPERTS // N_GROUPS
TOP_K = 2
D_EXPERT = 512
ROUTE_BLOCK = 128
Q_BLOCK = 128
N_EVEN = (DEPTH + 1) // 2
N_ODD = DEPTH // 2
RMS_EPS = 1e-6
NEG = -1e30
FORCE = 1e9
A_Q = N_ATT_HEADS * HEAD_DIM
A_KV = 2 * N_KV_HEADS * HEAD_DIM
MIX_WIDTH = A_Q + CONV_CH
EVEN_SPLITS = (A_Q, A_KV, IDX_HEADS * IDX_DIM, IDX_DIM, IDX_HEADS, CONV_CH, CONV_CH, CONV_CH)
P_EVEN = A_Q + A_KV + IDX_HEADS * IDX_DIM + IDX_DIM + IDX_HEADS + 3 * CONV_CH
C_SPLITS = (RWKV_DIM, RWKV_DIM, RWKV_DIM, LORA_W, LORA_A, LORA_G)
P_C = 3 * RWKV_DIM + LORA_W + LORA_A + LORA_G
ODD_SPLITS = (P_C, A_Q, A_KV, A_KV, A_KV, 3 * N_ATT_HEADS)
P_ODD = P_C + A_Q + 3 * A_KV + 3 * N_ATT_HEADS

kernel_name = 'hybrid_dsa_shortconv_rwkv7_nsa_moe_step'


def split_cols(z, sizes):
    return jnp.split(z, np.cumsum(sizes)[:-1].tolist(), axis=-1)


def rms_norm(x, g):
    xf = x.astype(jnp.float32)
    y = xf * lax.rsqrt(jnp.mean(xf * xf, axis=-1, keepdims=True) + RMS_EPS)
    return (y * g.astype(jnp.float32)).astype(x.dtype)


def norm_keys(kv, g):
    return jnp.stack([rms_norm(kv[:, :, 0], g), kv[:, :, 1]], axis=2)


def ada_mod(c, w, b):
    m = jax.nn.silu(c) @ w + b
    return [u[:, None, :] for u in jnp.split(m, 6, axis=-1)]


def modulate(x, g, shift, scale):
    return rms_norm(x, g) * (1 + scale) + shift


def t5_bucket(dist):
    dist = jnp.maximum(dist, 0)
    exact = N_BUCKETS // 2
    far = exact + (jnp.log(jnp.maximum(dist, 1).astype(jnp.float32) / exact)
                   / math.log(MAX_DISTANCE / exact) * (N_BUCKETS - exact)).astype(jnp.int32)
    return jnp.where(dist < exact, dist, jnp.minimum(far, N_BUCKETS - 1))


def masked_softmax(logits, mask):
    p = jax.nn.softmax(jnp.where(mask, logits, NEG), axis=-1)
    return jnp.where(mask, p, 0.0)


def to_blocks(x):
    b, s = x.shape[:2]
    return jnp.moveaxis(x.reshape(b, s // Q_BLOCK, Q_BLOCK, *x.shape[2:]), 1, 0)


def from_blocks(y):
    nb, b, qb = y.shape[:3]
    return jnp.moveaxis(y, 0, 1).reshape(b, nb * qb, *y.shape[3:])


def gather_pages(pool, page_table):
    g = pool[page_table]
    return g.reshape(g.shape[0], -1, *g.shape[3:])


def short_conv(u, buf, w, bias):
    t = u.shape[1]
    up = jnp.concatenate([buf, u], axis=1)
    y = bias + w[0] * up[:, 0:t]
    for j in range(1, CONV_WIDTH):
        y = y + w[j] * up[:, j:j + t]
    return y, up[:, -(CONV_WIDTH - 1):]


def dsa_core(q, qi, wi, t_pos, kidx, gather_kv, rel_bias, n_keep):
    b, t = q.shape[:2]
    f32 = jnp.float32
    s_pos = jnp.arange(kidx.shape[1])
    rel = jax.nn.relu(jnp.einsum('bthd,bsd->bths', qi.astype(f32), kidx.astype(f32)))
    score = jnp.einsum('bths,bth->bts', rel, wi.astype(f32)) * (IDX_HEADS * IDX_DIM) ** -0.5
    score = jnp.where(s_pos[None, None, :] <= t_pos[None, :, None], score, NEG)
    _, idx = lax.top_k(score, n_keep)
    k_sel, v_sel = gather_kv(idx)
    dist = t_pos[None, :, None] - idx
    bias = rel_bias[t5_bucket(dist)].reshape(b, t, n_keep, N_KV_HEADS, GQA).transpose(0, 1, 3, 4, 2)
    qg = q.reshape(b, t, N_KV_HEADS, GQA, HEAD_DIM)
    logits = jnp.einsum('bthgd,btnhd->bthgn', qg, k_sel).astype(f32) * HEAD_DIM ** -0.5 + bias
    p = masked_softmax(logits, (dist >= 0)[:, :, None, None, :])
    o = jnp.einsum('bthgn,btnhd->bthgd', p, v_sel.astype(f32))
    return o.reshape(b, t, A_Q).astype(q.dtype)


def even_project(h, w_in, q_norm, k_norm):
    b, t, _ = h.shape
    q, kv, qi, ki, wi, bg, cg, xin = split_cols(h @ w_in, EVEN_SPLITS)
    q = rms_norm(q.reshape(b, t, N_ATT_HEADS, HEAD_DIM), q_norm)
    kv = norm_keys(kv.reshape(b, t, 2, N_KV_HEADS, HEAD_DIM), k_norm)
    qi = qi.reshape(b, t, IDX_HEADS, IDX_DIM)
    return q, kv, qi, ki, wi, bg, cg, xin


def even_mixer_prompt(h, w_in, w_out, q_norm, k_norm, conv_w, conv_b, rel_bias):
    b, s, _ = h.shape
    q, kv, qi, ki, wi, bg, cg, xin = even_project(h, w_in, q_norm, k_norm)
    bi = jnp.arange(b)[:, None, None]

    def gather_kv(idx):
        sel = kv[bi, idx]
        return sel[:, :, :, 0], sel[:, :, :, 1]

    n_keep = min(TOPK_MAX, s // 4)

    def block(args):
        qb, qib, wib, tb = args
        return dsa_core(qb, qib, wib, tb, ki, gather_kv, rel_bias, n_keep)

    o_a = from_blocks(lax.map(block, (to_blocks(q), to_blocks(qi), to_blocks(wi),
                                      jnp.arange(s).reshape(-1, Q_BLOCK))))
    zero_buf = jnp.zeros((b, CONV_WIDTH - 1, CONV_CH), h.dtype)
    y_b, conv_state = short_conv(cg * xin, zero_buf, conv_w, conv_b)
    out = jnp.concatenate([o_a, bg * y_b], axis=-1) @ w_out
    return out, kv, ki, conv_state


def even_mixer_sample(h, kv_pool, kidx_pool, conv_buf, page_table, w_in, w_out, q_norm, k_norm,
                      conv_w, conv_b, rel_bias):
    b, t, _ = h.shape
    past = page_table.shape[1] * PAGE_SIZE
    q, kv, qi, ki, wi, bg, cg, xin = even_project(h, w_in, q_norm, k_norm)
    kidx_all = jnp.concatenate([gather_pages(kidx_pool, page_table), ki], axis=1)
    bi = jnp.arange(b)[:, None, None]

    def gather_kv(idx):
        pidx = jnp.minimum(idx, past - 1)
        sel_past = kv_pool[page_table[bi, pidx // PAGE_SIZE], pidx % PAGE_SIZE]
        sel_new = kv[bi, jnp.clip(idx - past, 0, t - 1)]
        sel = jnp.where((idx < past)[..., None, None, None], sel_past, sel_new)
        return sel[:, :, :, 0], sel[:, :, :, 1]

    n_keep = min(TOPK_MAX, (past + t) // 4)
    o_a = dsa_core(q, qi, wi, past + jnp.arange(t), kidx_all, gather_kv, rel_bias, n_keep)
    y_b, conv_state = short_conv(cg * xin, conv_buf, conv_w, conv_b)
    out = jnp.concatenate([o_a, bg * y_b], axis=-1) @ w_out
    return out, kv, ki, conv_state


def rwkv_mix(z, z_prev, wkv0, mu, w0, w_up, a0, a_up, g_up, k_k, k_a, r_k, ln_w, ln_b):
    b, t, _ = z.shape
    f32 = jnp.float32
    z_shift = jnp.concatenate([z_prev[:, None, :], z[:, :-1]], axis=1)
    zm = z + (z_shift - z) * mu
    r, k, v, wd, ad, gd = split_cols(zm, C_SPLITS)
    w_log = -jax.nn.softplus(-(w0 + jnp.tanh(wd) @ w_up).astype(f32)) - 0.5
    decay = jnp.exp(-jnp.exp(w_log))
    a = jax.nn.sigmoid((a0 + ad @ a_up).astype(f32))
    g = (jax.nn.sigmoid(gd) @ g_up).astype(f32)

    def heads(u):
        return u.astype(f32).reshape(b, t, RWKV_HEADS, HEAD_DIM)

    kk = heads(k * k_k)
    kk = kk / jnp.maximum(jnp.sqrt(jnp.sum(kk * kk, axis=-1, keepdims=True)), 1e-12)
    r, v, a, decay = heads(r), heads(v), heads(a), heads(decay)
    ka = k_a.astype(f32).reshape(RWKV_HEADS, HEAD_DIM)
    k = heads(k) * (1 + (a - 1) * ka)

    def step(S, inp):
        r_t, w_t, k_t, v_t, kk_t, a_t = inp
        sa = jnp.einsum('bhvk,bhk->bhv', S, -kk_t)
        S = (S * w_t[:, :, None, :] + sa[..., None] * (kk_t * a_t)[:, :, None, :]
             + v_t[..., None] * k_t[:, :, None, :])
        return S, jnp.einsum('bhvk,bhk->bhv', S, r_t)

    seq = tuple(jnp.moveaxis(u, 1, 0) for u in (r, decay, k, v, kk, a))
    S, y = lax.scan(step, wkv0.astype(f32), seq)
    y = jnp.moveaxis(y, 0, 1)
    y_mean = jnp.mean(y, axis=-1, keepdims=True)
    y_var = jnp.mean(jnp.square(y - y_mean), axis=-1, keepdims=True)
    yn = ((y - y_mean) * lax.rsqrt(y_var + GN_EPS)).reshape(b, t, RWKV_DIM) * ln_w + ln_b
    bonus = (jnp.sum(r * k * r_k.astype(f32), axis=-1, keepdims=True) * v).reshape(b, t, RWKV_DIM)
    out = (yn + bonus) * g
    return out.astype(z.dtype), S.astype(wkv0.dtype), z[:, -1]


def nsa_compress(rows, pe, w):
    b, L = rows.shape[:2]
    nc = L // CMP_BLOCK
    blk = rows[:, :nc * CMP_BLOCK].reshape(b, nc, CMP_BLOCK, N_KV_HEADS, HEAD_DIM) + pe[:, None, :]
    return jnp.einsum('bnrhd,rde->bnhe', blk, w.reshape(CMP_BLOCK, HEAD_DIM, HEAD_DIM))


def nsa_compress_kv(kvc, pe, w, k_norm_c):
    kc = rms_norm(nsa_compress(kvc[:, :, 0], pe[0], w[0]), k_norm_c)
    vc = nsa_compress(kvc[:, :, 1], pe[1], w[1])
    return kc, vc


def nsa_core(q, gates, t_pos, kc, vc, slc_gather, n_slc, n_sel, kw, vw, w_pos, rel_bias):
    b, t = q.shape[:2]
    f32 = jnp.float32
    scale = HEAD_DIM ** -0.5
    qg = q.reshape(b, t, N_KV_HEADS, GQA, HEAD_DIM)
    bias_hg = rel_bias.reshape(N_BUCKETS, N_KV_HEADS, GQA)
    nc = kc.shape[1]
    c_end = jnp.arange(nc) * CMP_BLOCK + CMP_BLOCK - 1
    dist_c = t_pos[:, None] - c_end[None, :]
    bias_c = jnp.moveaxis(bias_hg[t5_bucket(dist_c)], 1, -1)
    lc = jnp.einsum('bthgd,bnhd->bthgn', qg, kc).astype(f32) * scale + bias_c
    pc = masked_softmax(lc, (dist_c >= 0)[None, :, None, None, :])
    oc = jnp.einsum('bthgn,bnhd->bthgd', pc, vc.astype(f32))
    ps = jnp.sum(pc, axis=3)
    ps = jnp.pad(ps, ((0, 0), (0, 0), (0, 0), (0, n_slc * CMP_PER_SEL - nc)))
    ps = ps.reshape(b, t, N_KV_HEADS, n_slc, CMP_PER_SEL).sum(-1)
    j = jnp.arange(n_slc)
    cur = (t_pos // SEL_BLOCK)[:, None]
    forced = (j[None, :] == 0) | (j[None, :] == cur) | (j[None, :] == cur - 1)
    score = jnp.where(forced[None, :, None, :], FORCE, ps)
    score = jnp.where((j[None, :] * SEL_BLOCK <= t_pos[:, None])[None, :, None, :], score, NEG)
    _, blk = lax.top_k(score, n_sel)
    ks, vs = slc_gather(blk)
    ns = n_sel * SEL_BLOCK
    pos_s = (blk[..., None] * SEL_BLOCK + jnp.arange(SEL_BLOCK)).reshape(b, t, N_KV_HEADS, ns)
    dist_s = t_pos[None, :, None, None] - pos_s
    bias_s = jnp.moveaxis(bias_hg[t5_bucket(dist_s), jnp.arange(N_KV_HEADS)[None, None, :, None]], 3, -1)
    ls = jnp.einsum('bthgd,bthnd->bthgn', qg, ks.reshape(b, t, N_KV_HEADS, ns, HEAD_DIM)).astype(f32) * scale + bias_s
    p_s = masked_softmax(ls, (dist_s >= 0)[:, :, :, None, :])
    o_s = jnp.einsum('bthgn,bthnd->bthgd', p_s, vs.reshape(b, t, N_KV_HEADS, ns, HEAD_DIM).astype(f32))
    dist_w = t_pos[:, None] - w_pos[None, :]
    valid_w = (dist_w >= 0) & (dist_w < WINDOW) & (w_pos[None, :] >= 0)
    bias_w = jnp.moveaxis(bias_hg[t5_bucket(dist_w)], 1, -1)
    lw = jnp.einsum('bthgd,bshd->bthgs', qg, kw).astype(f32) * scale + bias_w
    pw = masked_softmax(lw, valid_w[None, :, None, None, :])
    o_w = jnp.einsum('bthgs,bshd->bthgd', pw, vw.astype(f32))
    g = gates.reshape(b, t, 3, N_KV_HEADS, GQA)[..., None].astype(f32)
    o = g[:, :, 0] * oc + g[:, :, 1] * o_s + g[:, :, 2] * o_w
    return o.reshape(b, t, A_Q).astype(q.dtype)


def odd_project(h, w_in, q_norm, k_norm):
    b, t, _ = h.shape
    zc, q, kvc, kvs, kvw, gates = split_cols(h @ w_in, ODD_SPLITS)
    kv_shape = (b, t, 2, N_KV_HEADS, HEAD_DIM)
    q = rms_norm(q.reshape(b, t, N_ATT_HEADS, HEAD_DIM), q_norm)
    kvc = kvc.reshape(kv_shape)
    kvs = norm_keys(kvs.reshape(kv_shape), k_norm[1])
    kvw = norm_keys(kvw.reshape(kv_shape), k_norm[2])
    gates = jax.nn.sigmoid(gates.reshape(b, t, 3, N_ATT_HEADS))
    return zc, q, kvc, kvs, kvw, gates


def odd_mixer_prompt(h, w_in, w_out, cp, q_norm, k_norm, cmp_pe, cmp_w, rel_bias):
    b, s, _ = h.shape
    zc, q, kvc, kvs, kvw, gates = odd_project(h, w_in, q_norm, k_norm)
    o_c, wkv, shift = rwkv_mix(zc, jnp.zeros((b, P_C), h.dtype),
                               jnp.zeros((b, RWKV_HEADS, HEAD_DIM, HEAD_DIM), h.dtype), *cp)
    kc, vc = nsa_compress_kv(kvc, cmp_pe, cmp_w, k_norm[0])
    n_slc = -(-s // SEL_BLOCK)
    n_sel = min(N_SEL_BLOCKS, n_slc)
    kvb = kvs.reshape(b, n_slc, SEL_BLOCK, 2, N_KV_HEADS, HEAD_DIM).transpose(0, 4, 1, 2, 3, 5)
    bi = jnp.arange(b)[:, None, None, None]
    hi = jnp.arange(N_KV_HEADS)[None, None, :, None]

    def slc_gather(blk):
        sel = kvb[bi, hi, blk]
        return sel[..., 0, :], sel[..., 1, :]

    kvw_pad = jnp.pad(kvw, ((0, 0), (WINDOW, 0), (0, 0), (0, 0), (0, 0)))

    def block(args):
        i, qb, gb = args
        q0 = i * Q_BLOCK
        win = lax.dynamic_slice_in_dim(kvw_pad, q0, WINDOW + Q_BLOCK, axis=1)
        w_pos = q0 - WINDOW + jnp.arange(WINDOW + Q_BLOCK)
        return nsa_core(qb, gb, q0 + jnp.arange(Q_BLOCK), kc, vc, slc_gather, n_slc, n_sel,
                        win[:, :, 0], win[:, :, 1], w_pos, rel_bias)

    o_d = from_blocks(lax.map(block, (jnp.arange(s // Q_BLOCK), to_blocks(q), to_blocks(gates))))
    out = jnp.concatenate([o_c, o_d], axis=-1) @ w_out
    return out, wkv, shift, kvc, kvs, kvw[:, -min(WINDOW, s):]


def odd_mixer_sample(h, wkv0, shift0, cmp_pool, slc_pool, win_buf, page_table, w_in, w_out, cp,
                     q_norm, k_norm, cmp_pe, cmp_w, rel_bias):
    b, t, _ = h.shape
    past = page_table.shape[1] * PAGE_SIZE
    zc, q, kvc, kvs, kvw, gates = odd_project(h, w_in, q_norm, k_norm)
    o_c, wkv, shift = rwkv_mix(zc, shift0, wkv0, *cp)
    kvc_all = jnp.concatenate([gather_pages(cmp_pool, page_table), kvc], axis=1)
    kc, vc = nsa_compress_kv(kvc_all, cmp_pe, cmp_w, k_norm[0])
    n_slc = -(-(past + t) // SEL_BLOCK)
    n_sel = min(N_SEL_BLOCKS, n_slc)
    n_past_blk = past // SEL_BLOCK
    n_new_blk = -(-t // SEL_BLOCK)
    per_page = PAGE_SIZE // SEL_BLOCK
    pool_blocks = slc_pool.reshape(slc_pool.shape[0], per_page, SEL_BLOCK, 2, N_KV_HEADS, HEAD_DIM)
    new_blocks = jnp.pad(kvs, ((0, 0), (0, n_new_blk * SEL_BLOCK - t), (0, 0), (0, 0), (0, 0)))
    new_blocks = new_blocks.reshape(b, n_new_blk, SEL_BLOCK, 2, N_KV_HEADS, HEAD_DIM)
    bi = jnp.arange(b)[:, None, None, None]
    hi = jnp.arange(N_KV_HEADS)[None, None, :, None]

    def slc_gather(blk):
        pb = jnp.minimum(blk, n_past_blk - 1)
        sel_past = pool_blocks[page_table[bi, pb // per_page], pb % per_page, :, :, hi]
        sel_new = new_blocks[bi, jnp.clip(blk - n_past_blk, 0, n_new_blk - 1), :, :, hi]
        sel = jnp.where((blk < n_past_blk)[..., None, None, None], sel_past, sel_new)
        return sel[..., 0, :], sel[..., 1, :]

    w_eff = win_buf.shape[1]
    kvw_all = jnp.concatenate([win_buf, kvw], axis=1)
    w_pos = past - w_eff + jnp.arange(w_eff + t)
    o_d = nsa_core(q, gates, past + jnp.arange(t), kc, vc, slc_gather, n_slc, n_sel,
                   kvw_all[:, :, 0], kvw_all[:, :, 1], w_pos, rel_bias)
    out = jnp.concatenate([o_c, o_d], axis=-1) @ w_out
    return out, wkv, shift, kvc, kvs, kvw_all[:, -w_eff:]


def route(x, w_router, b_router):
    n = x.shape[0]
    s = jax.nn.sigmoid(x.astype(jnp.float32) @ w_router.astype(jnp.float32))
    sel = (s + b_router.astype(jnp.float32)).reshape(n, N_GROUPS, EXPERTS_PER_GROUP)
    grp = jnp.sum(lax.top_k(sel, TOP_K)[0], axis=-1)
    g_best = jnp.argmax(grp, axis=-1)
    within = sel[jnp.arange(n), g_best]
    _, i2 = lax.top_k(within, TOP_K)
    idx = g_best[:, None] * EXPERTS_PER_GROUP + i2
    w = jnp.take_along_axis(s, idx, axis=-1)
    return idx, w / jnp.sum(w, axis=-1, keepdims=True)


def moe_ffn(h, w_router, b_router, w_gate, w_up, w_down):
    shp = h.shape
    x = h.reshape(-1, shp[-1])
    n = x.shape[0]
    idx, wts = route(x, w_router, b_router)
    flat_e = idx.reshape(-1)
    n_asg = flat_e.shape[0]
    order = jnp.argsort(flat_e)
    sorted_e = flat_e[order]
    counts = jnp.zeros((N_EXPERTS,), jnp.int32).at[flat_e].add(1)
    padded = (counts + ROUTE_BLOCK - 1) // ROUTE_BLOCK * ROUTE_BLOCK
    pad_end = jnp.cumsum(padded)
    pad_start = pad_end - padded
    start = jnp.cumsum(counts) - counts
    dest = pad_start[sorted_e] + jnp.arange(n_asg) - start[sorted_e]
    n_blocks = -(-n_asg // ROUTE_BLOCK) + N_EXPERTS
    tok = order // TOP_K
    slot_tok = jnp.zeros((n_blocks * ROUTE_BLOCK,), jnp.int32).at[dest].set(tok)
    block_e = jnp.minimum(jnp.searchsorted(pad_end, jnp.arange(n_blocks) * ROUTE_BLOCK, side='right'),
                          N_EXPERTS - 1)

    def expert_block(args):
        e, toks = args
        xb = x[toks]
        return (jax.nn.silu(xb @ w_gate[e]) * (xb @ w_up[e])) @ w_down[e]

    y_slots = lax.map(expert_block, (block_e, slot_tok.reshape(n_blocks, ROUTE_BLOCK)))
    y_asg = y_slots.reshape(n_blocks * ROUTE_BLOCK, -1)[dest] * wts.reshape(-1)[order][:, None].astype(x.dtype)
    return jax.ops.segment_sum(y_asg, tok, num_segments=n).reshape(shp)


def setup_inputs(seed: int = 0) -> dict:
    key = jax.random.key(seed)
    keys = iter(jax.random.split(key, 64))

    def nrm(shape, scale):
        return jax.random.normal(next(keys), shape, jnp.float32) * scale

    def gain(shape):
        return 1.0 + nrm(shape, 0.02)

    n_pages = PAST_LEN // PAGE_SIZE
    n_pool = (5 * DEC_BATCH * n_pages + 3) // 4
    perm = jax.random.permutation(next(keys), n_pool).astype(jnp.int32)
    page_table = perm[:DEC_BATCH * n_pages].reshape(DEC_BATCH, n_pages)
    w_eff = min(WINDOW, PAST_LEN)
    kv_row = (2, N_KV_HEADS, HEAD_DIM)
    return {
        'x_prompt': nrm((BATCH, SEQ, D_MODEL), 1.0),
        'x_sample': nrm((DEC_BATCH, DEC_SEQ, D_MODEL), 1.0),
        'c_prompt': nrm((BATCH, D_MODEL), 1.0),
        'c_sample': nrm((DEC_BATCH, D_MODEL), 1.0),
        'page_table': page_table,
        'cache_a_kv': nrm((N_EVEN, n_pool, PAGE_SIZE) + kv_row, 1.0),
        'cache_a_kidx': nrm((N_EVEN, n_pool, PAGE_SIZE, IDX_DIM), 1.0),
        'state_b_conv': nrm((N_EVEN, DEC_BATCH, CONV_WIDTH - 1, CONV_CH), 1.0),
        'state_c_wkv': nrm((N_ODD, DEC_BATCH, RWKV_HEADS, HEAD_DIM, HEAD_DIM), 0.3),
        'state_c_shift': nrm((N_ODD, DEC_BATCH, P_C), 1.0),
        'cache_d_cmp': nrm((N_ODD, n_pool, PAGE_SIZE) + kv_row, 1.0),
        'cache_d_slc': nrm((N_ODD, n_pool, PAGE_SIZE) + kv_row, 1.0),
        'cache_d_win': nrm((N_ODD, DEC_BATCH, w_eff) + kv_row, 1.0),
        'rel_bias': nrm((N_BUCKETS, N_ATT_HEADS), 0.5),
        'w_router': nrm((D_MODEL, N_EXPERTS), D_MODEL ** -0.5),
        'b_router': nrm((N_EXPERTS,), 0.01),
        'w_ada': nrm((DEPTH, D_MODEL, 6 * D_MODEL), 0.5 * D_MODEL ** -0.5),
        'b_ada': nrm((DEPTH, 6 * D_MODEL), 0.02),
        'g_norm_mix': gain((DEPTH, D_MODEL)),
        'g_norm_ffn': gain((DEPTH, D_MODEL)),
        'w_expert_gate': nrm((DEPTH, N_EXPERTS, D_MODEL, D_EXPERT), D_MODEL ** -0.5),
        'w_expert_up': nrm((DEPTH, N_EXPERTS, D_MODEL, D_EXPERT), D_MODEL ** -0.5),
        'w_expert_down': nrm((DEPTH, N_EXPERTS, D_EXPERT, D_MODEL), D_EXPERT ** -0.5),
        'e_w_in': nrm((N_EVEN, D_MODEL, P_EVEN), D_MODEL ** -0.5),
        'e_w_out': nrm((N_EVEN, MIX_WIDTH, D_MODEL), MIX_WIDTH ** -0.5),
        'a_q_norm': gain((N_EVEN, HEAD_DIM)),
        'a_k_norm': gain((N_EVEN, HEAD_DIM)),
        'b_conv_w': nrm((N_EVEN, CONV_WIDTH, CONV_CH), 0.5),
        'b_conv_b': nrm((N_EVEN, CONV_CH), 0.02),
        'o_w_in': nrm((N_ODD, D_MODEL, P_ODD), D_MODEL ** -0.5),
        'o_w_out': nrm((N_ODD, MIX_WIDTH, D_MODEL), MIX_WIDTH ** -0.5),
        'c_mu': jax.random.uniform(next(keys), (N_ODD, P_C), jnp.float32),
        'c_w0': nrm((N_ODD, RWKV_DIM), 0.5),
        'c_w_up': nrm((N_ODD, LORA_W, RWKV_DIM), 0.1),
        'c_a0': nrm((N_ODD, RWKV_DIM), 0.5),
        'c_a_up': nrm((N_ODD, LORA_A, RWKV_DIM), 0.1),
        'c_g_up': nrm((N_ODD, LORA_G, RWKV_DIM), LORA_G ** -0.5),
        'c_k_k': 0.85 + nrm((N_ODD, RWKV_DIM), 0.05),
        'c_k_a': 1.0 + nrm((N_ODD, RWKV_DIM), 0.05),
        'c_r_k': nrm((N_ODD, RWKV_HEADS, HEAD_DIM), 0.1),
        'c_ln_w': gain((N_ODD, RWKV_DIM)),
        'c_ln_b': nrm((N_ODD, RWKV_DIM), 0.02),
        'd_q_norm': gain((N_ODD, HEAD_DIM)),
        'd_k_norm': gain((N_ODD, 3, HEAD_DIM)),
        'd_cmp_pe': nrm((N_ODD, 2, CMP_BLOCK, HEAD_DIM), 0.1),
        'd_cmp_w': nrm((N_ODD, 2, CMP_BLOCK * HEAD_DIM, HEAD_DIM), (CMP_BLOCK * HEAD_DIM) ** -0.5),
    }


def reference(x_prompt, x_sample, c_prompt, c_sample, page_table,
              cache_a_kv, cache_a_kidx, state_b_conv, state_c_wkv, state_c_shift,
              cache_d_cmp, cache_d_slc, cache_d_win,
              rel_bias, w_router, b_router, w_ada, b_ada, g_norm_mix, g_norm_ffn,
              w_expert_gate, w_expert_up, w_expert_down,
              e_w_in, e_w_out, a_q_norm, a_k_norm, b_conv_w, b_conv_b,
              o_w_in, o_w_out, c_mu, c_w0, c_w_up, c_a0, c_a_up, c_g_up, c_k_k, c_k_a, c_r_k,
              c_ln_w, c_ln_b, d_q_norm, d_k_norm, d_cmp_pe, d_cmp_w):
    xp, xs = x_prompt, x_sample
    a_kv_p, a_kidx_p, b_conv_p, a_kv_s, a_kidx_s, b_conv_s = [], [], [], [], [], []
    c_wkv_p, c_shift_p, d_cmp_p, d_slc_p, d_win_p = [], [], [], [], []
    c_wkv_s, c_shift_s, d_cmp_s, d_slc_s, d_win_s = [], [], [], [], []
    for l in range(DEPTH):
        li = l // 2
        mp = ada_mod(c_prompt, w_ada[l], b_ada[l])
        ms = ada_mod(c_sample, w_ada[l], b_ada[l])
        hp = modulate(xp, g_norm_mix[l], mp[0], mp[1])
        hs = modulate(xs, g_norm_mix[l], ms[0], ms[1])
        if l % 2 == 0:
            ew = (e_w_in[li], e_w_out[li], a_q_norm[li], a_k_norm[li], b_conv_w[li], b_conv_b[li], rel_bias)
            op, kv, ki, cv = even_mixer_prompt(hp, *ew)
            a_kv_p.append(kv)
            a_kidx_p.append(ki)
            b_conv_p.append(cv)
            osm, kv, ki, cv = even_mixer_sample(hs, cache_a_kv[li], cache_a_kidx[li], state_b_conv[li],
                                                page_table, *ew)
            a_kv_s.append(kv)
            a_kidx_s.append(ki)
            b_conv_s.append(cv)
        else:
            cp = (c_mu[li], c_w0[li], c_w_up[li], c_a0[li], c_a_up[li], c_g_up[li], c_k_k[li],
                  c_k_a[li], c_r_k[li], c_ln_w[li], c_ln_b[li])
            ow = (o_w_in[li], o_w_out[li], cp, d_q_norm[li], d_k_norm[li], d_cmp_pe[li], d_cmp_w[li], rel_bias)
            op, wkv, sh, kvc, kvs, kvw = odd_mixer_prompt(hp, *ow)
            c_wkv_p.append(wkv)
            c_shift_p.append(sh)
            d_cmp_p.append(kvc)
            d_slc_p.append(kvs)
            d_win_p.append(kvw)
            osm, wkv, sh, kvc, kvs, kvw = odd_mixer_sample(hs, state_c_wkv[li], state_c_shift[li],
                                                           cache_d_cmp[li], cache_d_slc[li], cache_d_win[li],
                                                           page_table, *ow)
            c_wkv_s.append(wkv)
            c_shift_s.append(sh)
            d_cmp_s.append(kvc)
            d_slc_s.append(kvs)
            d_win_s.append(kvw)
        xp = xp + mp[2] * op
        xs = xs + ms[2] * osm
        ex = (w_router, b_router, w_expert_gate[l], w_expert_up[l], w_expert_down[l])
        xp = xp + mp[5] * moe_ffn(modulate(xp, g_norm_ffn[l], mp[3], mp[4]), *ex)
        xs = xs + ms[5] * moe_ffn(modulate(xs, g_norm_ffn[l], ms[3], ms[4]), *ex)
    return (xp, xs,
            jnp.stack(a_kv_p), jnp.stack(a_kidx_p), jnp.stack(b_conv_p),
            jnp.stack(c_wkv_p), jnp.stack(c_shift_p), jnp.stack(d_cmp_p), jnp.stack(d_slc_p), jnp.stack(d_win_p),
            jnp.stack(a_kv_s), jnp.stack(a_kidx_s), jnp.stack(b_conv_s),
            jnp.stack(c_wkv_s), jnp.stack(c_shift_s), jnp.stack(d_cmp_s), jnp.stack(d_slc_s), jnp.stack(d_win_s))
```

```python
import functools
import math

import jax
import jax.numpy as jnp
from jax import lax
from jax.experimental import pallas as pl
from jax.experimental.pallas import tpu as pltpu

F32 = jnp.float32
BF16 = jnp.bfloat16
I32 = jnp.int32

LANE = 128
HEAD_DIM = 64
N_ATT_HEADS = 8
N_KV_HEADS = 2
GQA = N_ATT_HEADS // N_KV_HEADS
IDX_HEADS = 4
IDX_DIM = 64
TOPK_MAX = 256
CONV_CH = 512
RWKV_HEADS = 8
RWKV_DIM = RWKV_HEADS * HEAD_DIM
LORA_W = 64
LORA_A = 64
LORA_G = 128
GN_EPS = 64e-5
CMP_BLOCK = 32
SEL_BLOCK = 64
N_SEL_BLOCKS = 8
WINDOW = 512
N_BUCKETS = 32
MAX_DISTANCE = 128
N_EXPERTS = 16
N_GROUPS = 4
EXPERTS_PER_GROUP = N_EXPERTS // N_GROUPS
D_EXPERT = 512
PAGE_SIZE = 128
RMS_EPS = 1e-6
NEG = -1e30
FORCE = 1e9
TAKEN = -3e38
ATT_SCALE = HEAD_DIM ** -0.5
VMEM_LIMIT = 56 * 1024 * 1024


def _cparams(*sem):
    return pltpu.CompilerParams(dimension_semantics=sem, vmem_limit_bytes=VMEM_LIMIT)


def _pick_tile(rows, pref):
    t = min(pref, rows)
    while rows % t or (t % 8 and t != rows):
        t -= 1
    return t


def _const_spec(a):
    nd = a.ndim
    return pl.BlockSpec(a.shape, lambda *_: (0,) * nd)


def _dot(a, b):
    return jnp.dot(a.astype(BF16), b.astype(BF16), preferred_element_type=F32)


def _dot_nt(a, b):
    return lax.dot_general(a.astype(BF16), b.astype(BF16), (((1,), (1,)), ((), ())),
                           preferred_element_type=F32)


def _dot_split(x, m):
    hi = x.astype(BF16)
    r1 = x - hi.astype(F32)
    mid = r1.astype(BF16)
    lo = (r1 - mid.astype(F32)).astype(BF16)
    return (jnp.dot(hi, m, preferred_element_type=F32) + jnp.dot(mid, m, preferred_element_type=F32)
            + jnp.dot(lo, m, preferred_element_type=F32))


def _sigmoid(x):
    return 1.0 / (1.0 + jnp.exp(-x))


def _silu(x):
    return x * _sigmoid(x)


def _modulate(x, g, shift, scale):
    y = x * lax.rsqrt(jnp.mean(x * x, axis=-1, keepdims=True) + RMS_EPS)
    return (y * g) * (1.0 + scale) + shift


def _group_rms(t, gmat, gain):
    ms = _dot_split(t * t, gmat)
    return (t * lax.rsqrt(ms + RMS_EPS)) * gain


def _ada_kernel(c_ref, w_ref, b_ref, o_ref):
    o_ref[...] = _dot(_silu(c_ref[...]), w_ref[...]) + b_ref[...]


def _ada(c, w_bf, b):
    r, d = c.shape
    n = w_bf.shape[1]
    tn = 512
    return pl.pallas_call(
        _ada_kernel,
        grid=(n // tn,),
        in_specs=[pl.BlockSpec((r, d), lambda j: (0, 0)),
                  pl.BlockSpec((d, tn), lambda j: (0, j)),
                  pl.BlockSpec((1, tn), lambda j: (0, j))],
        out_specs=pl.BlockSpec((r, tn), lambda j: (0, j)),
        out_shape=jax.ShapeDtypeStruct((r, n), F32),
        compiler_params=_cparams("parallel"),
        name="ada_mod",
    )(c, w_bf, b.reshape(1, n))


E_Q0, E_KV0, E_QI0, E_MISC0, E_BG0, E_CG0, E_XIN0, E_END = 0, 1024, 1280, 1792, 1920, 2432, 2944, 3456


def _inproj_even_kernel(x_ref, shift_ref, scale_ref, g_ref, w_ref, qg_ref, kg_ref, gone_ref, gtwo_ref,
                        q_ref, kv_ref, kvb_ref, qi_ref, misc_ref, miscb_ref, bg_ref, u_ref):
    h = _modulate(x_ref[...], g_ref[...], shift_ref[...], scale_ref[...])
    z = jnp.dot(h.astype(BF16), w_ref[...], preferred_element_type=F32)
    gone = gone_ref[...]
    for t in range(N_ATT_HEADS):
        sl = slice(t * LANE, (t + 1) * LANE)
        q_ref[:, sl] = _group_rms(z[:, E_Q0 + t * LANE:E_Q0 + (t + 1) * LANE], gone, qg_ref[:, sl]).astype(BF16)
    k = _group_rms(z[:, E_KV0:E_KV0 + LANE], gtwo_ref[...], kg_ref[...])
    v = z[:, E_KV0 + LANE:E_KV0 + 2 * LANE]
    kv_ref[:, 0:LANE] = k
    kv_ref[:, LANE:2 * LANE] = v
    kvb_ref[:, 0:LANE] = k.astype(BF16)
    kvb_ref[:, LANE:2 * LANE] = v.astype(BF16)
    qi_ref[...] = z[:, E_QI0:E_MISC0].astype(BF16)
    misc = z[:, E_MISC0:E_BG0]
    misc_ref[...] = misc
    miscb_ref[...] = misc.astype(BF16)
    bg_ref[...] = z[:, E_BG0:E_CG0]
    u_ref[...] = z[:, E_CG0:E_XIN0] * z[:, E_XIN0:E_END]


def _row_call(kernel, xs, mods, consts, outs, tile, tpb, name, outs_t=()):
    if not isinstance(xs, (list, tuple)):
        xs = [xs]
    rows = xs[0].shape[0]
    n_tiles = rows // tile
    in_specs = [pl.BlockSpec((tile, x.shape[1]), lambda t: (t, 0)) for x in xs]
    for m in mods:
        in_specs.append(pl.BlockSpec((None,) + m.shape[1:], lambda t: (t // tpb, 0, 0)))
    in_specs += [_const_spec(c) for c in consts]
    out_specs = [pl.BlockSpec((tile, w), lambda t: (t, 0)) for (w, _) in outs]
    out_shape = [jax.ShapeDtypeStruct((rows, w), dt) for (w, dt) in outs]
    out_specs += [pl.BlockSpec((hh, tile), lambda t: (0, t)) for (hh, _) in outs_t]
    out_shape += [jax.ShapeDtypeStruct((hh, rows), dt) for (hh, dt) in outs_t]
    return pl.pallas_call(kernel, grid=(n_tiles,), in_specs=in_specs, out_specs=out_specs, out_shape=out_shape,
                          compiler_params=_cparams("parallel"), name=name)(*xs, *mods, *consts)


def _pad_q_cols(wq):
    d = wq.shape[0]
    w = wq.reshape(d, N_ATT_HEADS, HEAD_DIM)
    z = jnp.zeros_like(w)
    lo = jnp.concatenate([w, z], -1)
    hi = jnp.concatenate([z, w], -1)
    sel = (jnp.arange(N_ATT_HEADS) >= GQA)[None, :, None]
    return jnp.where(sel, hi, lo).reshape(d, N_ATT_HEADS * LANE)


def _pad_o_rows(wo):
    return _pad_q_cols(wo.T).T


def _gmats():
    i = jnp.arange(LANE)
    gone = jnp.full((LANE, LANE), 1.0 / HEAD_DIM, F32).astype(BF16)
    gtwo = jnp.where((i[:, None] // HEAD_DIM) == (i[None, :] // HEAD_DIM), 1.0 / HEAD_DIM, 0.0).astype(BF16)
    tri = jnp.where(i[:, None] <= i[None, :], 1.0, 0.0).astype(BF16)
    return gone, gtwo, tri


def _even_weights(w_in, w_out, q_norm, k_norm):
    d = w_in.shape[0]
    a_q, a_kv = N_ATT_HEADS * HEAD_DIM, 2 * N_KV_HEADS * HEAD_DIM
    o = 0
    wq = w_in[:, o:o + a_q]; o += a_q
    wkv = w_in[:, o:o + a_kv]; o += a_kv
    wqi = w_in[:, o:o + IDX_HEADS * IDX_DIM]; o += IDX_HEADS * IDX_DIM
    wki = w_in[:, o:o + IDX_DIM]; o += IDX_DIM
    wwi = w_in[:, o:o + IDX_HEADS]; o += IDX_HEADS
    wrest = w_in[:, o:]
    wqi = jnp.concatenate([wqi.reshape(d, IDX_HEADS, IDX_DIM), jnp.zeros((d, IDX_HEADS, LANE - IDX_DIM), F32)],
                          -1).reshape(d, IDX_HEADS * LANE)
    wmisc = jnp.concatenate([wki, wwi, jnp.zeros((d, LANE - IDX_DIM - IDX_HEADS), F32)], -1)
    w_in_p = jnp.concatenate([_pad_q_cols(wq), wkv, wqi, wmisc, wrest], -1).astype(BF16)
    w_out_p = jnp.concatenate([_pad_o_rows(w_out[:a_q]), w_out[a_q:]], 0).astype(BF16)
    qg = jnp.tile(q_norm, 2 * N_ATT_HEADS).reshape(1, N_ATT_HEADS * LANE)
    kg = jnp.tile(k_norm, 2).reshape(1, LANE)
    return w_in_p, w_out_p, qg, kg


def _inproj_even(x, mods, g, wts, gm, tile, tpb):
    w_in_p, _, qg, kg = wts
    gone, gtwo, _ = gm
    outs = [(N_ATT_HEADS * LANE, BF16), (2 * LANE, F32), (2 * LANE, BF16), (IDX_HEADS * LANE, BF16),
            (LANE, F32), (LANE, BF16), (CONV_CH, F32), (CONV_CH, F32)]
    return _row_call(_inproj_even_kernel, x, mods, [g, w_in_p, qg, kg, gone, gtwo], outs, tile, tpb, "inproj_even")


def _t5_bucket(dist):
    dist = jnp.maximum(dist, 0)
    exact = N_BUCKETS // 2
    far = exact + (jnp.log(jnp.maximum(dist, 1).astype(F32) / exact)
                   / math.log(MAX_DISTANCE / exact) * (N_BUCKETS - exact)).astype(I32)
    return jnp.where(dist < exact, dist, jnp.minimum(far, N_BUCKETS - 1))


def _bias_tiles(rel_bias, qb):
    r = jnp.arange(qb)[:, None]
    c = jnp.arange(LANE)[None, :]
    tiles = [rel_bias[_t5_bucket(d * LANE + r - c)] for d in range(3)]
    return jnp.stack(tiles).transpose(0, 3, 1, 2)


def _flash_step(qs, kb, vb, bias, msk, m, l, acc):
    s = _dot_nt(qs, kb) * ATT_SCALE + bias
    s = jnp.where(msk, s, NEG)
    m_new = jnp.maximum(m, jnp.max(s, axis=1, keepdims=True))
    p = jnp.where(msk, jnp.exp(s - m_new), 0.0)
    alpha = jnp.exp(m - m_new)
    l = alpha * l + jnp.sum(p, axis=1, keepdims=True)
    acc = alpha * acc + jnp.dot(p.astype(BF16), vb, preferred_element_type=F32)
    return m_new, l, acc


def _flash_init(rows):
    return (jnp.full((rows, 1), NEG, F32), jnp.zeros((rows, 1), F32), jnp.zeros((rows, LANE), F32))


def _stack_heads(q_ref, hk):
    return jnp.concatenate([q_ref[:, (hk * GQA + g) * LANE:(hk * GQA + g + 1) * LANE] for g in range(GQA)], axis=0)


def _tile4(x):
    return jnp.concatenate([x] * GQA, axis=0)


def _dsa_kernel(q_ref, qi_ref, misc_ref, kidx_ref, kv_ref, bias_ref, tri_ref, o_ref, key_s, *, qb, q_base, n_keep):
    i = pl.program_id(1)
    q0 = q_base + i * qb
    nblk = (q0 + qb - 1) // LANE + 1
    dq = q0 // LANE
    row = lax.broadcasted_iota(I32, (qb, LANE), 0)
    lane = lax.broadcasted_iota(I32, (qb, LANE), 1)
    t_pos = q0 + row
    misc = misc_ref[...]
    wis = [jnp.broadcast_to(misc[:, IDX_DIM + h:IDX_DIM + h + 1], (qb, LANE)) for h in range(IDX_HEADS)]
    qi = qi_ref[...]
    idx_scale = (IDX_HEADS * IDX_DIM) ** -0.5

    def pass_a(j, c):
        kb = kidx_ref[pl.ds(pl.multiple_of(j * LANE, LANE), LANE), :]
        acc = jnp.zeros((qb, LANE), F32)
        for h in range(IDX_HEADS):
            acc = acc + jnp.maximum(_dot_nt(qi[:, h * LANE:(h + 1) * LANE], kb), 0.0) * wis[h]
        sc = acc * idx_scale
        sc = jnp.where(j * LANE + lane <= t_pos, sc, NEG)
        sc = jnp.where(sc == 0.0, 0.0, sc)
        bits = lax.bitcast_convert_type(sc, I32)
        key_s[j] = jnp.where(bits < 0, bits ^ jnp.int32(0x7FFFFFFF), bits)
        return c

    lax.fori_loop(0, nblk, pass_a, 0)

    def count(pred):
        def body(j, a):
            return a + jnp.where(pred(key_s[j]), 1.0, 0.0)
        a = lax.fori_loop(0, nblk, body, jnp.zeros((qb, LANE), F32))
        return jnp.sum(a, axis=1, keepdims=True)

    keep = float(n_keep)
    int_min = jnp.int32(-2 ** 31)
    thr = jnp.where(count(lambda k: k >= 0) >= keep, jnp.int32(0), int_min) + jnp.zeros((qb, 1), I32)

    def search(it, thr):
        cand = thr | lax.shift_left(jnp.int32(1), jnp.int32(30) - it)
        return jnp.where(count(lambda k: k >= cand) >= keep, cand, thr)

    thr = lax.fori_loop(0, 31, search, thr)
    need = keep - count(lambda k: k > thr)
    tri = tri_ref[...]

    def pass_c(j, run):
        key = key_s[j]
        eq = key == thr
        eqf = jnp.where(eq, 1.0, 0.0)
        cum = jnp.dot(eqf.astype(BF16), tri, preferred_element_type=F32) + run
        sel = (key > thr) | (eq & (cum <= need))
        sel = sel & (j * LANE + lane <= t_pos)
        key_s[j] = jnp.where(sel, 1, 0)
        return run + jnp.sum(eqf, axis=1, keepdims=True)

    lax.fori_loop(0, nblk, pass_c, jnp.zeros((qb, 1), F32))

    qs = [_stack_heads(q_ref, hk) for hk in range(N_KV_HEADS)]

    def pass_d(j, carry):
        off = pl.multiple_of(j * LANE, LANE)
        kb = kv_ref[pl.ds(off, LANE), 0:LANE]
        vb = kv_ref[pl.ds(off, LANE), LANE:2 * LANE]
        msk = _tile4(key_s[j] > 0)
        dsel = jnp.minimum(dq - j, 2)
        out = []
        for hk in range(N_KV_HEADS):
            bias = bias_ref[dsel, hk * GQA:(hk + 1) * GQA].reshape(GQA * qb, LANE)
            out.append(_flash_step(qs[hk], kb, vb, bias, msk, *carry[hk]))
        return tuple(out)

    res = lax.fori_loop(0, nblk, pass_d, tuple(_flash_init(GQA * qb) for _ in range(N_KV_HEADS)))
    _write_heads(o_ref, [acc / l for (_, l, acc) in res], qb)


def _write_heads(o_ref, outs, qb):
    lane = lax.broadcasted_iota(I32, (qb, LANE), 1)
    for hk in range(N_KV_HEADS):
        valid = (lane // HEAD_DIM) == hk
        for g in range(GQA):
            h = hk * GQA + g
            o_ref[:, h * LANE:(h + 1) * LANE] = jnp.where(valid, outs[hk][g * qb:(g + 1) * qb], 0.0).astype(BF16)


def _dsa(q, qi, misc, kidx_b, kv_b, bias, tri, *, qb, q_base, n_keep):
    b, rows, _ = q.shape
    nq = rows // qb
    s = kv_b.shape[1]
    kern = functools.partial(_dsa_kernel, qb=qb, q_base=q_base, n_keep=n_keep)
    qspec = lambda w: pl.BlockSpec((None, qb, w), lambda bi, i: (bi, i, 0))
    kspec = lambda w: pl.BlockSpec((None, s, w), lambda bi, i: (bi, 0, 0))
    return pl.pallas_call(
        kern,
        grid=(b, nq),
        in_specs=[qspec(N_ATT_HEADS * LANE), qspec(IDX_HEADS * LANE), qspec(LANE), kspec(LANE), kspec(2 * LANE),
                  _const_spec(bias), _const_spec(tri)],
        out_specs=qspec(N_ATT_HEADS * LANE),
        out_shape=jax.ShapeDtypeStruct((b, rows, N_ATT_HEADS * LANE), BF16),
        scratch_shapes=[pltpu.VMEM((s // LANE, qb, LANE), I32)],
        compiler_params=_cparams("parallel", "parallel"),
        name="dsa_attention",
    )(q, qi, misc, kidx_b, kv_b, bias, tri)


def _gather_kernel(pt_ref, *refs, n_pages, rpp, w_in, w_out):
    del pt_ref
    new_ref, o_ref = refs[n_pages], refs[n_pages + 1]
    for p in range(n_pages):
        v = refs[p][...]
        if w_in < w_out:
            v = jnp.concatenate([v, jnp.zeros((rpp, w_out - w_in), F32)], axis=1)
        o_ref[p * rpp:(p + 1) * rpp, :] = v.astype(o_ref.dtype)
    row = lax.broadcasted_iota(I32, (rpp, w_out), 0)
    tail = jnp.where(row == 0, jnp.broadcast_to(new_ref[...], (rpp, w_out)), 0.0)
    o_ref[n_pages * rpp:(n_pages + 1) * rpp, :] = tail.astype(o_ref.dtype)


def _gather_pages(pool, page_table, new_rows, w_out, out_dtype):
    b, n_pages = page_table.shape
    rpp, w_in = pool.shape[1], pool.shape[2]
    kern = functools.partial(_gather_kernel, n_pages=n_pages, rpp=rpp, w_in=w_in, w_out=w_out)
    in_specs = [pl.BlockSpec((None, rpp, w_in), lambda bi, pt, p=p: (pt[bi, p], 0, 0)) for p in range(n_pages)]
    in_specs.append(pl.BlockSpec((None, 1, w_out), lambda bi, pt: (bi, 0, 0)))
    s = (n_pages + 1) * rpp
    return pl.pallas_call(
        kern,
        grid_spec=pltpu.PrefetchScalarGridSpec(
            num_scalar_prefetch=1, grid=(b,), in_specs=in_specs,
            out_specs=pl.BlockSpec((None, s, w_out), lambda bi, pt: (bi, 0, 0))),
        out_shape=jax.ShapeDtypeStruct((b, s, w_out), out_dtype),
        compiler_params=_cparams("parallel"),
        name="gather_pages",
    )(page_table, *([pool] * n_pages), new_rows)


def _route(hf, wrt, br):
    logits = lax.dot_general(wrt, hf, (((1,), (1,)), ((), ())), precision=lax.Precision.HIGHEST,
                             preferred_element_type=F32)
    s = _sigmoid(logits)
    sel = s + br
    rows = [sel[e:e + 1, :] for e in range(N_EXPERTS)]
    grp = []
    for g in range(N_GROUPS):
        a = rows[g * EXPERTS_PER_GROUP:(g + 1) * EXPERTS_PER_GROUP]
        best = None
        for i in range(EXPERTS_PER_GROUP):
            for j in range(i + 1, EXPERTS_PER_GROUP):
                v = a[i] + a[j]
                best = v if best is None else jnp.maximum(best, v)
        grp.append(best)
    gbest = jnp.zeros_like(grp[0], dtype=I32)
    cur = grp[0]
    for g in range(1, N_GROUPS):
        better = grp[g] > cur
        gbest = jnp.where(better, g, gbest)
        cur = jnp.where(better, grp[g], cur)
    picked = []
    for g in range(N_GROUPS):
        a = rows[g * EXPERTS_PER_GROUP:(g + 1) * EXPERTS_PER_GROUP]
        for j in range(EXPERTS_PER_GROUP):
            rank = jnp.zeros_like(a[j])
            for jj in range(EXPERTS_PER_GROUP):
                if jj != j:
                    ahead = (a[jj] > a[j]) | (a[jj] == a[j]) if jj < j else (a[jj] > a[j])
                    rank = rank + jnp.where(ahead, 1.0, 0.0)
            e = g * EXPERTS_PER_GROUP + j
            picked.append(jnp.where((gbest == g) & (rank < 2.0), s[e:e + 1, :], 0.0))
    den = picked[0]
    for p in picked[1:]:
        den = den + p
    return jnp.concatenate([p / den for p in picked], axis=0)


def _post_tail(x, mix, gate, gf, shf, scf, wrt_ref, br_ref, x2_ref, hf_ref, cwt_ref):
    x2 = x + gate * mix
    x2_ref[...] = x2
    hf = _modulate(x2, gf, shf, scf)
    hf_ref[...] = hf.astype(BF16)
    cwt_ref[...] = _route(hf, wrt_ref[...], br_ref[...])


def _post_even_kernel(x_ref, oa_ref, bg_ref, u_ref, um1_ref, um2_ref, gate_ref, shf_ref, scf_ref,
                      gf_ref, cw_ref, cb_ref, wo_ref, wrt_ref, br_ref, x2_ref, hf_ref, cwt_ref):
    cw = cw_ref[...]
    y = cb_ref[...] + cw[0:1, :] * um2_ref[...]
    y = y + cw[1:2, :] * um1_ref[...]
    y = y + cw[2:3, :] * u_ref[...]
    n_a = N_ATT_HEADS * LANE
    mix = (jnp.dot(oa_ref[...], wo_ref[0:n_a, :], preferred_element_type=F32)
           + jnp.dot((bg_ref[...] * y).astype(BF16), wo_ref[n_a:n_a + CONV_CH, :], preferred_element_type=F32))
    _post_tail(x_ref[...], mix, gate_ref[...], gf_ref[...], shf_ref[...], scf_ref[...], wrt_ref, br_ref,
               x2_ref, hf_ref, cwt_ref)


POST_OUTS = [(1024, F32), (1024, BF16)]


def _moe_kernel(hf_ref, cw_ref, x2_ref, gate_ref, wg_ref, wu_ref, wd_ref, o_ref, acc_ref):
    e = pl.program_id(1)

    @pl.when(e == 0)
    def _():
        acc_ref[...] = jnp.zeros_like(acc_ref)

    hf = hf_ref[...]
    hmid = _silu(jnp.dot(hf, wg_ref[...], preferred_element_type=F32)) * jnp.dot(hf, wu_ref[...],
                                                                                 preferred_element_type=F32)
    cw = cw_ref[...]
    lane = lax.broadcasted_iota(I32, cw.shape, 1)
    wcol = jnp.sum(jnp.where(lane == e, cw, 0.0), axis=1, keepdims=True)
    acc_ref[...] += jnp.dot((hmid * wcol).astype(BF16), wd_ref[...], preferred_element_type=F32)

    @pl.when(e == N_EXPERTS - 1)
    def _():
        o_ref[...] = x2_ref[...] + gate_ref[...] * acc_ref[...]


def _moe(hf, cw, x2, gate, wg, wu, wd, tile, tpb):
    rows, d = x2.shape
    de = wg.shape[2]
    return pl.pallas_call(
        _moe_kernel,
        grid=(rows // tile, N_EXPERTS),
        in_specs=[pl.BlockSpec((tile, d), lambda t, e: (t, 0)),
                  pl.BlockSpec((tile, N_EXPERTS), lambda t, e: (t, 0)),
                  pl.BlockSpec((tile, d), lambda t, e: (t, 0)),
                  pl.BlockSpec((None,) + gate.shape[1:], lambda t, e: (t // tpb, 0, 0)),
                  pl.BlockSpec((None, d, de), lambda t, e: (e, 0, 0)),
                  pl.BlockSpec((None, d, de), lambda t, e: (e, 0, 0)),
                  pl.BlockSpec((None, de, d), lambda t, e: (e, 0, 0))],
        out_specs=pl.BlockSpec((tile, d), lambda t, e: (t, 0)),
        out_shape=jax.ShapeDtypeStruct((rows, d), F32),
        scratch_shapes=[pltpu.VMEM((tile, d), F32)],
        compiler_params=_cparams("parallel", "arbitrary"),
        name="moe_dense",
    )(hf, cw, x2, gate, wg, wu, wd)


def _shift_rows(u3, k):
    return jnp.pad(u3, ((0, 0), (k, 0), (0, 0)))[:, :u3.shape[1]]


def _even_layer(xp, xs, mp, ms, page_table, kv_pool, kidx_pool, conv_buf, rel_bias, router, experts,
                g_mix, g_ffn, w_in, w_out, q_norm, k_norm, conv_w, conv_b):
    B, S, D = xp.shape
    Bs = xs.shape[0]
    past = page_table.shape[1] * PAGE_SIZE
    wts = _even_weights(w_in, w_out, q_norm, k_norm)
    gm = _gmats()
    wrt, br = router
    wg, wu, wd = experts
    g_mix = g_mix.reshape(1, D)
    g_ffn = g_ffn.reshape(1, D)
    post_consts = [g_ffn, conv_w, conv_b.reshape(1, CONV_CH), wts[1], wrt, br]

    tp = 256
    q_p, kv_f, kv_b, qi_p, misc, misc_b, bg, u = _inproj_even(xp.reshape(B * S, D), [mp[0], mp[1]], g_mix, wts, gm,
                                                              tp, S // tp)
    r3 = lambda a: a.reshape(B, S, a.shape[-1])
    oa = _dsa(r3(q_p), r3(qi_p), r3(misc), r3(misc_b), r3(kv_b), _bias_tiles(rel_bias, LANE), gm[2],
              qb=LANE, q_base=0, n_keep=min(TOPK_MAX, S // 4))
    u3 = r3(u)
    um1 = _shift_rows(u3, 1).reshape(B * S, CONV_CH)
    um2 = _shift_rows(u3, 2).reshape(B * S, CONV_CH)
    x2, hf, cwt = _row_call(_post_even_kernel, [xp.reshape(B * S, D), oa.reshape(B * S, -1), bg, u, um1, um2],
                            [mp[2], mp[3], mp[4]], post_consts, POST_OUTS, tp, S // tp, "post_even",
                            outs_t=[(N_EXPERTS, F32)])
    tm = 512
    xp3 = _moe(hf, cwt.T, x2, mp[5], wg, wu, wd, tm, S // tm).reshape(B, S, D)
    outs_p = (kv_f.reshape(B, S, 2, N_KV_HEADS, HEAD_DIM), misc[:, :IDX_DIM].reshape(B, S, IDX_DIM), u3[:, S - 2:])

    q_s, kv_fs, _, qi_s, misc_s, _, bg_s, u_s = _inproj_even(xs, [ms[0], ms[1]], g_mix, wts, gm, Bs, 1)
    kidx_all = _gather_pages(kidx_pool, page_table, misc_s[:, None, :], LANE, BF16)
    kv_all = _gather_pages(kv_pool, page_table, kv_fs[:, None, :], 2 * LANE, BF16)
    qb = 8
    pad8 = lambda a: jnp.pad(a[:, None, :], ((0, 0), (0, qb - 1), (0, 0)))
    oa_s = _dsa(pad8(q_s), pad8(qi_s), pad8(misc_s), kidx_all, kv_all, _bias_tiles(rel_bias, qb), gm[2],
                qb=qb, q_base=past, n_keep=min(TOPK_MAX, (past + 1) // 4))[:, 0]
    x2s, hfs, cwts = _row_call(_post_even_kernel, [xs, oa_s, bg_s, u_s, conv_buf[:, 1], conv_buf[:, 0]],
                               [ms[2], ms[3], ms[4]], post_consts, POST_OUTS, Bs, 1, "post_even_s",
                               outs_t=[(N_EXPERTS, F32)])
    xs3 = _moe(hfs, cwts.T, x2s, ms[5], wg, wu, wd, Bs, 1)
    outs_s = (kv_fs.reshape(Bs, 1, 2, N_KV_HEADS, HEAD_DIM), misc_s[:, None, :IDX_DIM],
              jnp.concatenate([conv_buf[:, 1:], u_s[:, None, :]], axis=1))
    return xp3, xs3, outs_p, outs_s


O_Z0, O_Q0, O_KVC0, O_KVS0, O_KVW0, O_G0, O_END = 0, 1792, 2816, 3072, 3328, 3584, 3712
P_C = 3 * RWKV_DIM + LORA_W + LORA_A + LORA_G


def _inproj_odd_kernel(x_ref, shift_ref, scale_ref, g_ref, w_ref, qg_ref, ksg_ref, kwg_ref, gone_ref, gtwo_ref,
                       zc_ref, q_ref, kvc_ref, kvs_ref, kvsb_ref, kvw_ref, kvwb_ref, gates_ref):
    h = _modulate(x_ref[...], g_ref[...], shift_ref[...], scale_ref[...])
    z = jnp.dot(h.astype(BF16), w_ref[...], preferred_element_type=F32)
    zc_ref[...] = z[:, O_Z0:O_Q0]
    gone = gone_ref[...]
    gtwo = gtwo_ref[...]
    for t in range(N_ATT_HEADS):
        sl = slice(t * LANE, (t + 1) * LANE)
        q_ref[:, sl] = _group_rms(z[:, O_Q0 + t * LANE:O_Q0 + (t + 1) * LANE], gone, qg_ref[:, sl]).astype(BF16)
    kvc_ref[...] = z[:, O_KVC0:O_KVS0]
    for base, gain_ref, f_ref, b_ref in ((O_KVS0, ksg_ref, kvs_ref, kvsb_ref), (O_KVW0, kwg_ref, kvw_ref, kvwb_ref)):
        k = _group_rms(z[:, base:base + LANE], gtwo, gain_ref[...])
        v = z[:, base + LANE:base + 2 * LANE]
        f_ref[:, 0:LANE] = k
        f_ref[:, LANE:2 * LANE] = v
        b_ref[:, 0:LANE] = k.astype(BF16)
        b_ref[:, LANE:2 * LANE] = v.astype(BF16)
    gates_ref[...] = _sigmoid(z[:, O_G0:O_END])


def _odd_weights(w_in, w_out, q_norm, k_norm):
    d = w_in.shape[0]
    a_q, a_kv = N_ATT_HEADS * HEAD_DIM, 2 * N_KV_HEADS * HEAD_DIM
    o = P_C
    wz = w_in[:, :o]
    wq = w_in[:, o:o + a_q]; o += a_q
    wkv = w_in[:, o:o + 3 * a_kv]; o += 3 * a_kv
    wg = w_in[:, o:]
    wg = jnp.concatenate([wg, jnp.zeros((d, LANE - wg.shape[1]), F32)], -1)
    w_in_p = jnp.concatenate([wz, _pad_q_cols(wq), wkv, wg], -1).astype(BF16)
    w_out_p = jnp.concatenate([w_out[:RWKV_DIM], _pad_o_rows(w_out[RWKV_DIM:])], 0).astype(BF16)
    qg = jnp.tile(q_norm, 2 * N_ATT_HEADS).reshape(1, N_ATT_HEADS * LANE)
    ksg = jnp.tile(k_norm[1], 2).reshape(1, LANE)
    kwg = jnp.tile(k_norm[2], 2).reshape(1, LANE)
    return w_in_p, w_out_p, qg, ksg, kwg


def _inproj_odd(x, mods, g, wts, gm, tile, tpb):
    w_in_p, _, qg, ksg, kwg = wts
    gone, gtwo, _ = gm
    outs = [(P_C, F32), (N_ATT_HEADS * LANE, BF16), (2 * LANE, F32), (2 * LANE, F32), (2 * LANE, BF16),
            (2 * LANE, F32), (2 * LANE, BF16), (LANE, F32)]
    return _row_call(_inproj_odd_kernel, x, mods, [g, w_in_p, qg, ksg, kwg, gone, gtwo], outs, tile, tpb,
                     "inproj_odd")


def _rwkv_pre_kernel(z_ref, zp_ref, mu_ref, w0_ref, a0_ref, kk_ref, ka_ref, wup_ref, aup_ref, gup_ref, gsum_ref,
                     r_o, w_o, k_o, v_o, kk_o, kka_o, g_o):
    z = z_ref[...]
    zm = z + (zp_ref[...] - z) * mu_ref[...]
    r = zm[:, 0:RWKV_DIM]
    k = zm[:, RWKV_DIM:2 * RWKV_DIM]
    v = zm[:, 2 * RWKV_DIM:3 * RWKV_DIM]
    t12 = zm[:, 3 * RWKV_DIM:3 * RWKV_DIM + LANE]
    gd = zm[:, 3 * RWKV_DIM + LANE:P_C]
    xw = w0_ref[...] + _dot(jnp.tanh(t12), wup_ref[...])
    sp = jnp.maximum(-xw, 0.0) + jnp.log(1.0 + jnp.exp(-jnp.abs(xw)))
    w_o[...] = jnp.exp(-jnp.exp(-sp - 0.5))
    a = _sigmoid(a0_ref[...] + _dot(t12, aup_ref[...]))
    g_o[...] = _dot(_sigmoid(gd), gup_ref[...])
    kk = k * kk_ref[...]
    gsum = gsum_ref[...]
    for t in range(RWKV_DIM // LANE):
        sl = slice(t * LANE, (t + 1) * LANE)
        kt = kk[:, sl]
        nrm = jnp.maximum(jnp.sqrt(_dot_split(kt * kt, gsum)), 1e-12)
        kn = kt / nrm
        kk_o[:, sl] = kn
        kka_o[:, sl] = kn * a[:, sl]
    r_o[...] = r
    v_o[...] = v
    k_o[...] = k * (1.0 + (a - 1.0) * ka_ref[...])


def _rwkv_pre(zc, zprev, cpar, gsum, tile):
    mu, w0, w_up, a0, a_up, g_up, k_k, k_a = cpar
    z64 = jnp.zeros((LORA_W, RWKV_DIM), F32)
    consts = [mu.reshape(1, P_C), w0.reshape(1, -1), a0.reshape(1, -1), k_k.reshape(1, -1), k_a.reshape(1, -1),
              jnp.concatenate([w_up, z64], 0).astype(BF16), jnp.concatenate([z64, a_up], 0).astype(BF16),
              g_up.astype(BF16), gsum]
    return _row_call(_rwkv_pre_kernel, [zc, zprev], [], consts, [(RWKV_DIM, F32)] * 7, tile, 1, "rwkv_pre")


SCAN_P = 64
SCAN_VH = HEAD_DIM // 2


def _scan_kernel(kk_ref, w_ref, kka_ref, k_ref, r_ref, v_ref, s0_ref, y_ref, so_ref, st, *, tc):
    ti = pl.program_id(1)

    @pl.when(ti == 0)
    def _():
        st[...] = s0_ref[...]

    def step(t, c):
        kk, w, kka, kt, rt, vt = kk_ref[t], w_ref[t], kka_ref[t], k_ref[t], r_ref[t], v_ref[t]
        ys = []
        for vi in range(SCAN_VH):
            s = st[vi]
            sa = -jnp.sum(s * kk, axis=0, keepdims=True)
            sn = s * w + sa * kka + vt[vi:vi + 1, :] * kt
            st[vi] = sn
            ys.append(jnp.sum(sn * rt, axis=0, keepdims=True))
        y_ref[t] = jnp.concatenate(ys, axis=0)
        return c

    lax.fori_loop(0, tc, step, 0)

    @pl.when(ti == pl.num_programs(1) - 1)
    def _():
        so_ref[...] = st[...]


def _scan_layout_k(x, b, t):
    p = b * RWKV_HEADS
    a = x.reshape(b, t, RWKV_HEADS, HEAD_DIM).transpose(1, 3, 0, 2).reshape(t, HEAD_DIM, p)
    nc = -(-p // SCAN_P)
    a = jnp.pad(a, ((0, 0), (0, 0), (0, nc * SCAN_P - p))).reshape(t, HEAD_DIM, nc, SCAN_P).transpose(2, 0, 1, 3)
    return jnp.concatenate([a, a], -1)


def _scan_layout_v(x, b, t):
    p = b * RWKV_HEADS
    a = x.reshape(b, t, RWKV_HEADS, HEAD_DIM).transpose(1, 3, 0, 2).reshape(t, HEAD_DIM, p)
    nc = -(-p // SCAN_P)
    a = jnp.pad(a, ((0, 0), (0, 0), (0, nc * SCAN_P - p))).reshape(t, HEAD_DIM, nc, SCAN_P).transpose(2, 0, 1, 3)
    return jnp.concatenate([a[:, :, :SCAN_VH], a[:, :, SCAN_VH:]], -1)


def _scan_unlayout_y(y, b, t):
    p = b * RWKV_HEADS
    nc = y.shape[0]
    a = jnp.concatenate([y[..., :SCAN_P], y[..., SCAN_P:]], axis=2)
    a = a.transpose(1, 2, 0, 3).reshape(t, HEAD_DIM, nc * SCAN_P)[:, :, :p]
    return a.reshape(t, HEAD_DIM, b, RWKV_HEADS).transpose(2, 0, 3, 1).reshape(b * t, RWKV_DIM)


def _scan_layout_state(s):
    b = s.shape[0]
    p = b * RWKV_HEADS
    nc = -(-p // SCAN_P)
    a = jnp.pad(s.reshape(p, HEAD_DIM, HEAD_DIM), ((0, nc * SCAN_P - p), (0, 0), (0, 0)))
    a = a.reshape(nc, SCAN_P, HEAD_DIM, HEAD_DIM).transpose(0, 2, 3, 1)
    return jnp.concatenate([a[:, :SCAN_VH], a[:, SCAN_VH:]], -1)


def _scan_unlayout_state(st, b):
    p = b * RWKV_HEADS
    nc = st.shape[0]
    a = jnp.concatenate([st[..., :SCAN_P], st[..., SCAN_P:]], axis=1)
    a = a.transpose(0, 3, 1, 2).reshape(nc * SCAN_P, HEAD_DIM, HEAD_DIM)[:p]
    return a.reshape(b, RWKV_HEADS, HEAD_DIM, HEAD_DIM)


def _rwkv_scan(pre, s0, b, t, tc):
    r, w, k, v, kk, kka, _ = pre
    ks = [_scan_layout_k(a, b, t) for a in (kk, w, kka, k, r)]
    vs = _scan_layout_v(v, b, t)
    s0l = _scan_layout_state(s0)
    nc = s0l.shape[0]
    kspec = pl.BlockSpec((None, tc, HEAD_DIM, LANE), lambda c, i: (c, i, 0, 0))
    vspec = pl.BlockSpec((None, tc, SCAN_VH, LANE), lambda c, i: (c, i, 0, 0))
    sspec = pl.BlockSpec((None, SCAN_VH, HEAD_DIM, LANE), lambda c, i: (c, 0, 0, 0))
    y, so = pl.pallas_call(
        functools.partial(_scan_kernel, tc=tc),
        grid=(nc, t // tc),
        in_specs=[kspec] * 5 + [vspec, sspec],
        out_specs=[vspec, sspec],
        out_shape=[jax.ShapeDtypeStruct((nc, t, SCAN_VH, LANE), F32),
                   jax.ShapeDtypeStruct((nc, SCAN_VH, HEAD_DIM, LANE), F32)],
        scratch_shapes=[pltpu.VMEM((SCAN_VH, HEAD_DIM, LANE), F32)],
        compiler_params=_cparams("parallel", "arbitrary"),
        name="rwkv_scan",
    )(*ks, vs, s0l)
    return _scan_unlayout_y(y, b, t), _scan_unlayout_state(so, b)


def _compress_kernel(x_ref, pe_ref, w_ref, kg_ref, gtwo_ref, o_ref):
    z = jnp.dot((x_ref[...] + pe_ref[...]).astype(BF16), w_ref[...], preferred_element_type=F32)
    o_ref[:, 0:LANE] = _group_rms(z[:, 0:LANE], gtwo_ref[...], kg_ref[...])
    o_ref[:, LANE:2 * LANE] = z[:, LANE:2 * LANE]


def _compress_weights(cmp_pe, cmp_w, k_norm_c):
    wk = cmp_w[0].reshape(CMP_BLOCK, HEAD_DIM, HEAD_DIM)
    wv = cmp_w[1].reshape(CMP_BLOCK, HEAD_DIM, HEAD_DIM)
    full = jnp.zeros((CMP_BLOCK, 4, HEAD_DIM, 4, HEAD_DIM), F32)
    for s, w in enumerate((wk, wk, wv, wv)):
        full = full.at[:, s, :, s, :].set(w)
    pe = jnp.stack([cmp_pe[0], cmp_pe[0], cmp_pe[1], cmp_pe[1]], axis=1)
    return (full.reshape(CMP_BLOCK * 4 * HEAD_DIM, 4 * HEAD_DIM).astype(BF16), pe.reshape(1, -1),
            jnp.tile(k_norm_c, 2).reshape(1, LANE))


def _compress(rows, cw, gtwo, tile):
    wfull, pe, kg = cw
    return _row_call(_compress_kernel, rows, [], [pe, wfull, kg, gtwo], [(2 * LANE, F32)], tile, 1, "nsa_compress")[0]


def _nsa_kernel(q_ref, gates_ref, kcv_ref, kvs_ref, kvw_ref, bias_ref, biasc_ref, pair_ref, o_ref,
                *, qb, q_base, n_cmp, n_sel, w_off):
    i = pl.program_id(1)
    q0 = q_base + i * qb
    nblk = (q0 + qb - 1) // LANE + 1
    dq = q0 // LANE
    row = lax.broadcasted_iota(I32, (qb, LANE), 0)
    lane = lax.broadcasted_iota(I32, (qb, LANE), 1)
    t_pos = q0 + row
    qs = [_stack_heads(q_ref, hk) for hk in range(N_KV_HEADS)]
    gates = gates_ref[...]

    def gate_col(br, hk):
        return jnp.concatenate([gates[:, br * N_ATT_HEADS + hk * GQA + g:br * N_ATT_HEADS + hk * GQA + g + 1]
                                for g in range(GQA)], axis=0)

    kc = kcv_ref[:, 0:LANE]
    vc = kcv_ref[:, LANE:2 * LANE]
    mask_c = ((lane * CMP_BLOCK + CMP_BLOCK - 1) <= t_pos) & (lane < n_cmp)
    mask_c4 = _tile4(mask_c)
    o_cmp, selm = [], []
    cur = t_pos // SEL_BLOCK
    forced = (lane == 0) | (lane == cur) | (lane == cur - 1)
    lane_f = lane.astype(F32)
    for hk in range(N_KV_HEADS):
        s = _dot_nt(qs[hk], kc) * ATT_SCALE + biasc_ref[hk * GQA:(hk + 1) * GQA].reshape(GQA * qb, LANE)
        s = jnp.where(mask_c4, s, NEG)
        e = jnp.exp(s - jnp.max(s, axis=1, keepdims=True))
        p = jnp.where(mask_c4, e / jnp.sum(e, axis=1, keepdims=True), 0.0)
        o_cmp.append(jnp.dot(p.astype(BF16), vc, preferred_element_type=F32))
        ps = p[0:qb]
        for g in range(1, GQA):
            ps = ps + p[g * qb:(g + 1) * qb]
        score = _dot_split(ps, pair_ref[...])
        score = jnp.where(forced, FORCE, score)
        score = jnp.where(lane * SEL_BLOCK <= t_pos, score, NEG)
        picked = jnp.zeros((qb, LANE), F32)
        for _ in range(n_sel):
            mx = jnp.max(score, axis=1, keepdims=True)
            first = jnp.min(jnp.where(score == mx, lane_f, float(LANE)), axis=1, keepdims=True)
            hit = lane_f == first
            picked = jnp.where(hit, 1.0, picked)
            score = jnp.where(hit, TAKEN, score)
        selm.append(picked.astype(BF16))

    def key_blocks(ref, jb):
        off = pl.multiple_of(jb * LANE, LANE)
        return ref[pl.ds(off, LANE), 0:LANE], ref[pl.ds(off, LANE), LANE:2 * LANE]

    def bias_of(jb, hk):
        return bias_ref[jnp.minimum(dq - jb, 2), hk * GQA:(hk + 1) * GQA].reshape(GQA * qb, LANE)

    erow = lax.broadcasted_iota(I32, (LANE, LANE), 0)
    ecol = lax.broadcasted_iota(I32, (LANE, LANE), 1)

    def slc_body(jb, carry):
        kb, vb = key_blocks(kvs_ref, jb)
        expand = jnp.where(erow == 2 * jb + ecol // SEL_BLOCK, 1.0, 0.0).astype(BF16)
        causal = jb * LANE + lane <= t_pos
        out = []
        for hk in range(N_KV_HEADS):
            tok = jnp.dot(selm[hk], expand, preferred_element_type=F32) > 0.5
            out.append(_flash_step(qs[hk], kb, vb, bias_of(jb, hk), _tile4(tok & causal), *carry[hk]))
        return tuple(out)

    res_s = lax.fori_loop(0, nblk, slc_body, tuple(_flash_init(GQA * qb) for _ in range(N_KV_HEADS)))

    def win_body(jb, carry):
        kb, vb = key_blocks(kvw_ref, jb - w_off)
        dist = t_pos - (jb * LANE + lane)
        msk = _tile4((dist >= 0) & (dist < WINDOW))
        return tuple(_flash_step(qs[hk], kb, vb, bias_of(jb, hk), msk, *carry[hk]) for hk in range(N_KV_HEADS))

    res_w = lax.fori_loop(jnp.maximum(dq - WINDOW // LANE, 0), dq + 1, win_body,
                          tuple(_flash_init(GQA * qb) for _ in range(N_KV_HEADS)))

    outs = []
    for hk in range(N_KV_HEADS):
        o_s = res_s[hk][2] / res_s[hk][1]
        o_w = res_w[hk][2] / res_w[hk][1]
        outs.append(gate_col(0, hk) * o_cmp[hk] + gate_col(1, hk) * o_s + gate_col(2, hk) * o_w)
    _write_heads(o_ref, outs, qb)


def _bias_cmp(rel_bias, q_starts, qb):
    q0 = jnp.asarray(q_starts, I32)[:, None, None]
    r = jnp.arange(qb)[None, :, None]
    n = jnp.arange(LANE)[None, None, :]
    return rel_bias[_t5_bucket(q0 + r - (n * CMP_BLOCK + CMP_BLOCK - 1))].transpose(0, 3, 1, 2)


def _nsa(q, gates, kcv, kvs_b, kvw_b, bias, bias_c, pair, *, qb, q_base, n_cmp, n_sel, w_off):
    b, rows, _ = q.shape
    nq = rows // qb
    kern = functools.partial(_nsa_kernel, qb=qb, q_base=q_base, n_cmp=n_cmp, n_sel=n_sel, w_off=w_off)
    qspec = lambda w: pl.BlockSpec((None, qb, w), lambda bi, i: (bi, i, 0))
    kspec = lambda a: pl.BlockSpec((None,) + a.shape[1:], lambda bi, i: (bi, 0, 0))
    return pl.pallas_call(
        kern,
        grid=(b, nq),
        in_specs=[qspec(N_ATT_HEADS * LANE), qspec(LANE), kspec(kcv), kspec(kvs_b), kspec(kvw_b), _const_spec(bias),
                  pl.BlockSpec((None,) + bias_c.shape[1:], lambda bi, i: (i, 0, 0, 0)), _const_spec(pair)],
        out_specs=qspec(N_ATT_HEADS * LANE),
        out_shape=jax.ShapeDtypeStruct((b, rows, N_ATT_HEADS * LANE), BF16),
        compiler_params=_cparams("parallel", "parallel"),
        name="nsa_attention",
    )(q, gates, kcv, kvs_b, kvw_b, bias, bias_c, pair)


def _post_odd_kernel(x_ref, y_ref, r_ref, k_ref, v_ref, g_ref, od_ref, gate_ref, shf_ref, scf_ref,
                     gf_ref, lnw_ref, lnb_ref, rk_ref, gtwo_ref, wo_ref, wrt_ref, br_ref, x2_ref, hf_ref, cwt_ref):
    gtwo = gtwo_ref[...]
    mix = jnp.dot(od_ref[...], wo_ref[RWKV_DIM:RWKV_DIM + N_ATT_HEADS * LANE, :], preferred_element_type=F32)
    for t in range(RWKV_DIM // LANE):
        sl = slice(t * LANE, (t + 1) * LANE)
        y = y_ref[:, sl]
        dlt = y - _dot_split(y, gtwo)
        yn = (dlt * lax.rsqrt(_dot_split(dlt * dlt, gtwo) + GN_EPS)) * lnw_ref[:, sl] + lnb_ref[:, sl]
        dot_rk = _dot_split(r_ref[:, sl] * k_ref[:, sl] * rk_ref[:, sl], gtwo) * float(HEAD_DIM)
        oc = (yn + dot_rk * v_ref[:, sl]) * g_ref[:, sl]
        mix = mix + jnp.dot(oc.astype(BF16), wo_ref[sl, :], preferred_element_type=F32)
    _post_tail(x_ref[...], mix, gate_ref[...], gf_ref[...], shf_ref[...], scf_ref[...], wrt_ref, br_ref,
               x2_ref, hf_ref, cwt_ref)


def _odd_layer(xp, xs, mp, ms, page_table, wkv0, shift0, cmp_pool, slc_pool, win_buf, rel_bias, router, experts,
               g_mix, g_ffn, w_in, w_out, cpar, r_k, ln_w, ln_b, q_norm, k_norm, cmp_pe, cmp_w):
    B, S, D = xp.shape
    Bs = xs.shape[0]
    n_pages = page_table.shape[1]
    past = n_pages * PAGE_SIZE
    wts = _odd_weights(w_in, w_out, q_norm, k_norm)
    gm = _gmats()
    gone, gtwo, _ = gm
    gsum = (gtwo.astype(F32) * HEAD_DIM).astype(BF16)
    i = jnp.arange(LANE)
    pair = jnp.where(i[:, None] // 2 == i[None, :], 1.0, 0.0).astype(BF16)
    cw = _compress_weights(cmp_pe, cmp_w, k_norm[0])
    wrt, br = router
    wg, wu, wd = experts
    g_mix = g_mix.reshape(1, D)
    post_consts = [g_ffn.reshape(1, D), ln_w.reshape(1, -1), ln_b.reshape(1, -1), r_k.reshape(1, -1), gtwo, wts[1],
                   wrt, br]
    w_eff = win_buf.shape[1]

    tp = 256
    zc, q_p, kvc, kvs, kvs_b, kvw, kvw_b, gates = _inproj_odd(xp.reshape(B * S, D), [mp[0], mp[1]], g_mix, wts, gm,
                                                               tp, S // tp)
    r3 = lambda a: a.reshape(B, S, a.shape[-1])
    pre = _rwkv_pre(zc, _shift_rows(r3(zc), 1).reshape(B * S, P_C), cpar, gsum, tp)
    y, wkv_p = _rwkv_scan(pre, jnp.zeros((B, RWKV_HEADS, HEAD_DIM, HEAD_DIM), F32), B, S, 32)
    n_cmp = S // CMP_BLOCK
    kcv = _compress(kvc.reshape(B * n_cmp, CMP_BLOCK * 2 * LANE), cw, gtwo, _pick_tile(B * n_cmp, 256))
    kcv = jnp.pad(kcv.reshape(B, n_cmp, 2 * LANE), ((0, 0), (0, LANE - n_cmp), (0, 0))).astype(BF16)
    n_slc = -(-S // SEL_BLOCK)
    od = _nsa(r3(q_p), r3(gates), kcv, r3(kvs_b), r3(kvw_b), _bias_tiles(rel_bias, LANE),
              _bias_cmp(rel_bias, [j * LANE for j in range(S // LANE)], LANE), pair,
              qb=LANE, q_base=0, n_cmp=n_cmp, n_sel=min(N_SEL_BLOCKS, n_slc), w_off=0)
    x2, hf, cwt = _row_call(_post_odd_kernel,
                            [xp.reshape(B * S, D), y, pre[0], pre[2], pre[3], pre[6], od.reshape(B * S, -1)],
                            [mp[2], mp[3], mp[4]], post_consts, POST_OUTS, tp, S // tp, "post_odd",
                            outs_t=[(N_EXPERTS, F32)])
    tm = 512
    xp3 = _moe(hf, cwt.T, x2, mp[5], wg, wu, wd, tm, S // tm).reshape(B, S, D)
    kv5 = lambda a, n: a.reshape(-1, n, 2, N_KV_HEADS, HEAD_DIM)
    outs_p = (wkv_p, r3(zc)[:, S - 1], kv5(kvc, S), kv5(kvs, S), kv5(kvw, S)[:, S - min(WINDOW, S):])

    zc_s, q_s, kvc_s, kvs_s, _, kvw_s, _, gates_s = _inproj_odd(xs, [ms[0], ms[1]], g_mix, wts, gm, Bs, 1)
    pre_s = _rwkv_pre(zc_s, shift0, cpar, gsum, Bs)
    y_s, wkv_s = _rwkv_scan(pre_s, wkv0, Bs, 1, 1)
    n_pool = cmp_pool.shape[0]
    per_page = PAGE_SIZE // CMP_BLOCK
    kc_pool = _compress(cmp_pool.reshape(n_pool * per_page, CMP_BLOCK * 2 * LANE), cw, gtwo,
                        _pick_tile(n_pool * per_page, 256))
    kcv_s = _gather_pages(kc_pool.reshape(n_pool, per_page, 2 * LANE), page_table,
                          jnp.zeros((Bs, 1, 2 * LANE), F32), 2 * LANE, F32)
    n_cmp_s = (past + 1) // CMP_BLOCK
    kcv_s = jnp.pad(kcv_s, ((0, 0), (0, LANE - kcv_s.shape[1]), (0, 0))).astype(BF16)
    kvs_all = _gather_pages(slc_pool, page_table, kvs_s[:, None, :], 2 * LANE, BF16)
    win_all = jnp.concatenate([win_buf.reshape(Bs, w_eff, 2 * LANE), kvw_s[:, None, :],
                               jnp.zeros((Bs, LANE - 1, 2 * LANE), F32)], axis=1).astype(BF16)
    qb = 8
    pad8 = lambda a: jnp.pad(a[:, None, :], ((0, 0), (0, qb - 1), (0, 0)))
    n_slc_s = -(-(past + 1) // SEL_BLOCK)
    od_s = _nsa(pad8(q_s), pad8(gates_s), kcv_s, kvs_all, win_all, _bias_tiles(rel_bias, qb),
                _bias_cmp(rel_bias, [past], qb), pair, qb=qb, q_base=past, n_cmp=n_cmp_s,
                n_sel=min(N_SEL_BLOCKS, n_slc_s), w_off=(past - w_eff) // LANE)[:, 0]
    x2s, hfs, cwts = _row_call(_post_odd_kernel, [xs, y_s, pre_s[0], pre_s[2], pre_s[3], pre_s[6], od_s],
                               [ms[2], ms[3], ms[4]], post_consts, POST_OUTS, Bs, 1, "post_odd_s",
                               outs_t=[(N_EXPERTS, F32)])
    xs3 = _moe(hfs, cwts.T, x2s, ms[5], wg, wu, wd, Bs, 1)
    win_new = jnp.concatenate([win_buf[:, 1:], kv5(kvw_s, 1)], axis=1)
    outs_s = (wkv_s, zc_s, kv5(kvc_s, 1), kv5(kvs_s, 1), win_new)
    return xp3, xs3, outs_p, outs_s


def _mods(c_p, c_s, w, b):
    nb = c_p.shape[0]
    m = _ada(jnp.concatenate([c_p, c_s], 0), w.astype(BF16), b)
    parts = jnp.split(m, 6, axis=-1)
    return [p[:nb, None, :] for p in parts], [p[None, nb:, :] for p in parts]


def _forward(x_prompt, x_sample, c_prompt, c_sample, page_table, cache_a_kv, cache_a_kidx, state_b_conv,
             state_c_wkv, state_c_shift, cache_d_cmp, cache_d_slc, cache_d_win, rel_bias, w_router, b_router,
             w_ada, b_ada, g_norm_mix, g_norm_ffn, w_expert_gate, w_expert_up, w_expert_down, e_w_in, e_w_out,
             a_q_norm, a_k_norm, b_conv_w, b_conv_b, o_w_in, o_w_out, c_mu, c_w0, c_w_up, c_a0, c_a_up,
             c_g_up, c_k_k, c_k_a, c_r_k, c_ln_w, c_ln_b, d_q_norm, d_k_norm, d_cmp_pe, d_cmp_w):
    assert w_ada.shape[0] == 2 and e_w_in.shape[0] == 1 and o_w_in.shape[0] == 1
    B, S, D = x_prompt.shape
    Bs = x_sample.shape[0]
    assert x_sample.shape[1] == 1
    xp, xs = x_prompt, x_sample.reshape(Bs, D)
    router = (w_router.T, b_router.reshape(N_EXPERTS, 1))
    n_pool = cache_a_kv.shape[1]
    pool3 = lambda a: a.reshape(n_pool, PAGE_SIZE, -1)
    experts = lambda l: tuple(w[l].astype(BF16) for w in (w_expert_gate, w_expert_up, w_expert_down))

    mp, ms = _mods(c_prompt, c_sample, w_ada[0], b_ada[0])
    xp, xs, ep, es = _even_layer(xp, xs, mp, ms, page_table, pool3(cache_a_kv[0]), cache_a_kidx[0], state_b_conv[0],
                                 rel_bias, router, experts(0), g_norm_mix[0], g_norm_ffn[0], e_w_in[0], e_w_out[0],
                                 a_q_norm[0], a_k_norm[0], b_conv_w[0], b_conv_b[0])
    mp, ms = _mods(c_prompt, c_sample, w_ada[1], b_ada[1])
    cpar = (c_mu[0], c_w0[0], c_w_up[0], c_a0[0], c_a_up[0], c_g_up[0], c_k_k[0], c_k_a[0])
    xp, xs, op, os_ = _odd_layer(xp, xs, mp, ms, page_table, state_c_wkv[0], state_c_shift[0], pool3(cache_d_cmp[0]),
                                 pool3(cache_d_slc[0]), cache_d_win[0], rel_bias, router, experts(1), g_norm_mix[1],
                                 g_norm_ffn[1], o_w_in[0], o_w_out[0], cpar, c_r_k[0].reshape(-1), c_ln_w[0],
                                 c_ln_b[0], d_q_norm[0], d_k_norm[0], d_cmp_pe[0], d_cmp_w[0])
    stack = lambda ts: tuple(a[None] for a in ts)
    return (xp, xs.reshape(Bs, 1, D)) + stack(ep) + stack(op) + stack(es) + stack(os_)


def kernel(x_prompt, x_sample, c_prompt, c_sample, page_table, cache_a_kv, cache_a_kidx, state_b_conv, state_c_wkv, state_c_shift, cache_d_cmp, cache_d_slc, cache_d_win, rel_bias, w_router, b_router, w_ada, b_ada, g_norm_mix, g_norm_ffn, w_expert_gate, w_expert_up, w_expert_down, e_w_in, e_w_out, a_q_norm, a_k_norm, b_conv_w, b_conv_b, o_w_in, o_w_out, c_mu, c_w0, c_w_up, c_a0, c_a_up, c_g_up, c_k_k, c_k_a, c_r_k, c_ln_w, c_ln_b, d_q_norm, d_k_norm, d_cmp_pe, d_cmp_w):
    return _forward(x_prompt, x_sample, c_prompt, c_sample, page_table, cache_a_kv, cache_a_kidx, state_b_conv,
                    state_c_wkv, state_c_shift, cache_d_cmp, cache_d_slc, cache_d_win, rel_bias, w_router, b_router,
                    w_ada, b_ada, g_norm_mix, g_norm_ffn, w_expert_gate, w_expert_up, w_expert_down, e_w_in, e_w_out,
                    a_q_norm, a_k_norm, b_conv_w, b_conv_b, o_w_in, o_w_out, c_mu, c_w0, c_w_up, c_a0, c_a_up,
                    c_g_up, c_k_k, c_k_a, c_r_k, c_ln_w, c_ln_b, d_q_norm, d_k_norm, d_cmp_pe, d_cmp_w)
```

```python
import functools
import math

import jax
import jax.numpy as jnp
from jax import lax
from jax.experimental import pallas as pl
from jax.experimental.pallas import tpu as pltpu

F32 = jnp.float32
BF16 = jnp.bfloat16
I32 = jnp.int32

LANE = 128
HEAD_DIM = 64
N_ATT_HEADS = 8
N_KV_HEADS = 2
GQA = N_ATT_HEADS // N_KV_HEADS
IDX_HEADS = 4
IDX_DIM = 64
TOPK_MAX = 256
CONV_CH = 512
RWKV_HEADS = 8
RWKV_DIM = RWKV_HEADS * HEAD_DIM
LORA_W = 64
LORA_A = 64
LORA_G = 128
GN_EPS = 64e-5
CMP_BLOCK = 32
SEL_BLOCK = 64
N_SEL_BLOCKS = 8
WINDOW = 512
N_BUCKETS = 32
MAX_DISTANCE = 128
N_EXPERTS = 16
N_GROUPS = 4
EXPERTS_PER_GROUP = N_EXPERTS // N_GROUPS
D_EXPERT = 512
PAGE_SIZE = 128
RMS_EPS = 1e-6
NEG = -1e30
FORCE = 1e9
TAKEN = -3e38
ATT_SCALE = HEAD_DIM ** -0.5
VMEM_LIMIT = 56 * 1024 * 1024
ROW_TILE = 256
MOE_TILE = 512
SCAN_TIME_CHUNK = 32
SAMPLE_QB = 8
SAMPLE_GROUP = 8


def _cparams(*sem):
    return pltpu.CompilerParams(dimension_semantics=sem, vmem_limit_bytes=VMEM_LIMIT)


def _pick_tile(rows, pref):
    t = min(pref, rows)
    while rows % t or (t % 8 and t != rows):
        t -= 1
    return t


def _const_spec(a):
    nd = a.ndim
    return pl.BlockSpec(a.shape, lambda *_: (0,) * nd)


def _dot(a, b):
    return jnp.dot(a.astype(BF16), b.astype(BF16), preferred_element_type=F32)


def _dot_nt(a, b):
    return lax.dot_general(a.astype(BF16), b.astype(BF16), (((1,), (1,)), ((), ())),
                           preferred_element_type=F32)


def _dot_split(x, m):
    hi = x.astype(BF16)
    r1 = x - hi.astype(F32)
    mid = r1.astype(BF16)
    lo = (r1 - mid.astype(F32)).astype(BF16)
    return (jnp.dot(hi, m, preferred_element_type=F32) + jnp.dot(mid, m, preferred_element_type=F32)
            + jnp.dot(lo, m, preferred_element_type=F32))


def _sigmoid(x):
    return 1.0 / (1.0 + jnp.exp(-x))


def _silu(x):
    return x * _sigmoid(x)


def _modulate(x, g, shift, scale):
    y = x * lax.rsqrt(jnp.mean(x * x, axis=-1, keepdims=True) + RMS_EPS)
    return (y * g) * (1.0 + scale) + shift


def _group_rms(t, gmat, gain):
    ms = _dot_split(t * t, gmat)
    return (t * lax.rsqrt(ms + RMS_EPS)) * gain


def _ada_kernel(c_ref, w_ref, b_ref, o_ref):
    o_ref[...] = _dot(_silu(c_ref[...]), w_ref[...]) + b_ref[...]


def _ada(c, w_bf, b):
    r, d = c.shape
    n = w_bf.shape[1]
    tn = 512
    return pl.pallas_call(
        _ada_kernel,
        grid=(n // tn,),
        in_specs=[pl.BlockSpec((r, d), lambda j: (0, 0)),
                  pl.BlockSpec((d, tn), lambda j: (0, j)),
                  pl.BlockSpec((1, tn), lambda j: (0, j))],
        out_specs=pl.BlockSpec((r, tn), lambda j: (0, j)),
        out_shape=jax.ShapeDtypeStruct((r, n), F32),
        compiler_params=_cparams("parallel"),
        name="ada_mod",
    )(c, w_bf, b.reshape(1, n))


E_Q0, E_KV0, E_QI0, E_MISC0, E_BG0, E_CG0, E_XIN0, E_END = 0, 1024, 1280, 1792, 1920, 2432, 2944, 3456


def _inproj_even_kernel(x_ref, shift_ref, scale_ref, g_ref, w_ref, qg_ref, kg_ref, gone_ref, gtwo_ref,
                        q_ref, kv_ref, kvb_ref, qi_ref, misc_ref, miscb_ref, bg_ref, u_ref):
    h = _modulate(x_ref[...], g_ref[...], shift_ref[...], scale_ref[...])
    z = jnp.dot(h.astype(BF16), w_ref[...], preferred_element_type=F32)
    gone = gone_ref[...]
    for t in range(N_ATT_HEADS):
        sl = slice(t * LANE, (t + 1) * LANE)
        q_ref[:, sl] = _group_rms(z[:, E_Q0 + t * LANE:E_Q0 + (t + 1) * LANE], gone, qg_ref[:, sl]).astype(BF16)
    k = _group_rms(z[:, E_KV0:E_KV0 + LANE], gtwo_ref[...], kg_ref[...])
    v = z[:, E_KV0 + LANE:E_KV0 + 2 * LANE]
    kv_ref[:, 0:LANE] = k
    kv_ref[:, LANE:2 * LANE] = v
    kvb_ref[:, 0:LANE] = k.astype(BF16)
    kvb_ref[:, LANE:2 * LANE] = v.astype(BF16)
    qi_ref[...] = z[:, E_QI0:E_MISC0].astype(BF16)
    misc = z[:, E_MISC0:E_BG0]
    misc_ref[...] = misc
    miscb_ref[...] = misc.astype(BF16)
    bg_ref[...] = z[:, E_BG0:E_CG0]
    u_ref[...] = z[:, E_CG0:E_XIN0] * z[:, E_XIN0:E_END]


def _row_call(kernel, xs, mods, consts, outs, tile, tpb, name, outs_t=()):
    if not isinstance(xs, (list, tuple)):
        xs = [xs]
    rows = xs[0].shape[0]
    n_tiles = rows // tile
    in_specs = [pl.BlockSpec((tile, x.shape[1]), lambda t: (t, 0)) for x in xs]
    for m in mods:
        in_specs.append(pl.BlockSpec((None,) + m.shape[1:], lambda t: (t // tpb, 0, 0)))
    in_specs += [_const_spec(c) for c in consts]
    out_specs = [pl.BlockSpec((tile, w), lambda t: (t, 0)) for (w, _) in outs]
    out_shape = [jax.ShapeDtypeStruct((rows, w), dt) for (w, dt) in outs]
    out_specs += [pl.BlockSpec((hh, tile), lambda t: (0, t)) for (hh, _) in outs_t]
    out_shape += [jax.ShapeDtypeStruct((hh, rows), dt) for (hh, dt) in outs_t]
    return pl.pallas_call(kernel, grid=(n_tiles,), in_specs=in_specs, out_specs=out_specs, out_shape=out_shape,
                          compiler_params=_cparams("parallel"), name=name)(*xs, *mods, *consts)


def _pad_q_cols(wq):
    d = wq.shape[0]
    w = wq.reshape(d, N_ATT_HEADS, HEAD_DIM)
    z = jnp.zeros_like(w)
    lo = jnp.concatenate([w, z], -1)
    hi = jnp.concatenate([z, w], -1)
    sel = (jnp.arange(N_ATT_HEADS) >= GQA)[None, :, None]
    return jnp.where(sel, hi, lo).reshape(d, N_ATT_HEADS * LANE)


def _pad_o_rows(wo):
    return _pad_q_cols(wo.T).T


def _gmats():
    i = jnp.arange(LANE)
    gone = jnp.full((LANE, LANE), 1.0 / HEAD_DIM, F32).astype(BF16)
    gtwo = jnp.where((i[:, None] // HEAD_DIM) == (i[None, :] // HEAD_DIM), 1.0 / HEAD_DIM, 0.0).astype(BF16)
    tri = jnp.where(i[:, None] <= i[None, :], 1.0, 0.0).astype(BF16)
    return gone, gtwo, tri


def _even_weights(w_in, w_out, q_norm, k_norm):
    d = w_in.shape[0]
    a_q, a_kv = N_ATT_HEADS * HEAD_DIM, 2 * N_KV_HEADS * HEAD_DIM
    o = 0
    wq = w_in[:, o:o + a_q]; o += a_q
    wkv = w_in[:, o:o + a_kv]; o += a_kv
    wqi = w_in[:, o:o + IDX_HEADS * IDX_DIM]; o += IDX_HEADS * IDX_DIM
    wki = w_in[:, o:o + IDX_DIM]; o += IDX_DIM
    wwi = w_in[:, o:o + IDX_HEADS]; o += IDX_HEADS
    wrest = w_in[:, o:]
    wqi = jnp.concatenate([wqi.reshape(d, IDX_HEADS, IDX_DIM), jnp.zeros((d, IDX_HEADS, LANE - IDX_DIM), F32)],
                          -1).reshape(d, IDX_HEADS * LANE)
    wmisc = jnp.concatenate([wki, wwi, jnp.zeros((d, LANE - IDX_DIM - IDX_HEADS), F32)], -1)
    w_in_p = jnp.concatenate([_pad_q_cols(wq), wkv, wqi, wmisc, wrest], -1).astype(BF16)
    w_out_p = jnp.concatenate([_pad_o_rows(w_out[:a_q]), w_out[a_q:]], 0).astype(BF16)
    qg = jnp.tile(q_norm, 2 * N_ATT_HEADS).reshape(1, N_ATT_HEADS * LANE)
    kg = jnp.tile(k_norm, 2).reshape(1, LANE)
    return w_in_p, w_out_p, qg, kg


def _inproj_even(x, mods, g, wts, gm, tile, tpb):
    w_in_p, _, qg, kg = wts
    gone, gtwo, _ = gm
    outs = [(N_ATT_HEADS * LANE, BF16), (2 * LANE, F32), (2 * LANE, BF16), (IDX_HEADS * LANE, BF16),
            (LANE, F32), (LANE, BF16), (CONV_CH, F32), (CONV_CH, F32)]
    return _row_call(_inproj_even_kernel, x, mods, [g, w_in_p, qg, kg, gone, gtwo], outs, tile, tpb, "inproj_even")


def _t5_bucket(dist):
    dist = jnp.maximum(dist, 0)
    exact = N_BUCKETS // 2
    far = exact + (jnp.log(jnp.maximum(dist, 1).astype(F32) / exact)
                   / math.log(MAX_DISTANCE / exact) * (N_BUCKETS - exact)).astype(I32)
    return jnp.where(dist < exact, dist, jnp.minimum(far, N_BUCKETS - 1))


def _bias_lookup(rel_bias, dist):
    onehot = (_t5_bucket(dist)[..., None] == jnp.arange(N_BUCKETS)).astype(F32)
    return jnp.einsum("...k,kh->...h", onehot, rel_bias, precision=lax.Precision.HIGHEST)


def _bias_tiles(rel_bias, qb):
    r = jnp.arange(qb)[:, None]
    c = jnp.arange(LANE)[None, :]
    tiles = [_bias_lookup(rel_bias, d * LANE + r - c) for d in range(3)]
    return jnp.stack(tiles).transpose(0, 3, 1, 2)


def _flash_step(qs, kb, vb, bias, msk, m, l, acc):
    s = _dot_nt(qs, kb) * ATT_SCALE + bias
    s = jnp.where(msk, s, NEG)
    m_new = jnp.maximum(m, jnp.max(s, axis=1, keepdims=True))
    p = jnp.where(msk, jnp.exp(s - m_new), 0.0)
    alpha = jnp.exp(m - m_new)
    l = alpha * l + jnp.sum(p, axis=1, keepdims=True)
    acc = alpha * acc + jnp.dot(p.astype(BF16), vb, preferred_element_type=F32)
    return m_new, l, acc


def _flash_init(rows):
    return (jnp.full((rows, 1), NEG, F32), jnp.zeros((rows, 1), F32), jnp.zeros((rows, LANE), F32))


def _stack_heads(q_ref, hk):
    return jnp.concatenate([q_ref[:, (hk * GQA + g) * LANE:(hk * GQA + g + 1) * LANE] for g in range(GQA)], axis=0)


def _tile4(x):
    return jnp.concatenate([x] * GQA, axis=0)


def _loop(lo, hi, body, init, unroll):
    if unroll:
        for j in range(lo, hi):
            init = body(j, init)
        return init
    return lax.fori_loop(lo, hi, body, init)


def _dsa_kernel(q_ref, qi_ref, misc_ref, kidx_ref, kv_ref, bias_ref, tri_ref, o_ref, key_s,
                *, grp, qb, q_base, n_keep, single):
    q0 = q_base if single else q_base + pl.program_id(1) * qb
    nblk = (q0 + qb - 1) // LANE + 1
    dq = q0 // LANE
    seqs = range(grp)
    row = lax.broadcasted_iota(I32, (qb, LANE), 0)
    lane = lax.broadcasted_iota(I32, (qb, LANE), 1)
    t_pos = q0 + row
    wis = [[jnp.broadcast_to(misc_ref[g][:, IDX_DIM + h:IDX_DIM + h + 1], (qb, LANE)) for h in range(IDX_HEADS)]
           for g in seqs]
    idx_scale = (IDX_HEADS * IDX_DIM) ** -0.5

    def pass_a(j, c):
        off = pl.multiple_of(j * LANE, LANE)
        causal = j * LANE + lane <= t_pos
        for g in seqs:
            kb = kidx_ref[g, pl.ds(off, LANE), :]
            acc = jnp.zeros((qb, LANE), F32)
            for h in range(IDX_HEADS):
                acc = acc + jnp.maximum(_dot_nt(qi_ref[g, :, h * LANE:(h + 1) * LANE], kb), 0.0) * wis[g][h]
            sc = jnp.where(causal, acc * idx_scale, NEG)
            sc = jnp.where(sc == 0.0, 0.0, sc)
            bits = lax.bitcast_convert_type(sc, I32)
            key_s[g, j] = jnp.where(bits < 0, bits ^ jnp.int32(0x7FFFFFFF), bits)
        return c

    lax.fori_loop(0, nblk, pass_a, 0)

    def count(pred):
        def body(j, accs):
            return tuple(a + jnp.where(pred(g, key_s[g, j]), 1.0, 0.0) for g, a in zip(seqs, accs))
        accs = _loop(0, nblk, body, tuple(jnp.zeros((qb, LANE), F32) for _ in seqs), single)
        return [jnp.sum(a, axis=1, keepdims=True) for a in accs]

    keep = float(n_keep)
    int_min = jnp.int32(-2 ** 31)
    thr = tuple(jnp.where(c >= keep, jnp.int32(0), int_min) for c in count(lambda g, k: k >= 0))

    def search(it, thr):
        bit = lax.shift_left(jnp.int32(1), jnp.int32(30) - it)
        cand = [t | bit for t in thr]
        cnt = count(lambda g, k: k >= cand[g])
        return tuple(jnp.where(c >= keep, cd, t) for c, cd, t in zip(cnt, cand, thr))

    thr = lax.fori_loop(0, 31, search, thr)
    need = [keep - c for c in count(lambda g, k: k > thr[g])]
    tri = tri_ref[...]

    def pass_c(j, run):
        causal = j * LANE + lane <= t_pos
        out = []
        for g in seqs:
            key = key_s[g, j]
            eq = key == thr[g]
            eqf = jnp.where(eq, 1.0, 0.0)
            cum = jnp.dot(eqf.astype(BF16), tri, preferred_element_type=F32) + run[g]
            sel = (key > thr[g]) | (eq & (cum <= need[g]))
            key_s[g, j] = jnp.where(sel & causal, 1, 0)
            out.append(run[g] + jnp.sum(eqf, axis=1, keepdims=True))
        return tuple(out)

    lax.fori_loop(0, nblk, pass_c, tuple(jnp.zeros((qb, 1), F32) for _ in seqs))

    qs = [[_stack_heads(q_ref.at[g], hk) for hk in range(N_KV_HEADS)] for g in seqs]

    def pass_d(j, carry):
        off = pl.multiple_of(j * LANE, LANE)
        dsel = jnp.minimum(dq - j, 2)
        biases = [bias_ref[dsel, hk * GQA:(hk + 1) * GQA].reshape(GQA * qb, LANE) for hk in range(N_KV_HEADS)]
        out = []
        for g in seqs:
            kb = kv_ref[g, pl.ds(off, LANE), 0:LANE]
            vb = kv_ref[g, pl.ds(off, LANE), LANE:2 * LANE]
            msk = _tile4(key_s[g, j] > 0)
            out.append(tuple(_flash_step(qs[g][hk], kb, vb, biases[hk], msk, *carry[g][hk])
                             for hk in range(N_KV_HEADS)))
        return tuple(out)

    init = tuple(tuple(_flash_init(GQA * qb) for _ in range(N_KV_HEADS)) for _ in seqs)
    res = lax.fori_loop(0, nblk, pass_d, init)
    for g in seqs:
        _write_heads(o_ref.at[g], [acc / l for (_, l, acc) in res[g]], qb)


def _write_heads(o_ref, outs, qb):
    lane = lax.broadcasted_iota(I32, (qb, LANE), 1)
    for hk in range(N_KV_HEADS):
        valid = (lane // HEAD_DIM) == hk
        for g in range(GQA):
            h = hk * GQA + g
            o_ref[:, h * LANE:(h + 1) * LANE] = jnp.where(valid, outs[hk][g * qb:(g + 1) * qb], 0.0).astype(BF16)


def _dsa(q, qi, misc, kidx_b, kv_b, bias, tri, *, qb, q_base, n_keep, grp):
    b, rows, _ = q.shape
    nq = rows // qb
    s = kv_b.shape[1]
    kern = functools.partial(_dsa_kernel, grp=grp, qb=qb, q_base=q_base, n_keep=n_keep, single=(nq == 1))
    qspec = lambda w: pl.BlockSpec((grp, qb, w), lambda bi, i: (bi, i, 0))
    kspec = lambda w: pl.BlockSpec((grp, s, w), lambda bi, i: (bi, 0, 0))
    return pl.pallas_call(
        kern,
        grid=(b // grp, nq),
        in_specs=[qspec(N_ATT_HEADS * LANE), qspec(IDX_HEADS * LANE), qspec(LANE), kspec(LANE), kspec(2 * LANE),
                  _const_spec(bias), _const_spec(tri)],
        out_specs=qspec(N_ATT_HEADS * LANE),
        out_shape=jax.ShapeDtypeStruct((b, rows, N_ATT_HEADS * LANE), BF16),
        scratch_shapes=[pltpu.VMEM((grp, s // LANE, qb, LANE), I32)],
        compiler_params=_cparams("parallel", "parallel"),
        name="dsa_attention",
    )(q, qi, misc, kidx_b, kv_b, bias, tri)


def _gather_kernel(pt_ref, *refs, n_pages, rpp, w_in, w_out):
    del pt_ref
    new_ref, o_ref = refs[n_pages], refs[n_pages + 1]
    for p in range(n_pages):
        v = refs[p][...]
        if w_in < w_out:
            v = jnp.concatenate([v, jnp.zeros((rpp, w_out - w_in), F32)], axis=1)
        o_ref[p * rpp:(p + 1) * rpp, :] = v.astype(o_ref.dtype)
    row = lax.broadcasted_iota(I32, (rpp, w_out), 0)
    tail = jnp.where(row == 0, jnp.broadcast_to(new_ref[...], (rpp, w_out)), 0.0)
    o_ref[n_pages * rpp:(n_pages + 1) * rpp, :] = tail.astype(o_ref.dtype)


def _gather_pages(pool, page_table, new_rows, w_out, out_dtype):
    b, n_pages = page_table.shape
    rpp, w_in = pool.shape[1], pool.shape[2]
    kern = functools.partial(_gather_kernel, n_pages=n_pages, rpp=rpp, w_in=w_in, w_out=w_out)
    in_specs = [pl.BlockSpec((None, rpp, w_in), lambda bi, pt, p=p: (pt[bi, p], 0, 0)) for p in range(n_pages)]
    in_specs.append(pl.BlockSpec((None, 1, w_out), lambda bi, pt: (bi, 0, 0)))
    s = (n_pages + 1) * rpp
    return pl.pallas_call(
        kern,
        grid_spec=pltpu.PrefetchScalarGridSpec(
            num_scalar_prefetch=1, grid=(b,), in_specs=in_specs,
            out_specs=pl.BlockSpec((None, s, w_out), lambda bi, pt: (bi, 0, 0))),
        out_shape=jax.ShapeDtypeStruct((b, s, w_out), out_dtype),
        compiler_params=_cparams("parallel"),
        name="gather_pages",
    )(page_table, *([pool] * n_pages), new_rows)


def _route(hf, wrt, br):
    logits = lax.dot_general(wrt, hf, (((1,), (1,)), ((), ())), precision=lax.Precision.HIGHEST,
                             preferred_element_type=F32)
    s = _sigmoid(logits)
    sel = s + br
    rows = [sel[e:e + 1, :] for e in range(N_EXPERTS)]
    grp = []
    for g in range(N_GROUPS):
        a = rows[g * EXPERTS_PER_GROUP:(g + 1) * EXPERTS_PER_GROUP]
        best = None
        for i in range(EXPERTS_PER_GROUP):
            for j in range(i + 1, EXPERTS_PER_GROUP):
                v = a[i] + a[j]
                best = v if best is None else jnp.maximum(best, v)
        grp.append(best)
    gbest = jnp.zeros_like(grp[0], dtype=I32)
    cur = grp[0]
    for g in range(1, N_GROUPS):
        better = grp[g] > cur
        gbest = jnp.where(better, g, gbest)
        cur = jnp.where(better, grp[g], cur)
    picked = []
    for g in range(N_GROUPS):
        a = rows[g * EXPERTS_PER_GROUP:(g + 1) * EXPERTS_PER_GROUP]
        for j in range(EXPERTS_PER_GROUP):
            rank = jnp.zeros_like(a[j])
            for jj in range(EXPERTS_PER_GROUP):
                if jj != j:
                    ahead = (a[jj] > a[j]) | (a[jj] == a[j]) if jj < j else (a[jj] > a[j])
                    rank = rank + jnp.where(ahead, 1.0, 0.0)
            e = g * EXPERTS_PER_GROUP + j
            picked.append(jnp.where((gbest == g) & (rank < 2.0), s[e:e + 1, :], 0.0))
    den = picked[0]
    for p in picked[1:]:
        den = den + p
    return jnp.concatenate([p / den for p in picked], axis=0)


def _post_tail(x, mix, gate, gf, shf, scf, wrt_ref, br_ref, x2_ref, hf_ref, cwt_ref):
    x2 = x + gate * mix
    x2_ref[...] = x2
    hf = _modulate(x2, gf, shf, scf)
    hf_ref[...] = hf.astype(BF16)
    cwt_ref[...] = _route(hf, wrt_ref[...], br_ref[...])


def _post_even_kernel(x_ref, oa_ref, bg_ref, u_ref, um1_ref, um2_ref, gate_ref, shf_ref, scf_ref,
                      gf_ref, cw_ref, cb_ref, wo_ref, wrt_ref, br_ref, x2_ref, hf_ref, cwt_ref):
    cw = cw_ref[...]
    y = cb_ref[...] + cw[0:1, :] * um2_ref[...]
    y = y + cw[1:2, :] * um1_ref[...]
    y = y + cw[2:3, :] * u_ref[...]
    n_a = N_ATT_HEADS * LANE
    mix = (jnp.dot(oa_ref[...], wo_ref[0:n_a, :], preferred_element_type=F32)
           + jnp.dot((bg_ref[...] * y).astype(BF16), wo_ref[n_a:n_a + CONV_CH, :], preferred_element_type=F32))
    _post_tail(x_ref[...], mix, gate_ref[...], gf_ref[...], shf_ref[...], scf_ref[...], wrt_ref, br_ref,
               x2_ref, hf_ref, cwt_ref)


POST_OUTS = [(1024, F32), (1024, BF16)]


def _moe_kernel(hf_ref, cw_ref, x2_ref, gate_ref, wg_ref, wu_ref, wd_ref, o_ref, acc_ref):
    e = pl.program_id(1)

    @pl.when(e == 0)
    def _():
        acc_ref[...] = jnp.zeros_like(acc_ref)

    hf = hf_ref[...]
    hmid = _silu(jnp.dot(hf, wg_ref[...], preferred_element_type=F32)) * jnp.dot(hf, wu_ref[...],
                                                                                 preferred_element_type=F32)
    cw = cw_ref[...]
    lane = lax.broadcasted_iota(I32, cw.shape, 1)
    wcol = jnp.sum(jnp.where(lane == e, cw, 0.0), axis=1, keepdims=True)
    acc_ref[...] += jnp.dot((hmid * wcol).astype(BF16), wd_ref[...], preferred_element_type=F32)

    @pl.when(e == N_EXPERTS - 1)
    def _():
        o_ref[...] = x2_ref[...] + gate_ref[...] * acc_ref[...]


def _moe(hf, cw, x2, gate, wg, wu, wd, tile, tpb):
    rows, d = x2.shape
    de = wg.shape[2]
    return pl.pallas_call(
        _moe_kernel,
        grid=(rows // tile, N_EXPERTS),
        in_specs=[pl.BlockSpec((tile, d), lambda t, e: (t, 0)),
                  pl.BlockSpec((tile, N_EXPERTS), lambda t, e: (t, 0)),
                  pl.BlockSpec((tile, d), lambda t, e: (t, 0)),
                  pl.BlockSpec((None,) + gate.shape[1:], lambda t, e: (t // tpb, 0, 0)),
                  pl.BlockSpec((None, d, de), lambda t, e: (e, 0, 0)),
                  pl.BlockSpec((None, d, de), lambda t, e: (e, 0, 0)),
                  pl.BlockSpec((None, de, d), lambda t, e: (e, 0, 0))],
        out_specs=pl.BlockSpec((tile, d), lambda t, e: (t, 0)),
        out_shape=jax.ShapeDtypeStruct((rows, d), F32),
        scratch_shapes=[pltpu.VMEM((tile, d), F32)],
        compiler_params=_cparams("parallel", "arbitrary"),
        name="moe_dense",
    )(hf, cw, x2, gate, wg, wu, wd)


def _shift_rows(u3, k):
    return jnp.pad(u3, ((0, 0), (k, 0), (0, 0)))[:, :u3.shape[1]]


def _even_layer(xp, xs, mp, ms, page_table, kv_pool, kidx_pool, conv_buf, rel_bias, router, experts,
                g_mix, g_ffn, w_in, w_out, q_norm, k_norm, conv_w, conv_b):
    B, S, D = xp.shape
    Bs = xs.shape[0]
    past = page_table.shape[1] * PAGE_SIZE
    wts = _even_weights(w_in, w_out, q_norm, k_norm)
    gm = _gmats()
    wrt, br = router
    wg, wu, wd = experts
    g_mix = g_mix.reshape(1, D)
    g_ffn = g_ffn.reshape(1, D)
    post_consts = [g_ffn, conv_w, conv_b.reshape(1, CONV_CH), wts[1], wrt, br]

    tp = ROW_TILE
    q_p, kv_f, kv_b, qi_p, misc, misc_b, bg, u = _inproj_even(xp.reshape(B * S, D), [mp[0], mp[1]], g_mix, wts, gm,
                                                              tp, S // tp)
    r3 = lambda a: a.reshape(B, S, a.shape[-1])
    oa = _dsa(r3(q_p), r3(qi_p), r3(misc), r3(misc_b), r3(kv_b), _bias_tiles(rel_bias, LANE), gm[2],
              qb=LANE, q_base=0, n_keep=min(TOPK_MAX, S // 4), grp=1)
    u3 = r3(u)
    um1 = _shift_rows(u3, 1).reshape(B * S, CONV_CH)
    um2 = _shift_rows(u3, 2).reshape(B * S, CONV_CH)
    x2, hf, cwt = _row_call(_post_even_kernel, [xp.reshape(B * S, D), oa.reshape(B * S, -1), bg, u, um1, um2],
                            [mp[2], mp[3], mp[4]], post_consts, POST_OUTS, tp, S // tp, "post_even",
                            outs_t=[(N_EXPERTS, F32)])
    tm = MOE_TILE
    xp3 = _moe(hf, cwt.T, x2, mp[5], wg, wu, wd, tm, S // tm).reshape(B, S, D)
    outs_p = (kv_f.reshape(B, S, 2, N_KV_HEADS, HEAD_DIM), misc[:, :IDX_DIM].reshape(B, S, IDX_DIM), u3[:, S - 2:])

    q_s, kv_fs, _, qi_s, misc_s, _, bg_s, u_s = _inproj_even(xs, [ms[0], ms[1]], g_mix, wts, gm, Bs, 1)
    kidx_all = _gather_pages(kidx_pool, page_table, misc_s[:, None, :], LANE, BF16)
    kv_all = _gather_pages(kv_pool, page_table, kv_fs[:, None, :], 2 * LANE, BF16)
    qb = SAMPLE_QB
    pad8 = lambda a: jnp.pad(a[:, None, :], ((0, 0), (0, qb - 1), (0, 0)))
    oa_s = _dsa(pad8(q_s), pad8(qi_s), pad8(misc_s), kidx_all, kv_all, _bias_tiles(rel_bias, qb), gm[2],
                qb=qb, q_base=past, n_keep=min(TOPK_MAX, (past + 1) // 4), grp=_pick_tile(Bs, SAMPLE_GROUP))[:, 0]
    x2s, hfs, cwts = _row_call(_post_even_kernel, [xs, oa_s, bg_s, u_s, conv_buf[:, 1], conv_buf[:, 0]],
                               [ms[2], ms[3], ms[4]], post_consts, POST_OUTS, Bs, 1, "post_even_s",
                               outs_t=[(N_EXPERTS, F32)])
    xs3 = _moe(hfs, cwts.T, x2s, ms[5], wg, wu, wd, Bs, 1)
    outs_s = (kv_fs.reshape(Bs, 1, 2, N_KV_HEADS, HEAD_DIM), misc_s[:, None, :IDX_DIM],
              jnp.concatenate([conv_buf[:, 1:], u_s[:, None, :]], axis=1))
    return xp3, xs3, outs_p, outs_s


O_Z0, O_Q0, O_KVC0, O_KVS0, O_KVW0, O_G0, O_END = 0, 1792, 2816, 3072, 3328, 3584, 3712
P_C = 3 * RWKV_DIM + LORA_W + LORA_A + LORA_G


def _inproj_odd_kernel(x_ref, shift_ref, scale_ref, g_ref, w_ref, qg_ref, ksg_ref, kwg_ref, gone_ref, gtwo_ref,
                       zc_ref, q_ref, kvc_ref, kvs_ref, kvsb_ref, kvw_ref, kvwb_ref, gates_ref):
    h = _modulate(x_ref[...], g_ref[...], shift_ref[...], scale_ref[...])
    z = jnp.dot(h.astype(BF16), w_ref[...], preferred_element_type=F32)
    zc_ref[...] = z[:, O_Z0:O_Q0]
    gone = gone_ref[...]
    gtwo = gtwo_ref[...]
    for t in range(N_ATT_HEADS):
        sl = slice(t * LANE, (t + 1) * LANE)
        q_ref[:, sl] = _group_rms(z[:, O_Q0 + t * LANE:O_Q0 + (t + 1) * LANE], gone, qg_ref[:, sl]).astype(BF16)
    kvc_ref[...] = z[:, O_KVC0:O_KVS0]
    for base, gain_ref, f_ref, b_ref in ((O_KVS0, ksg_ref, kvs_ref, kvsb_ref), (O_KVW0, kwg_ref, kvw_ref, kvwb_ref)):
        k = _group_rms(z[:, base:base + LANE], gtwo, gain_ref[...])
        v = z[:, base + LANE:base + 2 * LANE]
        f_ref[:, 0:LANE] = k
        f_ref[:, LANE:2 * LANE] = v
        b_ref[:, 0:LANE] = k.astype(BF16)
        b_ref[:, LANE:2 * LANE] = v.astype(BF16)
    gates_ref[...] = _sigmoid(z[:, O_G0:O_END])


def _odd_weights(w_in, w_out, q_norm, k_norm):
    d = w_in.shape[0]
    a_q, a_kv = N_ATT_HEADS * HEAD_DIM, 2 * N_KV_HEADS * HEAD_DIM
    o = P_C
    wz = w_in[:, :o]
    wq = w_in[:, o:o + a_q]; o += a_q
    wkv = w_in[:, o:o + 3 * a_kv]; o += 3 * a_kv
    wg = w_in[:, o:]
    wg = jnp.concatenate([wg, jnp.zeros((d, LANE - wg.shape[1]), F32)], -1)
    w_in_p = jnp.concatenate([wz, _pad_q_cols(wq), wkv, wg], -1).astype(BF16)
    w_out_p = jnp.concatenate([w_out[:RWKV_DIM], _pad_o_rows(w_out[RWKV_DIM:])], 0).astype(BF16)
    qg = jnp.tile(q_norm, 2 * N_ATT_HEADS).reshape(1, N_ATT_HEADS * LANE)
    ksg = jnp.tile(k_norm[1], 2).reshape(1, LANE)
    kwg = jnp.tile(k_norm[2], 2).reshape(1, LANE)
    return w_in_p, w_out_p, qg, ksg, kwg


def _inproj_odd(x, mods, g, wts, gm, tile, tpb):
    w_in_p, _, qg, ksg, kwg = wts
    gone, gtwo, _ = gm
    outs = [(P_C, F32), (N_ATT_HEADS * LANE, BF16), (2 * LANE, F32), (2 * LANE, F32), (2 * LANE, BF16),
            (2 * LANE, F32), (2 * LANE, BF16), (LANE, F32)]
    return _row_call(_inproj_odd_kernel, x, mods, [g, w_in_p, qg, ksg, kwg, gone, gtwo], outs, tile, tpb,
                     "inproj_odd")


def _rwkv_pre_kernel(z_ref, zp_ref, mu_ref, w0_ref, a0_ref, kk_ref, ka_ref, wup_ref, aup_ref, gup_ref, gsum_ref,
                     r_o, w_o, k_o, v_o, kk_o, kka_o, g_o):
    z = z_ref[...]
    zm = z + (zp_ref[...] - z) * mu_ref[...]
    r = zm[:, 0:RWKV_DIM]
    k = zm[:, RWKV_DIM:2 * RWKV_DIM]
    v = zm[:, 2 * RWKV_DIM:3 * RWKV_DIM]
    t12 = zm[:, 3 * RWKV_DIM:3 * RWKV_DIM + LANE]
    gd = zm[:, 3 * RWKV_DIM + LANE:P_C]
    xw = w0_ref[...] + _dot(jnp.tanh(t12), wup_ref[...])
    sp = jnp.maximum(-xw, 0.0) + jnp.log(1.0 + jnp.exp(-jnp.abs(xw)))
    w_o[...] = jnp.exp(-jnp.exp(-sp - 0.5))
    a = _sigmoid(a0_ref[...] + _dot(t12, aup_ref[...]))
    g_o[...] = _dot(_sigmoid(gd), gup_ref[...])
    kk = k * kk_ref[...]
    gsum = gsum_ref[...]
    for t in range(RWKV_DIM // LANE):
        sl = slice(t * LANE, (t + 1) * LANE)
        kt = kk[:, sl]
        nrm = jnp.maximum(jnp.sqrt(_dot_split(kt * kt, gsum)), 1e-12)
        kn = kt / nrm
        kk_o[:, sl] = kn
        kka_o[:, sl] = kn * a[:, sl]
    r_o[...] = r
    v_o[...] = v
    k_o[...] = k * (1.0 + (a - 1.0) * ka_ref[...])


def _rwkv_pre(zc, zprev, cpar, gsum, tile):
    mu, w0, w_up, a0, a_up, g_up, k_k, k_a = cpar
    z64 = jnp.zeros((LORA_W, RWKV_DIM), F32)
    consts = [mu.reshape(1, P_C), w0.reshape(1, -1), a0.reshape(1, -1), k_k.reshape(1, -1), k_a.reshape(1, -1),
              jnp.concatenate([w_up, z64], 0).astype(BF16), jnp.concatenate([z64, a_up], 0).astype(BF16),
              g_up.astype(BF16), gsum]
    return _row_call(_rwkv_pre_kernel, [zc, zprev], [], consts, [(RWKV_DIM, F32)] * 7, tile, 1, "rwkv_pre")


SCAN_P = 64
SCAN_VH = HEAD_DIM // 2


def _scan_kernel(kk_ref, w_ref, kka_ref, k_ref, r_ref, v_ref, s0_ref, y_ref, so_ref, st, *, tc):
    ti = pl.program_id(1)

    @pl.when(ti == 0)
    def _():
        st[...] = s0_ref[...]

    def step(t, c):
        kk, w, kka, kt, rt, vt = kk_ref[t], w_ref[t], kka_ref[t], k_ref[t], r_ref[t], v_ref[t]
        ys = []
        for vi in range(SCAN_VH):
            s = st[vi]
            sa = -jnp.sum(s * kk, axis=0, keepdims=True)
            sn = s * w + sa * kka + vt[vi:vi + 1, :] * kt
            st[vi] = sn
            ys.append(jnp.sum(sn * rt, axis=0, keepdims=True))
        y_ref[t] = jnp.concatenate(ys, axis=0)
        return c

    lax.fori_loop(0, tc, step, 0)

    @pl.when(ti == pl.num_programs(1) - 1)
    def _():
        so_ref[...] = st[...]


def _scan_layout_k(x, b, t):
    p = b * RWKV_HEADS
    a = x.reshape(b, t, RWKV_HEADS, HEAD_DIM).transpose(1, 3, 0, 2).reshape(t, HEAD_DIM, p)
    nc = -(-p // SCAN_P)
    a = jnp.pad(a, ((0, 0), (0, 0), (0, nc * SCAN_P - p))).reshape(t, HEAD_DIM, nc, SCAN_P).transpose(2, 0, 1, 3)
    return jnp.concatenate([a, a], -1)


def _scan_layout_v(x, b, t):
    p = b * RWKV_HEADS
    a = x.reshape(b, t, RWKV_HEADS, HEAD_DIM).transpose(1, 3, 0, 2).reshape(t, HEAD_DIM, p)
    nc = -(-p // SCAN_P)
    a = jnp.pad(a, ((0, 0), (0, 0), (0, nc * SCAN_P - p))).reshape(t, HEAD_DIM, nc, SCAN_P).transpose(2, 0, 1, 3)
    return jnp.concatenate([a[:, :, :SCAN_VH], a[:, :, SCAN_VH:]], -1)


def _scan_unlayout_y(y, b, t):
    p = b * RWKV_HEADS
    nc = y.shape[0]
    a = jnp.concatenate([y[..., :SCAN_P], y[..., SCAN_P:]], axis=2)
    a = a.transpose(1, 2, 0, 3).reshape(t, HEAD_DIM, nc * SCAN_P)[:, :, :p]
    return a.reshape(t, HEAD_DIM, b, RWKV_HEADS).transpose(2, 0, 3, 1).reshape(b * t, RWKV_DIM)


def _scan_layout_state(s):
    b = s.shape[0]
    p = b * RWKV_HEADS
    nc = -(-p // SCAN_P)
    a = jnp.pad(s.reshape(p, HEAD_DIM, HEAD_DIM), ((0, nc * SCAN_P - p), (0, 0), (0, 0)))
    a = a.reshape(nc, SCAN_P, HEAD_DIM, HEAD_DIM).transpose(0, 2, 3, 1)
    return jnp.concatenate([a[:, :SCAN_VH], a[:, SCAN_VH:]], -1)


def _scan_unlayout_state(st, b):
    p = b * RWKV_HEADS
    nc = st.shape[0]
    a = jnp.concatenate([st[..., :SCAN_P], st[..., SCAN_P:]], axis=1)
    a = a.transpose(0, 3, 1, 2).reshape(nc * SCAN_P, HEAD_DIM, HEAD_DIM)[:p]
    return a.reshape(b, RWKV_HEADS, HEAD_DIM, HEAD_DIM)


def _rwkv_scan(pre, s0, b, t, tc):
    r, w, k, v, kk, kka, _ = pre
    ks = [_scan_layout_k(a, b, t) for a in (kk, w, kka, k, r)]
    vs = _scan_layout_v(v, b, t)
    s0l = _scan_layout_state(s0)
    nc = s0l.shape[0]
    kspec = pl.BlockSpec((None, tc, HEAD_DIM, LANE), lambda c, i: (c, i, 0, 0))
    vspec = pl.BlockSpec((None, tc, SCAN_VH, LANE), lambda c, i: (c, i, 0, 0))
    sspec = pl.BlockSpec((None, SCAN_VH, HEAD_DIM, LANE), lambda c, i: (c, 0, 0, 0))
    y, so = pl.pallas_call(
        functools.partial(_scan_kernel, tc=tc),
        grid=(nc, t // tc),
        in_specs=[kspec] * 5 + [vspec, sspec],
        out_specs=[vspec, sspec],
        out_shape=[jax.ShapeDtypeStruct((nc, t, SCAN_VH, LANE), F32),
                   jax.ShapeDtypeStruct((nc, SCAN_VH, HEAD_DIM, LANE), F32)],
        scratch_shapes=[pltpu.VMEM((SCAN_VH, HEAD_DIM, LANE), F32)],
        compiler_params=_cparams("parallel", "arbitrary"),
        name="rwkv_scan",
    )(*ks, vs, s0l)
    return _scan_unlayout_y(y, b, t), _scan_unlayout_state(so, b)


def _compress_kernel(x_ref, pe_ref, w_ref, kg_ref, gtwo_ref, o_ref):
    z = jnp.dot((x_ref[...] + pe_ref[...]).astype(BF16), w_ref[...], preferred_element_type=F32)
    o_ref[:, 0:LANE] = _group_rms(z[:, 0:LANE], gtwo_ref[...], kg_ref[...])
    o_ref[:, LANE:2 * LANE] = z[:, LANE:2 * LANE]


def _compress_weights(cmp_pe, cmp_w, k_norm_c):
    wk = cmp_w[0].reshape(CMP_BLOCK, HEAD_DIM, HEAD_DIM)
    wv = cmp_w[1].reshape(CMP_BLOCK, HEAD_DIM, HEAD_DIM)
    full = jnp.zeros((CMP_BLOCK, 4, HEAD_DIM, 4, HEAD_DIM), F32)
    for s, w in enumerate((wk, wk, wv, wv)):
        full = full.at[:, s, :, s, :].set(w)
    pe = jnp.stack([cmp_pe[0], cmp_pe[0], cmp_pe[1], cmp_pe[1]], axis=1)
    return (full.reshape(CMP_BLOCK * 4 * HEAD_DIM, 4 * HEAD_DIM).astype(BF16), pe.reshape(1, -1),
            jnp.tile(k_norm_c, 2).reshape(1, LANE))


def _compress(rows, cw, gtwo, tile):
    wfull, pe, kg = cw
    return _row_call(_compress_kernel, rows, [], [pe, wfull, kg, gtwo], [(2 * LANE, F32)], tile, 1, "nsa_compress")[0]


def _nsa_kernel(q_ref, gates_ref, kcv_ref, kvs_ref, kvw_ref, bias_ref, biasc_ref, pair_ref, o_ref,
                *, grp, qb, q_base, n_cmp, n_sel, w_off, single):
    q0 = q_base if single else q_base + pl.program_id(1) * qb
    nblk = (q0 + qb - 1) // LANE + 1
    dq = q0 // LANE
    seqs = range(grp)
    heads = range(N_KV_HEADS)
    row = lax.broadcasted_iota(I32, (qb, LANE), 0)
    lane = lax.broadcasted_iota(I32, (qb, LANE), 1)
    t_pos = q0 + row
    qs = [[_stack_heads(q_ref.at[g], hk) for hk in heads] for g in seqs]

    def gate_col(g, br, hk):
        gates = gates_ref[g]
        return jnp.concatenate([gates[:, br * N_ATT_HEADS + hk * GQA + a:br * N_ATT_HEADS + hk * GQA + a + 1]
                                for a in range(GQA)], axis=0)

    mask_c4 = _tile4(((lane * CMP_BLOCK + CMP_BLOCK - 1) <= t_pos) & (lane < n_cmp))
    cur = t_pos // SEL_BLOCK
    forced = (lane == 0) | (lane == cur) | (lane == cur - 1)
    sel_causal = lane * SEL_BLOCK <= t_pos
    lane_f = lane.astype(F32)
    o_cmp = [[None] * N_KV_HEADS for _ in seqs]
    selm = [[None] * N_KV_HEADS for _ in seqs]
    for g in seqs:
        kc = kcv_ref[g, :, 0:LANE]
        vc = kcv_ref[g, :, LANE:2 * LANE]
        for hk in heads:
            s = _dot_nt(qs[g][hk], kc) * ATT_SCALE + biasc_ref[hk * GQA:(hk + 1) * GQA].reshape(GQA * qb, LANE)
            s = jnp.where(mask_c4, s, NEG)
            e = jnp.exp(s - jnp.max(s, axis=1, keepdims=True))
            p = jnp.where(mask_c4, e / jnp.sum(e, axis=1, keepdims=True), 0.0)
            o_cmp[g][hk] = jnp.dot(p.astype(BF16), vc, preferred_element_type=F32)
            ps = p[0:qb]
            for a in range(1, GQA):
                ps = ps + p[a * qb:(a + 1) * qb]
            score = _dot_split(ps, pair_ref[...])
            score = jnp.where(sel_causal, jnp.where(forced, FORCE, score), NEG)
            picked = jnp.zeros((qb, LANE), F32)
            for _ in range(n_sel):
                mx = jnp.max(score, axis=1, keepdims=True)
                first = jnp.min(jnp.where(score == mx, lane_f, float(LANE)), axis=1, keepdims=True)
                hit = lane_f == first
                picked = jnp.where(hit, 1.0, picked)
                score = jnp.where(hit, TAKEN, score)
            selm[g][hk] = picked.astype(BF16)

    def key_blocks(ref, g, jb):
        off = pl.multiple_of(jb * LANE, LANE)
        return ref[g, pl.ds(off, LANE), 0:LANE], ref[g, pl.ds(off, LANE), LANE:2 * LANE]

    def biases_of(jb):
        dsel = jnp.minimum(dq - jb, 2)
        return [bias_ref[dsel, hk * GQA:(hk + 1) * GQA].reshape(GQA * qb, LANE) for hk in heads]

    init = tuple(tuple(_flash_init(GQA * qb) for _ in heads) for _ in seqs)

    erow = lax.broadcasted_iota(I32, (LANE, LANE), 0)
    ecol = lax.broadcasted_iota(I32, (LANE, LANE), 1)

    def slc_body(jb, carry):
        expand = jnp.where(erow == 2 * jb + ecol // SEL_BLOCK, 1.0, 0.0).astype(BF16)
        causal = jb * LANE + lane <= t_pos
        biases = biases_of(jb)
        out = []
        for g in seqs:
            kb, vb = key_blocks(kvs_ref, g, jb)
            res = []
            for hk in heads:
                tok = jnp.dot(selm[g][hk], expand, preferred_element_type=F32) > 0.5
                res.append(_flash_step(qs[g][hk], kb, vb, biases[hk], _tile4(tok & causal), *carry[g][hk]))
            out.append(tuple(res))
        return tuple(out)

    res_s = lax.fori_loop(0, nblk, slc_body, init)

    def win_body(jb, carry):
        dist = t_pos - (jb * LANE + lane)
        msk = _tile4((dist >= 0) & (dist < WINDOW))
        biases = biases_of(jb)
        out = []
        for g in seqs:
            kb, vb = key_blocks(kvw_ref, g, jb - w_off)
            out.append(tuple(_flash_step(qs[g][hk], kb, vb, biases[hk], msk, *carry[g][hk]) for hk in heads))
        return tuple(out)

    res_w = lax.fori_loop(jnp.maximum(dq - WINDOW // LANE, 0), dq + 1, win_body, init)

    for g in seqs:
        outs = []
        for hk in heads:
            o_s = res_s[g][hk][2] / res_s[g][hk][1]
            o_w = res_w[g][hk][2] / res_w[g][hk][1]
            outs.append(gate_col(g, 0, hk) * o_cmp[g][hk] + gate_col(g, 1, hk) * o_s + gate_col(g, 2, hk) * o_w)
        _write_heads(o_ref.at[g], outs, qb)


def _bias_cmp(rel_bias, q_starts, qb):
    q0 = jnp.asarray(q_starts, I32)[:, None, None]
    r = jnp.arange(qb)[None, :, None]
    n = jnp.arange(LANE)[None, None, :]
    return _bias_lookup(rel_bias, q0 + r - (n * CMP_BLOCK + CMP_BLOCK - 1)).transpose(0, 3, 1, 2)


def _nsa(q, gates, kcv, kvs_b, kvw_b, bias, bias_c, pair, *, qb, q_base, n_cmp, n_sel, w_off, grp):
    b, rows, _ = q.shape
    nq = rows // qb
    kern = functools.partial(_nsa_kernel, grp=grp, qb=qb, q_base=q_base, n_cmp=n_cmp, n_sel=n_sel, w_off=w_off,
                             single=(nq == 1))
    qspec = lambda w: pl.BlockSpec((grp, qb, w), lambda bi, i: (bi, i, 0))
    kspec = lambda a: pl.BlockSpec((grp,) + a.shape[1:], lambda bi, i: (bi, 0, 0))
    return pl.pallas_call(
        kern,
        grid=(b // grp, nq),
        in_specs=[qspec(N_ATT_HEADS * LANE), qspec(LANE), kspec(kcv), kspec(kvs_b), kspec(kvw_b), _const_spec(bias),
                  pl.BlockSpec((None,) + bias_c.shape[1:], lambda bi, i: (i, 0, 0, 0)), _const_spec(pair)],
        out_specs=qspec(N_ATT_HEADS * LANE),
        out_shape=jax.ShapeDtypeStruct((b, rows, N_ATT_HEADS * LANE), BF16),
        compiler_params=_cparams("parallel", "parallel"),
        name="nsa_attention",
    )(q, gates, kcv, kvs_b, kvw_b, bias, bias_c, pair)


def _post_odd_kernel(x_ref, y_ref, r_ref, k_ref, v_ref, g_ref, od_ref, gate_ref, shf_ref, scf_ref,
                     gf_ref, lnw_ref, lnb_ref, rk_ref, gtwo_ref, wo_ref, wrt_ref, br_ref, x2_ref, hf_ref, cwt_ref):
    gtwo = gtwo_ref[...]
    mix = jnp.dot(od_ref[...], wo_ref[RWKV_DIM:RWKV_DIM + N_ATT_HEADS * LANE, :], preferred_element_type=F32)
    for t in range(RWKV_DIM // LANE):
        sl = slice(t * LANE, (t + 1) * LANE)
        y = y_ref[:, sl]
        dlt = y - _dot_split(y, gtwo)
        yn = (dlt * lax.rsqrt(_dot_split(dlt * dlt, gtwo) + GN_EPS)) * lnw_ref[:, sl] + lnb_ref[:, sl]
        dot_rk = _dot_split(r_ref[:, sl] * k_ref[:, sl] * rk_ref[:, sl], gtwo) * float(HEAD_DIM)
        oc = (yn + dot_rk * v_ref[:, sl]) * g_ref[:, sl]
        mix = mix + jnp.dot(oc.astype(BF16), wo_ref[sl, :], preferred_element_type=F32)
    _post_tail(x_ref[...], mix, gate_ref[...], gf_ref[...], shf_ref[...], scf_ref[...], wrt_ref, br_ref,
               x2_ref, hf_ref, cwt_ref)


def _odd_layer(xp, xs, mp, ms, page_table, wkv0, shift0, cmp_pool, slc_pool, win_buf, rel_bias, router, experts,
               g_mix, g_ffn, w_in, w_out, cpar, r_k, ln_w, ln_b, q_norm, k_norm, cmp_pe, cmp_w):
    B, S, D = xp.shape
    Bs = xs.shape[0]
    n_pages = page_table.shape[1]
    past = n_pages * PAGE_SIZE
    wts = _odd_weights(w_in, w_out, q_norm, k_norm)
    gm = _gmats()
    gone, gtwo, _ = gm
    gsum = (gtwo.astype(F32) * HEAD_DIM).astype(BF16)
    i = jnp.arange(LANE)
    pair = jnp.where(i[:, None] // 2 == i[None, :], 1.0, 0.0).astype(BF16)
    cw = _compress_weights(cmp_pe, cmp_w, k_norm[0])
    wrt, br = router
    wg, wu, wd = experts
    g_mix = g_mix.reshape(1, D)
    post_consts = [g_ffn.reshape(1, D), ln_w.reshape(1, -1), ln_b.reshape(1, -1), r_k.reshape(1, -1), gtwo, wts[1],
                   wrt, br]
    w_eff = win_buf.shape[1]

    tp = ROW_TILE
    zc, q_p, kvc, kvs, kvs_b, kvw, kvw_b, gates = _inproj_odd(xp.reshape(B * S, D), [mp[0], mp[1]], g_mix, wts, gm,
                                                               tp, S // tp)
    r3 = lambda a: a.reshape(B, S, a.shape[-1])
    pre = _rwkv_pre(zc, _shift_rows(r3(zc), 1).reshape(B * S, P_C), cpar, gsum, tp)
    y, wkv_p = _rwkv_scan(pre, jnp.zeros((B, RWKV_HEADS, HEAD_DIM, HEAD_DIM), F32), B, S, SCAN_TIME_CHUNK)
    n_cmp = S // CMP_BLOCK
    kcv = _compress(kvc.reshape(B * n_cmp, CMP_BLOCK * 2 * LANE), cw, gtwo, _pick_tile(B * n_cmp, 256))
    kcv = jnp.pad(kcv.reshape(B, n_cmp, 2 * LANE), ((0, 0), (0, LANE - n_cmp), (0, 0))).astype(BF16)
    n_slc = -(-S // SEL_BLOCK)
    od = _nsa(r3(q_p), r3(gates), kcv, r3(kvs_b), r3(kvw_b), _bias_tiles(rel_bias, LANE),
              _bias_cmp(rel_bias, [j * LANE for j in range(S // LANE)], LANE), pair,
              qb=LANE, q_base=0, n_cmp=n_cmp, n_sel=min(N_SEL_BLOCKS, n_slc), w_off=0, grp=1)
    x2, hf, cwt = _row_call(_post_odd_kernel,
                            [xp.reshape(B * S, D), y, pre[0], pre[2], pre[3], pre[6], od.reshape(B * S, -1)],
                            [mp[2], mp[3], mp[4]], post_consts, POST_OUTS, tp, S // tp, "post_odd",
                            outs_t=[(N_EXPERTS, F32)])
    tm = MOE_TILE
    xp3 = _moe(hf, cwt.T, x2, mp[5], wg, wu, wd, tm, S // tm).reshape(B, S, D)
    kv5 = lambda a, n: a.reshape(-1, n, 2, N_KV_HEADS, HEAD_DIM)
    outs_p = (wkv_p, r3(zc)[:, S - 1], kv5(kvc, S), kv5(kvs, S), kv5(kvw, S)[:, S - min(WINDOW, S):])

    zc_s, q_s, kvc_s, kvs_s, _, kvw_s, _, gates_s = _inproj_odd(xs, [ms[0], ms[1]], g_mix, wts, gm, Bs, 1)
    pre_s = _rwkv_pre(zc_s, shift0, cpar, gsum, Bs)
    y_s, wkv_s = _rwkv_scan(pre_s, wkv0, Bs, 1, 1)
    n_pool = cmp_pool.shape[0]
    per_page = PAGE_SIZE // CMP_BLOCK
    kc_pool = _compress(cmp_pool.reshape(n_pool * per_page, CMP_BLOCK * 2 * LANE), cw, gtwo,
                        _pick_tile(n_pool * per_page, 256))
    kcv_s = _gather_pages(kc_pool.reshape(n_pool, per_page, 2 * LANE), page_table,
                          jnp.zeros((Bs, 1, 2 * LANE), F32), 2 * LANE, F32)
    n_cmp_s = (past + 1) // CMP_BLOCK
    kcv_s = jnp.pad(kcv_s, ((0, 0), (0, LANE - kcv_s.shape[1]), (0, 0))).astype(BF16)
    kvs_all = _gather_pages(slc_pool, page_table, kvs_s[:, None, :], 2 * LANE, BF16)
    win_all = jnp.concatenate([win_buf.reshape(Bs, w_eff, 2 * LANE), kvw_s[:, None, :],
                               jnp.zeros((Bs, LANE - 1, 2 * LANE), F32)], axis=1).astype(BF16)
    qb = SAMPLE_QB
    pad8 = lambda a: jnp.pad(a[:, None, :], ((0, 0), (0, qb - 1), (0, 0)))
    n_slc_s = -(-(past + 1) // SEL_BLOCK)
    od_s = _nsa(pad8(q_s), pad8(gates_s), kcv_s, kvs_all, win_all, _bias_tiles(rel_bias, qb),
                _bias_cmp(rel_bias, [past], qb), pair, qb=qb, q_base=past, n_cmp=n_cmp_s,
                n_sel=min(N_SEL_BLOCKS, n_slc_s), w_off=(past - w_eff) // LANE,
                grp=_pick_tile(Bs, SAMPLE_GROUP))[:, 0]
    x2s, hfs, cwts = _row_call(_post_odd_kernel, [xs, y_s, pre_s[0], pre_s[2], pre_s[3], pre_s[6], od_s],
                               [ms[2], ms[3], ms[4]], post_consts, POST_OUTS, Bs, 1, "post_odd_s",
                               outs_t=[(N_EXPERTS, F32)])
    xs3 = _moe(hfs, cwts.T, x2s, ms[5], wg, wu, wd, Bs, 1)
    win_new = jnp.concatenate([win_buf[:, 1:], kv5(kvw_s, 1)], axis=1)
    outs_s = (wkv_s, zc_s, kv5(kvc_s, 1), kv5(kvs_s, 1), win_new)
    return xp3, xs3, outs_p, outs_s


def _mods(c_p, c_s, w, b):
    nb = c_p.shape[0]
    m = _ada(jnp.concatenate([c_p, c_s], 0), w.astype(BF16), b)
    parts = jnp.split(m, 6, axis=-1)
    return [p[:nb, None, :] for p in parts], [p[None, nb:, :] for p in parts]


def _forward(x_prompt, x_sample, c_prompt, c_sample, page_table, cache_a_kv, cache_a_kidx, state_b_conv,
             state_c_wkv, state_c_shift, cache_d_cmp, cache_d_slc, cache_d_win, rel_bias, w_router, b_router,
             w_ada, b_ada, g_norm_mix, g_norm_ffn, w_expert_gate, w_expert_up, w_expert_down, e_w_in, e_w_out,
             a_q_norm, a_k_norm, b_conv_w, b_conv_b, o_w_in, o_w_out, c_mu, c_w0, c_w_up, c_a0, c_a_up,
             c_g_up, c_k_k, c_k_a, c_r_k, c_ln_w, c_ln_b, d_q_norm, d_k_norm, d_cmp_pe, d_cmp_w):
    assert w_ada.shape[0] == 2 and e_w_in.shape[0] == 1 and o_w_in.shape[0] == 1
    B, S, D = x_prompt.shape
    Bs = x_sample.shape[0]
    assert x_sample.shape[1] == 1
    xp, xs = x_prompt, x_sample.reshape(Bs, D)
    router = (w_router.T, b_router.reshape(N_EXPERTS, 1))
    n_pool = cache_a_kv.shape[1]
    pool3 = lambda a: a.reshape(n_pool, PAGE_SIZE, -1)
    experts = lambda l: tuple(w[l].astype(BF16) for w in (w_expert_gate, w_expert_up, w_expert_down))

    mp, ms = _mods(c_prompt, c_sample, w_ada[0], b_ada[0])
    xp, xs, ep, es = _even_layer(xp, xs, mp, ms, page_table, pool3(cache_a_kv[0]), cache_a_kidx[0], state_b_conv[0],
                                 rel_bias, router, experts(0), g_norm_mix[0], g_norm_ffn[0], e_w_in[0], e_w_out[0],
                                 a_q_norm[0], a_k_norm[0], b_conv_w[0], b_conv_b[0])
    mp, ms = _mods(c_prompt, c_sample, w_ada[1], b_ada[1])
    cpar = (c_mu[0], c_w0[0], c_w_up[0], c_a0[0], c_a_up[0], c_g_up[0], c_k_k[0], c_k_a[0])
    xp, xs, op, os_ = _odd_layer(xp, xs, mp, ms, page_table, state_c_wkv[0], state_c_shift[0], pool3(cache_d_cmp[0]),
                                 pool3(cache_d_slc[0]), cache_d_win[0], rel_bias, router, experts(1), g_norm_mix[1],
                                 g_norm_ffn[1], o_w_in[0], o_w_out[0], cpar, c_r_k[0].reshape(-1), c_ln_w[0],
                                 c_ln_b[0], d_q_norm[0], d_k_norm[0], d_cmp_pe[0], d_cmp_w[0])
    stack = lambda ts: tuple(a[None] for a in ts)
    return (xp, xs.reshape(Bs, 1, D)) + stack(ep) + stack(op) + stack(es) + stack(os_)


def kernel(x_prompt, x_sample, c_prompt, c_sample, page_table, cache_a_kv, cache_a_kidx, state_b_conv, state_c_wkv, state_c_shift, cache_d_cmp, cache_d_slc, cache_d_win, rel_bias, w_router, b_router, w_ada, b_ada, g_norm_mix, g_norm_ffn, w_expert_gate, w_expert_up, w_expert_down, e_w_in, e_w_out, a_q_norm, a_k_norm, b_conv_w, b_conv_b, o_w_in, o_w_out, c_mu, c_w0, c_w_up, c_a0, c_a_up, c_g_up, c_k_k, c_k_a, c_r_k, c_ln_w, c_ln_b, d_q_norm, d_k_norm, d_cmp_pe, d_cmp_w):
    return _forward(x_prompt, x_sample, c_prompt, c_sample, page_table, cache_a_kv, cache_a_kidx, state_b_conv,
                    state_c_wkv, state_c_shift, cache_d_cmp, cache_d_slc, cache_d_win, rel_bias, w_router, b_router,
                    w_ada, b_ada, g_norm_mix, g_norm_ffn, w_expert_gate, w_expert_up, w_expert_down, e_w_in, e_w_out,
                    a_q_norm, a_k_norm, b_conv_w, b_conv_b, o_w_in, o_w_out, c_mu, c_w0, c_w_up, c_a0, c_a_up,
                    c_g_up, c_k_k, c_k_a, c_r_k, c_ln_w, c_ln_b, d_q_norm, d_k_norm, d_cmp_pe, d_cmp_w)
```

```python
import functools
import math

import jax
import jax.numpy as jnp
from jax import lax
from jax.experimental import pallas as pl
from jax.experimental.pallas import tpu as pltpu

F32 = jnp.float32
BF16 = jnp.bfloat16
I32 = jnp.int32

LANE = 128
HEAD_DIM = 64
N_ATT_HEADS = 8
N_KV_HEADS = 2
GQA = N_ATT_HEADS // N_KV_HEADS
IDX_HEADS = 4
IDX_DIM = 64
TOPK_MAX = 256
CONV_CH = 512
RWKV_HEADS = 8
RWKV_DIM = RWKV_HEADS * HEAD_DIM
LORA_W = 64
LORA_A = 64
LORA_G = 128
GN_EPS = 64e-5
CMP_BLOCK = 32
SEL_BLOCK = 64
N_SEL_BLOCKS = 8
WINDOW = 512
N_BUCKETS = 32
MAX_DISTANCE = 128
N_EXPERTS = 16
N_GROUPS = 4
EXPERTS_PER_GROUP = N_EXPERTS // N_GROUPS
D_EXPERT = 512
PAGE_SIZE = 128
RMS_EPS = 1e-6
NEG = -1e30
FORCE = 1e9
TAKEN = -3e38
ATT_SCALE = HEAD_DIM ** -0.5
VMEM_LIMIT = 56 * 1024 * 1024
ROW_TILE = 256
MOE_TILE = 512
SCAN_TIME_CHUNK = 32
SAMPLE_QB = 8
SAMPLE_GROUP = 4


def _cparams(*sem):
    return pltpu.CompilerParams(dimension_semantics=sem, vmem_limit_bytes=VMEM_LIMIT)


def _pick_tile(rows, pref):
    t = min(pref, rows)
    while rows % t or (t % 8 and t != rows):
        t -= 1
    return t


def _pick_group(n, pref):
    g = min(pref, n)
    while n % g:
        g -= 1
    return g


def _const_spec(a):
    nd = a.ndim
    return pl.BlockSpec(a.shape, lambda *_: (0,) * nd)


def _dot(a, b):
    return jnp.dot(a.astype(BF16), b.astype(BF16), preferred_element_type=F32)


def _dot_nt(a, b):
    return lax.dot_general(a.astype(BF16), b.astype(BF16), (((1,), (1,)), ((), ())),
                           preferred_element_type=F32)


def _dot_split(x, m):
    hi = x.astype(BF16)
    r1 = x - hi.astype(F32)
    mid = r1.astype(BF16)
    lo = (r1 - mid.astype(F32)).astype(BF16)
    return (jnp.dot(hi, m, preferred_element_type=F32) + jnp.dot(mid, m, preferred_element_type=F32)
            + jnp.dot(lo, m, preferred_element_type=F32))


def _sigmoid(x):
    return 1.0 / (1.0 + jnp.exp(-x))


def _silu(x):
    return x * _sigmoid(x)


def _modulate(x, g, shift, scale):
    y = x * lax.rsqrt(jnp.mean(x * x, axis=-1, keepdims=True) + RMS_EPS)
    return (y * g) * (1.0 + scale) + shift


def _group_rms(t, gmat, gain):
    ms = _dot_split(t * t, gmat)
    return (t * lax.rsqrt(ms + RMS_EPS)) * gain


def _ada_kernel(c_ref, w_ref, b_ref, o_ref):
    o_ref[...] = _dot(_silu(c_ref[...]), w_ref[...]) + b_ref[...]


def _ada(c, w_bf, b):
    r, d = c.shape
    n = w_bf.shape[1]
    tn = 512
    return pl.pallas_call(
        _ada_kernel,
        grid=(n // tn,),
        in_specs=[pl.BlockSpec((r, d), lambda j: (0, 0)),
                  pl.BlockSpec((d, tn), lambda j: (0, j)),
                  pl.BlockSpec((1, tn), lambda j: (0, j))],
        out_specs=pl.BlockSpec((r, tn), lambda j: (0, j)),
        out_shape=jax.ShapeDtypeStruct((r, n), F32),
        compiler_params=_cparams("parallel"),
        name="ada_mod",
    )(c, w_bf, b.reshape(1, n))


E_Q0, E_KV0, E_QI0, E_MISC0, E_BG0, E_CG0, E_XIN0, E_END = 0, 1024, 1280, 1792, 1920, 2432, 2944, 3456


def _inproj_even_kernel(x_ref, shift_ref, scale_ref, g_ref, w_ref, qg_ref, kg_ref, gone_ref, gtwo_ref,
                        q_ref, kv_ref, kvb_ref, qi_ref, misc_ref, miscb_ref, bg_ref, u_ref):
    h = _modulate(x_ref[...], g_ref[...], shift_ref[...], scale_ref[...])
    z = jnp.dot(h.astype(BF16), w_ref[...], preferred_element_type=F32)
    gone = gone_ref[...]
    for t in range(N_ATT_HEADS):
        sl = slice(t * LANE, (t + 1) * LANE)
        q_ref[:, sl] = _group_rms(z[:, E_Q0 + t * LANE:E_Q0 + (t + 1) * LANE], gone, qg_ref[:, sl]).astype(BF16)
    k = _group_rms(z[:, E_KV0:E_KV0 + LANE], gtwo_ref[...], kg_ref[...])
    v = z[:, E_KV0 + LANE:E_KV0 + 2 * LANE]
    kv_ref[:, 0:LANE] = k
    kv_ref[:, LANE:2 * LANE] = v
    kvb_ref[:, 0:LANE] = k.astype(BF16)
    kvb_ref[:, LANE:2 * LANE] = v.astype(BF16)
    qi_ref[...] = z[:, E_QI0:E_MISC0].astype(BF16)
    misc = z[:, E_MISC0:E_BG0]
    misc_ref[...] = misc
    miscb_ref[...] = misc.astype(BF16)
    bg_ref[...] = z[:, E_BG0:E_CG0]
    u_ref[...] = z[:, E_CG0:E_XIN0] * z[:, E_XIN0:E_END]


def _row_call(kernel, xs, mods, consts, outs, tile, tpb, name, outs_t=()):
    if not isinstance(xs, (list, tuple)):
        xs = [xs]
    rows = xs[0].shape[0]
    n_tiles = rows // tile
    in_specs = [pl.BlockSpec((tile, x.shape[1]), lambda t: (t, 0)) for x in xs]
    for m in mods:
        in_specs.append(pl.BlockSpec((None,) + m.shape[1:], lambda t: (t // tpb, 0, 0)))
    in_specs += [_const_spec(c) for c in consts]
    out_specs = [pl.BlockSpec((tile, w), lambda t: (t, 0)) for (w, _) in outs]
    out_shape = [jax.ShapeDtypeStruct((rows, w), dt) for (w, dt) in outs]
    out_specs += [pl.BlockSpec((hh, tile), lambda t: (0, t)) for (hh, _) in outs_t]
    out_shape += [jax.ShapeDtypeStruct((hh, rows), dt) for (hh, dt) in outs_t]
    return pl.pallas_call(kernel, grid=(n_tiles,), in_specs=in_specs, out_specs=out_specs, out_shape=out_shape,
                          compiler_params=_cparams("parallel"), name=name)(*xs, *mods, *consts)


def _pad_q_cols(wq):
    d = wq.shape[0]
    w = wq.reshape(d, N_ATT_HEADS, HEAD_DIM)
    z = jnp.zeros_like(w)
    lo = jnp.concatenate([w, z], -1)
    hi = jnp.concatenate([z, w], -1)
    sel = (jnp.arange(N_ATT_HEADS) >= GQA)[None, :, None]
    return jnp.where(sel, hi, lo).reshape(d, N_ATT_HEADS * LANE)


def _pad_o_rows(wo):
    return _pad_q_cols(wo.T).T


def _gmats():
    i = jnp.arange(LANE)
    gone = jnp.full((LANE, LANE), 1.0 / HEAD_DIM, F32).astype(BF16)
    gtwo = jnp.where((i[:, None] // HEAD_DIM) == (i[None, :] // HEAD_DIM), 1.0 / HEAD_DIM, 0.0).astype(BF16)
    tri = jnp.where(i[:, None] <= i[None, :], 1.0, 0.0).astype(BF16)
    return gone, gtwo, tri


def _even_weights(w_in, w_out, q_norm, k_norm):
    d = w_in.shape[0]
    a_q, a_kv = N_ATT_HEADS * HEAD_DIM, 2 * N_KV_HEADS * HEAD_DIM
    o = 0
    wq = w_in[:, o:o + a_q]; o += a_q
    wkv = w_in[:, o:o + a_kv]; o += a_kv
    wqi = w_in[:, o:o + IDX_HEADS * IDX_DIM]; o += IDX_HEADS * IDX_DIM
    wki = w_in[:, o:o + IDX_DIM]; o += IDX_DIM
    wwi = w_in[:, o:o + IDX_HEADS]; o += IDX_HEADS
    wrest = w_in[:, o:]
    wqi = jnp.concatenate([wqi.reshape(d, IDX_HEADS, IDX_DIM), jnp.zeros((d, IDX_HEADS, LANE - IDX_DIM), F32)],
                          -1).reshape(d, IDX_HEADS * LANE)
    wmisc = jnp.concatenate([wki, wwi, jnp.zeros((d, LANE - IDX_DIM - IDX_HEADS), F32)], -1)
    w_in_p = jnp.concatenate([_pad_q_cols(wq), wkv, wqi, wmisc, wrest], -1).astype(BF16)
    w_out_p = jnp.concatenate([_pad_o_rows(w_out[:a_q]), w_out[a_q:]], 0).astype(BF16)
    qg = jnp.tile(q_norm, 2 * N_ATT_HEADS).reshape(1, N_ATT_HEADS * LANE)
    kg = jnp.tile(k_norm, 2).reshape(1, LANE)
    return w_in_p, w_out_p, qg, kg


def _inproj_even(x, mods, g, wts, gm, tile, tpb):
    w_in_p, _, qg, kg = wts
    gone, gtwo, _ = gm
    outs = [(N_ATT_HEADS * LANE, BF16), (2 * LANE, F32), (2 * LANE, BF16), (IDX_HEADS * LANE, BF16),
            (LANE, F32), (LANE, BF16), (CONV_CH, F32), (CONV_CH, F32)]
    return _row_call(_inproj_even_kernel, x, mods, [g, w_in_p, qg, kg, gone, gtwo], outs, tile, tpb, "inproj_even")


def _t5_bucket(dist):
    dist = jnp.maximum(dist, 0)
    exact = N_BUCKETS // 2
    far = exact + (jnp.log(jnp.maximum(dist, 1).astype(F32) / exact)
                   / math.log(MAX_DISTANCE / exact) * (N_BUCKETS - exact)).astype(I32)
    return jnp.where(dist < exact, dist, jnp.minimum(far, N_BUCKETS - 1))


def _bias_lookup(rel_bias, dist):
    onehot = (_t5_bucket(dist)[..., None] == jnp.arange(N_BUCKETS)).astype(F32)
    return jnp.einsum("...k,kh->...h", onehot, rel_bias, precision=lax.Precision.HIGHEST)


def _bias_row(rel_bias, t_pos):
    return _bias_lookup(rel_bias, t_pos - jnp.arange(t_pos + LANE)).T


def _bias_tiles(rel_bias, qb):
    r = jnp.arange(qb)[:, None]
    c = jnp.arange(LANE)[None, :]
    tiles = [_bias_lookup(rel_bias, d * LANE + r - c) for d in range(3)]
    return jnp.stack(tiles).transpose(0, 3, 1, 2)


def _flash_step(qs, kb, vb, bias, msk, m, l, acc):
    s = _dot_nt(qs, kb) * ATT_SCALE + bias
    s = jnp.where(msk, s, NEG)
    m_new = jnp.maximum(m, jnp.max(s, axis=1, keepdims=True))
    p = jnp.where(msk, jnp.exp(s - m_new), 0.0)
    alpha = jnp.exp(m - m_new)
    l = alpha * l + jnp.sum(p, axis=1, keepdims=True)
    acc = alpha * acc + jnp.dot(p.astype(BF16), vb, preferred_element_type=F32)
    return m_new, l, acc


def _flash_init(rows):
    return (jnp.full((rows, 1), NEG, F32), jnp.zeros((rows, 1), F32), jnp.zeros((rows, LANE), F32))


def _stack_heads(q_ref, hk):
    return jnp.concatenate([q_ref[:, (hk * GQA + g) * LANE:(hk * GQA + g + 1) * LANE] for g in range(GQA)], axis=0)


def _tile4(x):
    return jnp.concatenate([x] * GQA, axis=0)


def _loop(lo, hi, body, init, unroll):
    if unroll:
        for j in range(lo, hi):
            init = body(j, init)
        return init
    return lax.fori_loop(lo, hi, body, init)


def _dsa_kernel(q_ref, qi_ref, misc_ref, kidx_ref, kv_ref, bias_ref, tri_ref, o_ref, key_s,
                *, grp, qb, q_base, n_keep, single):
    q0 = q_base if single else q_base + pl.program_id(1) * qb
    nblk = (q0 + qb - 1) // LANE + 1
    dq = q0 // LANE
    seqs = range(grp)
    row = lax.broadcasted_iota(I32, (qb, LANE), 0)
    lane = lax.broadcasted_iota(I32, (qb, LANE), 1)
    t_pos = q0 + row
    wis = [[jnp.broadcast_to(misc_ref[g][:, IDX_DIM + h:IDX_DIM + h + 1], (qb, LANE)) for h in range(IDX_HEADS)]
           for g in seqs]
    idx_scale = (IDX_HEADS * IDX_DIM) ** -0.5

    def pass_a(j, c):
        off = pl.multiple_of(j * LANE, LANE)
        causal = j * LANE + lane <= t_pos
        for g in seqs:
            kb = kidx_ref[g, pl.ds(off, LANE), :]
            acc = jnp.zeros((qb, LANE), F32)
            for h in range(IDX_HEADS):
                acc = acc + jnp.maximum(_dot_nt(qi_ref[g, :, h * LANE:(h + 1) * LANE], kb), 0.0) * wis[g][h]
            sc = jnp.where(causal, acc * idx_scale, NEG)
            sc = jnp.where(sc == 0.0, 0.0, sc)
            bits = lax.bitcast_convert_type(sc, I32)
            key_s[g, j] = jnp.where(bits < 0, bits ^ jnp.int32(0x7FFFFFFF), bits)
        return c

    lax.fori_loop(0, nblk, pass_a, 0)

    def count(pred):
        def body(j, accs):
            return tuple(a + jnp.where(pred(g, key_s[g, j]), 1.0, 0.0) for g, a in zip(seqs, accs))
        accs = _loop(0, nblk, body, tuple(jnp.zeros((qb, LANE), F32) for _ in seqs), single)
        return [jnp.sum(a, axis=1, keepdims=True) for a in accs]

    keep = float(n_keep)
    int_min = jnp.int32(-2 ** 31)
    thr = tuple(jnp.where(c >= keep, jnp.int32(0), int_min) for c in count(lambda g, k: k >= 0))

    def search(it, thr):
        bit = lax.shift_left(jnp.int32(1), jnp.int32(30) - it)
        cand = [t | bit for t in thr]
        cnt = count(lambda g, k: k >= cand[g])
        return tuple(jnp.where(c >= keep, cd, t) for c, cd, t in zip(cnt, cand, thr))

    thr = lax.fori_loop(0, 31, search, thr)
    need = [keep - c for c in count(lambda g, k: k > thr[g])]
    tri = tri_ref[...]

    def pass_c(j, run):
        causal = j * LANE + lane <= t_pos
        out = []
        for g in seqs:
            key = key_s[g, j]
            eq = key == thr[g]
            eqf = jnp.where(eq, 1.0, 0.0)
            cum = jnp.dot(eqf.astype(BF16), tri, preferred_element_type=F32) + run[g]
            sel = (key > thr[g]) | (eq & (cum <= need[g]))
            key_s[g, j] = jnp.where(sel & causal, 1, 0)
            out.append(run[g] + jnp.sum(eqf, axis=1, keepdims=True))
        return tuple(out)

    lax.fori_loop(0, nblk, pass_c, tuple(jnp.zeros((qb, 1), F32) for _ in seqs))

    qs = [[_stack_heads(q_ref.at[g], hk) for hk in range(N_KV_HEADS)] for g in seqs]

    def pass_d(j, carry):
        off = pl.multiple_of(j * LANE, LANE)
        dsel = jnp.minimum(dq - j, 2)
        biases = [bias_ref[dsel, hk * GQA:(hk + 1) * GQA].reshape(GQA * qb, LANE) for hk in range(N_KV_HEADS)]
        out = []
        for g in seqs:
            kb = kv_ref[g, pl.ds(off, LANE), 0:LANE]
            vb = kv_ref[g, pl.ds(off, LANE), LANE:2 * LANE]
            msk = _tile4(key_s[g, j] > 0)
            out.append(tuple(_flash_step(qs[g][hk], kb, vb, biases[hk], msk, *carry[g][hk])
                             for hk in range(N_KV_HEADS)))
        return tuple(out)

    init = tuple(tuple(_flash_init(GQA * qb) for _ in range(N_KV_HEADS)) for _ in seqs)
    res = lax.fori_loop(0, nblk, pass_d, init)
    for g in seqs:
        _write_heads(o_ref.at[g], [acc / l for (_, l, acc) in res[g]], qb)


def _write_heads(o_ref, outs, qb):
    lane = lax.broadcasted_iota(I32, (qb, LANE), 1)
    for hk in range(N_KV_HEADS):
        valid = (lane // HEAD_DIM) == hk
        for g in range(GQA):
            h = hk * GQA + g
            o_ref[:, h * LANE:(h + 1) * LANE] = jnp.where(valid, outs[hk][g * qb:(g + 1) * qb], 0.0).astype(BF16)


def _dsa(q, qi, misc, kidx_b, kv_b, bias, tri, *, qb, q_base, n_keep, grp):
    b, rows, _ = q.shape
    nq = rows // qb
    s = kv_b.shape[1]
    kern = functools.partial(_dsa_kernel, grp=grp, qb=qb, q_base=q_base, n_keep=n_keep, single=(nq == 1))
    qspec = lambda w: pl.BlockSpec((grp, qb, w), lambda bi, i: (bi, i, 0))
    kspec = lambda w: pl.BlockSpec((grp, s, w), lambda bi, i: (bi, 0, 0))
    return pl.pallas_call(
        kern,
        grid=(b // grp, nq),
        in_specs=[qspec(N_ATT_HEADS * LANE), qspec(IDX_HEADS * LANE), qspec(LANE), kspec(LANE), kspec(2 * LANE),
                  _const_spec(bias), _const_spec(tri)],
        out_specs=qspec(N_ATT_HEADS * LANE),
        out_shape=jax.ShapeDtypeStruct((b, rows, N_ATT_HEADS * LANE), BF16),
        scratch_shapes=[pltpu.VMEM((grp, s // LANE, qb, LANE), I32)],
        compiler_params=_cparams("parallel", "parallel"),
        name="dsa_attention",
    )(q, qi, misc, kidx_b, kv_b, bias, tri)


def _select_top(keys, n_keep, tri):
    keep = float(n_keep)
    n = keys[0].shape[1]

    def count(pred):
        return [jnp.sum(jnp.where(pred(g, k), 1.0, 0.0), axis=1, keepdims=True) for g, k in enumerate(keys)]

    int_min = jnp.int32(-2 ** 31)
    thr = tuple(jnp.where(c >= keep, jnp.int32(0), int_min) for c in count(lambda g, k: k >= 0))

    def search(it, thr):
        bit = lax.shift_left(jnp.int32(1), jnp.int32(30) - it)
        cand = [t | bit for t in thr]
        cnt = count(lambda g, k: k >= cand[g])
        return tuple(jnp.where(c >= keep, cd, t) for c, cd, t in zip(cnt, cand, thr))

    thr = lax.fori_loop(0, 31, search, thr)
    need = [keep - c for c in count(lambda g, k: k > thr[g])]
    sels = []
    for g, k in enumerate(keys):
        run = jnp.zeros((1, 1), F32)
        parts = []
        for t in range(n // LANE):
            kt = k[:, t * LANE:(t + 1) * LANE]
            eq = kt == thr[g]
            eqf = jnp.where(eq, 1.0, 0.0)
            cum = jnp.dot(eqf.astype(BF16), tri, preferred_element_type=F32) + run
            parts.append((kt > thr[g]) | (eq & (cum <= need[g])))
            run = run + jnp.sum(eqf, axis=1, keepdims=True)
        sels.append(jnp.concatenate(parts, axis=1))
    return sels


def _order_keys(score):
    score = jnp.where(score == 0.0, 0.0, score)
    bits = lax.bitcast_convert_type(score, I32)
    return jnp.where(bits < 0, bits ^ jnp.int32(0x7FFFFFFF), bits)


def _head_rows_out(o_ref, g, acc):
    rowh = lax.broadcasted_iota(I32, (N_ATT_HEADS, LANE), 0)
    laneh = lax.broadcasted_iota(I32, (N_ATT_HEADS, LANE), 1)
    o_ref[g] = jnp.where((laneh // HEAD_DIM) == (rowh // GQA), acc, 0.0).astype(BF16)


def _dsa_sample_kernel(pt_ref, q_ref, qi_ref, wi_ref, knew_ref, kvnew_ref, bias_ref, tri_ref, *refs,
                       grp, n_pages, n_keep):
    del pt_ref
    ki_refs, kv_refs, o_ref = refs[:grp * n_pages], refs[grp * n_pages:2 * grp * n_pages], refs[-1]
    lane1 = lax.broadcasted_iota(I32, (1, LANE), 1)
    idx_scale = (IDX_HEADS * IDX_DIM) ** -0.5
    keys = []
    for g in range(grp):
        qi = qi_ref[g]
        wi = wi_ref[g]
        tiles = []
        for p in range(n_pages):
            rel = jnp.maximum(_dot(qi, ki_refs[g * n_pages + p][...]), 0.0)
            tiles.append(jnp.sum(rel * wi, axis=0, keepdims=True) * idx_scale)
        rel_new = jnp.maximum(jnp.sum(qi.astype(F32) * knew_ref[g], axis=1, keepdims=True), 0.0)
        sc_new = jnp.sum(rel_new * wi, axis=0, keepdims=True) * idx_scale
        tiles.append(jnp.where(lane1 == 0, sc_new, NEG))
        keys.append(_order_keys(jnp.concatenate(tiles, axis=1)))
    sels = _select_top(keys, n_keep, tri_ref[...])
    bias = bias_ref[...]
    for g in range(grp):
        q = q_ref[g]
        kvnew = kvnew_ref[g]
        tiles = [_dot(q, kv_refs[g * n_pages + p][0:LANE, :]) for p in range(n_pages)]
        s_new = jnp.sum(q.astype(F32) * kvnew[:, 0:LANE], axis=1, keepdims=True)
        tiles.append(jnp.where(lane1 == 0, s_new, 0.0))
        valid = sels[g] & (jnp.concatenate([lane1] * n_pages + [lane1 + LANE], axis=1) <= LANE)
        s = jnp.where(valid, jnp.concatenate(tiles, axis=1) * ATT_SCALE + bias, NEG)
        e = jnp.exp(s - jnp.max(s, axis=1, keepdims=True))
        p_all = jnp.where(valid, e / jnp.sum(e, axis=1, keepdims=True), 0.0)
        acc = p_all[:, n_pages * LANE:n_pages * LANE + 1] * kvnew[:, LANE:2 * LANE]
        for p in range(n_pages):
            acc = acc + _dot_nt(p_all[:, p * LANE:(p + 1) * LANE], kv_refs[g * n_pages + p][LANE:2 * LANE, :])
        _head_rows_out(o_ref, g, acc)


def _page_specs(pool_t, n_pages, grp):
    r, c = pool_t.shape[1:]
    return [pl.BlockSpec((None, r, c), lambda i, pt, g=g, p=p: (pt[i * grp + g, p], 0, 0))
            for g in range(grp) for p in range(n_pages)]


def _dsa_sample(q8, qi8, wi8, knew, kvnew, bias, tri, ki_t, kv_t, page_table, *, n_keep, grp):
    b, n_pages = page_table.shape
    gspec = lambda a: pl.BlockSpec((grp,) + a.shape[1:], lambda i, pt: (i,) + (0,) * (a.ndim - 1))
    cspec = lambda a: pl.BlockSpec(a.shape, lambda i, pt: (0,) * a.ndim)
    kern = functools.partial(_dsa_sample_kernel, grp=grp, n_pages=n_pages, n_keep=n_keep)
    return pl.pallas_call(
        kern,
        grid_spec=pltpu.PrefetchScalarGridSpec(
            num_scalar_prefetch=1, grid=(b // grp,),
            in_specs=[gspec(q8), gspec(qi8), gspec(wi8), gspec(knew), gspec(kvnew), cspec(bias), cspec(tri)]
            + _page_specs(ki_t, n_pages, grp) + _page_specs(kv_t, n_pages, grp),
            out_specs=pl.BlockSpec((grp, N_ATT_HEADS, LANE), lambda i, pt: (i, 0, 0))),
        out_shape=jax.ShapeDtypeStruct((b, N_ATT_HEADS, LANE), BF16),
        compiler_params=_cparams("parallel"),
        name="dsa_sample",
    )(page_table, q8, qi8, wi8, knew, kvnew, bias, tri, *([ki_t] * (grp * n_pages)), *([kv_t] * (grp * n_pages)))


def _gather_kernel(pt_ref, *refs, n_pages, rpp, w_in, w_out):
    del pt_ref
    new_ref, o_ref = refs[n_pages], refs[n_pages + 1]
    for p in range(n_pages):
        v = refs[p][...]
        if w_in < w_out:
            v = jnp.concatenate([v, jnp.zeros((rpp, w_out - w_in), F32)], axis=1)
        o_ref[p * rpp:(p + 1) * rpp, :] = v.astype(o_ref.dtype)
    row = lax.broadcasted_iota(I32, (rpp, w_out), 0)
    tail = jnp.where(row == 0, jnp.broadcast_to(new_ref[...], (rpp, w_out)), 0.0)
    o_ref[n_pages * rpp:(n_pages + 1) * rpp, :] = tail.astype(o_ref.dtype)


def _gather_pages(pool, page_table, new_rows, w_out, out_dtype):
    b, n_pages = page_table.shape
    rpp, w_in = pool.shape[1], pool.shape[2]
    kern = functools.partial(_gather_kernel, n_pages=n_pages, rpp=rpp, w_in=w_in, w_out=w_out)
    in_specs = [pl.BlockSpec((None, rpp, w_in), lambda bi, pt, p=p: (pt[bi, p], 0, 0)) for p in range(n_pages)]
    in_specs.append(pl.BlockSpec((None, 1, w_out), lambda bi, pt: (bi, 0, 0)))
    s = (n_pages + 1) * rpp
    return pl.pallas_call(
        kern,
        grid_spec=pltpu.PrefetchScalarGridSpec(
            num_scalar_prefetch=1, grid=(b,), in_specs=in_specs,
            out_specs=pl.BlockSpec((None, s, w_out), lambda bi, pt: (bi, 0, 0))),
        out_shape=jax.ShapeDtypeStruct((b, s, w_out), out_dtype),
        compiler_params=_cparams("parallel"),
        name="gather_pages",
    )(page_table, *([pool] * n_pages), new_rows)


def _route(hf, wrt, br):
    logits = lax.dot_general(wrt, hf, (((1,), (1,)), ((), ())), precision=lax.Precision.HIGHEST,
                             preferred_element_type=F32)
    s = _sigmoid(logits)
    sel = s + br
    rows = [sel[e:e + 1, :] for e in range(N_EXPERTS)]
    grp = []
    for g in range(N_GROUPS):
        a = rows[g * EXPERTS_PER_GROUP:(g + 1) * EXPERTS_PER_GROUP]
        best = None
        for i in range(EXPERTS_PER_GROUP):
            for j in range(i + 1, EXPERTS_PER_GROUP):
                v = a[i] + a[j]
                best = v if best is None else jnp.maximum(best, v)
        grp.append(best)
    gbest = jnp.zeros_like(grp[0], dtype=I32)
    cur = grp[0]
    for g in range(1, N_GROUPS):
        better = grp[g] > cur
        gbest = jnp.where(better, g, gbest)
        cur = jnp.where(better, grp[g], cur)
    picked = []
    for g in range(N_GROUPS):
        a = rows[g * EXPERTS_PER_GROUP:(g + 1) * EXPERTS_PER_GROUP]
        for j in range(EXPERTS_PER_GROUP):
            rank = jnp.zeros_like(a[j])
            for jj in range(EXPERTS_PER_GROUP):
                if jj != j:
                    ahead = (a[jj] > a[j]) | (a[jj] == a[j]) if jj < j else (a[jj] > a[j])
                    rank = rank + jnp.where(ahead, 1.0, 0.0)
            e = g * EXPERTS_PER_GROUP + j
            picked.append(jnp.where((gbest == g) & (rank < 2.0), s[e:e + 1, :], 0.0))
    den = picked[0]
    for p in picked[1:]:
        den = den + p
    return jnp.concatenate([p / den for p in picked], axis=0)


def _post_tail(x, mix, gate, gf, shf, scf, wrt_ref, br_ref, x2_ref, hf_ref, cwt_ref):
    x2 = x + gate * mix
    x2_ref[...] = x2
    hf = _modulate(x2, gf, shf, scf)
    hf_ref[...] = hf.astype(BF16)
    cwt_ref[...] = _route(hf, wrt_ref[...], br_ref[...])


def _post_even_kernel(x_ref, oa_ref, bg_ref, u_ref, um1_ref, um2_ref, gate_ref, shf_ref, scf_ref,
                      gf_ref, cw_ref, cb_ref, wo_ref, wrt_ref, br_ref, x2_ref, hf_ref, cwt_ref):
    cw = cw_ref[...]
    y = cb_ref[...] + cw[0:1, :] * um2_ref[...]
    y = y + cw[1:2, :] * um1_ref[...]
    y = y + cw[2:3, :] * u_ref[...]
    n_a = N_ATT_HEADS * LANE
    mix = (jnp.dot(oa_ref[...], wo_ref[0:n_a, :], preferred_element_type=F32)
           + jnp.dot((bg_ref[...] * y).astype(BF16), wo_ref[n_a:n_a + CONV_CH, :], preferred_element_type=F32))
    _post_tail(x_ref[...], mix, gate_ref[...], gf_ref[...], shf_ref[...], scf_ref[...], wrt_ref, br_ref,
               x2_ref, hf_ref, cwt_ref)


POST_OUTS = [(1024, F32), (1024, BF16)]


def _moe_kernel(hf_ref, cw_ref, x2_ref, gate_ref, wg_ref, wu_ref, wd_ref, o_ref, acc_ref):
    e = pl.program_id(1)

    @pl.when(e == 0)
    def _():
        acc_ref[...] = jnp.zeros_like(acc_ref)

    hf = hf_ref[...]
    hmid = _silu(jnp.dot(hf, wg_ref[...], preferred_element_type=F32)) * jnp.dot(hf, wu_ref[...],
                                                                                 preferred_element_type=F32)
    cw = cw_ref[...]
    lane = lax.broadcasted_iota(I32, cw.shape, 1)
    wcol = jnp.sum(jnp.where(lane == e, cw, 0.0), axis=1, keepdims=True)
    acc_ref[...] += jnp.dot((hmid * wcol).astype(BF16), wd_ref[...], preferred_element_type=F32)

    @pl.when(e == N_EXPERTS - 1)
    def _():
        o_ref[...] = x2_ref[...] + gate_ref[...] * acc_ref[...]


def _moe(hf, cw, x2, gate, wg, wu, wd, tile, tpb):
    rows, d = x2.shape
    de = wg.shape[2]
    return pl.pallas_call(
        _moe_kernel,
        grid=(rows // tile, N_EXPERTS),
        in_specs=[pl.BlockSpec((tile, d), lambda t, e: (t, 0)),
                  pl.BlockSpec((tile, N_EXPERTS), lambda t, e: (t, 0)),
                  pl.BlockSpec((tile, d), lambda t, e: (t, 0)),
                  pl.BlockSpec((None,) + gate.shape[1:], lambda t, e: (t // tpb, 0, 0)),
                  pl.BlockSpec((None, d, de), lambda t, e: (e, 0, 0)),
                  pl.BlockSpec((None, d, de), lambda t, e: (e, 0, 0)),
                  pl.BlockSpec((None, de, d), lambda t, e: (e, 0, 0))],
        out_specs=pl.BlockSpec((tile, d), lambda t, e: (t, 0)),
        out_shape=jax.ShapeDtypeStruct((rows, d), F32),
        scratch_shapes=[pltpu.VMEM((tile, d), F32)],
        compiler_params=_cparams("parallel", "arbitrary"),
        name="moe_dense",
    )(hf, cw, x2, gate, wg, wu, wd)


def _shift_rows(u3, k):
    return jnp.pad(u3, ((0, 0), (k, 0), (0, 0)))[:, :u3.shape[1]]


def _even_layer(xp, xs, mp, ms, page_table, kv_pool, kidx_pool, conv_buf, rel_bias, router, experts,
                g_mix, g_ffn, w_in, w_out, q_norm, k_norm, conv_w, conv_b):
    B, S, D = xp.shape
    Bs = xs.shape[0]
    past = page_table.shape[1] * PAGE_SIZE
    wts = _even_weights(w_in, w_out, q_norm, k_norm)
    gm = _gmats()
    wrt, br = router
    wg, wu, wd = experts
    g_mix = g_mix.reshape(1, D)
    g_ffn = g_ffn.reshape(1, D)
    post_consts = [g_ffn, conv_w, conv_b.reshape(1, CONV_CH), wts[1], wrt, br]

    tp = ROW_TILE
    q_p, kv_f, kv_b, qi_p, misc, misc_b, bg, u = _inproj_even(xp.reshape(B * S, D), [mp[0], mp[1]], g_mix, wts, gm,
                                                              tp, S // tp)
    r3 = lambda a: a.reshape(B, S, a.shape[-1])
    oa = _dsa(r3(q_p), r3(qi_p), r3(misc), r3(misc_b), r3(kv_b), _bias_tiles(rel_bias, LANE), gm[2],
              qb=LANE, q_base=0, n_keep=min(TOPK_MAX, S // 4), grp=1)
    u3 = r3(u)
    um1 = _shift_rows(u3, 1).reshape(B * S, CONV_CH)
    um2 = _shift_rows(u3, 2).reshape(B * S, CONV_CH)
    x2, hf, cwt = _row_call(_post_even_kernel, [xp.reshape(B * S, D), oa.reshape(B * S, -1), bg, u, um1, um2],
                            [mp[2], mp[3], mp[4]], post_consts, POST_OUTS, tp, S // tp, "post_even",
                            outs_t=[(N_EXPERTS, F32)])
    tm = MOE_TILE
    xp3 = _moe(hf, cwt.T, x2, mp[5], wg, wu, wd, tm, S // tm).reshape(B, S, D)
    outs_p = (kv_f.reshape(B, S, 2, N_KV_HEADS, HEAD_DIM), misc[:, :IDX_DIM].reshape(B, S, IDX_DIM), u3[:, S - 2:])

    q_s, kv_fs, _, qi_s, misc_s, _, bg_s, u_s = _inproj_even(xs, [ms[0], ms[1]], g_mix, wts, gm, Bs, 1)
    n_pool = kv_pool.shape[0]
    kv_t = kv_pool.transpose(0, 2, 3, 4, 1).reshape(n_pool, 2 * LANE, PAGE_SIZE)
    ki_t = kidx_pool.transpose(0, 2, 1)
    qi8 = jnp.pad(qi_s.reshape(Bs, IDX_HEADS, LANE)[:, :, :IDX_DIM], ((0, 0), (0, N_ATT_HEADS - IDX_HEADS), (0, 0)))
    wi8 = jnp.pad(misc_s[:, IDX_DIM:IDX_DIM + IDX_HEADS], ((0, 0), (0, N_ATT_HEADS - IDX_HEADS)))[:, :, None]
    oa_s = _dsa_sample(q_s.reshape(Bs, N_ATT_HEADS, LANE), qi8, wi8, misc_s[:, None, :IDX_DIM], kv_fs[:, None, :],
                       _bias_row(rel_bias, past), gm[2], ki_t, kv_t, page_table,
                       n_keep=min(TOPK_MAX, (past + 1) // 4), grp=_pick_group(Bs, SAMPLE_GROUP)).reshape(Bs, -1)
    x2s, hfs, cwts = _row_call(_post_even_kernel, [xs, oa_s, bg_s, u_s, conv_buf[:, 1], conv_buf[:, 0]],
                               [ms[2], ms[3], ms[4]], post_consts, POST_OUTS, Bs, 1, "post_even_s",
                               outs_t=[(N_EXPERTS, F32)])
    xs3 = _moe(hfs, cwts.T, x2s, ms[5], wg, wu, wd, Bs, 1)
    outs_s = (kv_fs.reshape(Bs, 1, 2, N_KV_HEADS, HEAD_DIM), misc_s[:, None, :IDX_DIM],
              jnp.concatenate([conv_buf[:, 1:], u_s[:, None, :]], axis=1))
    return xp3, xs3, outs_p, outs_s


O_Z0, O_Q0, O_KVC0, O_KVS0, O_KVW0, O_G0, O_END = 0, 1792, 2816, 3072, 3328, 3584, 3712
P_C = 3 * RWKV_DIM + LORA_W + LORA_A + LORA_G


def _inproj_odd_kernel(x_ref, shift_ref, scale_ref, g_ref, w_ref, qg_ref, ksg_ref, kwg_ref, gone_ref, gtwo_ref,
                       zc_ref, q_ref, kvc_ref, kvs_ref, kvsb_ref, kvw_ref, kvwb_ref, gates_ref):
    h = _modulate(x_ref[...], g_ref[...], shift_ref[...], scale_ref[...])
    z = jnp.dot(h.astype(BF16), w_ref[...], preferred_element_type=F32)
    zc_ref[...] = z[:, O_Z0:O_Q0]
    gone = gone_ref[...]
    gtwo = gtwo_ref[...]
    for t in range(N_ATT_HEADS):
        sl = slice(t * LANE, (t + 1) * LANE)
        q_ref[:, sl] = _group_rms(z[:, O_Q0 + t * LANE:O_Q0 + (t + 1) * LANE], gone, qg_ref[:, sl]).astype(BF16)
    kvc_ref[...] = z[:, O_KVC0:O_KVS0]
    for base, gain_ref, f_ref, b_ref in ((O_KVS0, ksg_ref, kvs_ref, kvsb_ref), (O_KVW0, kwg_ref, kvw_ref, kvwb_ref)):
        k = _group_rms(z[:, base:base + LANE], gtwo, gain_ref[...])
        v = z[:, base + LANE:base + 2 * LANE]
        f_ref[:, 0:LANE] = k
        f_ref[:, LANE:2 * LANE] = v
        b_ref[:, 0:LANE] = k.astype(BF16)
        b_ref[:, LANE:2 * LANE] = v.astype(BF16)
    gates_ref[...] = _sigmoid(z[:, O_G0:O_END])


def _odd_weights(w_in, w_out, q_norm, k_norm):
    d = w_in.shape[0]
    a_q, a_kv = N_ATT_HEADS * HEAD_DIM, 2 * N_KV_HEADS * HEAD_DIM
    o = P_C
    wz = w_in[:, :o]
    wq = w_in[:, o:o + a_q]; o += a_q
    wkv = w_in[:, o:o + 3 * a_kv]; o += 3 * a_kv
    wg = w_in[:, o:]
    wg = jnp.concatenate([wg, jnp.zeros((d, LANE - wg.shape[1]), F32)], -1)
    w_in_p = jnp.concatenate([wz, _pad_q_cols(wq), wkv, wg], -1).astype(BF16)
    w_out_p = jnp.concatenate([w_out[:RWKV_DIM], _pad_o_rows(w_out[RWKV_DIM:])], 0).astype(BF16)
    qg = jnp.tile(q_norm, 2 * N_ATT_HEADS).reshape(1, N_ATT_HEADS * LANE)
    ksg = jnp.tile(k_norm[1], 2).reshape(1, LANE)
    kwg = jnp.tile(k_norm[2], 2).reshape(1, LANE)
    return w_in_p, w_out_p, qg, ksg, kwg


def _inproj_odd(x, mods, g, wts, gm, tile, tpb):
    w_in_p, _, qg, ksg, kwg = wts
    gone, gtwo, _ = gm
    outs = [(P_C, F32), (N_ATT_HEADS * LANE, BF16), (2 * LANE, F32), (2 * LANE, F32), (2 * LANE, BF16),
            (2 * LANE, F32), (2 * LANE, BF16), (LANE, F32)]
    return _row_call(_inproj_odd_kernel, x, mods, [g, w_in_p, qg, ksg, kwg, gone, gtwo], outs, tile, tpb,
                     "inproj_odd")


def _rwkv_pre_kernel(z_ref, zp_ref, mu_ref, w0_ref, a0_ref, kk_ref, ka_ref, wup_ref, aup_ref, gup_ref, gsum_ref,
                     r_o, w_o, k_o, v_o, kk_o, kka_o, g_o):
    z = z_ref[...]
    zm = z + (zp_ref[...] - z) * mu_ref[...]
    r = zm[:, 0:RWKV_DIM]
    k = zm[:, RWKV_DIM:2 * RWKV_DIM]
    v = zm[:, 2 * RWKV_DIM:3 * RWKV_DIM]
    t12 = zm[:, 3 * RWKV_DIM:3 * RWKV_DIM + LANE]
    gd = zm[:, 3 * RWKV_DIM + LANE:P_C]
    xw = w0_ref[...] + _dot(jnp.tanh(t12), wup_ref[...])
    sp = jnp.maximum(-xw, 0.0) + jnp.log(1.0 + jnp.exp(-jnp.abs(xw)))
    w_o[...] = jnp.exp(-jnp.exp(-sp - 0.5))
    a = _sigmoid(a0_ref[...] + _dot(t12, aup_ref[...]))
    g_o[...] = _dot(_sigmoid(gd), gup_ref[...])
    kk = k * kk_ref[...]
    gsum = gsum_ref[...]
    for t in range(RWKV_DIM // LANE):
        sl = slice(t * LANE, (t + 1) * LANE)
        kt = kk[:, sl]
        nrm = jnp.maximum(jnp.sqrt(_dot_split(kt * kt, gsum)), 1e-12)
        kn = kt / nrm
        kk_o[:, sl] = kn
        kka_o[:, sl] = kn * a[:, sl]
    r_o[...] = r
    v_o[...] = v
    k_o[...] = k * (1.0 + (a - 1.0) * ka_ref[...])


def _rwkv_pre(zc, zprev, cpar, gsum, tile):
    mu, w0, w_up, a0, a_up, g_up, k_k, k_a = cpar
    z64 = jnp.zeros((LORA_W, RWKV_DIM), F32)
    consts = [mu.reshape(1, P_C), w0.reshape(1, -1), a0.reshape(1, -1), k_k.reshape(1, -1), k_a.reshape(1, -1),
              jnp.concatenate([w_up, z64], 0).astype(BF16), jnp.concatenate([z64, a_up], 0).astype(BF16),
              g_up.astype(BF16), gsum]
    return _row_call(_rwkv_pre_kernel, [zc, zprev], [], consts, [(RWKV_DIM, F32)] * 7, tile, 1, "rwkv_pre")


SCAN_P = 64
SCAN_VH = HEAD_DIM // 2


def _scan_kernel(kk_ref, w_ref, kka_ref, k_ref, r_ref, v_ref, s0_ref, y_ref, so_ref, st, *, tc):
    ti = pl.program_id(1)

    @pl.when(ti == 0)
    def _():
        st[...] = s0_ref[...]

    def step(t, c):
        kk, w, kka, kt, rt, vt = kk_ref[t], w_ref[t], kka_ref[t], k_ref[t], r_ref[t], v_ref[t]
        ys = []
        for vi in range(SCAN_VH):
            s = st[vi]
            sa = -jnp.sum(s * kk, axis=0, keepdims=True)
            sn = s * w + sa * kka + vt[vi:vi + 1, :] * kt
            st[vi] = sn
            ys.append(jnp.sum(sn * rt, axis=0, keepdims=True))
        y_ref[t] = jnp.concatenate(ys, axis=0)
        return c

    lax.fori_loop(0, tc, step, 0)

    @pl.when(ti == pl.num_programs(1) - 1)
    def _():
        so_ref[...] = st[...]


def _scan_layout_k(x, b, t):
    p = b * RWKV_HEADS
    a = x.reshape(b, t, RWKV_HEADS, HEAD_DIM).transpose(1, 3, 0, 2).reshape(t, HEAD_DIM, p)
    nc = -(-p // SCAN_P)
    a = jnp.pad(a, ((0, 0), (0, 0), (0, nc * SCAN_P - p))).reshape(t, HEAD_DIM, nc, SCAN_P).transpose(2, 0, 1, 3)
    return jnp.concatenate([a, a], -1)


def _scan_layout_v(x, b, t):
    p = b * RWKV_HEADS
    a = x.reshape(b, t, RWKV_HEADS, HEAD_DIM).transpose(1, 3, 0, 2).reshape(t, HEAD_DIM, p)
    nc = -(-p // SCAN_P)
    a = jnp.pad(a, ((0, 0), (0, 0), (0, nc * SCAN_P - p))).reshape(t, HEAD_DIM, nc, SCAN_P).transpose(2, 0, 1, 3)
    return jnp.concatenate([a[:, :, :SCAN_VH], a[:, :, SCAN_VH:]], -1)


def _scan_unlayout_y(y, b, t):
    p = b * RWKV_HEADS
    nc = y.shape[0]
    a = jnp.concatenate([y[..., :SCAN_P], y[..., SCAN_P:]], axis=2)
    a = a.transpose(1, 2, 0, 3).reshape(t, HEAD_DIM, nc * SCAN_P)[:, :, :p]
    return a.reshape(t, HEAD_DIM, b, RWKV_HEADS).transpose(2, 0, 3, 1).reshape(b * t, RWKV_DIM)


def _scan_layout_state(s):
    b = s.shape[0]
    p = b * RWKV_HEADS
    nc = -(-p // SCAN_P)
    a = jnp.pad(s.reshape(p, HEAD_DIM, HEAD_DIM), ((0, nc * SCAN_P - p), (0, 0), (0, 0)))
    a = a.reshape(nc, SCAN_P, HEAD_DIM, HEAD_DIM).transpose(0, 2, 3, 1)
    return jnp.concatenate([a[:, :SCAN_VH], a[:, SCAN_VH:]], -1)


def _scan_unlayout_state(st, b):
    p = b * RWKV_HEADS
    nc = st.shape[0]
    a = jnp.concatenate([st[..., :SCAN_P], st[..., SCAN_P:]], axis=1)
    a = a.transpose(0, 3, 1, 2).reshape(nc * SCAN_P, HEAD_DIM, HEAD_DIM)[:p]
    return a.reshape(b, RWKV_HEADS, HEAD_DIM, HEAD_DIM)


def _rwkv_scan(pre, s0, b, t, tc):
    r, w, k, v, kk, kka, _ = pre
    ks = [_scan_layout_k(a, b, t) for a in (kk, w, kka, k, r)]
    vs = _scan_layout_v(v, b, t)
    s0l = _scan_layout_state(s0)
    nc = s0l.shape[0]
    kspec = pl.BlockSpec((None, tc, HEAD_DIM, LANE), lambda c, i: (c, i, 0, 0))
    vspec = pl.BlockSpec((None, tc, SCAN_VH, LANE), lambda c, i: (c, i, 0, 0))
    sspec = pl.BlockSpec((None, SCAN_VH, HEAD_DIM, LANE), lambda c, i: (c, 0, 0, 0))
    y, so = pl.pallas_call(
        functools.partial(_scan_kernel, tc=tc),
        grid=(nc, t // tc),
        in_specs=[kspec] * 5 + [vspec, sspec],
        out_specs=[vspec, sspec],
        out_shape=[jax.ShapeDtypeStruct((nc, t, SCAN_VH, LANE), F32),
                   jax.ShapeDtypeStruct((nc, SCAN_VH, HEAD_DIM, LANE), F32)],
        scratch_shapes=[pltpu.VMEM((SCAN_VH, HEAD_DIM, LANE), F32)],
        compiler_params=_cparams("parallel", "arbitrary"),
        name="rwkv_scan",
    )(*ks, vs, s0l)
    return _scan_unlayout_y(y, b, t), _scan_unlayout_state(so, b)


def _compress_kernel(x_ref, pe_ref, w_ref, kg_ref, gtwo_ref, o_ref):
    z = jnp.dot((x_ref[...] + pe_ref[...]).astype(BF16), w_ref[...], preferred_element_type=F32)
    o_ref[:, 0:LANE] = _group_rms(z[:, 0:LANE], gtwo_ref[...], kg_ref[...])
    o_ref[:, LANE:2 * LANE] = z[:, LANE:2 * LANE]


def _compress_weights(cmp_pe, cmp_w, k_norm_c):
    wk = cmp_w[0].reshape(CMP_BLOCK, HEAD_DIM, HEAD_DIM)
    wv = cmp_w[1].reshape(CMP_BLOCK, HEAD_DIM, HEAD_DIM)
    full = jnp.zeros((CMP_BLOCK, 4, HEAD_DIM, 4, HEAD_DIM), F32)
    for s, w in enumerate((wk, wk, wv, wv)):
        full = full.at[:, s, :, s, :].set(w)
    pe = jnp.stack([cmp_pe[0], cmp_pe[0], cmp_pe[1], cmp_pe[1]], axis=1)
    return (full.reshape(CMP_BLOCK * 4 * HEAD_DIM, 4 * HEAD_DIM).astype(BF16), pe.reshape(1, -1),
            jnp.tile(k_norm_c, 2).reshape(1, LANE))


def _compress(rows, cw, gtwo, tile):
    wfull, pe, kg = cw
    return _row_call(_compress_kernel, rows, [], [pe, wfull, kg, gtwo], [(2 * LANE, F32)], tile, 1, "nsa_compress")[0]


def _nsa_kernel(q_ref, gates_ref, kcv_ref, kvs_ref, kvw_ref, bias_ref, biasc_ref, pair_ref, o_ref,
                *, grp, qb, q_base, n_cmp, n_sel, w_off, single):
    q0 = q_base if single else q_base + pl.program_id(1) * qb
    nblk = (q0 + qb - 1) // LANE + 1
    dq = q0 // LANE
    seqs = range(grp)
    heads = range(N_KV_HEADS)
    row = lax.broadcasted_iota(I32, (qb, LANE), 0)
    lane = lax.broadcasted_iota(I32, (qb, LANE), 1)
    t_pos = q0 + row
    qs = [[_stack_heads(q_ref.at[g], hk) for hk in heads] for g in seqs]

    def gate_col(g, br, hk):
        gates = gates_ref[g]
        return jnp.concatenate([gates[:, br * N_ATT_HEADS + hk * GQA + a:br * N_ATT_HEADS + hk * GQA + a + 1]
                                for a in range(GQA)], axis=0)

    mask_c4 = _tile4(((lane * CMP_BLOCK + CMP_BLOCK - 1) <= t_pos) & (lane < n_cmp))
    cur = t_pos // SEL_BLOCK
    forced = (lane == 0) | (lane == cur) | (lane == cur - 1)
    sel_causal = lane * SEL_BLOCK <= t_pos
    lane_f = lane.astype(F32)
    o_cmp = [[None] * N_KV_HEADS for _ in seqs]
    selm = [[None] * N_KV_HEADS for _ in seqs]
    for g in seqs:
        kc = kcv_ref[g, :, 0:LANE]
        vc = kcv_ref[g, :, LANE:2 * LANE]
        for hk in heads:
            s = _dot_nt(qs[g][hk], kc) * ATT_SCALE + biasc_ref[hk * GQA:(hk + 1) * GQA].reshape(GQA * qb, LANE)
            s = jnp.where(mask_c4, s, NEG)
            e = jnp.exp(s - jnp.max(s, axis=1, keepdims=True))
            p = jnp.where(mask_c4, e / jnp.sum(e, axis=1, keepdims=True), 0.0)
            o_cmp[g][hk] = jnp.dot(p.astype(BF16), vc, preferred_element_type=F32)
            ps = p[0:qb]
            for a in range(1, GQA):
                ps = ps + p[a * qb:(a + 1) * qb]
            score = _dot_split(ps, pair_ref[...])
            score = jnp.where(sel_causal, jnp.where(forced, FORCE, score), NEG)
            picked = jnp.zeros((qb, LANE), F32)
            for _ in range(n_sel):
                mx = jnp.max(score, axis=1, keepdims=True)
                first = jnp.min(jnp.where(score == mx, lane_f, float(LANE)), axis=1, keepdims=True)
                hit = lane_f == first
                picked = jnp.where(hit, 1.0, picked)
                score = jnp.where(hit, TAKEN, score)
            selm[g][hk] = picked.astype(BF16)

    def key_blocks(ref, g, jb):
        off = pl.multiple_of(jb * LANE, LANE)
        return ref[g, pl.ds(off, LANE), 0:LANE], ref[g, pl.ds(off, LANE), LANE:2 * LANE]

    def biases_of(jb):
        dsel = jnp.minimum(dq - jb, 2)
        return [bias_ref[dsel, hk * GQA:(hk + 1) * GQA].reshape(GQA * qb, LANE) for hk in heads]

    init = tuple(tuple(_flash_init(GQA * qb) for _ in heads) for _ in seqs)

    erow = lax.broadcasted_iota(I32, (LANE, LANE), 0)
    ecol = lax.broadcasted_iota(I32, (LANE, LANE), 1)

    def slc_body(jb, carry):
        expand = jnp.where(erow == 2 * jb + ecol // SEL_BLOCK, 1.0, 0.0).astype(BF16)
        causal = jb * LANE + lane <= t_pos
        biases = biases_of(jb)
        out = []
        for g in seqs:
            kb, vb = key_blocks(kvs_ref, g, jb)
            res = []
            for hk in heads:
                tok = jnp.dot(selm[g][hk], expand, preferred_element_type=F32) > 0.5
                res.append(_flash_step(qs[g][hk], kb, vb, biases[hk], _tile4(tok & causal), *carry[g][hk]))
            out.append(tuple(res))
        return tuple(out)

    res_s = lax.fori_loop(0, nblk, slc_body, init)

    def win_body(jb, carry):
        dist = t_pos - (jb * LANE + lane)
        msk = _tile4((dist >= 0) & (dist < WINDOW))
        biases = biases_of(jb)
        out = []
        for g in seqs:
            kb, vb = key_blocks(kvw_ref, g, jb - w_off)
            out.append(tuple(_flash_step(qs[g][hk], kb, vb, biases[hk], msk, *carry[g][hk]) for hk in heads))
        return tuple(out)

    res_w = lax.fori_loop(jnp.maximum(dq - WINDOW // LANE, 0), dq + 1, win_body, init)

    for g in seqs:
        outs = []
        for hk in heads:
            o_s = res_s[g][hk][2] / res_s[g][hk][1]
            o_w = res_w[g][hk][2] / res_w[g][hk][1]
            outs.append(gate_col(g, 0, hk) * o_cmp[g][hk] + gate_col(g, 1, hk) * o_s + gate_col(g, 2, hk) * o_w)
        _write_heads(o_ref.at[g], outs, qb)


def _bias_cmp(rel_bias, q_starts, qb):
    q0 = jnp.asarray(q_starts, I32)[:, None, None]
    r = jnp.arange(qb)[None, :, None]
    n = jnp.arange(LANE)[None, None, :]
    return _bias_lookup(rel_bias, q0 + r - (n * CMP_BLOCK + CMP_BLOCK - 1)).transpose(0, 3, 1, 2)


def _nsa(q, gates, kcv, kvs_b, kvw_b, bias, bias_c, pair, *, qb, q_base, n_cmp, n_sel, w_off, grp):
    b, rows, _ = q.shape
    nq = rows // qb
    kern = functools.partial(_nsa_kernel, grp=grp, qb=qb, q_base=q_base, n_cmp=n_cmp, n_sel=n_sel, w_off=w_off,
                             single=(nq == 1))
    qspec = lambda w: pl.BlockSpec((grp, qb, w), lambda bi, i: (bi, i, 0))
    kspec = lambda a: pl.BlockSpec((grp,) + a.shape[1:], lambda bi, i: (bi, 0, 0))
    return pl.pallas_call(
        kern,
        grid=(b // grp, nq),
        in_specs=[qspec(N_ATT_HEADS * LANE), qspec(LANE), kspec(kcv), kspec(kvs_b), kspec(kvw_b), _const_spec(bias),
                  pl.BlockSpec((None,) + bias_c.shape[1:], lambda bi, i: (i, 0, 0, 0)), _const_spec(pair)],
        out_specs=qspec(N_ATT_HEADS * LANE),
        out_shape=jax.ShapeDtypeStruct((b, rows, N_ATT_HEADS * LANE), BF16),
        compiler_params=_cparams("parallel", "parallel"),
        name="nsa_attention",
    )(q, gates, kcv, kvs_b, kvw_b, bias, bias_c, pair)


def _nsa_sample_kernel(pt_ref, q_ref, gates_ref, kcv_ref, snew_ref, wnew_ref, win_ref, bias_ref, biasc_ref,
                       biasw_ref, pair_ref, *refs, grp, n_pages, t_pos, n_cmp, n_sel, w_eff):
    del pt_ref
    slc_refs, o_ref = refs[:grp * n_pages], refs[-1]
    lane1 = lax.broadcasted_iota(I32, (1, LANE), 1)
    lane8 = lax.broadcasted_iota(I32, (N_ATT_HEADS, LANE), 1)
    row_all = lax.broadcasted_iota(I32, (N_ATT_HEADS, (n_pages + 1) * LANE), 0)
    lanew = lax.broadcasted_iota(I32, (N_ATT_HEADS, w_eff), 1)
    lane1_f = lane1.astype(F32)
    bias = bias_ref[...]
    bias_now = bias[:, n_pages * LANE:n_pages * LANE + 1]
    mask_c = ((lane8 * CMP_BLOCK + CMP_BLOCK - 1) <= t_pos) & (lane8 < n_cmp)
    cur = t_pos // SEL_BLOCK
    forced = (lane1 == 0) | (lane1 == cur) | (lane1 == cur - 1)
    sel_causal = lane1 * SEL_BLOCK <= t_pos
    tail_valid = jnp.concatenate([lane1] * n_pages + [lane1 + LANE], axis=1) <= LANE
    for g in range(grp):
        q = q_ref[g]
        qf = q.astype(F32)
        s = jnp.where(mask_c, _dot_nt(q, kcv_ref[g, :, 0:LANE]) * ATT_SCALE + biasc_ref[...], NEG)
        e = jnp.exp(s - jnp.max(s, axis=1, keepdims=True))
        pc = jnp.where(mask_c, e / jnp.sum(e, axis=1, keepdims=True), 0.0)
        o_c = jnp.dot(pc.astype(BF16), kcv_ref[g, :, LANE:2 * LANE], preferred_element_type=F32)
        masks = []
        for hk in range(N_KV_HEADS):
            ps = jnp.sum(pc[hk * GQA:(hk + 1) * GQA], axis=0, keepdims=True)
            score = _dot_split(ps, pair_ref[...])
            score = jnp.where(sel_causal, jnp.where(forced, FORCE, score), NEG)
            picked = jnp.zeros((1, LANE), F32)
            for _ in range(n_sel):
                mx = jnp.max(score, axis=1, keepdims=True)
                first = jnp.min(jnp.where(score == mx, lane1_f, float(LANE)), axis=1, keepdims=True)
                hit = lane1_f == first
                picked = jnp.where(hit, 1.0, picked)
                score = jnp.where(hit, TAKEN, score)
            per_page = PAGE_SIZE // SEL_BLOCK
            tiles = []
            for p in range(n_pages + 1):
                t = jnp.zeros((1, LANE), F32)
                for a in range(per_page):
                    blk = picked[:, p * per_page + a:p * per_page + a + 1]
                    t = jnp.where(lane1 // SEL_BLOCK == a, blk, t)
                tiles.append(t)
            masks.append(jnp.concatenate(tiles, axis=1))
        valid = (jnp.where(row_all < GQA, masks[0], masks[1]) > 0.5) & tail_valid
        snew = snew_ref[g]
        tiles = [_dot(q, slc_refs[g * n_pages + p][0:LANE, :]) for p in range(n_pages)]
        tiles.append(jnp.where(lane1 == 0, jnp.sum(qf * snew[:, 0:LANE], axis=1, keepdims=True), 0.0))
        s = jnp.where(valid, jnp.concatenate(tiles, axis=1) * ATT_SCALE + bias, NEG)
        e = jnp.exp(s - jnp.max(s, axis=1, keepdims=True))
        p_all = jnp.where(valid, e / jnp.sum(e, axis=1, keepdims=True), 0.0)
        o_s = p_all[:, n_pages * LANE:n_pages * LANE + 1] * snew[:, LANE:2 * LANE]
        for p in range(n_pages):
            o_s = o_s + _dot_nt(p_all[:, p * LANE:(p + 1) * LANE], slc_refs[g * n_pages + p][LANE:2 * LANE, :])
        wnew = wnew_ref[g]
        valid_w = (w_eff - lanew) < WINDOW
        s_w = jnp.where(valid_w, _dot(q, win_ref[g, 0:LANE, :]) * ATT_SCALE + biasw_ref[...], NEG)
        s_n = jnp.sum(qf * wnew[:, 0:LANE], axis=1, keepdims=True) * ATT_SCALE + bias_now
        m = jnp.maximum(jnp.max(s_w, axis=1, keepdims=True), s_n)
        e_w = jnp.where(valid_w, jnp.exp(s_w - m), 0.0)
        e_n = jnp.exp(s_n - m)
        den = jnp.sum(e_w, axis=1, keepdims=True) + e_n
        o_w = _dot_nt(e_w / den, win_ref[g, LANE:2 * LANE, :]) + (e_n / den) * wnew[:, LANE:2 * LANE]
        gates = gates_ref[g]
        _head_rows_out(o_ref, g, gates[:, 0:1] * o_c + gates[:, 1:2] * o_s + gates[:, 2:3] * o_w)


def _nsa_sample(q8, gates8, kcv, snew, wnew, win_t, bias, bias_c, bias_w, pair, slc_t, page_table,
                *, t_pos, n_cmp, n_sel, grp):
    b, n_pages = page_table.shape
    w_eff = win_t.shape[2]
    gspec = lambda a: pl.BlockSpec((grp,) + a.shape[1:], lambda i, pt: (i,) + (0,) * (a.ndim - 1))
    cspec = lambda a: pl.BlockSpec(a.shape, lambda i, pt: (0,) * a.ndim)
    kern = functools.partial(_nsa_sample_kernel, grp=grp, n_pages=n_pages, t_pos=t_pos, n_cmp=n_cmp, n_sel=n_sel,
                             w_eff=w_eff)
    return pl.pallas_call(
        kern,
        grid_spec=pltpu.PrefetchScalarGridSpec(
            num_scalar_prefetch=1, grid=(b // grp,),
            in_specs=[gspec(q8), gspec(gates8), gspec(kcv), gspec(snew), gspec(wnew), gspec(win_t), cspec(bias),
                      cspec(bias_c), cspec(bias_w), cspec(pair)] + _page_specs(slc_t, n_pages, grp),
            out_specs=pl.BlockSpec((grp, N_ATT_HEADS, LANE), lambda i, pt: (i, 0, 0))),
        out_shape=jax.ShapeDtypeStruct((b, N_ATT_HEADS, LANE), BF16),
        compiler_params=_cparams("parallel"),
        name="nsa_sample",
    )(page_table, q8, gates8, kcv, snew, wnew, win_t, bias, bias_c, bias_w, pair, *([slc_t] * (grp * n_pages)))


def _post_odd_kernel(x_ref, y_ref, r_ref, k_ref, v_ref, g_ref, od_ref, gate_ref, shf_ref, scf_ref,
                     gf_ref, lnw_ref, lnb_ref, rk_ref, gtwo_ref, wo_ref, wrt_ref, br_ref, x2_ref, hf_ref, cwt_ref):
    gtwo = gtwo_ref[...]
    mix = jnp.dot(od_ref[...], wo_ref[RWKV_DIM:RWKV_DIM + N_ATT_HEADS * LANE, :], preferred_element_type=F32)
    for t in range(RWKV_DIM // LANE):
        sl = slice(t * LANE, (t + 1) * LANE)
        y = y_ref[:, sl]
        dlt = y - _dot_split(y, gtwo)
        yn = (dlt * lax.rsqrt(_dot_split(dlt * dlt, gtwo) + GN_EPS)) * lnw_ref[:, sl] + lnb_ref[:, sl]
        dot_rk = _dot_split(r_ref[:, sl] * k_ref[:, sl] * rk_ref[:, sl], gtwo) * float(HEAD_DIM)
        oc = (yn + dot_rk * v_ref[:, sl]) * g_ref[:, sl]
        mix = mix + jnp.dot(oc.astype(BF16), wo_ref[sl, :], preferred_element_type=F32)
    _post_tail(x_ref[...], mix, gate_ref[...], gf_ref[...], shf_ref[...], scf_ref[...], wrt_ref, br_ref,
               x2_ref, hf_ref, cwt_ref)


def _odd_layer(xp, xs, mp, ms, page_table, wkv0, shift0, cmp_pool, slc_pool, win_buf, rel_bias, router, experts,
               g_mix, g_ffn, w_in, w_out, cpar, r_k, ln_w, ln_b, q_norm, k_norm, cmp_pe, cmp_w):
    B, S, D = xp.shape
    Bs = xs.shape[0]
    n_pages = page_table.shape[1]
    past = n_pages * PAGE_SIZE
    wts = _odd_weights(w_in, w_out, q_norm, k_norm)
    gm = _gmats()
    gone, gtwo, _ = gm
    gsum = (gtwo.astype(F32) * HEAD_DIM).astype(BF16)
    i = jnp.arange(LANE)
    pair = jnp.where(i[:, None] // 2 == i[None, :], 1.0, 0.0).astype(BF16)
    cw = _compress_weights(cmp_pe, cmp_w, k_norm[0])
    wrt, br = router
    wg, wu, wd = experts
    g_mix = g_mix.reshape(1, D)
    post_consts = [g_ffn.reshape(1, D), ln_w.reshape(1, -1), ln_b.reshape(1, -1), r_k.reshape(1, -1), gtwo, wts[1],
                   wrt, br]
    w_eff = win_buf.shape[1]

    tp = ROW_TILE
    zc, q_p, kvc, kvs, kvs_b, kvw, kvw_b, gates = _inproj_odd(xp.reshape(B * S, D), [mp[0], mp[1]], g_mix, wts, gm,
                                                               tp, S // tp)
    r3 = lambda a: a.reshape(B, S, a.shape[-1])
    pre = _rwkv_pre(zc, _shift_rows(r3(zc), 1).reshape(B * S, P_C), cpar, gsum, tp)
    y, wkv_p = _rwkv_scan(pre, jnp.zeros((B, RWKV_HEADS, HEAD_DIM, HEAD_DIM), F32), B, S, SCAN_TIME_CHUNK)
    n_cmp = S // CMP_BLOCK
    kcv = _compress(kvc.reshape(B * n_cmp, CMP_BLOCK * 2 * LANE), cw, gtwo, _pick_tile(B * n_cmp, 256))
    kcv = jnp.pad(kcv.reshape(B, n_cmp, 2 * LANE), ((0, 0), (0, LANE - n_cmp), (0, 0))).astype(BF16)
    n_slc = -(-S // SEL_BLOCK)
    od = _nsa(r3(q_p), r3(gates), kcv, r3(kvs_b), r3(kvw_b), _bias_tiles(rel_bias, LANE),
              _bias_cmp(rel_bias, [j * LANE for j in range(S // LANE)], LANE), pair,
              qb=LANE, q_base=0, n_cmp=n_cmp, n_sel=min(N_SEL_BLOCKS, n_slc), w_off=0, grp=1)
    x2, hf, cwt = _row_call(_post_odd_kernel,
                            [xp.reshape(B * S, D), y, pre[0], pre[2], pre[3], pre[6], od.reshape(B * S, -1)],
                            [mp[2], mp[3], mp[4]], post_consts, POST_OUTS, tp, S // tp, "post_odd",
                            outs_t=[(N_EXPERTS, F32)])
    tm = MOE_TILE
    xp3 = _moe(hf, cwt.T, x2, mp[5], wg, wu, wd, tm, S // tm).reshape(B, S, D)
    kv5 = lambda a, n: a.reshape(-1, n, 2, N_KV_HEADS, HEAD_DIM)
    outs_p = (wkv_p, r3(zc)[:, S - 1], kv5(kvc, S), kv5(kvs, S), kv5(kvw, S)[:, S - min(WINDOW, S):])

    zc_s, q_s, kvc_s, kvs_s, _, kvw_s, _, gates_s = _inproj_odd(xs, [ms[0], ms[1]], g_mix, wts, gm, Bs, 1)
    pre_s = _rwkv_pre(zc_s, shift0, cpar, gsum, Bs)
    y_s, wkv_s = _rwkv_scan(pre_s, wkv0, Bs, 1, 1)
    n_pool = cmp_pool.shape[0]
    per_page = PAGE_SIZE // CMP_BLOCK
    kc_pool = _compress(cmp_pool.reshape(n_pool * per_page, CMP_BLOCK * 2 * LANE), cw, gtwo,
                        _pick_tile(n_pool * per_page, 256))
    kcv_s = _gather_pages(kc_pool.reshape(n_pool, per_page, 2 * LANE), page_table,
                          jnp.zeros((Bs, 1, 2 * LANE), F32), 2 * LANE, F32)
    n_cmp_s = (past + 1) // CMP_BLOCK
    kcv_s = jnp.pad(kcv_s, ((0, 0), (0, LANE - kcv_s.shape[1]), (0, 0))).astype(BF16)
    slc_t = slc_pool.transpose(0, 2, 3, 4, 1).reshape(n_pool, 2 * LANE, PAGE_SIZE)
    win_t = win_buf.transpose(0, 2, 3, 4, 1).reshape(Bs, 2 * LANE, w_eff)
    gates8 = jnp.pad(gates_s[:, :3 * N_ATT_HEADS].reshape(Bs, 3, N_ATT_HEADS).transpose(0, 2, 1),
                     ((0, 0), (0, 0), (0, LANE - 3)))
    n_slc_s = -(-(past + 1) // SEL_BLOCK)
    od_s = _nsa_sample(q_s.reshape(Bs, N_ATT_HEADS, LANE), gates8, kcv_s, kvs_s[:, None, :], kvw_s[:, None, :],
                       win_t, _bias_row(rel_bias, past), _bias_cmp(rel_bias, [past], 1)[0, :, 0, :],
                       _bias_lookup(rel_bias, w_eff - jnp.arange(w_eff)).T, pair, slc_t, page_table,
                       t_pos=past, n_cmp=n_cmp_s, n_sel=min(N_SEL_BLOCKS, n_slc_s),
                       grp=_pick_group(Bs, SAMPLE_GROUP)).reshape(Bs, -1)
    x2s, hfs, cwts = _row_call(_post_odd_kernel, [xs, y_s, pre_s[0], pre_s[2], pre_s[3], pre_s[6], od_s],
                               [ms[2], ms[3], ms[4]], post_consts, POST_OUTS, Bs, 1, "post_odd_s",
                               outs_t=[(N_EXPERTS, F32)])
    xs3 = _moe(hfs, cwts.T, x2s, ms[5], wg, wu, wd, Bs, 1)
    win_new = jnp.concatenate([win_buf[:, 1:], kv5(kvw_s, 1)], axis=1)
    outs_s = (wkv_s, zc_s, kv5(kvc_s, 1), kv5(kvs_s, 1), win_new)
    return xp3, xs3, outs_p, outs_s


def _mods(c_p, c_s, w, b):
    nb = c_p.shape[0]
    m = _ada(jnp.concatenate([c_p, c_s], 0), w.astype(BF16), b)
    parts = jnp.split(m, 6, axis=-1)
    return [p[:nb, None, :] for p in parts], [p[None, nb:, :] for p in parts]


def _forward(x_prompt, x_sample, c_prompt, c_sample, page_table, cache_a_kv, cache_a_kidx, state_b_conv,
             state_c_wkv, state_c_shift, cache_d_cmp, cache_d_slc, cache_d_win, rel_bias, w_router, b_router,
             w_ada, b_ada, g_norm_mix, g_norm_ffn, w_expert_gate, w_expert_up, w_expert_down, e_w_in, e_w_out,
             a_q_norm, a_k_norm, b_conv_w, b_conv_b, o_w_in, o_w_out, c_mu, c_w0, c_w_up, c_a0, c_a_up,
             c_g_up, c_k_k, c_k_a, c_r_k, c_ln_w, c_ln_b, d_q_norm, d_k_norm, d_cmp_pe, d_cmp_w):
    assert w_ada.shape[0] == 2 and e_w_in.shape[0] == 1 and o_w_in.shape[0] == 1
    B, S, D = x_prompt.shape
    Bs = x_sample.shape[0]
    assert x_sample.shape[1] == 1
    xp, xs = x_prompt, x_sample.reshape(Bs, D)
    router = (w_router.T, b_router.reshape(N_EXPERTS, 1))
    n_pool = cache_a_kv.shape[1]
    pool3 = lambda a: a.reshape(n_pool, PAGE_SIZE, -1)
    experts = lambda l: tuple(w[l].astype(BF16) for w in (w_expert_gate, w_expert_up, w_expert_down))

    mp, ms = _mods(c_prompt, c_sample, w_ada[0], b_ada[0])
    xp, xs, ep, es = _even_layer(xp, xs, mp, ms, page_table, cache_a_kv[0], cache_a_kidx[0], state_b_conv[0],
                                 rel_bias, router, experts(0), g_norm_mix[0], g_norm_ffn[0], e_w_in[0], e_w_out[0],
                                 a_q_norm[0], a_k_norm[0], b_conv_w[0], b_conv_b[0])
    mp, ms = _mods(c_prompt, c_sample, w_ada[1], b_ada[1])
    cpar = (c_mu[0], c_w0[0], c_w_up[0], c_a0[0], c_a_up[0], c_g_up[0], c_k_k[0], c_k_a[0])
    xp, xs, op, os_ = _odd_layer(xp, xs, mp, ms, page_table, state_c_wkv[0], state_c_shift[0], pool3(cache_d_cmp[0]),
                                 cache_d_slc[0], cache_d_win[0], rel_bias, router, experts(1), g_norm_mix[1],
                                 g_norm_ffn[1], o_w_in[0], o_w_out[0], cpar, c_r_k[0].reshape(-1), c_ln_w[0],
                                 c_ln_b[0], d_q_norm[0], d_k_norm[0], d_cmp_pe[0], d_cmp_w[0])
    stack = lambda ts: tuple(a[None] for a in ts)
    return (xp, xs.reshape(Bs, 1, D)) + stack(ep) + stack(op) + stack(es) + stack(os_)


def kernel(x_prompt, x_sample, c_prompt, c_sample, page_table, cache_a_kv, cache_a_kidx, state_b_conv, state_c_wkv, state_c_shift, cache_d_cmp, cache_d_slc, cache_d_win, rel_bias, w_router, b_router, w_ada, b_ada, g_norm_mix, g_norm_ffn, w_expert_gate, w_expert_up, w_expert_down, e_w_in, e_w_out, a_q_norm, a_k_norm, b_conv_w, b_conv_b, o_w_in, o_w_out, c_mu, c_w0, c_w_up, c_a0, c_a_up, c_g_up, c_k_k, c_k_a, c_r_k, c_ln_w, c_ln_b, d_q_norm, d_k_norm, d_cmp_pe, d_cmp_w):
    return _forward(x_prompt, x_sample, c_prompt, c_sample, page_table, cache_a_kv, cache_a_kidx, state_b_conv,
                    state_c_wkv, state_c_shift, cache_d_cmp, cache_d_slc, cache_d_win, rel_bias, w_router, b_router,
                    w_ada, b_ada, g_norm_mix, g_norm_ffn, w_expert_gate, w_expert_up, w_expert_down, e_w_in, e_w_out,
                    a_q_norm, a_k_norm, b_conv_w, b_conv_b, o_w_in, o_w_out, c_mu, c_w0, c_w_up, c_a0, c_a_up,
                    c_g_up, c_k_k, c_k_a, c_r_k, c_ln_w, c_ln_b, d_q_norm, d_k_norm, d_cmp_pe, d_cmp_w)
```

```python
import functools
import math

import jax
import jax.numpy as jnp
from jax import lax
from jax.experimental import pallas as pl
from jax.experimental.pallas import tpu as pltpu

F32 = jnp.float32
BF16 = jnp.bfloat16
I32 = jnp.int32

LANE = 128
HEAD_DIM = 64
N_ATT_HEADS = 8
N_KV_HEADS = 2
GQA = N_ATT_HEADS // N_KV_HEADS
IDX_HEADS = 4
IDX_DIM = 64
TOPK_MAX = 256
CONV_CH = 512
RWKV_HEADS = 8
RWKV_DIM = RWKV_HEADS * HEAD_DIM
LORA_W = 64
LORA_A = 64
LORA_G = 128
GN_EPS = 64e-5
CMP_BLOCK = 32
SEL_BLOCK = 64
N_SEL_BLOCKS = 8
WINDOW = 512
N_BUCKETS = 32
MAX_DISTANCE = 128
N_EXPERTS = 16
N_GROUPS = 4
EXPERTS_PER_GROUP = N_EXPERTS // N_GROUPS
D_EXPERT = 512
PAGE_SIZE = 128
RMS_EPS = 1e-6
NEG = -1e30
FORCE = 1e9
TAKEN = -3e38
ATT_SCALE = HEAD_DIM ** -0.5
VMEM_LIMIT = 56 * 1024 * 1024
ROW_TILE = 256
MOE_TILE = 512
SCAN_TIME_CHUNK = 32
SAMPLE_GROUP = 4


def _cparams(*sem):
    return pltpu.CompilerParams(dimension_semantics=sem, vmem_limit_bytes=VMEM_LIMIT)


def _pick_tile(rows, pref):
    t = min(pref, rows)
    while rows % t or (t % 8 and t != rows):
        t -= 1
    return t


def _pick_group(n, pref):
    g = min(pref, n)
    while n % g:
        g -= 1
    return g


def _const_spec(a):
    nd = a.ndim
    return pl.BlockSpec(a.shape, lambda *_: (0,) * nd)


def _dot(a, b):
    return jnp.dot(a.astype(BF16), b.astype(BF16), preferred_element_type=F32)


def _dot_nt(a, b):
    return lax.dot_general(a.astype(BF16), b.astype(BF16), (((1,), (1,)), ((), ())),
                           preferred_element_type=F32)


def _dot_split(x, m):
    hi = x.astype(BF16)
    r1 = x - hi.astype(F32)
    mid = r1.astype(BF16)
    lo = (r1 - mid.astype(F32)).astype(BF16)
    return (jnp.dot(hi, m, preferred_element_type=F32) + jnp.dot(mid, m, preferred_element_type=F32)
            + jnp.dot(lo, m, preferred_element_type=F32))


def _sigmoid(x):
    return 1.0 / (1.0 + jnp.exp(-x))


def _silu(x):
    return x * _sigmoid(x)


def _modulate(x, g, shift, scale):
    y = x * lax.rsqrt(jnp.mean(x * x, axis=-1, keepdims=True) + RMS_EPS)
    return (y * g) * (1.0 + scale) + shift


def _group_rms(t, gmat, gain):
    ms = _dot_split(t * t, gmat)
    return (t * lax.rsqrt(ms + RMS_EPS)) * gain


def _ada_kernel(c_ref, w_ref, b_ref, o_ref):
    o_ref[...] = _dot(_silu(c_ref[...]), w_ref[...]) + b_ref[...]


def _ada(c, w_bf, b):
    r, d = c.shape
    n = w_bf.shape[1]
    tn = 512
    return pl.pallas_call(
        _ada_kernel,
        grid=(n // tn,),
        in_specs=[pl.BlockSpec((r, d), lambda j: (0, 0)),
                  pl.BlockSpec((d, tn), lambda j: (0, j)),
                  pl.BlockSpec((1, tn), lambda j: (0, j))],
        out_specs=pl.BlockSpec((r, tn), lambda j: (0, j)),
        out_shape=jax.ShapeDtypeStruct((r, n), F32),
        compiler_params=_cparams("parallel"),
        name="ada_mod",
    )(c, w_bf, b.reshape(1, n))


E_Q0, E_KV0, E_QI0, E_MISC0, E_BG0, E_CG0, E_XIN0, E_END = 0, 1024, 1280, 1792, 1920, 2432, 2944, 3456


def _inproj_even_kernel(x_ref, shift_ref, scale_ref, g_ref, w_ref, qg_ref, kg_ref, gone_ref, gtwo_ref,
                        q_ref, kv_ref, kvb_ref, qi_ref, misc_ref, miscb_ref, bg_ref, u_ref):
    h = _modulate(x_ref[...], g_ref[...], shift_ref[...], scale_ref[...])
    z = jnp.dot(h.astype(BF16), w_ref[...], preferred_element_type=F32)
    gone = gone_ref[...]
    for t in range(N_ATT_HEADS):
        sl = slice(t * LANE, (t + 1) * LANE)
        q_ref[:, sl] = _group_rms(z[:, E_Q0 + t * LANE:E_Q0 + (t + 1) * LANE], gone, qg_ref[:, sl]).astype(BF16)
    k = _group_rms(z[:, E_KV0:E_KV0 + LANE], gtwo_ref[...], kg_ref[...])
    v = z[:, E_KV0 + LANE:E_KV0 + 2 * LANE]
    kv_ref[:, 0:LANE] = k
    kv_ref[:, LANE:2 * LANE] = v
    kvb_ref[:, 0:LANE] = k.astype(BF16)
    kvb_ref[:, LANE:2 * LANE] = v.astype(BF16)
    qi_ref[...] = z[:, E_QI0:E_MISC0].astype(BF16)
    misc = z[:, E_MISC0:E_BG0]
    misc_ref[...] = misc
    miscb_ref[...] = misc.astype(BF16)
    bg_ref[...] = z[:, E_BG0:E_CG0]
    u_ref[...] = z[:, E_CG0:E_XIN0] * z[:, E_XIN0:E_END]


def _row_call(kernel, xs, mods, consts, outs, tile, tpb, name, outs_t=()):
    if not isinstance(xs, (list, tuple)):
        xs = [xs]
    rows = xs[0].shape[0]
    n_tiles = rows // tile
    in_specs = [pl.BlockSpec((tile, x.shape[1]), lambda t: (t, 0)) for x in xs]
    for m in mods:
        in_specs.append(pl.BlockSpec((None,) + m.shape[1:], lambda t: (t // tpb, 0, 0)))
    in_specs += [_const_spec(c) for c in consts]
    out_specs = [pl.BlockSpec((tile, w), lambda t: (t, 0)) for (w, _) in outs]
    out_shape = [jax.ShapeDtypeStruct((rows, w), dt) for (w, dt) in outs]
    out_specs += [pl.BlockSpec((hh, tile), lambda t: (0, t)) for (hh, _) in outs_t]
    out_shape += [jax.ShapeDtypeStruct((hh, rows), dt) for (hh, dt) in outs_t]
    return pl.pallas_call(kernel, grid=(n_tiles,), in_specs=in_specs, out_specs=out_specs, out_shape=out_shape,
                          compiler_params=_cparams("parallel"), name=name)(*xs, *mods, *consts)


def _pad_q_cols(wq):
    d = wq.shape[0]
    w = wq.reshape(d, N_ATT_HEADS, HEAD_DIM)
    z = jnp.zeros_like(w)
    lo = jnp.concatenate([w, z], -1)
    hi = jnp.concatenate([z, w], -1)
    sel = (jnp.arange(N_ATT_HEADS) >= GQA)[None, :, None]
    return jnp.where(sel, hi, lo).reshape(d, N_ATT_HEADS * LANE)


def _pad_o_rows(wo):
    return _pad_q_cols(wo.T).T


def _gmats():
    i = jnp.arange(LANE)
    gone = jnp.full((LANE, LANE), 1.0 / HEAD_DIM, F32).astype(BF16)
    gtwo = jnp.where((i[:, None] // HEAD_DIM) == (i[None, :] // HEAD_DIM), 1.0 / HEAD_DIM, 0.0).astype(BF16)
    tri = jnp.where(i[:, None] <= i[None, :], 1.0, 0.0).astype(BF16)
    return gone, gtwo, tri


def _even_weights(w_in, w_out, q_norm, k_norm):
    d = w_in.shape[0]
    a_q, a_kv = N_ATT_HEADS * HEAD_DIM, 2 * N_KV_HEADS * HEAD_DIM
    o = 0
    wq = w_in[:, o:o + a_q]; o += a_q
    wkv = w_in[:, o:o + a_kv]; o += a_kv
    wqi = w_in[:, o:o + IDX_HEADS * IDX_DIM]; o += IDX_HEADS * IDX_DIM
    wki = w_in[:, o:o + IDX_DIM]; o += IDX_DIM
    wwi = w_in[:, o:o + IDX_HEADS]; o += IDX_HEADS
    wrest = w_in[:, o:]
    wqi = jnp.concatenate([wqi.reshape(d, IDX_HEADS, IDX_DIM), jnp.zeros((d, IDX_HEADS, LANE - IDX_DIM), F32)],
                          -1).reshape(d, IDX_HEADS * LANE)
    wmisc = jnp.concatenate([wki, wwi, jnp.zeros((d, LANE - IDX_DIM - IDX_HEADS), F32)], -1)
    w_in_p = jnp.concatenate([_pad_q_cols(wq), wkv, wqi, wmisc, wrest], -1).astype(BF16)
    w_out_p = jnp.concatenate([_pad_o_rows(w_out[:a_q]), w_out[a_q:]], 0).astype(BF16)
    qg = jnp.tile(q_norm, 2 * N_ATT_HEADS).reshape(1, N_ATT_HEADS * LANE)
    kg = jnp.tile(k_norm, 2).reshape(1, LANE)
    return w_in_p, w_out_p, qg, kg


def _inproj_even(x, mods, g, wts, gm, tile, tpb):
    w_in_p, _, qg, kg = wts
    gone, gtwo, _ = gm
    outs = [(N_ATT_HEADS * LANE, BF16), (2 * LANE, F32), (2 * LANE, BF16), (IDX_HEADS * LANE, BF16),
            (LANE, F32), (LANE, BF16), (CONV_CH, F32), (CONV_CH, F32)]
    return _row_call(_inproj_even_kernel, x, mods, [g, w_in_p, qg, kg, gone, gtwo], outs, tile, tpb, "inproj_even")


def _t5_bucket(dist):
    dist = jnp.maximum(dist, 0)
    exact = N_BUCKETS // 2
    far = exact + (jnp.log(jnp.maximum(dist, 1).astype(F32) / exact)
                   / math.log(MAX_DISTANCE / exact) * (N_BUCKETS - exact)).astype(I32)
    return jnp.where(dist < exact, dist, jnp.minimum(far, N_BUCKETS - 1))


def _bias_lookup(rel_bias, dist):
    onehot = (_t5_bucket(dist)[..., None] == jnp.arange(N_BUCKETS)).astype(F32)
    return jnp.einsum("...k,kh->...h", onehot, rel_bias, precision=lax.Precision.HIGHEST)


def _bias_row(rel_bias, t_pos):
    return _bias_lookup(rel_bias, t_pos - jnp.arange(t_pos + LANE)).T


def _bias_tiles(rel_bias, qb):
    r = jnp.arange(qb)[:, None]
    c = jnp.arange(LANE)[None, :]
    tiles = [_bias_lookup(rel_bias, d * LANE + r - c) for d in range(3)]
    return jnp.stack(tiles).transpose(0, 3, 1, 2)


def _flash_step(qs, kb, vb, bias, msk, m, l, acc):
    s = _dot_nt(qs, kb) * ATT_SCALE + bias
    s = jnp.where(msk, s, NEG)
    m_new = jnp.maximum(m, jnp.max(s, axis=1, keepdims=True))
    p = jnp.where(msk, jnp.exp(s - m_new), 0.0)
    alpha = jnp.exp(m - m_new)
    l = alpha * l + jnp.sum(p, axis=1, keepdims=True)
    acc = alpha * acc + jnp.dot(p.astype(BF16), vb, preferred_element_type=F32)
    return m_new, l, acc


def _flash_init(rows):
    return (jnp.full((rows, 1), NEG, F32), jnp.zeros((rows, 1), F32), jnp.zeros((rows, LANE), F32))


def _stack_heads(q_ref, hk):
    return jnp.concatenate([q_ref[:, (hk * GQA + g) * LANE:(hk * GQA + g + 1) * LANE] for g in range(GQA)], axis=0)


def _tile4(x):
    return jnp.concatenate([x] * GQA, axis=0)


def _loop(lo, hi, body, init, unroll):
    if unroll:
        for j in range(lo, hi):
            init = body(j, init)
        return init
    return lax.fori_loop(lo, hi, body, init)


def _dsa_kernel(q_ref, qi_ref, misc_ref, kidx_ref, kv_ref, bias_ref, tri_ref, o_ref, key_s,
                *, grp, qb, q_base, n_keep, single):
    q0 = q_base if single else q_base + pl.program_id(1) * qb
    nblk = (q0 + qb - 1) // LANE + 1
    dq = q0 // LANE
    seqs = range(grp)
    row = lax.broadcasted_iota(I32, (qb, LANE), 0)
    lane = lax.broadcasted_iota(I32, (qb, LANE), 1)
    t_pos = q0 + row
    wis = [[jnp.broadcast_to(misc_ref[g][:, IDX_DIM + h:IDX_DIM + h + 1], (qb, LANE)) for h in range(IDX_HEADS)]
           for g in seqs]
    idx_scale = (IDX_HEADS * IDX_DIM) ** -0.5

    def pass_a(j, c):
        off = pl.multiple_of(j * LANE, LANE)
        causal = j * LANE + lane <= t_pos
        for g in seqs:
            kb = kidx_ref[g, pl.ds(off, LANE), :]
            acc = jnp.zeros((qb, LANE), F32)
            for h in range(IDX_HEADS):
                acc = acc + jnp.maximum(_dot_nt(qi_ref[g, :, h * LANE:(h + 1) * LANE], kb), 0.0) * wis[g][h]
            sc = jnp.where(causal, acc * idx_scale, NEG)
            sc = jnp.where(sc == 0.0, 0.0, sc)
            bits = lax.bitcast_convert_type(sc, I32)
            key_s[g, j] = jnp.where(bits < 0, bits ^ jnp.int32(0x7FFFFFFF), bits)
        return c

    lax.fori_loop(0, nblk, pass_a, 0)

    def count(pred):
        def body(j, accs):
            return tuple(a + jnp.where(pred(g, key_s[g, j]), 1.0, 0.0) for g, a in zip(seqs, accs))
        accs = _loop(0, nblk, body, tuple(jnp.zeros((qb, LANE), F32) for _ in seqs), single)
        return [jnp.sum(a, axis=1, keepdims=True) for a in accs]

    keep = float(n_keep)
    int_min = jnp.int32(-2 ** 31)
    thr = tuple(jnp.where(c >= keep, jnp.int32(0), int_min) for c in count(lambda g, k: k >= 0))

    def search(it, thr):
        bit = lax.shift_left(jnp.int32(1), jnp.int32(30) - it)
        cand = [t | bit for t in thr]
        cnt = count(lambda g, k: k >= cand[g])
        return tuple(jnp.where(c >= keep, cd, t) for c, cd, t in zip(cnt, cand, thr))

    thr = lax.fori_loop(0, 31, search, thr)
    need = [keep - c for c in count(lambda g, k: k > thr[g])]
    tri = tri_ref[...]

    def pass_c(j, run):
        causal = j * LANE + lane <= t_pos
        out = []
        for g in seqs:
            key = key_s[g, j]
            eq = key == thr[g]
            eqf = jnp.where(eq, 1.0, 0.0)
            cum = jnp.dot(eqf.astype(BF16), tri, preferred_element_type=F32) + run[g]
            sel = (key > thr[g]) | (eq & (cum <= need[g]))
            key_s[g, j] = jnp.where(sel & causal, 1, 0)
            out.append(run[g] + jnp.sum(eqf, axis=1, keepdims=True))
        return tuple(out)

    lax.fori_loop(0, nblk, pass_c, tuple(jnp.zeros((qb, 1), F32) for _ in seqs))

    qs = [[_stack_heads(q_ref.at[g], hk) for hk in range(N_KV_HEADS)] for g in seqs]

    def pass_d(j, carry):
        off = pl.multiple_of(j * LANE, LANE)
        dsel = jnp.minimum(dq - j, 2)
        biases = [bias_ref[dsel, hk * GQA:(hk + 1) * GQA].reshape(GQA * qb, LANE) for hk in range(N_KV_HEADS)]
        out = []
        for g in seqs:
            kb = kv_ref[g, pl.ds(off, LANE), 0:LANE]
            vb = kv_ref[g, pl.ds(off, LANE), LANE:2 * LANE]
            msk = _tile4(key_s[g, j] > 0)
            out.append(tuple(_flash_step(qs[g][hk], kb, vb, biases[hk], msk, *carry[g][hk])
                             for hk in range(N_KV_HEADS)))
        return tuple(out)

    init = tuple(tuple(_flash_init(GQA * qb) for _ in range(N_KV_HEADS)) for _ in seqs)
    res = lax.fori_loop(0, nblk, pass_d, init)
    for g in seqs:
        _write_heads(o_ref.at[g], [acc / l for (_, l, acc) in res[g]], qb)


def _write_heads(o_ref, outs, qb):
    lane = lax.broadcasted_iota(I32, (qb, LANE), 1)
    for hk in range(N_KV_HEADS):
        valid = (lane // HEAD_DIM) == hk
        for g in range(GQA):
            h = hk * GQA + g
            o_ref[:, h * LANE:(h + 1) * LANE] = jnp.where(valid, outs[hk][g * qb:(g + 1) * qb], 0.0).astype(BF16)


def _dsa(q, qi, misc, kidx_b, kv_b, bias, tri, *, qb, q_base, n_keep, grp):
    b, rows, _ = q.shape
    nq = rows // qb
    s = kv_b.shape[1]
    kern = functools.partial(_dsa_kernel, grp=grp, qb=qb, q_base=q_base, n_keep=n_keep, single=(nq == 1))
    qspec = lambda w: pl.BlockSpec((grp, qb, w), lambda bi, i: (bi, i, 0))
    kspec = lambda w: pl.BlockSpec((grp, s, w), lambda bi, i: (bi, 0, 0))
    return pl.pallas_call(
        kern,
        grid=(b // grp, nq),
        in_specs=[qspec(N_ATT_HEADS * LANE), qspec(IDX_HEADS * LANE), qspec(LANE), kspec(LANE), kspec(2 * LANE),
                  _const_spec(bias), _const_spec(tri)],
        out_specs=qspec(N_ATT_HEADS * LANE),
        out_shape=jax.ShapeDtypeStruct((b, rows, N_ATT_HEADS * LANE), BF16),
        scratch_shapes=[pltpu.VMEM((grp, s // LANE, qb, LANE), I32)],
        compiler_params=_cparams("parallel", "parallel"),
        name="dsa_attention",
    )(q, qi, misc, kidx_b, kv_b, bias, tri)


def _select_top(keys, n_keep, tri):
    keep = float(n_keep)
    n = keys[0].shape[1]

    def count(pred):
        return [jnp.sum(jnp.where(pred(g, k), 1.0, 0.0), axis=1, keepdims=True) for g, k in enumerate(keys)]

    int_min = jnp.int32(-2 ** 31)
    thr = tuple(jnp.where(c >= keep, jnp.int32(0), int_min) for c in count(lambda g, k: k >= 0))

    def search(it, thr):
        bit = lax.shift_left(jnp.int32(1), jnp.int32(30) - it)
        cand = [t | bit for t in thr]
        cnt = count(lambda g, k: k >= cand[g])
        return tuple(jnp.where(c >= keep, cd, t) for c, cd, t in zip(cnt, cand, thr))

    thr = lax.fori_loop(0, 31, search, thr)
    need = [keep - c for c in count(lambda g, k: k > thr[g])]
    sels = []
    for g, k in enumerate(keys):
        run = jnp.zeros((1, 1), F32)
        parts = []
        for t in range(n // LANE):
            kt = k[:, t * LANE:(t + 1) * LANE]
            eq = kt == thr[g]
            eqf = jnp.where(eq, 1.0, 0.0)
            cum = jnp.dot(eqf.astype(BF16), tri, preferred_element_type=F32) + run
            parts.append((kt > thr[g]) | (eq & (cum <= need[g])))
            run = run + jnp.sum(eqf, axis=1, keepdims=True)
        sels.append(jnp.concatenate(parts, axis=1))
    return sels


def _order_keys(score):
    score = jnp.where(score == 0.0, 0.0, score)
    bits = lax.bitcast_convert_type(score, I32)
    return jnp.where(bits < 0, bits ^ jnp.int32(0x7FFFFFFF), bits)


def _head_rows_out(o_ref, g, acc):
    rowh = lax.broadcasted_iota(I32, (N_ATT_HEADS, LANE), 0)
    laneh = lax.broadcasted_iota(I32, (N_ATT_HEADS, LANE), 1)
    o_ref[g] = jnp.where((laneh // HEAD_DIM) == (rowh // GQA), acc, 0.0).astype(BF16)


def _dsa_sample_kernel(pt_ref, q_ref, qi_ref, wi_ref, knew_ref, kvnew_ref, bias_ref, tri_ref, *refs,
                       grp, n_pages, n_keep):
    del pt_ref
    ki_refs, kv_refs, o_ref = refs[:grp * n_pages], refs[grp * n_pages:2 * grp * n_pages], refs[-1]
    lane1 = lax.broadcasted_iota(I32, (1, LANE), 1)
    idx_scale = (IDX_HEADS * IDX_DIM) ** -0.5
    keys = []
    for g in range(grp):
        qi = qi_ref[g]
        wi = wi_ref[g]
        tiles = []
        for p in range(n_pages):
            rel = jnp.maximum(_dot(qi, ki_refs[g * n_pages + p][...]), 0.0)
            tiles.append(jnp.sum(rel * wi, axis=0, keepdims=True) * idx_scale)
        rel_new = jnp.maximum(jnp.sum(qi.astype(F32) * knew_ref[g], axis=1, keepdims=True), 0.0)
        sc_new = jnp.sum(rel_new * wi, axis=0, keepdims=True) * idx_scale
        tiles.append(jnp.where(lane1 == 0, sc_new, NEG))
        keys.append(_order_keys(jnp.concatenate(tiles, axis=1)))
    sels = _select_top(keys, n_keep, tri_ref[...])
    bias = bias_ref[...]
    for g in range(grp):
        q = q_ref[g]
        kvnew = kvnew_ref[g]
        tiles = [_dot(q, kv_refs[g * n_pages + p][0:LANE, :]) for p in range(n_pages)]
        s_new = jnp.sum(q.astype(F32) * kvnew[:, 0:LANE], axis=1, keepdims=True)
        tiles.append(jnp.where(lane1 == 0, s_new, 0.0))
        valid = sels[g] & (jnp.concatenate([lane1] * n_pages + [lane1 + LANE], axis=1) <= LANE)
        s = jnp.where(valid, jnp.concatenate(tiles, axis=1) * ATT_SCALE + bias, NEG)
        e = jnp.exp(s - jnp.max(s, axis=1, keepdims=True))
        p_all = jnp.where(valid, e / jnp.sum(e, axis=1, keepdims=True), 0.0)
        acc = p_all[:, n_pages * LANE:n_pages * LANE + 1] * kvnew[:, LANE:2 * LANE]
        for p in range(n_pages):
            acc = acc + _dot_nt(p_all[:, p * LANE:(p + 1) * LANE], kv_refs[g * n_pages + p][LANE:2 * LANE, :])
        _head_rows_out(o_ref, g, acc)


def _page_specs(pool_t, n_pages, grp):
    r, c = pool_t.shape[1:]
    return [pl.BlockSpec((None, r, c), lambda i, pt, g=g, p=p: (pt[i * grp + g, p], 0, 0))
            for g in range(grp) for p in range(n_pages)]


def _dsa_sample(q8, qi8, wi8, knew, kvnew, bias, tri, ki_t, kv_t, page_table, *, n_keep, grp):
    b, n_pages = page_table.shape
    gspec = lambda a: pl.BlockSpec((grp,) + a.shape[1:], lambda i, pt: (i,) + (0,) * (a.ndim - 1))
    cspec = lambda a: pl.BlockSpec(a.shape, lambda i, pt: (0,) * a.ndim)
    kern = functools.partial(_dsa_sample_kernel, grp=grp, n_pages=n_pages, n_keep=n_keep)
    return pl.pallas_call(
        kern,
        grid_spec=pltpu.PrefetchScalarGridSpec(
            num_scalar_prefetch=1, grid=(b // grp,),
            in_specs=[gspec(q8), gspec(qi8), gspec(wi8), gspec(knew), gspec(kvnew), cspec(bias), cspec(tri)]
            + _page_specs(ki_t, n_pages, grp) + _page_specs(kv_t, n_pages, grp),
            out_specs=pl.BlockSpec((grp, N_ATT_HEADS, LANE), lambda i, pt: (i, 0, 0))),
        out_shape=jax.ShapeDtypeStruct((b, N_ATT_HEADS, LANE), BF16),
        compiler_params=_cparams("parallel"),
        name="dsa_sample",
    )(page_table, q8, qi8, wi8, knew, kvnew, bias, tri, *([ki_t] * (grp * n_pages)), *([kv_t] * (grp * n_pages)))


def _route(hf, wrt, br):
    logits = lax.dot_general(wrt, hf, (((1,), (1,)), ((), ())), precision=lax.Precision.HIGHEST,
                             preferred_element_type=F32)
    s = _sigmoid(logits)
    sel = s + br
    rows = [sel[e:e + 1, :] for e in range(N_EXPERTS)]
    grp = []
    for g in range(N_GROUPS):
        a = rows[g * EXPERTS_PER_GROUP:(g + 1) * EXPERTS_PER_GROUP]
        best = None
        for i in range(EXPERTS_PER_GROUP):
            for j in range(i + 1, EXPERTS_PER_GROUP):
                v = a[i] + a[j]
                best = v if best is None else jnp.maximum(best, v)
        grp.append(best)
    gbest = jnp.zeros_like(grp[0], dtype=I32)
    cur = grp[0]
    for g in range(1, N_GROUPS):
        better = grp[g] > cur
        gbest = jnp.where(better, g, gbest)
        cur = jnp.where(better, grp[g], cur)
    picked = []
    for g in range(N_GROUPS):
        a = rows[g * EXPERTS_PER_GROUP:(g + 1) * EXPERTS_PER_GROUP]
        for j in range(EXPERTS_PER_GROUP):
            rank = jnp.zeros_like(a[j])
            for jj in range(EXPERTS_PER_GROUP):
                if jj != j:
                    ahead = (a[jj] > a[j]) | (a[jj] == a[j]) if jj < j else (a[jj] > a[j])
                    rank = rank + jnp.where(ahead, 1.0, 0.0)
            e = g * EXPERTS_PER_GROUP + j
            picked.append(jnp.where((gbest == g) & (rank < 2.0), s[e:e + 1, :], 0.0))
    den = picked[0]
    for p in picked[1:]:
        den = den + p
    return jnp.concatenate([p / den for p in picked], axis=0)


def _post_tail(x, mix, gate, gf, shf, scf, wrt_ref, br_ref, x2_ref, hf_ref, cwt_ref):
    x2 = x + gate * mix
    x2_ref[...] = x2
    hf = _modulate(x2, gf, shf, scf)
    hf_ref[...] = hf.astype(BF16)
    cwt_ref[...] = _route(hf, wrt_ref[...], br_ref[...])


def _post_even_kernel(x_ref, oa_ref, bg_ref, u_ref, um1_ref, um2_ref, gate_ref, shf_ref, scf_ref,
                      gf_ref, cw_ref, cb_ref, wo_ref, wrt_ref, br_ref, x2_ref, hf_ref, cwt_ref):
    cw = cw_ref[...]
    y = cb_ref[...] + cw[0:1, :] * um2_ref[...]
    y = y + cw[1:2, :] * um1_ref[...]
    y = y + cw[2:3, :] * u_ref[...]
    n_a = N_ATT_HEADS * LANE
    mix = (jnp.dot(oa_ref[...], wo_ref[0:n_a, :], preferred_element_type=F32)
           + jnp.dot((bg_ref[...] * y).astype(BF16), wo_ref[n_a:n_a + CONV_CH, :], preferred_element_type=F32))
    _post_tail(x_ref[...], mix, gate_ref[...], gf_ref[...], shf_ref[...], scf_ref[...], wrt_ref, br_ref,
               x2_ref, hf_ref, cwt_ref)


POST_OUTS = [(1024, F32), (1024, BF16)]


def _moe_kernel(hf_ref, cw_ref, x2_ref, gate_ref, wg_ref, wu_ref, wd_ref, o_ref, acc_ref):
    e = pl.program_id(1)

    @pl.when(e == 0)
    def _():
        acc_ref[...] = jnp.zeros_like(acc_ref)

    hf = hf_ref[...]
    hmid = _silu(jnp.dot(hf, wg_ref[...], preferred_element_type=F32)) * jnp.dot(hf, wu_ref[...],
                                                                                 preferred_element_type=F32)
    cw = cw_ref[...]
    lane = lax.broadcasted_iota(I32, cw.shape, 1)
    wcol = jnp.sum(jnp.where(lane == e, cw, 0.0), axis=1, keepdims=True)
    acc_ref[...] += jnp.dot((hmid * wcol).astype(BF16), wd_ref[...], preferred_element_type=F32)

    @pl.when(e == N_EXPERTS - 1)
    def _():
        o_ref[...] = x2_ref[...] + gate_ref[...] * acc_ref[...]


def _moe(hf, cw, x2, gate, wg, wu, wd, tile, tpb):
    rows, d = x2.shape
    de = wg.shape[2]
    return pl.pallas_call(
        _moe_kernel,
        grid=(rows // tile, N_EXPERTS),
        in_specs=[pl.BlockSpec((tile, d), lambda t, e: (t, 0)),
                  pl.BlockSpec((tile, N_EXPERTS), lambda t, e: (t, 0)),
                  pl.BlockSpec((tile, d), lambda t, e: (t, 0)),
                  pl.BlockSpec((None,) + gate.shape[1:], lambda t, e: (t // tpb, 0, 0)),
                  pl.BlockSpec((None, d, de), lambda t, e: (e, 0, 0)),
                  pl.BlockSpec((None, d, de), lambda t, e: (e, 0, 0)),
                  pl.BlockSpec((None, de, d), lambda t, e: (e, 0, 0))],
        out_specs=pl.BlockSpec((tile, d), lambda t, e: (t, 0)),
        out_shape=jax.ShapeDtypeStruct((rows, d), F32),
        scratch_shapes=[pltpu.VMEM((tile, d), F32)],
        compiler_params=_cparams("parallel", "arbitrary"),
        name="moe_dense",
    )(hf, cw, x2, gate, wg, wu, wd)


def _shift_rows(u3, k):
    return jnp.pad(u3, ((0, 0), (k, 0), (0, 0)))[:, :u3.shape[1]]


def _even_layer(xp, xs, mp, ms, page_table, kv_pool, kidx_pool, conv_buf, rel_bias, router, experts,
                g_mix, g_ffn, w_in, w_out, q_norm, k_norm, conv_w, conv_b):
    B, S, D = xp.shape
    Bs = xs.shape[0]
    past = page_table.shape[1] * PAGE_SIZE
    wts = _even_weights(w_in, w_out, q_norm, k_norm)
    gm = _gmats()
    wrt, br = router
    wg, wu, wd = experts
    g_mix = g_mix.reshape(1, D)
    g_ffn = g_ffn.reshape(1, D)
    post_consts = [g_ffn, conv_w, conv_b.reshape(1, CONV_CH), wts[1], wrt, br]

    tp = ROW_TILE
    q_p, kv_f, kv_b, qi_p, misc, misc_b, bg, u = _inproj_even(xp.reshape(B * S, D), [mp[0], mp[1]], g_mix, wts, gm,
                                                              tp, S // tp)
    r3 = lambda a: a.reshape(B, S, a.shape[-1])
    oa = _dsa(r3(q_p), r3(qi_p), r3(misc), r3(misc_b), r3(kv_b), _bias_tiles(rel_bias, LANE), gm[2],
              qb=LANE, q_base=0, n_keep=min(TOPK_MAX, S // 4), grp=1)
    u3 = r3(u)
    um1 = _shift_rows(u3, 1).reshape(B * S, CONV_CH)
    um2 = _shift_rows(u3, 2).reshape(B * S, CONV_CH)
    x2, hf, cwt = _row_call(_post_even_kernel, [xp.reshape(B * S, D), oa.reshape(B * S, -1), bg, u, um1, um2],
                            [mp[2], mp[3], mp[4]], post_consts, POST_OUTS, tp, S // tp, "post_even",
                            outs_t=[(N_EXPERTS, F32)])
    tm = MOE_TILE
    xp3 = _moe(hf, cwt.T, x2, mp[5], wg, wu, wd, tm, S // tm).reshape(B, S, D)
    outs_p = (kv_f.reshape(B, S, 2, N_KV_HEADS, HEAD_DIM), misc[:, :IDX_DIM].reshape(B, S, IDX_DIM), u3[:, S - 2:])

    q_s, kv_fs, _, qi_s, misc_s, _, bg_s, u_s = _inproj_even(xs, [ms[0], ms[1]], g_mix, wts, gm, Bs, 1)
    n_pool = kv_pool.shape[0]
    kv_t = kv_pool.transpose(0, 2, 3, 4, 1).reshape(n_pool, 2 * LANE, PAGE_SIZE)
    ki_t = kidx_pool.transpose(0, 2, 1)
    qi8 = jnp.pad(qi_s.reshape(Bs, IDX_HEADS, LANE)[:, :, :IDX_DIM], ((0, 0), (0, N_ATT_HEADS - IDX_HEADS), (0, 0)))
    wi8 = jnp.pad(misc_s[:, IDX_DIM:IDX_DIM + IDX_HEADS], ((0, 0), (0, N_ATT_HEADS - IDX_HEADS)))[:, :, None]
    oa_s = _dsa_sample(q_s.reshape(Bs, N_ATT_HEADS, LANE), qi8, wi8, misc_s[:, None, :IDX_DIM], kv_fs[:, None, :],
                       _bias_row(rel_bias, past), gm[2], ki_t, kv_t, page_table,
                       n_keep=min(TOPK_MAX, (past + 1) // 4), grp=_pick_group(Bs, SAMPLE_GROUP)).reshape(Bs, -1)
    x2s, hfs, cwts = _row_call(_post_even_kernel, [xs, oa_s, bg_s, u_s, conv_buf[:, 1], conv_buf[:, 0]],
                               [ms[2], ms[3], ms[4]], post_consts, POST_OUTS, Bs, 1, "post_even_s",
                               outs_t=[(N_EXPERTS, F32)])
    xs3 = _moe(hfs, cwts.T, x2s, ms[5], wg, wu, wd, Bs, 1)
    outs_s = (kv_fs.reshape(Bs, 1, 2, N_KV_HEADS, HEAD_DIM), misc_s[:, None, :IDX_DIM],
              jnp.concatenate([conv_buf[:, 1:], u_s[:, None, :]], axis=1))
    return xp3, xs3, outs_p, outs_s


O_Z0, O_Q0, O_KVC0, O_KVS0, O_KVW0, O_G0, O_END = 0, 1792, 2816, 3072, 3328, 3584, 3712
P_C = 3 * RWKV_DIM + LORA_W + LORA_A + LORA_G


def _inproj_odd_kernel(x_ref, shift_ref, scale_ref, g_ref, w_ref, qg_ref, ksg_ref, kwg_ref, gone_ref, gtwo_ref,
                       zc_ref, q_ref, kvc_ref, kvs_ref, kvsb_ref, kvw_ref, kvwb_ref, gates_ref):
    h = _modulate(x_ref[...], g_ref[...], shift_ref[...], scale_ref[...])
    z = jnp.dot(h.astype(BF16), w_ref[...], preferred_element_type=F32)
    zc_ref[...] = z[:, O_Z0:O_Q0]
    gone = gone_ref[...]
    gtwo = gtwo_ref[...]
    for t in range(N_ATT_HEADS):
        sl = slice(t * LANE, (t + 1) * LANE)
        q_ref[:, sl] = _group_rms(z[:, O_Q0 + t * LANE:O_Q0 + (t + 1) * LANE], gone, qg_ref[:, sl]).astype(BF16)
    kvc_ref[...] = z[:, O_KVC0:O_KVS0]
    for base, gain_ref, f_ref, b_ref in ((O_KVS0, ksg_ref, kvs_ref, kvsb_ref), (O_KVW0, kwg_ref, kvw_ref, kvwb_ref)):
        k = _group_rms(z[:, base:base + LANE], gtwo, gain_ref[...])
        v = z[:, base + LANE:base + 2 * LANE]
        f_ref[:, 0:LANE] = k
        f_ref[:, LANE:2 * LANE] = v
        b_ref[:, 0:LANE] = k.astype(BF16)
        b_ref[:, LANE:2 * LANE] = v.astype(BF16)
    gates_ref[...] = _sigmoid(z[:, O_G0:O_END])


def _odd_weights(w_in, w_out, q_norm, k_norm):
    d = w_in.shape[0]
    a_q, a_kv = N_ATT_HEADS * HEAD_DIM, 2 * N_KV_HEADS * HEAD_DIM
    o = P_C
    wz = w_in[:, :o]
    wq = w_in[:, o:o + a_q]; o += a_q
    wkv = w_in[:, o:o + 3 * a_kv]; o += 3 * a_kv
    wg = w_in[:, o:]
    wg = jnp.concatenate([wg, jnp.zeros((d, LANE - wg.shape[1]), F32)], -1)
    w_in_p = jnp.concatenate([wz, _pad_q_cols(wq), wkv, wg], -1).astype(BF16)
    w_out_p = jnp.concatenate([w_out[:RWKV_DIM], _pad_o_rows(w_out[RWKV_DIM:])], 0).astype(BF16)
    qg = jnp.tile(q_norm, 2 * N_ATT_HEADS).reshape(1, N_ATT_HEADS * LANE)
    ksg = jnp.tile(k_norm[1], 2).reshape(1, LANE)
    kwg = jnp.tile(k_norm[2], 2).reshape(1, LANE)
    return w_in_p, w_out_p, qg, ksg, kwg


def _inproj_odd(x, mods, g, wts, gm, tile, tpb):
    w_in_p, _, qg, ksg, kwg = wts
    gone, gtwo, _ = gm
    outs = [(P_C, F32), (N_ATT_HEADS * LANE, BF16), (2 * LANE, F32), (2 * LANE, F32), (2 * LANE, BF16),
            (2 * LANE, F32), (2 * LANE, BF16), (LANE, F32)]
    return _row_call(_inproj_odd_kernel, x, mods, [g, w_in_p, qg, ksg, kwg, gone, gtwo], outs, tile, tpb,
                     "inproj_odd")


def _rwkv_pre_kernel(z_ref, zp_ref, mu_ref, w0_ref, a0_ref, kk_ref, ka_ref, wup_ref, aup_ref, gup_ref, gsum_ref,
                     r_o, w_o, k_o, v_o, kk_o, kka_o, g_o):
    z = z_ref[...]
    zm = z + (zp_ref[...] - z) * mu_ref[...]
    r = zm[:, 0:RWKV_DIM]
    k = zm[:, RWKV_DIM:2 * RWKV_DIM]
    v = zm[:, 2 * RWKV_DIM:3 * RWKV_DIM]
    t12 = zm[:, 3 * RWKV_DIM:3 * RWKV_DIM + LANE]
    gd = zm[:, 3 * RWKV_DIM + LANE:P_C]
    xw = w0_ref[...] + _dot(jnp.tanh(t12), wup_ref[...])
    sp = jnp.maximum(-xw, 0.0) + jnp.log(1.0 + jnp.exp(-jnp.abs(xw)))
    w_o[...] = jnp.exp(-jnp.exp(-sp - 0.5))
    a = _sigmoid(a0_ref[...] + _dot(t12, aup_ref[...]))
    g_o[...] = _dot(_sigmoid(gd), gup_ref[...])
    kk = k * kk_ref[...]
    gsum = gsum_ref[...]
    for t in range(RWKV_DIM // LANE):
        sl = slice(t * LANE, (t + 1) * LANE)
        kt = kk[:, sl]
        nrm = jnp.maximum(jnp.sqrt(_dot_split(kt * kt, gsum)), 1e-12)
        kn = kt / nrm
        kk_o[:, sl] = kn
        kka_o[:, sl] = kn * a[:, sl]
    r_o[...] = r
    v_o[...] = v
    k_o[...] = k * (1.0 + (a - 1.0) * ka_ref[...])


def _rwkv_pre(zc, zprev, cpar, gsum, tile):
    mu, w0, w_up, a0, a_up, g_up, k_k, k_a = cpar
    z64 = jnp.zeros((LORA_W, RWKV_DIM), F32)
    consts = [mu.reshape(1, P_C), w0.reshape(1, -1), a0.reshape(1, -1), k_k.reshape(1, -1), k_a.reshape(1, -1),
              jnp.concatenate([w_up, z64], 0).astype(BF16), jnp.concatenate([z64, a_up], 0).astype(BF16),
              g_up.astype(BF16), gsum]
    return _row_call(_rwkv_pre_kernel, [zc, zprev], [], consts, [(RWKV_DIM, F32)] * 7, tile, 1, "rwkv_pre")


SCAN_P = 64
SCAN_VH = HEAD_DIM // 2


def _scan_kernel(kk_ref, w_ref, kka_ref, k_ref, r_ref, v_ref, s0_ref, y_ref, so_ref, st, *, tc):
    ti = pl.program_id(1)

    @pl.when(ti == 0)
    def _():
        st[...] = s0_ref[...]

    def step(t, c):
        kk, w, kka, kt, rt, vt = kk_ref[t], w_ref[t], kka_ref[t], k_ref[t], r_ref[t], v_ref[t]
        ys = []
        for vi in range(SCAN_VH):
            s = st[vi]
            sa = -jnp.sum(s * kk, axis=0, keepdims=True)
            sn = s * w + sa * kka + vt[vi:vi + 1, :] * kt
            st[vi] = sn
            ys.append(jnp.sum(sn * rt, axis=0, keepdims=True))
        y_ref[t] = jnp.concatenate(ys, axis=0)
        return c

    lax.fori_loop(0, tc, step, 0)

    @pl.when(ti == pl.num_programs(1) - 1)
    def _():
        so_ref[...] = st[...]


def _scan_layout_k(x, b, t):
    p = b * RWKV_HEADS
    a = x.reshape(b, t, RWKV_HEADS, HEAD_DIM).transpose(1, 3, 0, 2).reshape(t, HEAD_DIM, p)
    nc = -(-p // SCAN_P)
    a = jnp.pad(a, ((0, 0), (0, 0), (0, nc * SCAN_P - p))).reshape(t, HEAD_DIM, nc, SCAN_P).transpose(2, 0, 1, 3)
    return jnp.concatenate([a, a], -1)


def _scan_layout_v(x, b, t):
    p = b * RWKV_HEADS
    a = x.reshape(b, t, RWKV_HEADS, HEAD_DIM).transpose(1, 3, 0, 2).reshape(t, HEAD_DIM, p)
    nc = -(-p // SCAN_P)
    a = jnp.pad(a, ((0, 0), (0, 0), (0, nc * SCAN_P - p))).reshape(t, HEAD_DIM, nc, SCAN_P).transpose(2, 0, 1, 3)
    return jnp.concatenate([a[:, :, :SCAN_VH], a[:, :, SCAN_VH:]], -1)


def _scan_unlayout_y(y, b, t):
    p = b * RWKV_HEADS
    nc = y.shape[0]
    a = jnp.concatenate([y[..., :SCAN_P], y[..., SCAN_P:]], axis=2)
    a = a.transpose(1, 2, 0, 3).reshape(t, HEAD_DIM, nc * SCAN_P)[:, :, :p]
    return a.reshape(t, HEAD_DIM, b, RWKV_HEADS).transpose(2, 0, 3, 1).reshape(b * t, RWKV_DIM)


def _scan_layout_state(s):
    b = s.shape[0]
    p = b * RWKV_HEADS
    nc = -(-p // SCAN_P)
    a = jnp.pad(s.reshape(p, HEAD_DIM, HEAD_DIM), ((0, nc * SCAN_P - p), (0, 0), (0, 0)))
    a = a.reshape(nc, SCAN_P, HEAD_DIM, HEAD_DIM).transpose(0, 2, 3, 1)
    return jnp.concatenate([a[:, :SCAN_VH], a[:, SCAN_VH:]], -1)


def _scan_unlayout_state(st, b):
    p = b * RWKV_HEADS
    nc = st.shape[0]
    a = jnp.concatenate([st[..., :SCAN_P], st[..., SCAN_P:]], axis=1)
    a = a.transpose(0, 3, 1, 2).reshape(nc * SCAN_P, HEAD_DIM, HEAD_DIM)[:p]
    return a.reshape(b, RWKV_HEADS, HEAD_DIM, HEAD_DIM)


def _rwkv_scan(pre, s0, b, t, tc):
    r, w, k, v, kk, kka, _ = pre
    ks = [_scan_layout_k(a, b, t) for a in (kk, w, kka, k, r)]
    vs = _scan_layout_v(v, b, t)
    s0l = _scan_layout_state(s0)
    nc = s0l.shape[0]
    kspec = pl.BlockSpec((None, tc, HEAD_DIM, LANE), lambda c, i: (c, i, 0, 0))
    vspec = pl.BlockSpec((None, tc, SCAN_VH, LANE), lambda c, i: (c, i, 0, 0))
    sspec = pl.BlockSpec((None, SCAN_VH, HEAD_DIM, LANE), lambda c, i: (c, 0, 0, 0))
    y, so = pl.pallas_call(
        functools.partial(_scan_kernel, tc=tc),
        grid=(nc, t // tc),
        in_specs=[kspec] * 5 + [vspec, sspec],
        out_specs=[vspec, sspec],
        out_shape=[jax.ShapeDtypeStruct((nc, t, SCAN_VH, LANE), F32),
                   jax.ShapeDtypeStruct((nc, SCAN_VH, HEAD_DIM, LANE), F32)],
        scratch_shapes=[pltpu.VMEM((SCAN_VH, HEAD_DIM, LANE), F32)],
        compiler_params=_cparams("parallel", "arbitrary"),
        name="rwkv_scan",
    )(*ks, vs, s0l)
    return _scan_unlayout_y(y, b, t), _scan_unlayout_state(so, b)


def _compress_kernel(x_ref, pe_ref, w_ref, kg_ref, gtwo_ref, o_ref):
    z = jnp.dot((x_ref[...] + pe_ref[...]).astype(BF16), w_ref[...], preferred_element_type=F32)
    o_ref[:, 0:LANE] = _group_rms(z[:, 0:LANE], gtwo_ref[...], kg_ref[...])
    o_ref[:, LANE:2 * LANE] = z[:, LANE:2 * LANE]


def _compress_weights(cmp_pe, cmp_w, k_norm_c):
    wk = cmp_w[0].reshape(CMP_BLOCK, HEAD_DIM, HEAD_DIM)
    wv = cmp_w[1].reshape(CMP_BLOCK, HEAD_DIM, HEAD_DIM)
    full = jnp.zeros((CMP_BLOCK, 4, HEAD_DIM, 4, HEAD_DIM), F32)
    for s, w in enumerate((wk, wk, wv, wv)):
        full = full.at[:, s, :, s, :].set(w)
    pe = jnp.stack([cmp_pe[0], cmp_pe[0], cmp_pe[1], cmp_pe[1]], axis=1)
    return (full.reshape(CMP_BLOCK * 4 * HEAD_DIM, 4 * HEAD_DIM).astype(BF16), pe.reshape(1, -1),
            jnp.tile(k_norm_c, 2).reshape(1, LANE))


def _compress(rows, cw, gtwo, tile):
    wfull, pe, kg = cw
    return _row_call(_compress_kernel, rows, [], [pe, wfull, kg, gtwo], [(2 * LANE, F32)], tile, 1, "nsa_compress")[0]


def _compress_paged_kernel(pt_ref, ident_ref, pe_ref, w_ref, kg_ref, gtwo_ref, *refs, grp, n_pages):
    del pt_ref
    page_refs, o_ref, xs = refs[:grp * n_pages], refs[-2], refs[-1]
    ident = ident_ref[...]
    for i in range(grp * n_pages):
        for half in range(2):
            xt = page_refs[i][half * LANE:(half + 1) * LANE, :]
            hi = xt.astype(BF16)
            r1 = xt - hi.astype(F32)
            mid = r1.astype(BF16)
            lo = (r1 - mid.astype(F32)).astype(BF16)
            xs[half, i * PAGE_SIZE:(i + 1) * PAGE_SIZE, :] = (_dot_nt(ident, hi) + _dot_nt(ident, mid)
                                                              + _dot_nt(ident, lo))
    n_blk = grp * n_pages * (PAGE_SIZE // CMP_BLOCK)
    acc = [jnp.zeros((n_blk, LANE), F32) for _ in range(2)]
    for r in range(CMP_BLOCK):
        for half in range(2):
            rows = xs[half, pl.ds(r, n_blk, stride=CMP_BLOCK), :] + pe_ref[half, r:r + 1, :]
            acc[half] = acc[half] + jnp.dot(rows.astype(BF16), w_ref[half, r], preferred_element_type=F32)
    kc = _group_rms(acc[0], gtwo_ref[...], kg_ref[...])
    per_seq = n_blk // grp
    for g in range(grp):
        o_ref[g, 0:per_seq, 0:LANE] = kc[g * per_seq:(g + 1) * per_seq].astype(BF16)
        o_ref[g, 0:per_seq, LANE:2 * LANE] = acc[1][g * per_seq:(g + 1) * per_seq].astype(BF16)
        o_ref[g, per_seq:LANE, :] = jnp.zeros((LANE - per_seq, 2 * LANE), BF16)


def _compress_paged(cmp_t, page_table, cmp_pe, cmp_w, k_norm_c, gtwo, grp):
    b, n_pages = page_table.shape
    wk = cmp_w[0].reshape(CMP_BLOCK, HEAD_DIM, HEAD_DIM)
    wv = cmp_w[1].reshape(CMP_BLOCK, HEAD_DIM, HEAD_DIM)
    wbd = jnp.zeros((2, CMP_BLOCK, N_KV_HEADS, HEAD_DIM, N_KV_HEADS, HEAD_DIM), F32)
    for half, w in enumerate((wk, wv)):
        for h in range(N_KV_HEADS):
            wbd = wbd.at[half, :, h, :, h, :].set(w)
    wbd = wbd.reshape(2, CMP_BLOCK, LANE, LANE).astype(BF16)
    pe = jnp.concatenate([cmp_pe, cmp_pe], axis=2)
    ident = jnp.eye(LANE, dtype=BF16)
    kg = jnp.tile(k_norm_c, 2).reshape(1, LANE)
    cspec = lambda a: pl.BlockSpec(a.shape, lambda i, pt: (0,) * a.ndim)
    kern = functools.partial(_compress_paged_kernel, grp=grp, n_pages=n_pages)
    return pl.pallas_call(
        kern,
        grid_spec=pltpu.PrefetchScalarGridSpec(
            num_scalar_prefetch=1, grid=(b // grp,),
            in_specs=[cspec(ident), cspec(pe), cspec(wbd), cspec(kg), cspec(gtwo)]
            + _page_specs(cmp_t, n_pages, grp),
            out_specs=pl.BlockSpec((grp, LANE, 2 * LANE), lambda i, pt: (i, 0, 0)),
            scratch_shapes=[pltpu.VMEM((2, grp * n_pages * PAGE_SIZE, LANE), F32)]),
        out_shape=jax.ShapeDtypeStruct((b, LANE, 2 * LANE), BF16),
        compiler_params=_cparams("parallel"),
        name="nsa_compress_paged",
    )(page_table, ident, pe, wbd, kg, gtwo, *([cmp_t] * (grp * n_pages)))


def _nsa_kernel(q_ref, gates_ref, kcv_ref, kvs_ref, kvw_ref, bias_ref, biasc_ref, pair_ref, o_ref,
                *, grp, qb, q_base, n_cmp, n_sel, w_off, single):
    q0 = q_base if single else q_base + pl.program_id(1) * qb
    nblk = (q0 + qb - 1) // LANE + 1
    dq = q0 // LANE
    seqs = range(grp)
    heads = range(N_KV_HEADS)
    row = lax.broadcasted_iota(I32, (qb, LANE), 0)
    lane = lax.broadcasted_iota(I32, (qb, LANE), 1)
    t_pos = q0 + row
    qs = [[_stack_heads(q_ref.at[g], hk) for hk in heads] for g in seqs]

    def gate_col(g, br, hk):
        gates = gates_ref[g]
        return jnp.concatenate([gates[:, br * N_ATT_HEADS + hk * GQA + a:br * N_ATT_HEADS + hk * GQA + a + 1]
                                for a in range(GQA)], axis=0)

    mask_c4 = _tile4(((lane * CMP_BLOCK + CMP_BLOCK - 1) <= t_pos) & (lane < n_cmp))
    cur = t_pos // SEL_BLOCK
    forced = (lane == 0) | (lane == cur) | (lane == cur - 1)
    sel_causal = lane * SEL_BLOCK <= t_pos
    lane_f = lane.astype(F32)
    o_cmp = [[None] * N_KV_HEADS for _ in seqs]
    selm = [[None] * N_KV_HEADS for _ in seqs]
    for g in seqs:
        kc = kcv_ref[g, :, 0:LANE]
        vc = kcv_ref[g, :, LANE:2 * LANE]
        for hk in heads:
            s = _dot_nt(qs[g][hk], kc) * ATT_SCALE + biasc_ref[hk * GQA:(hk + 1) * GQA].reshape(GQA * qb, LANE)
            s = jnp.where(mask_c4, s, NEG)
            e = jnp.exp(s - jnp.max(s, axis=1, keepdims=True))
            p = jnp.where(mask_c4, e / jnp.sum(e, axis=1, keepdims=True), 0.0)
            o_cmp[g][hk] = jnp.dot(p.astype(BF16), vc, preferred_element_type=F32)
            ps = p[0:qb]
            for a in range(1, GQA):
                ps = ps + p[a * qb:(a + 1) * qb]
            score = _dot_split(ps, pair_ref[...])
            score = jnp.where(sel_causal, jnp.where(forced, FORCE, score), NEG)
            picked = jnp.zeros((qb, LANE), F32)
            for _ in range(n_sel):
                mx = jnp.max(score, axis=1, keepdims=True)
                first = jnp.min(jnp.where(score == mx, lane_f, float(LANE)), axis=1, keepdims=True)
                hit = lane_f == first
                picked = jnp.where(hit, 1.0, picked)
                score = jnp.where(hit, TAKEN, score)
            selm[g][hk] = picked.astype(BF16)

    def key_blocks(ref, g, jb):
        off = pl.multiple_of(jb * LANE, LANE)
        return ref[g, pl.ds(off, LANE), 0:LANE], ref[g, pl.ds(off, LANE), LANE:2 * LANE]

    def biases_of(jb):
        dsel = jnp.minimum(dq - jb, 2)
        return [bias_ref[dsel, hk * GQA:(hk + 1) * GQA].reshape(GQA * qb, LANE) for hk in heads]

    init = tuple(tuple(_flash_init(GQA * qb) for _ in heads) for _ in seqs)

    erow = lax.broadcasted_iota(I32, (LANE, LANE), 0)
    ecol = lax.broadcasted_iota(I32, (LANE, LANE), 1)

    def slc_body(jb, carry):
        expand = jnp.where(erow == 2 * jb + ecol // SEL_BLOCK, 1.0, 0.0).astype(BF16)
        causal = jb * LANE + lane <= t_pos
        biases = biases_of(jb)
        out = []
        for g in seqs:
            kb, vb = key_blocks(kvs_ref, g, jb)
            res = []
            for hk in heads:
                tok = jnp.dot(selm[g][hk], expand, preferred_element_type=F32) > 0.5
                res.append(_flash_step(qs[g][hk], kb, vb, biases[hk], _tile4(tok & causal), *carry[g][hk]))
            out.append(tuple(res))
        return tuple(out)

    res_s = lax.fori_loop(0, nblk, slc_body, init)

    def win_body(jb, carry):
        dist = t_pos - (jb * LANE + lane)
        msk = _tile4((dist >= 0) & (dist < WINDOW))
        biases = biases_of(jb)
        out = []
        for g in seqs:
            kb, vb = key_blocks(kvw_ref, g, jb - w_off)
            out.append(tuple(_flash_step(qs[g][hk], kb, vb, biases[hk], msk, *carry[g][hk]) for hk in heads))
        return tuple(out)

    res_w = lax.fori_loop(jnp.maximum(dq - WINDOW // LANE, 0), dq + 1, win_body, init)

    for g in seqs:
        outs = []
        for hk in heads:
            o_s = res_s[g][hk][2] / res_s[g][hk][1]
            o_w = res_w[g][hk][2] / res_w[g][hk][1]
            outs.append(gate_col(g, 0, hk) * o_cmp[g][hk] + gate_col(g, 1, hk) * o_s + gate_col(g, 2, hk) * o_w)
        _write_heads(o_ref.at[g], outs, qb)


def _bias_cmp(rel_bias, q_starts, qb):
    q0 = jnp.asarray(q_starts, I32)[:, None, None]
    r = jnp.arange(qb)[None, :, None]
    n = jnp.arange(LANE)[None, None, :]
    return _bias_lookup(rel_bias, q0 + r - (n * CMP_BLOCK + CMP_BLOCK - 1)).transpose(0, 3, 1, 2)


def _nsa(q, gates, kcv, kvs_b, kvw_b, bias, bias_c, pair, *, qb, q_base, n_cmp, n_sel, w_off, grp):
    b, rows, _ = q.shape
    nq = rows // qb
    kern = functools.partial(_nsa_kernel, grp=grp, qb=qb, q_base=q_base, n_cmp=n_cmp, n_sel=n_sel, w_off=w_off,
                             single=(nq == 1))
    qspec = lambda w: pl.BlockSpec((grp, qb, w), lambda bi, i: (bi, i, 0))
    kspec = lambda a: pl.BlockSpec((grp,) + a.shape[1:], lambda bi, i: (bi, 0, 0))
    return pl.pallas_call(
        kern,
        grid=(b // grp, nq),
        in_specs=[qspec(N_ATT_HEADS * LANE), qspec(LANE), kspec(kcv), kspec(kvs_b), kspec(kvw_b), _const_spec(bias),
                  pl.BlockSpec((None,) + bias_c.shape[1:], lambda bi, i: (i, 0, 0, 0)), _const_spec(pair)],
        out_specs=qspec(N_ATT_HEADS * LANE),
        out_shape=jax.ShapeDtypeStruct((b, rows, N_ATT_HEADS * LANE), BF16),
        compiler_params=_cparams("parallel", "parallel"),
        name="nsa_attention",
    )(q, gates, kcv, kvs_b, kvw_b, bias, bias_c, pair)


def _nsa_sample_kernel(pt_ref, q_ref, gates_ref, kcv_ref, snew_ref, wnew_ref, win_ref, bias_ref, biasc_ref,
                       biasw_ref, pair_ref, *refs, grp, n_pages, t_pos, n_cmp, n_sel, w_eff):
    del pt_ref
    slc_refs, o_ref = refs[:grp * n_pages], refs[-1]
    lane1 = lax.broadcasted_iota(I32, (1, LANE), 1)
    lane8 = lax.broadcasted_iota(I32, (N_ATT_HEADS, LANE), 1)
    row_all = lax.broadcasted_iota(I32, (N_ATT_HEADS, (n_pages + 1) * LANE), 0)
    lanew = lax.broadcasted_iota(I32, (N_ATT_HEADS, w_eff), 1)
    lane1_f = lane1.astype(F32)
    bias = bias_ref[...]
    bias_now = bias[:, n_pages * LANE:n_pages * LANE + 1]
    mask_c = ((lane8 * CMP_BLOCK + CMP_BLOCK - 1) <= t_pos) & (lane8 < n_cmp)
    cur = t_pos // SEL_BLOCK
    forced = (lane1 == 0) | (lane1 == cur) | (lane1 == cur - 1)
    sel_causal = lane1 * SEL_BLOCK <= t_pos
    tail_valid = jnp.concatenate([lane1] * n_pages + [lane1 + LANE], axis=1) <= LANE
    for g in range(grp):
        q = q_ref[g]
        qf = q.astype(F32)
        s = jnp.where(mask_c, _dot_nt(q, kcv_ref[g, :, 0:LANE]) * ATT_SCALE + biasc_ref[...], NEG)
        e = jnp.exp(s - jnp.max(s, axis=1, keepdims=True))
        pc = jnp.where(mask_c, e / jnp.sum(e, axis=1, keepdims=True), 0.0)
        o_c = jnp.dot(pc.astype(BF16), kcv_ref[g, :, LANE:2 * LANE], preferred_element_type=F32)
        masks = []
        for hk in range(N_KV_HEADS):
            ps = jnp.sum(pc[hk * GQA:(hk + 1) * GQA], axis=0, keepdims=True)
            score = _dot_split(ps, pair_ref[...])
            score = jnp.where(sel_causal, jnp.where(forced, FORCE, score), NEG)
            picked = jnp.zeros((1, LANE), F32)
            for _ in range(n_sel):
                mx = jnp.max(score, axis=1, keepdims=True)
                first = jnp.min(jnp.where(score == mx, lane1_f, float(LANE)), axis=1, keepdims=True)
                hit = lane1_f == first
                picked = jnp.where(hit, 1.0, picked)
                score = jnp.where(hit, TAKEN, score)
            per_page = PAGE_SIZE // SEL_BLOCK
            tiles = []
            for p in range(n_pages + 1):
                t = jnp.zeros((1, LANE), F32)
                for a in range(per_page):
                    blk = picked[:, p * per_page + a:p * per_page + a + 1]
                    t = jnp.where(lane1 // SEL_BLOCK == a, blk, t)
                tiles.append(t)
            masks.append(jnp.concatenate(tiles, axis=1))
        valid = (jnp.where(row_all < GQA, masks[0], masks[1]) > 0.5) & tail_valid
        snew = snew_ref[g]
        tiles = [_dot(q, slc_refs[g * n_pages + p][0:LANE, :]) for p in range(n_pages)]
        tiles.append(jnp.where(lane1 == 0, jnp.sum(qf * snew[:, 0:LANE], axis=1, keepdims=True), 0.0))
        s = jnp.where(valid, jnp.concatenate(tiles, axis=1) * ATT_SCALE + bias, NEG)
        e = jnp.exp(s - jnp.max(s, axis=1, keepdims=True))
        p_all = jnp.where(valid, e / jnp.sum(e, axis=1, keepdims=True), 0.0)
        o_s = p_all[:, n_pages * LANE:n_pages * LANE + 1] * snew[:, LANE:2 * LANE]
        for p in range(n_pages):
            o_s = o_s + _dot_nt(p_all[:, p * LANE:(p + 1) * LANE], slc_refs[g * n_pages + p][LANE:2 * LANE, :])
        wnew = wnew_ref[g]
        valid_w = (w_eff - lanew) < WINDOW
        s_w = jnp.where(valid_w, _dot(q, win_ref[g, 0:LANE, :]) * ATT_SCALE + biasw_ref[...], NEG)
        s_n = jnp.sum(qf * wnew[:, 0:LANE], axis=1, keepdims=True) * ATT_SCALE + bias_now
        m = jnp.maximum(jnp.max(s_w, axis=1, keepdims=True), s_n)
        e_w = jnp.where(valid_w, jnp.exp(s_w - m), 0.0)
        e_n = jnp.exp(s_n - m)
        den = jnp.sum(e_w, axis=1, keepdims=True) + e_n
        o_w = _dot_nt(e_w / den, win_ref[g, LANE:2 * LANE, :]) + (e_n / den) * wnew[:, LANE:2 * LANE]
        gates = gates_ref[g]
        _head_rows_out(o_ref, g, gates[:, 0:1] * o_c + gates[:, 1:2] * o_s + gates[:, 2:3] * o_w)


def _nsa_sample(q8, gates8, kcv, snew, wnew, win_t, bias, bias_c, bias_w, pair, slc_t, page_table,
                *, t_pos, n_cmp, n_sel, grp):
    b, n_pages = page_table.shape
    w_eff = win_t.shape[2]
    gspec = lambda a: pl.BlockSpec((grp,) + a.shape[1:], lambda i, pt: (i,) + (0,) * (a.ndim - 1))
    cspec = lambda a: pl.BlockSpec(a.shape, lambda i, pt: (0,) * a.ndim)
    kern = functools.partial(_nsa_sample_kernel, grp=grp, n_pages=n_pages, t_pos=t_pos, n_cmp=n_cmp, n_sel=n_sel,
                             w_eff=w_eff)
    return pl.pallas_call(
        kern,
        grid_spec=pltpu.PrefetchScalarGridSpec(
            num_scalar_prefetch=1, grid=(b // grp,),
            in_specs=[gspec(q8), gspec(gates8), gspec(kcv), gspec(snew), gspec(wnew), gspec(win_t), cspec(bias),
                      cspec(bias_c), cspec(bias_w), cspec(pair)] + _page_specs(slc_t, n_pages, grp),
            out_specs=pl.BlockSpec((grp, N_ATT_HEADS, LANE), lambda i, pt: (i, 0, 0))),
        out_shape=jax.ShapeDtypeStruct((b, N_ATT_HEADS, LANE), BF16),
        compiler_params=_cparams("parallel"),
        name="nsa_sample",
    )(page_table, q8, gates8, kcv, snew, wnew, win_t, bias, bias_c, bias_w, pair, *([slc_t] * (grp * n_pages)))


def _post_odd_kernel(x_ref, y_ref, r_ref, k_ref, v_ref, g_ref, od_ref, gate_ref, shf_ref, scf_ref,
                     gf_ref, lnw_ref, lnb_ref, rk_ref, gtwo_ref, wo_ref, wrt_ref, br_ref, x2_ref, hf_ref, cwt_ref):
    gtwo = gtwo_ref[...]
    mix = jnp.dot(od_ref[...], wo_ref[RWKV_DIM:RWKV_DIM + N_ATT_HEADS * LANE, :], preferred_element_type=F32)
    for t in range(RWKV_DIM // LANE):
        sl = slice(t * LANE, (t + 1) * LANE)
        y = y_ref[:, sl]
        dlt = y - _dot_split(y, gtwo)
        yn = (dlt * lax.rsqrt(_dot_split(dlt * dlt, gtwo) + GN_EPS)) * lnw_ref[:, sl] + lnb_ref[:, sl]
        dot_rk = _dot_split(r_ref[:, sl] * k_ref[:, sl] * rk_ref[:, sl], gtwo) * float(HEAD_DIM)
        oc = (yn + dot_rk * v_ref[:, sl]) * g_ref[:, sl]
        mix = mix + jnp.dot(oc.astype(BF16), wo_ref[sl, :], preferred_element_type=F32)
    _post_tail(x_ref[...], mix, gate_ref[...], gf_ref[...], shf_ref[...], scf_ref[...], wrt_ref, br_ref,
               x2_ref, hf_ref, cwt_ref)


def _odd_layer(xp, xs, mp, ms, page_table, wkv0, shift0, cmp_pool, slc_pool, win_buf, rel_bias, router, experts,
               g_mix, g_ffn, w_in, w_out, cpar, r_k, ln_w, ln_b, q_norm, k_norm, cmp_pe, cmp_w):
    B, S, D = xp.shape
    Bs = xs.shape[0]
    n_pages = page_table.shape[1]
    past = n_pages * PAGE_SIZE
    wts = _odd_weights(w_in, w_out, q_norm, k_norm)
    gm = _gmats()
    gone, gtwo, _ = gm
    gsum = (gtwo.astype(F32) * HEAD_DIM).astype(BF16)
    i = jnp.arange(LANE)
    pair = jnp.where(i[:, None] // 2 == i[None, :], 1.0, 0.0).astype(BF16)
    cw = _compress_weights(cmp_pe, cmp_w, k_norm[0])
    wrt, br = router
    wg, wu, wd = experts
    g_mix = g_mix.reshape(1, D)
    post_consts = [g_ffn.reshape(1, D), ln_w.reshape(1, -1), ln_b.reshape(1, -1), r_k.reshape(1, -1), gtwo, wts[1],
                   wrt, br]
    w_eff = win_buf.shape[1]

    tp = ROW_TILE
    zc, q_p, kvc, kvs, kvs_b, kvw, kvw_b, gates = _inproj_odd(xp.reshape(B * S, D), [mp[0], mp[1]], g_mix, wts, gm,
                                                               tp, S // tp)
    r3 = lambda a: a.reshape(B, S, a.shape[-1])
    pre = _rwkv_pre(zc, _shift_rows(r3(zc), 1).reshape(B * S, P_C), cpar, gsum, tp)
    y, wkv_p = _rwkv_scan(pre, jnp.zeros((B, RWKV_HEADS, HEAD_DIM, HEAD_DIM), F32), B, S, SCAN_TIME_CHUNK)
    n_cmp = S // CMP_BLOCK
    kcv = _compress(kvc.reshape(B * n_cmp, CMP_BLOCK * 2 * LANE), cw, gtwo, _pick_tile(B * n_cmp, 256))
    kcv = jnp.pad(kcv.reshape(B, n_cmp, 2 * LANE), ((0, 0), (0, LANE - n_cmp), (0, 0))).astype(BF16)
    n_slc = -(-S // SEL_BLOCK)
    od = _nsa(r3(q_p), r3(gates), kcv, r3(kvs_b), r3(kvw_b), _bias_tiles(rel_bias, LANE),
              _bias_cmp(rel_bias, [j * LANE for j in range(S // LANE)], LANE), pair,
              qb=LANE, q_base=0, n_cmp=n_cmp, n_sel=min(N_SEL_BLOCKS, n_slc), w_off=0, grp=1)
    x2, hf, cwt = _row_call(_post_odd_kernel,
                            [xp.reshape(B * S, D), y, pre[0], pre[2], pre[3], pre[6], od.reshape(B * S, -1)],
                            [mp[2], mp[3], mp[4]], post_consts, POST_OUTS, tp, S // tp, "post_odd",
                            outs_t=[(N_EXPERTS, F32)])
    tm = MOE_TILE
    xp3 = _moe(hf, cwt.T, x2, mp[5], wg, wu, wd, tm, S // tm).reshape(B, S, D)
    kv5 = lambda a, n: a.reshape(-1, n, 2, N_KV_HEADS, HEAD_DIM)
    outs_p = (wkv_p, r3(zc)[:, S - 1], kv5(kvc, S), kv5(kvs, S), kv5(kvw, S)[:, S - min(WINDOW, S):])

    zc_s, q_s, kvc_s, kvs_s, _, kvw_s, _, gates_s = _inproj_odd(xs, [ms[0], ms[1]], g_mix, wts, gm, Bs, 1)
    pre_s = _rwkv_pre(zc_s, shift0, cpar, gsum, Bs)
    y_s, wkv_s = _rwkv_scan(pre_s, wkv0, Bs, 1, 1)
    n_pool = cmp_pool.shape[0]
    cmp_t = cmp_pool.transpose(0, 2, 3, 4, 1).reshape(n_pool, 2 * LANE, PAGE_SIZE)
    kcv_s = _compress_paged(cmp_t, page_table, cmp_pe, cmp_w, k_norm[0], gtwo, _pick_group(Bs, SAMPLE_GROUP))
    n_cmp_s = (past + 1) // CMP_BLOCK
    slc_t = slc_pool.transpose(0, 2, 3, 4, 1).reshape(n_pool, 2 * LANE, PAGE_SIZE)
    win_t = win_buf.transpose(0, 2, 3, 4, 1).reshape(Bs, 2 * LANE, w_eff)
    gates8 = jnp.pad(gates_s[:, :3 * N_ATT_HEADS].reshape(Bs, 3, N_ATT_HEADS).transpose(0, 2, 1),
                     ((0, 0), (0, 0), (0, LANE - 3)))
    n_slc_s = -(-(past + 1) // SEL_BLOCK)
    od_s = _nsa_sample(q_s.reshape(Bs, N_ATT_HEADS, LANE), gates8, kcv_s, kvs_s[:, None, :], kvw_s[:, None, :],
                       win_t, _bias_row(rel_bias, past), _bias_cmp(rel_bias, [past], 1)[0, :, 0, :],
                       _bias_lookup(rel_bias, w_eff - jnp.arange(w_eff)).T, pair, slc_t, page_table,
                       t_pos=past, n_cmp=n_cmp_s, n_sel=min(N_SEL_BLOCKS, n_slc_s),
                       grp=_pick_group(Bs, SAMPLE_GROUP)).reshape(Bs, -1)
    x2s, hfs, cwts = _row_call(_post_odd_kernel, [xs, y_s, pre_s[0], pre_s[2], pre_s[3], pre_s[6], od_s],
                               [ms[2], ms[3], ms[4]], post_consts, POST_OUTS, Bs, 1, "post_odd_s",
                               outs_t=[(N_EXPERTS, F32)])
    xs3 = _moe(hfs, cwts.T, x2s, ms[5], wg, wu, wd, Bs, 1)
    win_new = jnp.concatenate([win_buf[:, 1:], kv5(kvw_s, 1)], axis=1)
    outs_s = (wkv_s, zc_s, kv5(kvc_s, 1), kv5(kvs_s, 1), win_new)
    return xp3, xs3, outs_p, outs_s


def _mods(c_p, c_s, w, b):
    nb = c_p.shape[0]
    m = _ada(jnp.concatenate([c_p, c_s], 0), w.astype(BF16), b)
    parts = jnp.split(m, 6, axis=-1)
    return [p[:nb, None, :] for p in parts], [p[None, nb:, :] for p in parts]


def _forward(x_prompt, x_sample, c_prompt, c_sample, page_table, cache_a_kv, cache_a_kidx, state_b_conv,
             state_c_wkv, state_c_shift, cache_d_cmp, cache_d_slc, cache_d_win, rel_bias, w_router, b_router,
             w_ada, b_ada, g_norm_mix, g_norm_ffn, w_expert_gate, w_expert_up, w_expert_down, e_w_in, e_w_out,
             a_q_norm, a_k_norm, b_conv_w, b_conv_b, o_w_in, o_w_out, c_mu, c_w0, c_w_up, c_a0, c_a_up,
             c_g_up, c_k_k, c_k_a, c_r_k, c_ln_w, c_ln_b, d_q_norm, d_k_norm, d_cmp_pe, d_cmp_w):
    assert w_ada.shape[0] == 2 and e_w_in.shape[0] == 1 and o_w_in.shape[0] == 1
    B, S, D = x_prompt.shape
    Bs = x_sample.shape[0]
    assert x_sample.shape[1] == 1
    xp, xs = x_prompt, x_sample.reshape(Bs, D)
    router = (w_router.T, b_router.reshape(N_EXPERTS, 1))
    n_pool = cache_a_kv.shape[1]
    experts = lambda l: tuple(w[l].astype(BF16) for w in (w_expert_gate, w_expert_up, w_expert_down))

    mp, ms = _mods(c_prompt, c_sample, w_ada[0], b_ada[0])
    xp, xs, ep, es = _even_layer(xp, xs, mp, ms, page_table, cache_a_kv[0], cache_a_kidx[0], state_b_conv[0],
                                 rel_bias, router, experts(0), g_norm_mix[0], g_norm_ffn[0], e_w_in[0], e_w_out[0],
                                 a_q_norm[0], a_k_norm[0], b_conv_w[0], b_conv_b[0])
    mp, ms = _mods(c_prompt, c_sample, w_ada[1], b_ada[1])
    cpar = (c_mu[0], c_w0[0], c_w_up[0], c_a0[0], c_a_up[0], c_g_up[0], c_k_k[0], c_k_a[0])
    xp, xs, op, os_ = _odd_layer(xp, xs, mp, ms, page_table, state_c_wkv[0], state_c_shift[0], cache_d_cmp[0],
                                 cache_d_slc[0], cache_d_win[0], rel_bias, router, experts(1), g_norm_mix[1],
                                 g_norm_ffn[1], o_w_in[0], o_w_out[0], cpar, c_r_k[0].reshape(-1), c_ln_w[0],
                                 c_ln_b[0], d_q_norm[0], d_k_norm[0], d_cmp_pe[0], d_cmp_w[0])
    stack = lambda ts: tuple(a[None] for a in ts)
    return (xp, xs.reshape(Bs, 1, D)) + stack(ep) + stack(op) + stack(es) + stack(os_)


def kernel(x_prompt, x_sample, c_prompt, c_sample, page_table, cache_a_kv, cache_a_kidx, state_b_conv, state_c_wkv, state_c_shift, cache_d_cmp, cache_d_slc, cache_d_win, rel_bias, w_router, b_router, w_ada, b_ada, g_norm_mix, g_norm_ffn, w_expert_gate, w_expert_up, w_expert_down, e_w_in, e_w_out, a_q_norm, a_k_norm, b_conv_w, b_conv_b, o_w_in, o_w_out, c_mu, c_w0, c_w_up, c_a0, c_a_up, c_g_up, c_k_k, c_k_a, c_r_k, c_ln_w, c_ln_b, d_q_norm, d_k_norm, d_cmp_pe, d_cmp_w):
    return _forward(x_prompt, x_sample, c_prompt, c_sample, page_table, cache_a_kv, cache_a_kidx, state_b_conv,
                    state_c_wkv, state_c_shift, cache_d_cmp, cache_d_slc, cache_d_win, rel_bias, w_router, b_router,
                    w_ada, b_ada, g_norm_mix, g_norm_ffn, w_expert_gate, w_expert_up, w_expert_down, e_w_in, e_w_out,
                    a_q_norm, a_k_norm, b_conv_w, b_conv_b, o_w_in, o_w_out, c_mu, c_w0, c_w_up, c_a0, c_a_up,
                    c_g_up, c_k_k, c_k_a, c_r_k, c_ln_w, c_ln_b, d_q_norm, d_k_norm, d_cmp_pe, d_cmp_w)
```

```python
import functools
import math

import jax
import jax.numpy as jnp
from jax import lax
from jax.experimental import pallas as pl
from jax.experimental.pallas import tpu as pltpu

F32 = jnp.float32
BF16 = jnp.bfloat16
I32 = jnp.int32

LANE = 128
HEAD_DIM = 64
N_ATT_HEADS = 8
N_KV_HEADS = 2
GQA = N_ATT_HEADS // N_KV_HEADS
IDX_HEADS = 4
IDX_DIM = 64
TOPK_MAX = 256
CONV_CH = 512
RWKV_HEADS = 8
RWKV_DIM = RWKV_HEADS * HEAD_DIM
LORA_W = 64
LORA_A = 64
LORA_G = 128
GN_EPS = 64e-5
CMP_BLOCK = 32
SEL_BLOCK = 64
N_SEL_BLOCKS = 8
WINDOW = 512
N_BUCKETS = 32
MAX_DISTANCE = 128
N_EXPERTS = 16
N_GROUPS = 4
EXPERTS_PER_GROUP = N_EXPERTS // N_GROUPS
D_EXPERT = 512
PAGE_SIZE = 128
RMS_EPS = 1e-6
NEG = -1e30
FORCE = 1e9
TAKEN = -3e38
ATT_SCALE = HEAD_DIM ** -0.5
VMEM_LIMIT = 56 * 1024 * 1024
ROW_TILE = 256
MOE_TILE = 512
SCAN_TIME_CHUNK = 32
SAMPLE_GROUP = 4


def _cparams(*sem):
    return pltpu.CompilerParams(dimension_semantics=sem, vmem_limit_bytes=VMEM_LIMIT)


def _pick_tile(rows, pref):
    t = min(pref, rows)
    while rows % t or (t % 8 and t != rows):
        t -= 1
    return t


def _pick_group(n, pref):
    g = min(pref, n)
    while n % g:
        g -= 1
    return g


def _const_spec(a):
    nd = a.ndim
    return pl.BlockSpec(a.shape, lambda *_: (0,) * nd)


def _dot(a, b):
    return jnp.dot(a.astype(BF16), b.astype(BF16), preferred_element_type=F32)


def _dot_nt(a, b):
    return lax.dot_general(a.astype(BF16), b.astype(BF16), (((1,), (1,)), ((), ())),
                           preferred_element_type=F32)


def _dot_split(x, m):
    hi = x.astype(BF16)
    r1 = x - hi.astype(F32)
    mid = r1.astype(BF16)
    lo = (r1 - mid.astype(F32)).astype(BF16)
    return (jnp.dot(hi, m, preferred_element_type=F32) + jnp.dot(mid, m, preferred_element_type=F32)
            + jnp.dot(lo, m, preferred_element_type=F32))


def _bf16_round(x):
    return x.astype(BF16).astype(F32)


def _sigmoid(x):
    return 1.0 / (1.0 + jnp.exp(-x))


def _silu(x):
    return x * _sigmoid(x)


def _modulate(x, g, shift, scale):
    y = x * lax.rsqrt(jnp.mean(x * x, axis=-1, keepdims=True) + RMS_EPS)
    return (y * g) * (1.0 + scale) + shift


def _group_rms(t, gmat, gain):
    ms = _dot_split(t * t, gmat)
    return (t * lax.rsqrt(ms + RMS_EPS)) * gain


def _ada_kernel(c_ref, w_ref, b_ref, o_ref):
    o_ref[...] = _dot(_silu(c_ref[...]), w_ref[...]) + b_ref[...]


def _ada(c, w_bf, b):
    r, d = c.shape
    n = w_bf.shape[1]
    tn = 512
    return pl.pallas_call(
        _ada_kernel,
        grid=(n // tn,),
        in_specs=[pl.BlockSpec((r, d), lambda j: (0, 0)),
                  pl.BlockSpec((d, tn), lambda j: (0, j)),
                  pl.BlockSpec((1, tn), lambda j: (0, j))],
        out_specs=pl.BlockSpec((r, tn), lambda j: (0, j)),
        out_shape=jax.ShapeDtypeStruct((r, n), F32),
        compiler_params=_cparams("parallel"),
        name="ada_mod",
    )(c, w_bf, b.reshape(1, n))


E_Q0, E_KV0, E_QI0, E_MISC0, E_BG0, E_CG0, E_XIN0, E_END = 0, 1024, 1280, 1792, 1920, 2432, 2944, 3456


def _inproj_even_kernel(x_ref, shift_ref, scale_ref, g_ref, w_ref, qg_ref, kg_ref, gone_ref, gtwo_ref,
                        q_ref, kv_ref, kvb_ref, qi_ref, misc_ref, miscb_ref, bg_ref, u_ref):
    h = _modulate(x_ref[...], g_ref[...], shift_ref[...], scale_ref[...])
    z = jnp.dot(h.astype(BF16), w_ref[...], preferred_element_type=F32)
    gone = gone_ref[...]
    for t in range(N_ATT_HEADS):
        sl = slice(t * LANE, (t + 1) * LANE)
        q_ref[:, sl] = _group_rms(z[:, E_Q0 + t * LANE:E_Q0 + (t + 1) * LANE], gone, qg_ref[:, sl]).astype(BF16)
    k = _group_rms(z[:, E_KV0:E_KV0 + LANE], gtwo_ref[...], kg_ref[...])
    v = z[:, E_KV0 + LANE:E_KV0 + 2 * LANE]
    kv_ref[:, 0:LANE] = k
    kv_ref[:, LANE:2 * LANE] = v
    kvb_ref[:, 0:LANE] = k.astype(BF16)
    kvb_ref[:, LANE:2 * LANE] = v.astype(BF16)
    qi_ref[...] = z[:, E_QI0:E_MISC0].astype(BF16)
    misc = z[:, E_MISC0:E_BG0]
    misc_ref[...] = misc
    miscb_ref[...] = misc.astype(BF16)
    bg_ref[...] = z[:, E_BG0:E_CG0]
    u_ref[...] = z[:, E_CG0:E_XIN0] * z[:, E_XIN0:E_END]


def _row_call(kernel, xs, mods, consts, outs, tile, tpb, name, outs_t=()):
    if not isinstance(xs, (list, tuple)):
        xs = [xs]
    rows = xs[0].shape[0]
    n_tiles = rows // tile
    in_specs = [pl.BlockSpec((tile, x.shape[1]), lambda t: (t, 0)) for x in xs]
    for m in mods:
        in_specs.append(pl.BlockSpec((None,) + m.shape[1:], lambda t: (t // tpb, 0, 0)))
    in_specs += [_const_spec(c) for c in consts]
    out_specs = [pl.BlockSpec((tile, w), lambda t: (t, 0)) for (w, _) in outs]
    out_shape = [jax.ShapeDtypeStruct((rows, w), dt) for (w, dt) in outs]
    out_specs += [pl.BlockSpec((hh, tile), lambda t: (0, t)) for (hh, _) in outs_t]
    out_shape += [jax.ShapeDtypeStruct((hh, rows), dt) for (hh, dt) in outs_t]
    return pl.pallas_call(kernel, grid=(n_tiles,), in_specs=in_specs, out_specs=out_specs, out_shape=out_shape,
                          compiler_params=_cparams("parallel"), name=name)(*xs, *mods, *consts)


def _pad_q_cols(wq):
    d = wq.shape[0]
    w = wq.reshape(d, N_ATT_HEADS, HEAD_DIM)
    z = jnp.zeros_like(w)
    lo = jnp.concatenate([w, z], -1)
    hi = jnp.concatenate([z, w], -1)
    sel = (jnp.arange(N_ATT_HEADS) >= GQA)[None, :, None]
    return jnp.where(sel, hi, lo).reshape(d, N_ATT_HEADS * LANE)


def _pad_o_rows(wo):
    return _pad_q_cols(wo.T).T


def _gmats():
    i = jnp.arange(LANE)
    gone = jnp.full((LANE, LANE), 1.0 / HEAD_DIM, F32).astype(BF16)
    gtwo = jnp.where((i[:, None] // HEAD_DIM) == (i[None, :] // HEAD_DIM), 1.0 / HEAD_DIM, 0.0).astype(BF16)
    tri = jnp.where(i[:, None] <= i[None, :], 1.0, 0.0).astype(BF16)
    return gone, gtwo, tri


def _even_weights(w_in, w_out, q_norm, k_norm):
    d = w_in.shape[0]
    a_q, a_kv = N_ATT_HEADS * HEAD_DIM, 2 * N_KV_HEADS * HEAD_DIM
    o = 0
    wq = w_in[:, o:o + a_q]; o += a_q
    wkv = w_in[:, o:o + a_kv]; o += a_kv
    wqi = w_in[:, o:o + IDX_HEADS * IDX_DIM]; o += IDX_HEADS * IDX_DIM
    wki = w_in[:, o:o + IDX_DIM]; o += IDX_DIM
    wwi = w_in[:, o:o + IDX_HEADS]; o += IDX_HEADS
    wrest = w_in[:, o:]
    wqi = jnp.concatenate([wqi.reshape(d, IDX_HEADS, IDX_DIM), jnp.zeros((d, IDX_HEADS, LANE - IDX_DIM), F32)],
                          -1).reshape(d, IDX_HEADS * LANE)
    wmisc = jnp.concatenate([wki, wwi, jnp.zeros((d, LANE - IDX_DIM - IDX_HEADS), F32)], -1)
    w_in_p = jnp.concatenate([_pad_q_cols(wq), wkv, wqi, wmisc, wrest], -1).astype(BF16)
    w_out_p = jnp.concatenate([_pad_o_rows(w_out[:a_q]), w_out[a_q:]], 0).astype(BF16)
    qg = jnp.tile(q_norm, 2 * N_ATT_HEADS).reshape(1, N_ATT_HEADS * LANE)
    kg = jnp.tile(k_norm, 2).reshape(1, LANE)
    return w_in_p, w_out_p, qg, kg


def _inproj_even(x, mods, g, wts, gm, tile, tpb):
    w_in_p, _, qg, kg = wts
    gone, gtwo, _ = gm
    outs = [(N_ATT_HEADS * LANE, BF16), (2 * LANE, F32), (2 * LANE, BF16), (IDX_HEADS * LANE, BF16),
            (LANE, F32), (LANE, BF16), (CONV_CH, F32), (CONV_CH, F32)]
    return _row_call(_inproj_even_kernel, x, mods, [g, w_in_p, qg, kg, gone, gtwo], outs, tile, tpb, "inproj_even")


def _t5_bucket(dist):
    dist = jnp.maximum(dist, 0)
    exact = N_BUCKETS // 2
    far = exact + (jnp.log(jnp.maximum(dist, 1).astype(F32) / exact)
                   / math.log(MAX_DISTANCE / exact) * (N_BUCKETS - exact)).astype(I32)
    return jnp.where(dist < exact, dist, jnp.minimum(far, N_BUCKETS - 1))


def _bias_lookup(rel_bias, dist):
    onehot = (_t5_bucket(dist)[..., None] == jnp.arange(N_BUCKETS)).astype(F32)
    return jnp.einsum("...k,kh->...h", onehot, rel_bias, precision=lax.Precision.HIGHEST)


def _bias_row(rel_bias, t_pos):
    return _bias_lookup(rel_bias, t_pos - jnp.arange(t_pos + LANE)).T


def _bias_tiles(rel_bias, qb):
    r = jnp.arange(qb)[:, None]
    c = jnp.arange(LANE)[None, :]
    tiles = [_bias_lookup(rel_bias, d * LANE + r - c) for d in range(3)]
    return jnp.stack(tiles).transpose(0, 3, 1, 2)


def _flash_step(qs, kb, vb, bias, msk, m, l, acc):
    s = _dot_nt(qs, kb) * ATT_SCALE + bias
    s = jnp.where(msk, s, NEG)
    m_new = jnp.maximum(m, jnp.max(s, axis=1, keepdims=True))
    p = jnp.where(msk, jnp.exp(s - m_new), 0.0)
    alpha = jnp.exp(m - m_new)
    l = alpha * l + jnp.sum(p, axis=1, keepdims=True)
    acc = alpha * acc + jnp.dot(p.astype(BF16), vb, preferred_element_type=F32)
    return m_new, l, acc


def _flash_init(rows):
    return (jnp.full((rows, 1), NEG, F32), jnp.zeros((rows, 1), F32), jnp.zeros((rows, LANE), F32))


def _stack_heads(q_ref, hk):
    return jnp.concatenate([q_ref[:, (hk * GQA + g) * LANE:(hk * GQA + g + 1) * LANE] for g in range(GQA)], axis=0)


def _tile4(x):
    return jnp.concatenate([x] * GQA, axis=0)


def _loop(lo, hi, body, init, unroll):
    if unroll:
        for j in range(lo, hi):
            init = body(j, init)
        return init
    return lax.fori_loop(lo, hi, body, init)


def _dsa_kernel(q_ref, qi_ref, misc_ref, kidx_ref, kv_ref, bias_ref, tri_ref, o_ref, key_s,
                *, grp, qb, q_base, n_keep, single):
    q0 = q_base if single else q_base + pl.program_id(1) * qb
    nblk = (q0 + qb - 1) // LANE + 1
    dq = q0 // LANE
    seqs = range(grp)
    row = lax.broadcasted_iota(I32, (qb, LANE), 0)
    lane = lax.broadcasted_iota(I32, (qb, LANE), 1)
    t_pos = q0 + row
    wis = [[jnp.broadcast_to(misc_ref[g][:, IDX_DIM + h:IDX_DIM + h + 1], (qb, LANE)) for h in range(IDX_HEADS)]
           for g in seqs]
    idx_scale = (IDX_HEADS * IDX_DIM) ** -0.5

    def pass_a(j, c):
        off = pl.multiple_of(j * LANE, LANE)
        causal = j * LANE + lane <= t_pos
        for g in seqs:
            kb = kidx_ref[g, pl.ds(off, LANE), :]
            acc = jnp.zeros((qb, LANE), F32)
            for h in range(IDX_HEADS):
                acc = acc + jnp.maximum(_dot_nt(qi_ref[g, :, h * LANE:(h + 1) * LANE], kb), 0.0) * wis[g][h]
            sc = jnp.where(causal, acc * idx_scale, NEG)
            sc = jnp.where(sc == 0.0, 0.0, sc)
            bits = lax.bitcast_convert_type(sc, I32)
            key_s[g, j] = jnp.where(bits < 0, bits ^ jnp.int32(0x7FFFFFFF), bits)
        return c

    lax.fori_loop(0, nblk, pass_a, 0)

    def count(pred):
        def body(j, accs):
            return tuple(a + jnp.where(pred(g, key_s[g, j]), 1.0, 0.0) for g, a in zip(seqs, accs))
        accs = _loop(0, nblk, body, tuple(jnp.zeros((qb, LANE), F32) for _ in seqs), single)
        return [jnp.sum(a, axis=1, keepdims=True) for a in accs]

    keep = float(n_keep)
    int_min = jnp.int32(-2 ** 31)
    thr = tuple(jnp.where(c >= keep, jnp.int32(0), int_min) for c in count(lambda g, k: k >= 0))

    def search(it, thr):
        bit = lax.shift_left(jnp.int32(1), jnp.int32(30) - it)
        cand = [t | bit for t in thr]
        cnt = count(lambda g, k: k >= cand[g])
        return tuple(jnp.where(c >= keep, cd, t) for c, cd, t in zip(cnt, cand, thr))

    thr = lax.fori_loop(0, 31, search, thr)
    need = [keep - c for c in count(lambda g, k: k > thr[g])]
    tri = tri_ref[...]

    def pass_c(j, run):
        causal = j * LANE + lane <= t_pos
        out = []
        for g in seqs:
            key = key_s[g, j]
            eq = key == thr[g]
            eqf = jnp.where(eq, 1.0, 0.0)
            cum = jnp.dot(eqf.astype(BF16), tri, preferred_element_type=F32) + run[g]
            sel = (key > thr[g]) | (eq & (cum <= need[g]))
            key_s[g, j] = jnp.where(sel & causal, 1, 0)
            out.append(run[g] + jnp.sum(eqf, axis=1, keepdims=True))
        return tuple(out)

    lax.fori_loop(0, nblk, pass_c, tuple(jnp.zeros((qb, 1), F32) for _ in seqs))

    qs = [[_stack_heads(q_ref.at[g], hk) for hk in range(N_KV_HEADS)] for g in seqs]

    def pass_d(j, carry):
        off = pl.multiple_of(j * LANE, LANE)
        dsel = jnp.minimum(dq - j, 2)
        biases = [bias_ref[dsel, hk * GQA:(hk + 1) * GQA].reshape(GQA * qb, LANE) for hk in range(N_KV_HEADS)]
        out = []
        for g in seqs:
            kb = kv_ref[g, pl.ds(off, LANE), 0:LANE]
            vb = kv_ref[g, pl.ds(off, LANE), LANE:2 * LANE]
            msk = _tile4(key_s[g, j] > 0)
            out.append(tuple(_flash_step(qs[g][hk], kb, vb, biases[hk], msk, *carry[g][hk])
                             for hk in range(N_KV_HEADS)))
        return tuple(out)

    init = tuple(tuple(_flash_init(GQA * qb) for _ in range(N_KV_HEADS)) for _ in seqs)
    res = lax.fori_loop(0, nblk, pass_d, init)
    for g in seqs:
        _write_heads(o_ref.at[g], [acc / l for (_, l, acc) in res[g]], qb)


def _write_heads(o_ref, outs, qb):
    lane = lax.broadcasted_iota(I32, (qb, LANE), 1)
    for hk in range(N_KV_HEADS):
        valid = (lane // HEAD_DIM) == hk
        for g in range(GQA):
            h = hk * GQA + g
            o_ref[:, h * LANE:(h + 1) * LANE] = jnp.where(valid, outs[hk][g * qb:(g + 1) * qb], 0.0).astype(BF16)


def _dsa(q, qi, misc, kidx_b, kv_b, bias, tri, *, qb, q_base, n_keep, grp):
    b, rows, _ = q.shape
    nq = rows // qb
    s = kv_b.shape[1]
    kern = functools.partial(_dsa_kernel, grp=grp, qb=qb, q_base=q_base, n_keep=n_keep, single=(nq == 1))
    qspec = lambda w: pl.BlockSpec((grp, qb, w), lambda bi, i: (bi, i, 0))
    kspec = lambda w: pl.BlockSpec((grp, s, w), lambda bi, i: (bi, 0, 0))
    return pl.pallas_call(
        kern,
        grid=(b // grp, nq),
        in_specs=[qspec(N_ATT_HEADS * LANE), qspec(IDX_HEADS * LANE), qspec(LANE), kspec(LANE), kspec(2 * LANE),
                  _const_spec(bias), _const_spec(tri)],
        out_specs=qspec(N_ATT_HEADS * LANE),
        out_shape=jax.ShapeDtypeStruct((b, rows, N_ATT_HEADS * LANE), BF16),
        scratch_shapes=[pltpu.VMEM((grp, s // LANE, qb, LANE), I32)],
        compiler_params=_cparams("parallel", "parallel"),
        name="dsa_attention",
    )(q, qi, misc, kidx_b, kv_b, bias, tri)


QB = LANE
QW = GQA * QB


def _sub_sum(x):
    return jnp.sum(x, axis=0, keepdims=True)


def _flash_t_pair(blocks, qs, bias_ref, carry, acc_ref):
    logits = [[jnp.where(mk[hk], _dot_nt(kb, qs[hk]) + bias_ref[dsel, hk], NEG) for (kb, _, mk, dsel) in blocks]
              for hk in range(N_KV_HEADS)]
    out = []
    for hk in range(N_KV_HEADS):
        m, l = carry[hk]
        m_new = m
        for s in logits[hk]:
            m_new = jnp.maximum(m_new, jnp.max(s, axis=0, keepdims=True))
        alpha = jnp.exp(m - m_new)
        l = alpha * l
        pv = None
        for s, (_, vt, _, _) in zip(logits[hk], blocks):
            p = jnp.exp(s - m_new)
            l = l + _sub_sum(p)
            d = jnp.dot(vt, p.astype(BF16), preferred_element_type=F32)
            pv = d if pv is None else pv + d
        acc_ref[hk] = alpha * acc_ref[hk] + pv
        out.append((m_new, l))
    return tuple(out)


def _scaled_queries(q_ref, hk):
    return (_stack_heads(q_ref, hk).astype(F32) * ATT_SCALE).astype(BF16)


def _pair_loop(nblk, body, init):
    def body2(jj, c):
        return body(2 * jj + 1, body(2 * jj, c))
    return lax.fori_loop(0, (nblk + 1) // 2, body2, init)


def _flash_t_init():
    return (jnp.full((1, QW), NEG, F32), jnp.zeros((1, QW), F32))


def _tile_lanes(x):
    return jnp.concatenate([x] * GQA, axis=1)


def _write_heads_t(o_ref, o_ts):
    lane = lax.broadcasted_iota(I32, (QB, LANE), 1)
    for hk in range(N_KV_HEADS):
        valid = (lane // HEAD_DIM) == hk
        for g in range(GQA):
            h = hk * GQA + g
            o = o_ts[hk][:, g * QB:(g + 1) * QB].T
            o_ref[:, h * LANE:(h + 1) * LANE] = jnp.where(valid, o, 0.0).astype(BF16)


def _dsa_t_kernel(q_ref, qi_ref, misc_ref, kidx_ref, k_ref, vt_ref, bias_ref, trit_ref, o_ref, key_s, acc_s,
                  *, n_keep):
    i = pl.program_id(1)
    q0 = i * QB
    nblk = i + 1
    krow = lax.broadcasted_iota(I32, (LANE, QB), 0)
    qcol = lax.broadcasted_iota(I32, (LANE, QB), 1)
    misc_t = misc_ref[...].T
    wis = [_bf16_round(misc_t[IDX_DIM + h:IDX_DIM + h + 1, :]) for h in range(IDX_HEADS)]
    qi = qi_ref[...]
    idx_scale = (IDX_HEADS * IDX_DIM) ** -0.5

    def causal(j):
        return j * LANE + krow <= q0 + qcol

    def pass_a(j, c):
        kb = kidx_ref[pl.ds(pl.multiple_of(j * LANE, LANE), LANE), :]
        acc = jnp.zeros((LANE, QB), F32)
        for h in range(IDX_HEADS):
            acc = acc + _bf16_round(jnp.maximum(_dot_nt(kb, qi[:, h * LANE:(h + 1) * LANE]), 0.0)) * wis[h]
        key_s[j] = _order_keys(jnp.where(causal(j), acc * idx_scale, NEG))
        return c

    _pair_loop(nblk, pass_a, 0)

    def count(pred):
        def body(j, a):
            return a + jnp.where(pred(key_s[j]), 1.0, 0.0)
        return _sub_sum(_pair_loop(nblk, body, jnp.zeros((LANE, QB), F32)))

    keep = float(n_keep)
    thr = jnp.where(count(lambda k: k >= 0) >= keep, jnp.int32(0), jnp.int32(-2 ** 31))

    def search(it, thr):
        cand = thr | lax.shift_left(jnp.int32(1), jnp.int32(30) - it)
        return jnp.where(count(lambda k: k >= cand) >= keep, cand, thr)

    thr = lax.fori_loop(0, 31, search, thr)
    need = keep - count(lambda k: k > thr)
    trit = trit_ref[...]

    def pass_c(j, run):
        key = key_s[j]
        eq = key == thr
        eqf = jnp.where(eq, 1.0, 0.0)
        cum = jnp.dot(trit, eqf.astype(BF16), preferred_element_type=F32) + run
        sel = ((key > thr) | (eq & (cum <= need))) & causal(j)
        key_s[j] = jnp.where(sel, 1, 0)
        return run + _sub_sum(eqf)

    _pair_loop(nblk, pass_c, jnp.zeros((1, QB), F32))

    qs = [_scaled_queries(q_ref, hk) for hk in range(N_KV_HEADS)]
    acc_s[...] = jnp.zeros_like(acc_s)

    def pass_d(jj, carry):
        blocks = []
        for j in (2 * jj, 2 * jj + 1):
            kb = k_ref[pl.ds(pl.multiple_of(j * LANE, LANE), LANE), :]
            blocks.append((kb, vt_ref[j], [_tile_lanes(key_s[j] > 0)] * N_KV_HEADS, jnp.clip(i - j, 0, 2)))
        return _flash_t_pair(blocks, qs, bias_ref, carry, acc_s)

    res = lax.fori_loop(0, (nblk + 1) // 2, pass_d, tuple(_flash_t_init() for _ in range(N_KV_HEADS)))
    _write_heads_t(o_ref, [acc_s[hk] / res[hk][1] for hk in range(N_KV_HEADS)])


def _bias_tiles_t(rel_bias):
    t = _bias_tiles(rel_bias, QB)
    t = t.reshape(3, N_KV_HEADS, GQA, QB, LANE).transpose(0, 1, 4, 2, 3)
    return t.reshape(3, N_KV_HEADS, LANE, QW)


def _blocks_t(x):
    b, s, w = x.shape
    return x.reshape(b, s // LANE, LANE, w).transpose(0, 1, 3, 2)


def _dsa_t(q, qi, misc, kidx_b, kv_b, bias_t, trit, *, n_keep):
    b, s, _ = q.shape
    vt = _blocks_t(kv_b[:, :, LANE:])
    qspec = lambda w: pl.BlockSpec((None, QB, w), lambda bi, i: (bi, i, 0))
    kspec = pl.BlockSpec((None, s, LANE), lambda bi, i: (bi, 0, 0))
    return pl.pallas_call(
        functools.partial(_dsa_t_kernel, n_keep=n_keep),
        grid=(b, s // QB),
        in_specs=[qspec(N_ATT_HEADS * LANE), qspec(IDX_HEADS * LANE), qspec(LANE), kspec, kspec,
                  pl.BlockSpec((None, s // LANE, LANE, LANE), lambda bi, i: (bi, 0, 0, 0)),
                  _const_spec(bias_t), _const_spec(trit)],
        out_specs=qspec(N_ATT_HEADS * LANE),
        out_shape=jax.ShapeDtypeStruct((b, s, N_ATT_HEADS * LANE), BF16),
        scratch_shapes=[pltpu.VMEM((s // LANE, LANE, QB), I32), pltpu.VMEM((N_KV_HEADS, LANE, QW), F32)],
        compiler_params=_cparams("parallel", "parallel"),
        name="dsa_attention_t",
    )(q, qi, misc, kidx_b, kv_b, vt, bias_t, trit)


def _dot_split_rhs(m, x):
    hi = x.astype(BF16)
    r1 = x - hi.astype(F32)
    mid = r1.astype(BF16)
    lo = (r1 - mid.astype(F32)).astype(BF16)
    return (jnp.dot(m, hi, preferred_element_type=F32) + jnp.dot(m, mid, preferred_element_type=F32)
            + jnp.dot(m, lo, preferred_element_type=F32))


def _nsa_t_kernel(q_ref, gates_ref, kc_ref, vct_ref, ks_ref, vst_ref, kw_ref, vwt_ref, bias_ref, biasc_ref,
                  pairt_ref, o_ref, acc_s, *, n_cmp, n_sel):
    i = pl.program_id(1)
    q0 = i * QB
    nblk = i + 1
    heads = range(N_KV_HEADS)
    krow = lax.broadcasted_iota(I32, (LANE, QB), 0)
    qcol = lax.broadcasted_iota(I32, (LANE, QB), 1)
    t_pos = q0 + qcol
    krow_f = krow.astype(F32)
    qs = [_scaled_queries(q_ref, hk) for hk in heads]
    gates_t = gates_ref[...].T

    def gate_row(br, hk):
        return jnp.concatenate([gates_t[br * N_ATT_HEADS + hk * GQA + g:br * N_ATT_HEADS + hk * GQA + g + 1, :]
                                for g in range(GQA)], axis=1)

    mask_c = _tile_lanes(((krow * CMP_BLOCK + CMP_BLOCK - 1) <= t_pos) & (krow < n_cmp))
    cur = t_pos // SEL_BLOCK
    forced = (krow == 0) | (krow == cur) | (krow == cur - 1)
    sel_causal = krow * SEL_BLOCK <= t_pos
    o_cmp, picked = [], []
    for hk in heads:
        s = jnp.where(mask_c, _dot_nt(kc_ref[...], qs[hk]) + biasc_ref[hk], NEG)
        e = jnp.exp(s - jnp.max(s, axis=0, keepdims=True))
        p = jnp.where(mask_c, e / _sub_sum(e), 0.0)
        o_cmp.append(jnp.dot(vct_ref[...], p.astype(BF16), preferred_element_type=F32))
        ps = p[:, 0:QB]
        for g in range(1, GQA):
            ps = ps + p[:, g * QB:(g + 1) * QB]
        score = _dot_split_rhs(pairt_ref[...], ps)
        score = jnp.where(sel_causal, jnp.where(forced, FORCE, score), NEG)
        pk = jnp.zeros((LANE, QB), F32)
        for _ in range(n_sel):
            mx = jnp.max(score, axis=0, keepdims=True)
            first = jnp.min(jnp.where(score == mx, krow_f, float(LANE)), axis=0, keepdims=True)
            hit = krow_f == first
            pk = jnp.where(hit, 1.0, pk)
            score = jnp.where(hit, TAKEN, score)
        picked.append(pk.astype(BF16))

    def key_block(k_ref, vt_ref, j):
        return k_ref[pl.ds(pl.multiple_of(j * LANE, LANE), LANE), :], vt_ref[j]

    erow = lax.broadcasted_iota(I32, (LANE, LANE), 0)
    ecol = lax.broadcasted_iota(I32, (LANE, LANE), 1)
    acc_s[...] = jnp.zeros_like(acc_s)

    def slc_body(jj, carry):
        blocks = []
        for j in (2 * jj, 2 * jj + 1):
            kb, vt = key_block(ks_ref, vst_ref, j)
            expand = jnp.where(ecol == 2 * j + erow // SEL_BLOCK, 1.0, 0.0).astype(BF16)
            causal = j * LANE + krow <= t_pos
            masks = [_tile_lanes((jnp.dot(expand, picked[hk], preferred_element_type=F32) > 0.5) & causal)
                     for hk in heads]
            blocks.append((kb, vt, masks, jnp.clip(i - j, 0, 2)))
        return _flash_t_pair(blocks, qs, bias_ref, carry, acc_s.at[0])

    res_s = lax.fori_loop(0, (nblk + 1) // 2, slc_body, tuple(_flash_t_init() for _ in heads))

    lo = jnp.maximum(i - WINDOW // LANE - 1, 0)

    def win_body(jj, carry):
        blocks = []
        for j in (lo + 2 * jj, lo + 2 * jj + 1):
            kb, vt = key_block(kw_ref, vwt_ref, j)
            dist = t_pos - (j * LANE + krow)
            mask = _tile_lanes((dist >= 0) & (dist < WINDOW))
            blocks.append((kb, vt, [mask] * N_KV_HEADS, jnp.clip(i - j, 0, 2)))
        return _flash_t_pair(blocks, qs, bias_ref, carry, acc_s.at[1])

    res_w = lax.fori_loop(0, (i - lo + 2) // 2, win_body, tuple(_flash_t_init() for _ in heads))

    _write_heads_t(o_ref, [gate_row(0, hk) * o_cmp[hk] + gate_row(1, hk) * (acc_s[0, hk] / res_s[hk][1])
                           + gate_row(2, hk) * (acc_s[1, hk] / res_w[hk][1]) for hk in heads])


def _bias_cmp_t(rel_bias, s):
    t = _bias_cmp(rel_bias, [j * QB for j in range(s // QB)], QB)
    t = t.reshape(s // QB, N_KV_HEADS, GQA, QB, LANE).transpose(0, 1, 4, 2, 3)
    return t.reshape(s // QB, N_KV_HEADS, LANE, QW)


def _nsa_t(q, gates, kcv, kvs_b, kvw_b, bias_t, bias_c, pair_t, *, n_cmp, n_sel):
    b, s, _ = q.shape
    vct = kcv[:, :, LANE:].transpose(0, 2, 1)
    qspec = lambda w: pl.BlockSpec((None, QB, w), lambda bi, i: (bi, i, 0))
    kspec = pl.BlockSpec((None, s, LANE), lambda bi, i: (bi, 0, 0))
    vspec = pl.BlockSpec((None, s // LANE, LANE, LANE), lambda bi, i: (bi, 0, 0, 0))
    cspec = pl.BlockSpec((None, LANE, LANE), lambda bi, i: (bi, 0, 0))
    return pl.pallas_call(
        functools.partial(_nsa_t_kernel, n_cmp=n_cmp, n_sel=n_sel),
        grid=(b, s // QB),
        in_specs=[qspec(N_ATT_HEADS * LANE), qspec(LANE), cspec, cspec, kspec, vspec, kspec, vspec,
                  _const_spec(bias_t), pl.BlockSpec((None,) + bias_c.shape[1:], lambda bi, i: (i, 0, 0, 0)),
                  _const_spec(pair_t)],
        out_specs=qspec(N_ATT_HEADS * LANE),
        out_shape=jax.ShapeDtypeStruct((b, s, N_ATT_HEADS * LANE), BF16),
        scratch_shapes=[pltpu.VMEM((2, N_KV_HEADS, LANE, QW), F32)],
        compiler_params=_cparams("parallel", "parallel"),
        name="nsa_attention_t",
    )(q, gates, kcv, vct, kvs_b, _blocks_t(kvs_b[:, :, LANE:]), kvw_b, _blocks_t(kvw_b[:, :, LANE:]),
      bias_t, bias_c, pair_t)


def _select_top(keys, n_keep, tri):
    keep = float(n_keep)
    n = keys[0].shape[1]

    def count(pred):
        return [jnp.sum(jnp.where(pred(g, k), 1.0, 0.0), axis=1, keepdims=True) for g, k in enumerate(keys)]

    int_min = jnp.int32(-2 ** 31)
    thr = tuple(jnp.where(c >= keep, jnp.int32(0), int_min) for c in count(lambda g, k: k >= 0))

    def search(it, thr):
        bit = lax.shift_left(jnp.int32(1), jnp.int32(30) - it)
        cand = [t | bit for t in thr]
        cnt = count(lambda g, k: k >= cand[g])
        return tuple(jnp.where(c >= keep, cd, t) for c, cd, t in zip(cnt, cand, thr))

    thr = lax.fori_loop(0, 31, search, thr)
    need = [keep - c for c in count(lambda g, k: k > thr[g])]
    sels = []
    for g, k in enumerate(keys):
        run = jnp.zeros((1, 1), F32)
        parts = []
        for t in range(n // LANE):
            kt = k[:, t * LANE:(t + 1) * LANE]
            eq = kt == thr[g]
            eqf = jnp.where(eq, 1.0, 0.0)
            cum = jnp.dot(eqf.astype(BF16), tri, preferred_element_type=F32) + run
            parts.append((kt > thr[g]) | (eq & (cum <= need[g])))
            run = run + jnp.sum(eqf, axis=1, keepdims=True)
        sels.append(jnp.concatenate(parts, axis=1))
    return sels


def _order_keys(score):
    score = jnp.where(score == 0.0, 0.0, score)
    bits = lax.bitcast_convert_type(score, I32)
    return jnp.where(bits < 0, bits ^ jnp.int32(0x7FFFFFFF), bits)


def _head_rows_out(o_ref, g, acc):
    rowh = lax.broadcasted_iota(I32, (N_ATT_HEADS, LANE), 0)
    laneh = lax.broadcasted_iota(I32, (N_ATT_HEADS, LANE), 1)
    o_ref[g] = jnp.where((laneh // HEAD_DIM) == (rowh // GQA), acc, 0.0).astype(BF16)


def _dsa_sample_kernel(pt_ref, q_ref, qi_ref, wi_ref, knew_ref, kvnew_ref, bias_ref, tri_ref, *refs,
                       grp, n_pages, n_keep):
    del pt_ref
    ki_refs, kv_refs, o_ref = refs[:grp * n_pages], refs[grp * n_pages:2 * grp * n_pages], refs[-1]
    lane1 = lax.broadcasted_iota(I32, (1, LANE), 1)
    idx_scale = (IDX_HEADS * IDX_DIM) ** -0.5
    keys = []
    for g in range(grp):
        qi = qi_ref[g]
        wi = _bf16_round(wi_ref[g])
        tiles = []
        for p in range(n_pages):
            rel = _bf16_round(jnp.maximum(_dot(qi, ki_refs[g * n_pages + p][...]), 0.0))
            tiles.append(jnp.sum(rel * wi, axis=0, keepdims=True) * idx_scale)
        rel_new = _bf16_round(jnp.maximum(jnp.sum(qi.astype(F32) * _bf16_round(knew_ref[g]), axis=1, keepdims=True),
                                          0.0))
        sc_new = jnp.sum(rel_new * wi, axis=0, keepdims=True) * idx_scale
        tiles.append(jnp.where(lane1 == 0, sc_new, NEG))
        keys.append(_order_keys(jnp.concatenate(tiles, axis=1)))
    sels = _select_top(keys, n_keep, tri_ref[...])
    bias = bias_ref[...]
    for g in range(grp):
        q = q_ref[g]
        kvnew = _bf16_round(kvnew_ref[g])
        tiles = [_dot(q, kv_refs[g * n_pages + p][0:LANE, :]) for p in range(n_pages)]
        s_new = jnp.sum(q.astype(F32) * kvnew[:, 0:LANE], axis=1, keepdims=True)
        tiles.append(jnp.where(lane1 == 0, s_new, 0.0))
        valid = sels[g] & (jnp.concatenate([lane1] * n_pages + [lane1 + LANE], axis=1) <= LANE)
        s = jnp.where(valid, jnp.concatenate(tiles, axis=1) * ATT_SCALE + bias, NEG)
        e = jnp.exp(s - jnp.max(s, axis=1, keepdims=True))
        p_all = jnp.where(valid, e / jnp.sum(e, axis=1, keepdims=True), 0.0)
        acc = p_all[:, n_pages * LANE:n_pages * LANE + 1] * kvnew[:, LANE:2 * LANE]
        for p in range(n_pages):
            acc = acc + _dot_nt(p_all[:, p * LANE:(p + 1) * LANE], kv_refs[g * n_pages + p][LANE:2 * LANE, :])
        _head_rows_out(o_ref, g, acc)


def _page_specs(pool_t, n_pages, grp):
    r, c = pool_t.shape[1:]
    return [pl.BlockSpec((None, r, c), lambda i, pt, g=g, p=p: (pt[i * grp + g, p], 0, 0))
            for g in range(grp) for p in range(n_pages)]


def _dsa_sample(q8, qi8, wi8, knew, kvnew, bias, tri, ki_t, kv_t, page_table, *, n_keep, grp):
    b, n_pages = page_table.shape
    gspec = lambda a: pl.BlockSpec((grp,) + a.shape[1:], lambda i, pt: (i,) + (0,) * (a.ndim - 1))
    cspec = lambda a: pl.BlockSpec(a.shape, lambda i, pt: (0,) * a.ndim)
    kern = functools.partial(_dsa_sample_kernel, grp=grp, n_pages=n_pages, n_keep=n_keep)
    return pl.pallas_call(
        kern,
        grid_spec=pltpu.PrefetchScalarGridSpec(
            num_scalar_prefetch=1, grid=(b // grp,),
            in_specs=[gspec(q8), gspec(qi8), gspec(wi8), gspec(knew), gspec(kvnew), cspec(bias), cspec(tri)]
            + _page_specs(ki_t, n_pages, grp) + _page_specs(kv_t, n_pages, grp),
            out_specs=pl.BlockSpec((grp, N_ATT_HEADS, LANE), lambda i, pt: (i, 0, 0))),
        out_shape=jax.ShapeDtypeStruct((b, N_ATT_HEADS, LANE), BF16),
        compiler_params=_cparams("parallel"),
        name="dsa_sample",
    )(page_table, q8, qi8, wi8, knew, kvnew, bias, tri, *([ki_t] * (grp * n_pages)), *([kv_t] * (grp * n_pages)))


def _route(hf, wrt, br):
    logits = _dot_nt(wrt, hf)
    s = _sigmoid(logits)
    sel = s + br
    rows = [sel[e:e + 1, :] for e in range(N_EXPERTS)]
    grp = []
    for g in range(N_GROUPS):
        a = rows[g * EXPERTS_PER_GROUP:(g + 1) * EXPERTS_PER_GROUP]
        best = None
        for i in range(EXPERTS_PER_GROUP):
            for j in range(i + 1, EXPERTS_PER_GROUP):
                v = a[i] + a[j]
                best = v if best is None else jnp.maximum(best, v)
        grp.append(best)
    gbest = jnp.zeros_like(grp[0], dtype=I32)
    cur = grp[0]
    for g in range(1, N_GROUPS):
        better = grp[g] > cur
        gbest = jnp.where(better, g, gbest)
        cur = jnp.where(better, grp[g], cur)
    picked = []
    for g in range(N_GROUPS):
        a = rows[g * EXPERTS_PER_GROUP:(g + 1) * EXPERTS_PER_GROUP]
        for j in range(EXPERTS_PER_GROUP):
            rank = jnp.zeros_like(a[j])
            for jj in range(EXPERTS_PER_GROUP):
                if jj != j:
                    ahead = (a[jj] > a[j]) | (a[jj] == a[j]) if jj < j else (a[jj] > a[j])
                    rank = rank + jnp.where(ahead, 1.0, 0.0)
            e = g * EXPERTS_PER_GROUP + j
            picked.append(jnp.where((gbest == g) & (rank < 2.0), s[e:e + 1, :], 0.0))
    den = picked[0]
    for p in picked[1:]:
        den = den + p
    return jnp.concatenate([p / den for p in picked], axis=0)


def _post_tail(x, mix, gate, gf, shf, scf, wrt_ref, br_ref, x2_ref, hf_ref, cwt_ref):
    x2 = x + gate * mix
    x2_ref[...] = x2
    hf = _modulate(x2, gf, shf, scf)
    hf_ref[...] = hf.astype(BF16)
    cwt_ref[...] = _route(hf, wrt_ref[...], br_ref[...])


def _post_even_kernel(x_ref, oa_ref, bg_ref, u_ref, um1_ref, um2_ref, gate_ref, shf_ref, scf_ref,
                      gf_ref, cw_ref, cb_ref, wo_ref, wrt_ref, br_ref, x2_ref, hf_ref, cwt_ref):
    cw = cw_ref[...]
    y = cb_ref[...] + cw[0:1, :] * um2_ref[...]
    y = y + cw[1:2, :] * um1_ref[...]
    y = y + cw[2:3, :] * u_ref[...]
    n_a = N_ATT_HEADS * LANE
    mix = (jnp.dot(oa_ref[...], wo_ref[0:n_a, :], preferred_element_type=F32)
           + jnp.dot((bg_ref[...] * y).astype(BF16), wo_ref[n_a:n_a + CONV_CH, :], preferred_element_type=F32))
    _post_tail(x_ref[...], mix, gate_ref[...], gf_ref[...], shf_ref[...], scf_ref[...], wrt_ref, br_ref,
               x2_ref, hf_ref, cwt_ref)


POST_OUTS = [(1024, F32), (1024, BF16)]


def _moe_kernel(hf_ref, cw_ref, x2_ref, gate_ref, wg_ref, wu_ref, wd_ref, o_ref, acc_ref):
    e = pl.program_id(1)

    @pl.when(e == 0)
    def _():
        acc_ref[...] = jnp.zeros_like(acc_ref)

    hf = hf_ref[...]
    hmid = _silu(jnp.dot(hf, wg_ref[...], preferred_element_type=F32)) * jnp.dot(hf, wu_ref[...],
                                                                                 preferred_element_type=F32)
    cw = cw_ref[...]
    lane = lax.broadcasted_iota(I32, cw.shape, 1)
    wcol = jnp.sum(jnp.where(lane == e, cw, 0.0), axis=1, keepdims=True)
    acc_ref[...] += jnp.dot(hmid.astype(BF16), wd_ref[...], preferred_element_type=F32) * wcol

    @pl.when(e == N_EXPERTS - 1)
    def _():
        o_ref[...] = x2_ref[...] + gate_ref[...] * acc_ref[...]


def _moe(hf, cw, x2, gate, wg, wu, wd, tile, tpb):
    rows, d = x2.shape
    de = wg.shape[2]
    return pl.pallas_call(
        _moe_kernel,
        grid=(rows // tile, N_EXPERTS),
        in_specs=[pl.BlockSpec((tile, d), lambda t, e: (t, 0)),
                  pl.BlockSpec((tile, N_EXPERTS), lambda t, e: (t, 0)),
                  pl.BlockSpec((tile, d), lambda t, e: (t, 0)),
                  pl.BlockSpec((None,) + gate.shape[1:], lambda t, e: (t // tpb, 0, 0)),
                  pl.BlockSpec((None, d, de), lambda t, e: (e, 0, 0)),
                  pl.BlockSpec((None, d, de), lambda t, e: (e, 0, 0)),
                  pl.BlockSpec((None, de, d), lambda t, e: (e, 0, 0))],
        out_specs=pl.BlockSpec((tile, d), lambda t, e: (t, 0)),
        out_shape=jax.ShapeDtypeStruct((rows, d), F32),
        scratch_shapes=[pltpu.VMEM((tile, d), F32)],
        compiler_params=_cparams("parallel", "arbitrary"),
        name="moe_dense",
    )(hf, cw, x2, gate, wg, wu, wd)


def _shift_rows(u3, k):
    return jnp.pad(u3, ((0, 0), (k, 0), (0, 0)))[:, :u3.shape[1]]


def _even_layer(xp, xs, mp, ms, page_table, kv_pool, kidx_pool, conv_buf, rel_bias, router, experts,
                g_mix, g_ffn, w_in, w_out, q_norm, k_norm, conv_w, conv_b):
    B, S, D = xp.shape
    Bs = xs.shape[0]
    past = page_table.shape[1] * PAGE_SIZE
    wts = _even_weights(w_in, w_out, q_norm, k_norm)
    gm = _gmats()
    wrt, br = router
    wg, wu, wd = experts
    g_mix = g_mix.reshape(1, D)
    g_ffn = g_ffn.reshape(1, D)
    post_consts = [g_ffn, conv_w, conv_b.reshape(1, CONV_CH), wts[1], wrt, br]

    tp = ROW_TILE
    q_p, kv_f, kv_b, qi_p, misc, misc_b, bg, u = _inproj_even(xp.reshape(B * S, D), [mp[0], mp[1]], g_mix, wts, gm,
                                                              tp, S // tp)
    r3 = lambda a: a.reshape(B, S, a.shape[-1])
    oa = _dsa_t(r3(q_p), r3(qi_p), r3(misc), r3(misc_b), r3(kv_b), _bias_tiles_t(rel_bias), gm[2].T,
                n_keep=min(TOPK_MAX, S // 4))
    u3 = r3(u)
    um1 = _shift_rows(u3, 1).reshape(B * S, CONV_CH)
    um2 = _shift_rows(u3, 2).reshape(B * S, CONV_CH)
    x2, hf, cwt = _row_call(_post_even_kernel, [xp.reshape(B * S, D), oa.reshape(B * S, -1), bg, u, um1, um2],
                            [mp[2], mp[3], mp[4]], post_consts, POST_OUTS, tp, S // tp, "post_even",
                            outs_t=[(N_EXPERTS, F32)])
    tm = MOE_TILE
    xp3 = _moe(hf, cwt.T, x2, mp[5], wg, wu, wd, tm, S // tm).reshape(B, S, D)
    outs_p = (kv_f.reshape(B, S, 2, N_KV_HEADS, HEAD_DIM), misc[:, :IDX_DIM].reshape(B, S, IDX_DIM), u3[:, S - 2:])

    q_s, kv_fs, _, qi_s, misc_s, _, bg_s, u_s = _inproj_even(xs, [ms[0], ms[1]], g_mix, wts, gm, Bs, 1)
    n_pool = kv_pool.shape[0]
    kv_t = kv_pool.transpose(0, 2, 3, 4, 1).reshape(n_pool, 2 * LANE, PAGE_SIZE)
    ki_t = kidx_pool.transpose(0, 2, 1)
    qi8 = jnp.pad(qi_s.reshape(Bs, IDX_HEADS, LANE)[:, :, :IDX_DIM], ((0, 0), (0, N_ATT_HEADS - IDX_HEADS), (0, 0)))
    wi8 = jnp.pad(misc_s[:, IDX_DIM:IDX_DIM + IDX_HEADS], ((0, 0), (0, N_ATT_HEADS - IDX_HEADS)))[:, :, None]
    oa_s = _dsa_sample(q_s.reshape(Bs, N_ATT_HEADS, LANE), qi8, wi8, misc_s[:, None, :IDX_DIM], kv_fs[:, None, :],
                       _bias_row(rel_bias, past), gm[2], ki_t, kv_t, page_table,
                       n_keep=min(TOPK_MAX, (past + 1) // 4), grp=_pick_group(Bs, SAMPLE_GROUP)).reshape(Bs, -1)
    x2s, hfs, cwts = _row_call(_post_even_kernel, [xs, oa_s, bg_s, u_s, conv_buf[:, 1], conv_buf[:, 0]],
                               [ms[2], ms[3], ms[4]], post_consts, POST_OUTS, Bs, 1, "post_even_s",
                               outs_t=[(N_EXPERTS, F32)])
    xs3 = _moe(hfs, cwts.T, x2s, ms[5], wg, wu, wd, Bs, 1)
    outs_s = (kv_fs.reshape(Bs, 1, 2, N_KV_HEADS, HEAD_DIM), misc_s[:, None, :IDX_DIM],
              jnp.concatenate([conv_buf[:, 1:], u_s[:, None, :]], axis=1))
    return xp3, xs3, outs_p, outs_s


O_Z0, O_Q0, O_KVC0, O_KVS0, O_KVW0, O_G0, O_END = 0, 1792, 2816, 3072, 3328, 3584, 3712
P_C = 3 * RWKV_DIM + LORA_W + LORA_A + LORA_G


def _inproj_odd_kernel(x_ref, shift_ref, scale_ref, g_ref, w_ref, qg_ref, ksg_ref, kwg_ref, gone_ref, gtwo_ref,
                       zc_ref, q_ref, kvc_ref, kvs_ref, kvsb_ref, kvw_ref, kvwb_ref, gates_ref):
    h = _modulate(x_ref[...], g_ref[...], shift_ref[...], scale_ref[...])
    z = jnp.dot(h.astype(BF16), w_ref[...], preferred_element_type=F32)
    zc_ref[...] = z[:, O_Z0:O_Q0]
    gone = gone_ref[...]
    gtwo = gtwo_ref[...]
    for t in range(N_ATT_HEADS):
        sl = slice(t * LANE, (t + 1) * LANE)
        q_ref[:, sl] = _group_rms(z[:, O_Q0 + t * LANE:O_Q0 + (t + 1) * LANE], gone, qg_ref[:, sl]).astype(BF16)
    kvc_ref[...] = z[:, O_KVC0:O_KVS0]
    for base, gain_ref, f_ref, b_ref in ((O_KVS0, ksg_ref, kvs_ref, kvsb_ref), (O_KVW0, kwg_ref, kvw_ref, kvwb_ref)):
        k = _group_rms(z[:, base:base + LANE], gtwo, gain_ref[...])
        v = z[:, base + LANE:base + 2 * LANE]
        f_ref[:, 0:LANE] = k
        f_ref[:, LANE:2 * LANE] = v
        b_ref[:, 0:LANE] = k.astype(BF16)
        b_ref[:, LANE:2 * LANE] = v.astype(BF16)
    gates_ref[...] = _sigmoid(z[:, O_G0:O_END])


def _odd_weights(w_in, w_out, q_norm, k_norm):
    d = w_in.shape[0]
    a_q, a_kv = N_ATT_HEADS * HEAD_DIM, 2 * N_KV_HEADS * HEAD_DIM
    o = P_C
    wz = w_in[:, :o]
    wq = w_in[:, o:o + a_q]; o += a_q
    wkv = w_in[:, o:o + 3 * a_kv]; o += 3 * a_kv
    wg = w_in[:, o:]
    wg = jnp.concatenate([wg, jnp.zeros((d, LANE - wg.shape[1]), F32)], -1)
    w_in_p = jnp.concatenate([wz, _pad_q_cols(wq), wkv, wg], -1).astype(BF16)
    w_out_p = jnp.concatenate([w_out[:RWKV_DIM], _pad_o_rows(w_out[RWKV_DIM:])], 0).astype(BF16)
    qg = jnp.tile(q_norm, 2 * N_ATT_HEADS).reshape(1, N_ATT_HEADS * LANE)
    ksg = jnp.tile(k_norm[1], 2).reshape(1, LANE)
    kwg = jnp.tile(k_norm[2], 2).reshape(1, LANE)
    return w_in_p, w_out_p, qg, ksg, kwg


def _inproj_odd(x, mods, g, wts, gm, tile, tpb):
    w_in_p, _, qg, ksg, kwg = wts
    gone, gtwo, _ = gm
    outs = [(P_C, F32), (N_ATT_HEADS * LANE, BF16), (2 * LANE, F32), (2 * LANE, F32), (2 * LANE, BF16),
            (2 * LANE, F32), (2 * LANE, BF16), (LANE, F32)]
    return _row_call(_inproj_odd_kernel, x, mods, [g, w_in_p, qg, ksg, kwg, gone, gtwo], outs, tile, tpb,
                     "inproj_odd")


def _rwkv_pre_kernel(z_ref, zp_ref, mu_ref, w0_ref, a0_ref, kk_ref, ka_ref, wup_ref, aup_ref, gup_ref, gsum_ref,
                     r_o, w_o, k_o, v_o, kk_o, kka_o, g_o):
    z = z_ref[...]
    zm = z + (zp_ref[...] - z) * mu_ref[...]
    r = zm[:, 0:RWKV_DIM]
    k = zm[:, RWKV_DIM:2 * RWKV_DIM]
    v = zm[:, 2 * RWKV_DIM:3 * RWKV_DIM]
    t12 = zm[:, 3 * RWKV_DIM:3 * RWKV_DIM + LANE]
    gd = zm[:, 3 * RWKV_DIM + LANE:P_C]
    xw = w0_ref[...] + _dot(jnp.tanh(t12), wup_ref[...])
    sp = jnp.maximum(-xw, 0.0) + jnp.log(1.0 + jnp.exp(-jnp.abs(xw)))
    w_o[...] = jnp.exp(-jnp.exp(-sp - 0.5))
    a = _sigmoid(a0_ref[...] + _dot(t12, aup_ref[...]))
    g_o[...] = _dot(_sigmoid(gd), gup_ref[...])
    kk = k * kk_ref[...]
    gsum = gsum_ref[...]
    for t in range(RWKV_DIM // LANE):
        sl = slice(t * LANE, (t + 1) * LANE)
        kt = kk[:, sl]
        nrm = jnp.maximum(jnp.sqrt(_dot_split(kt * kt, gsum)), 1e-12)
        kn = kt / nrm
        kk_o[:, sl] = kn
        kka_o[:, sl] = kn * a[:, sl]
    r_o[...] = r
    v_o[...] = v
    k_o[...] = k * (1.0 + (a - 1.0) * ka_ref[...])


def _rwkv_pre(zc, zprev, cpar, gsum, tile):
    mu, w0, w_up, a0, a_up, g_up, k_k, k_a = cpar
    z64 = jnp.zeros((LORA_W, RWKV_DIM), F32)
    consts = [mu.reshape(1, P_C), w0.reshape(1, -1), a0.reshape(1, -1), k_k.reshape(1, -1), k_a.reshape(1, -1),
              jnp.concatenate([w_up, z64], 0).astype(BF16), jnp.concatenate([z64, a_up], 0).astype(BF16),
              g_up.astype(BF16), gsum]
    return _row_call(_rwkv_pre_kernel, [zc, zprev], [], consts, [(RWKV_DIM, F32)] * 7, tile, 1, "rwkv_pre")


SCAN_P = 64
SCAN_VH = HEAD_DIM // 2


def _scan_kernel(kk_ref, w_ref, kka_ref, k_ref, r_ref, v_ref, s0_ref, y_ref, so_ref, st, *, tc):
    ti = pl.program_id(1)

    @pl.when(ti == 0)
    def _():
        st[...] = s0_ref[...]

    def step(t, c):
        kk, w, kka, kt, rt, vt = kk_ref[t], w_ref[t], kka_ref[t], k_ref[t], r_ref[t], v_ref[t]
        ys = []
        for vi in range(SCAN_VH):
            s = st[vi]
            sa = -jnp.sum(s * kk, axis=0, keepdims=True)
            sn = s * w + sa * kka + vt[vi:vi + 1, :] * kt
            st[vi] = sn
            ys.append(jnp.sum(sn * rt, axis=0, keepdims=True))
        y_ref[t] = jnp.concatenate(ys, axis=0)
        return c

    lax.fori_loop(0, tc, step, 0)

    @pl.when(ti == pl.num_programs(1) - 1)
    def _():
        so_ref[...] = st[...]


def _scan_layout_k(x, b, t):
    p = b * RWKV_HEADS
    a = x.reshape(b, t, RWKV_HEADS, HEAD_DIM).transpose(1, 3, 0, 2).reshape(t, HEAD_DIM, p)
    nc = -(-p // SCAN_P)
    a = jnp.pad(a, ((0, 0), (0, 0), (0, nc * SCAN_P - p))).reshape(t, HEAD_DIM, nc, SCAN_P).transpose(2, 0, 1, 3)
    return jnp.concatenate([a, a], -1)


def _scan_layout_v(x, b, t):
    p = b * RWKV_HEADS
    a = x.reshape(b, t, RWKV_HEADS, HEAD_DIM).transpose(1, 3, 0, 2).reshape(t, HEAD_DIM, p)
    nc = -(-p // SCAN_P)
    a = jnp.pad(a, ((0, 0), (0, 0), (0, nc * SCAN_P - p))).reshape(t, HEAD_DIM, nc, SCAN_P).transpose(2, 0, 1, 3)
    return jnp.concatenate([a[:, :, :SCAN_VH], a[:, :, SCAN_VH:]], -1)


def _scan_unlayout_y(y, b, t):
    p = b * RWKV_HEADS
    nc = y.shape[0]
    a = jnp.concatenate([y[..., :SCAN_P], y[..., SCAN_P:]], axis=2)
    a = a.transpose(1, 2, 0, 3).reshape(t, HEAD_DIM, nc * SCAN_P)[:, :, :p]
    return a.reshape(t, HEAD_DIM, b, RWKV_HEADS).transpose(2, 0, 3, 1).reshape(b * t, RWKV_DIM)


def _scan_layout_state(s):
    b = s.shape[0]
    p = b * RWKV_HEADS
    nc = -(-p // SCAN_P)
    a = jnp.pad(s.reshape(p, HEAD_DIM, HEAD_DIM), ((0, nc * SCAN_P - p), (0, 0), (0, 0)))
    a = a.reshape(nc, SCAN_P, HEAD_DIM, HEAD_DIM).transpose(0, 2, 3, 1)
    return jnp.concatenate([a[:, :SCAN_VH], a[:, SCAN_VH:]], -1)


def _scan_unlayout_state(st, b):
    p = b * RWKV_HEADS
    nc = st.shape[0]
    a = jnp.concatenate([st[..., :SCAN_P], st[..., SCAN_P:]], axis=1)
    a = a.transpose(0, 3, 1, 2).reshape(nc * SCAN_P, HEAD_DIM, HEAD_DIM)[:p]
    return a.reshape(b, RWKV_HEADS, HEAD_DIM, HEAD_DIM)


def _rwkv_scan(pre, s0, b, t, tc):
    r, w, k, v, kk, kka, _ = pre
    ks = [_scan_layout_k(a, b, t) for a in (kk, w, kka, k, r)]
    vs = _scan_layout_v(v, b, t)
    s0l = _scan_layout_state(s0)
    nc = s0l.shape[0]
    kspec = pl.BlockSpec((None, tc, HEAD_DIM, LANE), lambda c, i: (c, i, 0, 0))
    vspec = pl.BlockSpec((None, tc, SCAN_VH, LANE), lambda c, i: (c, i, 0, 0))
    sspec = pl.BlockSpec((None, SCAN_VH, HEAD_DIM, LANE), lambda c, i: (c, 0, 0, 0))
    y, so = pl.pallas_call(
        functools.partial(_scan_kernel, tc=tc),
        grid=(nc, t // tc),
        in_specs=[kspec] * 5 + [vspec, sspec],
        out_specs=[vspec, sspec],
        out_shape=[jax.ShapeDtypeStruct((nc, t, SCAN_VH, LANE), F32),
                   jax.ShapeDtypeStruct((nc, SCAN_VH, HEAD_DIM, LANE), F32)],
        scratch_shapes=[pltpu.VMEM((SCAN_VH, HEAD_DIM, LANE), F32)],
        compiler_params=_cparams("parallel", "arbitrary"),
        name="rwkv_scan",
    )(*ks, vs, s0l)
    return _scan_unlayout_y(y, b, t), _scan_unlayout_state(so, b)


def _compress_kernel(x_ref, pe_ref, w_ref, kg_ref, gtwo_ref, o_ref):
    z = jnp.dot((x_ref[...] + pe_ref[...]).astype(BF16), w_ref[...], preferred_element_type=F32)
    o_ref[:, 0:LANE] = _group_rms(z[:, 0:LANE], gtwo_ref[...], kg_ref[...])
    o_ref[:, LANE:2 * LANE] = z[:, LANE:2 * LANE]


def _compress_weights(cmp_pe, cmp_w, k_norm_c):
    wk = cmp_w[0].reshape(CMP_BLOCK, HEAD_DIM, HEAD_DIM)
    wv = cmp_w[1].reshape(CMP_BLOCK, HEAD_DIM, HEAD_DIM)
    full = jnp.zeros((CMP_BLOCK, 4, HEAD_DIM, 4, HEAD_DIM), F32)
    for s, w in enumerate((wk, wk, wv, wv)):
        full = full.at[:, s, :, s, :].set(w)
    pe = jnp.stack([cmp_pe[0], cmp_pe[0], cmp_pe[1], cmp_pe[1]], axis=1)
    return (full.reshape(CMP_BLOCK * 4 * HEAD_DIM, 4 * HEAD_DIM).astype(BF16), pe.reshape(1, -1),
            jnp.tile(k_norm_c, 2).reshape(1, LANE))


def _compress(rows, cw, gtwo, tile):
    wfull, pe, kg = cw
    return _row_call(_compress_kernel, rows, [], [pe, wfull, kg, gtwo], [(2 * LANE, F32)], tile, 1, "nsa_compress")[0]


def _compress_paged_kernel(pt_ref, ident_ref, pe_ref, w_ref, kg_ref, gtwo_ref, *refs, grp, n_pages):
    del pt_ref
    page_refs, o_ref, xs = refs[:grp * n_pages], refs[-2], refs[-1]
    ident = ident_ref[...]
    for i in range(grp * n_pages):
        for half in range(2):
            xt = page_refs[i][half * LANE:(half + 1) * LANE, :]
            hi = xt.astype(BF16)
            r1 = xt - hi.astype(F32)
            mid = r1.astype(BF16)
            lo = (r1 - mid.astype(F32)).astype(BF16)
            xs[half, i * PAGE_SIZE:(i + 1) * PAGE_SIZE, :] = (_dot_nt(ident, hi) + _dot_nt(ident, mid)
                                                              + _dot_nt(ident, lo))
    n_blk = grp * n_pages * (PAGE_SIZE // CMP_BLOCK)
    acc = [jnp.zeros((n_blk, LANE), F32) for _ in range(2)]
    for r in range(CMP_BLOCK):
        for half in range(2):
            rows = xs[half, pl.ds(r, n_blk, stride=CMP_BLOCK), :] + pe_ref[half, r:r + 1, :]
            acc[half] = acc[half] + jnp.dot(rows.astype(BF16), w_ref[half, r], preferred_element_type=F32)
    kc = _group_rms(acc[0], gtwo_ref[...], kg_ref[...])
    per_seq = n_blk // grp
    for g in range(grp):
        o_ref[g, 0:per_seq, 0:LANE] = kc[g * per_seq:(g + 1) * per_seq].astype(BF16)
        o_ref[g, 0:per_seq, LANE:2 * LANE] = acc[1][g * per_seq:(g + 1) * per_seq].astype(BF16)
        o_ref[g, per_seq:LANE, :] = jnp.zeros((LANE - per_seq, 2 * LANE), BF16)


def _compress_paged(cmp_t, page_table, cmp_pe, cmp_w, k_norm_c, gtwo, grp):
    b, n_pages = page_table.shape
    wk = cmp_w[0].reshape(CMP_BLOCK, HEAD_DIM, HEAD_DIM)
    wv = cmp_w[1].reshape(CMP_BLOCK, HEAD_DIM, HEAD_DIM)
    wbd = jnp.zeros((2, CMP_BLOCK, N_KV_HEADS, HEAD_DIM, N_KV_HEADS, HEAD_DIM), F32)
    for half, w in enumerate((wk, wv)):
        for h in range(N_KV_HEADS):
            wbd = wbd.at[half, :, h, :, h, :].set(w)
    wbd = wbd.reshape(2, CMP_BLOCK, LANE, LANE).astype(BF16)
    pe = jnp.concatenate([cmp_pe, cmp_pe], axis=2)
    ident = jnp.eye(LANE, dtype=BF16)
    kg = jnp.tile(k_norm_c, 2).reshape(1, LANE)
    cspec = lambda a: pl.BlockSpec(a.shape, lambda i, pt: (0,) * a.ndim)
    kern = functools.partial(_compress_paged_kernel, grp=grp, n_pages=n_pages)
    return pl.pallas_call(
        kern,
        grid_spec=pltpu.PrefetchScalarGridSpec(
            num_scalar_prefetch=1, grid=(b // grp,),
            in_specs=[cspec(ident), cspec(pe), cspec(wbd), cspec(kg), cspec(gtwo)]
            + _page_specs(cmp_t, n_pages, grp),
            out_specs=pl.BlockSpec((grp, LANE, 2 * LANE), lambda i, pt: (i, 0, 0)),
            scratch_shapes=[pltpu.VMEM((2, grp * n_pages * PAGE_SIZE, LANE), F32)]),
        out_shape=jax.ShapeDtypeStruct((b, LANE, 2 * LANE), BF16),
        compiler_params=_cparams("parallel"),
        name="nsa_compress_paged",
    )(page_table, ident, pe, wbd, kg, gtwo, *([cmp_t] * (grp * n_pages)))


def _nsa_kernel(q_ref, gates_ref, kcv_ref, kvs_ref, kvw_ref, bias_ref, biasc_ref, pair_ref, o_ref,
                *, grp, qb, q_base, n_cmp, n_sel, w_off, single):
    q0 = q_base if single else q_base + pl.program_id(1) * qb
    nblk = (q0 + qb - 1) // LANE + 1
    dq = q0 // LANE
    seqs = range(grp)
    heads = range(N_KV_HEADS)
    row = lax.broadcasted_iota(I32, (qb, LANE), 0)
    lane = lax.broadcasted_iota(I32, (qb, LANE), 1)
    t_pos = q0 + row
    qs = [[_stack_heads(q_ref.at[g], hk) for hk in heads] for g in seqs]

    def gate_col(g, br, hk):
        gates = gates_ref[g]
        return jnp.concatenate([gates[:, br * N_ATT_HEADS + hk * GQA + a:br * N_ATT_HEADS + hk * GQA + a + 1]
                                for a in range(GQA)], axis=0)

    mask_c4 = _tile4(((lane * CMP_BLOCK + CMP_BLOCK - 1) <= t_pos) & (lane < n_cmp))
    cur = t_pos // SEL_BLOCK
    forced = (lane == 0) | (lane == cur) | (lane == cur - 1)
    sel_causal = lane * SEL_BLOCK <= t_pos
    lane_f = lane.astype(F32)
    o_cmp = [[None] * N_KV_HEADS for _ in seqs]
    selm = [[None] * N_KV_HEADS for _ in seqs]
    for g in seqs:
        kc = kcv_ref[g, :, 0:LANE]
        vc = kcv_ref[g, :, LANE:2 * LANE]
        for hk in heads:
            s = _dot_nt(qs[g][hk], kc) * ATT_SCALE + biasc_ref[hk * GQA:(hk + 1) * GQA].reshape(GQA * qb, LANE)
            s = jnp.where(mask_c4, s, NEG)
            e = jnp.exp(s - jnp.max(s, axis=1, keepdims=True))
            p = jnp.where(mask_c4, e / jnp.sum(e, axis=1, keepdims=True), 0.0)
            o_cmp[g][hk] = jnp.dot(p.astype(BF16), vc, preferred_element_type=F32)
            ps = p[0:qb]
            for a in range(1, GQA):
                ps = ps + p[a * qb:(a + 1) * qb]
            score = _dot_split(ps, pair_ref[...])
            score = jnp.where(sel_causal, jnp.where(forced, FORCE, score), NEG)
            picked = jnp.zeros((qb, LANE), F32)
            for _ in range(n_sel):
                mx = jnp.max(score, axis=1, keepdims=True)
                first = jnp.min(jnp.where(score == mx, lane_f, float(LANE)), axis=1, keepdims=True)
                hit = lane_f == first
                picked = jnp.where(hit, 1.0, picked)
                score = jnp.where(hit, TAKEN, score)
            selm[g][hk] = picked.astype(BF16)

    def key_blocks(ref, g, jb):
        off = pl.multiple_of(jb * LANE, LANE)
        return ref[g, pl.ds(off, LANE), 0:LANE], ref[g, pl.ds(off, LANE), LANE:2 * LANE]

    def biases_of(jb):
        dsel = jnp.minimum(dq - jb, 2)
        return [bias_ref[dsel, hk * GQA:(hk + 1) * GQA].reshape(GQA * qb, LANE) for hk in heads]

    init = tuple(tuple(_flash_init(GQA * qb) for _ in heads) for _ in seqs)

    erow = lax.broadcasted_iota(I32, (LANE, LANE), 0)
    ecol = lax.broadcasted_iota(I32, (LANE, LANE), 1)

    def slc_body(jb, carry):
        expand = jnp.where(erow == 2 * jb + ecol // SEL_BLOCK, 1.0, 0.0).astype(BF16)
        causal = jb * LANE + lane <= t_pos
        biases = biases_of(jb)
        out = []
        for g in seqs:
            kb, vb = key_blocks(kvs_ref, g, jb)
            res = []
            for hk in heads:
                tok = jnp.dot(selm[g][hk], expand, preferred_element_type=F32) > 0.5
                res.append(_flash_step(qs[g][hk], kb, vb, biases[hk], _tile4(tok & causal), *carry[g][hk]))
            out.append(tuple(res))
        return tuple(out)

    res_s = lax.fori_loop(0, nblk, slc_body, init)

    def win_body(jb, carry):
        dist = t_pos - (jb * LANE + lane)
        msk = _tile4((dist >= 0) & (dist < WINDOW))
        biases = biases_of(jb)
        out = []
        for g in seqs:
            kb, vb = key_blocks(kvw_ref, g, jb - w_off)
            out.append(tuple(_flash_step(qs[g][hk], kb, vb, biases[hk], msk, *carry[g][hk]) for hk in heads))
        return tuple(out)

    res_w = lax.fori_loop(jnp.maximum(dq - WINDOW // LANE, 0), dq + 1, win_body, init)

    for g in seqs:
        outs = []
        for hk in heads:
            o_s = res_s[g][hk][2] / res_s[g][hk][1]
            o_w = res_w[g][hk][2] / res_w[g][hk][1]
            outs.append(gate_col(g, 0, hk) * o_cmp[g][hk] + gate_col(g, 1, hk) * o_s + gate_col(g, 2, hk) * o_w)
        _write_heads(o_ref.at[g], outs, qb)


def _bias_cmp(rel_bias, q_starts, qb):
    q0 = jnp.asarray(q_starts, I32)[:, None, None]
    r = jnp.arange(qb)[None, :, None]
    n = jnp.arange(LANE)[None, None, :]
    return _bias_lookup(rel_bias, q0 + r - (n * CMP_BLOCK + CMP_BLOCK - 1)).transpose(0, 3, 1, 2)


def _nsa(q, gates, kcv, kvs_b, kvw_b, bias, bias_c, pair, *, qb, q_base, n_cmp, n_sel, w_off, grp):
    b, rows, _ = q.shape
    nq = rows // qb
    kern = functools.partial(_nsa_kernel, grp=grp, qb=qb, q_base=q_base, n_cmp=n_cmp, n_sel=n_sel, w_off=w_off,
                             single=(nq == 1))
    qspec = lambda w: pl.BlockSpec((grp, qb, w), lambda bi, i: (bi, i, 0))
    kspec = lambda a: pl.BlockSpec((grp,) + a.shape[1:], lambda bi, i: (bi, 0, 0))
    return pl.pallas_call(
        kern,
        grid=(b // grp, nq),
        in_specs=[qspec(N_ATT_HEADS * LANE), qspec(LANE), kspec(kcv), kspec(kvs_b), kspec(kvw_b), _const_spec(bias),
                  pl.BlockSpec((None,) + bias_c.shape[1:], lambda bi, i: (i, 0, 0, 0)), _const_spec(pair)],
        out_specs=qspec(N_ATT_HEADS * LANE),
        out_shape=jax.ShapeDtypeStruct((b, rows, N_ATT_HEADS * LANE), BF16),
        compiler_params=_cparams("parallel", "parallel"),
        name="nsa_attention",
    )(q, gates, kcv, kvs_b, kvw_b, bias, bias_c, pair)


def _nsa_sample_kernel(pt_ref, q_ref, gates_ref, kcv_ref, snew_ref, wnew_ref, win_ref, bias_ref, biasc_ref,
                       biasw_ref, pair_ref, *refs, grp, n_pages, t_pos, n_cmp, n_sel, w_eff):
    del pt_ref
    slc_refs, o_ref = refs[:grp * n_pages], refs[-1]
    lane1 = lax.broadcasted_iota(I32, (1, LANE), 1)
    lane8 = lax.broadcasted_iota(I32, (N_ATT_HEADS, LANE), 1)
    row_all = lax.broadcasted_iota(I32, (N_ATT_HEADS, (n_pages + 1) * LANE), 0)
    lanew = lax.broadcasted_iota(I32, (N_ATT_HEADS, w_eff), 1)
    lane1_f = lane1.astype(F32)
    bias = bias_ref[...]
    bias_now = bias[:, n_pages * LANE:n_pages * LANE + 1]
    mask_c = ((lane8 * CMP_BLOCK + CMP_BLOCK - 1) <= t_pos) & (lane8 < n_cmp)
    cur = t_pos // SEL_BLOCK
    forced = (lane1 == 0) | (lane1 == cur) | (lane1 == cur - 1)
    sel_causal = lane1 * SEL_BLOCK <= t_pos
    tail_valid = jnp.concatenate([lane1] * n_pages + [lane1 + LANE], axis=1) <= LANE
    for g in range(grp):
        q = q_ref[g]
        qf = q.astype(F32)
        s = jnp.where(mask_c, _dot_nt(q, kcv_ref[g, :, 0:LANE]) * ATT_SCALE + biasc_ref[...], NEG)
        e = jnp.exp(s - jnp.max(s, axis=1, keepdims=True))
        pc = jnp.where(mask_c, e / jnp.sum(e, axis=1, keepdims=True), 0.0)
        o_c = jnp.dot(pc.astype(BF16), kcv_ref[g, :, LANE:2 * LANE], preferred_element_type=F32)
        masks = []
        for hk in range(N_KV_HEADS):
            ps = jnp.sum(pc[hk * GQA:(hk + 1) * GQA], axis=0, keepdims=True)
            score = _dot_split(ps, pair_ref[...])
            score = jnp.where(sel_causal, jnp.where(forced, FORCE, score), NEG)
            picked = jnp.zeros((1, LANE), F32)
            for _ in range(n_sel):
                mx = jnp.max(score, axis=1, keepdims=True)
                first = jnp.min(jnp.where(score == mx, lane1_f, float(LANE)), axis=1, keepdims=True)
                hit = lane1_f == first
                picked = jnp.where(hit, 1.0, picked)
                score = jnp.where(hit, TAKEN, score)
            per_page = PAGE_SIZE // SEL_BLOCK
            tiles = []
            for p in range(n_pages + 1):
                t = jnp.zeros((1, LANE), F32)
                for a in range(per_page):
                    blk = picked[:, p * per_page + a:p * per_page + a + 1]
                    t = jnp.where(lane1 // SEL_BLOCK == a, blk, t)
                tiles.append(t)
            masks.append(jnp.concatenate(tiles, axis=1))
        valid = (jnp.where(row_all < GQA, masks[0], masks[1]) > 0.5) & tail_valid
        snew = _bf16_round(snew_ref[g])
        tiles = [_dot(q, slc_refs[g * n_pages + p][0:LANE, :]) for p in range(n_pages)]
        tiles.append(jnp.where(lane1 == 0, jnp.sum(qf * snew[:, 0:LANE], axis=1, keepdims=True), 0.0))
        s = jnp.where(valid, jnp.concatenate(tiles, axis=1) * ATT_SCALE + bias, NEG)
        e = jnp.exp(s - jnp.max(s, axis=1, keepdims=True))
        p_all = jnp.where(valid, e / jnp.sum(e, axis=1, keepdims=True), 0.0)
        o_s = p_all[:, n_pages * LANE:n_pages * LANE + 1] * snew[:, LANE:2 * LANE]
        for p in range(n_pages):
            o_s = o_s + _dot_nt(p_all[:, p * LANE:(p + 1) * LANE], slc_refs[g * n_pages + p][LANE:2 * LANE, :])
        wnew = _bf16_round(wnew_ref[g])
        valid_w = (w_eff - lanew) < WINDOW
        s_w = jnp.where(valid_w, _dot(q, win_ref[g, 0:LANE, :]) * ATT_SCALE + biasw_ref[...], NEG)
        s_n = jnp.sum(qf * wnew[:, 0:LANE], axis=1, keepdims=True) * ATT_SCALE + bias_now
        m = jnp.maximum(jnp.max(s_w, axis=1, keepdims=True), s_n)
        e_w = jnp.where(valid_w, jnp.exp(s_w - m), 0.0)
        e_n = jnp.exp(s_n - m)
        den = jnp.sum(e_w, axis=1, keepdims=True) + e_n
        o_w = _dot_nt(e_w / den, win_ref[g, LANE:2 * LANE, :]) + (e_n / den) * wnew[:, LANE:2 * LANE]
        gates = gates_ref[g]
        _head_rows_out(o_ref, g, gates[:, 0:1] * o_c + gates[:, 1:2] * o_s + gates[:, 2:3] * o_w)


def _nsa_sample(q8, gates8, kcv, snew, wnew, win_t, bias, bias_c, bias_w, pair, slc_t, page_table,
                *, t_pos, n_cmp, n_sel, grp):
    b, n_pages = page_table.shape
    w_eff = win_t.shape[2]
    gspec = lambda a: pl.BlockSpec((grp,) + a.shape[1:], lambda i, pt: (i,) + (0,) * (a.ndim - 1))
    cspec = lambda a: pl.BlockSpec(a.shape, lambda i, pt: (0,) * a.ndim)
    kern = functools.partial(_nsa_sample_kernel, grp=grp, n_pages=n_pages, t_pos=t_pos, n_cmp=n_cmp, n_sel=n_sel,
                             w_eff=w_eff)
    return pl.pallas_call(
        kern,
        grid_spec=pltpu.PrefetchScalarGridSpec(
            num_scalar_prefetch=1, grid=(b // grp,),
            in_specs=[gspec(q8), gspec(gates8), gspec(kcv), gspec(snew), gspec(wnew), gspec(win_t), cspec(bias),
                      cspec(bias_c), cspec(bias_w), cspec(pair)] + _page_specs(slc_t, n_pages, grp),
            out_specs=pl.BlockSpec((grp, N_ATT_HEADS, LANE), lambda i, pt: (i, 0, 0))),
        out_shape=jax.ShapeDtypeStruct((b, N_ATT_HEADS, LANE), BF16),
        compiler_params=_cparams("parallel"),
        name="nsa_sample",
    )(page_table, q8, gates8, kcv, snew, wnew, win_t, bias, bias_c, bias_w, pair, *([slc_t] * (grp * n_pages)))


def _post_odd_kernel(x_ref, y_ref, r_ref, k_ref, v_ref, g_ref, od_ref, gate_ref, shf_ref, scf_ref,
                     gf_ref, lnw_ref, lnb_ref, rk_ref, gtwo_ref, wo_ref, wrt_ref, br_ref, x2_ref, hf_ref, cwt_ref):
    gtwo = gtwo_ref[...]
    mix = jnp.dot(od_ref[...], wo_ref[RWKV_DIM:RWKV_DIM + N_ATT_HEADS * LANE, :], preferred_element_type=F32)
    for t in range(RWKV_DIM // LANE):
        sl = slice(t * LANE, (t + 1) * LANE)
        y = y_ref[:, sl]
        dlt = y - _dot_split(y, gtwo)
        yn = (dlt * lax.rsqrt(_dot_split(dlt * dlt, gtwo) + GN_EPS)) * lnw_ref[:, sl] + lnb_ref[:, sl]
        dot_rk = _dot_split(r_ref[:, sl] * k_ref[:, sl] * rk_ref[:, sl], gtwo) * float(HEAD_DIM)
        oc = (yn + dot_rk * v_ref[:, sl]) * g_ref[:, sl]
        mix = mix + jnp.dot(oc.astype(BF16), wo_ref[sl, :], preferred_element_type=F32)
    _post_tail(x_ref[...], mix, gate_ref[...], gf_ref[...], shf_ref[...], scf_ref[...], wrt_ref, br_ref,
               x2_ref, hf_ref, cwt_ref)


def _odd_layer(xp, xs, mp, ms, page_table, wkv0, shift0, cmp_pool, slc_pool, win_buf, rel_bias, router, experts,
               g_mix, g_ffn, w_in, w_out, cpar, r_k, ln_w, ln_b, q_norm, k_norm, cmp_pe, cmp_w):
    B, S, D = xp.shape
    Bs = xs.shape[0]
    n_pages = page_table.shape[1]
    past = n_pages * PAGE_SIZE
    wts = _odd_weights(w_in, w_out, q_norm, k_norm)
    gm = _gmats()
    gone, gtwo, _ = gm
    gsum = (gtwo.astype(F32) * HEAD_DIM).astype(BF16)
    i = jnp.arange(LANE)
    pair = jnp.where(i[:, None] // 2 == i[None, :], 1.0, 0.0).astype(BF16)
    cw = _compress_weights(cmp_pe, cmp_w, k_norm[0])
    wrt, br = router
    wg, wu, wd = experts
    g_mix = g_mix.reshape(1, D)
    post_consts = [g_ffn.reshape(1, D), ln_w.reshape(1, -1), ln_b.reshape(1, -1), r_k.reshape(1, -1), gtwo, wts[1],
                   wrt, br]
    w_eff = win_buf.shape[1]

    tp = ROW_TILE
    zc, q_p, kvc, kvs, kvs_b, kvw, kvw_b, gates = _inproj_odd(xp.reshape(B * S, D), [mp[0], mp[1]], g_mix, wts, gm,
                                                               tp, S // tp)
    r3 = lambda a: a.reshape(B, S, a.shape[-1])
    pre = _rwkv_pre(zc, _shift_rows(r3(zc), 1).reshape(B * S, P_C), cpar, gsum, tp)
    y, wkv_p = _rwkv_scan(pre, jnp.zeros((B, RWKV_HEADS, HEAD_DIM, HEAD_DIM), F32), B, S, SCAN_TIME_CHUNK)
    n_cmp = S // CMP_BLOCK
    kcv = _compress(kvc.reshape(B * n_cmp, CMP_BLOCK * 2 * LANE), cw, gtwo, _pick_tile(B * n_cmp, 256))
    kcv = jnp.pad(kcv.reshape(B, n_cmp, 2 * LANE), ((0, 0), (0, LANE - n_cmp), (0, 0))).astype(BF16)
    n_slc = -(-S // SEL_BLOCK)
    od = _nsa_t(r3(q_p), r3(gates), kcv, r3(kvs_b), r3(kvw_b), _bias_tiles_t(rel_bias), _bias_cmp_t(rel_bias, S),
                pair.T, n_cmp=n_cmp, n_sel=min(N_SEL_BLOCKS, n_slc))
    x2, hf, cwt = _row_call(_post_odd_kernel,
                            [xp.reshape(B * S, D), y, pre[0], pre[2], pre[3], pre[6], od.reshape(B * S, -1)],
                            [mp[2], mp[3], mp[4]], post_consts, POST_OUTS, tp, S // tp, "post_odd",
                            outs_t=[(N_EXPERTS, F32)])
    tm = MOE_TILE
    xp3 = _moe(hf, cwt.T, x2, mp[5], wg, wu, wd, tm, S // tm).reshape(B, S, D)
    kv5 = lambda a, n: a.reshape(-1, n, 2, N_KV_HEADS, HEAD_DIM)
    outs_p = (wkv_p, r3(zc)[:, S - 1], kv5(kvc, S), kv5(kvs, S), kv5(kvw, S)[:, S - min(WINDOW, S):])

    zc_s, q_s, kvc_s, kvs_s, _, kvw_s, _, gates_s = _inproj_odd(xs, [ms[0], ms[1]], g_mix, wts, gm, Bs, 1)
    pre_s = _rwkv_pre(zc_s, shift0, cpar, gsum, Bs)
    y_s, wkv_s = _rwkv_scan(pre_s, wkv0, Bs, 1, 1)
    n_pool = cmp_pool.shape[0]
    cmp_t = cmp_pool.transpose(0, 2, 3, 4, 1).reshape(n_pool, 2 * LANE, PAGE_SIZE)
    kcv_s = _compress_paged(cmp_t, page_table, cmp_pe, cmp_w, k_norm[0], gtwo, _pick_group(Bs, SAMPLE_GROUP))
    n_cmp_s = (past + 1) // CMP_BLOCK
    slc_t = slc_pool.transpose(0, 2, 3, 4, 1).reshape(n_pool, 2 * LANE, PAGE_SIZE)
    win_t = win_buf.transpose(0, 2, 3, 4, 1).reshape(Bs, 2 * LANE, w_eff)
    gates8 = jnp.pad(gates_s[:, :3 * N_ATT_HEADS].reshape(Bs, 3, N_ATT_HEADS).transpose(0, 2, 1),
                     ((0, 0), (0, 0), (0, LANE - 3)))
    n_slc_s = -(-(past + 1) // SEL_BLOCK)
    od_s = _nsa_sample(q_s.reshape(Bs, N_ATT_HEADS, LANE), gates8, kcv_s, kvs_s[:, None, :], kvw_s[:, None, :],
                       win_t, _bias_row(rel_bias, past), _bias_cmp(rel_bias, [past], 1)[0, :, 0, :],
                       _bias_lookup(rel_bias, w_eff - jnp.arange(w_eff)).T, pair, slc_t, page_table,
                       t_pos=past, n_cmp=n_cmp_s, n_sel=min(N_SEL_BLOCKS, n_slc_s),
                       grp=_pick_group(Bs, SAMPLE_GROUP)).reshape(Bs, -1)
    x2s, hfs, cwts = _row_call(_post_odd_kernel, [xs, y_s, pre_s[0], pre_s[2], pre_s[3], pre_s[6], od_s],
                               [ms[2], ms[3], ms[4]], post_consts, POST_OUTS, Bs, 1, "post_odd_s",
                               outs_t=[(N_EXPERTS, F32)])
    xs3 = _moe(hfs, cwts.T, x2s, ms[5], wg, wu, wd, Bs, 1)
    win_new = jnp.concatenate([win_buf[:, 1:], kv5(kvw_s, 1)], axis=1)
    outs_s = (wkv_s, zc_s, kv5(kvc_s, 1), kv5(kvs_s, 1), win_new)
    return xp3, xs3, outs_p, outs_s


def _mods(c_p, c_s, w, b):
    nb = c_p.shape[0]
    m = _ada(jnp.concatenate([c_p, c_s], 0), w.astype(BF16), b)
    parts = jnp.split(m, 6, axis=-1)
    return [p[:nb, None, :] for p in parts], [p[None, nb:, :] for p in parts]


def _forward(x_prompt, x_sample, c_prompt, c_sample, page_table, cache_a_kv, cache_a_kidx, state_b_conv,
             state_c_wkv, state_c_shift, cache_d_cmp, cache_d_slc, cache_d_win, rel_bias, w_router, b_router,
             w_ada, b_ada, g_norm_mix, g_norm_ffn, w_expert_gate, w_expert_up, w_expert_down, e_w_in, e_w_out,
             a_q_norm, a_k_norm, b_conv_w, b_conv_b, o_w_in, o_w_out, c_mu, c_w0, c_w_up, c_a0, c_a_up,
             c_g_up, c_k_k, c_k_a, c_r_k, c_ln_w, c_ln_b, d_q_norm, d_k_norm, d_cmp_pe, d_cmp_w):
    assert w_ada.shape[0] == 2 and e_w_in.shape[0] == 1 and o_w_in.shape[0] == 1
    B, S, D = x_prompt.shape
    Bs = x_sample.shape[0]
    assert x_sample.shape[1] == 1
    xp, xs = x_prompt, x_sample.reshape(Bs, D)
    router = (w_router.T, b_router.reshape(N_EXPERTS, 1))
    n_pool = cache_a_kv.shape[1]
    experts = lambda l: tuple(w[l].astype(BF16) for w in (w_expert_gate, w_expert_up, w_expert_down))

    mp, ms = _mods(c_prompt, c_sample, w_ada[0], b_ada[0])
    xp, xs, ep, es = _even_layer(xp, xs, mp, ms, page_table, cache_a_kv[0], cache_a_kidx[0], state_b_conv[0],
                                 rel_bias, router, experts(0), g_norm_mix[0], g_norm_ffn[0], e_w_in[0], e_w_out[0],
                                 a_q_norm[0], a_k_norm[0], b_conv_w[0], b_conv_b[0])
    mp, ms = _mods(c_prompt, c_sample, w_ada[1], b_ada[1])
    cpar = (c_mu[0], c_w0[0], c_w_up[0], c_a0[0], c_a_up[0], c_g_up[0], c_k_k[0], c_k_a[0])
    xp, xs, op, os_ = _odd_layer(xp, xs, mp, ms, page_table, state_c_wkv[0], state_c_shift[0], cache_d_cmp[0],
                                 cache_d_slc[0], cache_d_win[0], rel_bias, router, experts(1), g_norm_mix[1],
                                 g_norm_ffn[1], o_w_in[0], o_w_out[0], cpar, c_r_k[0].reshape(-1), c_ln_w[0],
                                 c_ln_b[0], d_q_norm[0], d_k_norm[0], d_cmp_pe[0], d_cmp_w[0])
    stack = lambda ts: tuple(a[None] for a in ts)
    return (xp, xs.reshape(Bs, 1, D)) + stack(ep) + stack(op) + stack(es) + stack(os_)


def kernel(x_prompt, x_sample, c_prompt, c_sample, page_table, cache_a_kv, cache_a_kidx, state_b_conv, state_c_wkv, state_c_shift, cache_d_cmp, cache_d_slc, cache_d_win, rel_bias, w_router, b_router, w_ada, b_ada, g_norm_mix, g_norm_ffn, w_expert_gate, w_expert_up, w_expert_down, e_w_in, e_w_out, a_q_norm, a_k_norm, b_conv_w, b_conv_b, o_w_in, o_w_out, c_mu, c_w0, c_w_up, c_a0, c_a_up, c_g_up, c_k_k, c_k_a, c_r_k, c_ln_w, c_ln_b, d_q_norm, d_k_norm, d_cmp_pe, d_cmp_w):
    return _forward(x_prompt, x_sample, c_prompt, c_sample, page_table, cache_a_kv, cache_a_kidx, state_b_conv,
                    state_c_wkv, state_c_shift, cache_d_cmp, cache_d_slc, cache_d_win, rel_bias, w_router, b_router,
                    w_ada, b_ada, g_norm_mix, g_norm_ffn, w_expert_gate, w_expert_up, w_expert_down, e_w_in, e_w_out,
                    a_q_norm, a_k_norm, b_conv_w, b_conv_b, o_w_in, o_w_out, c_mu, c_w0, c_w_up, c_a0, c_a_up,
                    c_g_up, c_k_k, c_k_a, c_r_k, c_ln_w, c_ln_b, d_q_norm, d_k_norm, d_cmp_pe, d_cmp_w)
```

```python
import functools
import math

import jax
import jax.numpy as jnp
from jax import lax
from jax.experimental import pallas as pl
from jax.experimental.pallas import tpu as pltpu

F32 = jnp.float32
BF16 = jnp.bfloat16
I32 = jnp.int32

LANE = 128
HEAD_DIM = 64
N_ATT_HEADS = 8
N_KV_HEADS = 2
GQA = N_ATT_HEADS // N_KV_HEADS
IDX_HEADS = 4
IDX_DIM = 64
TOPK_MAX = 256
CONV_CH = 512
RWKV_HEADS = 8
RWKV_DIM = RWKV_HEADS * HEAD_DIM
LORA_W = 64
LORA_A = 64
LORA_G = 128
GN_EPS = 64e-5
CMP_BLOCK = 32
SEL_BLOCK = 64
N_SEL_BLOCKS = 8
WINDOW = 512
N_BUCKETS = 32
MAX_DISTANCE = 128
N_EXPERTS = 16
N_GROUPS = 4
EXPERTS_PER_GROUP = N_EXPERTS // N_GROUPS
D_EXPERT = 512
PAGE_SIZE = 128
RMS_EPS = 1e-6
NEG = -1e30
FORCE = 1e9
TAKEN = -3e38
ATT_SCALE = HEAD_DIM ** -0.5
VMEM_LIMIT = 56 * 1024 * 1024
ROW_TILE = 256
MOE_TILE = 512
SCAN_TIME_CHUNK = 32
SAMPLE_GROUP = 4


def _cparams(*sem):
    return pltpu.CompilerParams(dimension_semantics=sem, vmem_limit_bytes=VMEM_LIMIT)


def _pick_tile(rows, pref):
    t = min(pref, rows)
    while rows % t or (t % 8 and t != rows):
        t -= 1
    return t


def _pick_group(n, pref):
    g = min(pref, n)
    while n % g:
        g -= 1
    return g


def _const_spec(a):
    nd = a.ndim
    return pl.BlockSpec(a.shape, lambda *_: (0,) * nd)


def _dot(a, b):
    return jnp.dot(a.astype(BF16), b.astype(BF16), preferred_element_type=F32)


def _dot_nt(a, b):
    return lax.dot_general(a.astype(BF16), b.astype(BF16), (((1,), (1,)), ((), ())),
                           preferred_element_type=F32)


def _dot_split(x, m):
    hi = x.astype(BF16)
    r1 = x - hi.astype(F32)
    mid = r1.astype(BF16)
    lo = (r1 - mid.astype(F32)).astype(BF16)
    return (jnp.dot(hi, m, preferred_element_type=F32) + jnp.dot(mid, m, preferred_element_type=F32)
            + jnp.dot(lo, m, preferred_element_type=F32))


def _bf16_round(x):
    return x.astype(BF16).astype(F32)


def _sigmoid(x):
    return 1.0 / (1.0 + jnp.exp(-x))


def _silu(x):
    return x * _sigmoid(x)


def _modulate(x, g, shift, scale):
    y = x * lax.rsqrt(jnp.mean(x * x, axis=-1, keepdims=True) + RMS_EPS)
    return (y * g) * (1.0 + scale) + shift


def _group_rms(t, gmat, gain):
    ms = _dot_split(t * t, gmat)
    return (t * lax.rsqrt(ms + RMS_EPS)) * gain


def _ada_kernel(c_ref, w_ref, b_ref, o_ref):
    o_ref[...] = _dot(_silu(c_ref[...]), w_ref[...]) + b_ref[...]


def _ada(c, w_all, b, layer):
    r, d = c.shape
    n = w_all.shape[2]
    tn = 512
    return pl.pallas_call(
        _ada_kernel,
        grid=(n // tn,),
        in_specs=[pl.BlockSpec((r, d), lambda j: (0, 0)),
                  pl.BlockSpec((None, d, tn), lambda j: (layer, 0, j)),
                  pl.BlockSpec((1, tn), lambda j: (0, j))],
        out_specs=pl.BlockSpec((r, tn), lambda j: (0, j)),
        out_shape=jax.ShapeDtypeStruct((r, n), F32),
        compiler_params=_cparams("parallel"),
        name="ada_mod",
    )(c, w_all, b.reshape(1, n))


E_Q0, E_KV0, E_QI0, E_MISC0, E_BG0, E_CG0, E_XIN0, E_END = 0, 1024, 1280, 1792, 1920, 2432, 2944, 3456


def _inproj_even_kernel(x_ref, shift_ref, scale_ref, g_ref, w_ref, qg_ref, kg_ref, gone_ref, gtwo_ref,
                        q_ref, kv_ref, kvb_ref, qi_ref, misc_ref, miscb_ref, bg_ref, u_ref):
    h = _modulate(x_ref[...], g_ref[...], shift_ref[...], scale_ref[...])
    z = jnp.dot(h.astype(BF16), w_ref[...], preferred_element_type=F32)
    gone = gone_ref[...]
    for t in range(N_ATT_HEADS):
        sl = slice(t * LANE, (t + 1) * LANE)
        q_ref[:, sl] = _group_rms(z[:, E_Q0 + t * LANE:E_Q0 + (t + 1) * LANE], gone, qg_ref[:, sl]).astype(BF16)
    k = _group_rms(z[:, E_KV0:E_KV0 + LANE], gtwo_ref[...], kg_ref[...])
    v = z[:, E_KV0 + LANE:E_KV0 + 2 * LANE]
    kv_ref[:, 0:LANE] = k
    kv_ref[:, LANE:2 * LANE] = v
    kvb_ref[:, 0:LANE] = k.astype(BF16)
    kvb_ref[:, LANE:2 * LANE] = v.astype(BF16)
    qi_ref[...] = z[:, E_QI0:E_MISC0].astype(BF16)
    misc = z[:, E_MISC0:E_BG0]
    misc_ref[...] = misc
    miscb_ref[...] = misc.astype(BF16)
    bg_ref[...] = z[:, E_BG0:E_CG0]
    u_ref[...] = z[:, E_CG0:E_XIN0] * z[:, E_XIN0:E_END]


def _row_call(kernel, xs, mods, consts, outs, tile, tpb, name, outs_t=()):
    if not isinstance(xs, (list, tuple)):
        xs = [xs]
    rows = xs[0].shape[0]
    n_tiles = rows // tile
    in_specs = [pl.BlockSpec((tile, x.shape[1]), lambda t: (t, 0)) for x in xs]
    for m in mods:
        in_specs.append(pl.BlockSpec((None,) + m.shape[1:], lambda t: (t // tpb, 0, 0)))
    in_specs += [_const_spec(c) for c in consts]
    out_specs = [pl.BlockSpec((tile, w), lambda t: (t, 0)) for (w, _) in outs]
    out_shape = [jax.ShapeDtypeStruct((rows, w), dt) for (w, dt) in outs]
    out_specs += [pl.BlockSpec((hh, tile), lambda t: (0, t)) for (hh, _) in outs_t]
    out_shape += [jax.ShapeDtypeStruct((hh, rows), dt) for (hh, dt) in outs_t]
    return pl.pallas_call(kernel, grid=(n_tiles,), in_specs=in_specs, out_specs=out_specs, out_shape=out_shape,
                          compiler_params=_cparams("parallel"), name=name)(*xs, *mods, *consts)


def _pad_q_cols(wq):
    d = wq.shape[0]
    w = wq.reshape(d, N_ATT_HEADS, HEAD_DIM)
    z = jnp.zeros_like(w)
    lo = jnp.concatenate([w, z], -1)
    hi = jnp.concatenate([z, w], -1)
    sel = (jnp.arange(N_ATT_HEADS) >= GQA)[None, :, None]
    return jnp.where(sel, hi, lo).reshape(d, N_ATT_HEADS * LANE)


def _pad_o_rows(wo):
    d = wo.shape[1]
    w = wo.reshape(N_ATT_HEADS, HEAD_DIM, d)
    z = jnp.zeros_like(w)
    lo = jnp.concatenate([w, z], 1)
    hi = jnp.concatenate([z, w], 1)
    sel = (jnp.arange(N_ATT_HEADS) >= GQA)[:, None, None]
    return jnp.where(sel, hi, lo).reshape(N_ATT_HEADS * LANE, d)


def _gmats():
    i = jnp.arange(LANE)
    gone = jnp.full((LANE, LANE), 1.0 / HEAD_DIM, F32).astype(BF16)
    gtwo = jnp.where((i[:, None] // HEAD_DIM) == (i[None, :] // HEAD_DIM), 1.0 / HEAD_DIM, 0.0).astype(BF16)
    tri = jnp.where(i[:, None] <= i[None, :], 1.0, 0.0).astype(BF16)
    return gone, gtwo, tri


def _even_weights(w_in, w_out, q_norm, k_norm):
    d = w_in.shape[0]
    a_q, a_kv = N_ATT_HEADS * HEAD_DIM, 2 * N_KV_HEADS * HEAD_DIM
    o = 0
    wq = w_in[:, o:o + a_q]; o += a_q
    wkv = w_in[:, o:o + a_kv]; o += a_kv
    wqi = w_in[:, o:o + IDX_HEADS * IDX_DIM]; o += IDX_HEADS * IDX_DIM
    wki = w_in[:, o:o + IDX_DIM]; o += IDX_DIM
    wwi = w_in[:, o:o + IDX_HEADS]; o += IDX_HEADS
    wrest = w_in[:, o:]
    wqi = jnp.concatenate([wqi.reshape(d, IDX_HEADS, IDX_DIM), jnp.zeros((d, IDX_HEADS, LANE - IDX_DIM), F32)],
                          -1).reshape(d, IDX_HEADS * LANE)
    wmisc = jnp.concatenate([wki, wwi, jnp.zeros((d, LANE - IDX_DIM - IDX_HEADS), F32)], -1)
    w_in_p = jnp.concatenate([_pad_q_cols(wq), wkv, wqi, wmisc, wrest], -1).astype(BF16)
    w_out_p = jnp.concatenate([_pad_o_rows(w_out[:a_q]), w_out[a_q:]], 0).astype(BF16)
    qg = jnp.tile(q_norm, 2 * N_ATT_HEADS).reshape(1, N_ATT_HEADS * LANE)
    kg = jnp.tile(k_norm, 2).reshape(1, LANE)
    return w_in_p, w_out_p, qg, kg


def _inproj_even(x, mods, g, wts, gm, tile, tpb):
    w_in_p, _, qg, kg = wts
    gone, gtwo, _ = gm
    outs = [(N_ATT_HEADS * LANE, BF16), (2 * LANE, F32), (2 * LANE, BF16), (IDX_HEADS * LANE, BF16),
            (LANE, F32), (LANE, BF16), (CONV_CH, F32), (CONV_CH, F32)]
    return _row_call(_inproj_even_kernel, x, mods, [g, w_in_p, qg, kg, gone, gtwo], outs, tile, tpb, "inproj_even")


def _t5_bucket(dist):
    dist = jnp.maximum(dist, 0)
    exact = N_BUCKETS // 2
    far = exact + (jnp.log(jnp.maximum(dist, 1).astype(F32) / exact)
                   / math.log(MAX_DISTANCE / exact) * (N_BUCKETS - exact)).astype(I32)
    return jnp.where(dist < exact, dist, jnp.minimum(far, N_BUCKETS - 1))


def _bias_lookup(rel_bias, dist):
    onehot = (_t5_bucket(dist)[..., None] == jnp.arange(N_BUCKETS)).astype(F32)
    return jnp.einsum("...k,kh->...h", onehot, rel_bias, precision=lax.Precision.HIGHEST)


def _bias_row(rel_bias, t_pos):
    return _bias_lookup(rel_bias, t_pos - jnp.arange(t_pos + LANE)).T


def _bias_tiles(rel_bias, qb):
    r = jnp.arange(qb)[:, None]
    c = jnp.arange(LANE)[None, :]
    tiles = [_bias_lookup(rel_bias, d * LANE + r - c) for d in range(3)]
    return jnp.stack(tiles).transpose(0, 3, 1, 2)


def _flash_step(qs, kb, vb, bias, msk, m, l, acc):
    s = _dot_nt(qs, kb) * ATT_SCALE + bias
    s = jnp.where(msk, s, NEG)
    m_new = jnp.maximum(m, jnp.max(s, axis=1, keepdims=True))
    p = jnp.where(msk, jnp.exp(s - m_new), 0.0)
    alpha = jnp.exp(m - m_new)
    l = alpha * l + jnp.sum(p, axis=1, keepdims=True)
    acc = alpha * acc + jnp.dot(p.astype(BF16), vb, preferred_element_type=F32)
    return m_new, l, acc


def _flash_init(rows):
    return (jnp.full((rows, 1), NEG, F32), jnp.zeros((rows, 1), F32), jnp.zeros((rows, LANE), F32))


def _stack_heads(q_ref, hk):
    return jnp.concatenate([q_ref[:, (hk * GQA + g) * LANE:(hk * GQA + g + 1) * LANE] for g in range(GQA)], axis=0)


def _tile4(x):
    return jnp.concatenate([x] * GQA, axis=0)


def _loop(lo, hi, body, init, unroll):
    if unroll:
        for j in range(lo, hi):
            init = body(j, init)
        return init
    return lax.fori_loop(lo, hi, body, init)


def _dsa_kernel(q_ref, qi_ref, misc_ref, kidx_ref, kv_ref, bias_ref, tri_ref, o_ref, key_s,
                *, grp, qb, q_base, n_keep, single):
    q0 = q_base if single else q_base + pl.program_id(1) * qb
    nblk = (q0 + qb - 1) // LANE + 1
    dq = q0 // LANE
    seqs = range(grp)
    row = lax.broadcasted_iota(I32, (qb, LANE), 0)
    lane = lax.broadcasted_iota(I32, (qb, LANE), 1)
    t_pos = q0 + row
    wis = [[jnp.broadcast_to(misc_ref[g][:, IDX_DIM + h:IDX_DIM + h + 1], (qb, LANE)) for h in range(IDX_HEADS)]
           for g in seqs]
    idx_scale = (IDX_HEADS * IDX_DIM) ** -0.5

    def pass_a(j, c):
        off = pl.multiple_of(j * LANE, LANE)
        causal = j * LANE + lane <= t_pos
        for g in seqs:
            kb = kidx_ref[g, pl.ds(off, LANE), :]
            acc = jnp.zeros((qb, LANE), F32)
            for h in range(IDX_HEADS):
                acc = acc + jnp.maximum(_dot_nt(qi_ref[g, :, h * LANE:(h + 1) * LANE], kb), 0.0) * wis[g][h]
            sc = jnp.where(causal, acc * idx_scale, NEG)
            sc = jnp.where(sc == 0.0, 0.0, sc)
            bits = lax.bitcast_convert_type(sc, I32)
            key_s[g, j] = jnp.where(bits < 0, bits ^ jnp.int32(0x7FFFFFFF), bits)
        return c

    lax.fori_loop(0, nblk, pass_a, 0)

    def count(pred):
        def body(j, accs):
            return tuple(a + jnp.where(pred(g, key_s[g, j]), 1.0, 0.0) for g, a in zip(seqs, accs))
        accs = _loop(0, nblk, body, tuple(jnp.zeros((qb, LANE), F32) for _ in seqs), single)
        return [jnp.sum(a, axis=1, keepdims=True) for a in accs]

    keep = float(n_keep)
    int_min = jnp.int32(-2 ** 31)
    thr = tuple(jnp.where(c >= keep, jnp.int32(0), int_min) for c in count(lambda g, k: k >= 0))

    def search(it, thr):
        bit = lax.shift_left(jnp.int32(1), jnp.int32(30) - it)
        cand = [t | bit for t in thr]
        cnt = count(lambda g, k: k >= cand[g])
        return tuple(jnp.where(c >= keep, cd, t) for c, cd, t in zip(cnt, cand, thr))

    thr = lax.fori_loop(0, 31, search, thr)
    need = [keep - c for c in count(lambda g, k: k > thr[g])]
    tri = tri_ref[...]

    def pass_c(j, run):
        causal = j * LANE + lane <= t_pos
        out = []
        for g in seqs:
            key = key_s[g, j]
            eq = key == thr[g]
            eqf = jnp.where(eq, 1.0, 0.0)
            cum = jnp.dot(eqf.astype(BF16), tri, preferred_element_type=F32) + run[g]
            sel = (key > thr[g]) | (eq & (cum <= need[g]))
            key_s[g, j] = jnp.where(sel & causal, 1, 0)
            out.append(run[g] + jnp.sum(eqf, axis=1, keepdims=True))
        return tuple(out)

    lax.fori_loop(0, nblk, pass_c, tuple(jnp.zeros((qb, 1), F32) for _ in seqs))

    qs = [[_stack_heads(q_ref.at[g], hk) for hk in range(N_KV_HEADS)] for g in seqs]

    def pass_d(j, carry):
        off = pl.multiple_of(j * LANE, LANE)
        dsel = jnp.minimum(dq - j, 2)
        biases = [bias_ref[dsel, hk * GQA:(hk + 1) * GQA].reshape(GQA * qb, LANE) for hk in range(N_KV_HEADS)]
        out = []
        for g in seqs:
            kb = kv_ref[g, pl.ds(off, LANE), 0:LANE]
            vb = kv_ref[g, pl.ds(off, LANE), LANE:2 * LANE]
            msk = _tile4(key_s[g, j] > 0)
            out.append(tuple(_flash_step(qs[g][hk], kb, vb, biases[hk], msk, *carry[g][hk])
                             for hk in range(N_KV_HEADS)))
        return tuple(out)

    init = tuple(tuple(_flash_init(GQA * qb) for _ in range(N_KV_HEADS)) for _ in seqs)
    res = lax.fori_loop(0, nblk, pass_d, init)
    for g in seqs:
        _write_heads(o_ref.at[g], [acc / l for (_, l, acc) in res[g]], qb)


def _write_heads(o_ref, outs, qb):
    lane = lax.broadcasted_iota(I32, (qb, LANE), 1)
    for hk in range(N_KV_HEADS):
        valid = (lane // HEAD_DIM) == hk
        for g in range(GQA):
            h = hk * GQA + g
            o_ref[:, h * LANE:(h + 1) * LANE] = jnp.where(valid, outs[hk][g * qb:(g + 1) * qb], 0.0).astype(BF16)


def _dsa(q, qi, misc, kidx_b, kv_b, bias, tri, *, qb, q_base, n_keep, grp):
    b, rows, _ = q.shape
    nq = rows // qb
    s = kv_b.shape[1]
    kern = functools.partial(_dsa_kernel, grp=grp, qb=qb, q_base=q_base, n_keep=n_keep, single=(nq == 1))
    qspec = lambda w: pl.BlockSpec((grp, qb, w), lambda bi, i: (bi, i, 0))
    kspec = lambda w: pl.BlockSpec((grp, s, w), lambda bi, i: (bi, 0, 0))
    return pl.pallas_call(
        kern,
        grid=(b // grp, nq),
        in_specs=[qspec(N_ATT_HEADS * LANE), qspec(IDX_HEADS * LANE), qspec(LANE), kspec(LANE), kspec(2 * LANE),
                  _const_spec(bias), _const_spec(tri)],
        out_specs=qspec(N_ATT_HEADS * LANE),
        out_shape=jax.ShapeDtypeStruct((b, rows, N_ATT_HEADS * LANE), BF16),
        scratch_shapes=[pltpu.VMEM((grp, s // LANE, qb, LANE), I32)],
        compiler_params=_cparams("parallel", "parallel"),
        name="dsa_attention",
    )(q, qi, misc, kidx_b, kv_b, bias, tri)


QB = LANE
QW = GQA * QB


def _sub_sum(x):
    return jnp.sum(x, axis=0, keepdims=True)


def _flash_t_pair(blocks, qs, bias_ref, carry, acc_ref):
    logits = [[jnp.where(mk[hk], _dot_nt(kb, qs[hk]) + bias_ref[dsel, hk], NEG) for (kb, _, mk, dsel) in blocks]
              for hk in range(N_KV_HEADS)]
    out = []
    for hk in range(N_KV_HEADS):
        m, l = carry[hk]
        m_new = m
        for s in logits[hk]:
            m_new = jnp.maximum(m_new, jnp.max(s, axis=0, keepdims=True))
        alpha = jnp.exp(m - m_new)
        l = alpha * l
        pv = None
        for s, (_, vt, _, _) in zip(logits[hk], blocks):
            p = jnp.exp(s - m_new)
            l = l + _sub_sum(p)
            d = jnp.dot(vt, p.astype(BF16), preferred_element_type=F32)
            pv = d if pv is None else pv + d
        acc_ref[hk] = alpha * acc_ref[hk] + pv
        out.append((m_new, l))
    return tuple(out)


def _scaled_queries(q_ref, hk):
    return (_stack_heads(q_ref, hk).astype(F32) * ATT_SCALE).astype(BF16)


def _pair_loop(nblk, body, init):
    def body2(jj, c):
        return body(2 * jj + 1, body(2 * jj, c))
    return lax.fori_loop(0, (nblk + 1) // 2, body2, init)


def _flash_t_init():
    return (jnp.full((1, QW), NEG, F32), jnp.zeros((1, QW), F32))


def _tile_lanes(x):
    return jnp.concatenate([x] * GQA, axis=1)


def _write_heads_t(o_ref, o_ts):
    lane = lax.broadcasted_iota(I32, (QB, LANE), 1)
    for hk in range(N_KV_HEADS):
        valid = (lane // HEAD_DIM) == hk
        for g in range(GQA):
            h = hk * GQA + g
            o = o_ts[hk][:, g * QB:(g + 1) * QB].T
            o_ref[:, h * LANE:(h + 1) * LANE] = jnp.where(valid, o, 0.0).astype(BF16)


def _dsa_t_kernel(q_ref, qi_ref, misc_ref, kidx_ref, k_ref, vt_ref, bias_ref, trit_ref, o_ref, key_s, acc_s,
                  *, n_keep):
    i = pl.program_id(1)
    q0 = i * QB
    nblk = i + 1
    krow = lax.broadcasted_iota(I32, (LANE, QB), 0)
    qcol = lax.broadcasted_iota(I32, (LANE, QB), 1)
    misc_t = misc_ref[...].T
    wis = [_bf16_round(misc_t[IDX_DIM + h:IDX_DIM + h + 1, :]) for h in range(IDX_HEADS)]
    qi = qi_ref[...]
    idx_scale = (IDX_HEADS * IDX_DIM) ** -0.5

    def causal(j):
        return j * LANE + krow <= q0 + qcol

    def pass_a(j, c):
        kb = kidx_ref[pl.ds(pl.multiple_of(j * LANE, LANE), LANE), :]
        acc = jnp.zeros((LANE, QB), F32)
        for h in range(IDX_HEADS):
            acc = acc + _bf16_round(jnp.maximum(_dot_nt(kb, qi[:, h * LANE:(h + 1) * LANE]), 0.0)) * wis[h]
        key_s[j] = _order_keys(jnp.where(causal(j), acc * idx_scale, NEG))
        return c

    _pair_loop(nblk, pass_a, 0)

    def count(pred):
        def body(j, a):
            return a + jnp.where(pred(key_s[j]), 1.0, 0.0)
        return _sub_sum(_pair_loop(nblk, body, jnp.zeros((LANE, QB), F32)))

    keep = float(n_keep)
    thr = jnp.where(count(lambda k: k >= 0) >= keep, jnp.int32(0), jnp.int32(-2 ** 31))

    def search(it, thr):
        cand = thr | lax.shift_left(jnp.int32(1), jnp.int32(30) - it)
        return jnp.where(count(lambda k: k >= cand) >= keep, cand, thr)

    thr = lax.fori_loop(0, 31, search, thr)
    need = keep - count(lambda k: k > thr)
    trit = trit_ref[...]

    def pass_c(j, run):
        key = key_s[j]
        eq = key == thr
        eqf = jnp.where(eq, 1.0, 0.0)
        cum = jnp.dot(trit, eqf.astype(BF16), preferred_element_type=F32) + run
        sel = ((key > thr) | (eq & (cum <= need))) & causal(j)
        key_s[j] = jnp.where(sel, 1, 0)
        return run + _sub_sum(eqf)

    _pair_loop(nblk, pass_c, jnp.zeros((1, QB), F32))

    qs = [_scaled_queries(q_ref, hk) for hk in range(N_KV_HEADS)]
    acc_s[...] = jnp.zeros_like(acc_s)

    def pass_d(jj, carry):
        blocks = []
        for j in (2 * jj, 2 * jj + 1):
            kb = k_ref[pl.ds(pl.multiple_of(j * LANE, LANE), LANE), :]
            blocks.append((kb, vt_ref[j], [_tile_lanes(key_s[j] > 0)] * N_KV_HEADS, jnp.clip(i - j, 0, 2)))
        return _flash_t_pair(blocks, qs, bias_ref, carry, acc_s)

    res = lax.fori_loop(0, (nblk + 1) // 2, pass_d, tuple(_flash_t_init() for _ in range(N_KV_HEADS)))
    _write_heads_t(o_ref, [acc_s[hk] / res[hk][1] for hk in range(N_KV_HEADS)])


def _bias_tiles_t(rel_bias):
    t = _bias_tiles(rel_bias, QB)
    t = t.reshape(3, N_KV_HEADS, GQA, QB, LANE).transpose(0, 1, 4, 2, 3)
    return t.reshape(3, N_KV_HEADS, LANE, QW)


def _blocks_t(x):
    b, s, w = x.shape
    return x.reshape(b, s // LANE, LANE, w).transpose(0, 1, 3, 2)


def _dsa_t(q, qi, misc, kidx_b, kv_b, bias_t, trit, *, n_keep):
    b, s, _ = q.shape
    vt = _blocks_t(kv_b[:, :, LANE:])
    qspec = lambda w: pl.BlockSpec((None, QB, w), lambda bi, i: (bi, i, 0))
    kspec = pl.BlockSpec((None, s, LANE), lambda bi, i: (bi, 0, 0))
    return pl.pallas_call(
        functools.partial(_dsa_t_kernel, n_keep=n_keep),
        grid=(b, s // QB),
        in_specs=[qspec(N_ATT_HEADS * LANE), qspec(IDX_HEADS * LANE), qspec(LANE), kspec, kspec,
                  pl.BlockSpec((None, s // LANE, LANE, LANE), lambda bi, i: (bi, 0, 0, 0)),
                  _const_spec(bias_t), _const_spec(trit)],
        out_specs=qspec(N_ATT_HEADS * LANE),
        out_shape=jax.ShapeDtypeStruct((b, s, N_ATT_HEADS * LANE), BF16),
        scratch_shapes=[pltpu.VMEM((s // LANE, LANE, QB), I32), pltpu.VMEM((N_KV_HEADS, LANE, QW), F32)],
        compiler_params=_cparams("parallel", "parallel"),
        name="dsa_attention_t",
    )(q, qi, misc, kidx_b, kv_b, vt, bias_t, trit)


def _dot_split_rhs(m, x):
    hi = x.astype(BF16)
    r1 = x - hi.astype(F32)
    mid = r1.astype(BF16)
    lo = (r1 - mid.astype(F32)).astype(BF16)
    return (jnp.dot(m, hi, preferred_element_type=F32) + jnp.dot(m, mid, preferred_element_type=F32)
            + jnp.dot(m, lo, preferred_element_type=F32))


def _nsa_t_kernel(q_ref, gates_ref, kc_ref, vct_ref, ks_ref, vst_ref, kw_ref, vwt_ref, bias_ref, biasc_ref,
                  pairt_ref, o_ref, acc_s, *, n_cmp, n_sel):
    i = pl.program_id(1)
    q0 = i * QB
    nblk = i + 1
    heads = range(N_KV_HEADS)
    krow = lax.broadcasted_iota(I32, (LANE, QB), 0)
    qcol = lax.broadcasted_iota(I32, (LANE, QB), 1)
    t_pos = q0 + qcol
    krow_f = krow.astype(F32)
    qs = [_scaled_queries(q_ref, hk) for hk in heads]
    gates_t = gates_ref[...].T

    def gate_row(br, hk):
        return jnp.concatenate([gates_t[br * N_ATT_HEADS + hk * GQA + g:br * N_ATT_HEADS + hk * GQA + g + 1, :]
                                for g in range(GQA)], axis=1)

    mask_c = _tile_lanes(((krow * CMP_BLOCK + CMP_BLOCK - 1) <= t_pos) & (krow < n_cmp))
    cur = t_pos // SEL_BLOCK
    forced = (krow == 0) | (krow == cur) | (krow == cur - 1)
    sel_causal = krow * SEL_BLOCK <= t_pos
    o_cmp, picked = [], []
    for hk in heads:
        s = jnp.where(mask_c, _dot_nt(kc_ref[...], qs[hk]) + biasc_ref[hk], NEG)
        e = jnp.exp(s - jnp.max(s, axis=0, keepdims=True))
        p = jnp.where(mask_c, e / _sub_sum(e), 0.0)
        o_cmp.append(jnp.dot(vct_ref[...], p.astype(BF16), preferred_element_type=F32))
        ps = p[:, 0:QB]
        for g in range(1, GQA):
            ps = ps + p[:, g * QB:(g + 1) * QB]
        score = _dot_split_rhs(pairt_ref[...], ps)
        score = jnp.where(sel_causal, jnp.where(forced, FORCE, score), NEG)
        pk = jnp.zeros((LANE, QB), F32)
        for _ in range(n_sel):
            mx = jnp.max(score, axis=0, keepdims=True)
            first = jnp.min(jnp.where(score == mx, krow_f, float(LANE)), axis=0, keepdims=True)
            hit = krow_f == first
            pk = jnp.where(hit, 1.0, pk)
            score = jnp.where(hit, TAKEN, score)
        picked.append(pk.astype(BF16))

    def key_block(k_ref, vt_ref, j):
        return k_ref[pl.ds(pl.multiple_of(j * LANE, LANE), LANE), :], vt_ref[j]

    erow = lax.broadcasted_iota(I32, (LANE, LANE), 0)
    ecol = lax.broadcasted_iota(I32, (LANE, LANE), 1)
    acc_s[...] = jnp.zeros_like(acc_s)

    def slc_body(jj, carry):
        blocks = []
        for j in (2 * jj, 2 * jj + 1):
            kb, vt = key_block(ks_ref, vst_ref, j)
            expand = jnp.where(ecol == 2 * j + erow // SEL_BLOCK, 1.0, 0.0).astype(BF16)
            causal = j * LANE + krow <= t_pos
            masks = [_tile_lanes((jnp.dot(expand, picked[hk], preferred_element_type=F32) > 0.5) & causal)
                     for hk in heads]
            blocks.append((kb, vt, masks, jnp.clip(i - j, 0, 2)))
        return _flash_t_pair(blocks, qs, bias_ref, carry, acc_s.at[0])

    res_s = lax.fori_loop(0, (nblk + 1) // 2, slc_body, tuple(_flash_t_init() for _ in heads))

    lo = jnp.maximum(i - WINDOW // LANE - 1, 0)

    def win_body(jj, carry):
        blocks = []
        for j in (lo + 2 * jj, lo + 2 * jj + 1):
            kb, vt = key_block(kw_ref, vwt_ref, j)
            dist = t_pos - (j * LANE + krow)
            mask = _tile_lanes((dist >= 0) & (dist < WINDOW))
            blocks.append((kb, vt, [mask] * N_KV_HEADS, jnp.clip(i - j, 0, 2)))
        return _flash_t_pair(blocks, qs, bias_ref, carry, acc_s.at[1])

    res_w = lax.fori_loop(0, (i - lo + 2) // 2, win_body, tuple(_flash_t_init() for _ in heads))

    _write_heads_t(o_ref, [gate_row(0, hk) * o_cmp[hk] + gate_row(1, hk) * (acc_s[0, hk] / res_s[hk][1])
                           + gate_row(2, hk) * (acc_s[1, hk] / res_w[hk][1]) for hk in heads])


def _bias_cmp_t(rel_bias, s):
    t = _bias_cmp(rel_bias, [j * QB for j in range(s // QB)], QB)
    t = t.reshape(s // QB, N_KV_HEADS, GQA, QB, LANE).transpose(0, 1, 4, 2, 3)
    return t.reshape(s // QB, N_KV_HEADS, LANE, QW)


def _nsa_t(q, gates, kcv, kvs_b, kvw_b, bias_t, bias_c, pair_t, *, n_cmp, n_sel):
    b, s, _ = q.shape
    vct = kcv[:, :, LANE:].transpose(0, 2, 1)
    qspec = lambda w: pl.BlockSpec((None, QB, w), lambda bi, i: (bi, i, 0))
    kspec = pl.BlockSpec((None, s, LANE), lambda bi, i: (bi, 0, 0))
    vspec = pl.BlockSpec((None, s // LANE, LANE, LANE), lambda bi, i: (bi, 0, 0, 0))
    cspec = pl.BlockSpec((None, LANE, LANE), lambda bi, i: (bi, 0, 0))
    return pl.pallas_call(
        functools.partial(_nsa_t_kernel, n_cmp=n_cmp, n_sel=n_sel),
        grid=(b, s // QB),
        in_specs=[qspec(N_ATT_HEADS * LANE), qspec(LANE), cspec, cspec, kspec, vspec, kspec, vspec,
                  _const_spec(bias_t), pl.BlockSpec((None,) + bias_c.shape[1:], lambda bi, i: (i, 0, 0, 0)),
                  _const_spec(pair_t)],
        out_specs=qspec(N_ATT_HEADS * LANE),
        out_shape=jax.ShapeDtypeStruct((b, s, N_ATT_HEADS * LANE), BF16),
        scratch_shapes=[pltpu.VMEM((2, N_KV_HEADS, LANE, QW), F32)],
        compiler_params=_cparams("parallel", "parallel"),
        name="nsa_attention_t",
    )(q, gates, kcv, vct, kvs_b, _blocks_t(kvs_b[:, :, LANE:]), kvw_b, _blocks_t(kvw_b[:, :, LANE:]),
      bias_t, bias_c, pair_t)


def _select_top(keys, n_keep, tri):
    keep = float(n_keep)
    n = keys[0].shape[1]

    def count(pred):
        return [jnp.sum(jnp.where(pred(g, k), 1.0, 0.0), axis=1, keepdims=True) for g, k in enumerate(keys)]

    int_min = jnp.int32(-2 ** 31)
    thr = tuple(jnp.where(c >= keep, jnp.int32(0), int_min) for c in count(lambda g, k: k >= 0))

    def search(it, thr):
        bit = lax.shift_left(jnp.int32(1), jnp.int32(30) - it)
        cand = [t | bit for t in thr]
        cnt = count(lambda g, k: k >= cand[g])
        return tuple(jnp.where(c >= keep, cd, t) for c, cd, t in zip(cnt, cand, thr))

    thr = lax.fori_loop(0, 31, search, thr)
    need = [keep - c for c in count(lambda g, k: k > thr[g])]
    sels = []
    for g, k in enumerate(keys):
        run = jnp.zeros((1, 1), F32)
        parts = []
        for t in range(n // LANE):
            kt = k[:, t * LANE:(t + 1) * LANE]
            eq = kt == thr[g]
            eqf = jnp.where(eq, 1.0, 0.0)
            cum = jnp.dot(eqf.astype(BF16), tri, preferred_element_type=F32) + run
            parts.append((kt > thr[g]) | (eq & (cum <= need[g])))
            run = run + jnp.sum(eqf, axis=1, keepdims=True)
        sels.append(jnp.concatenate(parts, axis=1))
    return sels


def _order_keys(score):
    score = jnp.where(score == 0.0, 0.0, score)
    bits = lax.bitcast_convert_type(score, I32)
    return jnp.where(bits < 0, bits ^ jnp.int32(0x7FFFFFFF), bits)


def _head_rows_out(o_ref, g, acc):
    rowh = lax.broadcasted_iota(I32, (N_ATT_HEADS, LANE), 0)
    laneh = lax.broadcasted_iota(I32, (N_ATT_HEADS, LANE), 1)
    o_ref[g] = jnp.where((laneh // HEAD_DIM) == (rowh // GQA), acc, 0.0).astype(BF16)


def _dsa_sample_kernel(pt_ref, q_ref, qi_ref, wi_ref, knew_ref, kvnew_ref, bias_ref, tri_ref, *refs,
                       grp, n_pages, n_keep):
    del pt_ref
    ki_refs, kv_refs, o_ref = refs[:grp * n_pages], refs[grp * n_pages:2 * grp * n_pages], refs[-1]
    lane1 = lax.broadcasted_iota(I32, (1, LANE), 1)
    idx_scale = (IDX_HEADS * IDX_DIM) ** -0.5
    keys = []
    for g in range(grp):
        qi = qi_ref[g]
        wi = _bf16_round(wi_ref[g])
        tiles = []
        for p in range(n_pages):
            rel = _bf16_round(jnp.maximum(_dot(qi, ki_refs[g * n_pages + p][...]), 0.0))
            tiles.append(jnp.sum(rel * wi, axis=0, keepdims=True) * idx_scale)
        rel_new = _bf16_round(jnp.maximum(jnp.sum(qi.astype(F32) * _bf16_round(knew_ref[g]), axis=1, keepdims=True),
                                          0.0))
        sc_new = jnp.sum(rel_new * wi, axis=0, keepdims=True) * idx_scale
        tiles.append(jnp.where(lane1 == 0, sc_new, NEG))
        keys.append(_order_keys(jnp.concatenate(tiles, axis=1)))
    sels = _select_top(keys, n_keep, tri_ref[...])
    bias = bias_ref[...]
    for g in range(grp):
        q = q_ref[g]
        kvnew = _bf16_round(kvnew_ref[g])
        tiles = [_dot(q, kv_refs[g * n_pages + p][0:LANE, :]) for p in range(n_pages)]
        s_new = jnp.sum(q.astype(F32) * kvnew[:, 0:LANE], axis=1, keepdims=True)
        tiles.append(jnp.where(lane1 == 0, s_new, 0.0))
        valid = sels[g] & (jnp.concatenate([lane1] * n_pages + [lane1 + LANE], axis=1) <= LANE)
        s = jnp.where(valid, jnp.concatenate(tiles, axis=1) * ATT_SCALE + bias, NEG)
        e = jnp.exp(s - jnp.max(s, axis=1, keepdims=True))
        p_all = jnp.where(valid, e / jnp.sum(e, axis=1, keepdims=True), 0.0)
        acc = p_all[:, n_pages * LANE:n_pages * LANE + 1] * kvnew[:, LANE:2 * LANE]
        for p in range(n_pages):
            acc = acc + _dot_nt(p_all[:, p * LANE:(p + 1) * LANE], kv_refs[g * n_pages + p][LANE:2 * LANE, :])
        _head_rows_out(o_ref, g, acc)


def _page_specs(pool_t, n_pages, grp):
    r, c = pool_t.shape[1:]
    return [pl.BlockSpec((None, r, c), lambda i, pt, g=g, p=p: (pt[i * grp + g, p], 0, 0))
            for g in range(grp) for p in range(n_pages)]


def _dsa_sample(q8, qi8, wi8, knew, kvnew, bias, tri, ki_t, kv_t, page_table, *, n_keep, grp):
    b, n_pages = page_table.shape
    gspec = lambda a: pl.BlockSpec((grp,) + a.shape[1:], lambda i, pt: (i,) + (0,) * (a.ndim - 1))
    cspec = lambda a: pl.BlockSpec(a.shape, lambda i, pt: (0,) * a.ndim)
    kern = functools.partial(_dsa_sample_kernel, grp=grp, n_pages=n_pages, n_keep=n_keep)
    return pl.pallas_call(
        kern,
        grid_spec=pltpu.PrefetchScalarGridSpec(
            num_scalar_prefetch=1, grid=(b // grp,),
            in_specs=[gspec(q8), gspec(qi8), gspec(wi8), gspec(knew), gspec(kvnew), cspec(bias), cspec(tri)]
            + _page_specs(ki_t, n_pages, grp) + _page_specs(kv_t, n_pages, grp),
            out_specs=pl.BlockSpec((grp, N_ATT_HEADS, LANE), lambda i, pt: (i, 0, 0))),
        out_shape=jax.ShapeDtypeStruct((b, N_ATT_HEADS, LANE), BF16),
        compiler_params=_cparams("parallel"),
        name="dsa_sample",
    )(page_table, q8, qi8, wi8, knew, kvnew, bias, tri, *([ki_t] * (grp * n_pages)), *([kv_t] * (grp * n_pages)))


def _route(hf, wrt, br):
    logits = _dot_nt(wrt, hf)
    s = _sigmoid(logits)
    sel = s + br
    rows = [sel[e:e + 1, :] for e in range(N_EXPERTS)]
    grp = []
    for g in range(N_GROUPS):
        a = rows[g * EXPERTS_PER_GROUP:(g + 1) * EXPERTS_PER_GROUP]
        best = None
        for i in range(EXPERTS_PER_GROUP):
            for j in range(i + 1, EXPERTS_PER_GROUP):
                v = a[i] + a[j]
                best = v if best is None else jnp.maximum(best, v)
        grp.append(best)
    gbest = jnp.zeros_like(grp[0], dtype=I32)
    cur = grp[0]
    for g in range(1, N_GROUPS):
        better = grp[g] > cur
        gbest = jnp.where(better, g, gbest)
        cur = jnp.where(better, grp[g], cur)
    picked = []
    for g in range(N_GROUPS):
        a = rows[g * EXPERTS_PER_GROUP:(g + 1) * EXPERTS_PER_GROUP]
        for j in range(EXPERTS_PER_GROUP):
            rank = jnp.zeros_like(a[j])
            for jj in range(EXPERTS_PER_GROUP):
                if jj != j:
                    ahead = (a[jj] > a[j]) | (a[jj] == a[j]) if jj < j else (a[jj] > a[j])
                    rank = rank + jnp.where(ahead, 1.0, 0.0)
            e = g * EXPERTS_PER_GROUP + j
            picked.append(jnp.where((gbest == g) & (rank < 2.0), s[e:e + 1, :], 0.0))
    den = picked[0]
    for p in picked[1:]:
        den = den + p
    return jnp.concatenate([p / den for p in picked], axis=0)


def _post_tail(x, mix, gate, gf, shf, scf, wrt_ref, br_ref, x2_ref, hf_ref, cwt_ref):
    x2 = x + gate * mix
    x2_ref[...] = x2
    hf = _modulate(x2, gf, shf, scf)
    hf_ref[...] = hf.astype(BF16)
    cwt_ref[...] = _route(hf, wrt_ref[...], br_ref[...])


def _post_even_kernel(x_ref, oa_ref, bg_ref, u_ref, um1_ref, um2_ref, gate_ref, shf_ref, scf_ref,
                      gf_ref, cw_ref, cb_ref, wo_ref, wrt_ref, br_ref, x2_ref, hf_ref, cwt_ref):
    cw = cw_ref[...]
    y = cb_ref[...] + cw[0:1, :] * um2_ref[...]
    y = y + cw[1:2, :] * um1_ref[...]
    y = y + cw[2:3, :] * u_ref[...]
    n_a = N_ATT_HEADS * LANE
    mix = (jnp.dot(oa_ref[...], wo_ref[0:n_a, :], preferred_element_type=F32)
           + jnp.dot((bg_ref[...] * y).astype(BF16), wo_ref[n_a:n_a + CONV_CH, :], preferred_element_type=F32))
    _post_tail(x_ref[...], mix, gate_ref[...], gf_ref[...], shf_ref[...], scf_ref[...], wrt_ref, br_ref,
               x2_ref, hf_ref, cwt_ref)


POST_OUTS = [(1024, F32), (1024, BF16)]


def _moe_kernel(hf_ref, cw_ref, x2_ref, gate_ref, wg_ref, wu_ref, wd_ref, o_ref, acc_ref):
    e = pl.program_id(1)

    @pl.when(e == 0)
    def _():
        acc_ref[...] = jnp.zeros_like(acc_ref)

    hf = hf_ref[...]
    hmid = _silu(_dot(hf, wg_ref[...])) * _dot(hf, wu_ref[...])
    cw = cw_ref[...]
    lane = lax.broadcasted_iota(I32, cw.shape, 1)
    wcol = jnp.sum(jnp.where(lane == e, cw, 0.0), axis=1, keepdims=True)
    acc_ref[...] += _dot(hmid, wd_ref[...]) * wcol

    @pl.when(e == N_EXPERTS - 1)
    def _():
        o_ref[...] = x2_ref[...] + gate_ref[...] * acc_ref[...]


def _moe(hf, cw, x2, gate, wg, wu, wd, layer, tile, tpb):
    rows, d = x2.shape
    de = wg.shape[3]
    return pl.pallas_call(
        _moe_kernel,
        grid=(rows // tile, N_EXPERTS),
        in_specs=[pl.BlockSpec((tile, d), lambda t, e: (t, 0)),
                  pl.BlockSpec((tile, N_EXPERTS), lambda t, e: (t, 0)),
                  pl.BlockSpec((tile, d), lambda t, e: (t, 0)),
                  pl.BlockSpec((None,) + gate.shape[1:], lambda t, e: (t // tpb, 0, 0)),
                  pl.BlockSpec((None, None, d, de), lambda t, e: (layer, e, 0, 0)),
                  pl.BlockSpec((None, None, d, de), lambda t, e: (layer, e, 0, 0)),
                  pl.BlockSpec((None, None, de, d), lambda t, e: (layer, e, 0, 0))],
        out_specs=pl.BlockSpec((tile, d), lambda t, e: (t, 0)),
        out_shape=jax.ShapeDtypeStruct((rows, d), F32),
        scratch_shapes=[pltpu.VMEM((tile, d), F32)],
        compiler_params=_cparams("parallel", "arbitrary"),
        name="moe_dense",
    )(hf, cw, x2, gate, wg, wu, wd)


def _shift_rows(u3, k):
    return jnp.pad(u3, ((0, 0), (k, 0), (0, 0)))[:, :u3.shape[1]]


def _even_layer(xp, xs, mp, ms, page_table, kv_pool, kidx_pool, conv_buf, rel_bias, router, experts,
                g_mix, g_ffn, w_in, w_out, q_norm, k_norm, conv_w, conv_b):
    B, S, D = xp.shape
    Bs = xs.shape[0]
    past = page_table.shape[1] * PAGE_SIZE
    wts = _even_weights(w_in, w_out, q_norm, k_norm)
    gm = _gmats()
    wrt, br = router
    wg, wu, wd, lyr = experts
    g_mix = g_mix.reshape(1, D)
    g_ffn = g_ffn.reshape(1, D)
    post_consts = [g_ffn, conv_w, conv_b.reshape(1, CONV_CH), wts[1], wrt, br]

    tp = ROW_TILE
    q_p, kv_f, kv_b, qi_p, misc, misc_b, bg, u = _inproj_even(xp.reshape(B * S, D), [mp[0], mp[1]], g_mix, wts, gm,
                                                              tp, S // tp)
    r3 = lambda a: a.reshape(B, S, a.shape[-1])
    oa = _dsa_t(r3(q_p), r3(qi_p), r3(misc), r3(misc_b), r3(kv_b), _bias_tiles_t(rel_bias), gm[2].T,
                n_keep=min(TOPK_MAX, S // 4))
    u3 = r3(u)
    um1 = _shift_rows(u3, 1).reshape(B * S, CONV_CH)
    um2 = _shift_rows(u3, 2).reshape(B * S, CONV_CH)
    x2, hf, cwt = _row_call(_post_even_kernel, [xp.reshape(B * S, D), oa.reshape(B * S, -1), bg, u, um1, um2],
                            [mp[2], mp[3], mp[4]], post_consts, POST_OUTS, tp, S // tp, "post_even",
                            outs_t=[(N_EXPERTS, F32)])
    tm = MOE_TILE
    xp3 = _moe(hf, cwt.T, x2, mp[5], wg, wu, wd, lyr, tm, S // tm).reshape(B, S, D)
    outs_p = (kv_f.reshape(B, S, 2, N_KV_HEADS, HEAD_DIM), misc[:, :IDX_DIM].reshape(B, S, IDX_DIM), u3[:, S - 2:])

    q_s, kv_fs, _, qi_s, misc_s, _, bg_s, u_s = _inproj_even(xs, [ms[0], ms[1]], g_mix, wts, gm, Bs, 1)
    n_pool = kv_pool.shape[0]
    kv_t = kv_pool.transpose(0, 2, 3, 4, 1).reshape(n_pool, 2 * LANE, PAGE_SIZE)
    ki_t = kidx_pool.transpose(0, 2, 1)
    qi8 = jnp.pad(qi_s.reshape(Bs, IDX_HEADS, LANE)[:, :, :IDX_DIM], ((0, 0), (0, N_ATT_HEADS - IDX_HEADS), (0, 0)))
    wi8 = jnp.pad(misc_s[:, IDX_DIM:IDX_DIM + IDX_HEADS], ((0, 0), (0, N_ATT_HEADS - IDX_HEADS)))[:, :, None]
    oa_s = _dsa_sample(q_s.reshape(Bs, N_ATT_HEADS, LANE), qi8, wi8, misc_s[:, None, :IDX_DIM], kv_fs[:, None, :],
                       _bias_row(rel_bias, past), gm[2], ki_t, kv_t, page_table,
                       n_keep=min(TOPK_MAX, (past + 1) // 4), grp=_pick_group(Bs, SAMPLE_GROUP)).reshape(Bs, -1)
    x2s, hfs, cwts = _row_call(_post_even_kernel, [xs, oa_s, bg_s, u_s, conv_buf[:, 1], conv_buf[:, 0]],
                               [ms[2], ms[3], ms[4]], post_consts, POST_OUTS, Bs, 1, "post_even_s",
                               outs_t=[(N_EXPERTS, F32)])
    xs3 = _moe(hfs, cwts.T, x2s, ms[5], wg, wu, wd, lyr, Bs, 1)
    outs_s = (kv_fs.reshape(Bs, 1, 2, N_KV_HEADS, HEAD_DIM), misc_s[:, None, :IDX_DIM],
              jnp.concatenate([conv_buf[:, 1:], u_s[:, None, :]], axis=1))
    return xp3, xs3, outs_p, outs_s


O_Z0, O_Q0, O_KVC0, O_KVS0, O_KVW0, O_G0, O_END = 0, 1792, 2816, 3072, 3328, 3584, 3712
P_C = 3 * RWKV_DIM + LORA_W + LORA_A + LORA_G


def _inproj_odd_kernel(x_ref, shift_ref, scale_ref, g_ref, w_ref, qg_ref, ksg_ref, kwg_ref, gone_ref, gtwo_ref,
                       zc_ref, q_ref, kvc_ref, kvs_ref, kvsb_ref, kvw_ref, kvwb_ref, gates_ref):
    h = _modulate(x_ref[...], g_ref[...], shift_ref[...], scale_ref[...])
    z = jnp.dot(h.astype(BF16), w_ref[...], preferred_element_type=F32)
    zc_ref[...] = z[:, O_Z0:O_Q0]
    gone = gone_ref[...]
    gtwo = gtwo_ref[...]
    for t in range(N_ATT_HEADS):
        sl = slice(t * LANE, (t + 1) * LANE)
        q_ref[:, sl] = _group_rms(z[:, O_Q0 + t * LANE:O_Q0 + (t + 1) * LANE], gone, qg_ref[:, sl]).astype(BF16)
    kvc_ref[...] = z[:, O_KVC0:O_KVS0]
    for base, gain_ref, f_ref, b_ref in ((O_KVS0, ksg_ref, kvs_ref, kvsb_ref), (O_KVW0, kwg_ref, kvw_ref, kvwb_ref)):
        k = _group_rms(z[:, base:base + LANE], gtwo, gain_ref[...])
        v = z[:, base + LANE:base + 2 * LANE]
        f_ref[:, 0:LANE] = k
        f_ref[:, LANE:2 * LANE] = v
        b_ref[:, 0:LANE] = k.astype(BF16)
        b_ref[:, LANE:2 * LANE] = v.astype(BF16)
    gates_ref[...] = _sigmoid(z[:, O_G0:O_END])


def _odd_weights(w_in, w_out, q_norm, k_norm):
    d = w_in.shape[0]
    a_q, a_kv = N_ATT_HEADS * HEAD_DIM, 2 * N_KV_HEADS * HEAD_DIM
    o = P_C
    wz = w_in[:, :o]
    wq = w_in[:, o:o + a_q]; o += a_q
    wkv = w_in[:, o:o + 3 * a_kv]; o += 3 * a_kv
    wg = w_in[:, o:]
    wg = jnp.concatenate([wg, jnp.zeros((d, LANE - wg.shape[1]), F32)], -1)
    w_in_p = jnp.concatenate([wz, _pad_q_cols(wq), wkv, wg], -1).astype(BF16)
    w_out_p = jnp.concatenate([w_out[:RWKV_DIM], _pad_o_rows(w_out[RWKV_DIM:])], 0).astype(BF16)
    qg = jnp.tile(q_norm, 2 * N_ATT_HEADS).reshape(1, N_ATT_HEADS * LANE)
    ksg = jnp.tile(k_norm[1], 2).reshape(1, LANE)
    kwg = jnp.tile(k_norm[2], 2).reshape(1, LANE)
    return w_in_p, w_out_p, qg, ksg, kwg


def _inproj_odd(x, mods, g, wts, gm, tile, tpb):
    w_in_p, _, qg, ksg, kwg = wts
    gone, gtwo, _ = gm
    outs = [(P_C, F32), (N_ATT_HEADS * LANE, BF16), (2 * LANE, F32), (2 * LANE, F32), (2 * LANE, BF16),
            (2 * LANE, F32), (2 * LANE, BF16), (LANE, F32)]
    return _row_call(_inproj_odd_kernel, x, mods, [g, w_in_p, qg, ksg, kwg, gone, gtwo], outs, tile, tpb,
                     "inproj_odd")


def _rwkv_pre_kernel(z_ref, zp_ref, mu_ref, w0_ref, a0_ref, kk_ref, ka_ref, wup_ref, aup_ref, gup_ref, gsum_ref,
                     r_o, w_o, k_o, v_o, kk_o, kka_o, g_o):
    z = z_ref[...]
    zm = z + (zp_ref[...] - z) * mu_ref[...]
    r = zm[:, 0:RWKV_DIM]
    k = zm[:, RWKV_DIM:2 * RWKV_DIM]
    v = zm[:, 2 * RWKV_DIM:3 * RWKV_DIM]
    t12 = zm[:, 3 * RWKV_DIM:3 * RWKV_DIM + LANE]
    gd = zm[:, 3 * RWKV_DIM + LANE:P_C]
    xw = w0_ref[...] + _dot(jnp.tanh(t12), wup_ref[...])
    sp = jnp.maximum(-xw, 0.0) + jnp.log(1.0 + jnp.exp(-jnp.abs(xw)))
    w_o[...] = jnp.exp(-jnp.exp(-sp - 0.5))
    a = _sigmoid(a0_ref[...] + _dot(t12, aup_ref[...]))
    g_o[...] = _dot(_sigmoid(gd), gup_ref[...])
    kk = k * kk_ref[...]
    gsum = gsum_ref[...]
    for t in range(RWKV_DIM // LANE):
        sl = slice(t * LANE, (t + 1) * LANE)
        kt = kk[:, sl]
        nrm = jnp.maximum(jnp.sqrt(_dot_split(kt * kt, gsum)), 1e-12)
        kn = kt / nrm
        kk_o[:, sl] = kn
        kka_o[:, sl] = kn * a[:, sl]
    r_o[...] = r
    v_o[...] = v
    k_o[...] = k * (1.0 + (a - 1.0) * ka_ref[...])


def _rwkv_pre(zc, zprev, cpar, gsum, tile):
    mu, w0, w_up, a0, a_up, g_up, k_k, k_a = cpar
    z64 = jnp.zeros((LORA_W, RWKV_DIM), F32)
    consts = [mu.reshape(1, P_C), w0.reshape(1, -1), a0.reshape(1, -1), k_k.reshape(1, -1), k_a.reshape(1, -1),
              jnp.concatenate([w_up, z64], 0).astype(BF16), jnp.concatenate([z64, a_up], 0).astype(BF16),
              g_up.astype(BF16), gsum]
    return _row_call(_rwkv_pre_kernel, [zc, zprev], [], consts, [(RWKV_DIM, F32)] * 7, tile, 1, "rwkv_pre")


SCAN_P = 64
SCAN_VH = HEAD_DIM // 2


def _scan_kernel(kk_ref, w_ref, kka_ref, k_ref, r_ref, v_ref, s0_ref, y_ref, so_ref, st, *, tc):
    ti = pl.program_id(1)

    @pl.when(ti == 0)
    def _():
        st[...] = s0_ref[...]

    def step(t, c):
        kk, w, kka, kt, rt, vt = kk_ref[t], w_ref[t], kka_ref[t], k_ref[t], r_ref[t], v_ref[t]
        ys = []
        for vi in range(SCAN_VH):
            s = st[vi]
            sa = -jnp.sum(s * kk, axis=0, keepdims=True)
            sn = s * w + sa * kka + vt[vi:vi + 1, :] * kt
            st[vi] = sn
            ys.append(jnp.sum(sn * rt, axis=0, keepdims=True))
        y_ref[t] = jnp.concatenate(ys, axis=0)
        return c

    lax.fori_loop(0, tc, step, 0)

    @pl.when(ti == pl.num_programs(1) - 1)
    def _():
        so_ref[...] = st[...]


def _scan_layout(x, b, t):
    p = b * RWKV_HEADS
    a = x.reshape(b, t, RWKV_HEADS, HEAD_DIM).transpose(1, 3, 0, 2).reshape(t, HEAD_DIM, p)
    nc = -(-p // SCAN_P)
    return jnp.pad(a, ((0, 0), (0, 0), (0, nc * SCAN_P - p))).reshape(t, HEAD_DIM, nc, SCAN_P).transpose(2, 0, 1, 3)


def _scan_unlayout_y(y, b, t):
    p = b * RWKV_HEADS
    nc = y.shape[0]
    a = jnp.concatenate([y[..., :SCAN_P], y[..., SCAN_P:]], axis=2)
    a = a.transpose(1, 2, 0, 3).reshape(t, HEAD_DIM, nc * SCAN_P)[:, :, :p]
    return a.reshape(t, HEAD_DIM, b, RWKV_HEADS).transpose(2, 0, 3, 1).reshape(b * t, RWKV_DIM)


def _scan_layout_state(s):
    b = s.shape[0]
    p = b * RWKV_HEADS
    nc = -(-p // SCAN_P)
    a = jnp.pad(s.reshape(p, HEAD_DIM, HEAD_DIM), ((0, nc * SCAN_P - p), (0, 0), (0, 0)))
    a = a.reshape(nc, SCAN_P, HEAD_DIM, HEAD_DIM).transpose(0, 2, 3, 1)
    return jnp.concatenate([a[:, :SCAN_VH], a[:, SCAN_VH:]], -1)


def _scan_unlayout_state(st, b):
    p = b * RWKV_HEADS
    nc = st.shape[0]
    a = jnp.concatenate([st[..., :SCAN_P], st[..., SCAN_P:]], axis=1)
    a = a.transpose(0, 3, 1, 2).reshape(nc * SCAN_P, HEAD_DIM, HEAD_DIM)[:p]
    return a.reshape(b, RWKV_HEADS, HEAD_DIM, HEAD_DIM)


def _rwkv_scan(pre, s0, b, t, tc):
    r, w, k, v, kk, kka, _ = pre
    ks = [jnp.concatenate([a, a], -1) for a in (_scan_layout(x, b, t) for x in (kk, w, kka, k, r))]
    vs = _scan_layout(v, b, t)
    vs = jnp.concatenate([vs[:, :, :SCAN_VH], vs[:, :, SCAN_VH:]], -1)
    s0l = _scan_layout_state(s0)
    nc = s0l.shape[0]
    kspec = pl.BlockSpec((None, tc, HEAD_DIM, LANE), lambda c, i: (c, i, 0, 0))
    vspec = pl.BlockSpec((None, tc, SCAN_VH, LANE), lambda c, i: (c, i, 0, 0))
    sspec = pl.BlockSpec((None, SCAN_VH, HEAD_DIM, LANE), lambda c, i: (c, 0, 0, 0))
    y, so = pl.pallas_call(
        functools.partial(_scan_kernel, tc=tc),
        grid=(nc, t // tc),
        in_specs=[kspec] * 5 + [vspec, sspec],
        out_specs=[vspec, sspec],
        out_shape=[jax.ShapeDtypeStruct((nc, t, SCAN_VH, LANE), F32),
                   jax.ShapeDtypeStruct((nc, SCAN_VH, HEAD_DIM, LANE), F32)],
        scratch_shapes=[pltpu.VMEM((SCAN_VH, HEAD_DIM, LANE), F32)],
        compiler_params=_cparams("parallel", "arbitrary"),
        name="rwkv_scan",
    )(*ks, vs, s0l)
    return _scan_unlayout_y(y, b, t), _scan_unlayout_state(so, b)


def _compress_kernel(x_ref, pe_ref, w_ref, kg_ref, gtwo_ref, o_ref):
    z = jnp.dot((x_ref[...] + pe_ref[...]).astype(BF16), w_ref[...], preferred_element_type=F32)
    o_ref[:, 0:LANE] = _group_rms(z[:, 0:LANE], gtwo_ref[...], kg_ref[...])
    o_ref[:, LANE:2 * LANE] = z[:, LANE:2 * LANE]


def _compress_weights(cmp_pe, cmp_w, k_norm_c):
    wk = cmp_w[0].reshape(CMP_BLOCK, HEAD_DIM, HEAD_DIM)
    wv = cmp_w[1].reshape(CMP_BLOCK, HEAD_DIM, HEAD_DIM)
    full = jnp.zeros((CMP_BLOCK, 4, HEAD_DIM, 4, HEAD_DIM), F32)
    for s, w in enumerate((wk, wk, wv, wv)):
        full = full.at[:, s, :, s, :].set(w)
    pe = jnp.stack([cmp_pe[0], cmp_pe[0], cmp_pe[1], cmp_pe[1]], axis=1)
    return (full.reshape(CMP_BLOCK * 4 * HEAD_DIM, 4 * HEAD_DIM).astype(BF16), pe.reshape(1, -1),
            jnp.tile(k_norm_c, 2).reshape(1, LANE))


def _compress(rows, cw, gtwo, tile):
    wfull, pe, kg = cw
    return _row_call(_compress_kernel, rows, [], [pe, wfull, kg, gtwo], [(2 * LANE, F32)], tile, 1, "nsa_compress")[0]


def _compress_paged_kernel(pt_ref, ident_ref, pe_ref, w_ref, kg_ref, gtwo_ref, *refs, grp, n_pages):
    del pt_ref
    page_refs, o_ref, xs = refs[:grp * n_pages], refs[-2], refs[-1]
    ident = ident_ref[...]
    for i in range(grp * n_pages):
        for half in range(2):
            xt = (page_refs[i][half * LANE:(half + 1) * LANE, :] + pe_ref[half]).astype(BF16)
            xs[half, i * PAGE_SIZE:(i + 1) * PAGE_SIZE, :] = _dot_nt(ident, xt)
    n_blk = grp * n_pages * (PAGE_SIZE // CMP_BLOCK)
    acc = [jnp.zeros((n_blk, LANE), F32) for _ in range(2)]
    for r in range(CMP_BLOCK):
        for half in range(2):
            rows = xs[half, pl.ds(r, n_blk, stride=CMP_BLOCK), :]
            acc[half] = acc[half] + jnp.dot(rows.astype(BF16), w_ref[half, r], preferred_element_type=F32)
    kc = _group_rms(acc[0], gtwo_ref[...], kg_ref[...])
    per_seq = n_blk // grp
    for g in range(grp):
        o_ref[g, 0:per_seq, 0:LANE] = kc[g * per_seq:(g + 1) * per_seq].astype(BF16)
        o_ref[g, 0:per_seq, LANE:2 * LANE] = acc[1][g * per_seq:(g + 1) * per_seq].astype(BF16)
        o_ref[g, per_seq:LANE, :] = jnp.zeros((LANE - per_seq, 2 * LANE), BF16)


def _compress_paged(cmp_t, page_table, cmp_pe, cmp_w, k_norm_c, gtwo, grp):
    b, n_pages = page_table.shape
    wk = cmp_w[0].reshape(CMP_BLOCK, HEAD_DIM, HEAD_DIM)
    wv = cmp_w[1].reshape(CMP_BLOCK, HEAD_DIM, HEAD_DIM)
    wbd = jnp.zeros((2, CMP_BLOCK, N_KV_HEADS, HEAD_DIM, N_KV_HEADS, HEAD_DIM), F32)
    for half, w in enumerate((wk, wv)):
        for h in range(N_KV_HEADS):
            wbd = wbd.at[half, :, h, :, h, :].set(w)
    wbd = wbd.reshape(2, CMP_BLOCK, LANE, LANE).astype(BF16)
    pe = jnp.tile(cmp_pe.transpose(0, 2, 1), (1, N_KV_HEADS, PAGE_SIZE // CMP_BLOCK))
    ident = jnp.eye(LANE, dtype=BF16)
    kg = jnp.tile(k_norm_c, 2).reshape(1, LANE)
    cspec = lambda a: pl.BlockSpec(a.shape, lambda i, pt: (0,) * a.ndim)
    kern = functools.partial(_compress_paged_kernel, grp=grp, n_pages=n_pages)
    return pl.pallas_call(
        kern,
        grid_spec=pltpu.PrefetchScalarGridSpec(
            num_scalar_prefetch=1, grid=(b // grp,),
            in_specs=[cspec(ident), cspec(pe), cspec(wbd), cspec(kg), cspec(gtwo)]
            + _page_specs(cmp_t, n_pages, grp),
            out_specs=pl.BlockSpec((grp, LANE, 2 * LANE), lambda i, pt: (i, 0, 0)),
            scratch_shapes=[pltpu.VMEM((2, grp * n_pages * PAGE_SIZE, LANE), F32)]),
        out_shape=jax.ShapeDtypeStruct((b, LANE, 2 * LANE), BF16),
        compiler_params=_cparams("parallel"),
        name="nsa_compress_paged",
    )(page_table, ident, pe, wbd, kg, gtwo, *([cmp_t] * (grp * n_pages)))


def _nsa_kernel(q_ref, gates_ref, kcv_ref, kvs_ref, kvw_ref, bias_ref, biasc_ref, pair_ref, o_ref,
                *, grp, qb, q_base, n_cmp, n_sel, w_off, single):
    q0 = q_base if single else q_base + pl.program_id(1) * qb
    nblk = (q0 + qb - 1) // LANE + 1
    dq = q0 // LANE
    seqs = range(grp)
    heads = range(N_KV_HEADS)
    row = lax.broadcasted_iota(I32, (qb, LANE), 0)
    lane = lax.broadcasted_iota(I32, (qb, LANE), 1)
    t_pos = q0 + row
    qs = [[_stack_heads(q_ref.at[g], hk) for hk in heads] for g in seqs]

    def gate_col(g, br, hk):
        gates = gates_ref[g]
        return jnp.concatenate([gates[:, br * N_ATT_HEADS + hk * GQA + a:br * N_ATT_HEADS + hk * GQA + a + 1]
                                for a in range(GQA)], axis=0)

    mask_c4 = _tile4(((lane * CMP_BLOCK + CMP_BLOCK - 1) <= t_pos) & (lane < n_cmp))
    cur = t_pos // SEL_BLOCK
    forced = (lane == 0) | (lane == cur) | (lane == cur - 1)
    sel_causal = lane * SEL_BLOCK <= t_pos
    lane_f = lane.astype(F32)
    o_cmp = [[None] * N_KV_HEADS for _ in seqs]
    selm = [[None] * N_KV_HEADS for _ in seqs]
    for g in seqs:
        kc = kcv_ref[g, :, 0:LANE]
        vc = kcv_ref[g, :, LANE:2 * LANE]
        for hk in heads:
            s = _dot_nt(qs[g][hk], kc) * ATT_SCALE + biasc_ref[hk * GQA:(hk + 1) * GQA].reshape(GQA * qb, LANE)
            s = jnp.where(mask_c4, s, NEG)
            e = jnp.exp(s - jnp.max(s, axis=1, keepdims=True))
            p = jnp.where(mask_c4, e / jnp.sum(e, axis=1, keepdims=True), 0.0)
            o_cmp[g][hk] = jnp.dot(p.astype(BF16), vc, preferred_element_type=F32)
            ps = p[0:qb]
            for a in range(1, GQA):
                ps = ps + p[a * qb:(a + 1) * qb]
            score = _dot_split(ps, pair_ref[...])
            score = jnp.where(sel_causal, jnp.where(forced, FORCE, score), NEG)
            picked = jnp.zeros((qb, LANE), F32)
            for _ in range(n_sel):
                mx = jnp.max(score, axis=1, keepdims=True)
                first = jnp.min(jnp.where(score == mx, lane_f, float(LANE)), axis=1, keepdims=True)
                hit = lane_f == first
                picked = jnp.where(hit, 1.0, picked)
                score = jnp.where(hit, TAKEN, score)
            selm[g][hk] = picked.astype(BF16)

    def key_blocks(ref, g, jb):
        off = pl.multiple_of(jb * LANE, LANE)
        return ref[g, pl.ds(off, LANE), 0:LANE], ref[g, pl.ds(off, LANE), LANE:2 * LANE]

    def biases_of(jb):
        dsel = jnp.minimum(dq - jb, 2)
        return [bias_ref[dsel, hk * GQA:(hk + 1) * GQA].reshape(GQA * qb, LANE) for hk in heads]

    init = tuple(tuple(_flash_init(GQA * qb) for _ in heads) for _ in seqs)

    erow = lax.broadcasted_iota(I32, (LANE, LANE), 0)
    ecol = lax.broadcasted_iota(I32, (LANE, LANE), 1)

    def slc_body(jb, carry):
        expand = jnp.where(erow == 2 * jb + ecol // SEL_BLOCK, 1.0, 0.0).astype(BF16)
        causal = jb * LANE + lane <= t_pos
        biases = biases_of(jb)
        out = []
        for g in seqs:
            kb, vb = key_blocks(kvs_ref, g, jb)
            res = []
            for hk in heads:
                tok = jnp.dot(selm[g][hk], expand, preferred_element_type=F32) > 0.5
                res.append(_flash_step(qs[g][hk], kb, vb, biases[hk], _tile4(tok & causal), *carry[g][hk]))
            out.append(tuple(res))
        return tuple(out)

    res_s = lax.fori_loop(0, nblk, slc_body, init)

    def win_body(jb, carry):
        dist = t_pos - (jb * LANE + lane)
        msk = _tile4((dist >= 0) & (dist < WINDOW))
        biases = biases_of(jb)
        out = []
        for g in seqs:
            kb, vb = key_blocks(kvw_ref, g, jb - w_off)
            out.append(tuple(_flash_step(qs[g][hk], kb, vb, biases[hk], msk, *carry[g][hk]) for hk in heads))
        return tuple(out)

    res_w = lax.fori_loop(jnp.maximum(dq - WINDOW // LANE, 0), dq + 1, win_body, init)

    for g in seqs:
        outs = []
        for hk in heads:
            o_s = res_s[g][hk][2] / res_s[g][hk][1]
            o_w = res_w[g][hk][2] / res_w[g][hk][1]
            outs.append(gate_col(g, 0, hk) * o_cmp[g][hk] + gate_col(g, 1, hk) * o_s + gate_col(g, 2, hk) * o_w)
        _write_heads(o_ref.at[g], outs, qb)


def _bias_cmp(rel_bias, q_starts, qb):
    q0 = jnp.asarray(q_starts, I32)[:, None, None]
    r = jnp.arange(qb)[None, :, None]
    n = jnp.arange(LANE)[None, None, :]
    return _bias_lookup(rel_bias, q0 + r - (n * CMP_BLOCK + CMP_BLOCK - 1)).transpose(0, 3, 1, 2)


def _nsa(q, gates, kcv, kvs_b, kvw_b, bias, bias_c, pair, *, qb, q_base, n_cmp, n_sel, w_off, grp):
    b, rows, _ = q.shape
    nq = rows // qb
    kern = functools.partial(_nsa_kernel, grp=grp, qb=qb, q_base=q_base, n_cmp=n_cmp, n_sel=n_sel, w_off=w_off,
                             single=(nq == 1))
    qspec = lambda w: pl.BlockSpec((grp, qb, w), lambda bi, i: (bi, i, 0))
    kspec = lambda a: pl.BlockSpec((grp,) + a.shape[1:], lambda bi, i: (bi, 0, 0))
    return pl.pallas_call(
        kern,
        grid=(b // grp, nq),
        in_specs=[qspec(N_ATT_HEADS * LANE), qspec(LANE), kspec(kcv), kspec(kvs_b), kspec(kvw_b), _const_spec(bias),
                  pl.BlockSpec((None,) + bias_c.shape[1:], lambda bi, i: (i, 0, 0, 0)), _const_spec(pair)],
        out_specs=qspec(N_ATT_HEADS * LANE),
        out_shape=jax.ShapeDtypeStruct((b, rows, N_ATT_HEADS * LANE), BF16),
        compiler_params=_cparams("parallel", "parallel"),
        name="nsa_attention",
    )(q, gates, kcv, kvs_b, kvw_b, bias, bias_c, pair)


def _nsa_sample_kernel(pt_ref, q_ref, gates_ref, kcv_ref, snew_ref, wnew_ref, win_ref, bias_ref, biasc_ref,
                       biasw_ref, pair_ref, *refs, grp, n_pages, t_pos, n_cmp, n_sel, w_eff):
    del pt_ref
    slc_refs, o_ref = refs[:grp * n_pages], refs[-1]
    lane1 = lax.broadcasted_iota(I32, (1, LANE), 1)
    lane8 = lax.broadcasted_iota(I32, (N_ATT_HEADS, LANE), 1)
    row_all = lax.broadcasted_iota(I32, (N_ATT_HEADS, (n_pages + 1) * LANE), 0)
    lanew = lax.broadcasted_iota(I32, (N_ATT_HEADS, w_eff), 1)
    lane1_f = lane1.astype(F32)
    bias = bias_ref[...]
    bias_now = bias[:, n_pages * LANE:n_pages * LANE + 1]
    mask_c = ((lane8 * CMP_BLOCK + CMP_BLOCK - 1) <= t_pos) & (lane8 < n_cmp)
    cur = t_pos // SEL_BLOCK
    forced = (lane1 == 0) | (lane1 == cur) | (lane1 == cur - 1)
    sel_causal = lane1 * SEL_BLOCK <= t_pos
    tail_valid = jnp.concatenate([lane1] * n_pages + [lane1 + LANE], axis=1) <= LANE
    for g in range(grp):
        q = q_ref[g]
        qf = q.astype(F32)
        s = jnp.where(mask_c, _dot_nt(q, kcv_ref[g, :, 0:LANE]) * ATT_SCALE + biasc_ref[...], NEG)
        e = jnp.exp(s - jnp.max(s, axis=1, keepdims=True))
        pc = jnp.where(mask_c, e / jnp.sum(e, axis=1, keepdims=True), 0.0)
        o_c = jnp.dot(pc.astype(BF16), kcv_ref[g, :, LANE:2 * LANE], preferred_element_type=F32)
        masks = []
        for hk in range(N_KV_HEADS):
            ps = jnp.sum(pc[hk * GQA:(hk + 1) * GQA], axis=0, keepdims=True)
            score = _dot_split(ps, pair_ref[...])
            score = jnp.where(sel_causal, jnp.where(forced, FORCE, score), NEG)
            picked = jnp.zeros((1, LANE), F32)
            for _ in range(n_sel):
                mx = jnp.max(score, axis=1, keepdims=True)
                first = jnp.min(jnp.where(score == mx, lane1_f, float(LANE)), axis=1, keepdims=True)
                hit = lane1_f == first
                picked = jnp.where(hit, 1.0, picked)
                score = jnp.where(hit, TAKEN, score)
            per_page = PAGE_SIZE // SEL_BLOCK
            tiles = []
            for p in range(n_pages + 1):
                t = jnp.zeros((1, LANE), F32)
                for a in range(per_page):
                    blk = picked[:, p * per_page + a:p * per_page + a + 1]
                    t = jnp.where(lane1 // SEL_BLOCK == a, blk, t)
                tiles.append(t)
            masks.append(jnp.concatenate(tiles, axis=1))
        valid = (jnp.where(row_all < GQA, masks[0], masks[1]) > 0.5) & tail_valid
        snew = _bf16_round(snew_ref[g])
        tiles = [_dot(q, slc_refs[g * n_pages + p][0:LANE, :]) for p in range(n_pages)]
        tiles.append(jnp.where(lane1 == 0, jnp.sum(qf * snew[:, 0:LANE], axis=1, keepdims=True), 0.0))
        s = jnp.where(valid, jnp.concatenate(tiles, axis=1) * ATT_SCALE + bias, NEG)
        e = jnp.exp(s - jnp.max(s, axis=1, keepdims=True))
        p_all = jnp.where(valid, e / jnp.sum(e, axis=1, keepdims=True), 0.0)
        o_s = p_all[:, n_pages * LANE:n_pages * LANE + 1] * snew[:, LANE:2 * LANE]
        for p in range(n_pages):
            o_s = o_s + _dot_nt(p_all[:, p * LANE:(p + 1) * LANE], slc_refs[g * n_pages + p][LANE:2 * LANE, :])
        wnew = _bf16_round(wnew_ref[g])
        valid_w = (w_eff - lanew) < WINDOW
        s_w = jnp.where(valid_w, _dot(q, win_ref[g, 0:LANE, :]) * ATT_SCALE + biasw_ref[...], NEG)
        s_n = jnp.sum(qf * wnew[:, 0:LANE], axis=1, keepdims=True) * ATT_SCALE + bias_now
        m = jnp.maximum(jnp.max(s_w, axis=1, keepdims=True), s_n)
        e_w = jnp.where(valid_w, jnp.exp(s_w - m), 0.0)
        e_n = jnp.exp(s_n - m)
        den = jnp.sum(e_w, axis=1, keepdims=True) + e_n
        o_w = _dot_nt(e_w / den, win_ref[g, LANE:2 * LANE, :]) + (e_n / den) * wnew[:, LANE:2 * LANE]
        gates = gates_ref[g]
        _head_rows_out(o_ref, g, gates[:, 0:1] * o_c + gates[:, 1:2] * o_s + gates[:, 2:3] * o_w)


def _nsa_sample(q8, gates8, kcv, snew, wnew, win_t, bias, bias_c, bias_w, pair, slc_t, page_table,
                *, t_pos, n_cmp, n_sel, grp):
    b, n_pages = page_table.shape
    w_eff = win_t.shape[2]
    gspec = lambda a: pl.BlockSpec((grp,) + a.shape[1:], lambda i, pt: (i,) + (0,) * (a.ndim - 1))
    cspec = lambda a: pl.BlockSpec(a.shape, lambda i, pt: (0,) * a.ndim)
    kern = functools.partial(_nsa_sample_kernel, grp=grp, n_pages=n_pages, t_pos=t_pos, n_cmp=n_cmp, n_sel=n_sel,
                             w_eff=w_eff)
    return pl.pallas_call(
        kern,
        grid_spec=pltpu.PrefetchScalarGridSpec(
            num_scalar_prefetch=1, grid=(b // grp,),
            in_specs=[gspec(q8), gspec(gates8), gspec(kcv), gspec(snew), gspec(wnew), gspec(win_t), cspec(bias),
                      cspec(bias_c), cspec(bias_w), cspec(pair)] + _page_specs(slc_t, n_pages, grp),
            out_specs=pl.BlockSpec((grp, N_ATT_HEADS, LANE), lambda i, pt: (i, 0, 0))),
        out_shape=jax.ShapeDtypeStruct((b, N_ATT_HEADS, LANE), BF16),
        compiler_params=_cparams("parallel"),
        name="nsa_sample",
    )(page_table, q8, gates8, kcv, snew, wnew, win_t, bias, bias_c, bias_w, pair, *([slc_t] * (grp * n_pages)))


def _post_odd_kernel(x_ref, y_ref, r_ref, k_ref, v_ref, g_ref, od_ref, gate_ref, shf_ref, scf_ref,
                     gf_ref, lnw_ref, lnb_ref, rk_ref, gtwo_ref, wo_ref, wrt_ref, br_ref, x2_ref, hf_ref, cwt_ref):
    gtwo = gtwo_ref[...]
    mix = jnp.dot(od_ref[...], wo_ref[RWKV_DIM:RWKV_DIM + N_ATT_HEADS * LANE, :], preferred_element_type=F32)
    for t in range(RWKV_DIM // LANE):
        sl = slice(t * LANE, (t + 1) * LANE)
        y = y_ref[:, sl]
        dlt = y - _dot_split(y, gtwo)
        yn = (dlt * lax.rsqrt(_dot_split(dlt * dlt, gtwo) + GN_EPS)) * lnw_ref[:, sl] + lnb_ref[:, sl]
        dot_rk = _dot_split(r_ref[:, sl] * k_ref[:, sl] * rk_ref[:, sl], gtwo) * float(HEAD_DIM)
        oc = (yn + dot_rk * v_ref[:, sl]) * g_ref[:, sl]
        mix = mix + jnp.dot(oc.astype(BF16), wo_ref[sl, :], preferred_element_type=F32)
    _post_tail(x_ref[...], mix, gate_ref[...], gf_ref[...], shf_ref[...], scf_ref[...], wrt_ref, br_ref,
               x2_ref, hf_ref, cwt_ref)


def _odd_layer(xp, xs, mp, ms, page_table, wkv0, shift0, cmp_pool, slc_pool, win_buf, rel_bias, router, experts,
               g_mix, g_ffn, w_in, w_out, cpar, r_k, ln_w, ln_b, q_norm, k_norm, cmp_pe, cmp_w):
    B, S, D = xp.shape
    Bs = xs.shape[0]
    n_pages = page_table.shape[1]
    past = n_pages * PAGE_SIZE
    wts = _odd_weights(w_in, w_out, q_norm, k_norm)
    gm = _gmats()
    gone, gtwo, _ = gm
    gsum = (gtwo.astype(F32) * HEAD_DIM).astype(BF16)
    i = jnp.arange(LANE)
    pair = jnp.where(i[:, None] // 2 == i[None, :], 1.0, 0.0).astype(BF16)
    cw = _compress_weights(cmp_pe, cmp_w, k_norm[0])
    wrt, br = router
    wg, wu, wd, lyr = experts
    g_mix = g_mix.reshape(1, D)
    post_consts = [g_ffn.reshape(1, D), ln_w.reshape(1, -1), ln_b.reshape(1, -1), r_k.reshape(1, -1), gtwo, wts[1],
                   wrt, br]
    w_eff = win_buf.shape[1]

    tp = ROW_TILE
    zc, q_p, kvc, kvs, kvs_b, kvw, kvw_b, gates = _inproj_odd(xp.reshape(B * S, D), [mp[0], mp[1]], g_mix, wts, gm,
                                                               tp, S // tp)
    r3 = lambda a: a.reshape(B, S, a.shape[-1])
    pre = _rwkv_pre(zc, _shift_rows(r3(zc), 1).reshape(B * S, P_C), cpar, gsum, tp)
    y, wkv_p = _rwkv_scan(pre, jnp.zeros((B, RWKV_HEADS, HEAD_DIM, HEAD_DIM), F32), B, S, SCAN_TIME_CHUNK)
    n_cmp = S // CMP_BLOCK
    kcv = _compress(kvc.reshape(B * n_cmp, CMP_BLOCK * 2 * LANE), cw, gtwo, _pick_tile(B * n_cmp, 256))
    kcv = jnp.pad(kcv.reshape(B, n_cmp, 2 * LANE), ((0, 0), (0, LANE - n_cmp), (0, 0))).astype(BF16)
    n_slc = -(-S // SEL_BLOCK)
    od = _nsa_t(r3(q_p), r3(gates), kcv, r3(kvs_b), r3(kvw_b), _bias_tiles_t(rel_bias), _bias_cmp_t(rel_bias, S),
                pair.T, n_cmp=n_cmp, n_sel=min(N_SEL_BLOCKS, n_slc))
    x2, hf, cwt = _row_call(_post_odd_kernel,
                            [xp.reshape(B * S, D), y, pre[0], pre[2], pre[3], pre[6], od.reshape(B * S, -1)],
                            [mp[2], mp[3], mp[4]], post_consts, POST_OUTS, tp, S // tp, "post_odd",
                            outs_t=[(N_EXPERTS, F32)])
    tm = MOE_TILE
    xp3 = _moe(hf, cwt.T, x2, mp[5], wg, wu, wd, lyr, tm, S // tm).reshape(B, S, D)
    kv5 = lambda a, n: a.reshape(-1, n, 2, N_KV_HEADS, HEAD_DIM)
    outs_p = (wkv_p, r3(zc)[:, S - 1], kv5(kvc, S), kv5(kvs, S), kv5(kvw, S)[:, S - min(WINDOW, S):])

    zc_s, q_s, kvc_s, kvs_s, _, kvw_s, _, gates_s = _inproj_odd(xs, [ms[0], ms[1]], g_mix, wts, gm, Bs, 1)
    pre_s = _rwkv_pre(zc_s, shift0, cpar, gsum, Bs)
    y_s, wkv_s = _rwkv_scan(pre_s, wkv0, Bs, 1, 1)
    n_pool = cmp_pool.shape[0]
    cmp_t = cmp_pool.transpose(0, 2, 3, 4, 1).reshape(n_pool, 2 * LANE, PAGE_SIZE)
    kcv_s = _compress_paged(cmp_t, page_table, cmp_pe, cmp_w, k_norm[0], gtwo, _pick_group(Bs, SAMPLE_GROUP))
    n_cmp_s = (past + 1) // CMP_BLOCK
    slc_t = slc_pool.transpose(0, 2, 3, 4, 1).reshape(n_pool, 2 * LANE, PAGE_SIZE)
    win_t = win_buf.transpose(0, 2, 3, 4, 1).reshape(Bs, 2 * LANE, w_eff)
    gates8 = jnp.pad(gates_s[:, :3 * N_ATT_HEADS].reshape(Bs, 3, N_ATT_HEADS).transpose(0, 2, 1),
                     ((0, 0), (0, 0), (0, LANE - 3)))
    n_slc_s = -(-(past + 1) // SEL_BLOCK)
    od_s = _nsa_sample(q_s.reshape(Bs, N_ATT_HEADS, LANE), gates8, kcv_s, kvs_s[:, None, :], kvw_s[:, None, :],
                       win_t, _bias_row(rel_bias, past), _bias_cmp(rel_bias, [past], 1)[0, :, 0, :],
                       _bias_lookup(rel_bias, w_eff - jnp.arange(w_eff)).T, pair, slc_t, page_table,
                       t_pos=past, n_cmp=n_cmp_s, n_sel=min(N_SEL_BLOCKS, n_slc_s),
                       grp=_pick_group(Bs, SAMPLE_GROUP)).reshape(Bs, -1)
    x2s, hfs, cwts = _row_call(_post_odd_kernel, [xs, y_s, pre_s[0], pre_s[2], pre_s[3], pre_s[6], od_s],
                               [ms[2], ms[3], ms[4]], post_consts, POST_OUTS, Bs, 1, "post_odd_s",
                               outs_t=[(N_EXPERTS, F32)])
    xs3 = _moe(hfs, cwts.T, x2s, ms[5], wg, wu, wd, lyr, Bs, 1)
    win_new = jnp.concatenate([win_buf[:, 1:], kv5(kvw_s, 1)], axis=1)
    outs_s = (wkv_s, zc_s, kv5(kvc_s, 1), kv5(kvs_s, 1), win_new)
    return xp3, xs3, outs_p, outs_s


def _mods(c_p, c_s, w_all, b, layer):
    nb = c_p.shape[0]
    m = _ada(jnp.concatenate([c_p, c_s], 0), w_all, b, layer)
    parts = jnp.split(m, 6, axis=-1)
    return [p[:nb, None, :] for p in parts], [p[None, nb:, :] for p in parts]


def _forward(x_prompt, x_sample, c_prompt, c_sample, page_table, cache_a_kv, cache_a_kidx, state_b_conv,
             state_c_wkv, state_c_shift, cache_d_cmp, cache_d_slc, cache_d_win, rel_bias, w_router, b_router,
             w_ada, b_ada, g_norm_mix, g_norm_ffn, w_expert_gate, w_expert_up, w_expert_down, e_w_in, e_w_out,
             a_q_norm, a_k_norm, b_conv_w, b_conv_b, o_w_in, o_w_out, c_mu, c_w0, c_w_up, c_a0, c_a_up,
             c_g_up, c_k_k, c_k_a, c_r_k, c_ln_w, c_ln_b, d_q_norm, d_k_norm, d_cmp_pe, d_cmp_w):
    assert w_ada.shape[0] == 2 and e_w_in.shape[0] == 1 and o_w_in.shape[0] == 1
    B, S, D = x_prompt.shape
    Bs = x_sample.shape[0]
    assert x_sample.shape[1] == 1
    xp, xs = x_prompt, x_sample.reshape(Bs, D)
    router = (w_router.T, b_router.reshape(N_EXPERTS, 1))
    n_pool = cache_a_kv.shape[1]
    experts = lambda l: (w_expert_gate, w_expert_up, w_expert_down, l)

    mp, ms = _mods(c_prompt, c_sample, w_ada, b_ada[0], 0)
    xp, xs, ep, es = _even_layer(xp, xs, mp, ms, page_table, cache_a_kv[0], cache_a_kidx[0], state_b_conv[0],
                                 rel_bias, router, experts(0), g_norm_mix[0], g_norm_ffn[0], e_w_in[0], e_w_out[0],
                                 a_q_norm[0], a_k_norm[0], b_conv_w[0], b_conv_b[0])
    mp, ms = _mods(c_prompt, c_sample, w_ada, b_ada[1], 1)
    cpar = (c_mu[0], c_w0[0], c_w_up[0], c_a0[0], c_a_up[0], c_g_up[0], c_k_k[0], c_k_a[0])
    xp, xs, op, os_ = _odd_layer(xp, xs, mp, ms, page_table, state_c_wkv[0], state_c_shift[0], cache_d_cmp[0],
                                 cache_d_slc[0], cache_d_win[0], rel_bias, router, experts(1), g_norm_mix[1],
                                 g_norm_ffn[1], o_w_in[0], o_w_out[0], cpar, c_r_k[0].reshape(-1), c_ln_w[0],
                                 c_ln_b[0], d_q_norm[0], d_k_norm[0], d_cmp_pe[0], d_cmp_w[0])
    stack = lambda ts: tuple(a[None] for a in ts)
    return (xp, xs.reshape(Bs, 1, D)) + stack(ep) + stack(op) + stack(es) + stack(os_)


def kernel(x_prompt, x_sample, c_prompt, c_sample, page_table, cache_a_kv, cache_a_kidx, state_b_conv, state_c_wkv, state_c_shift, cache_d_cmp, cache_d_slc, cache_d_win, rel_bias, w_router, b_router, w_ada, b_ada, g_norm_mix, g_norm_ffn, w_expert_gate, w_expert_up, w_expert_down, e_w_in, e_w_out, a_q_norm, a_k_norm, b_conv_w, b_conv_b, o_w_in, o_w_out, c_mu, c_w0, c_w_up, c_a0, c_a_up, c_g_up, c_k_k, c_k_a, c_r_k, c_ln_w, c_ln_b, d_q_norm, d_k_norm, d_cmp_pe, d_cmp_w):
    return _forward(x_prompt, x_sample, c_prompt, c_sample, page_table, cache_a_kv, cache_a_kidx, state_b_conv,
                    state_c_wkv, state_c_shift, cache_d_cmp, cache_d_slc, cache_d_win, rel_bias, w_router, b_router,
                    w_ada, b_ada, g_norm_mix, g_norm_ffn, w_expert_gate, w_expert_up, w_expert_down, e_w_in, e_w_out,
                    a_q_norm, a_k_norm, b_conv_w, b_conv_b, o_w_in, o_w_out, c_mu, c_w0, c_w_up, c_a0, c_a_up,
                    c_g_up, c_k_k, c_k_a, c_r_k, c_ln_w, c_ln_b, d_q_norm, d_k_norm, d_cmp_pe, d_cmp_w)
```

```python
import functools
import math

import jax
import jax.numpy as jnp
from jax import lax
from jax.experimental import pallas as pl
from jax.experimental.pallas import tpu as pltpu

F32 = jnp.float32
BF16 = jnp.bfloat16
I32 = jnp.int32

LANE = 128
HEAD_DIM = 64
N_ATT_HEADS = 8
N_KV_HEADS = 2
GQA = N_ATT_HEADS // N_KV_HEADS
IDX_HEADS = 4
IDX_DIM = 64
TOPK_MAX = 256
CONV_CH = 512
RWKV_HEADS = 8
RWKV_DIM = RWKV_HEADS * HEAD_DIM
LORA_W = 64
LORA_A = 64
LORA_G = 128
GN_EPS = 64e-5
CMP_BLOCK = 32
SEL_BLOCK = 64
N_SEL_BLOCKS = 8
WINDOW = 512
N_BUCKETS = 32
MAX_DISTANCE = 128
N_EXPERTS = 16
N_GROUPS = 4
EXPERTS_PER_GROUP = N_EXPERTS // N_GROUPS
D_EXPERT = 512
PAGE_SIZE = 128
RMS_EPS = 1e-6
NEG = -1e30
FORCE = 1e9
TAKEN = -3e38
ATT_SCALE = HEAD_DIM ** -0.5
VMEM_LIMIT = 56 * 1024 * 1024
ROW_TILE = 256
MOE_TILE = 1024
SCAN_TIME_CHUNK = 32
SAMPLE_GROUP = 4


def _cparams(*sem):
    return pltpu.CompilerParams(dimension_semantics=sem, vmem_limit_bytes=VMEM_LIMIT)


def _pick_tile(rows, pref):
    t = min(pref, rows)
    while rows % t or (t % 8 and t != rows):
        t -= 1
    return t


def _pick_group(n, pref):
    g = min(pref, n)
    while n % g:
        g -= 1
    return g


def _const_spec(a):
    nd = a.ndim
    return pl.BlockSpec(a.shape, lambda *_: (0,) * nd)


def _dot(a, b):
    return jnp.dot(a.astype(BF16), b.astype(BF16), preferred_element_type=F32)


def _dot_nt(a, b):
    return lax.dot_general(a.astype(BF16), b.astype(BF16), (((1,), (1,)), ((), ())),
                           preferred_element_type=F32)


def _dot_split(x, m):
    hi = x.astype(BF16)
    r1 = x - hi.astype(F32)
    mid = r1.astype(BF16)
    lo = (r1 - mid.astype(F32)).astype(BF16)
    return (jnp.dot(hi, m, preferred_element_type=F32) + jnp.dot(mid, m, preferred_element_type=F32)
            + jnp.dot(lo, m, preferred_element_type=F32))


def _bf16_round(x):
    return x.astype(BF16).astype(F32)


def _sigmoid(x):
    return 1.0 / (1.0 + jnp.exp(-x))


def _silu(x):
    return x * _sigmoid(x)


def _modulate(x, g, shift, scale):
    y = x * lax.rsqrt(jnp.mean(x * x, axis=-1, keepdims=True) + RMS_EPS)
    return (y * g) * (1.0 + scale) + shift


def _group_rms(t, gmat, gain):
    ms = _dot_split(t * t, gmat)
    return (t * lax.rsqrt(ms + RMS_EPS)) * gain


def _ada_kernel(c_ref, w_ref, b_ref, o_ref):
    o_ref[...] = _dot(_silu(c_ref[...]), w_ref[...]) + b_ref[...]


def _ada(c, w_all, b, layer):
    r, d = c.shape
    n = w_all.shape[2]
    tn = 512
    return pl.pallas_call(
        _ada_kernel,
        grid=(n // tn,),
        in_specs=[pl.BlockSpec((r, d), lambda j: (0, 0)),
                  pl.BlockSpec((None, d, tn), lambda j: (layer, 0, j)),
                  pl.BlockSpec((1, tn), lambda j: (0, j))],
        out_specs=pl.BlockSpec((r, tn), lambda j: (0, j)),
        out_shape=jax.ShapeDtypeStruct((r, n), F32),
        compiler_params=_cparams("parallel"),
        name="ada_mod",
    )(c, w_all, b.reshape(1, n))


E_Q0, E_KV0, E_QI0, E_MISC0, E_BG0, E_CG0, E_XIN0, E_END = 0, 1024, 1280, 1792, 1920, 2432, 2944, 3456


def _inproj_even_kernel(x_ref, shift_ref, scale_ref, g_ref, w_ref, qg_ref, kg_ref, gone_ref, gtwo_ref,
                        q_ref, kv_ref, kvb_ref, qi_ref, misc_ref, miscb_ref, bg_ref, u_ref, kvt_ref, misct_ref):
    h = _modulate(x_ref[...], g_ref[...], shift_ref[...], scale_ref[...])
    z = jnp.dot(h.astype(BF16), w_ref[...], preferred_element_type=F32)
    gone = gone_ref[...]
    for t in range(N_ATT_HEADS):
        sl = slice(t * LANE, (t + 1) * LANE)
        q_ref[:, sl] = _group_rms(z[:, E_Q0 + t * LANE:E_Q0 + (t + 1) * LANE], gone, qg_ref[:, sl]).astype(BF16)
    k = _group_rms(z[:, E_KV0:E_KV0 + LANE], gtwo_ref[...], kg_ref[...])
    v = z[:, E_KV0 + LANE:E_KV0 + 2 * LANE]
    kv_ref[:, 0:LANE] = k
    kv_ref[:, LANE:2 * LANE] = v
    kvb_ref[:, 0:LANE] = k.astype(BF16)
    kvb_ref[:, LANE:2 * LANE] = v.astype(BF16)
    qi_ref[...] = z[:, E_QI0:E_MISC0].astype(BF16)
    misc = z[:, E_MISC0:E_BG0]
    misc_ref[...] = misc
    miscb_ref[...] = misc.astype(BF16)
    bg_ref[...] = z[:, E_BG0:E_CG0]
    u_ref[...] = z[:, E_CG0:E_XIN0] * z[:, E_XIN0:E_END]
    kvt_ref[0:LANE, :] = k.T
    kvt_ref[LANE:2 * LANE, :] = v.T
    misct_ref[...] = misc.T


def _row_call(kernel, xs, mods, consts, outs, tile, tpb, name, outs_t=(), outs_bt=()):
    if not isinstance(xs, (list, tuple)):
        xs = [xs]
    rows = xs[0].shape[0]
    n_tiles = rows // tile
    in_specs = [pl.BlockSpec((tile, x.shape[1]), lambda t: (t, 0)) for x in xs]
    for m in mods:
        in_specs.append(pl.BlockSpec((None,) + m.shape[1:], lambda t: (t // tpb, 0, 0)))
    in_specs += [_const_spec(c) for c in consts]
    out_specs = [pl.BlockSpec((tile, w), lambda t: (t, 0)) for (w, _) in outs]
    out_shape = [jax.ShapeDtypeStruct((rows, w), dt) for (w, dt) in outs]
    out_specs += [pl.BlockSpec((hh, tile), lambda t: (0, t)) for (hh, _) in outs_t]
    out_shape += [jax.ShapeDtypeStruct((hh, rows), dt) for (hh, dt) in outs_t]
    out_specs += [pl.BlockSpec((None, hh, tile), lambda t: (t // tpb, 0, t % tpb)) for (hh, _) in outs_bt]
    out_shape += [jax.ShapeDtypeStruct((n_tiles // tpb, hh, tile * tpb), dt) for (hh, dt) in outs_bt]
    return pl.pallas_call(kernel, grid=(n_tiles,), in_specs=in_specs, out_specs=out_specs, out_shape=out_shape,
                          compiler_params=_cparams("parallel"), name=name)(*xs, *mods, *consts)


def _pad_q_cols(wq):
    d = wq.shape[0]
    w = wq.reshape(d, N_ATT_HEADS, HEAD_DIM)
    z = jnp.zeros_like(w)
    lo = jnp.concatenate([w, z], -1)
    hi = jnp.concatenate([z, w], -1)
    sel = (jnp.arange(N_ATT_HEADS) >= GQA)[None, :, None]
    return jnp.where(sel, hi, lo).reshape(d, N_ATT_HEADS * LANE)


def _pad_o_rows(wo):
    d = wo.shape[1]
    w = wo.reshape(N_ATT_HEADS, HEAD_DIM, d)
    z = jnp.zeros_like(w)
    lo = jnp.concatenate([w, z], 1)
    hi = jnp.concatenate([z, w], 1)
    sel = (jnp.arange(N_ATT_HEADS) >= GQA)[:, None, None]
    return jnp.where(sel, hi, lo).reshape(N_ATT_HEADS * LANE, d)


def _gmats():
    i = jnp.arange(LANE)
    gone = jnp.full((LANE, LANE), 1.0 / HEAD_DIM, F32).astype(BF16)
    gtwo = jnp.where((i[:, None] // HEAD_DIM) == (i[None, :] // HEAD_DIM), 1.0 / HEAD_DIM, 0.0).astype(BF16)
    tri = jnp.where(i[:, None] <= i[None, :], 1.0, 0.0).astype(BF16)
    return gone, gtwo, tri


def _even_weights(w_in, w_out, q_norm, k_norm):
    d = w_in.shape[0]
    a_q, a_kv = N_ATT_HEADS * HEAD_DIM, 2 * N_KV_HEADS * HEAD_DIM
    o = 0
    wq = w_in[:, o:o + a_q]; o += a_q
    wkv = w_in[:, o:o + a_kv]; o += a_kv
    wqi = w_in[:, o:o + IDX_HEADS * IDX_DIM]; o += IDX_HEADS * IDX_DIM
    wki = w_in[:, o:o + IDX_DIM]; o += IDX_DIM
    wwi = w_in[:, o:o + IDX_HEADS]; o += IDX_HEADS
    wrest = w_in[:, o:]
    wqi = jnp.concatenate([wqi.reshape(d, IDX_HEADS, IDX_DIM), jnp.zeros((d, IDX_HEADS, LANE - IDX_DIM), F32)],
                          -1).reshape(d, IDX_HEADS * LANE)
    wmisc = jnp.concatenate([wki, wwi, jnp.zeros((d, LANE - IDX_DIM - IDX_HEADS), F32)], -1)
    w_in_p = jnp.concatenate([_pad_q_cols(wq), wkv, wqi, wmisc, wrest], -1).astype(BF16)
    w_out_p = jnp.concatenate([_pad_o_rows(w_out[:a_q]), w_out[a_q:]], 0).astype(BF16)
    qg = jnp.tile(q_norm, 2 * N_ATT_HEADS).reshape(1, N_ATT_HEADS * LANE)
    kg = jnp.tile(k_norm, 2).reshape(1, LANE)
    return w_in_p, w_out_p, qg, kg


def _inproj_even(x, mods, g, wts, gm, tile, tpb):
    w_in_p, _, qg, kg = wts
    gone, gtwo, _ = gm
    outs = [(N_ATT_HEADS * LANE, BF16), (2 * LANE, F32), (2 * LANE, BF16), (IDX_HEADS * LANE, BF16),
            (LANE, F32), (LANE, BF16), (CONV_CH, F32), (CONV_CH, F32)]
    return _row_call(_inproj_even_kernel, x, mods, [g, w_in_p, qg, kg, gone, gtwo], outs, tile, tpb, "inproj_even",
                     outs_bt=[(2 * LANE, F32), (LANE, F32)])


def _t5_bucket(dist):
    dist = jnp.maximum(dist, 0)
    exact = N_BUCKETS // 2
    far = exact + (jnp.log(jnp.maximum(dist, 1).astype(F32) / exact)
                   / math.log(MAX_DISTANCE / exact) * (N_BUCKETS - exact)).astype(I32)
    return jnp.where(dist < exact, dist, jnp.minimum(far, N_BUCKETS - 1))


def _bias_lookup(rel_bias, dist):
    onehot = (_t5_bucket(dist)[..., None] == jnp.arange(N_BUCKETS)).astype(F32)
    return jnp.einsum("...k,kh->...h", onehot, rel_bias, precision=lax.Precision.HIGHEST)


def _bias_row(rel_bias, t_pos):
    return _bias_lookup(rel_bias, t_pos - jnp.arange(t_pos + LANE)).T


def _bias_tiles(rel_bias, qb):
    r = jnp.arange(qb)[:, None]
    c = jnp.arange(LANE)[None, :]
    tiles = [_bias_lookup(rel_bias, d * LANE + r - c) for d in range(3)]
    return jnp.stack(tiles).transpose(0, 3, 1, 2)


def _flash_step(qs, kb, vb, bias, msk, m, l, acc):
    s = _dot_nt(qs, kb) * ATT_SCALE + bias
    s = jnp.where(msk, s, NEG)
    m_new = jnp.maximum(m, jnp.max(s, axis=1, keepdims=True))
    p = jnp.where(msk, jnp.exp(s - m_new), 0.0)
    alpha = jnp.exp(m - m_new)
    l = alpha * l + jnp.sum(p, axis=1, keepdims=True)
    acc = alpha * acc + jnp.dot(p.astype(BF16), vb, preferred_element_type=F32)
    return m_new, l, acc


def _flash_init(rows):
    return (jnp.full((rows, 1), NEG, F32), jnp.zeros((rows, 1), F32), jnp.zeros((rows, LANE), F32))


def _stack_heads(q_ref, hk):
    return jnp.concatenate([q_ref[:, (hk * GQA + g) * LANE:(hk * GQA + g + 1) * LANE] for g in range(GQA)], axis=0)


def _tile4(x):
    return jnp.concatenate([x] * GQA, axis=0)


def _loop(lo, hi, body, init, unroll):
    if unroll:
        for j in range(lo, hi):
            init = body(j, init)
        return init
    return lax.fori_loop(lo, hi, body, init)


def _dsa_kernel(q_ref, qi_ref, misc_ref, kidx_ref, kv_ref, bias_ref, tri_ref, o_ref, key_s,
                *, grp, qb, q_base, n_keep, single):
    q0 = q_base if single else q_base + pl.program_id(1) * qb
    nblk = (q0 + qb - 1) // LANE + 1
    dq = q0 // LANE
    seqs = range(grp)
    row = lax.broadcasted_iota(I32, (qb, LANE), 0)
    lane = lax.broadcasted_iota(I32, (qb, LANE), 1)
    t_pos = q0 + row
    wis = [[jnp.broadcast_to(misc_ref[g][:, IDX_DIM + h:IDX_DIM + h + 1], (qb, LANE)) for h in range(IDX_HEADS)]
           for g in seqs]
    idx_scale = (IDX_HEADS * IDX_DIM) ** -0.5

    def pass_a(j, c):
        off = pl.multiple_of(j * LANE, LANE)
        causal = j * LANE + lane <= t_pos
        for g in seqs:
            kb = kidx_ref[g, pl.ds(off, LANE), :]
            acc = jnp.zeros((qb, LANE), F32)
            for h in range(IDX_HEADS):
                acc = acc + jnp.maximum(_dot_nt(qi_ref[g, :, h * LANE:(h + 1) * LANE], kb), 0.0) * wis[g][h]
            sc = jnp.where(causal, acc * idx_scale, NEG)
            sc = jnp.where(sc == 0.0, 0.0, sc)
            bits = lax.bitcast_convert_type(sc, I32)
            key_s[g, j] = jnp.where(bits < 0, bits ^ jnp.int32(0x7FFFFFFF), bits)
        return c

    lax.fori_loop(0, nblk, pass_a, 0)

    def count(pred):
        def body(j, accs):
            return tuple(a + jnp.where(pred(g, key_s[g, j]), 1.0, 0.0) for g, a in zip(seqs, accs))
        accs = _loop(0, nblk, body, tuple(jnp.zeros((qb, LANE), F32) for _ in seqs), single)
        return [jnp.sum(a, axis=1, keepdims=True) for a in accs]

    keep = float(n_keep)
    int_min = jnp.int32(-2 ** 31)
    thr = tuple(jnp.where(c >= keep, jnp.int32(0), int_min) for c in count(lambda g, k: k >= 0))

    def search(it, thr):
        bit = lax.shift_left(jnp.int32(1), jnp.int32(30) - it)
        cand = [t | bit for t in thr]
        cnt = count(lambda g, k: k >= cand[g])
        return tuple(jnp.where(c >= keep, cd, t) for c, cd, t in zip(cnt, cand, thr))

    thr = lax.fori_loop(0, 31, search, thr)
    need = [keep - c for c in count(lambda g, k: k > thr[g])]
    tri = tri_ref[...]

    def pass_c(j, run):
        causal = j * LANE + lane <= t_pos
        out = []
        for g in seqs:
            key = key_s[g, j]
            eq = key == thr[g]
            eqf = jnp.where(eq, 1.0, 0.0)
            cum = jnp.dot(eqf.astype(BF16), tri, preferred_element_type=F32) + run[g]
            sel = (key > thr[g]) | (eq & (cum <= need[g]))
            key_s[g, j] = jnp.where(sel & causal, 1, 0)
            out.append(run[g] + jnp.sum(eqf, axis=1, keepdims=True))
        return tuple(out)

    lax.fori_loop(0, nblk, pass_c, tuple(jnp.zeros((qb, 1), F32) for _ in seqs))

    qs = [[_stack_heads(q_ref.at[g], hk) for hk in range(N_KV_HEADS)] for g in seqs]

    def pass_d(j, carry):
        off = pl.multiple_of(j * LANE, LANE)
        dsel = jnp.minimum(dq - j, 2)
        biases = [bias_ref[dsel, hk * GQA:(hk + 1) * GQA].reshape(GQA * qb, LANE) for hk in range(N_KV_HEADS)]
        out = []
        for g in seqs:
            kb = kv_ref[g, pl.ds(off, LANE), 0:LANE]
            vb = kv_ref[g, pl.ds(off, LANE), LANE:2 * LANE]
            msk = _tile4(key_s[g, j] > 0)
            out.append(tuple(_flash_step(qs[g][hk], kb, vb, biases[hk], msk, *carry[g][hk])
                             for hk in range(N_KV_HEADS)))
        return tuple(out)

    init = tuple(tuple(_flash_init(GQA * qb) for _ in range(N_KV_HEADS)) for _ in seqs)
    res = lax.fori_loop(0, nblk, pass_d, init)
    for g in seqs:
        _write_heads(o_ref.at[g], [acc / l for (_, l, acc) in res[g]], qb)


def _write_heads(o_ref, outs, qb):
    lane = lax.broadcasted_iota(I32, (qb, LANE), 1)
    for hk in range(N_KV_HEADS):
        valid = (lane // HEAD_DIM) == hk
        for g in range(GQA):
            h = hk * GQA + g
            o_ref[:, h * LANE:(h + 1) * LANE] = jnp.where(valid, outs[hk][g * qb:(g + 1) * qb], 0.0).astype(BF16)


def _dsa(q, qi, misc, kidx_b, kv_b, bias, tri, *, qb, q_base, n_keep, grp):
    b, rows, _ = q.shape
    nq = rows // qb
    s = kv_b.shape[1]
    kern = functools.partial(_dsa_kernel, grp=grp, qb=qb, q_base=q_base, n_keep=n_keep, single=(nq == 1))
    qspec = lambda w: pl.BlockSpec((grp, qb, w), lambda bi, i: (bi, i, 0))
    kspec = lambda w: pl.BlockSpec((grp, s, w), lambda bi, i: (bi, 0, 0))
    return pl.pallas_call(
        kern,
        grid=(b // grp, nq),
        in_specs=[qspec(N_ATT_HEADS * LANE), qspec(IDX_HEADS * LANE), qspec(LANE), kspec(LANE), kspec(2 * LANE),
                  _const_spec(bias), _const_spec(tri)],
        out_specs=qspec(N_ATT_HEADS * LANE),
        out_shape=jax.ShapeDtypeStruct((b, rows, N_ATT_HEADS * LANE), BF16),
        scratch_shapes=[pltpu.VMEM((grp, s // LANE, qb, LANE), I32)],
        compiler_params=_cparams("parallel", "parallel"),
        name="dsa_attention",
    )(q, qi, misc, kidx_b, kv_b, bias, tri)


QB = LANE
QW = GQA * QB


def _sub_sum(x):
    return jnp.sum(x, axis=0, keepdims=True)


def _flash_t_pair(blocks, qs, bias_ref, carry, acc_ref):
    logits = [[jnp.where(mk[hk], _dot_nt(kb, qs[hk]) + bias_ref[dsel, hk], NEG) for (kb, _, mk, dsel) in blocks]
              for hk in range(N_KV_HEADS)]
    out = []
    for hk in range(N_KV_HEADS):
        m, l = carry[hk]
        m_new = m
        for s in logits[hk]:
            m_new = jnp.maximum(m_new, jnp.max(s, axis=0, keepdims=True))
        alpha = jnp.exp(m - m_new)
        l = alpha * l
        pv = None
        for s, (_, vt, _, _) in zip(logits[hk], blocks):
            p = jnp.exp(s - m_new)
            l = l + _sub_sum(p)
            d = jnp.dot(vt, p.astype(BF16), preferred_element_type=F32)
            pv = d if pv is None else pv + d
        acc_ref[hk] = alpha * acc_ref[hk] + pv
        out.append((m_new, l))
    return tuple(out)


def _scaled_queries(q_ref, hk):
    return (_stack_heads(q_ref, hk).astype(F32) * ATT_SCALE).astype(BF16)


def _pair_loop(nblk, body, init):
    def body2(jj, c):
        return body(2 * jj + 1, body(2 * jj, c))
    return lax.fori_loop(0, (nblk + 1) // 2, body2, init)


def _flash_t_init():
    return (jnp.full((1, QW), NEG, F32), jnp.zeros((1, QW), F32))


def _tile_lanes(x):
    return jnp.concatenate([x] * GQA, axis=1)


def _write_heads_t(o_ref, o_ts):
    lane = lax.broadcasted_iota(I32, (QB, LANE), 1)
    for hk in range(N_KV_HEADS):
        valid = (lane // HEAD_DIM) == hk
        for g in range(GQA):
            h = hk * GQA + g
            o = o_ts[hk][:, g * QB:(g + 1) * QB].T
            o_ref[:, h * LANE:(h + 1) * LANE] = jnp.where(valid, o, 0.0).astype(BF16)


def _dsa_t_kernel(q_ref, qi_ref, misc_ref, kidx_ref, k_ref, vt_ref, bias_ref, trit_ref, o_ref, key_s, acc_s,
                  *, n_keep):
    i = pl.program_id(1)
    q0 = i * QB
    nblk = i + 1
    krow = lax.broadcasted_iota(I32, (LANE, QB), 0)
    qcol = lax.broadcasted_iota(I32, (LANE, QB), 1)
    misc_t = misc_ref[...].T
    wis = [_bf16_round(misc_t[IDX_DIM + h:IDX_DIM + h + 1, :]) for h in range(IDX_HEADS)]
    qi = qi_ref[...]
    idx_scale = (IDX_HEADS * IDX_DIM) ** -0.5

    def causal(j):
        return j * LANE + krow <= q0 + qcol

    def pass_a(j, c):
        kb = kidx_ref[pl.ds(pl.multiple_of(j * LANE, LANE), LANE), :]
        acc = jnp.zeros((LANE, QB), F32)
        for h in range(IDX_HEADS):
            acc = acc + _bf16_round(jnp.maximum(_dot_nt(kb, qi[:, h * LANE:(h + 1) * LANE]), 0.0)) * wis[h]
        key_s[j] = _order_keys(jnp.where(causal(j), acc * idx_scale, NEG))
        return c

    _pair_loop(nblk, pass_a, 0)

    def count(pred):
        def body(j, a):
            return a + jnp.where(pred(key_s[j]), 1.0, 0.0)
        return _sub_sum(_pair_loop(nblk, body, jnp.zeros((LANE, QB), F32)))

    keep = float(n_keep)
    thr = jnp.where(count(lambda k: k >= 0) >= keep, jnp.int32(0), jnp.int32(-2 ** 31))

    def search(it, thr):
        cand = thr | lax.shift_left(jnp.int32(1), jnp.int32(30) - it)
        return jnp.where(count(lambda k: k >= cand) >= keep, cand, thr)

    thr = lax.fori_loop(0, 31, search, thr)
    need = keep - count(lambda k: k > thr)
    trit = trit_ref[...]

    def pass_c(j, run):
        key = key_s[j]
        eq = key == thr
        eqf = jnp.where(eq, 1.0, 0.0)
        cum = jnp.dot(trit, eqf.astype(BF16), preferred_element_type=F32) + run
        sel = ((key > thr) | (eq & (cum <= need))) & causal(j)
        key_s[j] = jnp.where(sel, 1, 0)
        return run + _sub_sum(eqf)

    _pair_loop(nblk, pass_c, jnp.zeros((1, QB), F32))

    qs = [_scaled_queries(q_ref, hk) for hk in range(N_KV_HEADS)]
    acc_s[...] = jnp.zeros_like(acc_s)

    def pass_d(jj, carry):
        blocks = []
        for j in (2 * jj, 2 * jj + 1):
            kb = k_ref[pl.ds(pl.multiple_of(j * LANE, LANE), LANE), :]
            blocks.append((kb, vt_ref[j], [_tile_lanes(key_s[j] > 0)] * N_KV_HEADS, jnp.clip(i - j, 0, 2)))
        return _flash_t_pair(blocks, qs, bias_ref, carry, acc_s)

    res = lax.fori_loop(0, (nblk + 1) // 2, pass_d, tuple(_flash_t_init() for _ in range(N_KV_HEADS)))
    _write_heads_t(o_ref, [acc_s[hk] / res[hk][1] for hk in range(N_KV_HEADS)])


def _bias_tiles_t(rel_bias):
    t = _bias_tiles(rel_bias, QB)
    t = t.reshape(3, N_KV_HEADS, GQA, QB, LANE).transpose(0, 1, 4, 2, 3)
    return t.reshape(3, N_KV_HEADS, LANE, QW)


def _blocks_t(x):
    b, s, w = x.shape
    return x.reshape(b, s // LANE, LANE, w).transpose(0, 1, 3, 2)


def _dsa_t(q, qi, misc, kidx_b, kv_b, bias_t, trit, *, n_keep):
    b, s, _ = q.shape
    vt = _blocks_t(kv_b[:, :, LANE:])
    qspec = lambda w: pl.BlockSpec((None, QB, w), lambda bi, i: (bi, i, 0))
    kspec = pl.BlockSpec((None, s, LANE), lambda bi, i: (bi, 0, 0))
    return pl.pallas_call(
        functools.partial(_dsa_t_kernel, n_keep=n_keep),
        grid=(b, s // QB),
        in_specs=[qspec(N_ATT_HEADS * LANE), qspec(IDX_HEADS * LANE), qspec(LANE), kspec, kspec,
                  pl.BlockSpec((None, s // LANE, LANE, LANE), lambda bi, i: (bi, 0, 0, 0)),
                  _const_spec(bias_t), _const_spec(trit)],
        out_specs=qspec(N_ATT_HEADS * LANE),
        out_shape=jax.ShapeDtypeStruct((b, s, N_ATT_HEADS * LANE), BF16),
        scratch_shapes=[pltpu.VMEM((s // LANE, LANE, QB), I32), pltpu.VMEM((N_KV_HEADS, LANE, QW), F32)],
        compiler_params=_cparams("parallel", "parallel"),
        name="dsa_attention_t",
    )(q, qi, misc, kidx_b, kv_b, vt, bias_t, trit)


def _dot_split_rhs(m, x):
    hi = x.astype(BF16)
    r1 = x - hi.astype(F32)
    mid = r1.astype(BF16)
    lo = (r1 - mid.astype(F32)).astype(BF16)
    return (jnp.dot(m, hi, preferred_element_type=F32) + jnp.dot(m, mid, preferred_element_type=F32)
            + jnp.dot(m, lo, preferred_element_type=F32))


def _nsa_t_kernel(q_ref, gates_ref, kc_ref, vct_ref, ks_ref, vst_ref, kw_ref, vwt_ref, bias_ref, biasc_ref,
                  pairt_ref, o_ref, acc_s, *, n_cmp, n_sel):
    i = pl.program_id(1)
    q0 = i * QB
    nblk = i + 1
    heads = range(N_KV_HEADS)
    krow = lax.broadcasted_iota(I32, (LANE, QB), 0)
    qcol = lax.broadcasted_iota(I32, (LANE, QB), 1)
    t_pos = q0 + qcol
    krow_f = krow.astype(F32)
    qs = [_scaled_queries(q_ref, hk) for hk in heads]
    gates_t = gates_ref[...].T

    def gate_row(br, hk):
        return jnp.concatenate([gates_t[br * N_ATT_HEADS + hk * GQA + g:br * N_ATT_HEADS + hk * GQA + g + 1, :]
                                for g in range(GQA)], axis=1)

    mask_c = _tile_lanes(((krow * CMP_BLOCK + CMP_BLOCK - 1) <= t_pos) & (krow < n_cmp))
    cur = t_pos // SEL_BLOCK
    forced = (krow == 0) | (krow == cur) | (krow == cur - 1)
    sel_causal = krow * SEL_BLOCK <= t_pos
    o_cmp, picked = [], []
    for hk in heads:
        s = jnp.where(mask_c, _dot_nt(kc_ref[...], qs[hk]) + biasc_ref[hk], NEG)
        e = jnp.exp(s - jnp.max(s, axis=0, keepdims=True))
        p = jnp.where(mask_c, e / _sub_sum(e), 0.0)
        o_cmp.append(jnp.dot(vct_ref[...], p.astype(BF16), preferred_element_type=F32))
        ps = p[:, 0:QB]
        for g in range(1, GQA):
            ps = ps + p[:, g * QB:(g + 1) * QB]
        score = _dot_split_rhs(pairt_ref[...], ps)
        score = jnp.where(sel_causal, jnp.where(forced, FORCE, score), NEG)
        pk = jnp.zeros((LANE, QB), F32)
        for _ in range(n_sel):
            mx = jnp.max(score, axis=0, keepdims=True)
            first = jnp.min(jnp.where(score == mx, krow_f, float(LANE)), axis=0, keepdims=True)
            hit = krow_f == first
            pk = jnp.where(hit, 1.0, pk)
            score = jnp.where(hit, TAKEN, score)
        picked.append(pk.astype(BF16))

    def key_block(k_ref, vt_ref, j):
        return k_ref[pl.ds(pl.multiple_of(j * LANE, LANE), LANE), :], vt_ref[j]

    erow = lax.broadcasted_iota(I32, (LANE, LANE), 0)
    ecol = lax.broadcasted_iota(I32, (LANE, LANE), 1)
    acc_s[...] = jnp.zeros_like(acc_s)

    def slc_body(jj, carry):
        blocks = []
        for j in (2 * jj, 2 * jj + 1):
            kb, vt = key_block(ks_ref, vst_ref, j)
            expand = jnp.where(ecol == 2 * j + erow // SEL_BLOCK, 1.0, 0.0).astype(BF16)
            causal = j * LANE + krow <= t_pos
            masks = [_tile_lanes((jnp.dot(expand, picked[hk], preferred_element_type=F32) > 0.5) & causal)
                     for hk in heads]
            blocks.append((kb, vt, masks, jnp.clip(i - j, 0, 2)))
        return _flash_t_pair(blocks, qs, bias_ref, carry, acc_s.at[0])

    res_s = lax.fori_loop(0, (nblk + 1) // 2, slc_body, tuple(_flash_t_init() for _ in heads))

    lo = jnp.maximum(i - WINDOW // LANE - 1, 0)

    def win_body(jj, carry):
        blocks = []
        for j in (lo + 2 * jj, lo + 2 * jj + 1):
            kb, vt = key_block(kw_ref, vwt_ref, j)
            dist = t_pos - (j * LANE + krow)
            mask = _tile_lanes((dist >= 0) & (dist < WINDOW))
            blocks.append((kb, vt, [mask] * N_KV_HEADS, jnp.clip(i - j, 0, 2)))
        return _flash_t_pair(blocks, qs, bias_ref, carry, acc_s.at[1])

    res_w = lax.fori_loop(0, (i - lo + 2) // 2, win_body, tuple(_flash_t_init() for _ in heads))

    _write_heads_t(o_ref, [gate_row(0, hk) * o_cmp[hk] + gate_row(1, hk) * (acc_s[0, hk] / res_s[hk][1])
                           + gate_row(2, hk) * (acc_s[1, hk] / res_w[hk][1]) for hk in heads])


def _bias_cmp_t(rel_bias, s):
    t = _bias_cmp(rel_bias, [j * QB for j in range(s // QB)], QB)
    t = t.reshape(s // QB, N_KV_HEADS, GQA, QB, LANE).transpose(0, 1, 4, 2, 3)
    return t.reshape(s // QB, N_KV_HEADS, LANE, QW)


def _nsa_t(q, gates, kcv, kvs_b, kvw_b, bias_t, bias_c, pair_t, *, n_cmp, n_sel):
    b, s, _ = q.shape
    vct = kcv[:, :, LANE:].transpose(0, 2, 1)
    qspec = lambda w: pl.BlockSpec((None, QB, w), lambda bi, i: (bi, i, 0))
    kspec = pl.BlockSpec((None, s, LANE), lambda bi, i: (bi, 0, 0))
    vspec = pl.BlockSpec((None, s // LANE, LANE, LANE), lambda bi, i: (bi, 0, 0, 0))
    cspec = pl.BlockSpec((None, LANE, LANE), lambda bi, i: (bi, 0, 0))
    return pl.pallas_call(
        functools.partial(_nsa_t_kernel, n_cmp=n_cmp, n_sel=n_sel),
        grid=(b, s // QB),
        in_specs=[qspec(N_ATT_HEADS * LANE), qspec(LANE), cspec, cspec, kspec, vspec, kspec, vspec,
                  _const_spec(bias_t), pl.BlockSpec((None,) + bias_c.shape[1:], lambda bi, i: (i, 0, 0, 0)),
                  _const_spec(pair_t)],
        out_specs=qspec(N_ATT_HEADS * LANE),
        out_shape=jax.ShapeDtypeStruct((b, s, N_ATT_HEADS * LANE), BF16),
        scratch_shapes=[pltpu.VMEM((2, N_KV_HEADS, LANE, QW), F32)],
        compiler_params=_cparams("parallel", "parallel"),
        name="nsa_attention_t",
    )(q, gates, kcv, vct, kvs_b, _blocks_t(kvs_b[:, :, LANE:]), kvw_b, _blocks_t(kvw_b[:, :, LANE:]),
      bias_t, bias_c, pair_t)


def _select_top(keys, n_keep, tri):
    keep = float(n_keep)
    n = keys[0].shape[1]

    def count(pred):
        return [jnp.sum(jnp.where(pred(g, k), 1.0, 0.0), axis=1, keepdims=True) for g, k in enumerate(keys)]

    int_min = jnp.int32(-2 ** 31)
    thr = tuple(jnp.where(c >= keep, jnp.int32(0), int_min) for c in count(lambda g, k: k >= 0))

    def search(it, thr):
        bit = lax.shift_left(jnp.int32(1), jnp.int32(30) - it)
        cand = [t | bit for t in thr]
        cnt = count(lambda g, k: k >= cand[g])
        return tuple(jnp.where(c >= keep, cd, t) for c, cd, t in zip(cnt, cand, thr))

    thr = lax.fori_loop(0, 31, search, thr)
    need = [keep - c for c in count(lambda g, k: k > thr[g])]
    sels = []
    for g, k in enumerate(keys):
        run = jnp.zeros((1, 1), F32)
        parts = []
        for t in range(n // LANE):
            kt = k[:, t * LANE:(t + 1) * LANE]
            eq = kt == thr[g]
            eqf = jnp.where(eq, 1.0, 0.0)
            cum = jnp.dot(eqf.astype(BF16), tri, preferred_element_type=F32) + run
            parts.append((kt > thr[g]) | (eq & (cum <= need[g])))
            run = run + jnp.sum(eqf, axis=1, keepdims=True)
        sels.append(jnp.concatenate(parts, axis=1))
    return sels


def _order_keys(score):
    score = jnp.where(score == 0.0, 0.0, score)
    bits = lax.bitcast_convert_type(score, I32)
    return jnp.where(bits < 0, bits ^ jnp.int32(0x7FFFFFFF), bits)


def _head_rows_out(o_ref, g, acc):
    rowh = lax.broadcasted_iota(I32, (N_ATT_HEADS, LANE), 0)
    laneh = lax.broadcasted_iota(I32, (N_ATT_HEADS, LANE), 1)
    o_ref[g] = jnp.where((laneh // HEAD_DIM) == (rowh // GQA), acc, 0.0).astype(BF16)


def _dsa_sample_kernel(pt_ref, q_ref, qi_ref, wi_ref, knew_ref, kvnew_ref, bias_ref, tri_ref, *refs,
                       grp, n_pages, n_keep):
    del pt_ref
    ki_refs, kv_refs, o_ref = refs[:grp * n_pages], refs[grp * n_pages:2 * grp * n_pages], refs[-1]
    lane1 = lax.broadcasted_iota(I32, (1, LANE), 1)
    idx_scale = (IDX_HEADS * IDX_DIM) ** -0.5
    keys = []
    for g in range(grp):
        qi = qi_ref[g]
        wi = _bf16_round(wi_ref[g])
        tiles = []
        for p in range(n_pages):
            rel = _bf16_round(jnp.maximum(_dot(qi, ki_refs[g * n_pages + p][...]), 0.0))
            tiles.append(jnp.sum(rel * wi, axis=0, keepdims=True) * idx_scale)
        rel_new = _bf16_round(jnp.maximum(jnp.sum(qi.astype(F32) * _bf16_round(knew_ref[g]), axis=1, keepdims=True),
                                          0.0))
        sc_new = jnp.sum(rel_new * wi, axis=0, keepdims=True) * idx_scale
        tiles.append(jnp.where(lane1 == 0, sc_new, NEG))
        keys.append(_order_keys(jnp.concatenate(tiles, axis=1)))
    sels = _select_top(keys, n_keep, tri_ref[...])
    bias = bias_ref[...]
    for g in range(grp):
        q = q_ref[g]
        kvnew = _bf16_round(kvnew_ref[g])
        tiles = [_dot(q, kv_refs[g * n_pages + p][0:LANE, :]) for p in range(n_pages)]
        s_new = jnp.sum(q.astype(F32) * kvnew[:, 0:LANE], axis=1, keepdims=True)
        tiles.append(jnp.where(lane1 == 0, s_new, 0.0))
        valid = sels[g] & (jnp.concatenate([lane1] * n_pages + [lane1 + LANE], axis=1) <= LANE)
        s = jnp.where(valid, jnp.concatenate(tiles, axis=1) * ATT_SCALE + bias, NEG)
        e = jnp.exp(s - jnp.max(s, axis=1, keepdims=True))
        p_all = jnp.where(valid, e / jnp.sum(e, axis=1, keepdims=True), 0.0)
        acc = p_all[:, n_pages * LANE:n_pages * LANE + 1] * kvnew[:, LANE:2 * LANE]
        for p in range(n_pages):
            acc = acc + _dot_nt(p_all[:, p * LANE:(p + 1) * LANE], kv_refs[g * n_pages + p][LANE:2 * LANE, :])
        _head_rows_out(o_ref, g, acc)


def _page_specs(pool_t, n_pages, grp):
    r, c = pool_t.shape[1:]
    return [pl.BlockSpec((None, r, c), lambda i, pt, g=g, p=p: (pt[i * grp + g, p], 0, 0))
            for g in range(grp) for p in range(n_pages)]


def _dsa_sample(q8, qi8, wi8, knew, kvnew, bias, tri, ki_t, kv_t, page_table, *, n_keep, grp):
    b, n_pages = page_table.shape
    gspec = lambda a: pl.BlockSpec((grp,) + a.shape[1:], lambda i, pt: (i,) + (0,) * (a.ndim - 1))
    cspec = lambda a: pl.BlockSpec(a.shape, lambda i, pt: (0,) * a.ndim)
    kern = functools.partial(_dsa_sample_kernel, grp=grp, n_pages=n_pages, n_keep=n_keep)
    return pl.pallas_call(
        kern,
        grid_spec=pltpu.PrefetchScalarGridSpec(
            num_scalar_prefetch=1, grid=(b // grp,),
            in_specs=[gspec(q8), gspec(qi8), gspec(wi8), gspec(knew), gspec(kvnew), cspec(bias), cspec(tri)]
            + _page_specs(ki_t, n_pages, grp) + _page_specs(kv_t, n_pages, grp),
            out_specs=pl.BlockSpec((grp, N_ATT_HEADS, LANE), lambda i, pt: (i, 0, 0))),
        out_shape=jax.ShapeDtypeStruct((b, N_ATT_HEADS, LANE), BF16),
        compiler_params=_cparams("parallel"),
        name="dsa_sample",
    )(page_table, q8, qi8, wi8, knew, kvnew, bias, tri, *([ki_t] * (grp * n_pages)), *([kv_t] * (grp * n_pages)))


def _route(hf, wrt, br):
    logits = _dot_nt(wrt, hf)
    s = _sigmoid(logits)
    sel = s + br
    rows = [sel[e:e + 1, :] for e in range(N_EXPERTS)]
    grp = []
    for g in range(N_GROUPS):
        a = rows[g * EXPERTS_PER_GROUP:(g + 1) * EXPERTS_PER_GROUP]
        best = None
        for i in range(EXPERTS_PER_GROUP):
            for j in range(i + 1, EXPERTS_PER_GROUP):
                v = a[i] + a[j]
                best = v if best is None else jnp.maximum(best, v)
        grp.append(best)
    gbest = jnp.zeros_like(grp[0], dtype=I32)
    cur = grp[0]
    for g in range(1, N_GROUPS):
        better = grp[g] > cur
        gbest = jnp.where(better, g, gbest)
        cur = jnp.where(better, grp[g], cur)
    picked = []
    for g in range(N_GROUPS):
        a = rows[g * EXPERTS_PER_GROUP:(g + 1) * EXPERTS_PER_GROUP]
        for j in range(EXPERTS_PER_GROUP):
            rank = jnp.zeros_like(a[j])
            for jj in range(EXPERTS_PER_GROUP):
                if jj != j:
                    ahead = (a[jj] > a[j]) | (a[jj] == a[j]) if jj < j else (a[jj] > a[j])
                    rank = rank + jnp.where(ahead, 1.0, 0.0)
            e = g * EXPERTS_PER_GROUP + j
            picked.append(jnp.where((gbest == g) & (rank < 2.0), s[e:e + 1, :], 0.0))
    den = picked[0]
    for p in picked[1:]:
        den = den + p
    return jnp.concatenate([p / den for p in picked], axis=0)


def _post_tail(x, mix, gate, gf, shf, scf, wrt_ref, br_ref, x2_ref, hf_ref, cwt_ref):
    x2 = x + gate * mix
    x2_ref[...] = x2
    hf = _modulate(x2, gf, shf, scf)
    hf_ref[...] = hf.astype(BF16)
    cwt_ref[...] = _route(hf, wrt_ref[...], br_ref[...])


def _post_even_kernel(x_ref, oa_ref, bg_ref, u_ref, um1_ref, um2_ref, gate_ref, shf_ref, scf_ref,
                      gf_ref, cw_ref, cb_ref, wo_ref, wrt_ref, br_ref, x2_ref, hf_ref, cwt_ref):
    cw = cw_ref[...]
    y = cb_ref[...] + cw[0:1, :] * um2_ref[...]
    y = y + cw[1:2, :] * um1_ref[...]
    y = y + cw[2:3, :] * u_ref[...]
    n_a = N_ATT_HEADS * LANE
    mix = (jnp.dot(oa_ref[...], wo_ref[0:n_a, :], preferred_element_type=F32)
           + jnp.dot((bg_ref[...] * y).astype(BF16), wo_ref[n_a:n_a + CONV_CH, :], preferred_element_type=F32))
    _post_tail(x_ref[...], mix, gate_ref[...], gf_ref[...], shf_ref[...], scf_ref[...], wrt_ref, br_ref,
               x2_ref, hf_ref, cwt_ref)


POST_OUTS = [(1024, F32), (1024, BF16)]


def _moe_kernel(hf_ref, cw_ref, x2_ref, gate_ref, wg_ref, wu_ref, wd_ref, o_ref, acc_ref):
    e = pl.program_id(1)

    @pl.when(e == 0)
    def _():
        acc_ref[...] = jnp.zeros_like(acc_ref)

    hf = hf_ref[...]
    hmid = _silu(_dot(hf, wg_ref[...])) * _dot(hf, wu_ref[...])
    cw = cw_ref[...]
    lane = lax.broadcasted_iota(I32, cw.shape, 1)
    wcol = jnp.sum(jnp.where(lane == e, cw, 0.0), axis=1, keepdims=True)
    acc_ref[...] += _dot(hmid, wd_ref[...]) * wcol

    @pl.when(e == N_EXPERTS - 1)
    def _():
        o_ref[...] = x2_ref[...] + gate_ref[...] * acc_ref[...]


def _moe(hf, cw, x2, gate, wg, wu, wd, layer, tile, tpb):
    rows, d = x2.shape
    de = wg.shape[3]
    return pl.pallas_call(
        _moe_kernel,
        grid=(rows // tile, N_EXPERTS),
        in_specs=[pl.BlockSpec((tile, d), lambda t, e: (t, 0)),
                  pl.BlockSpec((tile, N_EXPERTS), lambda t, e: (t, 0)),
                  pl.BlockSpec((tile, d), lambda t, e: (t, 0)),
                  pl.BlockSpec((None,) + gate.shape[1:], lambda t, e: (t // tpb, 0, 0)),
                  pl.BlockSpec((None, None, d, de), lambda t, e: (layer, e, 0, 0)),
                  pl.BlockSpec((None, None, d, de), lambda t, e: (layer, e, 0, 0)),
                  pl.BlockSpec((None, None, de, d), lambda t, e: (layer, e, 0, 0))],
        out_specs=pl.BlockSpec((tile, d), lambda t, e: (t, 0)),
        out_shape=jax.ShapeDtypeStruct((rows, d), F32),
        scratch_shapes=[pltpu.VMEM((tile, d), F32)],
        compiler_params=_cparams("parallel", "arbitrary"),
        name="moe_dense",
    )(hf, cw, x2, gate, wg, wu, wd)


def _cache_rows(kv_t):
    b, _, s = kv_t.shape
    return kv_t.reshape(b, 2, N_KV_HEADS, HEAD_DIM, s).transpose(0, 4, 1, 2, 3)


def _shift_rows(u3, k):
    return jnp.pad(u3, ((0, 0), (k, 0), (0, 0)))[:, :u3.shape[1]]


def _even_layer(xp, xs, mp, ms, page_table, kv_pool, kidx_pool, conv_buf, rel_bias, router, experts,
                g_mix, g_ffn, w_in, w_out, q_norm, k_norm, conv_w, conv_b):
    B, S, D = xp.shape
    Bs = xs.shape[0]
    past = page_table.shape[1] * PAGE_SIZE
    wts = _even_weights(w_in, w_out, q_norm, k_norm)
    gm = _gmats()
    wrt, br = router
    wg, wu, wd, lyr = experts
    g_mix = g_mix.reshape(1, D)
    g_ffn = g_ffn.reshape(1, D)
    post_consts = [g_ffn, conv_w, conv_b.reshape(1, CONV_CH), wts[1], wrt, br]

    tp = ROW_TILE
    q_p, _, kv_b, qi_p, misc, misc_b, bg, u, kv_t, misc_t = _inproj_even(xp.reshape(B * S, D), [mp[0], mp[1]],
                                                                        g_mix, wts, gm, tp, S // tp)
    r3 = lambda a: a.reshape(B, S, a.shape[-1])
    oa = _dsa_t(r3(q_p), r3(qi_p), r3(misc), r3(misc_b), r3(kv_b), _bias_tiles_t(rel_bias), gm[2].T,
                n_keep=min(TOPK_MAX, S // 4))
    u3 = r3(u)
    um1 = _shift_rows(u3, 1).reshape(B * S, CONV_CH)
    um2 = _shift_rows(u3, 2).reshape(B * S, CONV_CH)
    x2, hf, cwt = _row_call(_post_even_kernel, [xp.reshape(B * S, D), oa.reshape(B * S, -1), bg, u, um1, um2],
                            [mp[2], mp[3], mp[4]], post_consts, POST_OUTS, tp, S // tp, "post_even",
                            outs_t=[(N_EXPERTS, F32)])
    tm = min(MOE_TILE, S)
    xp3 = _moe(hf, cwt.T, x2, mp[5], wg, wu, wd, lyr, tm, S // tm).reshape(B, S, D)
    outs_p = (_cache_rows(kv_t), misc_t[:, :IDX_DIM].transpose(0, 2, 1), u3[:, S - 2:])

    q_s, kv_fs, _, qi_s, misc_s, _, bg_s, u_s, _, _ = _inproj_even(xs, [ms[0], ms[1]], g_mix, wts, gm, Bs, 1)
    n_pool = kv_pool.shape[0]
    kv_t = kv_pool.transpose(0, 2, 3, 4, 1).reshape(n_pool, 2 * LANE, PAGE_SIZE)
    ki_t = kidx_pool.transpose(0, 2, 1)
    qi8 = jnp.pad(qi_s.reshape(Bs, IDX_HEADS, LANE)[:, :, :IDX_DIM], ((0, 0), (0, N_ATT_HEADS - IDX_HEADS), (0, 0)))
    wi8 = jnp.pad(misc_s[:, IDX_DIM:IDX_DIM + IDX_HEADS], ((0, 0), (0, N_ATT_HEADS - IDX_HEADS)))[:, :, None]
    oa_s = _dsa_sample(q_s.reshape(Bs, N_ATT_HEADS, LANE), qi8, wi8, misc_s[:, None, :IDX_DIM], kv_fs[:, None, :],
                       _bias_row(rel_bias, past), gm[2], ki_t, kv_t, page_table,
                       n_keep=min(TOPK_MAX, (past + 1) // 4), grp=_pick_group(Bs, SAMPLE_GROUP)).reshape(Bs, -1)
    x2s, hfs, cwts = _row_call(_post_even_kernel, [xs, oa_s, bg_s, u_s, conv_buf[:, 1], conv_buf[:, 0]],
                               [ms[2], ms[3], ms[4]], post_consts, POST_OUTS, Bs, 1, "post_even_s",
                               outs_t=[(N_EXPERTS, F32)])
    xs3 = _moe(hfs, cwts.T, x2s, ms[5], wg, wu, wd, lyr, Bs, 1)
    outs_s = (kv_fs.reshape(Bs, 1, 2, N_KV_HEADS, HEAD_DIM), misc_s[:, None, :IDX_DIM],
              jnp.concatenate([conv_buf[:, 1:], u_s[:, None, :]], axis=1))
    return xp3, xs3, outs_p, outs_s


O_Z0, O_Q0, O_KVC0, O_KVS0, O_KVW0, O_G0, O_END = 0, 1792, 2816, 3072, 3328, 3584, 3712
P_C = 3 * RWKV_DIM + LORA_W + LORA_A + LORA_G


def _inproj_odd_kernel(x_ref, shift_ref, scale_ref, g_ref, w_ref, qg_ref, ksg_ref, kwg_ref, gone_ref, gtwo_ref,
                       zc_ref, q_ref, kvc_ref, kvs_ref, kvsb_ref, kvw_ref, kvwb_ref, gates_ref,
                       kvct_ref, kvst_ref, kvwt_ref):
    h = _modulate(x_ref[...], g_ref[...], shift_ref[...], scale_ref[...])
    z = jnp.dot(h.astype(BF16), w_ref[...], preferred_element_type=F32)
    zc_ref[...] = z[:, O_Z0:O_Q0]
    gone = gone_ref[...]
    gtwo = gtwo_ref[...]
    for t in range(N_ATT_HEADS):
        sl = slice(t * LANE, (t + 1) * LANE)
        q_ref[:, sl] = _group_rms(z[:, O_Q0 + t * LANE:O_Q0 + (t + 1) * LANE], gone, qg_ref[:, sl]).astype(BF16)
    kvc_ref[...] = z[:, O_KVC0:O_KVS0]
    kvct_ref[0:LANE, :] = z[:, O_KVC0:O_KVC0 + LANE].T
    kvct_ref[LANE:2 * LANE, :] = z[:, O_KVC0 + LANE:O_KVS0].T
    for base, gain_ref, f_ref, b_ref, t_ref in ((O_KVS0, ksg_ref, kvs_ref, kvsb_ref, kvst_ref),
                                                (O_KVW0, kwg_ref, kvw_ref, kvwb_ref, kvwt_ref)):
        k = _group_rms(z[:, base:base + LANE], gtwo, gain_ref[...])
        v = z[:, base + LANE:base + 2 * LANE]
        t_ref[0:LANE, :] = k.T
        t_ref[LANE:2 * LANE, :] = v.T
        f_ref[:, 0:LANE] = k
        f_ref[:, LANE:2 * LANE] = v
        b_ref[:, 0:LANE] = k.astype(BF16)
        b_ref[:, LANE:2 * LANE] = v.astype(BF16)
    gates_ref[...] = _sigmoid(z[:, O_G0:O_END])


def _odd_weights(w_in, w_out, q_norm, k_norm):
    d = w_in.shape[0]
    a_q, a_kv = N_ATT_HEADS * HEAD_DIM, 2 * N_KV_HEADS * HEAD_DIM
    o = P_C
    wz = w_in[:, :o]
    wq = w_in[:, o:o + a_q]; o += a_q
    wkv = w_in[:, o:o + 3 * a_kv]; o += 3 * a_kv
    wg = w_in[:, o:]
    wg = jnp.concatenate([wg, jnp.zeros((d, LANE - wg.shape[1]), F32)], -1)
    w_in_p = jnp.concatenate([wz, _pad_q_cols(wq), wkv, wg], -1).astype(BF16)
    w_out_p = jnp.concatenate([w_out[:RWKV_DIM], _pad_o_rows(w_out[RWKV_DIM:])], 0).astype(BF16)
    qg = jnp.tile(q_norm, 2 * N_ATT_HEADS).reshape(1, N_ATT_HEADS * LANE)
    ksg = jnp.tile(k_norm[1], 2).reshape(1, LANE)
    kwg = jnp.tile(k_norm[2], 2).reshape(1, LANE)
    return w_in_p, w_out_p, qg, ksg, kwg


def _inproj_odd(x, mods, g, wts, gm, tile, tpb):
    w_in_p, _, qg, ksg, kwg = wts
    gone, gtwo, _ = gm
    outs = [(P_C, F32), (N_ATT_HEADS * LANE, BF16), (2 * LANE, F32), (2 * LANE, F32), (2 * LANE, BF16),
            (2 * LANE, F32), (2 * LANE, BF16), (LANE, F32)]
    return _row_call(_inproj_odd_kernel, x, mods, [g, w_in_p, qg, ksg, kwg, gone, gtwo], outs, tile, tpb,
                     "inproj_odd", outs_bt=[(2 * LANE, F32)] * 3)


def _rwkv_pre_kernel(z_ref, zp_ref, mu_ref, w0_ref, a0_ref, kk_ref, ka_ref, wup_ref, aup_ref, gup_ref, gsum_ref,
                     r_o, w_o, k_o, v_o, kk_o, kka_o, g_o):
    z = z_ref[...]
    zm = z + (zp_ref[...] - z) * mu_ref[...]
    r = zm[:, 0:RWKV_DIM]
    k = zm[:, RWKV_DIM:2 * RWKV_DIM]
    v = zm[:, 2 * RWKV_DIM:3 * RWKV_DIM]
    t12 = zm[:, 3 * RWKV_DIM:3 * RWKV_DIM + LANE]
    gd = zm[:, 3 * RWKV_DIM + LANE:P_C]
    xw = w0_ref[...] + _dot(jnp.tanh(t12), wup_ref[...])
    sp = jnp.maximum(-xw, 0.0) + jnp.log(1.0 + jnp.exp(-jnp.abs(xw)))
    w_o[...] = jnp.exp(-jnp.exp(-sp - 0.5))
    a = _sigmoid(a0_ref[...] + _dot(t12, aup_ref[...]))
    g_o[...] = _dot(_sigmoid(gd), gup_ref[...])
    kk = k * kk_ref[...]
    gsum = gsum_ref[...]
    for t in range(RWKV_DIM // LANE):
        sl = slice(t * LANE, (t + 1) * LANE)
        kt = kk[:, sl]
        nrm = jnp.maximum(jnp.sqrt(_dot_split(kt * kt, gsum)), 1e-12)
        kn = kt / nrm
        kk_o[:, sl] = kn
        kka_o[:, sl] = kn * a[:, sl]
    r_o[...] = r
    v_o[...] = v
    k_o[...] = k * (1.0 + (a - 1.0) * ka_ref[...])


def _rwkv_pre(zc, zprev, cpar, gsum, tile):
    mu, w0, w_up, a0, a_up, g_up, k_k, k_a = cpar
    z64 = jnp.zeros((LORA_W, RWKV_DIM), F32)
    consts = [mu.reshape(1, P_C), w0.reshape(1, -1), a0.reshape(1, -1), k_k.reshape(1, -1), k_a.reshape(1, -1),
              jnp.concatenate([w_up, z64], 0).astype(BF16), jnp.concatenate([z64, a_up], 0).astype(BF16),
              g_up.astype(BF16), gsum]
    return _row_call(_rwkv_pre_kernel, [zc, zprev], [], consts, [(RWKV_DIM, F32)] * 7, tile, 1, "rwkv_pre")


SCAN_P = 64
SCAN_VH = HEAD_DIM // 2


def _scan_kernel(kk_ref, w_ref, kka_ref, k_ref, r_ref, v_ref, s0_ref, y_ref, so_ref, st, *, tc):
    ti = pl.program_id(1)

    @pl.when(ti == 0)
    def _():
        st[...] = s0_ref[...]

    def step(t, c):
        kk, w, kka, kt, rt, vt = kk_ref[t], w_ref[t], kka_ref[t], k_ref[t], r_ref[t], v_ref[t]
        ys = []
        for vi in range(SCAN_VH):
            s = st[vi]
            sa = -jnp.sum(s * kk, axis=0, keepdims=True)
            sn = s * w + sa * kka + vt[vi:vi + 1, :] * kt
            st[vi] = sn
            ys.append(jnp.sum(sn * rt, axis=0, keepdims=True))
        y_ref[t] = jnp.concatenate(ys, axis=0)
        return c

    lax.fori_loop(0, tc, step, 0)

    @pl.when(ti == pl.num_programs(1) - 1)
    def _():
        so_ref[...] = st[...]


def _scan_layout(x, b, t):
    p = b * RWKV_HEADS
    a = x.reshape(b, t, RWKV_HEADS, HEAD_DIM).transpose(1, 3, 0, 2).reshape(t, HEAD_DIM, p)
    nc = -(-p // SCAN_P)
    return jnp.pad(a, ((0, 0), (0, 0), (0, nc * SCAN_P - p))).reshape(t, HEAD_DIM, nc, SCAN_P).transpose(2, 0, 1, 3)


def _scan_unlayout_y(y, b, t):
    p = b * RWKV_HEADS
    nc = y.shape[0]
    a = jnp.concatenate([y[..., :SCAN_P], y[..., SCAN_P:]], axis=2)
    a = a.transpose(1, 2, 0, 3).reshape(t, HEAD_DIM, nc * SCAN_P)[:, :, :p]
    return a.reshape(t, HEAD_DIM, b, RWKV_HEADS).transpose(2, 0, 3, 1).reshape(b * t, RWKV_DIM)


def _scan_layout_state(s):
    b = s.shape[0]
    p = b * RWKV_HEADS
    nc = -(-p // SCAN_P)
    a = jnp.pad(s.reshape(p, HEAD_DIM, HEAD_DIM), ((0, nc * SCAN_P - p), (0, 0), (0, 0)))
    a = a.reshape(nc, SCAN_P, HEAD_DIM, HEAD_DIM).transpose(0, 2, 3, 1)
    return jnp.concatenate([a[:, :SCAN_VH], a[:, SCAN_VH:]], -1)


def _scan_unlayout_state(st, b):
    p = b * RWKV_HEADS
    nc = st.shape[0]
    a = jnp.concatenate([st[..., :SCAN_P], st[..., SCAN_P:]], axis=1)
    a = a.transpose(0, 3, 1, 2).reshape(nc * SCAN_P, HEAD_DIM, HEAD_DIM)[:p]
    return a.reshape(b, RWKV_HEADS, HEAD_DIM, HEAD_DIM)


def _rwkv_scan(pre, s0, b, t, tc):
    r, w, k, v, kk, kka, _ = pre
    ks = [jnp.concatenate([a, a], -1) for a in (_scan_layout(x, b, t) for x in (kk, w, kka, k, r))]
    vs = _scan_layout(v, b, t)
    vs = jnp.concatenate([vs[:, :, :SCAN_VH], vs[:, :, SCAN_VH:]], -1)
    s0l = _scan_layout_state(s0)
    nc = s0l.shape[0]
    kspec = pl.BlockSpec((None, tc, HEAD_DIM, LANE), lambda c, i: (c, i, 0, 0))
    vspec = pl.BlockSpec((None, tc, SCAN_VH, LANE), lambda c, i: (c, i, 0, 0))
    sspec = pl.BlockSpec((None, SCAN_VH, HEAD_DIM, LANE), lambda c, i: (c, 0, 0, 0))
    y, so = pl.pallas_call(
        functools.partial(_scan_kernel, tc=tc),
        grid=(nc, t // tc),
        in_specs=[kspec] * 5 + [vspec, sspec],
        out_specs=[vspec, sspec],
        out_shape=[jax.ShapeDtypeStruct((nc, t, SCAN_VH, LANE), F32),
                   jax.ShapeDtypeStruct((nc, SCAN_VH, HEAD_DIM, LANE), F32)],
        scratch_shapes=[pltpu.VMEM((SCAN_VH, HEAD_DIM, LANE), F32)],
        compiler_params=_cparams("parallel", "arbitrary"),
        name="rwkv_scan",
    )(*ks, vs, s0l)
    return _scan_unlayout_y(y, b, t), _scan_unlayout_state(so, b)


def _compress_kernel(x_ref, pe_ref, w_ref, kg_ref, gtwo_ref, o_ref):
    z = jnp.dot((x_ref[...] + pe_ref[...]).astype(BF16), w_ref[...], preferred_element_type=F32)
    o_ref[:, 0:LANE] = _group_rms(z[:, 0:LANE], gtwo_ref[...], kg_ref[...])
    o_ref[:, LANE:2 * LANE] = z[:, LANE:2 * LANE]


def _compress_weights(cmp_pe, cmp_w, k_norm_c):
    wk = cmp_w[0].reshape(CMP_BLOCK, HEAD_DIM, HEAD_DIM)
    wv = cmp_w[1].reshape(CMP_BLOCK, HEAD_DIM, HEAD_DIM)
    full = jnp.einsum("srde,st->rsdte", jnp.stack([wk, wk, wv, wv]), jnp.eye(4, dtype=F32))
    pe = jnp.stack([cmp_pe[0], cmp_pe[0], cmp_pe[1], cmp_pe[1]], axis=1)
    return (full.reshape(CMP_BLOCK * 4 * HEAD_DIM, 4 * HEAD_DIM).astype(BF16), pe.reshape(1, -1),
            jnp.tile(k_norm_c, 2).reshape(1, LANE))


def _compress(rows, cw, gtwo, tile):
    wfull, pe, kg = cw
    return _row_call(_compress_kernel, rows, [], [pe, wfull, kg, gtwo], [(2 * LANE, F32)], tile, 1, "nsa_compress")[0]


def _compress_paged_kernel(pt_ref, ident_ref, pe_ref, w_ref, kg_ref, gtwo_ref, *refs, grp, n_pages):
    del pt_ref
    page_refs, o_ref, xs = refs[:grp * n_pages], refs[-2], refs[-1]
    ident = ident_ref[...]
    for i in range(grp * n_pages):
        for half in range(2):
            xt = (page_refs[i][half * LANE:(half + 1) * LANE, :] + pe_ref[half]).astype(BF16)
            xs[half, i * PAGE_SIZE:(i + 1) * PAGE_SIZE, :] = _dot_nt(ident, xt)
    n_blk = grp * n_pages * (PAGE_SIZE // CMP_BLOCK)
    acc = [jnp.zeros((n_blk, LANE), F32) for _ in range(2)]
    for r in range(CMP_BLOCK):
        for half in range(2):
            rows = xs[half, pl.ds(r, n_blk, stride=CMP_BLOCK), :]
            acc[half] = acc[half] + jnp.dot(rows.astype(BF16), w_ref[half, r], preferred_element_type=F32)
    kc = _group_rms(acc[0], gtwo_ref[...], kg_ref[...])
    per_seq = n_blk // grp
    for g in range(grp):
        o_ref[g, 0:per_seq, 0:LANE] = kc[g * per_seq:(g + 1) * per_seq].astype(BF16)
        o_ref[g, 0:per_seq, LANE:2 * LANE] = acc[1][g * per_seq:(g + 1) * per_seq].astype(BF16)
        o_ref[g, per_seq:LANE, :] = jnp.zeros((LANE - per_seq, 2 * LANE), BF16)


def _compress_paged(cmp_t, page_table, cmp_pe, cmp_w, k_norm_c, gtwo, grp):
    b, n_pages = page_table.shape
    wk = cmp_w[0].reshape(CMP_BLOCK, HEAD_DIM, HEAD_DIM)
    wv = cmp_w[1].reshape(CMP_BLOCK, HEAD_DIM, HEAD_DIM)
    wbd = jnp.einsum("krde,ht->krhdte", jnp.stack([wk, wv]), jnp.eye(N_KV_HEADS, dtype=F32))
    wbd = wbd.reshape(2, CMP_BLOCK, LANE, LANE).astype(BF16)
    pe = jnp.tile(cmp_pe.transpose(0, 2, 1), (1, N_KV_HEADS, PAGE_SIZE // CMP_BLOCK))
    ident = jnp.eye(LANE, dtype=BF16)
    kg = jnp.tile(k_norm_c, 2).reshape(1, LANE)
    cspec = lambda a: pl.BlockSpec(a.shape, lambda i, pt: (0,) * a.ndim)
    kern = functools.partial(_compress_paged_kernel, grp=grp, n_pages=n_pages)
    return pl.pallas_call(
        kern,
        grid_spec=pltpu.PrefetchScalarGridSpec(
            num_scalar_prefetch=1, grid=(b // grp,),
            in_specs=[cspec(ident), cspec(pe), cspec(wbd), cspec(kg), cspec(gtwo)]
            + _page_specs(cmp_t, n_pages, grp),
            out_specs=pl.BlockSpec((grp, LANE, 2 * LANE), lambda i, pt: (i, 0, 0)),
            scratch_shapes=[pltpu.VMEM((2, grp * n_pages * PAGE_SIZE, LANE), F32)]),
        out_shape=jax.ShapeDtypeStruct((b, LANE, 2 * LANE), BF16),
        compiler_params=_cparams("parallel"),
        name="nsa_compress_paged",
    )(page_table, ident, pe, wbd, kg, gtwo, *([cmp_t] * (grp * n_pages)))


def _nsa_kernel(q_ref, gates_ref, kcv_ref, kvs_ref, kvw_ref, bias_ref, biasc_ref, pair_ref, o_ref,
                *, grp, qb, q_base, n_cmp, n_sel, w_off, single):
    q0 = q_base if single else q_base + pl.program_id(1) * qb
    nblk = (q0 + qb - 1) // LANE + 1
    dq = q0 // LANE
    seqs = range(grp)
    heads = range(N_KV_HEADS)
    row = lax.broadcasted_iota(I32, (qb, LANE), 0)
    lane = lax.broadcasted_iota(I32, (qb, LANE), 1)
    t_pos = q0 + row
    qs = [[_stack_heads(q_ref.at[g], hk) for hk in heads] for g in seqs]

    def gate_col(g, br, hk):
        gates = gates_ref[g]
        return jnp.concatenate([gates[:, br * N_ATT_HEADS + hk * GQA + a:br * N_ATT_HEADS + hk * GQA + a + 1]
                                for a in range(GQA)], axis=0)

    mask_c4 = _tile4(((lane * CMP_BLOCK + CMP_BLOCK - 1) <= t_pos) & (lane < n_cmp))
    cur = t_pos // SEL_BLOCK
    forced = (lane == 0) | (lane == cur) | (lane == cur - 1)
    sel_causal = lane * SEL_BLOCK <= t_pos
    lane_f = lane.astype(F32)
    o_cmp = [[None] * N_KV_HEADS for _ in seqs]
    selm = [[None] * N_KV_HEADS for _ in seqs]
    for g in seqs:
        kc = kcv_ref[g, :, 0:LANE]
        vc = kcv_ref[g, :, LANE:2 * LANE]
        for hk in heads:
            s = _dot_nt(qs[g][hk], kc) * ATT_SCALE + biasc_ref[hk * GQA:(hk + 1) * GQA].reshape(GQA * qb, LANE)
            s = jnp.where(mask_c4, s, NEG)
            e = jnp.exp(s - jnp.max(s, axis=1, keepdims=True))
            p = jnp.where(mask_c4, e / jnp.sum(e, axis=1, keepdims=True), 0.0)
            o_cmp[g][hk] = jnp.dot(p.astype(BF16), vc, preferred_element_type=F32)
            ps = p[0:qb]
            for a in range(1, GQA):
                ps = ps + p[a * qb:(a + 1) * qb]
            score = _dot_split(ps, pair_ref[...])
            score = jnp.where(sel_causal, jnp.where(forced, FORCE, score), NEG)
            picked = jnp.zeros((qb, LANE), F32)
            for _ in range(n_sel):
                mx = jnp.max(score, axis=1, keepdims=True)
                first = jnp.min(jnp.where(score == mx, lane_f, float(LANE)), axis=1, keepdims=True)
                hit = lane_f == first
                picked = jnp.where(hit, 1.0, picked)
                score = jnp.where(hit, TAKEN, score)
            selm[g][hk] = picked.astype(BF16)

    def key_blocks(ref, g, jb):
        off = pl.multiple_of(jb * LANE, LANE)
        return ref[g, pl.ds(off, LANE), 0:LANE], ref[g, pl.ds(off, LANE), LANE:2 * LANE]

    def biases_of(jb):
        dsel = jnp.minimum(dq - jb, 2)
        return [bias_ref[dsel, hk * GQA:(hk + 1) * GQA].reshape(GQA * qb, LANE) for hk in heads]

    init = tuple(tuple(_flash_init(GQA * qb) for _ in heads) for _ in seqs)

    erow = lax.broadcasted_iota(I32, (LANE, LANE), 0)
    ecol = lax.broadcasted_iota(I32, (LANE, LANE), 1)

    def slc_body(jb, carry):
        expand = jnp.where(erow == 2 * jb + ecol // SEL_BLOCK, 1.0, 0.0).astype(BF16)
        causal = jb * LANE + lane <= t_pos
        biases = biases_of(jb)
        out = []
        for g in seqs:
            kb, vb = key_blocks(kvs_ref, g, jb)
            res = []
            for hk in heads:
                tok = jnp.dot(selm[g][hk], expand, preferred_element_type=F32) > 0.5
                res.append(_flash_step(qs[g][hk], kb, vb, biases[hk], _tile4(tok & causal), *carry[g][hk]))
            out.append(tuple(res))
        return tuple(out)

    res_s = lax.fori_loop(0, nblk, slc_body, init)

    def win_body(jb, carry):
        dist = t_pos - (jb * LANE + lane)
        msk = _tile4((dist >= 0) & (dist < WINDOW))
        biases = biases_of(jb)
        out = []
        for g in seqs:
            kb, vb = key_blocks(kvw_ref, g, jb - w_off)
            out.append(tuple(_flash_step(qs[g][hk], kb, vb, biases[hk], msk, *carry[g][hk]) for hk in heads))
        return tuple(out)

    res_w = lax.fori_loop(jnp.maximum(dq - WINDOW // LANE, 0), dq + 1, win_body, init)

    for g in seqs:
        outs = []
        for hk in heads:
            o_s = res_s[g][hk][2] / res_s[g][hk][1]
            o_w = res_w[g][hk][2] / res_w[g][hk][1]
            outs.append(gate_col(g, 0, hk) * o_cmp[g][hk] + gate_col(g, 1, hk) * o_s + gate_col(g, 2, hk) * o_w)
        _write_heads(o_ref.at[g], outs, qb)


def _bias_cmp(rel_bias, q_starts, qb):
    q0 = jnp.asarray(q_starts, I32)[:, None, None]
    r = jnp.arange(qb)[None, :, None]
    n = jnp.arange(LANE)[None, None, :]
    return _bias_lookup(rel_bias, q0 + r - (n * CMP_BLOCK + CMP_BLOCK - 1)).transpose(0, 3, 1, 2)


def _nsa(q, gates, kcv, kvs_b, kvw_b, bias, bias_c, pair, *, qb, q_base, n_cmp, n_sel, w_off, grp):
    b, rows, _ = q.shape
    nq = rows // qb
    kern = functools.partial(_nsa_kernel, grp=grp, qb=qb, q_base=q_base, n_cmp=n_cmp, n_sel=n_sel, w_off=w_off,
                             single=(nq == 1))
    qspec = lambda w: pl.BlockSpec((grp, qb, w), lambda bi, i: (bi, i, 0))
    kspec = lambda a: pl.BlockSpec((grp,) + a.shape[1:], lambda bi, i: (bi, 0, 0))
    return pl.pallas_call(
        kern,
        grid=(b // grp, nq),
        in_specs=[qspec(N_ATT_HEADS * LANE), qspec(LANE), kspec(kcv), kspec(kvs_b), kspec(kvw_b), _const_spec(bias),
                  pl.BlockSpec((None,) + bias_c.shape[1:], lambda bi, i: (i, 0, 0, 0)), _const_spec(pair)],
        out_specs=qspec(N_ATT_HEADS * LANE),
        out_shape=jax.ShapeDtypeStruct((b, rows, N_ATT_HEADS * LANE), BF16),
        compiler_params=_cparams("parallel", "parallel"),
        name="nsa_attention",
    )(q, gates, kcv, kvs_b, kvw_b, bias, bias_c, pair)


def _nsa_sample_kernel(pt_ref, q_ref, gates_ref, kcv_ref, snew_ref, wnew_ref, win_ref, bias_ref, biasc_ref,
                       biasw_ref, pair_ref, *refs, grp, n_pages, t_pos, n_cmp, n_sel, w_eff):
    del pt_ref
    slc_refs, o_ref = refs[:grp * n_pages], refs[-1]
    lane1 = lax.broadcasted_iota(I32, (1, LANE), 1)
    lane8 = lax.broadcasted_iota(I32, (N_ATT_HEADS, LANE), 1)
    row_all = lax.broadcasted_iota(I32, (N_ATT_HEADS, (n_pages + 1) * LANE), 0)
    lanew = lax.broadcasted_iota(I32, (N_ATT_HEADS, w_eff), 1)
    lane1_f = lane1.astype(F32)
    bias = bias_ref[...]
    bias_now = bias[:, n_pages * LANE:n_pages * LANE + 1]
    mask_c = ((lane8 * CMP_BLOCK + CMP_BLOCK - 1) <= t_pos) & (lane8 < n_cmp)
    cur = t_pos // SEL_BLOCK
    forced = (lane1 == 0) | (lane1 == cur) | (lane1 == cur - 1)
    sel_causal = lane1 * SEL_BLOCK <= t_pos
    tail_valid = jnp.concatenate([lane1] * n_pages + [lane1 + LANE], axis=1) <= LANE
    for g in range(grp):
        q = q_ref[g]
        qf = q.astype(F32)
        s = jnp.where(mask_c, _dot_nt(q, kcv_ref[g, :, 0:LANE]) * ATT_SCALE + biasc_ref[...], NEG)
        e = jnp.exp(s - jnp.max(s, axis=1, keepdims=True))
        pc = jnp.where(mask_c, e / jnp.sum(e, axis=1, keepdims=True), 0.0)
        o_c = jnp.dot(pc.astype(BF16), kcv_ref[g, :, LANE:2 * LANE], preferred_element_type=F32)
        masks = []
        for hk in range(N_KV_HEADS):
            ps = jnp.sum(pc[hk * GQA:(hk + 1) * GQA], axis=0, keepdims=True)
            score = _dot_split(ps, pair_ref[...])
            score = jnp.where(sel_causal, jnp.where(forced, FORCE, score), NEG)
            picked = jnp.zeros((1, LANE), F32)
            for _ in range(n_sel):
                mx = jnp.max(score, axis=1, keepdims=True)
                first = jnp.min(jnp.where(score == mx, lane1_f, float(LANE)), axis=1, keepdims=True)
                hit = lane1_f == first
                picked = jnp.where(hit, 1.0, picked)
                score = jnp.where(hit, TAKEN, score)
            per_page = PAGE_SIZE // SEL_BLOCK
            tiles = []
            for p in range(n_pages + 1):
                t = jnp.zeros((1, LANE), F32)
                for a in range(per_page):
                    blk = picked[:, p * per_page + a:p * per_page + a + 1]
                    t = jnp.where(lane1 // SEL_BLOCK == a, blk, t)
                tiles.append(t)
            masks.append(jnp.concatenate(tiles, axis=1))
        valid = (jnp.where(row_all < GQA, masks[0], masks[1]) > 0.5) & tail_valid
        snew = _bf16_round(snew_ref[g])
        tiles = [_dot(q, slc_refs[g * n_pages + p][0:LANE, :]) for p in range(n_pages)]
        tiles.append(jnp.where(lane1 == 0, jnp.sum(qf * snew[:, 0:LANE], axis=1, keepdims=True), 0.0))
        s = jnp.where(valid, jnp.concatenate(tiles, axis=1) * ATT_SCALE + bias, NEG)
        e = jnp.exp(s - jnp.max(s, axis=1, keepdims=True))
        p_all = jnp.where(valid, e / jnp.sum(e, axis=1, keepdims=True), 0.0)
        o_s = p_all[:, n_pages * LANE:n_pages * LANE + 1] * snew[:, LANE:2 * LANE]
        for p in range(n_pages):
            o_s = o_s + _dot_nt(p_all[:, p * LANE:(p + 1) * LANE], slc_refs[g * n_pages + p][LANE:2 * LANE, :])
        wnew = _bf16_round(wnew_ref[g])
        valid_w = (w_eff - lanew) < WINDOW
        s_w = jnp.where(valid_w, _dot(q, win_ref[g, 0:LANE, :]) * ATT_SCALE + biasw_ref[...], NEG)
        s_n = jnp.sum(qf * wnew[:, 0:LANE], axis=1, keepdims=True) * ATT_SCALE + bias_now
        m = jnp.maximum(jnp.max(s_w, axis=1, keepdims=True), s_n)
        e_w = jnp.where(valid_w, jnp.exp(s_w - m), 0.0)
        e_n = jnp.exp(s_n - m)
        den = jnp.sum(e_w, axis=1, keepdims=True) + e_n
        o_w = _dot_nt(e_w / den, win_ref[g, LANE:2 * LANE, :]) + (e_n / den) * wnew[:, LANE:2 * LANE]
        gates = gates_ref[g]
        _head_rows_out(o_ref, g, gates[:, 0:1] * o_c + gates[:, 1:2] * o_s + gates[:, 2:3] * o_w)


def _nsa_sample(q8, gates8, kcv, snew, wnew, win_t, bias, bias_c, bias_w, pair, slc_t, page_table,
                *, t_pos, n_cmp, n_sel, grp):
    b, n_pages = page_table.shape
    w_eff = win_t.shape[2]
    gspec = lambda a: pl.BlockSpec((grp,) + a.shape[1:], lambda i, pt: (i,) + (0,) * (a.ndim - 1))
    cspec = lambda a: pl.BlockSpec(a.shape, lambda i, pt: (0,) * a.ndim)
    kern = functools.partial(_nsa_sample_kernel, grp=grp, n_pages=n_pages, t_pos=t_pos, n_cmp=n_cmp, n_sel=n_sel,
                             w_eff=w_eff)
    return pl.pallas_call(
        kern,
        grid_spec=pltpu.PrefetchScalarGridSpec(
            num_scalar_prefetch=1, grid=(b // grp,),
            in_specs=[gspec(q8), gspec(gates8), gspec(kcv), gspec(snew), gspec(wnew), gspec(win_t), cspec(bias),
                      cspec(bias_c), cspec(bias_w), cspec(pair)] + _page_specs(slc_t, n_pages, grp),
            out_specs=pl.BlockSpec((grp, N_ATT_HEADS, LANE), lambda i, pt: (i, 0, 0))),
        out_shape=jax.ShapeDtypeStruct((b, N_ATT_HEADS, LANE), BF16),
        compiler_params=_cparams("parallel"),
        name="nsa_sample",
    )(page_table, q8, gates8, kcv, snew, wnew, win_t, bias, bias_c, bias_w, pair, *([slc_t] * (grp * n_pages)))


def _post_odd_kernel(x_ref, y_ref, r_ref, k_ref, v_ref, g_ref, od_ref, gate_ref, shf_ref, scf_ref,
                     gf_ref, lnw_ref, lnb_ref, rk_ref, gtwo_ref, wo_ref, wrt_ref, br_ref, x2_ref, hf_ref, cwt_ref):
    gtwo = gtwo_ref[...]
    mix = jnp.dot(od_ref[...], wo_ref[RWKV_DIM:RWKV_DIM + N_ATT_HEADS * LANE, :], preferred_element_type=F32)
    for t in range(RWKV_DIM // LANE):
        sl = slice(t * LANE, (t + 1) * LANE)
        y = y_ref[:, sl]
        dlt = y - _dot_split(y, gtwo)
        yn = (dlt * lax.rsqrt(_dot_split(dlt * dlt, gtwo) + GN_EPS)) * lnw_ref[:, sl] + lnb_ref[:, sl]
        dot_rk = _dot_split(r_ref[:, sl] * k_ref[:, sl] * rk_ref[:, sl], gtwo) * float(HEAD_DIM)
        oc = (yn + dot_rk * v_ref[:, sl]) * g_ref[:, sl]
        mix = mix + jnp.dot(oc.astype(BF16), wo_ref[sl, :], preferred_element_type=F32)
    _post_tail(x_ref[...], mix, gate_ref[...], gf_ref[...], shf_ref[...], scf_ref[...], wrt_ref, br_ref,
               x2_ref, hf_ref, cwt_ref)


def _odd_layer(xp, xs, mp, ms, page_table, wkv0, shift0, cmp_pool, slc_pool, win_buf, rel_bias, router, experts,
               g_mix, g_ffn, w_in, w_out, cpar, r_k, ln_w, ln_b, q_norm, k_norm, cmp_pe, cmp_w):
    B, S, D = xp.shape
    Bs = xs.shape[0]
    n_pages = page_table.shape[1]
    past = n_pages * PAGE_SIZE
    wts = _odd_weights(w_in, w_out, q_norm, k_norm)
    gm = _gmats()
    gone, gtwo, _ = gm
    gsum = (gtwo.astype(F32) * HEAD_DIM).astype(BF16)
    i = jnp.arange(LANE)
    pair = jnp.where(i[:, None] // 2 == i[None, :], 1.0, 0.0).astype(BF16)
    cw = _compress_weights(cmp_pe, cmp_w, k_norm[0])
    wrt, br = router
    wg, wu, wd, lyr = experts
    g_mix = g_mix.reshape(1, D)
    post_consts = [g_ffn.reshape(1, D), ln_w.reshape(1, -1), ln_b.reshape(1, -1), r_k.reshape(1, -1), gtwo, wts[1],
                   wrt, br]
    w_eff = win_buf.shape[1]

    tp = ROW_TILE
    zc, q_p, kvc, _, kvs_b, _, kvw_b, gates, kvc_t, kvs_t, kvw_t = _inproj_odd(xp.reshape(B * S, D), [mp[0], mp[1]],
                                                                               g_mix, wts, gm, tp, S // tp)
    r3 = lambda a: a.reshape(B, S, a.shape[-1])
    pre = _rwkv_pre(zc, _shift_rows(r3(zc), 1).reshape(B * S, P_C), cpar, gsum, tp)
    y, wkv_p = _rwkv_scan(pre, jnp.zeros((B, RWKV_HEADS, HEAD_DIM, HEAD_DIM), F32), B, S, SCAN_TIME_CHUNK)
    n_cmp = S // CMP_BLOCK
    kcv = _compress(kvc.reshape(B * n_cmp, CMP_BLOCK * 2 * LANE), cw, gtwo, _pick_tile(B * n_cmp, 256))
    kcv = jnp.pad(kcv.reshape(B, n_cmp, 2 * LANE), ((0, 0), (0, LANE - n_cmp), (0, 0))).astype(BF16)
    n_slc = -(-S // SEL_BLOCK)
    od = _nsa_t(r3(q_p), r3(gates), kcv, r3(kvs_b), r3(kvw_b), _bias_tiles_t(rel_bias), _bias_cmp_t(rel_bias, S),
                pair.T, n_cmp=n_cmp, n_sel=min(N_SEL_BLOCKS, n_slc))
    x2, hf, cwt = _row_call(_post_odd_kernel,
                            [xp.reshape(B * S, D), y, pre[0], pre[2], pre[3], pre[6], od.reshape(B * S, -1)],
                            [mp[2], mp[3], mp[4]], post_consts, POST_OUTS, tp, S // tp, "post_odd",
                            outs_t=[(N_EXPERTS, F32)])
    tm = min(MOE_TILE, S)
    xp3 = _moe(hf, cwt.T, x2, mp[5], wg, wu, wd, lyr, tm, S // tm).reshape(B, S, D)
    kv5 = lambda a, n: a.reshape(-1, n, 2, N_KV_HEADS, HEAD_DIM)
    outs_p = (wkv_p, r3(zc)[:, S - 1], _cache_rows(kvc_t), _cache_rows(kvs_t),
              _cache_rows(kvw_t[:, :, S - min(WINDOW, S):]))

    zc_s, q_s, kvc_s, kvs_s, _, kvw_s, _, gates_s, _, _, _ = _inproj_odd(xs, [ms[0], ms[1]], g_mix, wts, gm, Bs, 1)
    pre_s = _rwkv_pre(zc_s, shift0, cpar, gsum, Bs)
    y_s, wkv_s = _rwkv_scan(pre_s, wkv0, Bs, 1, 1)
    n_pool = cmp_pool.shape[0]
    cmp_t = cmp_pool.transpose(0, 2, 3, 4, 1).reshape(n_pool, 2 * LANE, PAGE_SIZE)
    kcv_s = _compress_paged(cmp_t, page_table, cmp_pe, cmp_w, k_norm[0], gtwo, _pick_group(Bs, SAMPLE_GROUP))
    n_cmp_s = (past + 1) // CMP_BLOCK
    slc_t = slc_pool.transpose(0, 2, 3, 4, 1).reshape(n_pool, 2 * LANE, PAGE_SIZE)
    win_t = win_buf.transpose(0, 2, 3, 4, 1).reshape(Bs, 2 * LANE, w_eff)
    gates8 = jnp.pad(gates_s[:, :3 * N_ATT_HEADS].reshape(Bs, 3, N_ATT_HEADS).transpose(0, 2, 1),
                     ((0, 0), (0, 0), (0, LANE - 3)))
    n_slc_s = -(-(past + 1) // SEL_BLOCK)
    od_s = _nsa_sample(q_s.reshape(Bs, N_ATT_HEADS, LANE), gates8, kcv_s, kvs_s[:, None, :], kvw_s[:, None, :],
                       win_t, _bias_row(rel_bias, past), _bias_cmp(rel_bias, [past], 1)[0, :, 0, :],
                       _bias_lookup(rel_bias, w_eff - jnp.arange(w_eff)).T, pair, slc_t, page_table,
                       t_pos=past, n_cmp=n_cmp_s, n_sel=min(N_SEL_BLOCKS, n_slc_s),
                       grp=_pick_group(Bs, SAMPLE_GROUP)).reshape(Bs, -1)
    x2s, hfs, cwts = _row_call(_post_odd_kernel, [xs, y_s, pre_s[0], pre_s[2], pre_s[3], pre_s[6], od_s],
                               [ms[2], ms[3], ms[4]], post_consts, POST_OUTS, Bs, 1, "post_odd_s",
                               outs_t=[(N_EXPERTS, F32)])
    xs3 = _moe(hfs, cwts.T, x2s, ms[5], wg, wu, wd, lyr, Bs, 1)
    win_new = jnp.concatenate([win_buf[:, 1:], kv5(kvw_s, 1)], axis=1)
    outs_s = (wkv_s, zc_s, kv5(kvc_s, 1), kv5(kvs_s, 1), win_new)
    return xp3, xs3, outs_p, outs_s


def _mods(c_p, c_s, w_all, b, layer):
    nb = c_p.shape[0]
    m = _ada(jnp.concatenate([c_p, c_s], 0), w_all, b, layer)
    parts = jnp.split(m, 6, axis=-1)
    return [p[:nb, None, :] for p in parts], [p[None, nb:, :] for p in parts]


def _forward(x_prompt, x_sample, c_prompt, c_sample, page_table, cache_a_kv, cache_a_kidx, state_b_conv,
             state_c_wkv, state_c_shift, cache_d_cmp, cache_d_slc, cache_d_win, rel_bias, w_router, b_router,
             w_ada, b_ada, g_norm_mix, g_norm_ffn, w_expert_gate, w_expert_up, w_expert_down, e_w_in, e_w_out,
             a_q_norm, a_k_norm, b_conv_w, b_conv_b, o_w_in, o_w_out, c_mu, c_w0, c_w_up, c_a0, c_a_up,
             c_g_up, c_k_k, c_k_a, c_r_k, c_ln_w, c_ln_b, d_q_norm, d_k_norm, d_cmp_pe, d_cmp_w):
    assert w_ada.shape[0] == 2 and e_w_in.shape[0] == 1 and o_w_in.shape[0] == 1
    B, S, D = x_prompt.shape
    Bs = x_sample.shape[0]
    assert x_sample.shape[1] == 1
    xp, xs = x_prompt, x_sample.reshape(Bs, D)
    router = (w_router.T, b_router.reshape(N_EXPERTS, 1))
    n_pool = cache_a_kv.shape[1]
    experts = lambda l: (w_expert_gate, w_expert_up, w_expert_down, l)

    mp, ms = _mods(c_prompt, c_sample, w_ada, b_ada[0], 0)
    xp, xs, ep, es = _even_layer(xp, xs, mp, ms, page_table, cache_a_kv[0], cache_a_kidx[0], state_b_conv[0],
                                 rel_bias, router, experts(0), g_norm_mix[0], g_norm_ffn[0], e_w_in[0], e_w_out[0],
                                 a_q_norm[0], a_k_norm[0], b_conv_w[0], b_conv_b[0])
    mp, ms = _mods(c_prompt, c_sample, w_ada, b_ada[1], 1)
    cpar = (c_mu[0], c_w0[0], c_w_up[0], c_a0[0], c_a_up[0], c_g_up[0], c_k_k[0], c_k_a[0])
    xp, xs, op, os_ = _odd_layer(xp, xs, mp, ms, page_table, state_c_wkv[0], state_c_shift[0], cache_d_cmp[0],
                                 cache_d_slc[0], cache_d_win[0], rel_bias, router, experts(1), g_norm_mix[1],
                                 g_norm_ffn[1], o_w_in[0], o_w_out[0], cpar, c_r_k[0].reshape(-1), c_ln_w[0],
                                 c_ln_b[0], d_q_norm[0], d_k_norm[0], d_cmp_pe[0], d_cmp_w[0])
    stack = lambda ts: tuple(a[None] for a in ts)
    return (xp, xs.reshape(Bs, 1, D)) + stack(ep) + stack(op) + stack(es) + stack(os_)


def kernel(x_prompt, x_sample, c_prompt, c_sample, page_table, cache_a_kv, cache_a_kidx, state_b_conv, state_c_wkv, state_c_shift, cache_d_cmp, cache_d_slc, cache_d_win, rel_bias, w_router, b_router, w_ada, b_ada, g_norm_mix, g_norm_ffn, w_expert_gate, w_expert_up, w_expert_down, e_w_in, e_w_out, a_q_norm, a_k_norm, b_conv_w, b_conv_b, o_w_in, o_w_out, c_mu, c_w0, c_w_up, c_a0, c_a_up, c_g_up, c_k_k, c_k_a, c_r_k, c_ln_w, c_ln_b, d_q_norm, d_k_norm, d_cmp_pe, d_cmp_w):
    return _forward(x_prompt, x_sample, c_prompt, c_sample, page_table, cache_a_kv, cache_a_kidx, state_b_conv,
                    state_c_wkv, state_c_shift, cache_d_cmp, cache_d_slc, cache_d_win, rel_bias, w_router, b_router,
                    w_ada, b_ada, g_norm_mix, g_norm_ffn, w_expert_gate, w_expert_up, w_expert_down, e_w_in, e_w_out,
                    a_q_norm, a_k_norm, b_conv_w, b_conv_b, o_w_in, o_w_out, c_mu, c_w0, c_w_up, c_a0, c_a_up,
                    c_g_up, c_k_k, c_k_a, c_r_k, c_ln_w, c_ln_b, d_q_norm, d_k_norm, d_cmp_pe, d_cmp_w)
```

```python
import functools
import math

import jax
import jax.numpy as jnp
from jax import lax
from jax.experimental import pallas as pl
from jax.experimental.pallas import tpu as pltpu

F32 = jnp.float32
BF16 = jnp.bfloat16
I32 = jnp.int32

LANE = 128
HEAD_DIM = 64
N_ATT_HEADS = 8
N_KV_HEADS = 2
GQA = N_ATT_HEADS // N_KV_HEADS
IDX_HEADS = 4
IDX_DIM = 64
TOPK_MAX = 256
CONV_CH = 512
RWKV_HEADS = 8
RWKV_DIM = RWKV_HEADS * HEAD_DIM
LORA_W = 64
LORA_A = 64
LORA_G = 128
GN_EPS = 64e-5
CMP_BLOCK = 32
SEL_BLOCK = 64
N_SEL_BLOCKS = 8
WINDOW = 512
N_BUCKETS = 32
MAX_DISTANCE = 128
N_EXPERTS = 16
N_GROUPS = 4
EXPERTS_PER_GROUP = N_EXPERTS // N_GROUPS
PAGE_SIZE = 128
RMS_EPS = 1e-6
NEG = -1e30
FORCE = 1e9
TAKEN = -3e38
ATT_SCALE = HEAD_DIM ** -0.5
VMEM_LIMIT = 56 * 1024 * 1024
ROW_TILE = 256
MOE_TILE = 1024
SCAN_TIME_CHUNK = 32
SAMPLE_GROUP = 4


def _cparams(*sem):
    return pltpu.CompilerParams(dimension_semantics=sem, vmem_limit_bytes=VMEM_LIMIT)


def _pick_tile(rows, pref):
    t = min(pref, rows)
    while rows % t or (t % 8 and t != rows):
        t -= 1
    return t


def _pick_group(n, pref):
    g = min(pref, n)
    while n % g:
        g -= 1
    return g


def _const_spec(a):
    nd = a.ndim
    return pl.BlockSpec(a.shape, lambda *_: (0,) * nd)


def _dot(a, b):
    return jnp.dot(a.astype(BF16), b.astype(BF16), preferred_element_type=F32)


def _dot_nt(a, b):
    return lax.dot_general(a.astype(BF16), b.astype(BF16), (((1,), (1,)), ((), ())),
                           preferred_element_type=F32)


def _dot_split(x, m):
    hi = x.astype(BF16)
    r1 = x - hi.astype(F32)
    mid = r1.astype(BF16)
    lo = (r1 - mid.astype(F32)).astype(BF16)
    return (jnp.dot(hi, m, preferred_element_type=F32) + jnp.dot(mid, m, preferred_element_type=F32)
            + jnp.dot(lo, m, preferred_element_type=F32))


def _bf16_round(x):
    return x.astype(BF16).astype(F32)


def _sigmoid(x):
    return 1.0 / (1.0 + jnp.exp(-x))


def _silu(x):
    return x * _sigmoid(x)


def _modulate(x, g, shift, scale):
    y = x * lax.rsqrt(jnp.mean(x * x, axis=-1, keepdims=True) + RMS_EPS)
    return (y * g) * (1.0 + scale) + shift


def _group_rms(t, gmat, gain):
    ms = _dot_split(t * t, gmat)
    return (t * lax.rsqrt(ms + RMS_EPS)) * gain


def _ada_kernel(c_ref, w_ref, b_ref, o_ref):
    o_ref[...] = _dot(_silu(c_ref[...]), w_ref[...]) + b_ref[...]


def _ada(c, w_all, b, layer):
    r, d = c.shape
    n = w_all.shape[2]
    tn = 512
    return pl.pallas_call(
        _ada_kernel,
        grid=(n // tn,),
        in_specs=[pl.BlockSpec((r, d), lambda j: (0, 0)),
                  pl.BlockSpec((None, d, tn), lambda j: (layer, 0, j)),
                  pl.BlockSpec((1, tn), lambda j: (0, j))],
        out_specs=pl.BlockSpec((r, tn), lambda j: (0, j)),
        out_shape=jax.ShapeDtypeStruct((r, n), F32),
        compiler_params=_cparams("parallel"),
        name="ada_mod",
    )(c, w_all, b.reshape(1, n))


E_Q0, E_KV0, E_QI0, E_MISC0, E_BG0, E_CG0, E_XIN0, E_END = 0, 1024, 1280, 1792, 1920, 2432, 2944, 3456


def _inproj_even_kernel(x_ref, shift_ref, scale_ref, g_ref, w_ref, qg_ref, kg_ref, gone_ref, gtwo_ref,
                        q_ref, kv_ref, kvb_ref, qi_ref, misc_ref, miscb_ref, bg_ref, u_ref, kvt_ref, misct_ref):
    h = _modulate(x_ref[...], g_ref[...], shift_ref[...], scale_ref[...])
    z = jnp.dot(h.astype(BF16), w_ref[...], preferred_element_type=F32)
    gone = gone_ref[...]
    for t in range(N_ATT_HEADS):
        sl = slice(t * LANE, (t + 1) * LANE)
        q_ref[:, sl] = _group_rms(z[:, E_Q0 + t * LANE:E_Q0 + (t + 1) * LANE], gone, qg_ref[:, sl]).astype(BF16)
    k = _group_rms(z[:, E_KV0:E_KV0 + LANE], gtwo_ref[...], kg_ref[...])
    v = z[:, E_KV0 + LANE:E_KV0 + 2 * LANE]
    kv_ref[:, 0:LANE] = k
    kv_ref[:, LANE:2 * LANE] = v
    kvb_ref[:, 0:LANE] = k.astype(BF16)
    kvb_ref[:, LANE:2 * LANE] = v.astype(BF16)
    qi_ref[...] = z[:, E_QI0:E_MISC0].astype(BF16)
    misc = z[:, E_MISC0:E_BG0]
    misc_ref[...] = misc
    miscb_ref[...] = misc.astype(BF16)
    bg_ref[...] = z[:, E_BG0:E_CG0]
    u_ref[...] = z[:, E_CG0:E_XIN0] * z[:, E_XIN0:E_END]
    kvt_ref[0:LANE, :] = k.T
    kvt_ref[LANE:2 * LANE, :] = v.T
    misct_ref[...] = misc.T


def _row_call(kernel, xs, mods, consts, outs, tile, tpb, name, outs_t=(), outs_bt=()):
    if not isinstance(xs, (list, tuple)):
        xs = [xs]
    rows = xs[0].shape[0]
    n_tiles = rows // tile
    in_specs = [pl.BlockSpec((tile, x.shape[1]), lambda t: (t, 0)) for x in xs]
    for m in mods:
        in_specs.append(pl.BlockSpec((None,) + m.shape[1:], lambda t: (t // tpb, 0, 0)))
    in_specs += [_const_spec(c) for c in consts]
    out_specs = [pl.BlockSpec((tile, w), lambda t: (t, 0)) for (w, _) in outs]
    out_shape = [jax.ShapeDtypeStruct((rows, w), dt) for (w, dt) in outs]
    out_specs += [pl.BlockSpec((hh, tile), lambda t: (0, t)) for (hh, _) in outs_t]
    out_shape += [jax.ShapeDtypeStruct((hh, rows), dt) for (hh, dt) in outs_t]
    out_specs += [pl.BlockSpec((None, hh, tile), lambda t: (t // tpb, 0, t % tpb)) for (hh, _) in outs_bt]
    out_shape += [jax.ShapeDtypeStruct((n_tiles // tpb, hh, tile * tpb), dt) for (hh, dt) in outs_bt]
    return pl.pallas_call(kernel, grid=(n_tiles,), in_specs=in_specs, out_specs=out_specs, out_shape=out_shape,
                          compiler_params=_cparams("parallel"), name=name)(*xs, *mods, *consts)


def _pad_q_cols(wq):
    d = wq.shape[0]
    w = wq.reshape(d, N_ATT_HEADS, HEAD_DIM)
    z = jnp.zeros_like(w)
    lo = jnp.concatenate([w, z], -1)
    hi = jnp.concatenate([z, w], -1)
    sel = (jnp.arange(N_ATT_HEADS) >= GQA)[None, :, None]
    return jnp.where(sel, hi, lo).reshape(d, N_ATT_HEADS * LANE)


def _pad_o_rows(wo):
    d = wo.shape[1]
    w = wo.reshape(N_ATT_HEADS, HEAD_DIM, d)
    z = jnp.zeros_like(w)
    lo = jnp.concatenate([w, z], 1)
    hi = jnp.concatenate([z, w], 1)
    sel = (jnp.arange(N_ATT_HEADS) >= GQA)[:, None, None]
    return jnp.where(sel, hi, lo).reshape(N_ATT_HEADS * LANE, d)


def _gmats():
    i = jnp.arange(LANE)
    gone = jnp.full((LANE, LANE), 1.0 / HEAD_DIM, F32).astype(BF16)
    gtwo = jnp.where((i[:, None] // HEAD_DIM) == (i[None, :] // HEAD_DIM), 1.0 / HEAD_DIM, 0.0).astype(BF16)
    tri = jnp.where(i[:, None] <= i[None, :], 1.0, 0.0).astype(BF16)
    return gone, gtwo, tri


def _even_weights(w_in, w_out, q_norm, k_norm):
    d = w_in.shape[0]
    a_q, a_kv = N_ATT_HEADS * HEAD_DIM, 2 * N_KV_HEADS * HEAD_DIM
    o = 0
    wq = w_in[:, o:o + a_q]; o += a_q
    wkv = w_in[:, o:o + a_kv]; o += a_kv
    wqi = w_in[:, o:o + IDX_HEADS * IDX_DIM]; o += IDX_HEADS * IDX_DIM
    wki = w_in[:, o:o + IDX_DIM]; o += IDX_DIM
    wwi = w_in[:, o:o + IDX_HEADS]; o += IDX_HEADS
    wrest = w_in[:, o:]
    wqi = jnp.concatenate([wqi.reshape(d, IDX_HEADS, IDX_DIM), jnp.zeros((d, IDX_HEADS, LANE - IDX_DIM), F32)],
                          -1).reshape(d, IDX_HEADS * LANE)
    wmisc = jnp.concatenate([wki, wwi, jnp.zeros((d, LANE - IDX_DIM - IDX_HEADS), F32)], -1)
    w_in_p = jnp.concatenate([_pad_q_cols(wq), wkv, wqi, wmisc, wrest], -1).astype(BF16)
    w_out_p = jnp.concatenate([_pad_o_rows(w_out[:a_q]), w_out[a_q:]], 0).astype(BF16)
    qg = jnp.tile(q_norm, 2 * N_ATT_HEADS).reshape(1, N_ATT_HEADS * LANE)
    kg = jnp.tile(k_norm, 2).reshape(1, LANE)
    return w_in_p, w_out_p, qg, kg


def _inproj_even(x, mods, g, wts, gm, tile, tpb):
    w_in_p, _, qg, kg = wts
    gone, gtwo, _ = gm
    outs = [(N_ATT_HEADS * LANE, BF16), (2 * LANE, F32), (2 * LANE, BF16), (IDX_HEADS * LANE, BF16),
            (LANE, F32), (LANE, BF16), (CONV_CH, F32), (CONV_CH, F32)]
    return _row_call(_inproj_even_kernel, x, mods, [g, w_in_p, qg, kg, gone, gtwo], outs, tile, tpb, "inproj_even",
                     outs_bt=[(2 * LANE, F32), (LANE, F32)])


def _t5_bucket(dist):
    dist = jnp.maximum(dist, 0)
    exact = N_BUCKETS // 2
    far = exact + (jnp.log(jnp.maximum(dist, 1).astype(F32) / exact)
                   / math.log(MAX_DISTANCE / exact) * (N_BUCKETS - exact)).astype(I32)
    return jnp.where(dist < exact, dist, jnp.minimum(far, N_BUCKETS - 1))


def _bias_lookup(rel_bias, dist):
    onehot = (_t5_bucket(dist)[..., None] == jnp.arange(N_BUCKETS)).astype(F32)
    return jnp.einsum("...k,kh->...h", onehot, rel_bias, precision=lax.Precision.HIGHEST)


def _bias_row(rel_bias, t_pos):
    return _bias_lookup(rel_bias, t_pos - jnp.arange(t_pos + LANE)).T


def _bias_tiles(rel_bias, qb):
    r = jnp.arange(qb)[:, None]
    c = jnp.arange(LANE)[None, :]
    tiles = [_bias_lookup(rel_bias, d * LANE + r - c) for d in range(3)]
    return jnp.stack(tiles).transpose(0, 3, 1, 2)


def _stack_heads(q_ref, hk):
    return jnp.concatenate([q_ref[:, (hk * GQA + g) * LANE:(hk * GQA + g + 1) * LANE] for g in range(GQA)], axis=0)


QB = LANE
QW = GQA * QB


def _sub_sum(x):
    return jnp.sum(x, axis=0, keepdims=True)


def _flash_t_pair(blocks, qs, bias_ref, carry, acc_ref):
    logits = [[jnp.where(mk[hk], _dot_nt(kb, qs[hk]) + bias_ref[dsel, hk], NEG) for (kb, _, mk, dsel) in blocks]
              for hk in range(N_KV_HEADS)]
    out = []
    for hk in range(N_KV_HEADS):
        m, l = carry[hk]
        m_new = m
        for s in logits[hk]:
            m_new = jnp.maximum(m_new, jnp.max(s, axis=0, keepdims=True))
        alpha = jnp.exp(m - m_new)
        l = alpha * l
        pv = None
        for s, (_, vt, _, _) in zip(logits[hk], blocks):
            p = jnp.exp(s - m_new)
            l = l + _sub_sum(p)
            d = jnp.dot(vt, p.astype(BF16), preferred_element_type=F32)
            pv = d if pv is None else pv + d
        acc_ref[hk] = alpha * acc_ref[hk] + pv
        out.append((m_new, l))
    return tuple(out)


def _scaled_queries(q_ref, hk):
    return (_stack_heads(q_ref, hk).astype(F32) * ATT_SCALE).astype(BF16)


def _pair_loop(nblk, body, init):
    def body2(jj, c):
        return body(2 * jj + 1, body(2 * jj, c))
    return lax.fori_loop(0, (nblk + 1) // 2, body2, init)


def _flash_t_init():
    return (jnp.full((1, QW), NEG, F32), jnp.zeros((1, QW), F32))


def _tile_lanes(x):
    return jnp.concatenate([x] * GQA, axis=1)


def _write_heads_t(o_ref, o_ts):
    lane = lax.broadcasted_iota(I32, (QB, LANE), 1)
    for hk in range(N_KV_HEADS):
        valid = (lane // HEAD_DIM) == hk
        for g in range(GQA):
            h = hk * GQA + g
            o = o_ts[hk][:, g * QB:(g + 1) * QB].T
            o_ref[:, h * LANE:(h + 1) * LANE] = jnp.where(valid, o, 0.0).astype(BF16)


def _dsa_t_kernel(q_ref, qi_ref, misc_ref, kidx_ref, k_ref, vt_ref, bias_ref, trit_ref, o_ref, key_s, acc_s,
                  *, n_keep):
    i = pl.program_id(1)
    q0 = i * QB
    nblk = i + 1
    krow = lax.broadcasted_iota(I32, (LANE, QB), 0)
    qcol = lax.broadcasted_iota(I32, (LANE, QB), 1)
    misc_t = misc_ref[...].T
    wis = [_bf16_round(misc_t[IDX_DIM + h:IDX_DIM + h + 1, :]) for h in range(IDX_HEADS)]
    qi = qi_ref[...]
    idx_scale = (IDX_HEADS * IDX_DIM) ** -0.5

    def causal(j):
        return j * LANE + krow <= q0 + qcol

    def pass_a(j, c):
        kb = kidx_ref[pl.ds(pl.multiple_of(j * LANE, LANE), LANE), :]
        acc = jnp.zeros((LANE, QB), F32)
        for h in range(IDX_HEADS):
            acc = acc + _bf16_round(jnp.maximum(_dot_nt(kb, qi[:, h * LANE:(h + 1) * LANE]), 0.0)) * wis[h]
        key_s[j] = _order_keys(jnp.where(causal(j), acc * idx_scale, NEG))
        return c

    _pair_loop(nblk, pass_a, 0)

    def count(pred):
        def body(j, a):
            return a + jnp.where(pred(key_s[j]), 1.0, 0.0)
        return _sub_sum(_pair_loop(nblk, body, jnp.zeros((LANE, QB), F32)))

    keep = float(n_keep)
    thr = jnp.where(count(lambda k: k >= 0) >= keep, jnp.int32(0), jnp.int32(-2 ** 31))

    def search(it, thr):
        cand = thr | lax.shift_left(jnp.int32(1), jnp.int32(30) - it)
        return jnp.where(count(lambda k: k >= cand) >= keep, cand, thr)

    thr = lax.fori_loop(0, 31, search, thr)
    need = keep - count(lambda k: k > thr)
    trit = trit_ref[...]

    def pass_c(j, run):
        key = key_s[j]
        eq = key == thr
        eqf = jnp.where(eq, 1.0, 0.0)
        cum = jnp.dot(trit, eqf.astype(BF16), preferred_element_type=F32) + run
        sel = ((key > thr) | (eq & (cum <= need))) & causal(j)
        key_s[j] = jnp.where(sel, 1, 0)
        return run + _sub_sum(eqf)

    _pair_loop(nblk, pass_c, jnp.zeros((1, QB), F32))

    qs = [_scaled_queries(q_ref, hk) for hk in range(N_KV_HEADS)]
    acc_s[...] = jnp.zeros_like(acc_s)

    def pass_d(jj, carry):
        blocks = []
        for j in (2 * jj, 2 * jj + 1):
            kb = k_ref[pl.ds(pl.multiple_of(j * LANE, LANE), LANE), :]
            blocks.append((kb, vt_ref[j], [_tile_lanes(key_s[j] > 0)] * N_KV_HEADS, jnp.clip(i - j, 0, 2)))
        return _flash_t_pair(blocks, qs, bias_ref, carry, acc_s)

    res = lax.fori_loop(0, (nblk + 1) // 2, pass_d, tuple(_flash_t_init() for _ in range(N_KV_HEADS)))
    _write_heads_t(o_ref, [acc_s[hk] / res[hk][1] for hk in range(N_KV_HEADS)])


def _bias_tiles_t(rel_bias):
    t = _bias_tiles(rel_bias, QB)
    t = t.reshape(3, N_KV_HEADS, GQA, QB, LANE).transpose(0, 1, 4, 2, 3)
    return t.reshape(3, N_KV_HEADS, LANE, QW)


def _blocks_t(x):
    b, s, w = x.shape
    return x.reshape(b, s // LANE, LANE, w).transpose(0, 1, 3, 2)


def _dsa_t(q, qi, misc, kidx_b, kv_b, bias_t, trit, *, n_keep):
    b, s, _ = q.shape
    vt = _blocks_t(kv_b[:, :, LANE:])
    qspec = lambda w: pl.BlockSpec((None, QB, w), lambda bi, i: (bi, i, 0))
    kspec = pl.BlockSpec((None, s, LANE), lambda bi, i: (bi, 0, 0))
    return pl.pallas_call(
        functools.partial(_dsa_t_kernel, n_keep=n_keep),
        grid=(b, s // QB),
        in_specs=[qspec(N_ATT_HEADS * LANE), qspec(IDX_HEADS * LANE), qspec(LANE), kspec, kspec,
                  pl.BlockSpec((None, s // LANE, LANE, LANE), lambda bi, i: (bi, 0, 0, 0)),
                  _const_spec(bias_t), _const_spec(trit)],
        out_specs=qspec(N_ATT_HEADS * LANE),
        out_shape=jax.ShapeDtypeStruct((b, s, N_ATT_HEADS * LANE), BF16),
        scratch_shapes=[pltpu.VMEM((s // LANE, LANE, QB), I32), pltpu.VMEM((N_KV_HEADS, LANE, QW), F32)],
        compiler_params=_cparams("parallel", "parallel"),
        name="dsa_attention_t",
    )(q, qi, misc, kidx_b, kv_b, vt, bias_t, trit)


def _dot_split_rhs(m, x):
    hi = x.astype(BF16)
    r1 = x - hi.astype(F32)
    mid = r1.astype(BF16)
    lo = (r1 - mid.astype(F32)).astype(BF16)
    return (jnp.dot(m, hi, preferred_element_type=F32) + jnp.dot(m, mid, preferred_element_type=F32)
            + jnp.dot(m, lo, preferred_element_type=F32))


def _nsa_t_kernel(q_ref, gates_ref, kc_ref, vct_ref, ks_ref, vst_ref, kw_ref, vwt_ref, bias_ref, biasc_ref,
                  pairt_ref, o_ref, acc_s, *, n_cmp, n_sel):
    i = pl.program_id(1)
    q0 = i * QB
    nblk = i + 1
    heads = range(N_KV_HEADS)
    krow = lax.broadcasted_iota(I32, (LANE, QB), 0)
    qcol = lax.broadcasted_iota(I32, (LANE, QB), 1)
    t_pos = q0 + qcol
    krow_f = krow.astype(F32)
    qs = [_scaled_queries(q_ref, hk) for hk in heads]
    gates_t = gates_ref[...].T

    def gate_row(br, hk):
        return jnp.concatenate([gates_t[br * N_ATT_HEADS + hk * GQA + g:br * N_ATT_HEADS + hk * GQA + g + 1, :]
                                for g in range(GQA)], axis=1)

    mask_c = _tile_lanes(((krow * CMP_BLOCK + CMP_BLOCK - 1) <= t_pos) & (krow < n_cmp))
    cur = t_pos // SEL_BLOCK
    forced = (krow == 0) | (krow == cur) | (krow == cur - 1)
    sel_causal = krow * SEL_BLOCK <= t_pos
    o_cmp, picked = [], []
    for hk in heads:
        s = jnp.where(mask_c, _dot_nt(kc_ref[...], qs[hk]) + biasc_ref[hk], NEG)
        e = jnp.exp(s - jnp.max(s, axis=0, keepdims=True))
        p = jnp.where(mask_c, e / _sub_sum(e), 0.0)
        o_cmp.append(jnp.dot(vct_ref[...], p.astype(BF16), preferred_element_type=F32))
        ps = p[:, 0:QB]
        for g in range(1, GQA):
            ps = ps + p[:, g * QB:(g + 1) * QB]
        score = _dot_split_rhs(pairt_ref[...], ps)
        score = jnp.where(sel_causal, jnp.where(forced, FORCE, score), NEG)
        pk = jnp.zeros((LANE, QB), F32)
        for _ in range(n_sel):
            mx = jnp.max(score, axis=0, keepdims=True)
            first = jnp.min(jnp.where(score == mx, krow_f, float(LANE)), axis=0, keepdims=True)
            hit = krow_f == first
            pk = jnp.where(hit, 1.0, pk)
            score = jnp.where(hit, TAKEN, score)
        picked.append(pk.astype(BF16))

    def key_block(k_ref, vt_ref, j):
        return k_ref[pl.ds(pl.multiple_of(j * LANE, LANE), LANE), :], vt_ref[j]

    erow = lax.broadcasted_iota(I32, (LANE, LANE), 0)
    ecol = lax.broadcasted_iota(I32, (LANE, LANE), 1)
    acc_s[...] = jnp.zeros_like(acc_s)

    def slc_body(jj, carry):
        blocks = []
        for j in (2 * jj, 2 * jj + 1):
            kb, vt = key_block(ks_ref, vst_ref, j)
            expand = jnp.where(ecol == 2 * j + erow // SEL_BLOCK, 1.0, 0.0).astype(BF16)
            causal = j * LANE + krow <= t_pos
            masks = [_tile_lanes((jnp.dot(expand, picked[hk], preferred_element_type=F32) > 0.5) & causal)
                     for hk in heads]
            blocks.append((kb, vt, masks, jnp.clip(i - j, 0, 2)))
        return _flash_t_pair(blocks, qs, bias_ref, carry, acc_s.at[0])

    res_s = lax.fori_loop(0, (nblk + 1) // 2, slc_body, tuple(_flash_t_init() for _ in heads))

    lo = jnp.maximum(i - WINDOW // LANE - 1, 0)

    def win_body(jj, carry):
        blocks = []
        for j in (lo + 2 * jj, lo + 2 * jj + 1):
            kb, vt = key_block(kw_ref, vwt_ref, j)
            dist = t_pos - (j * LANE + krow)
            mask = _tile_lanes((dist >= 0) & (dist < WINDOW))
            blocks.append((kb, vt, [mask] * N_KV_HEADS, jnp.clip(i - j, 0, 2)))
        return _flash_t_pair(blocks, qs, bias_ref, carry, acc_s.at[1])

    res_w = lax.fori_loop(0, (i - lo + 2) // 2, win_body, tuple(_flash_t_init() for _ in heads))

    _write_heads_t(o_ref, [gate_row(0, hk) * o_cmp[hk] + gate_row(1, hk) * (acc_s[0, hk] / res_s[hk][1])
                           + gate_row(2, hk) * (acc_s[1, hk] / res_w[hk][1]) for hk in heads])


def _bias_cmp_t(rel_bias, s):
    t = _bias_cmp(rel_bias, [j * QB for j in range(s // QB)], QB)
    t = t.reshape(s // QB, N_KV_HEADS, GQA, QB, LANE).transpose(0, 1, 4, 2, 3)
    return t.reshape(s // QB, N_KV_HEADS, LANE, QW)


def _nsa_t(q, gates, kcv, kvs_b, kvw_b, bias_t, bias_c, pair_t, *, n_cmp, n_sel):
    b, s, _ = q.shape
    vct = kcv[:, :, LANE:].transpose(0, 2, 1)
    qspec = lambda w: pl.BlockSpec((None, QB, w), lambda bi, i: (bi, i, 0))
    kspec = pl.BlockSpec((None, s, LANE), lambda bi, i: (bi, 0, 0))
    vspec = pl.BlockSpec((None, s // LANE, LANE, LANE), lambda bi, i: (bi, 0, 0, 0))
    cspec = pl.BlockSpec((None, LANE, LANE), lambda bi, i: (bi, 0, 0))
    return pl.pallas_call(
        functools.partial(_nsa_t_kernel, n_cmp=n_cmp, n_sel=n_sel),
        grid=(b, s // QB),
        in_specs=[qspec(N_ATT_HEADS * LANE), qspec(LANE), cspec, cspec, kspec, vspec, kspec, vspec,
                  _const_spec(bias_t), pl.BlockSpec((None,) + bias_c.shape[1:], lambda bi, i: (i, 0, 0, 0)),
                  _const_spec(pair_t)],
        out_specs=qspec(N_ATT_HEADS * LANE),
        out_shape=jax.ShapeDtypeStruct((b, s, N_ATT_HEADS * LANE), BF16),
        scratch_shapes=[pltpu.VMEM((2, N_KV_HEADS, LANE, QW), F32)],
        compiler_params=_cparams("parallel", "parallel"),
        name="nsa_attention_t",
    )(q, gates, kcv, vct, kvs_b, _blocks_t(kvs_b[:, :, LANE:]), kvw_b, _blocks_t(kvw_b[:, :, LANE:]),
      bias_t, bias_c, pair_t)


def _select_top(keys, n_keep, tri):
    keep = float(n_keep)
    n = keys[0].shape[1]

    def count(pred):
        return [jnp.sum(jnp.where(pred(g, k), 1.0, 0.0), axis=1, keepdims=True) for g, k in enumerate(keys)]

    int_min = jnp.int32(-2 ** 31)
    thr = tuple(jnp.where(c >= keep, jnp.int32(0), int_min) for c in count(lambda g, k: k >= 0))

    def search(it, thr):
        bit = lax.shift_left(jnp.int32(1), jnp.int32(30) - it)
        cand = [t | bit for t in thr]
        cnt = count(lambda g, k: k >= cand[g])
        return tuple(jnp.where(c >= keep, cd, t) for c, cd, t in zip(cnt, cand, thr))

    thr = lax.fori_loop(0, 31, search, thr)
    need = [keep - c for c in count(lambda g, k: k > thr[g])]
    sels = []
    for g, k in enumerate(keys):
        run = jnp.zeros((1, 1), F32)
        parts = []
        for t in range(n // LANE):
            kt = k[:, t * LANE:(t + 1) * LANE]
            eq = kt == thr[g]
            eqf = jnp.where(eq, 1.0, 0.0)
            cum = jnp.dot(eqf.astype(BF16), tri, preferred_element_type=F32) + run
            parts.append((kt > thr[g]) | (eq & (cum <= need[g])))
            run = run + jnp.sum(eqf, axis=1, keepdims=True)
        sels.append(jnp.concatenate(parts, axis=1))
    return sels


def _order_keys(score):
    score = jnp.where(score == 0.0, 0.0, score)
    bits = lax.bitcast_convert_type(score, I32)
    return jnp.where(bits < 0, bits ^ jnp.int32(0x7FFFFFFF), bits)


def _head_rows_out(o_ref, g, acc):
    rowh = lax.broadcasted_iota(I32, (N_ATT_HEADS, LANE), 0)
    laneh = lax.broadcasted_iota(I32, (N_ATT_HEADS, LANE), 1)
    o_ref[g] = jnp.where((laneh // HEAD_DIM) == (rowh // GQA), acc, 0.0).astype(BF16)


def _dsa_sample_kernel(pt_ref, q_ref, qi_ref, wi_ref, knew_ref, kvnew_ref, bias_ref, tri_ref, *refs,
                       grp, n_pages, n_keep):
    del pt_ref
    ki_refs, kv_refs, o_ref = refs[:grp * n_pages], refs[grp * n_pages:2 * grp * n_pages], refs[-1]
    lane1 = lax.broadcasted_iota(I32, (1, LANE), 1)
    idx_scale = (IDX_HEADS * IDX_DIM) ** -0.5
    keys = []
    for g in range(grp):
        qi = qi_ref[g]
        wi = _bf16_round(wi_ref[g])
        tiles = []
        for p in range(n_pages):
            rel = _bf16_round(jnp.maximum(_dot(qi, ki_refs[g * n_pages + p][...]), 0.0))
            tiles.append(jnp.sum(rel * wi, axis=0, keepdims=True) * idx_scale)
        rel_new = _bf16_round(jnp.maximum(jnp.sum(qi.astype(F32) * _bf16_round(knew_ref[g]), axis=1, keepdims=True),
                                          0.0))
        sc_new = jnp.sum(rel_new * wi, axis=0, keepdims=True) * idx_scale
        tiles.append(jnp.where(lane1 == 0, sc_new, NEG))
        keys.append(_order_keys(jnp.concatenate(tiles, axis=1)))
    sels = _select_top(keys, n_keep, tri_ref[...])
    bias = bias_ref[...]
    for g in range(grp):
        q = q_ref[g]
        kvnew = _bf16_round(kvnew_ref[g])
        tiles = [_dot(q, kv_refs[g * n_pages + p][0:LANE, :]) for p in range(n_pages)]
        s_new = jnp.sum(q.astype(F32) * kvnew[:, 0:LANE], axis=1, keepdims=True)
        tiles.append(jnp.where(lane1 == 0, s_new, 0.0))
        valid = sels[g] & (jnp.concatenate([lane1] * n_pages + [lane1 + LANE], axis=1) <= LANE)
        s = jnp.where(valid, jnp.concatenate(tiles, axis=1) * ATT_SCALE + bias, NEG)
        e = jnp.exp(s - jnp.max(s, axis=1, keepdims=True))
        p_all = jnp.where(valid, e / jnp.sum(e, axis=1, keepdims=True), 0.0)
        acc = p_all[:, n_pages * LANE:n_pages * LANE + 1] * kvnew[:, LANE:2 * LANE]
        for p in range(n_pages):
            acc = acc + _dot_nt(p_all[:, p * LANE:(p + 1) * LANE], kv_refs[g * n_pages + p][LANE:2 * LANE, :])
        _head_rows_out(o_ref, g, acc)


def _page_specs(pool_t, n_pages, grp):
    r, c = pool_t.shape[1:]
    return [pl.BlockSpec((None, r, c), lambda i, pt, g=g, p=p: (pt[i * grp + g, p], 0, 0))
            for g in range(grp) for p in range(n_pages)]


def _dsa_sample(q8, qi8, wi8, knew, kvnew, bias, tri, ki_t, kv_t, page_table, *, n_keep, grp):
    b, n_pages = page_table.shape
    gspec = lambda a: pl.BlockSpec((grp,) + a.shape[1:], lambda i, pt: (i,) + (0,) * (a.ndim - 1))
    cspec = lambda a: pl.BlockSpec(a.shape, lambda i, pt: (0,) * a.ndim)
    kern = functools.partial(_dsa_sample_kernel, grp=grp, n_pages=n_pages, n_keep=n_keep)
    return pl.pallas_call(
        kern,
        grid_spec=pltpu.PrefetchScalarGridSpec(
            num_scalar_prefetch=1, grid=(b // grp,),
            in_specs=[gspec(q8), gspec(qi8), gspec(wi8), gspec(knew), gspec(kvnew), cspec(bias), cspec(tri)]
            + _page_specs(ki_t, n_pages, grp) + _page_specs(kv_t, n_pages, grp),
            out_specs=pl.BlockSpec((grp, N_ATT_HEADS, LANE), lambda i, pt: (i, 0, 0))),
        out_shape=jax.ShapeDtypeStruct((b, N_ATT_HEADS, LANE), BF16),
        compiler_params=_cparams("parallel"),
        name="dsa_sample",
    )(page_table, q8, qi8, wi8, knew, kvnew, bias, tri, *([ki_t] * (grp * n_pages)), *([kv_t] * (grp * n_pages)))


def _route(hf, wrt, br):
    logits = _dot_nt(wrt, hf)
    s = _sigmoid(logits)
    sel = s + br
    rows = [sel[e:e + 1, :] for e in range(N_EXPERTS)]
    grp = []
    for g in range(N_GROUPS):
        a = rows[g * EXPERTS_PER_GROUP:(g + 1) * EXPERTS_PER_GROUP]
        best = None
        for i in range(EXPERTS_PER_GROUP):
            for j in range(i + 1, EXPERTS_PER_GROUP):
                v = a[i] + a[j]
                best = v if best is None else jnp.maximum(best, v)
        grp.append(best)
    gbest = jnp.zeros_like(grp[0], dtype=I32)
    cur = grp[0]
    for g in range(1, N_GROUPS):
        better = grp[g] > cur
        gbest = jnp.where(better, g, gbest)
        cur = jnp.where(better, grp[g], cur)
    picked = []
    for g in range(N_GROUPS):
        a = rows[g * EXPERTS_PER_GROUP:(g + 1) * EXPERTS_PER_GROUP]
        for j in range(EXPERTS_PER_GROUP):
            rank = jnp.zeros_like(a[j])
            for jj in range(EXPERTS_PER_GROUP):
                if jj != j:
                    ahead = (a[jj] > a[j]) | (a[jj] == a[j]) if jj < j else (a[jj] > a[j])
                    rank = rank + jnp.where(ahead, 1.0, 0.0)
            e = g * EXPERTS_PER_GROUP + j
            picked.append(jnp.where((gbest == g) & (rank < 2.0), s[e:e + 1, :], 0.0))
    den = picked[0]
    for p in picked[1:]:
        den = den + p
    return jnp.concatenate([p / den for p in picked], axis=0)


def _post_tail(x, mix, gate, gf, shf, scf, wrt_ref, br_ref, x2_ref, hf_ref, cwt_ref):
    x2 = x + gate * mix
    x2_ref[...] = x2
    hf = _modulate(x2, gf, shf, scf)
    hf_ref[...] = hf.astype(BF16)
    cwt_ref[...] = _route(hf, wrt_ref[...], br_ref[...])


def _post_even_kernel(x_ref, oa_ref, bg_ref, u_ref, um1_ref, um2_ref, gate_ref, shf_ref, scf_ref,
                      gf_ref, cw_ref, cb_ref, wo_ref, wrt_ref, br_ref, x2_ref, hf_ref, cwt_ref):
    cw = cw_ref[...]
    y = cb_ref[...] + cw[0:1, :] * um2_ref[...]
    y = y + cw[1:2, :] * um1_ref[...]
    y = y + cw[2:3, :] * u_ref[...]
    n_a = N_ATT_HEADS * LANE
    mix = (jnp.dot(oa_ref[...], wo_ref[0:n_a, :], preferred_element_type=F32)
           + jnp.dot((bg_ref[...] * y).astype(BF16), wo_ref[n_a:n_a + CONV_CH, :], preferred_element_type=F32))
    _post_tail(x_ref[...], mix, gate_ref[...], gf_ref[...], shf_ref[...], scf_ref[...], wrt_ref, br_ref,
               x2_ref, hf_ref, cwt_ref)


POST_OUTS = [(1024, F32), (1024, BF16)]


def _moe_kernel(hf_ref, cw_ref, x2_ref, gate_ref, wg_ref, wu_ref, wd_ref, o_ref, acc_ref):
    e = pl.program_id(1)

    @pl.when(e == 0)
    def _():
        acc_ref[...] = jnp.zeros_like(acc_ref)

    hf = hf_ref[...]
    hmid = _silu(_dot(hf, wg_ref[...])) * _dot(hf, wu_ref[...])
    cw = cw_ref[...]
    lane = lax.broadcasted_iota(I32, cw.shape, 1)
    wcol = jnp.sum(jnp.where(lane == e, cw, 0.0), axis=1, keepdims=True)
    acc_ref[...] += _dot(hmid, wd_ref[...]) * wcol

    @pl.when(e == N_EXPERTS - 1)
    def _():
        o_ref[...] = x2_ref[...] + gate_ref[...] * acc_ref[...]


def _moe(hf, cw, x2, gate, wg, wu, wd, layer, tile, tpb):
    rows, d = x2.shape
    de = wg.shape[3]
    return pl.pallas_call(
        _moe_kernel,
        grid=(rows // tile, N_EXPERTS),
        in_specs=[pl.BlockSpec((tile, d), lambda t, e: (t, 0)),
                  pl.BlockSpec((tile, N_EXPERTS), lambda t, e: (t, 0)),
                  pl.BlockSpec((tile, d), lambda t, e: (t, 0)),
                  pl.BlockSpec((None,) + gate.shape[1:], lambda t, e: (t // tpb, 0, 0)),
                  pl.BlockSpec((None, None, d, de), lambda t, e: (layer, e, 0, 0)),
                  pl.BlockSpec((None, None, d, de), lambda t, e: (layer, e, 0, 0)),
                  pl.BlockSpec((None, None, de, d), lambda t, e: (layer, e, 0, 0))],
        out_specs=pl.BlockSpec((tile, d), lambda t, e: (t, 0)),
        out_shape=jax.ShapeDtypeStruct((rows, d), F32),
        scratch_shapes=[pltpu.VMEM((tile, d), F32)],
        compiler_params=_cparams("parallel", "arbitrary"),
        name="moe_dense",
    )(hf, cw, x2, gate, wg, wu, wd)


def _cache_rows(kv_t):
    b, _, s = kv_t.shape
    return kv_t.reshape(b, 2, N_KV_HEADS, HEAD_DIM, s).transpose(0, 4, 1, 2, 3)


def _shift_rows(u3, k):
    return jnp.pad(u3, ((0, 0), (k, 0), (0, 0)))[:, :u3.shape[1]]


def _even_layer(xp, xs, mp, ms, page_table, kv_pool, kidx_pool, conv_buf, rel_bias, router, experts,
                g_mix, g_ffn, w_in, w_out, q_norm, k_norm, conv_w, conv_b):
    B, S, D = xp.shape
    Bs = xs.shape[0]
    past = page_table.shape[1] * PAGE_SIZE
    wts = _even_weights(w_in, w_out, q_norm, k_norm)
    gm = _gmats()
    wrt, br = router
    wg, wu, wd, lyr = experts
    g_mix = g_mix.reshape(1, D)
    g_ffn = g_ffn.reshape(1, D)
    post_consts = [g_ffn, conv_w, conv_b.reshape(1, CONV_CH), wts[1], wrt, br]

    tp = ROW_TILE
    q_p, _, kv_b, qi_p, misc, misc_b, bg, u, kv_t, misc_t = _inproj_even(xp.reshape(B * S, D), [mp[0], mp[1]],
                                                                        g_mix, wts, gm, tp, S // tp)
    r3 = lambda a: a.reshape(B, S, a.shape[-1])
    oa = _dsa_t(r3(q_p), r3(qi_p), r3(misc), r3(misc_b), r3(kv_b), _bias_tiles_t(rel_bias), gm[2].T,
                n_keep=min(TOPK_MAX, S // 4))
    u3 = r3(u)
    um1 = _shift_rows(u3, 1).reshape(B * S, CONV_CH)
    um2 = _shift_rows(u3, 2).reshape(B * S, CONV_CH)
    x2, hf, cwt = _row_call(_post_even_kernel, [xp.reshape(B * S, D), oa.reshape(B * S, -1), bg, u, um1, um2],
                            [mp[2], mp[3], mp[4]], post_consts, POST_OUTS, tp, S // tp, "post_even",
                            outs_t=[(N_EXPERTS, F32)])
    tm = min(MOE_TILE, S)
    xp3 = _moe(hf, cwt.T, x2, mp[5], wg, wu, wd, lyr, tm, S // tm).reshape(B, S, D)
    outs_p = (_cache_rows(kv_t), misc_t[:, :IDX_DIM].transpose(0, 2, 1), u3[:, S - 2:])

    q_s, kv_fs, _, qi_s, misc_s, _, bg_s, u_s, _, _ = _inproj_even(xs, [ms[0], ms[1]], g_mix, wts, gm, Bs, 1)
    n_pool = kv_pool.shape[0]
    kv_t = kv_pool.transpose(0, 2, 3, 4, 1).reshape(n_pool, 2 * LANE, PAGE_SIZE)
    ki_t = kidx_pool.transpose(0, 2, 1)
    qi8 = jnp.pad(qi_s.reshape(Bs, IDX_HEADS, LANE)[:, :, :IDX_DIM], ((0, 0), (0, N_ATT_HEADS - IDX_HEADS), (0, 0)))
    wi8 = jnp.pad(misc_s[:, IDX_DIM:IDX_DIM + IDX_HEADS], ((0, 0), (0, N_ATT_HEADS - IDX_HEADS)))[:, :, None]
    oa_s = _dsa_sample(q_s.reshape(Bs, N_ATT_HEADS, LANE), qi8, wi8, misc_s[:, None, :IDX_DIM], kv_fs[:, None, :],
                       _bias_row(rel_bias, past), gm[2], ki_t, kv_t, page_table,
                       n_keep=min(TOPK_MAX, (past + 1) // 4), grp=_pick_group(Bs, SAMPLE_GROUP)).reshape(Bs, -1)
    x2s, hfs, cwts = _row_call(_post_even_kernel, [xs, oa_s, bg_s, u_s, conv_buf[:, 1], conv_buf[:, 0]],
                               [ms[2], ms[3], ms[4]], post_consts, POST_OUTS, Bs, 1, "post_even_s",
                               outs_t=[(N_EXPERTS, F32)])
    xs3 = _moe(hfs, cwts.T, x2s, ms[5], wg, wu, wd, lyr, Bs, 1)
    outs_s = (kv_fs.reshape(Bs, 1, 2, N_KV_HEADS, HEAD_DIM), misc_s[:, None, :IDX_DIM],
              jnp.concatenate([conv_buf[:, 1:], u_s[:, None, :]], axis=1))
    return xp3, xs3, outs_p, outs_s


O_Z0, O_Q0, O_KVC0, O_KVS0, O_KVW0, O_G0, O_END = 0, 1792, 2816, 3072, 3328, 3584, 3712
P_C = 3 * RWKV_DIM + LORA_W + LORA_A + LORA_G


def _inproj_odd_kernel(x_ref, shift_ref, scale_ref, g_ref, w_ref, qg_ref, ksg_ref, kwg_ref, gone_ref, gtwo_ref,
                       zc_ref, q_ref, kvc_ref, kvs_ref, kvsb_ref, kvw_ref, kvwb_ref, gates_ref,
                       kvct_ref, kvst_ref, kvwt_ref):
    h = _modulate(x_ref[...], g_ref[...], shift_ref[...], scale_ref[...])
    z = jnp.dot(h.astype(BF16), w_ref[...], preferred_element_type=F32)
    zc_ref[...] = z[:, O_Z0:O_Q0]
    gone = gone_ref[...]
    gtwo = gtwo_ref[...]
    for t in range(N_ATT_HEADS):
        sl = slice(t * LANE, (t + 1) * LANE)
        q_ref[:, sl] = _group_rms(z[:, O_Q0 + t * LANE:O_Q0 + (t + 1) * LANE], gone, qg_ref[:, sl]).astype(BF16)
    kvc_ref[...] = z[:, O_KVC0:O_KVS0]
    kvct_ref[0:LANE, :] = z[:, O_KVC0:O_KVC0 + LANE].T
    kvct_ref[LANE:2 * LANE, :] = z[:, O_KVC0 + LANE:O_KVS0].T
    for base, gain_ref, f_ref, b_ref, t_ref in ((O_KVS0, ksg_ref, kvs_ref, kvsb_ref, kvst_ref),
                                                (O_KVW0, kwg_ref, kvw_ref, kvwb_ref, kvwt_ref)):
        k = _group_rms(z[:, base:base + LANE], gtwo, gain_ref[...])
        v = z[:, base + LANE:base + 2 * LANE]
        t_ref[0:LANE, :] = k.T
        t_ref[LANE:2 * LANE, :] = v.T
        f_ref[:, 0:LANE] = k
        f_ref[:, LANE:2 * LANE] = v
        b_ref[:, 0:LANE] = k.astype(BF16)
        b_ref[:, LANE:2 * LANE] = v.astype(BF16)
    gates_ref[...] = _sigmoid(z[:, O_G0:O_END])


def _odd_weights(w_in, w_out, q_norm, k_norm):
    d = w_in.shape[0]
    a_q, a_kv = N_ATT_HEADS * HEAD_DIM, 2 * N_KV_HEADS * HEAD_DIM
    o = P_C
    wz = w_in[:, :o]
    wq = w_in[:, o:o + a_q]; o += a_q
    wkv = w_in[:, o:o + 3 * a_kv]; o += 3 * a_kv
    wg = w_in[:, o:]
    wg = jnp.concatenate([wg, jnp.zeros((d, LANE - wg.shape[1]), F32)], -1)
    w_in_p = jnp.concatenate([wz, _pad_q_cols(wq), wkv, wg], -1).astype(BF16)
    w_out_p = jnp.concatenate([w_out[:RWKV_DIM], _pad_o_rows(w_out[RWKV_DIM:])], 0).astype(BF16)
    qg = jnp.tile(q_norm, 2 * N_ATT_HEADS).reshape(1, N_ATT_HEADS * LANE)
    ksg = jnp.tile(k_norm[1], 2).reshape(1, LANE)
    kwg = jnp.tile(k_norm[2], 2).reshape(1, LANE)
    return w_in_p, w_out_p, qg, ksg, kwg


def _inproj_odd(x, mods, g, wts, gm, tile, tpb):
    w_in_p, _, qg, ksg, kwg = wts
    gone, gtwo, _ = gm
    outs = [(P_C, F32), (N_ATT_HEADS * LANE, BF16), (2 * LANE, F32), (2 * LANE, F32), (2 * LANE, BF16),
            (2 * LANE, F32), (2 * LANE, BF16), (LANE, F32)]
    return _row_call(_inproj_odd_kernel, x, mods, [g, w_in_p, qg, ksg, kwg, gone, gtwo], outs, tile, tpb,
                     "inproj_odd", outs_bt=[(2 * LANE, F32)] * 3)


def _rwkv_pre_kernel(z_ref, zp_ref, mu_ref, w0_ref, a0_ref, kk_ref, ka_ref, wup_ref, aup_ref, gup_ref, gsum_ref,
                     r_o, w_o, k_o, v_o, kk_o, kka_o, g_o):
    z = z_ref[...]
    zm = z + (zp_ref[...] - z) * mu_ref[...]
    r = zm[:, 0:RWKV_DIM]
    k = zm[:, RWKV_DIM:2 * RWKV_DIM]
    v = zm[:, 2 * RWKV_DIM:3 * RWKV_DIM]
    t12 = zm[:, 3 * RWKV_DIM:3 * RWKV_DIM + LANE]
    gd = zm[:, 3 * RWKV_DIM + LANE:P_C]
    xw = w0_ref[...] + _dot(jnp.tanh(t12), wup_ref[...])
    sp = jnp.maximum(-xw, 0.0) + jnp.log(1.0 + jnp.exp(-jnp.abs(xw)))
    w_o[...] = jnp.exp(-jnp.exp(-sp - 0.5))
    a = _sigmoid(a0_ref[...] + _dot(t12, aup_ref[...]))
    g_o[...] = _dot(_sigmoid(gd), gup_ref[...])
    kk = k * kk_ref[...]
    gsum = gsum_ref[...]
    for t in range(RWKV_DIM // LANE):
        sl = slice(t * LANE, (t + 1) * LANE)
        kt = kk[:, sl]
        nrm = jnp.maximum(jnp.sqrt(_dot_split(kt * kt, gsum)), 1e-12)
        kn = kt / nrm
        kk_o[:, sl] = kn
        kka_o[:, sl] = kn * a[:, sl]
    r_o[...] = r
    v_o[...] = v
    k_o[...] = k * (1.0 + (a - 1.0) * ka_ref[...])


def _rwkv_pre(zc, zprev, cpar, gsum, tile):
    mu, w0, w_up, a0, a_up, g_up, k_k, k_a = cpar
    z64 = jnp.zeros((LORA_W, RWKV_DIM), F32)
    consts = [mu.reshape(1, P_C), w0.reshape(1, -1), a0.reshape(1, -1), k_k.reshape(1, -1), k_a.reshape(1, -1),
              jnp.concatenate([w_up, z64], 0).astype(BF16), jnp.concatenate([z64, a_up], 0).astype(BF16),
              g_up.astype(BF16), gsum]
    return _row_call(_rwkv_pre_kernel, [zc, zprev], [], consts, [(RWKV_DIM, F32)] * 7, tile, 1, "rwkv_pre")


SCAN_P = 64
SCAN_VH = HEAD_DIM // 2


def _scan_kernel(kk_ref, w_ref, kka_ref, k_ref, r_ref, v_ref, s0_ref, y_ref, so_ref, st, *, tc):
    ti = pl.program_id(1)

    @pl.when(ti == 0)
    def _():
        st[...] = s0_ref[...]

    lo_half = lax.broadcasted_iota(I32, (1, LANE), 1) < SCAN_P

    def lanes_of(ref, t):
        x = ref[:, t].reshape(SCAN_P, HEAD_DIM)
        return jnp.concatenate([x, x], axis=0).T

    def step(t, c):
        kk, w, kka, kt, rt, vf = (lanes_of(ref, t) for ref in (kk_ref, w_ref, kka_ref, k_ref, r_ref, v_ref))
        ys = []
        for vi in range(SCAN_VH):
            s = st[vi]
            sa = -jnp.sum(s * kk, axis=0, keepdims=True)
            vrow = jnp.where(lo_half, vf[vi:vi + 1, :], vf[vi + SCAN_VH:vi + SCAN_VH + 1, :])
            sn = s * w + sa * kka + vrow * kt
            st[vi] = sn
            ys.append(jnp.sum(sn * rt, axis=0, keepdims=True))
        y_ref[t] = jnp.concatenate(ys, axis=0)
        return c

    lax.fori_loop(0, tc, step, 0)

    @pl.when(ti == pl.num_programs(1) - 1)
    def _():
        so_ref[...] = st[...]


def _scan_unlayout_y(y, b, t):
    p = b * RWKV_HEADS
    nc = y.shape[0]
    a = jnp.concatenate([y[..., :SCAN_P], y[..., SCAN_P:]], axis=2)
    a = a.transpose(1, 2, 0, 3).reshape(t, HEAD_DIM, nc * SCAN_P)[:, :, :p]
    return a.reshape(t, HEAD_DIM, b, RWKV_HEADS).transpose(2, 0, 3, 1).reshape(b * t, RWKV_DIM)


def _scan_layout_state(s):
    b = s.shape[0]
    p = b * RWKV_HEADS
    nc = -(-p // SCAN_P)
    a = jnp.pad(s.reshape(p, HEAD_DIM, HEAD_DIM), ((0, nc * SCAN_P - p), (0, 0), (0, 0)))
    a = a.reshape(nc, SCAN_P, HEAD_DIM, HEAD_DIM).transpose(0, 2, 3, 1)
    return jnp.concatenate([a[:, :SCAN_VH], a[:, SCAN_VH:]], -1)


def _scan_unlayout_state(st, b):
    p = b * RWKV_HEADS
    nc = st.shape[0]
    a = jnp.concatenate([st[..., :SCAN_P], st[..., SCAN_P:]], axis=1)
    a = a.transpose(0, 3, 1, 2).reshape(nc * SCAN_P, HEAD_DIM, HEAD_DIM)[:p]
    return a.reshape(b, RWKV_HEADS, HEAD_DIM, HEAD_DIM)


def _rwkv_scan(pre, s0, b, t, tc):
    r, w, k, v, kk, kka, _ = pre
    seqs = SCAN_P // RWKV_HEADS
    bp = -(-b // seqs) * seqs
    ops = [jnp.pad(x.reshape(b, t, RWKV_HEADS, HEAD_DIM), ((0, bp - b), (0, 0), (0, 0), (0, 0)))
           for x in (kk, w, kka, k, r, v)]
    s0l = _scan_layout_state(s0)
    nc = s0l.shape[0]
    kspec = pl.BlockSpec((seqs, tc, RWKV_HEADS, HEAD_DIM), lambda c, i: (c, i, 0, 0))
    vspec = pl.BlockSpec((None, tc, SCAN_VH, LANE), lambda c, i: (c, i, 0, 0))
    sspec = pl.BlockSpec((None, SCAN_VH, HEAD_DIM, LANE), lambda c, i: (c, 0, 0, 0))
    y, so = pl.pallas_call(
        functools.partial(_scan_kernel, tc=tc),
        grid=(nc, t // tc),
        in_specs=[kspec] * 6 + [sspec],
        out_specs=[vspec, sspec],
        out_shape=[jax.ShapeDtypeStruct((nc, t, SCAN_VH, LANE), F32),
                   jax.ShapeDtypeStruct((nc, SCAN_VH, HEAD_DIM, LANE), F32)],
        scratch_shapes=[pltpu.VMEM((SCAN_VH, HEAD_DIM, LANE), F32)],
        compiler_params=_cparams("parallel", "arbitrary"),
        name="rwkv_scan",
    )(*ops, s0l)
    return _scan_unlayout_y(y, b, t), _scan_unlayout_state(so, b)


def _compress_kernel(x_ref, pe_ref, w_ref, kg_ref, gtwo_ref, o_ref):
    z = jnp.dot((x_ref[...] + pe_ref[...]).astype(BF16), w_ref[...], preferred_element_type=F32)
    o_ref[:, 0:LANE] = _group_rms(z[:, 0:LANE], gtwo_ref[...], kg_ref[...])
    o_ref[:, LANE:2 * LANE] = z[:, LANE:2 * LANE]


def _compress_weights(cmp_pe, cmp_w, k_norm_c):
    wk = cmp_w[0].reshape(CMP_BLOCK, HEAD_DIM, HEAD_DIM)
    wv = cmp_w[1].reshape(CMP_BLOCK, HEAD_DIM, HEAD_DIM)
    full = jnp.einsum("srde,st->rsdte", jnp.stack([wk, wk, wv, wv]), jnp.eye(4, dtype=F32))
    pe = jnp.stack([cmp_pe[0], cmp_pe[0], cmp_pe[1], cmp_pe[1]], axis=1)
    return (full.reshape(CMP_BLOCK * 4 * HEAD_DIM, 4 * HEAD_DIM).astype(BF16), pe.reshape(1, -1),
            jnp.tile(k_norm_c, 2).reshape(1, LANE))


def _compress(rows, cw, gtwo, tile):
    wfull, pe, kg = cw
    return _row_call(_compress_kernel, rows, [], [pe, wfull, kg, gtwo], [(2 * LANE, F32)], tile, 1, "nsa_compress")[0]


def _compress_paged_kernel(pt_ref, ident_ref, pe_ref, w_ref, kg_ref, gtwo_ref, *refs, grp, n_pages):
    del pt_ref
    page_refs, o_ref, xs = refs[:grp * n_pages], refs[-2], refs[-1]
    ident = ident_ref[...]
    for i in range(grp * n_pages):
        for half in range(2):
            xt = (page_refs[i][half * LANE:(half + 1) * LANE, :] + pe_ref[half]).astype(BF16)
            xs[half, i * PAGE_SIZE:(i + 1) * PAGE_SIZE, :] = _dot_nt(ident, xt)
    n_blk = grp * n_pages * (PAGE_SIZE // CMP_BLOCK)
    acc = [jnp.zeros((n_blk, LANE), F32) for _ in range(2)]
    for r in range(CMP_BLOCK):
        for half in range(2):
            rows = xs[half, pl.ds(r, n_blk, stride=CMP_BLOCK), :]
            acc[half] = acc[half] + jnp.dot(rows.astype(BF16), w_ref[half, r], preferred_element_type=F32)
    kc = _group_rms(acc[0], gtwo_ref[...], kg_ref[...])
    per_seq = n_blk // grp
    for g in range(grp):
        o_ref[g, 0:per_seq, 0:LANE] = kc[g * per_seq:(g + 1) * per_seq].astype(BF16)
        o_ref[g, 0:per_seq, LANE:2 * LANE] = acc[1][g * per_seq:(g + 1) * per_seq].astype(BF16)
        o_ref[g, per_seq:LANE, :] = jnp.zeros((LANE - per_seq, 2 * LANE), BF16)


def _compress_paged(cmp_t, page_table, cmp_pe, cmp_w, k_norm_c, gtwo, grp):
    b, n_pages = page_table.shape
    wk = cmp_w[0].reshape(CMP_BLOCK, HEAD_DIM, HEAD_DIM)
    wv = cmp_w[1].reshape(CMP_BLOCK, HEAD_DIM, HEAD_DIM)
    wbd = jnp.einsum("krde,ht->krhdte", jnp.stack([wk, wv]), jnp.eye(N_KV_HEADS, dtype=F32))
    wbd = wbd.reshape(2, CMP_BLOCK, LANE, LANE).astype(BF16)
    pe = jnp.tile(cmp_pe.transpose(0, 2, 1), (1, N_KV_HEADS, PAGE_SIZE // CMP_BLOCK))
    ident = jnp.eye(LANE, dtype=BF16)
    kg = jnp.tile(k_norm_c, 2).reshape(1, LANE)
    cspec = lambda a: pl.BlockSpec(a.shape, lambda i, pt: (0,) * a.ndim)
    kern = functools.partial(_compress_paged_kernel, grp=grp, n_pages=n_pages)
    return pl.pallas_call(
        kern,
        grid_spec=pltpu.PrefetchScalarGridSpec(
            num_scalar_prefetch=1, grid=(b // grp,),
            in_specs=[cspec(ident), cspec(pe), cspec(wbd), cspec(kg), cspec(gtwo)]
            + _page_specs(cmp_t, n_pages, grp),
            out_specs=pl.BlockSpec((grp, LANE, 2 * LANE), lambda i, pt: (i, 0, 0)),
            scratch_shapes=[pltpu.VMEM((2, grp * n_pages * PAGE_SIZE, LANE), F32)]),
        out_shape=jax.ShapeDtypeStruct((b, LANE, 2 * LANE), BF16),
        compiler_params=_cparams("parallel"),
        name="nsa_compress_paged",
    )(page_table, ident, pe, wbd, kg, gtwo, *([cmp_t] * (grp * n_pages)))


def _bias_cmp(rel_bias, q_starts, qb):
    q0 = jnp.asarray(q_starts, I32)[:, None, None]
    r = jnp.arange(qb)[None, :, None]
    n = jnp.arange(LANE)[None, None, :]
    return _bias_lookup(rel_bias, q0 + r - (n * CMP_BLOCK + CMP_BLOCK - 1)).transpose(0, 3, 1, 2)


def _nsa_sample_kernel(pt_ref, q_ref, gates_ref, kcv_ref, snew_ref, wnew_ref, win_ref, bias_ref, biasc_ref,
                       biasw_ref, pair_ref, *refs, grp, n_pages, t_pos, n_cmp, n_sel, w_eff):
    del pt_ref
    slc_refs, o_ref = refs[:grp * n_pages], refs[-1]
    lane1 = lax.broadcasted_iota(I32, (1, LANE), 1)
    lane8 = lax.broadcasted_iota(I32, (N_ATT_HEADS, LANE), 1)
    row_all = lax.broadcasted_iota(I32, (N_ATT_HEADS, (n_pages + 1) * LANE), 0)
    lanew = lax.broadcasted_iota(I32, (N_ATT_HEADS, w_eff), 1)
    lane1_f = lane1.astype(F32)
    bias = bias_ref[...]
    bias_now = bias[:, n_pages * LANE:n_pages * LANE + 1]
    mask_c = ((lane8 * CMP_BLOCK + CMP_BLOCK - 1) <= t_pos) & (lane8 < n_cmp)
    cur = t_pos // SEL_BLOCK
    forced = (lane1 == 0) | (lane1 == cur) | (lane1 == cur - 1)
    sel_causal = lane1 * SEL_BLOCK <= t_pos
    tail_valid = jnp.concatenate([lane1] * n_pages + [lane1 + LANE], axis=1) <= LANE
    for g in range(grp):
        q = q_ref[g]
        qf = q.astype(F32)
        s = jnp.where(mask_c, _dot_nt(q, kcv_ref[g, :, 0:LANE]) * ATT_SCALE + biasc_ref[...], NEG)
        e = jnp.exp(s - jnp.max(s, axis=1, keepdims=True))
        pc = jnp.where(mask_c, e / jnp.sum(e, axis=1, keepdims=True), 0.0)
        o_c = jnp.dot(pc.astype(BF16), kcv_ref[g, :, LANE:2 * LANE], preferred_element_type=F32)
        masks = []
        for hk in range(N_KV_HEADS):
            ps = jnp.sum(pc[hk * GQA:(hk + 1) * GQA], axis=0, keepdims=True)
            score = _dot_split(ps, pair_ref[...])
            score = jnp.where(sel_causal, jnp.where(forced, FORCE, score), NEG)
            picked = jnp.zeros((1, LANE), F32)
            for _ in range(n_sel):
                mx = jnp.max(score, axis=1, keepdims=True)
                first = jnp.min(jnp.where(score == mx, lane1_f, float(LANE)), axis=1, keepdims=True)
                hit = lane1_f == first
                picked = jnp.where(hit, 1.0, picked)
                score = jnp.where(hit, TAKEN, score)
            per_page = PAGE_SIZE // SEL_BLOCK
            tiles = []
            for p in range(n_pages + 1):
                t = jnp.zeros((1, LANE), F32)
                for a in range(per_page):
                    blk = picked[:, p * per_page + a:p * per_page + a + 1]
                    t = jnp.where(lane1 // SEL_BLOCK == a, blk, t)
                tiles.append(t)
            masks.append(jnp.concatenate(tiles, axis=1))
        valid = (jnp.where(row_all < GQA, masks[0], masks[1]) > 0.5) & tail_valid
        snew = _bf16_round(snew_ref[g])
        tiles = [_dot(q, slc_refs[g * n_pages + p][0:LANE, :]) for p in range(n_pages)]
        tiles.append(jnp.where(lane1 == 0, jnp.sum(qf * snew[:, 0:LANE], axis=1, keepdims=True), 0.0))
        s = jnp.where(valid, jnp.concatenate(tiles, axis=1) * ATT_SCALE + bias, NEG)
        e = jnp.exp(s - jnp.max(s, axis=1, keepdims=True))
        p_all = jnp.where(valid, e / jnp.sum(e, axis=1, keepdims=True), 0.0)
        o_s = p_all[:, n_pages * LANE:n_pages * LANE + 1] * snew[:, LANE:2 * LANE]
        for p in range(n_pages):
            o_s = o_s + _dot_nt(p_all[:, p * LANE:(p + 1) * LANE], slc_refs[g * n_pages + p][LANE:2 * LANE, :])
        wnew = _bf16_round(wnew_ref[g])
        valid_w = (w_eff - lanew) < WINDOW
        s_w = jnp.where(valid_w, _dot(q, win_ref[g, 0:LANE, :]) * ATT_SCALE + biasw_ref[...], NEG)
        s_n = jnp.sum(qf * wnew[:, 0:LANE], axis=1, keepdims=True) * ATT_SCALE + bias_now
        m = jnp.maximum(jnp.max(s_w, axis=1, keepdims=True), s_n)
        e_w = jnp.where(valid_w, jnp.exp(s_w - m), 0.0)
        e_n = jnp.exp(s_n - m)
        den = jnp.sum(e_w, axis=1, keepdims=True) + e_n
        o_w = _dot_nt(e_w / den, win_ref[g, LANE:2 * LANE, :]) + (e_n / den) * wnew[:, LANE:2 * LANE]
        gates = gates_ref[g]
        _head_rows_out(o_ref, g, gates[:, 0:1] * o_c + gates[:, 1:2] * o_s + gates[:, 2:3] * o_w)


def _nsa_sample(q8, gates8, kcv, snew, wnew, win_t, bias, bias_c, bias_w, pair, slc_t, page_table,
                *, t_pos, n_cmp, n_sel, grp):
    b, n_pages = page_table.shape
    w_eff = win_t.shape[2]
    gspec = lambda a: pl.BlockSpec((grp,) + a.shape[1:], lambda i, pt: (i,) + (0,) * (a.ndim - 1))
    cspec = lambda a: pl.BlockSpec(a.shape, lambda i, pt: (0,) * a.ndim)
    kern = functools.partial(_nsa_sample_kernel, grp=grp, n_pages=n_pages, t_pos=t_pos, n_cmp=n_cmp, n_sel=n_sel,
                             w_eff=w_eff)
    return pl.pallas_call(
        kern,
        grid_spec=pltpu.PrefetchScalarGridSpec(
            num_scalar_prefetch=1, grid=(b // grp,),
            in_specs=[gspec(q8), gspec(gates8), gspec(kcv), gspec(snew), gspec(wnew), gspec(win_t), cspec(bias),
                      cspec(bias_c), cspec(bias_w), cspec(pair)] + _page_specs(slc_t, n_pages, grp),
            out_specs=pl.BlockSpec((grp, N_ATT_HEADS, LANE), lambda i, pt: (i, 0, 0))),
        out_shape=jax.ShapeDtypeStruct((b, N_ATT_HEADS, LANE), BF16),
        compiler_params=_cparams("parallel"),
        name="nsa_sample",
    )(page_table, q8, gates8, kcv, snew, wnew, win_t, bias, bias_c, bias_w, pair, *([slc_t] * (grp * n_pages)))


def _post_odd_kernel(x_ref, y_ref, r_ref, k_ref, v_ref, g_ref, od_ref, gate_ref, shf_ref, scf_ref,
                     gf_ref, lnw_ref, lnb_ref, rk_ref, gtwo_ref, wo_ref, wrt_ref, br_ref, x2_ref, hf_ref, cwt_ref):
    gtwo = gtwo_ref[...]
    mix = jnp.dot(od_ref[...], wo_ref[RWKV_DIM:RWKV_DIM + N_ATT_HEADS * LANE, :], preferred_element_type=F32)
    for t in range(RWKV_DIM // LANE):
        sl = slice(t * LANE, (t + 1) * LANE)
        y = y_ref[:, sl]
        dlt = y - _dot_split(y, gtwo)
        yn = (dlt * lax.rsqrt(_dot_split(dlt * dlt, gtwo) + GN_EPS)) * lnw_ref[:, sl] + lnb_ref[:, sl]
        dot_rk = _dot_split(r_ref[:, sl] * k_ref[:, sl] * rk_ref[:, sl], gtwo) * float(HEAD_DIM)
        oc = (yn + dot_rk * v_ref[:, sl]) * g_ref[:, sl]
        mix = mix + jnp.dot(oc.astype(BF16), wo_ref[sl, :], preferred_element_type=F32)
    _post_tail(x_ref[...], mix, gate_ref[...], gf_ref[...], shf_ref[...], scf_ref[...], wrt_ref, br_ref,
               x2_ref, hf_ref, cwt_ref)


def _odd_layer(xp, xs, mp, ms, page_table, wkv0, shift0, cmp_pool, slc_pool, win_buf, rel_bias, router, experts,
               g_mix, g_ffn, w_in, w_out, cpar, r_k, ln_w, ln_b, q_norm, k_norm, cmp_pe, cmp_w):
    B, S, D = xp.shape
    Bs = xs.shape[0]
    n_pages = page_table.shape[1]
    past = n_pages * PAGE_SIZE
    wts = _odd_weights(w_in, w_out, q_norm, k_norm)
    gm = _gmats()
    gone, gtwo, _ = gm
    gsum = (gtwo.astype(F32) * HEAD_DIM).astype(BF16)
    i = jnp.arange(LANE)
    pair = jnp.where(i[:, None] // 2 == i[None, :], 1.0, 0.0).astype(BF16)
    cw = _compress_weights(cmp_pe, cmp_w, k_norm[0])
    wrt, br = router
    wg, wu, wd, lyr = experts
    g_mix = g_mix.reshape(1, D)
    post_consts = [g_ffn.reshape(1, D), ln_w.reshape(1, -1), ln_b.reshape(1, -1), r_k.reshape(1, -1), gtwo, wts[1],
                   wrt, br]
    w_eff = win_buf.shape[1]

    tp = ROW_TILE
    zc, q_p, kvc, _, kvs_b, _, kvw_b, gates, kvc_t, kvs_t, kvw_t = _inproj_odd(xp.reshape(B * S, D), [mp[0], mp[1]],
                                                                               g_mix, wts, gm, tp, S // tp)
    r3 = lambda a: a.reshape(B, S, a.shape[-1])
    pre = _rwkv_pre(zc, _shift_rows(r3(zc), 1).reshape(B * S, P_C), cpar, gsum, tp)
    y, wkv_p = _rwkv_scan(pre, jnp.zeros((B, RWKV_HEADS, HEAD_DIM, HEAD_DIM), F32), B, S, SCAN_TIME_CHUNK)
    n_cmp = S // CMP_BLOCK
    kcv = _compress(kvc.reshape(B * n_cmp, CMP_BLOCK * 2 * LANE), cw, gtwo, _pick_tile(B * n_cmp, 256))
    kcv = jnp.pad(kcv.reshape(B, n_cmp, 2 * LANE), ((0, 0), (0, LANE - n_cmp), (0, 0))).astype(BF16)
    n_slc = -(-S // SEL_BLOCK)
    od = _nsa_t(r3(q_p), r3(gates), kcv, r3(kvs_b), r3(kvw_b), _bias_tiles_t(rel_bias), _bias_cmp_t(rel_bias, S),
                pair.T, n_cmp=n_cmp, n_sel=min(N_SEL_BLOCKS, n_slc))
    x2, hf, cwt = _row_call(_post_odd_kernel,
                            [xp.reshape(B * S, D), y, pre[0], pre[2], pre[3], pre[6], od.reshape(B * S, -1)],
                            [mp[2], mp[3], mp[4]], post_consts, POST_OUTS, tp, S // tp, "post_odd",
                            outs_t=[(N_EXPERTS, F32)])
    tm = min(MOE_TILE, S)
    xp3 = _moe(hf, cwt.T, x2, mp[5], wg, wu, wd, lyr, tm, S // tm).reshape(B, S, D)
    kv5 = lambda a, n: a.reshape(-1, n, 2, N_KV_HEADS, HEAD_DIM)
    outs_p = (wkv_p, r3(zc)[:, S - 1], _cache_rows(kvc_t), _cache_rows(kvs_t),
              _cache_rows(kvw_t[:, :, S - min(WINDOW, S):]))

    zc_s, q_s, kvc_s, kvs_s, _, kvw_s, _, gates_s, _, _, _ = _inproj_odd(xs, [ms[0], ms[1]], g_mix, wts, gm, Bs, 1)
    pre_s = _rwkv_pre(zc_s, shift0, cpar, gsum, Bs)
    y_s, wkv_s = _rwkv_scan(pre_s, wkv0, Bs, 1, 1)
    n_pool = cmp_pool.shape[0]
    cmp_t = cmp_pool.transpose(0, 2, 3, 4, 1).reshape(n_pool, 2 * LANE, PAGE_SIZE)
    kcv_s = _compress_paged(cmp_t, page_table, cmp_pe, cmp_w, k_norm[0], gtwo, _pick_group(Bs, SAMPLE_GROUP))
    n_cmp_s = (past + 1) // CMP_BLOCK
    slc_t = slc_pool.transpose(0, 2, 3, 4, 1).reshape(n_pool, 2 * LANE, PAGE_SIZE)
    win_t = win_buf.transpose(0, 2, 3, 4, 1).reshape(Bs, 2 * LANE, w_eff)
    gates8 = jnp.pad(gates_s[:, :3 * N_ATT_HEADS].reshape(Bs, 3, N_ATT_HEADS).transpose(0, 2, 1),
                     ((0, 0), (0, 0), (0, LANE - 3)))
    n_slc_s = -(-(past + 1) // SEL_BLOCK)
    od_s = _nsa_sample(q_s.reshape(Bs, N_ATT_HEADS, LANE), gates8, kcv_s, kvs_s[:, None, :], kvw_s[:, None, :],
                       win_t, _bias_row(rel_bias, past), _bias_cmp(rel_bias, [past], 1)[0, :, 0, :],
                       _bias_lookup(rel_bias, w_eff - jnp.arange(w_eff)).T, pair, slc_t, page_table,
                       t_pos=past, n_cmp=n_cmp_s, n_sel=min(N_SEL_BLOCKS, n_slc_s),
                       grp=_pick_group(Bs, SAMPLE_GROUP)).reshape(Bs, -1)
    x2s, hfs, cwts = _row_call(_post_odd_kernel, [xs, y_s, pre_s[0], pre_s[2], pre_s[3], pre_s[6], od_s],
                               [ms[2], ms[3], ms[4]], post_consts, POST_OUTS, Bs, 1, "post_odd_s",
                               outs_t=[(N_EXPERTS, F32)])
    xs3 = _moe(hfs, cwts.T, x2s, ms[5], wg, wu, wd, lyr, Bs, 1)
    win_new = jnp.concatenate([win_buf[:, 1:], kv5(kvw_s, 1)], axis=1)
    outs_s = (wkv_s, zc_s, kv5(kvc_s, 1), kv5(kvs_s, 1), win_new)
    return xp3, xs3, outs_p, outs_s


def _mods(c_p, c_s, w_all, b, layer):
    nb = c_p.shape[0]
    m = _ada(jnp.concatenate([c_p, c_s], 0), w_all, b, layer)
    parts = jnp.split(m, 6, axis=-1)
    return [p[:nb, None, :] for p in parts], [p[None, nb:, :] for p in parts]


def _forward(x_prompt, x_sample, c_prompt, c_sample, page_table, cache_a_kv, cache_a_kidx, state_b_conv,
             state_c_wkv, state_c_shift, cache_d_cmp, cache_d_slc, cache_d_win, rel_bias, w_router, b_router,
             w_ada, b_ada, g_norm_mix, g_norm_ffn, w_expert_gate, w_expert_up, w_expert_down, e_w_in, e_w_out,
             a_q_norm, a_k_norm, b_conv_w, b_conv_b, o_w_in, o_w_out, c_mu, c_w0, c_w_up, c_a0, c_a_up,
             c_g_up, c_k_k, c_k_a, c_r_k, c_ln_w, c_ln_b, d_q_norm, d_k_norm, d_cmp_pe, d_cmp_w):
    assert w_ada.shape[0] == 2 and e_w_in.shape[0] == 1 and o_w_in.shape[0] == 1
    B, S, D = x_prompt.shape
    Bs = x_sample.shape[0]
    assert x_sample.shape[1] == 1
    xp, xs = x_prompt, x_sample.reshape(Bs, D)
    router = (w_router.T, b_router.reshape(N_EXPERTS, 1))
    n_pool = cache_a_kv.shape[1]
    experts = lambda l: (w_expert_gate, w_expert_up, w_expert_down, l)

    mp, ms = _mods(c_prompt, c_sample, w_ada, b_ada[0], 0)
    xp, xs, ep, es = _even_layer(xp, xs, mp, ms, page_table, cache_a_kv[0], cache_a_kidx[0], state_b_conv[0],
                                 rel_bias, router, experts(0), g_norm_mix[0], g_norm_ffn[0], e_w_in[0], e_w_out[0],
                                 a_q_norm[0], a_k_norm[0], b_conv_w[0], b_conv_b[0])
    mp, ms = _mods(c_prompt, c_sample, w_ada, b_ada[1], 1)
    cpar = (c_mu[0], c_w0[0], c_w_up[0], c_a0[0], c_a_up[0], c_g_up[0], c_k_k[0], c_k_a[0])
    xp, xs, op, os_ = _odd_layer(xp, xs, mp, ms, page_table, state_c_wkv[0], state_c_shift[0], cache_d_cmp[0],
                                 cache_d_slc[0], cache_d_win[0], rel_bias, router, experts(1), g_norm_mix[1],
                                 g_norm_ffn[1], o_w_in[0], o_w_out[0], cpar, c_r_k[0].reshape(-1), c_ln_w[0],
                                 c_ln_b[0], d_q_norm[0], d_k_norm[0], d_cmp_pe[0], d_cmp_w[0])
    stack = lambda ts: tuple(a[None] for a in ts)
    return (xp, xs.reshape(Bs, 1, D)) + stack(ep) + stack(op) + stack(es) + stack(os_)


def kernel(x_prompt, x_sample, c_prompt, c_sample, page_table, cache_a_kv, cache_a_kidx, state_b_conv, state_c_wkv, state_c_shift, cache_d_cmp, cache_d_slc, cache_d_win, rel_bias, w_router, b_router, w_ada, b_ada, g_norm_mix, g_norm_ffn, w_expert_gate, w_expert_up, w_expert_down, e_w_in, e_w_out, a_q_norm, a_k_norm, b_conv_w, b_conv_b, o_w_in, o_w_out, c_mu, c_w0, c_w_up, c_a0, c_a_up, c_g_up, c_k_k, c_k_a, c_r_k, c_ln_w, c_ln_b, d_q_norm, d_k_norm, d_cmp_pe, d_cmp_w):
    return _forward(x_prompt, x_sample, c_prompt, c_sample, page_table, cache_a_kv, cache_a_kidx, state_b_conv,
                    state_c_wkv, state_c_shift, cache_d_cmp, cache_d_slc, cache_d_win, rel_bias, w_router, b_router,
                    w_ada, b_ada, g_norm_mix, g_norm_ffn, w_expert_gate, w_expert_up, w_expert_down, e_w_in, e_w_out,
                    a_q_norm, a_k_norm, b_conv_w, b_conv_b, o_w_in, o_w_out, c_mu, c_w0, c_w_up, c_a0, c_a_up,
                    c_g_up, c_k_k, c_k_a, c_r_k, c_ln_w, c_ln_b, d_q_norm, d_k_norm, d_cmp_pe, d_cmp_w)
```

```python
import functools
import math

import jax
import jax.numpy as jnp
from jax import lax
from jax.experimental import pallas as pl
from jax.experimental.pallas import tpu as pltpu

F32 = jnp.float32
BF16 = jnp.bfloat16
I32 = jnp.int32

LANE = 128
HEAD_DIM = 64
N_ATT_HEADS = 8
N_KV_HEADS = 2
GQA = N_ATT_HEADS // N_KV_HEADS
IDX_HEADS = 4
IDX_DIM = 64
TOPK_MAX = 256
CONV_CH = 512
RWKV_HEADS = 8
RWKV_DIM = RWKV_HEADS * HEAD_DIM
LORA_W = 64
LORA_A = 64
LORA_G = 128
GN_EPS = 64e-5
CMP_BLOCK = 32
SEL_BLOCK = 64
N_SEL_BLOCKS = 8
WINDOW = 512
N_BUCKETS = 32
MAX_DISTANCE = 128
N_EXPERTS = 16
N_GROUPS = 4
EXPERTS_PER_GROUP = N_EXPERTS // N_GROUPS
PAGE_SIZE = 128
RMS_EPS = 1e-6
NEG = -1e30
FORCE = 1e9
TAKEN = -3e38
ATT_SCALE = HEAD_DIM ** -0.5
VMEM_LIMIT = 56 * 1024 * 1024
ROW_TILE = 256
MOE_TILE = 1024
SCAN_TIME_CHUNK = 32
SAMPLE_GROUP = 4


def _cparams(*sem):
    return pltpu.CompilerParams(dimension_semantics=sem, vmem_limit_bytes=VMEM_LIMIT)


def _pick_tile(rows, pref):
    t = min(pref, rows)
    while rows % t or (t % 8 and t != rows):
        t -= 1
    return t


def _pick_group(n, pref):
    g = min(pref, n)
    while n % g:
        g -= 1
    return g


def _const_spec(a):
    nd = a.ndim
    return pl.BlockSpec(a.shape, lambda *_: (0,) * nd)


def _dot(a, b):
    return jnp.dot(a.astype(BF16), b.astype(BF16), preferred_element_type=F32)


def _dot_nt(a, b):
    return lax.dot_general(a.astype(BF16), b.astype(BF16), (((1,), (1,)), ((), ())),
                           preferred_element_type=F32)


def _dot_split(x, m):
    hi = x.astype(BF16)
    r1 = x - hi.astype(F32)
    mid = r1.astype(BF16)
    lo = (r1 - mid.astype(F32)).astype(BF16)
    return (jnp.dot(hi, m, preferred_element_type=F32) + jnp.dot(mid, m, preferred_element_type=F32)
            + jnp.dot(lo, m, preferred_element_type=F32))


def _bf16_round(x):
    return x.astype(BF16).astype(F32)


def _sigmoid(x):
    return 1.0 / (1.0 + jnp.exp(-x))


def _silu(x):
    return x * _sigmoid(x)


def _modulate(x, g, shift, scale):
    y = x * lax.rsqrt(jnp.mean(x * x, axis=-1, keepdims=True) + RMS_EPS)
    return (y * g) * (1.0 + scale) + shift


def _group_rms(t, gmat, gain):
    ms = _dot_split(t * t, gmat)
    return (t * lax.rsqrt(ms + RMS_EPS)) * gain


def _ada_kernel(c_ref, w_ref, b_ref, o_ref):
    o_ref[...] = _dot(_silu(c_ref[...]), w_ref[...]) + b_ref[...]


def _ada(c, w_all, b, layer):
    r, d = c.shape
    n = w_all.shape[2]
    tn = 512
    return pl.pallas_call(
        _ada_kernel,
        grid=(n // tn,),
        in_specs=[pl.BlockSpec((r, d), lambda j: (0, 0)),
                  pl.BlockSpec((None, d, tn), lambda j: (layer, 0, j)),
                  pl.BlockSpec((1, tn), lambda j: (0, j))],
        out_specs=pl.BlockSpec((r, tn), lambda j: (0, j)),
        out_shape=jax.ShapeDtypeStruct((r, n), F32),
        compiler_params=_cparams("parallel"),
        name="ada_mod",
    )(c, w_all, b.reshape(1, n))


E_Q0, E_KV0, E_QI0, E_MISC0, E_BG0, E_CG0, E_XIN0, E_END = 0, 1024, 1280, 1792, 1920, 2432, 2944, 3456


def _inproj_even_kernel(x_ref, shift_ref, scale_ref, g_ref, w_ref, qg_ref, kg_ref, gone_ref, gtwo_ref,
                        q_ref, kv_ref, kvb_ref, qi_ref, misc_ref, miscb_ref, bg_ref, u_ref, kvt_ref, misct_ref):
    h = _modulate(x_ref[...], g_ref[...], shift_ref[...], scale_ref[...])
    z = jnp.dot(h.astype(BF16), w_ref[...], preferred_element_type=F32)
    gone = gone_ref[...]
    for t in range(N_ATT_HEADS):
        sl = slice(t * LANE, (t + 1) * LANE)
        q_ref[:, sl] = _group_rms(z[:, E_Q0 + t * LANE:E_Q0 + (t + 1) * LANE], gone, qg_ref[:, sl]).astype(BF16)
    k = _group_rms(z[:, E_KV0:E_KV0 + LANE], gtwo_ref[...], kg_ref[...])
    v = z[:, E_KV0 + LANE:E_KV0 + 2 * LANE]
    kv_ref[:, 0:LANE] = k
    kv_ref[:, LANE:2 * LANE] = v
    kvb_ref[:, 0:LANE] = k.astype(BF16)
    kvb_ref[:, LANE:2 * LANE] = v.astype(BF16)
    qi_ref[...] = z[:, E_QI0:E_MISC0].astype(BF16)
    misc = z[:, E_MISC0:E_BG0]
    misc_ref[...] = misc
    miscb_ref[...] = misc.astype(BF16)
    bg_ref[...] = z[:, E_BG0:E_CG0]
    u_ref[...] = z[:, E_CG0:E_XIN0] * z[:, E_XIN0:E_END]
    kvt_ref[0:LANE, :] = k.T
    kvt_ref[LANE:2 * LANE, :] = v.T
    misct_ref[...] = misc.T


def _row_call(kernel, xs, mods, consts, outs, tile, tpb, name, outs_t=(), outs_bt=()):
    if not isinstance(xs, (list, tuple)):
        xs = [xs]
    rows = xs[0].shape[0]
    n_tiles = rows // tile
    in_specs = [pl.BlockSpec((tile, x.shape[1]), lambda t: (t, 0)) for x in xs]
    for m in mods:
        in_specs.append(pl.BlockSpec((None,) + m.shape[1:], lambda t: (t // tpb, 0, 0)))
    in_specs += [_const_spec(c) for c in consts]
    out_specs = [pl.BlockSpec((tile, w), lambda t: (t, 0)) for (w, _) in outs]
    out_shape = [jax.ShapeDtypeStruct((rows, w), dt) for (w, dt) in outs]
    out_specs += [pl.BlockSpec((hh, tile), lambda t: (0, t)) for (hh, _) in outs_t]
    out_shape += [jax.ShapeDtypeStruct((hh, rows), dt) for (hh, dt) in outs_t]
    out_specs += [pl.BlockSpec((None, hh, tile), lambda t: (t // tpb, 0, t % tpb)) for (hh, _) in outs_bt]
    out_shape += [jax.ShapeDtypeStruct((n_tiles // tpb, hh, tile * tpb), dt) for (hh, dt) in outs_bt]
    return pl.pallas_call(kernel, grid=(n_tiles,), in_specs=in_specs, out_specs=out_specs, out_shape=out_shape,
                          compiler_params=_cparams("parallel"), name=name)(*xs, *mods, *consts)


def _pad_q_cols(wq):
    d = wq.shape[0]
    w = wq.reshape(d, N_ATT_HEADS, HEAD_DIM)
    z = jnp.zeros_like(w)
    lo = jnp.concatenate([w, z], -1)
    hi = jnp.concatenate([z, w], -1)
    sel = (jnp.arange(N_ATT_HEADS) >= GQA)[None, :, None]
    return jnp.where(sel, hi, lo).reshape(d, N_ATT_HEADS * LANE)


def _pad_o_rows(wo):
    d = wo.shape[1]
    w = wo.reshape(N_ATT_HEADS, HEAD_DIM, d)
    z = jnp.zeros_like(w)
    lo = jnp.concatenate([w, z], 1)
    hi = jnp.concatenate([z, w], 1)
    sel = (jnp.arange(N_ATT_HEADS) >= GQA)[:, None, None]
    return jnp.where(sel, hi, lo).reshape(N_ATT_HEADS * LANE, d)


def _gmats():
    i = jnp.arange(LANE)
    gone = jnp.full((LANE, LANE), 1.0 / HEAD_DIM, F32).astype(BF16)
    gtwo = jnp.where((i[:, None] // HEAD_DIM) == (i[None, :] // HEAD_DIM), 1.0 / HEAD_DIM, 0.0).astype(BF16)
    tri = jnp.where(i[:, None] <= i[None, :], 1.0, 0.0).astype(BF16)
    return gone, gtwo, tri


def _even_weights(w_in, w_out, q_norm, k_norm):
    d = w_in.shape[0]
    a_q, a_kv = N_ATT_HEADS * HEAD_DIM, 2 * N_KV_HEADS * HEAD_DIM
    o = 0
    wq = w_in[:, o:o + a_q]; o += a_q
    wkv = w_in[:, o:o + a_kv]; o += a_kv
    wqi = w_in[:, o:o + IDX_HEADS * IDX_DIM]; o += IDX_HEADS * IDX_DIM
    wki = w_in[:, o:o + IDX_DIM]; o += IDX_DIM
    wwi = w_in[:, o:o + IDX_HEADS]; o += IDX_HEADS
    wrest = w_in[:, o:]
    wqi = jnp.concatenate([wqi.reshape(d, IDX_HEADS, IDX_DIM), jnp.zeros((d, IDX_HEADS, LANE - IDX_DIM), F32)],
                          -1).reshape(d, IDX_HEADS * LANE)
    wmisc = jnp.concatenate([wki, wwi, jnp.zeros((d, LANE - IDX_DIM - IDX_HEADS), F32)], -1)
    w_in_p = jnp.concatenate([_pad_q_cols(wq), wkv, wqi, wmisc, wrest], -1).astype(BF16)
    w_out_p = jnp.concatenate([_pad_o_rows(w_out[:a_q]), w_out[a_q:]], 0).astype(BF16)
    qg = jnp.tile(q_norm, 2 * N_ATT_HEADS).reshape(1, N_ATT_HEADS * LANE)
    kg = jnp.tile(k_norm, 2).reshape(1, LANE)
    return w_in_p, w_out_p, qg, kg


def _inproj_even(x, mods, g, wts, gm, tile, tpb):
    w_in_p, _, qg, kg = wts
    gone, gtwo, _ = gm
    outs = [(N_ATT_HEADS * LANE, BF16), (2 * LANE, F32), (2 * LANE, BF16), (IDX_HEADS * LANE, BF16),
            (LANE, F32), (LANE, BF16), (CONV_CH, F32), (CONV_CH, F32)]
    return _row_call(_inproj_even_kernel, x, mods, [g, w_in_p, qg, kg, gone, gtwo], outs, tile, tpb, "inproj_even",
                     outs_bt=[(2 * LANE, F32), (LANE, F32)])


def _t5_bucket(dist):
    dist = jnp.maximum(dist, 0)
    exact = N_BUCKETS // 2
    far = exact + (jnp.log(jnp.maximum(dist, 1).astype(F32) / exact)
                   / math.log(MAX_DISTANCE / exact) * (N_BUCKETS - exact)).astype(I32)
    return jnp.where(dist < exact, dist, jnp.minimum(far, N_BUCKETS - 1))


def _bias_lookup(rel_bias, dist):
    onehot = (_t5_bucket(dist)[..., None] == jnp.arange(N_BUCKETS)).astype(F32)
    return jnp.einsum("...k,kh->...h", onehot, rel_bias, precision=lax.Precision.HIGHEST)


def _bias_row(rel_bias, t_pos):
    return _bias_lookup(rel_bias, t_pos - jnp.arange(t_pos + LANE)).T


def _bias_tiles(rel_bias, qb):
    r = jnp.arange(qb)[:, None]
    c = jnp.arange(LANE)[None, :]
    tiles = [_bias_lookup(rel_bias, d * LANE + r - c) for d in range(3)]
    return jnp.stack(tiles).transpose(0, 3, 1, 2)


def _stack_heads(q_ref, hk):
    return jnp.concatenate([q_ref[:, (hk * GQA + g) * LANE:(hk * GQA + g + 1) * LANE] for g in range(GQA)], axis=0)


QB = LANE
QW = GQA * QB


def _sub_sum(x):
    return jnp.sum(x, axis=0, keepdims=True)


def _flash_t_pair(blocks, qs, bias_ref, carry, acc_ref):
    logits = [[jnp.where(mk[hk], _dot_nt(kb, qs[hk]) + bias_ref[dsel, hk], NEG) for (kb, _, mk, dsel) in blocks]
              for hk in range(N_KV_HEADS)]
    out = []
    for hk in range(N_KV_HEADS):
        m, l = carry[hk]
        m_new = m
        for s in logits[hk]:
            m_new = jnp.maximum(m_new, jnp.max(s, axis=0, keepdims=True))
        alpha = jnp.exp(m - m_new)
        l = alpha * l
        pv = None
        for s, (_, vt, _, _) in zip(logits[hk], blocks):
            p = jnp.exp(s - m_new)
            l = l + _sub_sum(p)
            d = jnp.dot(vt, p.astype(BF16), preferred_element_type=F32)
            pv = d if pv is None else pv + d
        acc_ref[hk] = alpha * acc_ref[hk] + pv
        out.append((m_new, l))
    return tuple(out)


def _scaled_queries(q_ref, hk):
    return (_stack_heads(q_ref, hk).astype(F32) * ATT_SCALE).astype(BF16)


def _pair_loop(nblk, body, init):
    def body2(jj, c):
        return body(2 * jj + 1, body(2 * jj, c))
    return lax.fori_loop(0, (nblk + 1) // 2, body2, init)


def _flash_t_init():
    return (jnp.full((1, QW), NEG, F32), jnp.zeros((1, QW), F32))


def _tile_lanes(x):
    return jnp.concatenate([x] * GQA, axis=1)


def _write_heads_t(o_ref, o_ts):
    lane = lax.broadcasted_iota(I32, (QB, LANE), 1)
    for hk in range(N_KV_HEADS):
        valid = (lane // HEAD_DIM) == hk
        for g in range(GQA):
            h = hk * GQA + g
            o = o_ts[hk][:, g * QB:(g + 1) * QB].T
            o_ref[:, h * LANE:(h + 1) * LANE] = jnp.where(valid, o, 0.0).astype(BF16)


def _dsa_t_kernel(q_ref, qi_ref, misc_ref, kidx_ref, k_ref, vt_ref, bias_ref, trit_ref, o_ref, key_s, acc_s,
                  *, n_keep):
    i = pl.program_id(1)
    q0 = i * QB
    nblk = i + 1
    krow = lax.broadcasted_iota(I32, (LANE, QB), 0)
    qcol = lax.broadcasted_iota(I32, (LANE, QB), 1)
    misc_t = misc_ref[...].T
    wis = [_bf16_round(misc_t[IDX_DIM + h:IDX_DIM + h + 1, :]) for h in range(IDX_HEADS)]
    qi = qi_ref[...]
    idx_scale = (IDX_HEADS * IDX_DIM) ** -0.5

    def causal(j):
        return j * LANE + krow <= q0 + qcol

    def pass_a(j, c):
        kb = kidx_ref[pl.ds(pl.multiple_of(j * LANE, LANE), LANE), :]
        acc = jnp.zeros((LANE, QB), F32)
        for h in range(IDX_HEADS):
            acc = acc + _bf16_round(jnp.maximum(_dot_nt(kb, qi[:, h * LANE:(h + 1) * LANE]), 0.0)) * wis[h]
        key_s[j] = _order_keys(jnp.where(causal(j), acc * idx_scale, NEG))
        return c

    _pair_loop(nblk, pass_a, 0)

    def count(pred):
        def body(j, a):
            return a + jnp.where(pred(key_s[j]), 1.0, 0.0)
        return _sub_sum(_pair_loop(nblk, body, jnp.zeros((LANE, QB), F32)))

    keep = float(n_keep)
    thr = jnp.where(count(lambda k: k >= 0) >= keep, jnp.int32(0), jnp.int32(-2 ** 31))

    def search(it, thr):
        cand = thr | lax.shift_left(jnp.int32(1), jnp.int32(30) - it)
        return jnp.where(count(lambda k: k >= cand) >= keep, cand, thr)

    thr = lax.fori_loop(0, 31, search, thr)
    need = keep - count(lambda k: k > thr)
    trit = trit_ref[...]

    def pass_c(j, run):
        key = key_s[j]
        eq = key == thr
        eqf = jnp.where(eq, 1.0, 0.0)
        cum = jnp.dot(trit, eqf.astype(BF16), preferred_element_type=F32) + run
        sel = ((key > thr) | (eq & (cum <= need))) & causal(j)
        key_s[j] = jnp.where(sel, 1, 0)
        return run + _sub_sum(eqf)

    _pair_loop(nblk, pass_c, jnp.zeros((1, QB), F32))

    qs = [_scaled_queries(q_ref, hk) for hk in range(N_KV_HEADS)]
    acc_s[...] = jnp.zeros_like(acc_s)

    def pass_d(jj, carry):
        blocks = []
        for j in (2 * jj, 2 * jj + 1):
            kb = k_ref[pl.ds(pl.multiple_of(j * LANE, LANE), LANE), :]
            blocks.append((kb, vt_ref[j], [_tile_lanes(key_s[j] > 0)] * N_KV_HEADS, jnp.clip(i - j, 0, 2)))
        return _flash_t_pair(blocks, qs, bias_ref, carry, acc_s)

    res = lax.fori_loop(0, (nblk + 1) // 2, pass_d, tuple(_flash_t_init() for _ in range(N_KV_HEADS)))
    _write_heads_t(o_ref, [acc_s[hk] / res[hk][1] for hk in range(N_KV_HEADS)])


def _bias_tiles_t(rel_bias):
    t = _bias_tiles(rel_bias, QB)
    t = t.reshape(3, N_KV_HEADS, GQA, QB, LANE).transpose(0, 1, 4, 2, 3)
    return t.reshape(3, N_KV_HEADS, LANE, QW)


def _blocks_t(x):
    b, s, w = x.shape
    return x.reshape(b, s // LANE, LANE, w).transpose(0, 1, 3, 2)


def _dsa_t(q, qi, misc, kidx_b, kv_b, bias_t, trit, *, n_keep):
    b, s, _ = q.shape
    vt = _blocks_t(kv_b[:, :, LANE:])
    qspec = lambda w: pl.BlockSpec((None, QB, w), lambda bi, i: (bi, i, 0))
    kspec = pl.BlockSpec((None, s, LANE), lambda bi, i: (bi, 0, 0))
    return pl.pallas_call(
        functools.partial(_dsa_t_kernel, n_keep=n_keep),
        grid=(b, s // QB),
        in_specs=[qspec(N_ATT_HEADS * LANE), qspec(IDX_HEADS * LANE), qspec(LANE), kspec, kspec,
                  pl.BlockSpec((None, s // LANE, LANE, LANE), lambda bi, i: (bi, 0, 0, 0)),
                  _const_spec(bias_t), _const_spec(trit)],
        out_specs=qspec(N_ATT_HEADS * LANE),
        out_shape=jax.ShapeDtypeStruct((b, s, N_ATT_HEADS * LANE), BF16),
        scratch_shapes=[pltpu.VMEM((s // LANE, LANE, QB), I32), pltpu.VMEM((N_KV_HEADS, LANE, QW), F32)],
        compiler_params=_cparams("parallel", "parallel"),
        name="dsa_attention_t",
    )(q, qi, misc, kidx_b, kv_b, vt, bias_t, trit)


def _dot_split_rhs(m, x):
    hi = x.astype(BF16)
    r1 = x - hi.astype(F32)
    mid = r1.astype(BF16)
    lo = (r1 - mid.astype(F32)).astype(BF16)
    return (jnp.dot(m, hi, preferred_element_type=F32) + jnp.dot(m, mid, preferred_element_type=F32)
            + jnp.dot(m, lo, preferred_element_type=F32))


def _nsa_t_kernel(q_ref, gates_ref, kc_ref, vct_ref, ks_ref, vst_ref, kw_ref, vwt_ref, bias_ref, biasc_ref,
                  pairt_ref, o_ref, acc_s, *, n_cmp, n_sel):
    i = pl.program_id(1)
    q0 = i * QB
    nblk = i + 1
    heads = range(N_KV_HEADS)
    krow = lax.broadcasted_iota(I32, (LANE, QB), 0)
    qcol = lax.broadcasted_iota(I32, (LANE, QB), 1)
    t_pos = q0 + qcol
    krow_f = krow.astype(F32)
    qs = [_scaled_queries(q_ref, hk) for hk in heads]
    gates_t = gates_ref[...].T

    def gate_row(br, hk):
        return jnp.concatenate([gates_t[br * N_ATT_HEADS + hk * GQA + g:br * N_ATT_HEADS + hk * GQA + g + 1, :]
                                for g in range(GQA)], axis=1)

    mask_c = _tile_lanes(((krow * CMP_BLOCK + CMP_BLOCK - 1) <= t_pos) & (krow < n_cmp))
    cur = t_pos // SEL_BLOCK
    forced = (krow == 0) | (krow == cur) | (krow == cur - 1)
    sel_causal = krow * SEL_BLOCK <= t_pos
    o_cmp, picked = [], []
    for hk in heads:
        s = jnp.where(mask_c, _dot_nt(kc_ref[...], qs[hk]) + biasc_ref[hk], NEG)
        e = jnp.exp(s - jnp.max(s, axis=0, keepdims=True))
        p = jnp.where(mask_c, e / _sub_sum(e), 0.0)
        o_cmp.append(jnp.dot(vct_ref[...], p.astype(BF16), preferred_element_type=F32))
        ps = p[:, 0:QB]
        for g in range(1, GQA):
            ps = ps + p[:, g * QB:(g + 1) * QB]
        score = _dot_split_rhs(pairt_ref[...], ps)
        score = jnp.where(sel_causal, jnp.where(forced, FORCE, score), NEG)
        pk = jnp.zeros((LANE, QB), F32)
        for _ in range(n_sel):
            mx = jnp.max(score, axis=0, keepdims=True)
            first = jnp.min(jnp.where(score == mx, krow_f, float(LANE)), axis=0, keepdims=True)
            hit = krow_f == first
            pk = jnp.where(hit, 1.0, pk)
            score = jnp.where(hit, TAKEN, score)
        picked.append(pk.astype(BF16))

    def key_block(k_ref, vt_ref, j):
        return k_ref[pl.ds(pl.multiple_of(j * LANE, LANE), LANE), :], vt_ref[j]

    erow = lax.broadcasted_iota(I32, (LANE, LANE), 0)
    ecol = lax.broadcasted_iota(I32, (LANE, LANE), 1)
    acc_s[...] = jnp.zeros_like(acc_s)

    def slc_body(jj, carry):
        blocks = []
        for j in (2 * jj, 2 * jj + 1):
            kb, vt = key_block(ks_ref, vst_ref, j)
            expand = jnp.where(ecol == 2 * j + erow // SEL_BLOCK, 1.0, 0.0).astype(BF16)
            causal = j * LANE + krow <= t_pos
            masks = [_tile_lanes((jnp.dot(expand, picked[hk], preferred_element_type=F32) > 0.5) & causal)
                     for hk in heads]
            blocks.append((kb, vt, masks, jnp.clip(i - j, 0, 2)))
        return _flash_t_pair(blocks, qs, bias_ref, carry, acc_s.at[0])

    res_s = lax.fori_loop(0, (nblk + 1) // 2, slc_body, tuple(_flash_t_init() for _ in heads))

    lo = jnp.maximum(i - WINDOW // LANE - 1, 0)

    def win_body(jj, carry):
        blocks = []
        for j in (lo + 2 * jj, lo + 2 * jj + 1):
            kb, vt = key_block(kw_ref, vwt_ref, j)
            dist = t_pos - (j * LANE + krow)
            mask = _tile_lanes((dist >= 0) & (dist < WINDOW))
            blocks.append((kb, vt, [mask] * N_KV_HEADS, jnp.clip(i - j, 0, 2)))
        return _flash_t_pair(blocks, qs, bias_ref, carry, acc_s.at[1])

    res_w = lax.fori_loop(0, (i - lo + 2) // 2, win_body, tuple(_flash_t_init() for _ in heads))

    _write_heads_t(o_ref, [gate_row(0, hk) * o_cmp[hk] + gate_row(1, hk) * (acc_s[0, hk] / res_s[hk][1])
                           + gate_row(2, hk) * (acc_s[1, hk] / res_w[hk][1]) for hk in heads])


def _bias_cmp_t(rel_bias, s):
    t = _bias_cmp(rel_bias, [j * QB for j in range(s // QB)], QB)
    t = t.reshape(s // QB, N_KV_HEADS, GQA, QB, LANE).transpose(0, 1, 4, 2, 3)
    return t.reshape(s // QB, N_KV_HEADS, LANE, QW)


def _nsa_t(q, gates, kcv, kvs_b, kvw_b, bias_t, bias_c, pair_t, *, n_cmp, n_sel):
    b, s, _ = q.shape
    vct = kcv[:, :, LANE:].transpose(0, 2, 1)
    qspec = lambda w: pl.BlockSpec((None, QB, w), lambda bi, i: (bi, i, 0))
    kspec = pl.BlockSpec((None, s, LANE), lambda bi, i: (bi, 0, 0))
    vspec = pl.BlockSpec((None, s // LANE, LANE, LANE), lambda bi, i: (bi, 0, 0, 0))
    cspec = pl.BlockSpec((None, LANE, LANE), lambda bi, i: (bi, 0, 0))
    return pl.pallas_call(
        functools.partial(_nsa_t_kernel, n_cmp=n_cmp, n_sel=n_sel),
        grid=(b, s // QB),
        in_specs=[qspec(N_ATT_HEADS * LANE), qspec(LANE), cspec, cspec, kspec, vspec, kspec, vspec,
                  _const_spec(bias_t), pl.BlockSpec((None,) + bias_c.shape[1:], lambda bi, i: (i, 0, 0, 0)),
                  _const_spec(pair_t)],
        out_specs=qspec(N_ATT_HEADS * LANE),
        out_shape=jax.ShapeDtypeStruct((b, s, N_ATT_HEADS * LANE), BF16),
        scratch_shapes=[pltpu.VMEM((2, N_KV_HEADS, LANE, QW), F32)],
        compiler_params=_cparams("parallel", "parallel"),
        name="nsa_attention_t",
    )(q, gates, kcv, vct, kvs_b, _blocks_t(kvs_b[:, :, LANE:]), kvw_b, _blocks_t(kvw_b[:, :, LANE:]),
      bias_t, bias_c, pair_t)


def _select_top(keys, n_keep, tri):
    keep = float(n_keep)
    n = keys[0].shape[1]

    def count(pred):
        return [jnp.sum(jnp.where(pred(g, k), 1.0, 0.0), axis=1, keepdims=True) for g, k in enumerate(keys)]

    int_min = jnp.int32(-2 ** 31)
    thr = tuple(jnp.where(c >= keep, jnp.int32(0), int_min) for c in count(lambda g, k: k >= 0))

    def search(it, thr):
        bit = lax.shift_left(jnp.int32(1), jnp.int32(30) - it)
        cand = [t | bit for t in thr]
        cnt = count(lambda g, k: k >= cand[g])
        return tuple(jnp.where(c >= keep, cd, t) for c, cd, t in zip(cnt, cand, thr))

    thr = lax.fori_loop(0, 31, search, thr)
    need = [keep - c for c in count(lambda g, k: k > thr[g])]
    sels = []
    for g, k in enumerate(keys):
        run = jnp.zeros((1, 1), F32)
        parts = []
        for t in range(n // LANE):
            kt = k[:, t * LANE:(t + 1) * LANE]
            eq = kt == thr[g]
            eqf = jnp.where(eq, 1.0, 0.0)
            cum = jnp.dot(eqf.astype(BF16), tri, preferred_element_type=F32) + run
            parts.append((kt > thr[g]) | (eq & (cum <= need[g])))
            run = run + jnp.sum(eqf, axis=1, keepdims=True)
        sels.append(jnp.concatenate(parts, axis=1))
    return sels


def _order_keys(score):
    score = jnp.where(score == 0.0, 0.0, score)
    bits = lax.bitcast_convert_type(score, I32)
    return jnp.where(bits < 0, bits ^ jnp.int32(0x7FFFFFFF), bits)


def _head_rows_out(o_ref, g, acc):
    rowh = lax.broadcasted_iota(I32, (N_ATT_HEADS, LANE), 0)
    laneh = lax.broadcasted_iota(I32, (N_ATT_HEADS, LANE), 1)
    o_ref[g] = jnp.where((laneh // HEAD_DIM) == (rowh // GQA), acc, 0.0).astype(BF16)


def _dsa_sample_kernel(pt_ref, q_ref, qi_ref, wi_ref, knew_ref, kvnew_ref, bias_ref, tri_ref, *refs,
                       grp, n_pages, n_keep):
    del pt_ref
    ki_refs, kv_refs, o_ref = refs[:grp * n_pages], refs[grp * n_pages:2 * grp * n_pages], refs[-1]
    lane1 = lax.broadcasted_iota(I32, (1, LANE), 1)
    idx_scale = (IDX_HEADS * IDX_DIM) ** -0.5
    keys = []
    for g in range(grp):
        qi = qi_ref[g]
        wi = _bf16_round(wi_ref[g])
        tiles = []
        for p in range(n_pages):
            rel = _bf16_round(jnp.maximum(_dot(qi, ki_refs[g * n_pages + p][...]), 0.0))
            tiles.append(jnp.sum(rel * wi, axis=0, keepdims=True) * idx_scale)
        rel_new = _bf16_round(jnp.maximum(jnp.sum(qi.astype(F32) * _bf16_round(knew_ref[g]), axis=1, keepdims=True),
                                          0.0))
        sc_new = jnp.sum(rel_new * wi, axis=0, keepdims=True) * idx_scale
        tiles.append(jnp.where(lane1 == 0, sc_new, NEG))
        keys.append(_order_keys(jnp.concatenate(tiles, axis=1)))
    sels = _select_top(keys, n_keep, tri_ref[...])
    bias = bias_ref[...]
    for g in range(grp):
        q = q_ref[g]
        kvnew = _bf16_round(kvnew_ref[g])
        tiles = [_dot(q, kv_refs[g * n_pages + p][0:LANE, :]) for p in range(n_pages)]
        s_new = jnp.sum(q.astype(F32) * kvnew[:, 0:LANE], axis=1, keepdims=True)
        tiles.append(jnp.where(lane1 == 0, s_new, 0.0))
        valid = sels[g] & (jnp.concatenate([lane1] * n_pages + [lane1 + LANE], axis=1) <= LANE)
        s = jnp.where(valid, jnp.concatenate(tiles, axis=1) * ATT_SCALE + bias, NEG)
        e = jnp.exp(s - jnp.max(s, axis=1, keepdims=True))
        p_all = jnp.where(valid, e / jnp.sum(e, axis=1, keepdims=True), 0.0)
        acc = p_all[:, n_pages * LANE:n_pages * LANE + 1] * kvnew[:, LANE:2 * LANE]
        for p in range(n_pages):
            acc = acc + _dot_nt(p_all[:, p * LANE:(p + 1) * LANE], kv_refs[g * n_pages + p][LANE:2 * LANE, :])
        _head_rows_out(o_ref, g, acc)


def _page_specs(pool_t, n_pages, grp):
    r, c = pool_t.shape[1:]
    return [pl.BlockSpec((None, r, c), lambda i, pt, g=g, p=p: (pt[i * grp + g, p], 0, 0))
            for g in range(grp) for p in range(n_pages)]


def _dsa_sample(q8, qi8, wi8, knew, kvnew, bias, tri, ki_t, kv_t, page_table, *, n_keep, grp):
    b, n_pages = page_table.shape
    gspec = lambda a: pl.BlockSpec((grp,) + a.shape[1:], lambda i, pt: (i,) + (0,) * (a.ndim - 1))
    cspec = lambda a: pl.BlockSpec(a.shape, lambda i, pt: (0,) * a.ndim)
    kern = functools.partial(_dsa_sample_kernel, grp=grp, n_pages=n_pages, n_keep=n_keep)
    return pl.pallas_call(
        kern,
        grid_spec=pltpu.PrefetchScalarGridSpec(
            num_scalar_prefetch=1, grid=(b // grp,),
            in_specs=[gspec(q8), gspec(qi8), gspec(wi8), gspec(knew), gspec(kvnew), cspec(bias), cspec(tri)]
            + _page_specs(ki_t, n_pages, grp) + _page_specs(kv_t, n_pages, grp),
            out_specs=pl.BlockSpec((grp, N_ATT_HEADS, LANE), lambda i, pt: (i, 0, 0))),
        out_shape=jax.ShapeDtypeStruct((b, N_ATT_HEADS, LANE), BF16),
        compiler_params=_cparams("parallel"),
        name="dsa_sample",
    )(page_table, q8, qi8, wi8, knew, kvnew, bias, tri, *([ki_t] * (grp * n_pages)), *([kv_t] * (grp * n_pages)))


def _route(hf, wrt, br):
    logits = _dot_nt(wrt, hf)
    s = _sigmoid(logits)
    sel = s + br
    rows = [sel[e:e + 1, :] for e in range(N_EXPERTS)]
    grp = []
    for g in range(N_GROUPS):
        a = rows[g * EXPERTS_PER_GROUP:(g + 1) * EXPERTS_PER_GROUP]
        best = None
        for i in range(EXPERTS_PER_GROUP):
            for j in range(i + 1, EXPERTS_PER_GROUP):
                v = a[i] + a[j]
                best = v if best is None else jnp.maximum(best, v)
        grp.append(best)
    gbest = jnp.zeros_like(grp[0], dtype=I32)
    cur = grp[0]
    for g in range(1, N_GROUPS):
        better = grp[g] > cur
        gbest = jnp.where(better, g, gbest)
        cur = jnp.where(better, grp[g], cur)
    picked = []
    for g in range(N_GROUPS):
        a = rows[g * EXPERTS_PER_GROUP:(g + 1) * EXPERTS_PER_GROUP]
        for j in range(EXPERTS_PER_GROUP):
            rank = jnp.zeros_like(a[j])
            for jj in range(EXPERTS_PER_GROUP):
                if jj != j:
                    ahead = (a[jj] > a[j]) | (a[jj] == a[j]) if jj < j else (a[jj] > a[j])
                    rank = rank + jnp.where(ahead, 1.0, 0.0)
            e = g * EXPERTS_PER_GROUP + j
            picked.append(jnp.where((gbest == g) & (rank < 2.0), s[e:e + 1, :], 0.0))
    den = picked[0]
    for p in picked[1:]:
        den = den + p
    return jnp.concatenate([p / den for p in picked], axis=0)


def _post_tail(x, mix, gate, gf, shf, scf, wrt_ref, br_ref, x2_ref, hf_ref, cwt_ref):
    x2 = x + gate * mix
    x2_ref[...] = x2
    hf = _modulate(x2, gf, shf, scf)
    hf_ref[...] = hf.astype(BF16)
    cwt_ref[...] = _route(hf, wrt_ref[...], br_ref[...])


def _post_even_kernel(x_ref, oa_ref, bg_ref, u_ref, um1_ref, um2_ref, gate_ref, shf_ref, scf_ref,
                      gf_ref, cw_ref, cb_ref, wo_ref, wrt_ref, br_ref, x2_ref, hf_ref, cwt_ref):
    cw = cw_ref[...]
    y = cb_ref[...] + cw[0:1, :] * um2_ref[...]
    y = y + cw[1:2, :] * um1_ref[...]
    y = y + cw[2:3, :] * u_ref[...]
    n_a = N_ATT_HEADS * LANE
    mix = (jnp.dot(oa_ref[...], wo_ref[0:n_a, :], preferred_element_type=F32)
           + jnp.dot((bg_ref[...] * y).astype(BF16), wo_ref[n_a:n_a + CONV_CH, :], preferred_element_type=F32))
    _post_tail(x_ref[...], mix, gate_ref[...], gf_ref[...], shf_ref[...], scf_ref[...], wrt_ref, br_ref,
               x2_ref, hf_ref, cwt_ref)


POST_OUTS = [(1024, F32), (1024, BF16)]


def _moe_kernel(hf_ref, cw_ref, x2_ref, gate_ref, wg_ref, wu_ref, wd_ref, o_ref, acc_ref):
    e = pl.program_id(1)

    @pl.when(e == 0)
    def _():
        acc_ref[...] = jnp.zeros_like(acc_ref)

    hf = hf_ref[...]
    hmid = _silu(_dot(hf, wg_ref[...])) * _dot(hf, wu_ref[...])
    cw = cw_ref[...]
    lane = lax.broadcasted_iota(I32, cw.shape, 1)
    wcol = jnp.sum(jnp.where(lane == e, cw, 0.0), axis=1, keepdims=True)
    acc_ref[...] += _dot(hmid, wd_ref[...]) * wcol

    @pl.when(e == N_EXPERTS - 1)
    def _():
        o_ref[...] = x2_ref[...] + gate_ref[...] * acc_ref[...]


def _moe(hf, cw, x2, gate, wg, wu, wd, layer, tile, tpb):
    rows, d = x2.shape
    de = wg.shape[3]
    return pl.pallas_call(
        _moe_kernel,
        grid=(rows // tile, N_EXPERTS),
        in_specs=[pl.BlockSpec((tile, d), lambda t, e: (t, 0)),
                  pl.BlockSpec((tile, N_EXPERTS), lambda t, e: (t, 0)),
                  pl.BlockSpec((tile, d), lambda t, e: (t, 0)),
                  pl.BlockSpec((None,) + gate.shape[1:], lambda t, e: (t // tpb, 0, 0)),
                  pl.BlockSpec((None, None, d, de), lambda t, e: (layer, e, 0, 0)),
                  pl.BlockSpec((None, None, d, de), lambda t, e: (layer, e, 0, 0)),
                  pl.BlockSpec((None, None, de, d), lambda t, e: (layer, e, 0, 0))],
        out_specs=pl.BlockSpec((tile, d), lambda t, e: (t, 0)),
        out_shape=jax.ShapeDtypeStruct((rows, d), F32),
        scratch_shapes=[pltpu.VMEM((tile, d), F32)],
        compiler_params=_cparams("parallel", "arbitrary"),
        name="moe_dense",
    )(hf, cw, x2, gate, wg, wu, wd)


MOE_WIN = LANE


def _moe_sorted_kernel(plan_ref, hf_ref, cw_ref, cwt_ref, x2_ref, gate_ref, tril_ref, g16_ref, g16t_ref,
                       wg_ref, wu_ref, wd_ref, o_ref, p_s, pt_s, xs_s, cws_s, ys_s):
    ti = pl.program_id(0)
    e = pl.program_id(1)
    t = hf_ref.shape[0]
    g = e // EXPERTS_PER_GROUP

    @pl.when(e == 0)
    def _():
        tril = tril_ref[...]
        memb_col = jnp.dot(jnp.where(cw_ref[...] > 0.0, 1.0, 0.0).astype(BF16), g16_ref[...],
                           preferred_element_type=F32) > 0.5
        memb_row = jnp.dot(g16t_ref[...], jnp.where(cwt_ref[...] > 0.0, 1.0, 0.0).astype(BF16),
                           preferred_element_type=F32) > 0.5
        mcf = jnp.where(memb_col, 1.0, 0.0)
        mrf = jnp.where(memb_row, 1.0, 0.0)
        rank_col = jnp.dot(tril, mcf.astype(BF16), preferred_element_type=F32)
        rank_row = _dot_nt(mrf, tril)
        lane = lax.broadcasted_iota(I32, (t, LANE), 1)
        row8 = lax.broadcasted_iota(I32, (8, t), 0)
        base_col = jnp.zeros((t, LANE), F32)
        base_row = jnp.zeros((8, t), F32)
        for gg in range(N_GROUPS):
            start = plan_ref[ti, gg].astype(F32)
            base_col = jnp.where(lane == gg, start, base_col)
            base_row = jnp.where(row8 == gg, start, base_row)
        slot_col = jnp.sum(mcf * (base_col + rank_col - 1.0), axis=1, keepdims=True)
        slot_row = jnp.sum(mrf * (base_row + rank_row - 1.0), axis=0, keepdims=True)
        col_iota = lax.broadcasted_iota(I32, (MOE_WIN, t), 1).astype(F32)
        row_iota = lax.broadcasted_iota(I32, (MOE_WIN, t), 0).astype(F32)
        for c in range(t // MOE_WIN):
            rows = slice(c * MOE_WIN, (c + 1) * MOE_WIN)
            p_s[rows, :] = jnp.where(slot_row == row_iota + float(c * MOE_WIN), 1.0, 0.0).astype(BF16)
            pt_s[rows, :] = jnp.where(slot_col[rows] == col_iota, 1.0, 0.0).astype(BF16)
        p = p_s[...]
        xs_s[...] = jnp.dot(p, hf_ref[...], preferred_element_type=F32).astype(BF16)
        cws_s[...] = _dot_split_rhs(p, cw_ref[...])
        ys_s[...] = jnp.zeros_like(ys_s)

    lane16 = lax.broadcasted_iota(I32, (MOE_WIN, N_EXPERTS), 1)

    def window(c, carry):
        rows = pl.ds(pl.multiple_of(c * MOE_WIN, MOE_WIN), MOE_WIN)
        xw = xs_s[rows, :]
        hmid = _silu(_dot(xw, wg_ref[...])) * _dot(xw, wu_ref[...])
        wcol = jnp.sum(jnp.where(lane16 == e, cws_s[rows, :], 0.0), axis=1, keepdims=True)
        ys_s[rows, :] = ys_s[rows, :] + _dot(hmid, wd_ref[...]) * wcol
        return carry

    lax.fori_loop(plan_ref[ti, N_GROUPS + g], plan_ref[ti, 2 * N_GROUPS + g], window, 0)

    @pl.when(e == N_EXPERTS - 1)
    def _():
        ys = ys_s[...]
        hi = ys.astype(BF16)
        lo = (ys - hi.astype(F32)).astype(BF16)
        pt = pt_s[...]
        back = jnp.dot(pt, hi, preferred_element_type=F32) + jnp.dot(pt, lo, preferred_element_type=F32)
        o_ref[...] = x2_ref[...] + gate_ref[...] * back


def _moe_plan(cwt, tile):
    n = cwt.shape[1]
    member = (cwt.reshape(N_GROUPS, EXPERTS_PER_GROUP, n // tile, tile) > 0.0).any(axis=1)
    cnt = member.sum(axis=-1).astype(I32).T
    start = jnp.cumsum(cnt, axis=1) - cnt
    lo = start // MOE_WIN
    hi = jnp.where(cnt > 0, (start + cnt + MOE_WIN - 1) // MOE_WIN, lo)
    return jnp.concatenate([start, lo, hi], axis=1)


def _moe_sorted(hf, cwt, x2, gate, wg, wu, wd, layer, tile, tpb):
    rows, d = x2.shape
    de = wg.shape[3]
    i = jnp.arange(tile)
    tril = jnp.where(i[None, :] <= i[:, None], 1.0, 0.0).astype(BF16)
    e16 = jnp.arange(N_EXPERTS)
    g16 = jnp.where(e16[:, None] // EXPERTS_PER_GROUP == jnp.arange(LANE)[None, :], 1.0, 0.0).astype(BF16)
    g16t = jnp.where(jnp.arange(8)[:, None] == e16[None, :] // EXPERTS_PER_GROUP, 1.0, 0.0).astype(BF16)
    cspec = lambda a: pl.BlockSpec(a.shape, lambda t, e, plan: (0,) * a.ndim)
    return pl.pallas_call(
        _moe_sorted_kernel,
        grid_spec=pltpu.PrefetchScalarGridSpec(
            num_scalar_prefetch=1, grid=(rows // tile, N_EXPERTS),
            in_specs=[pl.BlockSpec((tile, d), lambda t, e, plan: (t, 0)),
                      pl.BlockSpec((tile, N_EXPERTS), lambda t, e, plan: (t, 0)),
                      pl.BlockSpec((N_EXPERTS, tile), lambda t, e, plan: (0, t)),
                      pl.BlockSpec((tile, d), lambda t, e, plan: (t, 0)),
                      pl.BlockSpec((None,) + gate.shape[1:], lambda t, e, plan: (t // tpb, 0, 0)),
                      cspec(tril), cspec(g16), cspec(g16t),
                      pl.BlockSpec((None, None, d, de), lambda t, e, plan: (layer, e, 0, 0)),
                      pl.BlockSpec((None, None, d, de), lambda t, e, plan: (layer, e, 0, 0)),
                      pl.BlockSpec((None, None, de, d), lambda t, e, plan: (layer, e, 0, 0))],
            out_specs=pl.BlockSpec((tile, d), lambda t, e, plan: (t, 0)),
            scratch_shapes=[pltpu.VMEM((tile, tile), BF16), pltpu.VMEM((tile, tile), BF16),
                            pltpu.VMEM((tile, d), BF16), pltpu.VMEM((tile, N_EXPERTS), F32),
                            pltpu.VMEM((tile, d), F32)]),
        out_shape=jax.ShapeDtypeStruct((rows, d), F32),
        compiler_params=_cparams("parallel", "arbitrary"),
        name="moe_sorted",
    )(_moe_plan(cwt, tile), hf, cwt.T, cwt, x2, gate, tril, g16, g16t, wg, wu, wd)


def _cache_rows(kv_t):
    b, _, s = kv_t.shape
    return kv_t.reshape(b, 2, N_KV_HEADS, HEAD_DIM, s).transpose(0, 4, 1, 2, 3)


def _shift_rows(u3, k):
    return jnp.pad(u3, ((0, 0), (k, 0), (0, 0)))[:, :u3.shape[1]]


def _even_layer(xp, xs, mp, ms, page_table, kv_pool, kidx_pool, conv_buf, rel_bias, router, experts,
                g_mix, g_ffn, w_in, w_out, q_norm, k_norm, conv_w, conv_b):
    B, S, D = xp.shape
    Bs = xs.shape[0]
    past = page_table.shape[1] * PAGE_SIZE
    wts = _even_weights(w_in, w_out, q_norm, k_norm)
    gm = _gmats()
    wrt, br = router
    wg, wu, wd, lyr = experts
    g_mix = g_mix.reshape(1, D)
    g_ffn = g_ffn.reshape(1, D)
    post_consts = [g_ffn, conv_w, conv_b.reshape(1, CONV_CH), wts[1], wrt, br]

    tp = ROW_TILE
    q_p, _, kv_b, qi_p, misc, misc_b, bg, u, kv_t, misc_t = _inproj_even(xp.reshape(B * S, D), [mp[0], mp[1]],
                                                                        g_mix, wts, gm, tp, S // tp)
    r3 = lambda a: a.reshape(B, S, a.shape[-1])
    oa = _dsa_t(r3(q_p), r3(qi_p), r3(misc), r3(misc_b), r3(kv_b), _bias_tiles_t(rel_bias), gm[2].T,
                n_keep=min(TOPK_MAX, S // 4))
    u3 = r3(u)
    um1 = _shift_rows(u3, 1).reshape(B * S, CONV_CH)
    um2 = _shift_rows(u3, 2).reshape(B * S, CONV_CH)
    x2, hf, cwt = _row_call(_post_even_kernel, [xp.reshape(B * S, D), oa.reshape(B * S, -1), bg, u, um1, um2],
                            [mp[2], mp[3], mp[4]], post_consts, POST_OUTS, tp, S // tp, "post_even",
                            outs_t=[(N_EXPERTS, F32)])
    tm = min(MOE_TILE, S)
    xp3 = _moe_sorted(hf, cwt, x2, mp[5], wg, wu, wd, lyr, tm, S // tm).reshape(B, S, D)
    outs_p = (_cache_rows(kv_t), misc_t[:, :IDX_DIM].transpose(0, 2, 1), u3[:, S - 2:])

    q_s, kv_fs, _, qi_s, misc_s, _, bg_s, u_s, _, _ = _inproj_even(xs, [ms[0], ms[1]], g_mix, wts, gm, Bs, 1)
    n_pool = kv_pool.shape[0]
    kv_t = kv_pool.transpose(0, 2, 3, 4, 1).reshape(n_pool, 2 * LANE, PAGE_SIZE)
    ki_t = kidx_pool.transpose(0, 2, 1)
    qi8 = jnp.pad(qi_s.reshape(Bs, IDX_HEADS, LANE)[:, :, :IDX_DIM], ((0, 0), (0, N_ATT_HEADS - IDX_HEADS), (0, 0)))
    wi8 = jnp.pad(misc_s[:, IDX_DIM:IDX_DIM + IDX_HEADS], ((0, 0), (0, N_ATT_HEADS - IDX_HEADS)))[:, :, None]
    oa_s = _dsa_sample(q_s.reshape(Bs, N_ATT_HEADS, LANE), qi8, wi8, misc_s[:, None, :IDX_DIM], kv_fs[:, None, :],
                       _bias_row(rel_bias, past), gm[2], ki_t, kv_t, page_table,
                       n_keep=min(TOPK_MAX, (past + 1) // 4), grp=_pick_group(Bs, SAMPLE_GROUP)).reshape(Bs, -1)
    x2s, hfs, cwts = _row_call(_post_even_kernel, [xs, oa_s, bg_s, u_s, conv_buf[:, 1], conv_buf[:, 0]],
                               [ms[2], ms[3], ms[4]], post_consts, POST_OUTS, Bs, 1, "post_even_s",
                               outs_t=[(N_EXPERTS, F32)])
    xs3 = _moe(hfs, cwts.T, x2s, ms[5], wg, wu, wd, lyr, Bs, 1)
    outs_s = (kv_fs.reshape(Bs, 1, 2, N_KV_HEADS, HEAD_DIM), misc_s[:, None, :IDX_DIM],
              jnp.concatenate([conv_buf[:, 1:], u_s[:, None, :]], axis=1))
    return xp3, xs3, outs_p, outs_s


O_Z0, O_Q0, O_KVC0, O_KVS0, O_KVW0, O_G0, O_END = 0, 1792, 2816, 3072, 3328, 3584, 3712
P_C = 3 * RWKV_DIM + LORA_W + LORA_A + LORA_G


def _inproj_odd_kernel(x_ref, shift_ref, scale_ref, g_ref, w_ref, qg_ref, ksg_ref, kwg_ref, gone_ref, gtwo_ref,
                       zc_ref, q_ref, kvc_ref, kvs_ref, kvsb_ref, kvw_ref, kvwb_ref, gates_ref,
                       kvct_ref, kvst_ref, kvwt_ref):
    h = _modulate(x_ref[...], g_ref[...], shift_ref[...], scale_ref[...])
    z = jnp.dot(h.astype(BF16), w_ref[...], preferred_element_type=F32)
    zc_ref[...] = z[:, O_Z0:O_Q0]
    gone = gone_ref[...]
    gtwo = gtwo_ref[...]
    for t in range(N_ATT_HEADS):
        sl = slice(t * LANE, (t + 1) * LANE)
        q_ref[:, sl] = _group_rms(z[:, O_Q0 + t * LANE:O_Q0 + (t + 1) * LANE], gone, qg_ref[:, sl]).astype(BF16)
    kvc_ref[...] = z[:, O_KVC0:O_KVS0]
    kvct_ref[0:LANE, :] = z[:, O_KVC0:O_KVC0 + LANE].T
    kvct_ref[LANE:2 * LANE, :] = z[:, O_KVC0 + LANE:O_KVS0].T
    for base, gain_ref, f_ref, b_ref, t_ref in ((O_KVS0, ksg_ref, kvs_ref, kvsb_ref, kvst_ref),
                                                (O_KVW0, kwg_ref, kvw_ref, kvwb_ref, kvwt_ref)):
        k = _group_rms(z[:, base:base + LANE], gtwo, gain_ref[...])
        v = z[:, base + LANE:base + 2 * LANE]
        t_ref[0:LANE, :] = k.T
        t_ref[LANE:2 * LANE, :] = v.T
        f_ref[:, 0:LANE] = k
        f_ref[:, LANE:2 * LANE] = v
        b_ref[:, 0:LANE] = k.astype(BF16)
        b_ref[:, LANE:2 * LANE] = v.astype(BF16)
    gates_ref[...] = _sigmoid(z[:, O_G0:O_END])


def _odd_weights(w_in, w_out, q_norm, k_norm):
    d = w_in.shape[0]
    a_q, a_kv = N_ATT_HEADS * HEAD_DIM, 2 * N_KV_HEADS * HEAD_DIM
    o = P_C
    wz = w_in[:, :o]
    wq = w_in[:, o:o + a_q]; o += a_q
    wkv = w_in[:, o:o + 3 * a_kv]; o += 3 * a_kv
    wg = w_in[:, o:]
    wg = jnp.concatenate([wg, jnp.zeros((d, LANE - wg.shape[1]), F32)], -1)
    w_in_p = jnp.concatenate([wz, _pad_q_cols(wq), wkv, wg], -1).astype(BF16)
    w_out_p = jnp.concatenate([w_out[:RWKV_DIM], _pad_o_rows(w_out[RWKV_DIM:])], 0).astype(BF16)
    qg = jnp.tile(q_norm, 2 * N_ATT_HEADS).reshape(1, N_ATT_HEADS * LANE)
    ksg = jnp.tile(k_norm[1], 2).reshape(1, LANE)
    kwg = jnp.tile(k_norm[2], 2).reshape(1, LANE)
    return w_in_p, w_out_p, qg, ksg, kwg


def _inproj_odd(x, mods, g, wts, gm, tile, tpb):
    w_in_p, _, qg, ksg, kwg = wts
    gone, gtwo, _ = gm
    outs = [(P_C, F32), (N_ATT_HEADS * LANE, BF16), (2 * LANE, F32), (2 * LANE, F32), (2 * LANE, BF16),
            (2 * LANE, F32), (2 * LANE, BF16), (LANE, F32)]
    return _row_call(_inproj_odd_kernel, x, mods, [g, w_in_p, qg, ksg, kwg, gone, gtwo], outs, tile, tpb,
                     "inproj_odd", outs_bt=[(2 * LANE, F32)] * 3)


def _rwkv_pre_kernel(z_ref, zp_ref, mu_ref, w0_ref, a0_ref, kk_ref, ka_ref, wup_ref, aup_ref, gup_ref, gsum_ref,
                     r_o, w_o, k_o, v_o, kk_o, kka_o, g_o):
    z = z_ref[...]
    zm = z + (zp_ref[...] - z) * mu_ref[...]
    r = zm[:, 0:RWKV_DIM]
    k = zm[:, RWKV_DIM:2 * RWKV_DIM]
    v = zm[:, 2 * RWKV_DIM:3 * RWKV_DIM]
    t12 = zm[:, 3 * RWKV_DIM:3 * RWKV_DIM + LANE]
    gd = zm[:, 3 * RWKV_DIM + LANE:P_C]
    xw = w0_ref[...] + _dot(jnp.tanh(t12), wup_ref[...])
    sp = jnp.maximum(-xw, 0.0) + jnp.log(1.0 + jnp.exp(-jnp.abs(xw)))
    w_o[...] = jnp.exp(-jnp.exp(-sp - 0.5))
    a = _sigmoid(a0_ref[...] + _dot(t12, aup_ref[...]))
    g_o[...] = _dot(_sigmoid(gd), gup_ref[...])
    kk = k * kk_ref[...]
    gsum = gsum_ref[...]
    for t in range(RWKV_DIM // LANE):
        sl = slice(t * LANE, (t + 1) * LANE)
        kt = kk[:, sl]
        nrm = jnp.maximum(jnp.sqrt(_dot_split(kt * kt, gsum)), 1e-12)
        kn = kt / nrm
        kk_o[:, sl] = kn
        kka_o[:, sl] = kn * a[:, sl]
    r_o[...] = r
    v_o[...] = v
    k_o[...] = k * (1.0 + (a - 1.0) * ka_ref[...])


def _rwkv_pre(zc, zprev, cpar, gsum, tile):
    mu, w0, w_up, a0, a_up, g_up, k_k, k_a = cpar
    z64 = jnp.zeros((LORA_W, RWKV_DIM), F32)
    consts = [mu.reshape(1, P_C), w0.reshape(1, -1), a0.reshape(1, -1), k_k.reshape(1, -1), k_a.reshape(1, -1),
              jnp.concatenate([w_up, z64], 0).astype(BF16), jnp.concatenate([z64, a_up], 0).astype(BF16),
              g_up.astype(BF16), gsum]
    return _row_call(_rwkv_pre_kernel, [zc, zprev], [], consts, [(RWKV_DIM, F32)] * 7, tile, 1, "rwkv_pre")


SCAN_P = 64
SCAN_VH = HEAD_DIM // 2
SCAN_SEQS = SCAN_P // RWKV_HEADS


def _scan_kernel(kk_ref, w_ref, kka_ref, k_ref, r_ref, v_ref, s0_ref, y_ref, so_ref, st, *, tc):
    ti = pl.program_id(1)

    @pl.when(ti == 0)
    def _():
        st[...] = s0_ref[...]

    lo_half = lax.broadcasted_iota(I32, (1, LANE), 1) < SCAN_P

    def lanes_of(ref, t):
        x = ref[:, t]
        xs = [x[:, h * HEAD_DIM:(h + 1) * HEAD_DIM] for h in range(RWKV_HEADS)]
        return jnp.concatenate(xs + xs, axis=0).T

    def step(t, c):
        kk, w, kka, kt, rt, vf = (lanes_of(ref, t) for ref in (kk_ref, w_ref, kka_ref, k_ref, r_ref, v_ref))
        ys = []
        for vi in range(SCAN_VH):
            s = st[vi]
            sa = -jnp.sum(s * kk, axis=0, keepdims=True)
            vrow = jnp.where(lo_half, vf[vi:vi + 1, :], vf[vi + SCAN_VH:vi + SCAN_VH + 1, :])
            sn = s * w + sa * kka + vrow * kt
            st[vi] = sn
            ys.append(jnp.sum(sn * rt, axis=0, keepdims=True))
        y_ref[t] = jnp.concatenate(ys, axis=0)
        return c

    lax.fori_loop(0, tc, step, 0)

    @pl.when(ti == pl.num_programs(1) - 1)
    def _():
        so_ref[...] = st[...]


def _scan_unlayout_y(y, b, t):
    nc = y.shape[0]
    a = jnp.concatenate([y[..., :SCAN_P], y[..., SCAN_P:]], axis=2)
    a = a.reshape(nc, t, HEAD_DIM, RWKV_HEADS, SCAN_SEQS).transpose(0, 4, 1, 3, 2)
    return a.reshape(nc * SCAN_SEQS, t, RWKV_DIM)[:b].reshape(b * t, RWKV_DIM)


def _scan_layout_state(s):
    b = s.shape[0]
    nc = -(-b // SCAN_SEQS)
    a = jnp.pad(s, ((0, nc * SCAN_SEQS - b), (0, 0), (0, 0), (0, 0)))
    a = a.reshape(nc, SCAN_SEQS, RWKV_HEADS, HEAD_DIM, HEAD_DIM).transpose(0, 3, 4, 2, 1)
    a = a.reshape(nc, HEAD_DIM, HEAD_DIM, SCAN_P)
    return jnp.concatenate([a[:, :SCAN_VH], a[:, SCAN_VH:]], -1)


def _scan_unlayout_state(st, b):
    nc = st.shape[0]
    a = jnp.concatenate([st[..., :SCAN_P], st[..., SCAN_P:]], axis=1)
    a = a.reshape(nc, HEAD_DIM, HEAD_DIM, RWKV_HEADS, SCAN_SEQS).transpose(0, 4, 3, 1, 2)
    return a.reshape(nc * SCAN_SEQS, RWKV_HEADS, HEAD_DIM, HEAD_DIM)[:b]


def _rwkv_scan(pre, s0, b, t, tc):
    r, w, k, v, kk, kka, _ = pre
    bp = -(-b // SCAN_SEQS) * SCAN_SEQS
    ops = [jnp.pad(x.reshape(b, t, RWKV_DIM), ((0, bp - b), (0, 0), (0, 0))) for x in (kk, w, kka, k, r, v)]
    s0l = _scan_layout_state(s0)
    nc = s0l.shape[0]
    kspec = pl.BlockSpec((SCAN_SEQS, tc, RWKV_DIM), lambda c, i: (c, i, 0))
    vspec = pl.BlockSpec((None, tc, SCAN_VH, LANE), lambda c, i: (c, i, 0, 0))
    sspec = pl.BlockSpec((None, SCAN_VH, HEAD_DIM, LANE), lambda c, i: (c, 0, 0, 0))
    y, so = pl.pallas_call(
        functools.partial(_scan_kernel, tc=tc),
        grid=(nc, t // tc),
        in_specs=[kspec] * 6 + [sspec],
        out_specs=[vspec, sspec],
        out_shape=[jax.ShapeDtypeStruct((nc, t, SCAN_VH, LANE), F32),
                   jax.ShapeDtypeStruct((nc, SCAN_VH, HEAD_DIM, LANE), F32)],
        scratch_shapes=[pltpu.VMEM((SCAN_VH, HEAD_DIM, LANE), F32)],
        compiler_params=_cparams("parallel", "arbitrary"),
        name="rwkv_scan",
    )(*ops, s0l)
    return _scan_unlayout_y(y, b, t), _scan_unlayout_state(so, b)


def _compress_kernel(x_ref, pe_ref, w_ref, kg_ref, gtwo_ref, o_ref):
    z = jnp.dot((x_ref[...] + pe_ref[...]).astype(BF16), w_ref[...], preferred_element_type=F32)
    o_ref[:, 0:LANE] = _group_rms(z[:, 0:LANE], gtwo_ref[...], kg_ref[...])
    o_ref[:, LANE:2 * LANE] = z[:, LANE:2 * LANE]


def _compress_weights(cmp_pe, cmp_w, k_norm_c):
    wk = cmp_w[0].reshape(CMP_BLOCK, HEAD_DIM, HEAD_DIM)
    wv = cmp_w[1].reshape(CMP_BLOCK, HEAD_DIM, HEAD_DIM)
    full = jnp.einsum("srde,st->rsdte", jnp.stack([wk, wk, wv, wv]), jnp.eye(4, dtype=F32))
    pe = jnp.stack([cmp_pe[0], cmp_pe[0], cmp_pe[1], cmp_pe[1]], axis=1)
    return (full.reshape(CMP_BLOCK * 4 * HEAD_DIM, 4 * HEAD_DIM).astype(BF16), pe.reshape(1, -1),
            jnp.tile(k_norm_c, 2).reshape(1, LANE))


def _compress(rows, cw, gtwo, tile):
    wfull, pe, kg = cw
    return _row_call(_compress_kernel, rows, [], [pe, wfull, kg, gtwo], [(2 * LANE, F32)], tile, 1, "nsa_compress")[0]


def _compress_paged_kernel(pt_ref, ident_ref, pe_ref, w_ref, kg_ref, gtwo_ref, *refs, grp, n_pages):
    del pt_ref
    page_refs, o_ref, xs = refs[:grp * n_pages], refs[-2], refs[-1]
    ident = ident_ref[...]
    for i in range(grp * n_pages):
        for half in range(2):
            xt = (page_refs[i][half * LANE:(half + 1) * LANE, :] + pe_ref[half]).astype(BF16)
            xs[half, i * PAGE_SIZE:(i + 1) * PAGE_SIZE, :] = _dot_nt(ident, xt)
    n_blk = grp * n_pages * (PAGE_SIZE // CMP_BLOCK)
    acc = [jnp.zeros((n_blk, LANE), F32) for _ in range(2)]
    for r in range(CMP_BLOCK):
        for half in range(2):
            rows = xs[half, pl.ds(r, n_blk, stride=CMP_BLOCK), :]
            acc[half] = acc[half] + jnp.dot(rows.astype(BF16), w_ref[half, r], preferred_element_type=F32)
    kc = _group_rms(acc[0], gtwo_ref[...], kg_ref[...])
    per_seq = n_blk // grp
    for g in range(grp):
        o_ref[g, 0:per_seq, 0:LANE] = kc[g * per_seq:(g + 1) * per_seq].astype(BF16)
        o_ref[g, 0:per_seq, LANE:2 * LANE] = acc[1][g * per_seq:(g + 1) * per_seq].astype(BF16)
        o_ref[g, per_seq:LANE, :] = jnp.zeros((LANE - per_seq, 2 * LANE), BF16)


def _compress_paged(cmp_t, page_table, cmp_pe, cmp_w, k_norm_c, gtwo, grp):
    b, n_pages = page_table.shape
    wk = cmp_w[0].reshape(CMP_BLOCK, HEAD_DIM, HEAD_DIM)
    wv = cmp_w[1].reshape(CMP_BLOCK, HEAD_DIM, HEAD_DIM)
    wbd = jnp.einsum("krde,ht->krhdte", jnp.stack([wk, wv]), jnp.eye(N_KV_HEADS, dtype=F32))
    wbd = wbd.reshape(2, CMP_BLOCK, LANE, LANE).astype(BF16)
    pe = jnp.tile(cmp_pe.transpose(0, 2, 1), (1, N_KV_HEADS, PAGE_SIZE // CMP_BLOCK))
    ident = jnp.eye(LANE, dtype=BF16)
    kg = jnp.tile(k_norm_c, 2).reshape(1, LANE)
    cspec = lambda a: pl.BlockSpec(a.shape, lambda i, pt: (0,) * a.ndim)
    kern = functools.partial(_compress_paged_kernel, grp=grp, n_pages=n_pages)
    return pl.pallas_call(
        kern,
        grid_spec=pltpu.PrefetchScalarGridSpec(
            num_scalar_prefetch=1, grid=(b // grp,),
            in_specs=[cspec(ident), cspec(pe), cspec(wbd), cspec(kg), cspec(gtwo)]
            + _page_specs(cmp_t, n_pages, grp),
            out_specs=pl.BlockSpec((grp, LANE, 2 * LANE), lambda i, pt: (i, 0, 0)),
            scratch_shapes=[pltpu.VMEM((2, grp * n_pages * PAGE_SIZE, LANE), F32)]),
        out_shape=jax.ShapeDtypeStruct((b, LANE, 2 * LANE), BF16),
        compiler_params=_cparams("parallel"),
        name="nsa_compress_paged",
    )(page_table, ident, pe, wbd, kg, gtwo, *([cmp_t] * (grp * n_pages)))


def _bias_cmp(rel_bias, q_starts, qb):
    q0 = jnp.asarray(q_starts, I32)[:, None, None]
    r = jnp.arange(qb)[None, :, None]
    n = jnp.arange(LANE)[None, None, :]
    return _bias_lookup(rel_bias, q0 + r - (n * CMP_BLOCK + CMP_BLOCK - 1)).transpose(0, 3, 1, 2)


def _nsa_sample_kernel(pt_ref, q_ref, gates_ref, kcv_ref, snew_ref, wnew_ref, win_ref, bias_ref, biasc_ref,
                       biasw_ref, pair_ref, *refs, grp, n_pages, t_pos, n_cmp, n_sel, w_eff):
    del pt_ref
    slc_refs, o_ref = refs[:grp * n_pages], refs[-1]
    lane1 = lax.broadcasted_iota(I32, (1, LANE), 1)
    lane8 = lax.broadcasted_iota(I32, (N_ATT_HEADS, LANE), 1)
    row_all = lax.broadcasted_iota(I32, (N_ATT_HEADS, (n_pages + 1) * LANE), 0)
    lanew = lax.broadcasted_iota(I32, (N_ATT_HEADS, w_eff), 1)
    lane1_f = lane1.astype(F32)
    bias = bias_ref[...]
    bias_now = bias[:, n_pages * LANE:n_pages * LANE + 1]
    mask_c = ((lane8 * CMP_BLOCK + CMP_BLOCK - 1) <= t_pos) & (lane8 < n_cmp)
    cur = t_pos // SEL_BLOCK
    forced = (lane1 == 0) | (lane1 == cur) | (lane1 == cur - 1)
    sel_causal = lane1 * SEL_BLOCK <= t_pos
    tail_valid = jnp.concatenate([lane1] * n_pages + [lane1 + LANE], axis=1) <= LANE
    for g in range(grp):
        q = q_ref[g]
        qf = q.astype(F32)
        s = jnp.where(mask_c, _dot_nt(q, kcv_ref[g, :, 0:LANE]) * ATT_SCALE + biasc_ref[...], NEG)
        e = jnp.exp(s - jnp.max(s, axis=1, keepdims=True))
        pc = jnp.where(mask_c, e / jnp.sum(e, axis=1, keepdims=True), 0.0)
        o_c = jnp.dot(pc.astype(BF16), kcv_ref[g, :, LANE:2 * LANE], preferred_element_type=F32)
        masks = []
        for hk in range(N_KV_HEADS):
            ps = jnp.sum(pc[hk * GQA:(hk + 1) * GQA], axis=0, keepdims=True)
            score = _dot_split(ps, pair_ref[...])
            score = jnp.where(sel_causal, jnp.where(forced, FORCE, score), NEG)
            picked = jnp.zeros((1, LANE), F32)
            for _ in range(n_sel):
                mx = jnp.max(score, axis=1, keepdims=True)
                first = jnp.min(jnp.where(score == mx, lane1_f, float(LANE)), axis=1, keepdims=True)
                hit = lane1_f == first
                picked = jnp.where(hit, 1.0, picked)
                score = jnp.where(hit, TAKEN, score)
            per_page = PAGE_SIZE // SEL_BLOCK
            tiles = []
            for p in range(n_pages + 1):
                t = jnp.zeros((1, LANE), F32)
                for a in range(per_page):
                    blk = picked[:, p * per_page + a:p * per_page + a + 1]
                    t = jnp.where(lane1 // SEL_BLOCK == a, blk, t)
                tiles.append(t)
            masks.append(jnp.concatenate(tiles, axis=1))
        valid = (jnp.where(row_all < GQA, masks[0], masks[1]) > 0.5) & tail_valid
        snew = _bf16_round(snew_ref[g])
        tiles = [_dot(q, slc_refs[g * n_pages + p][0:LANE, :]) for p in range(n_pages)]
        tiles.append(jnp.where(lane1 == 0, jnp.sum(qf * snew[:, 0:LANE], axis=1, keepdims=True), 0.0))
        s = jnp.where(valid, jnp.concatenate(tiles, axis=1) * ATT_SCALE + bias, NEG)
        e = jnp.exp(s - jnp.max(s, axis=1, keepdims=True))
        p_all = jnp.where(valid, e / jnp.sum(e, axis=1, keepdims=True), 0.0)
        o_s = p_all[:, n_pages * LANE:n_pages * LANE + 1] * snew[:, LANE:2 * LANE]
        for p in range(n_pages):
            o_s = o_s + _dot_nt(p_all[:, p * LANE:(p + 1) * LANE], slc_refs[g * n_pages + p][LANE:2 * LANE, :])
        wnew = _bf16_round(wnew_ref[g])
        valid_w = (w_eff - lanew) < WINDOW
        s_w = jnp.where(valid_w, _dot(q, win_ref[g, 0:LANE, :]) * ATT_SCALE + biasw_ref[...], NEG)
        s_n = jnp.sum(qf * wnew[:, 0:LANE], axis=1, keepdims=True) * ATT_SCALE + bias_now
        m = jnp.maximum(jnp.max(s_w, axis=1, keepdims=True), s_n)
        e_w = jnp.where(valid_w, jnp.exp(s_w - m), 0.0)
        e_n = jnp.exp(s_n - m)
        den = jnp.sum(e_w, axis=1, keepdims=True) + e_n
        o_w = _dot_nt(e_w / den, win_ref[g, LANE:2 * LANE, :]) + (e_n / den) * wnew[:, LANE:2 * LANE]
        gates = gates_ref[g]
        _head_rows_out(o_ref, g, gates[:, 0:1] * o_c + gates[:, 1:2] * o_s + gates[:, 2:3] * o_w)


def _nsa_sample(q8, gates8, kcv, snew, wnew, win_t, bias, bias_c, bias_w, pair, slc_t, page_table,
                *, t_pos, n_cmp, n_sel, grp):
    b, n_pages = page_table.shape
    w_eff = win_t.shape[2]
    gspec = lambda a: pl.BlockSpec((grp,) + a.shape[1:], lambda i, pt: (i,) + (0,) * (a.ndim - 1))
    cspec = lambda a: pl.BlockSpec(a.shape, lambda i, pt: (0,) * a.ndim)
    kern = functools.partial(_nsa_sample_kernel, grp=grp, n_pages=n_pages, t_pos=t_pos, n_cmp=n_cmp, n_sel=n_sel,
                             w_eff=w_eff)
    return pl.pallas_call(
        kern,
        grid_spec=pltpu.PrefetchScalarGridSpec(
            num_scalar_prefetch=1, grid=(b // grp,),
            in_specs=[gspec(q8), gspec(gates8), gspec(kcv), gspec(snew), gspec(wnew), gspec(win_t), cspec(bias),
                      cspec(bias_c), cspec(bias_w), cspec(pair)] + _page_specs(slc_t, n_pages, grp),
            out_specs=pl.BlockSpec((grp, N_ATT_HEADS, LANE), lambda i, pt: (i, 0, 0))),
        out_shape=jax.ShapeDtypeStruct((b, N_ATT_HEADS, LANE), BF16),
        compiler_params=_cparams("parallel"),
        name="nsa_sample",
    )(page_table, q8, gates8, kcv, snew, wnew, win_t, bias, bias_c, bias_w, pair, *([slc_t] * (grp * n_pages)))


def _post_odd_kernel(x_ref, y_ref, r_ref, k_ref, v_ref, g_ref, od_ref, gate_ref, shf_ref, scf_ref,
                     gf_ref, lnw_ref, lnb_ref, rk_ref, gtwo_ref, wo_ref, wrt_ref, br_ref, x2_ref, hf_ref, cwt_ref):
    gtwo = gtwo_ref[...]
    mix = jnp.dot(od_ref[...], wo_ref[RWKV_DIM:RWKV_DIM + N_ATT_HEADS * LANE, :], preferred_element_type=F32)
    for t in range(RWKV_DIM // LANE):
        sl = slice(t * LANE, (t + 1) * LANE)
        y = y_ref[:, sl]
        dlt = y - _dot_split(y, gtwo)
        yn = (dlt * lax.rsqrt(_dot_split(dlt * dlt, gtwo) + GN_EPS)) * lnw_ref[:, sl] + lnb_ref[:, sl]
        dot_rk = _dot_split(r_ref[:, sl] * k_ref[:, sl] * rk_ref[:, sl], gtwo) * float(HEAD_DIM)
        oc = (yn + dot_rk * v_ref[:, sl]) * g_ref[:, sl]
        mix = mix + jnp.dot(oc.astype(BF16), wo_ref[sl, :], preferred_element_type=F32)
    _post_tail(x_ref[...], mix, gate_ref[...], gf_ref[...], shf_ref[...], scf_ref[...], wrt_ref, br_ref,
               x2_ref, hf_ref, cwt_ref)


def _odd_layer(xp, xs, mp, ms, page_table, wkv0, shift0, cmp_pool, slc_pool, win_buf, rel_bias, router, experts,
               g_mix, g_ffn, w_in, w_out, cpar, r_k, ln_w, ln_b, q_norm, k_norm, cmp_pe, cmp_w):
    B, S, D = xp.shape
    Bs = xs.shape[0]
    n_pages = page_table.shape[1]
    past = n_pages * PAGE_SIZE
    wts = _odd_weights(w_in, w_out, q_norm, k_norm)
    gm = _gmats()
    gone, gtwo, _ = gm
    gsum = (gtwo.astype(F32) * HEAD_DIM).astype(BF16)
    i = jnp.arange(LANE)
    pair = jnp.where(i[:, None] // 2 == i[None, :], 1.0, 0.0).astype(BF16)
    cw = _compress_weights(cmp_pe, cmp_w, k_norm[0])
    wrt, br = router
    wg, wu, wd, lyr = experts
    g_mix = g_mix.reshape(1, D)
    post_consts = [g_ffn.reshape(1, D), ln_w.reshape(1, -1), ln_b.reshape(1, -1), r_k.reshape(1, -1), gtwo, wts[1],
                   wrt, br]
    w_eff = win_buf.shape[1]

    tp = ROW_TILE
    zc, q_p, kvc, _, kvs_b, _, kvw_b, gates, kvc_t, kvs_t, kvw_t = _inproj_odd(xp.reshape(B * S, D), [mp[0], mp[1]],
                                                                               g_mix, wts, gm, tp, S // tp)
    r3 = lambda a: a.reshape(B, S, a.shape[-1])
    pre = _rwkv_pre(zc, _shift_rows(r3(zc), 1).reshape(B * S, P_C), cpar, gsum, tp)
    y, wkv_p = _rwkv_scan(pre, jnp.zeros((B, RWKV_HEADS, HEAD_DIM, HEAD_DIM), F32), B, S, SCAN_TIME_CHUNK)
    n_cmp = S // CMP_BLOCK
    kcv = _compress(kvc.reshape(B * n_cmp, CMP_BLOCK * 2 * LANE), cw, gtwo, _pick_tile(B * n_cmp, 256))
    kcv = jnp.pad(kcv.reshape(B, n_cmp, 2 * LANE), ((0, 0), (0, LANE - n_cmp), (0, 0))).astype(BF16)
    n_slc = -(-S // SEL_BLOCK)
    od = _nsa_t(r3(q_p), r3(gates), kcv, r3(kvs_b), r3(kvw_b), _bias_tiles_t(rel_bias), _bias_cmp_t(rel_bias, S),
                pair.T, n_cmp=n_cmp, n_sel=min(N_SEL_BLOCKS, n_slc))
    x2, hf, cwt = _row_call(_post_odd_kernel,
                            [xp.reshape(B * S, D), y, pre[0], pre[2], pre[3], pre[6], od.reshape(B * S, -1)],
                            [mp[2], mp[3], mp[4]], post_consts, POST_OUTS, tp, S // tp, "post_odd",
                            outs_t=[(N_EXPERTS, F32)])
    tm = min(MOE_TILE, S)
    xp3 = _moe_sorted(hf, cwt, x2, mp[5], wg, wu, wd, lyr, tm, S // tm).reshape(B, S, D)
    kv5 = lambda a, n: a.reshape(-1, n, 2, N_KV_HEADS, HEAD_DIM)
    outs_p = (wkv_p, r3(zc)[:, S - 1], _cache_rows(kvc_t), _cache_rows(kvs_t),
              _cache_rows(kvw_t[:, :, S - min(WINDOW, S):]))

    zc_s, q_s, kvc_s, kvs_s, _, kvw_s, _, gates_s, _, _, _ = _inproj_odd(xs, [ms[0], ms[1]], g_mix, wts, gm, Bs, 1)
    pre_s = _rwkv_pre(zc_s, shift0, cpar, gsum, Bs)
    y_s, wkv_s = _rwkv_scan(pre_s, wkv0, Bs, 1, 1)
    n_pool = cmp_pool.shape[0]
    cmp_t = cmp_pool.transpose(0, 2, 3, 4, 1).reshape(n_pool, 2 * LANE, PAGE_SIZE)
    kcv_s = _compress_paged(cmp_t, page_table, cmp_pe, cmp_w, k_norm[0], gtwo, _pick_group(Bs, SAMPLE_GROUP))
    n_cmp_s = (past + 1) // CMP_BLOCK
    slc_t = slc_pool.transpose(0, 2, 3, 4, 1).reshape(n_pool, 2 * LANE, PAGE_SIZE)
    win_t = win_buf.transpose(0, 2, 3, 4, 1).reshape(Bs, 2 * LANE, w_eff)
    gates8 = jnp.pad(gates_s[:, :3 * N_ATT_HEADS].reshape(Bs, 3, N_ATT_HEADS).transpose(0, 2, 1),
                     ((0, 0), (0, 0), (0, LANE - 3)))
    n_slc_s = -(-(past + 1) // SEL_BLOCK)
    od_s = _nsa_sample(q_s.reshape(Bs, N_ATT_HEADS, LANE), gates8, kcv_s, kvs_s[:, None, :], kvw_s[:, None, :],
                       win_t, _bias_row(rel_bias, past), _bias_cmp(rel_bias, [past], 1)[0, :, 0, :],
                       _bias_lookup(rel_bias, w_eff - jnp.arange(w_eff)).T, pair, slc_t, page_table,
                       t_pos=past, n_cmp=n_cmp_s, n_sel=min(N_SEL_BLOCKS, n_slc_s),
                       grp=_pick_group(Bs, SAMPLE_GROUP)).reshape(Bs, -1)
    x2s, hfs, cwts = _row_call(_post_odd_kernel, [xs, y_s, pre_s[0], pre_s[2], pre_s[3], pre_s[6], od_s],
                               [ms[2], ms[3], ms[4]], post_consts, POST_OUTS, Bs, 1, "post_odd_s",
                               outs_t=[(N_EXPERTS, F32)])
    xs3 = _moe(hfs, cwts.T, x2s, ms[5], wg, wu, wd, lyr, Bs, 1)
    win_new = jnp.concatenate([win_buf[:, 1:], kv5(kvw_s, 1)], axis=1)
    outs_s = (wkv_s, zc_s, kv5(kvc_s, 1), kv5(kvs_s, 1), win_new)
    return xp3, xs3, outs_p, outs_s


def _mods(c_p, c_s, w_all, b, layer):
    nb = c_p.shape[0]
    m = _ada(jnp.concatenate([c_p, c_s], 0), w_all, b, layer)
    parts = jnp.split(m, 6, axis=-1)
    return [p[:nb, None, :] for p in parts], [p[None, nb:, :] for p in parts]


def _forward(x_prompt, x_sample, c_prompt, c_sample, page_table, cache_a_kv, cache_a_kidx, state_b_conv,
             state_c_wkv, state_c_shift, cache_d_cmp, cache_d_slc, cache_d_win, rel_bias, w_router, b_router,
             w_ada, b_ada, g_norm_mix, g_norm_ffn, w_expert_gate, w_expert_up, w_expert_down, e_w_in, e_w_out,
             a_q_norm, a_k_norm, b_conv_w, b_conv_b, o_w_in, o_w_out, c_mu, c_w0, c_w_up, c_a0, c_a_up,
             c_g_up, c_k_k, c_k_a, c_r_k, c_ln_w, c_ln_b, d_q_norm, d_k_norm, d_cmp_pe, d_cmp_w):
    assert w_ada.shape[0] == 2 and e_w_in.shape[0] == 1 and o_w_in.shape[0] == 1
    B, S, D = x_prompt.shape
    Bs = x_sample.shape[0]
    assert x_sample.shape[1] == 1
    xp, xs = x_prompt, x_sample.reshape(Bs, D)
    router = (w_router.T, b_router.reshape(N_EXPERTS, 1))
    n_pool = cache_a_kv.shape[1]
    wbf = tuple(w.astype(BF16) for w in (w_expert_gate, w_expert_up, w_expert_down))
    experts = lambda l: wbf + (l,)

    mp, ms = _mods(c_prompt, c_sample, w_ada, b_ada[0], 0)
    xp, xs, ep, es = _even_layer(xp, xs, mp, ms, page_table, cache_a_kv[0], cache_a_kidx[0], state_b_conv[0],
                                 rel_bias, router, experts(0), g_norm_mix[0], g_norm_ffn[0], e_w_in[0], e_w_out[0],
                                 a_q_norm[0], a_k_norm[0], b_conv_w[0], b_conv_b[0])
    mp, ms = _mods(c_prompt, c_sample, w_ada, b_ada[1], 1)
    cpar = (c_mu[0], c_w0[0], c_w_up[0], c_a0[0], c_a_up[0], c_g_up[0], c_k_k[0], c_k_a[0])
    xp, xs, op, os_ = _odd_layer(xp, xs, mp, ms, page_table, state_c_wkv[0], state_c_shift[0], cache_d_cmp[0],
                                 cache_d_slc[0], cache_d_win[0], rel_bias, router, experts(1), g_norm_mix[1],
                                 g_norm_ffn[1], o_w_in[0], o_w_out[0], cpar, c_r_k[0].reshape(-1), c_ln_w[0],
                                 c_ln_b[0], d_q_norm[0], d_k_norm[0], d_cmp_pe[0], d_cmp_w[0])
    stack = lambda ts: tuple(a[None] for a in ts)
    return (xp, xs.reshape(Bs, 1, D)) + stack(ep) + stack(op) + stack(es) + stack(os_)


def kernel(x_prompt, x_sample, c_prompt, c_sample, page_table, cache_a_kv, cache_a_kidx, state_b_conv, state_c_wkv, state_c_shift, cache_d_cmp, cache_d_slc, cache_d_win, rel_bias, w_router, b_router, w_ada, b_ada, g_norm_mix, g_norm_ffn, w_expert_gate, w_expert_up, w_expert_down, e_w_in, e_w_out, a_q_norm, a_k_norm, b_conv_w, b_conv_b, o_w_in, o_w_out, c_mu, c_w0, c_w_up, c_a0, c_a_up, c_g_up, c_k_k, c_k_a, c_r_k, c_ln_w, c_ln_b, d_q_norm, d_k_norm, d_cmp_pe, d_cmp_w):
    return _forward(x_prompt, x_sample, c_prompt, c_sample, page_table, cache_a_kv, cache_a_kidx, state_b_conv,
                    state_c_wkv, state_c_shift, cache_d_cmp, cache_d_slc, cache_d_win, rel_bias, w_router, b_router,
                    w_ada, b_ada, g_norm_mix, g_norm_ffn, w_expert_gate, w_expert_up, w_expert_down, e_w_in, e_w_out,
                    a_q_norm, a_k_norm, b_conv_w, b_conv_b, o_w_in, o_w_out, c_mu, c_w0, c_w_up, c_a0, c_a_up,
                    c_g_up, c_k_k, c_k_a, c_r_k, c_ln_w, c_ln_b, d_q_norm, d_k_norm, d_cmp_pe, d_cmp_w)
```

```python
import functools
import math

import jax
import jax.numpy as jnp
from jax import lax
from jax.experimental import pallas as pl
from jax.experimental.pallas import tpu as pltpu

F32 = jnp.float32
BF16 = jnp.bfloat16
I32 = jnp.int32

LANE = 128
HEAD_DIM = 64
N_ATT_HEADS = 8
N_KV_HEADS = 2
GQA = N_ATT_HEADS // N_KV_HEADS
IDX_HEADS = 4
IDX_DIM = 64
TOPK_MAX = 256
CONV_CH = 512
RWKV_HEADS = 8
RWKV_DIM = RWKV_HEADS * HEAD_DIM
LORA_W = 64
LORA_A = 64
LORA_G = 128
GN_EPS = 64e-5
CMP_BLOCK = 32
SEL_BLOCK = 64
N_SEL_BLOCKS = 8
WINDOW = 512
N_BUCKETS = 32
MAX_DISTANCE = 128
N_EXPERTS = 16
N_GROUPS = 4
EXPERTS_PER_GROUP = N_EXPERTS // N_GROUPS
PAGE_SIZE = 128
RMS_EPS = 1e-6
NEG = -1e30
FORCE = 1e9
TAKEN = -3e38
ATT_SCALE = HEAD_DIM ** -0.5
VMEM_LIMIT = 56 * 1024 * 1024
ROW_TILE = 256
MOE_TILE = 1024
SCAN_TIME_CHUNK = 32
SAMPLE_GROUP = 4


def _cparams(*sem):
    return pltpu.CompilerParams(dimension_semantics=sem, vmem_limit_bytes=VMEM_LIMIT)


def _pick_tile(rows, pref):
    t = min(pref, rows)
    while rows % t or (t % 8 and t != rows):
        t -= 1
    return t


def _pick_group(n, pref):
    g = min(pref, n)
    while n % g:
        g -= 1
    return g


def _const_spec(a):
    nd = a.ndim
    return pl.BlockSpec(a.shape, lambda *_: (0,) * nd)


def _dot(a, b):
    return jnp.dot(a.astype(BF16), b.astype(BF16), preferred_element_type=F32)


def _dot_nt(a, b):
    return lax.dot_general(a.astype(BF16), b.astype(BF16), (((1,), (1,)), ((), ())),
                           preferred_element_type=F32)


def _dot_split(x, m):
    hi = x.astype(BF16)
    r1 = x - hi.astype(F32)
    mid = r1.astype(BF16)
    lo = (r1 - mid.astype(F32)).astype(BF16)
    return (jnp.dot(hi, m, preferred_element_type=F32) + jnp.dot(mid, m, preferred_element_type=F32)
            + jnp.dot(lo, m, preferred_element_type=F32))


def _bf16_round(x):
    return x.astype(BF16).astype(F32)


def _sigmoid(x):
    return 1.0 / (1.0 + jnp.exp(-x))


def _silu(x):
    return x * _sigmoid(x)


def _modulate(x, g, shift, scale):
    y = x * lax.rsqrt(jnp.mean(x * x, axis=-1, keepdims=True) + RMS_EPS)
    return (y * g) * (1.0 + scale) + shift


def _group_rms(t, gmat, gain):
    ms = _dot_split(t * t, gmat)
    return (t * lax.rsqrt(ms + RMS_EPS)) * gain


def _ada_kernel(c_ref, w_ref, b_ref, o_ref):
    o_ref[...] = _dot(_silu(c_ref[...]), w_ref[...]) + b_ref[...]


def _ada(c, w_all, b, layer):
    r, d = c.shape
    n = w_all.shape[2]
    tn = 512
    return pl.pallas_call(
        _ada_kernel,
        grid=(n // tn,),
        in_specs=[pl.BlockSpec((r, d), lambda j: (0, 0)),
                  pl.BlockSpec((None, d, tn), lambda j: (layer, 0, j)),
                  pl.BlockSpec((1, tn), lambda j: (0, j))],
        out_specs=pl.BlockSpec((r, tn), lambda j: (0, j)),
        out_shape=jax.ShapeDtypeStruct((r, n), F32),
        compiler_params=_cparams("parallel"),
        name="ada_mod",
    )(c, w_all, b.reshape(1, n))


E_Q0, E_KV0, E_QI0, E_MISC0, E_BG0, E_CG0, E_XIN0, E_END = 0, 1024, 1280, 1792, 1920, 2432, 2944, 3456


def _inproj_even_kernel(x_ref, shift_ref, scale_ref, g_ref, w_ref, qg_ref, kg_ref, gone_ref, gtwo_ref,
                        q_ref, kv_ref, kvb_ref, qi_ref, misc_ref, miscb_ref, bg_ref, u_ref, kvt_ref, misct_ref):
    h = _modulate(x_ref[...], g_ref[...], shift_ref[...], scale_ref[...])
    z = jnp.dot(h.astype(BF16), w_ref[...], preferred_element_type=F32)
    gone = gone_ref[...]
    for t in range(N_ATT_HEADS):
        sl = slice(t * LANE, (t + 1) * LANE)
        q_ref[:, sl] = _group_rms(z[:, E_Q0 + t * LANE:E_Q0 + (t + 1) * LANE], gone, qg_ref[:, sl]).astype(BF16)
    k = _group_rms(z[:, E_KV0:E_KV0 + LANE], gtwo_ref[...], kg_ref[...])
    v = z[:, E_KV0 + LANE:E_KV0 + 2 * LANE]
    kv_ref[:, 0:LANE] = k
    kv_ref[:, LANE:2 * LANE] = v
    kvb_ref[:, 0:LANE] = k.astype(BF16)
    kvb_ref[:, LANE:2 * LANE] = v.astype(BF16)
    qi_ref[...] = z[:, E_QI0:E_MISC0].astype(BF16)
    misc = z[:, E_MISC0:E_BG0]
    misc_ref[...] = misc
    miscb_ref[...] = misc.astype(BF16)
    bg_ref[...] = z[:, E_BG0:E_CG0]
    u_ref[...] = z[:, E_CG0:E_XIN0] * z[:, E_XIN0:E_END]
    kvt_ref[0:LANE, :] = k.T
    kvt_ref[LANE:2 * LANE, :] = v.T
    misct_ref[...] = misc.T


HALO = 8


def _shifted(z, halo, k, tpb):
    first = (pl.program_id(0) % tpb) == 0
    row = lax.broadcasted_iota(I32, z.shape, 0)
    out = pltpu.roll(z, k, axis=0)
    for j in range(k):
        prev = jnp.where(first, 0.0, halo[HALO - k + j:HALO - k + j + 1, :])
        out = jnp.where(row == j, prev, out)
    return out


def _row_call(kernel, xs, mods, consts, outs, tile, tpb, name, outs_t=(), outs_bt=(), halos=()):
    if not isinstance(xs, (list, tuple)):
        xs = [xs]
    rows = xs[0].shape[0]
    n_tiles = rows // tile
    in_specs = [pl.BlockSpec((tile, x.shape[1]), lambda t: (t, 0)) for x in xs]
    in_specs += [pl.BlockSpec((HALO, xs[i].shape[1]), lambda t: (jnp.maximum(t * (tile // HALO) - 1, 0), 0))
                 for i in halos]
    xs = list(xs) + [xs[i] for i in halos]
    for m in mods:
        in_specs.append(pl.BlockSpec((None,) + m.shape[1:], lambda t: (t // tpb, 0, 0)))
    in_specs += [_const_spec(c) for c in consts]
    out_specs = [pl.BlockSpec((tile, w), lambda t: (t, 0)) for (w, _) in outs]
    out_shape = [jax.ShapeDtypeStruct((rows, w), dt) for (w, dt) in outs]
    out_specs += [pl.BlockSpec((hh, tile), lambda t: (0, t)) for (hh, _) in outs_t]
    out_shape += [jax.ShapeDtypeStruct((hh, rows), dt) for (hh, dt) in outs_t]
    out_specs += [pl.BlockSpec((None, hh, tile), lambda t: (t // tpb, 0, t % tpb)) for (hh, _) in outs_bt]
    out_shape += [jax.ShapeDtypeStruct((n_tiles // tpb, hh, tile * tpb), dt) for (hh, dt) in outs_bt]
    return pl.pallas_call(kernel, grid=(n_tiles,), in_specs=in_specs, out_specs=out_specs, out_shape=out_shape,
                          compiler_params=_cparams("parallel"), name=name)(*xs, *mods, *consts)


def _pad_q_cols(wq):
    d = wq.shape[0]
    w = wq.reshape(d, N_ATT_HEADS, HEAD_DIM)
    z = jnp.zeros_like(w)
    lo = jnp.concatenate([w, z], -1)
    hi = jnp.concatenate([z, w], -1)
    sel = (jnp.arange(N_ATT_HEADS) >= GQA)[None, :, None]
    return jnp.where(sel, hi, lo).reshape(d, N_ATT_HEADS * LANE)


def _pad_o_rows(wo):
    d = wo.shape[1]
    w = wo.reshape(N_ATT_HEADS, HEAD_DIM, d)
    z = jnp.zeros_like(w)
    lo = jnp.concatenate([w, z], 1)
    hi = jnp.concatenate([z, w], 1)
    sel = (jnp.arange(N_ATT_HEADS) >= GQA)[:, None, None]
    return jnp.where(sel, hi, lo).reshape(N_ATT_HEADS * LANE, d)


def _gmats():
    i = jnp.arange(LANE)
    gone = jnp.full((LANE, LANE), 1.0 / HEAD_DIM, F32).astype(BF16)
    gtwo = jnp.where((i[:, None] // HEAD_DIM) == (i[None, :] // HEAD_DIM), 1.0 / HEAD_DIM, 0.0).astype(BF16)
    tri = jnp.where(i[:, None] <= i[None, :], 1.0, 0.0).astype(BF16)
    return gone, gtwo, tri


def _even_weights(w_in, w_out, q_norm, k_norm):
    d = w_in.shape[0]
    a_q, a_kv = N_ATT_HEADS * HEAD_DIM, 2 * N_KV_HEADS * HEAD_DIM
    o = 0
    wq = w_in[:, o:o + a_q]; o += a_q
    wkv = w_in[:, o:o + a_kv]; o += a_kv
    wqi = w_in[:, o:o + IDX_HEADS * IDX_DIM]; o += IDX_HEADS * IDX_DIM
    wki = w_in[:, o:o + IDX_DIM]; o += IDX_DIM
    wwi = w_in[:, o:o + IDX_HEADS]; o += IDX_HEADS
    wrest = w_in[:, o:]
    wqi = jnp.concatenate([wqi.reshape(d, IDX_HEADS, IDX_DIM), jnp.zeros((d, IDX_HEADS, LANE - IDX_DIM), F32)],
                          -1).reshape(d, IDX_HEADS * LANE)
    wmisc = jnp.concatenate([wki, wwi, jnp.zeros((d, LANE - IDX_DIM - IDX_HEADS), F32)], -1)
    w_in_p = jnp.concatenate([_pad_q_cols(wq), wkv, wqi, wmisc, wrest], -1).astype(BF16)
    w_out_p = jnp.concatenate([_pad_o_rows(w_out[:a_q]), w_out[a_q:]], 0).astype(BF16)
    qg = jnp.tile(q_norm, 2 * N_ATT_HEADS).reshape(1, N_ATT_HEADS * LANE)
    kg = jnp.tile(k_norm, 2).reshape(1, LANE)
    return w_in_p, w_out_p, qg, kg


def _inproj_even(x, mods, g, wts, gm, tile, tpb):
    w_in_p, _, qg, kg = wts
    gone, gtwo, _ = gm
    outs = [(N_ATT_HEADS * LANE, BF16), (2 * LANE, F32), (2 * LANE, BF16), (IDX_HEADS * LANE, BF16),
            (LANE, F32), (LANE, BF16), (CONV_CH, F32), (CONV_CH, F32)]
    return _row_call(_inproj_even_kernel, x, mods, [g, w_in_p, qg, kg, gone, gtwo], outs, tile, tpb, "inproj_even",
                     outs_bt=[(2 * LANE, F32), (LANE, F32)])


def _t5_bucket(dist):
    dist = jnp.maximum(dist, 0)
    exact = N_BUCKETS // 2
    far = exact + (jnp.log(jnp.maximum(dist, 1).astype(F32) / exact)
                   / math.log(MAX_DISTANCE / exact) * (N_BUCKETS - exact)).astype(I32)
    return jnp.where(dist < exact, dist, jnp.minimum(far, N_BUCKETS - 1))


def _bias_lookup(rel_bias, dist):
    onehot = (_t5_bucket(dist)[..., None] == jnp.arange(N_BUCKETS)).astype(F32)
    return jnp.einsum("...k,kh->...h", onehot, rel_bias, precision=lax.Precision.HIGHEST)


def _bias_row(rel_bias, t_pos):
    return _bias_lookup(rel_bias, t_pos - jnp.arange(t_pos + LANE)).T


def _bias_tiles(rel_bias, qb):
    r = jnp.arange(qb)[:, None]
    c = jnp.arange(LANE)[None, :]
    tiles = [_bias_lookup(rel_bias, d * LANE + r - c) for d in range(3)]
    return jnp.stack(tiles).transpose(0, 3, 1, 2)


def _stack_heads(q_ref, hk):
    return jnp.concatenate([q_ref[:, (hk * GQA + g) * LANE:(hk * GQA + g + 1) * LANE] for g in range(GQA)], axis=0)


QB = LANE
QW = GQA * QB


def _sub_sum(x):
    return jnp.sum(x, axis=0, keepdims=True)


def _flash_t_pair(blocks, qs, bias_ref, carry, acc_ref):
    logits = [[jnp.where(mk[hk], _dot_nt(kb, qs[hk]) + bias_ref[dsel, hk], NEG) for (kb, _, mk, dsel) in blocks]
              for hk in range(N_KV_HEADS)]
    out = []
    for hk in range(N_KV_HEADS):
        m, l = carry[hk]
        m_new = m
        for s in logits[hk]:
            m_new = jnp.maximum(m_new, jnp.max(s, axis=0, keepdims=True))
        alpha = jnp.exp(m - m_new)
        l = alpha * l
        pv = None
        for s, (_, vt, _, _) in zip(logits[hk], blocks):
            p = jnp.exp(s - m_new)
            l = l + _sub_sum(p)
            d = jnp.dot(vt, p.astype(BF16), preferred_element_type=F32)
            pv = d if pv is None else pv + d
        acc_ref[hk] = alpha * acc_ref[hk] + pv
        out.append((m_new, l))
    return tuple(out)


def _scaled_queries(q_ref, hk):
    return (_stack_heads(q_ref, hk).astype(F32) * ATT_SCALE).astype(BF16)


def _pair_loop(nblk, body, init):
    def body2(jj, c):
        return body(2 * jj + 1, body(2 * jj, c))
    return lax.fori_loop(0, (nblk + 1) // 2, body2, init)


def _flash_t_init():
    return (jnp.full((1, QW), NEG, F32), jnp.zeros((1, QW), F32))


def _tile_lanes(x):
    return jnp.concatenate([x] * GQA, axis=1)


def _write_heads_t(o_ref, o_ts):
    lane = lax.broadcasted_iota(I32, (QB, LANE), 1)
    for hk in range(N_KV_HEADS):
        valid = (lane // HEAD_DIM) == hk
        for g in range(GQA):
            h = hk * GQA + g
            o = o_ts[hk][:, g * QB:(g + 1) * QB].T
            o_ref[:, h * LANE:(h + 1) * LANE] = jnp.where(valid, o, 0.0).astype(BF16)


def _dsa_t_kernel(q_ref, qi_ref, misc_ref, kidx_ref, k_ref, vt_ref, bias_ref, trit_ref, o_ref, key_s, acc_s,
                  *, n_keep):
    i = pl.program_id(1)
    q0 = i * QB
    nblk = i + 1
    krow = lax.broadcasted_iota(I32, (LANE, QB), 0)
    qcol = lax.broadcasted_iota(I32, (LANE, QB), 1)
    misc_t = misc_ref[...].T
    wis = [_bf16_round(misc_t[IDX_DIM + h:IDX_DIM + h + 1, :]) for h in range(IDX_HEADS)]
    qi = qi_ref[...]
    idx_scale = (IDX_HEADS * IDX_DIM) ** -0.5

    def causal(j):
        return j * LANE + krow <= q0 + qcol

    def pass_a(j, c):
        kb = kidx_ref[pl.ds(pl.multiple_of(j * LANE, LANE), LANE), :]
        acc = jnp.zeros((LANE, QB), F32)
        for h in range(IDX_HEADS):
            acc = acc + _bf16_round(jnp.maximum(_dot_nt(kb, qi[:, h * LANE:(h + 1) * LANE]), 0.0)) * wis[h]
        key_s[j] = _order_keys(jnp.where(causal(j), acc * idx_scale, NEG))
        return c

    _pair_loop(nblk, pass_a, 0)

    def count(pred):
        def body(j, a):
            return a + jnp.where(pred(key_s[j]), 1.0, 0.0)
        return _sub_sum(_pair_loop(nblk, body, jnp.zeros((LANE, QB), F32)))

    keep = float(n_keep)
    thr = jnp.where(count(lambda k: k >= 0) >= keep, jnp.int32(0), jnp.int32(-2 ** 31))

    def search(it, thr):
        cand = thr | lax.shift_left(jnp.int32(1), jnp.int32(30) - it)
        return jnp.where(count(lambda k: k >= cand) >= keep, cand, thr)

    thr = lax.fori_loop(0, 31, search, thr)
    need = keep - count(lambda k: k > thr)
    trit = trit_ref[...]

    def pass_c(j, run):
        key = key_s[j]
        eq = key == thr
        eqf = jnp.where(eq, 1.0, 0.0)
        cum = jnp.dot(trit, eqf.astype(BF16), preferred_element_type=F32) + run
        sel = ((key > thr) | (eq & (cum <= need))) & causal(j)
        key_s[j] = jnp.where(sel, 1, 0)
        return run + _sub_sum(eqf)

    _pair_loop(nblk, pass_c, jnp.zeros((1, QB), F32))

    qs = [_scaled_queries(q_ref, hk) for hk in range(N_KV_HEADS)]
    acc_s[...] = jnp.zeros_like(acc_s)

    def pass_d(jj, carry):
        blocks = []
        for j in (2 * jj, 2 * jj + 1):
            kb = k_ref[pl.ds(pl.multiple_of(j * LANE, LANE), LANE), :]
            blocks.append((kb, vt_ref[j], [_tile_lanes(key_s[j] > 0)] * N_KV_HEADS, jnp.clip(i - j, 0, 2)))
        return _flash_t_pair(blocks, qs, bias_ref, carry, acc_s)

    res = lax.fori_loop(0, (nblk + 1) // 2, pass_d, tuple(_flash_t_init() for _ in range(N_KV_HEADS)))
    _write_heads_t(o_ref, [acc_s[hk] / res[hk][1] for hk in range(N_KV_HEADS)])


def _bias_tiles_t(rel_bias):
    t = _bias_tiles(rel_bias, QB)
    t = t.reshape(3, N_KV_HEADS, GQA, QB, LANE).transpose(0, 1, 4, 2, 3)
    return t.reshape(3, N_KV_HEADS, LANE, QW)


def _blocks_t(x):
    b, s, w = x.shape
    return x.reshape(b, s // LANE, LANE, w).transpose(0, 1, 3, 2)


def _dsa_t(q, qi, misc, kidx_b, kv_b, bias_t, trit, *, n_keep):
    b, s, _ = q.shape
    vt = _blocks_t(kv_b[:, :, LANE:])
    qspec = lambda w: pl.BlockSpec((None, QB, w), lambda bi, i: (bi, i, 0))
    kspec = pl.BlockSpec((None, s, LANE), lambda bi, i: (bi, 0, 0))
    return pl.pallas_call(
        functools.partial(_dsa_t_kernel, n_keep=n_keep),
        grid=(b, s // QB),
        in_specs=[qspec(N_ATT_HEADS * LANE), qspec(IDX_HEADS * LANE), qspec(LANE), kspec, kspec,
                  pl.BlockSpec((None, s // LANE, LANE, LANE), lambda bi, i: (bi, 0, 0, 0)),
                  _const_spec(bias_t), _const_spec(trit)],
        out_specs=qspec(N_ATT_HEADS * LANE),
        out_shape=jax.ShapeDtypeStruct((b, s, N_ATT_HEADS * LANE), BF16),
        scratch_shapes=[pltpu.VMEM((s // LANE, LANE, QB), I32), pltpu.VMEM((N_KV_HEADS, LANE, QW), F32)],
        compiler_params=_cparams("parallel", "parallel"),
        name="dsa_attention_t",
    )(q, qi, misc, kidx_b, kv_b, vt, bias_t, trit)


def _dot_split_rhs(m, x):
    hi = x.astype(BF16)
    r1 = x - hi.astype(F32)
    mid = r1.astype(BF16)
    lo = (r1 - mid.astype(F32)).astype(BF16)
    return (jnp.dot(m, hi, preferred_element_type=F32) + jnp.dot(m, mid, preferred_element_type=F32)
            + jnp.dot(m, lo, preferred_element_type=F32))


def _nsa_t_kernel(q_ref, gates_ref, kc_ref, vct_ref, ks_ref, vst_ref, kw_ref, vwt_ref, bias_ref, biasc_ref,
                  pairt_ref, o_ref, acc_s, *, n_cmp, n_sel):
    i = pl.program_id(1)
    q0 = i * QB
    nblk = i + 1
    heads = range(N_KV_HEADS)
    krow = lax.broadcasted_iota(I32, (LANE, QB), 0)
    qcol = lax.broadcasted_iota(I32, (LANE, QB), 1)
    t_pos = q0 + qcol
    krow_f = krow.astype(F32)
    qs = [_scaled_queries(q_ref, hk) for hk in heads]
    gates_t = gates_ref[...].T

    def gate_row(br, hk):
        return jnp.concatenate([gates_t[br * N_ATT_HEADS + hk * GQA + g:br * N_ATT_HEADS + hk * GQA + g + 1, :]
                                for g in range(GQA)], axis=1)

    mask_c = _tile_lanes(((krow * CMP_BLOCK + CMP_BLOCK - 1) <= t_pos) & (krow < n_cmp))
    cur = t_pos // SEL_BLOCK
    forced = (krow == 0) | (krow == cur) | (krow == cur - 1)
    sel_causal = krow * SEL_BLOCK <= t_pos
    o_cmp, picked = [], []
    for hk in heads:
        s = jnp.where(mask_c, _dot_nt(kc_ref[...], qs[hk]) + biasc_ref[hk], NEG)
        e = jnp.exp(s - jnp.max(s, axis=0, keepdims=True))
        p = jnp.where(mask_c, e / _sub_sum(e), 0.0)
        o_cmp.append(jnp.dot(vct_ref[...], p.astype(BF16), preferred_element_type=F32))
        ps = p[:, 0:QB]
        for g in range(1, GQA):
            ps = ps + p[:, g * QB:(g + 1) * QB]
        score = _dot_split_rhs(pairt_ref[...], ps)
        score = jnp.where(sel_causal, jnp.where(forced, FORCE, score), NEG)
        pk = jnp.zeros((LANE, QB), F32)
        for _ in range(n_sel):
            mx = jnp.max(score, axis=0, keepdims=True)
            first = jnp.min(jnp.where(score == mx, krow_f, float(LANE)), axis=0, keepdims=True)
            hit = krow_f == first
            pk = jnp.where(hit, 1.0, pk)
            score = jnp.where(hit, TAKEN, score)
        picked.append(pk.astype(BF16))

    def key_block(k_ref, vt_ref, j):
        return k_ref[pl.ds(pl.multiple_of(j * LANE, LANE), LANE), :], vt_ref[j]

    erow = lax.broadcasted_iota(I32, (LANE, LANE), 0)
    ecol = lax.broadcasted_iota(I32, (LANE, LANE), 1)
    acc_s[...] = jnp.zeros_like(acc_s)

    def slc_body(jj, carry):
        blocks = []
        for j in (2 * jj, 2 * jj + 1):
            kb, vt = key_block(ks_ref, vst_ref, j)
            expand = jnp.where(ecol == 2 * j + erow // SEL_BLOCK, 1.0, 0.0).astype(BF16)
            causal = j * LANE + krow <= t_pos
            masks = [_tile_lanes((jnp.dot(expand, picked[hk], preferred_element_type=F32) > 0.5) & causal)
                     for hk in heads]
            blocks.append((kb, vt, masks, jnp.clip(i - j, 0, 2)))
        return _flash_t_pair(blocks, qs, bias_ref, carry, acc_s.at[0])

    res_s = lax.fori_loop(0, (nblk + 1) // 2, slc_body, tuple(_flash_t_init() for _ in heads))

    lo = jnp.maximum(i - WINDOW // LANE - 1, 0)

    def win_body(jj, carry):
        blocks = []
        for j in (lo + 2 * jj, lo + 2 * jj + 1):
            kb, vt = key_block(kw_ref, vwt_ref, j)
            dist = t_pos - (j * LANE + krow)
            mask = _tile_lanes((dist >= 0) & (dist < WINDOW))
            blocks.append((kb, vt, [mask] * N_KV_HEADS, jnp.clip(i - j, 0, 2)))
        return _flash_t_pair(blocks, qs, bias_ref, carry, acc_s.at[1])

    res_w = lax.fori_loop(0, (i - lo + 2) // 2, win_body, tuple(_flash_t_init() for _ in heads))

    _write_heads_t(o_ref, [gate_row(0, hk) * o_cmp[hk] + gate_row(1, hk) * (acc_s[0, hk] / res_s[hk][1])
                           + gate_row(2, hk) * (acc_s[1, hk] / res_w[hk][1]) for hk in heads])


def _bias_cmp_t(rel_bias, s):
    t = _bias_cmp(rel_bias, [j * QB for j in range(s // QB)], QB)
    t = t.reshape(s // QB, N_KV_HEADS, GQA, QB, LANE).transpose(0, 1, 4, 2, 3)
    return t.reshape(s // QB, N_KV_HEADS, LANE, QW)


def _nsa_t(q, gates, kcv, kvs_b, kvw_b, bias_t, bias_c, pair_t, *, n_cmp, n_sel):
    b, s, _ = q.shape
    vct = kcv[:, :, LANE:].transpose(0, 2, 1)
    qspec = lambda w: pl.BlockSpec((None, QB, w), lambda bi, i: (bi, i, 0))
    kspec = pl.BlockSpec((None, s, LANE), lambda bi, i: (bi, 0, 0))
    vspec = pl.BlockSpec((None, s // LANE, LANE, LANE), lambda bi, i: (bi, 0, 0, 0))
    cspec = pl.BlockSpec((None, LANE, LANE), lambda bi, i: (bi, 0, 0))
    return pl.pallas_call(
        functools.partial(_nsa_t_kernel, n_cmp=n_cmp, n_sel=n_sel),
        grid=(b, s // QB),
        in_specs=[qspec(N_ATT_HEADS * LANE), qspec(LANE), cspec, cspec, kspec, vspec, kspec, vspec,
                  _const_spec(bias_t), pl.BlockSpec((None,) + bias_c.shape[1:], lambda bi, i: (i, 0, 0, 0)),
                  _const_spec(pair_t)],
        out_specs=qspec(N_ATT_HEADS * LANE),
        out_shape=jax.ShapeDtypeStruct((b, s, N_ATT_HEADS * LANE), BF16),
        scratch_shapes=[pltpu.VMEM((2, N_KV_HEADS, LANE, QW), F32)],
        compiler_params=_cparams("parallel", "parallel"),
        name="nsa_attention_t",
    )(q, gates, kcv, vct, kvs_b, _blocks_t(kvs_b[:, :, LANE:]), kvw_b, _blocks_t(kvw_b[:, :, LANE:]),
      bias_t, bias_c, pair_t)


def _select_top(keys, n_keep, tri):
    keep = float(n_keep)
    n = keys[0].shape[1]

    def count(pred):
        return [jnp.sum(jnp.where(pred(g, k), 1.0, 0.0), axis=1, keepdims=True) for g, k in enumerate(keys)]

    int_min = jnp.int32(-2 ** 31)
    thr = tuple(jnp.where(c >= keep, jnp.int32(0), int_min) for c in count(lambda g, k: k >= 0))

    def search(it, thr):
        bit = lax.shift_left(jnp.int32(1), jnp.int32(30) - it)
        cand = [t | bit for t in thr]
        cnt = count(lambda g, k: k >= cand[g])
        return tuple(jnp.where(c >= keep, cd, t) for c, cd, t in zip(cnt, cand, thr))

    thr = lax.fori_loop(0, 31, search, thr)
    need = [keep - c for c in count(lambda g, k: k > thr[g])]
    sels = []
    for g, k in enumerate(keys):
        run = jnp.zeros((1, 1), F32)
        parts = []
        for t in range(n // LANE):
            kt = k[:, t * LANE:(t + 1) * LANE]
            eq = kt == thr[g]
            eqf = jnp.where(eq, 1.0, 0.0)
            cum = jnp.dot(eqf.astype(BF16), tri, preferred_element_type=F32) + run
            parts.append((kt > thr[g]) | (eq & (cum <= need[g])))
            run = run + jnp.sum(eqf, axis=1, keepdims=True)
        sels.append(jnp.concatenate(parts, axis=1))
    return sels


def _order_keys(score):
    score = jnp.where(score == 0.0, 0.0, score)
    bits = lax.bitcast_convert_type(score, I32)
    return jnp.where(bits < 0, bits ^ jnp.int32(0x7FFFFFFF), bits)


def _head_rows_out(o_ref, g, acc):
    rowh = lax.broadcasted_iota(I32, (N_ATT_HEADS, LANE), 0)
    laneh = lax.broadcasted_iota(I32, (N_ATT_HEADS, LANE), 1)
    o_ref[g] = jnp.where((laneh // HEAD_DIM) == (rowh // GQA), acc, 0.0).astype(BF16)


def _dsa_sample_kernel(pt_ref, q_ref, qi_ref, wi_ref, knew_ref, kvnew_ref, bias_ref, tri_ref, *refs,
                       grp, n_pages, n_keep):
    del pt_ref
    ki_refs, kv_refs, o_ref = refs[:grp * n_pages], refs[grp * n_pages:2 * grp * n_pages], refs[-1]
    lane1 = lax.broadcasted_iota(I32, (1, LANE), 1)
    idx_scale = (IDX_HEADS * IDX_DIM) ** -0.5
    keys = []
    for g in range(grp):
        qi = qi_ref[g]
        wi = _bf16_round(wi_ref[g])
        tiles = []
        for p in range(n_pages):
            rel = _bf16_round(jnp.maximum(_dot(qi, ki_refs[g * n_pages + p][...]), 0.0))
            tiles.append(jnp.sum(rel * wi, axis=0, keepdims=True) * idx_scale)
        rel_new = _bf16_round(jnp.maximum(jnp.sum(qi.astype(F32) * _bf16_round(knew_ref[g]), axis=1, keepdims=True),
                                          0.0))
        sc_new = jnp.sum(rel_new * wi, axis=0, keepdims=True) * idx_scale
        tiles.append(jnp.where(lane1 == 0, sc_new, NEG))
        keys.append(_order_keys(jnp.concatenate(tiles, axis=1)))
    sels = _select_top(keys, n_keep, tri_ref[...])
    bias = bias_ref[...]
    for g in range(grp):
        q = q_ref[g]
        kvnew = _bf16_round(kvnew_ref[g])
        tiles = [_dot(q, kv_refs[g * n_pages + p][0:LANE, :]) for p in range(n_pages)]
        s_new = jnp.sum(q.astype(F32) * kvnew[:, 0:LANE], axis=1, keepdims=True)
        tiles.append(jnp.where(lane1 == 0, s_new, 0.0))
        valid = sels[g] & (jnp.concatenate([lane1] * n_pages + [lane1 + LANE], axis=1) <= LANE)
        s = jnp.where(valid, jnp.concatenate(tiles, axis=1) * ATT_SCALE + bias, NEG)
        e = jnp.exp(s - jnp.max(s, axis=1, keepdims=True))
        p_all = jnp.where(valid, e / jnp.sum(e, axis=1, keepdims=True), 0.0)
        acc = p_all[:, n_pages * LANE:n_pages * LANE + 1] * kvnew[:, LANE:2 * LANE]
        for p in range(n_pages):
            acc = acc + _dot_nt(p_all[:, p * LANE:(p + 1) * LANE], kv_refs[g * n_pages + p][LANE:2 * LANE, :])
        _head_rows_out(o_ref, g, acc)


def _page_specs(pool_t, n_pages, grp):
    r, c = pool_t.shape[1:]
    return [pl.BlockSpec((None, r, c), lambda i, pt, g=g, p=p: (pt[i * grp + g, p], 0, 0))
            for g in range(grp) for p in range(n_pages)]


def _dsa_sample(q8, qi8, wi8, knew, kvnew, bias, tri, ki_t, kv_t, page_table, *, n_keep, grp):
    b, n_pages = page_table.shape
    gspec = lambda a: pl.BlockSpec((grp,) + a.shape[1:], lambda i, pt: (i,) + (0,) * (a.ndim - 1))
    cspec = lambda a: pl.BlockSpec(a.shape, lambda i, pt: (0,) * a.ndim)
    kern = functools.partial(_dsa_sample_kernel, grp=grp, n_pages=n_pages, n_keep=n_keep)
    return pl.pallas_call(
        kern,
        grid_spec=pltpu.PrefetchScalarGridSpec(
            num_scalar_prefetch=1, grid=(b // grp,),
            in_specs=[gspec(q8), gspec(qi8), gspec(wi8), gspec(knew), gspec(kvnew), cspec(bias), cspec(tri)]
            + _page_specs(ki_t, n_pages, grp) + _page_specs(kv_t, n_pages, grp),
            out_specs=pl.BlockSpec((grp, N_ATT_HEADS, LANE), lambda i, pt: (i, 0, 0))),
        out_shape=jax.ShapeDtypeStruct((b, N_ATT_HEADS, LANE), BF16),
        compiler_params=_cparams("parallel"),
        name="dsa_sample",
    )(page_table, q8, qi8, wi8, knew, kvnew, bias, tri, *([ki_t] * (grp * n_pages)), *([kv_t] * (grp * n_pages)))


def _route(hf, wrt, br):
    logits = _dot_nt(wrt, hf)
    s = _sigmoid(logits)
    sel = s + br
    rows = [sel[e:e + 1, :] for e in range(N_EXPERTS)]
    grp = []
    for g in range(N_GROUPS):
        a = rows[g * EXPERTS_PER_GROUP:(g + 1) * EXPERTS_PER_GROUP]
        best = None
        for i in range(EXPERTS_PER_GROUP):
            for j in range(i + 1, EXPERTS_PER_GROUP):
                v = a[i] + a[j]
                best = v if best is None else jnp.maximum(best, v)
        grp.append(best)
    gbest = jnp.zeros_like(grp[0], dtype=I32)
    cur = grp[0]
    for g in range(1, N_GROUPS):
        better = grp[g] > cur
        gbest = jnp.where(better, g, gbest)
        cur = jnp.where(better, grp[g], cur)
    picked = []
    for g in range(N_GROUPS):
        a = rows[g * EXPERTS_PER_GROUP:(g + 1) * EXPERTS_PER_GROUP]
        for j in range(EXPERTS_PER_GROUP):
            rank = jnp.zeros_like(a[j])
            for jj in range(EXPERTS_PER_GROUP):
                if jj != j:
                    ahead = (a[jj] > a[j]) | (a[jj] == a[j]) if jj < j else (a[jj] > a[j])
                    rank = rank + jnp.where(ahead, 1.0, 0.0)
            e = g * EXPERTS_PER_GROUP + j
            picked.append(jnp.where((gbest == g) & (rank < 2.0), s[e:e + 1, :], 0.0))
    den = picked[0]
    for p in picked[1:]:
        den = den + p
    return jnp.concatenate([p / den for p in picked], axis=0)


def _post_tail(x, mix, gate, gf, shf, scf, wrt_ref, br_ref, x2_ref, hf_ref, cwt_ref):
    x2 = x + gate * mix
    x2_ref[...] = x2
    hf = _modulate(x2, gf, shf, scf)
    hf_ref[...] = hf.astype(BF16)
    cwt_ref[...] = _route(hf, wrt_ref[...], br_ref[...])


def _post_even_kernel(x_ref, oa_ref, bg_ref, u_ref, um1_ref, um2_ref, *rest):
    _post_even_body(x_ref, oa_ref, bg_ref, u_ref[...], um1_ref[...], um2_ref[...], *rest)


def _post_even_seq_kernel(x_ref, oa_ref, bg_ref, u_ref, uh_ref, *rest, tpb):
    u = u_ref[...]
    _post_even_body(x_ref, oa_ref, bg_ref, u, _shifted(u, uh_ref[...], 1, tpb), _shifted(u, uh_ref[...], 2, tpb),
                    *rest)


def _post_even_body(x_ref, oa_ref, bg_ref, u, um1, um2, gate_ref, shf_ref, scf_ref,
                    gf_ref, cw_ref, cb_ref, wo_ref, wrt_ref, br_ref, x2_ref, hf_ref, cwt_ref):
    cw = cw_ref[...]
    y = cb_ref[...] + cw[0:1, :] * um2
    y = y + cw[1:2, :] * um1
    y = y + cw[2:3, :] * u
    n_a = N_ATT_HEADS * LANE
    mix = (jnp.dot(oa_ref[...], wo_ref[0:n_a, :], preferred_element_type=F32)
           + jnp.dot((bg_ref[...] * y).astype(BF16), wo_ref[n_a:n_a + CONV_CH, :], preferred_element_type=F32))
    _post_tail(x_ref[...], mix, gate_ref[...], gf_ref[...], shf_ref[...], scf_ref[...], wrt_ref, br_ref,
               x2_ref, hf_ref, cwt_ref)


POST_OUTS = [(1024, F32), (1024, BF16)]


def _moe_kernel(hf_ref, cw_ref, x2_ref, gate_ref, wg_ref, wu_ref, wd_ref, o_ref, acc_ref):
    e = pl.program_id(1)

    @pl.when(e == 0)
    def _():
        acc_ref[...] = jnp.zeros_like(acc_ref)

    hf = hf_ref[...]
    hmid = _silu(_dot(hf, wg_ref[...])) * _dot(hf, wu_ref[...])
    cw = cw_ref[...]
    lane = lax.broadcasted_iota(I32, cw.shape, 1)
    wcol = jnp.sum(jnp.where(lane == e, cw, 0.0), axis=1, keepdims=True)
    acc_ref[...] += _dot(hmid, wd_ref[...]) * wcol

    @pl.when(e == N_EXPERTS - 1)
    def _():
        o_ref[...] = x2_ref[...] + gate_ref[...] * acc_ref[...]


def _moe(hf, cw, x2, gate, wg, wu, wd, layer, tile, tpb):
    rows, d = x2.shape
    de = wg.shape[3]
    return pl.pallas_call(
        _moe_kernel,
        grid=(rows // tile, N_EXPERTS),
        in_specs=[pl.BlockSpec((tile, d), lambda t, e: (t, 0)),
                  pl.BlockSpec((tile, N_EXPERTS), lambda t, e: (t, 0)),
                  pl.BlockSpec((tile, d), lambda t, e: (t, 0)),
                  pl.BlockSpec((None,) + gate.shape[1:], lambda t, e: (t // tpb, 0, 0)),
                  pl.BlockSpec((None, None, d, de), lambda t, e: (layer, e, 0, 0)),
                  pl.BlockSpec((None, None, d, de), lambda t, e: (layer, e, 0, 0)),
                  pl.BlockSpec((None, None, de, d), lambda t, e: (layer, e, 0, 0))],
        out_specs=pl.BlockSpec((tile, d), lambda t, e: (t, 0)),
        out_shape=jax.ShapeDtypeStruct((rows, d), F32),
        scratch_shapes=[pltpu.VMEM((tile, d), F32)],
        compiler_params=_cparams("parallel", "arbitrary"),
        name="moe_dense",
    )(hf, cw, x2, gate, wg, wu, wd)


MOE_WIN = LANE


def _moe_sorted_kernel(plan_ref, hf_ref, cw_ref, cwt_ref, x2_ref, gate_ref, tril_ref, g16_ref, g16t_ref,
                       wg_ref, wu_ref, wd_ref, o_ref, p_s, pt_s, xs_s, cws_s, ys_s):
    ti = pl.program_id(0)
    e = pl.program_id(1)
    t = hf_ref.shape[0]
    g = e // EXPERTS_PER_GROUP

    @pl.when(e == 0)
    def _():
        tril = tril_ref[...]
        memb_col = jnp.dot(jnp.where(cw_ref[...] > 0.0, 1.0, 0.0).astype(BF16), g16_ref[...],
                           preferred_element_type=F32) > 0.5
        memb_row = jnp.dot(g16t_ref[...], jnp.where(cwt_ref[...] > 0.0, 1.0, 0.0).astype(BF16),
                           preferred_element_type=F32) > 0.5
        mcf = jnp.where(memb_col, 1.0, 0.0)
        mrf = jnp.where(memb_row, 1.0, 0.0)
        rank_col = jnp.dot(tril, mcf.astype(BF16), preferred_element_type=F32)
        rank_row = _dot_nt(mrf, tril)
        lane = lax.broadcasted_iota(I32, (t, LANE), 1)
        row8 = lax.broadcasted_iota(I32, (8, t), 0)
        base_col = jnp.zeros((t, LANE), F32)
        base_row = jnp.zeros((8, t), F32)
        for gg in range(N_GROUPS):
            start = plan_ref[ti, gg].astype(F32)
            base_col = jnp.where(lane == gg, start, base_col)
            base_row = jnp.where(row8 == gg, start, base_row)
        slot_col = jnp.sum(mcf * (base_col + rank_col - 1.0), axis=1, keepdims=True)
        slot_row = jnp.sum(mrf * (base_row + rank_row - 1.0), axis=0, keepdims=True)
        col_iota = lax.broadcasted_iota(I32, (MOE_WIN, t), 1).astype(F32)
        row_iota = lax.broadcasted_iota(I32, (MOE_WIN, t), 0).astype(F32)
        for c in range(t // MOE_WIN):
            rows = slice(c * MOE_WIN, (c + 1) * MOE_WIN)
            p_s[rows, :] = jnp.where(slot_row == row_iota + float(c * MOE_WIN), 1.0, 0.0).astype(BF16)
            pt_s[rows, :] = jnp.where(slot_col[rows] == col_iota, 1.0, 0.0).astype(BF16)
        p = p_s[...]
        xs_s[...] = jnp.dot(p, hf_ref[...], preferred_element_type=F32).astype(BF16)
        cws_s[...] = _dot_split_rhs(p, cw_ref[...])
        ys_s[...] = jnp.zeros_like(ys_s)

    lane16 = lax.broadcasted_iota(I32, (MOE_WIN, N_EXPERTS), 1)

    def window(c, carry):
        rows = pl.ds(pl.multiple_of(c * MOE_WIN, MOE_WIN), MOE_WIN)
        xw = xs_s[rows, :]
        hmid = _silu(_dot(xw, wg_ref[...])) * _dot(xw, wu_ref[...])
        wcol = jnp.sum(jnp.where(lane16 == e, cws_s[rows, :], 0.0), axis=1, keepdims=True)
        ys_s[rows, :] = ys_s[rows, :] + _dot(hmid, wd_ref[...]) * wcol
        return carry

    lax.fori_loop(plan_ref[ti, N_GROUPS + g], plan_ref[ti, 2 * N_GROUPS + g], window, 0)

    @pl.when(e == N_EXPERTS - 1)
    def _():
        ys = ys_s[...]
        hi = ys.astype(BF16)
        lo = (ys - hi.astype(F32)).astype(BF16)
        pt = pt_s[...]
        back = jnp.dot(pt, hi, preferred_element_type=F32) + jnp.dot(pt, lo, preferred_element_type=F32)
        o_ref[...] = x2_ref[...] + gate_ref[...] * back


def _moe_plan(cwt, tile):
    n = cwt.shape[1]
    member = (cwt.reshape(N_GROUPS, EXPERTS_PER_GROUP, n // tile, tile) > 0.0).any(axis=1)
    cnt = member.sum(axis=-1).astype(I32).T
    start = jnp.cumsum(cnt, axis=1) - cnt
    lo = start // MOE_WIN
    hi = jnp.where(cnt > 0, (start + cnt + MOE_WIN - 1) // MOE_WIN, lo)
    return jnp.concatenate([start, lo, hi], axis=1)


def _moe_sorted(hf, cwt, x2, gate, wg, wu, wd, layer, tile, tpb):
    rows, d = x2.shape
    de = wg.shape[3]
    i = jnp.arange(tile)
    tril = jnp.where(i[None, :] <= i[:, None], 1.0, 0.0).astype(BF16)
    e16 = jnp.arange(N_EXPERTS)
    g16 = jnp.where(e16[:, None] // EXPERTS_PER_GROUP == jnp.arange(LANE)[None, :], 1.0, 0.0).astype(BF16)
    g16t = jnp.where(jnp.arange(8)[:, None] == e16[None, :] // EXPERTS_PER_GROUP, 1.0, 0.0).astype(BF16)
    cspec = lambda a: pl.BlockSpec(a.shape, lambda t, e, plan: (0,) * a.ndim)
    return pl.pallas_call(
        _moe_sorted_kernel,
        grid_spec=pltpu.PrefetchScalarGridSpec(
            num_scalar_prefetch=1, grid=(rows // tile, N_EXPERTS),
            in_specs=[pl.BlockSpec((tile, d), lambda t, e, plan: (t, 0)),
                      pl.BlockSpec((tile, N_EXPERTS), lambda t, e, plan: (t, 0)),
                      pl.BlockSpec((N_EXPERTS, tile), lambda t, e, plan: (0, t)),
                      pl.BlockSpec((tile, d), lambda t, e, plan: (t, 0)),
                      pl.BlockSpec((None,) + gate.shape[1:], lambda t, e, plan: (t // tpb, 0, 0)),
                      cspec(tril), cspec(g16), cspec(g16t),
                      pl.BlockSpec((None, None, d, de), lambda t, e, plan: (layer, e, 0, 0)),
                      pl.BlockSpec((None, None, d, de), lambda t, e, plan: (layer, e, 0, 0)),
                      pl.BlockSpec((None, None, de, d), lambda t, e, plan: (layer, e, 0, 0))],
            out_specs=pl.BlockSpec((tile, d), lambda t, e, plan: (t, 0)),
            scratch_shapes=[pltpu.VMEM((tile, tile), BF16), pltpu.VMEM((tile, tile), BF16),
                            pltpu.VMEM((tile, d), BF16), pltpu.VMEM((tile, N_EXPERTS), F32),
                            pltpu.VMEM((tile, d), F32)]),
        out_shape=jax.ShapeDtypeStruct((rows, d), F32),
        compiler_params=_cparams("parallel", "arbitrary"),
        name="moe_sorted",
    )(_moe_plan(cwt, tile), hf, cwt.T, cwt, x2, gate, tril, g16, g16t, wg, wu, wd)


def _cache_rows(kv_t):
    b, _, s = kv_t.shape
    return kv_t.reshape(b, 2, N_KV_HEADS, HEAD_DIM, s).transpose(0, 4, 1, 2, 3)


def _even_layer(xp, xs, mp, ms, page_table, kv_pool, kidx_pool, conv_buf, rel_bias, router, experts,
                g_mix, g_ffn, w_in, w_out, q_norm, k_norm, conv_w, conv_b):
    B, S, D = xp.shape
    Bs = xs.shape[0]
    past = page_table.shape[1] * PAGE_SIZE
    wts = _even_weights(w_in, w_out, q_norm, k_norm)
    gm = _gmats()
    wrt, br = router
    wg, wu, wd, lyr = experts
    g_mix = g_mix.reshape(1, D)
    g_ffn = g_ffn.reshape(1, D)
    post_consts = [g_ffn, conv_w, conv_b.reshape(1, CONV_CH), wts[1], wrt, br]

    tp = ROW_TILE
    q_p, _, kv_b, qi_p, misc, misc_b, bg, u, kv_t, misc_t = _inproj_even(xp.reshape(B * S, D), [mp[0], mp[1]],
                                                                        g_mix, wts, gm, tp, S // tp)
    r3 = lambda a: a.reshape(B, S, a.shape[-1])
    oa = _dsa_t(r3(q_p), r3(qi_p), r3(misc), r3(misc_b), r3(kv_b), _bias_tiles_t(rel_bias), gm[2].T,
                n_keep=min(TOPK_MAX, S // 4))
    u3 = r3(u)
    x2, hf, cwt = _row_call(functools.partial(_post_even_seq_kernel, tpb=S // tp),
                            [xp.reshape(B * S, D), oa.reshape(B * S, -1), bg, u],
                            [mp[2], mp[3], mp[4]], post_consts, POST_OUTS, tp, S // tp, "post_even",
                            outs_t=[(N_EXPERTS, F32)], halos=[3])
    tm = min(MOE_TILE, S)
    xp3 = _moe_sorted(hf, cwt, x2, mp[5], wg, wu, wd, lyr, tm, S // tm).reshape(B, S, D)
    outs_p = (_cache_rows(kv_t), misc_t[:, :IDX_DIM].transpose(0, 2, 1), u3[:, S - 2:])

    q_s, kv_fs, _, qi_s, misc_s, _, bg_s, u_s, _, _ = _inproj_even(xs, [ms[0], ms[1]], g_mix, wts, gm, Bs, 1)
    n_pool = kv_pool.shape[0]
    kv_t = kv_pool.transpose(0, 2, 3, 4, 1).reshape(n_pool, 2 * LANE, PAGE_SIZE)
    ki_t = kidx_pool.transpose(0, 2, 1)
    qi8 = jnp.pad(qi_s.reshape(Bs, IDX_HEADS, LANE)[:, :, :IDX_DIM], ((0, 0), (0, N_ATT_HEADS - IDX_HEADS), (0, 0)))
    wi8 = jnp.pad(misc_s[:, IDX_DIM:IDX_DIM + IDX_HEADS], ((0, 0), (0, N_ATT_HEADS - IDX_HEADS)))[:, :, None]
    oa_s = _dsa_sample(q_s.reshape(Bs, N_ATT_HEADS, LANE), qi8, wi8, misc_s[:, None, :IDX_DIM], kv_fs[:, None, :],
                       _bias_row(rel_bias, past), gm[2], ki_t, kv_t, page_table,
                       n_keep=min(TOPK_MAX, (past + 1) // 4), grp=_pick_group(Bs, SAMPLE_GROUP)).reshape(Bs, -1)
    x2s, hfs, cwts = _row_call(_post_even_kernel, [xs, oa_s, bg_s, u_s, conv_buf[:, 1], conv_buf[:, 0]],
                               [ms[2], ms[3], ms[4]], post_consts, POST_OUTS, Bs, 1, "post_even_s",
                               outs_t=[(N_EXPERTS, F32)])
    xs3 = _moe(hfs, cwts.T, x2s, ms[5], wg, wu, wd, lyr, Bs, 1)
    outs_s = (kv_fs.reshape(Bs, 1, 2, N_KV_HEADS, HEAD_DIM), misc_s[:, None, :IDX_DIM],
              jnp.concatenate([conv_buf[:, 1:], u_s[:, None, :]], axis=1))
    return xp3, xs3, outs_p, outs_s


O_Z0, O_Q0, O_KVC0, O_KVS0, O_KVW0, O_G0, O_END = 0, 1792, 2816, 3072, 3328, 3584, 3712
P_C = 3 * RWKV_DIM + LORA_W + LORA_A + LORA_G


def _inproj_odd_kernel(x_ref, shift_ref, scale_ref, g_ref, w_ref, qg_ref, ksg_ref, kwg_ref, gone_ref, gtwo_ref,
                       zc_ref, q_ref, kvc_ref, kvs_ref, kvsb_ref, kvw_ref, kvwb_ref, gates_ref,
                       kvct_ref, kvst_ref, kvwt_ref):
    h = _modulate(x_ref[...], g_ref[...], shift_ref[...], scale_ref[...])
    z = jnp.dot(h.astype(BF16), w_ref[...], preferred_element_type=F32)
    zc_ref[...] = z[:, O_Z0:O_Q0]
    gone = gone_ref[...]
    gtwo = gtwo_ref[...]
    for t in range(N_ATT_HEADS):
        sl = slice(t * LANE, (t + 1) * LANE)
        q_ref[:, sl] = _group_rms(z[:, O_Q0 + t * LANE:O_Q0 + (t + 1) * LANE], gone, qg_ref[:, sl]).astype(BF16)
    kvc_ref[...] = z[:, O_KVC0:O_KVS0]
    kvct_ref[0:LANE, :] = z[:, O_KVC0:O_KVC0 + LANE].T
    kvct_ref[LANE:2 * LANE, :] = z[:, O_KVC0 + LANE:O_KVS0].T
    for base, gain_ref, f_ref, b_ref, t_ref in ((O_KVS0, ksg_ref, kvs_ref, kvsb_ref, kvst_ref),
                                                (O_KVW0, kwg_ref, kvw_ref, kvwb_ref, kvwt_ref)):
        k = _group_rms(z[:, base:base + LANE], gtwo, gain_ref[...])
        v = z[:, base + LANE:base + 2 * LANE]
        t_ref[0:LANE, :] = k.T
        t_ref[LANE:2 * LANE, :] = v.T
        f_ref[:, 0:LANE] = k
        f_ref[:, LANE:2 * LANE] = v
        b_ref[:, 0:LANE] = k.astype(BF16)
        b_ref[:, LANE:2 * LANE] = v.astype(BF16)
    gates_ref[...] = _sigmoid(z[:, O_G0:O_END])


def _odd_weights(w_in, w_out, q_norm, k_norm):
    d = w_in.shape[0]
    a_q, a_kv = N_ATT_HEADS * HEAD_DIM, 2 * N_KV_HEADS * HEAD_DIM
    o = P_C
    wz = w_in[:, :o]
    wq = w_in[:, o:o + a_q]; o += a_q
    wkv = w_in[:, o:o + 3 * a_kv]; o += 3 * a_kv
    wg = w_in[:, o:]
    wg = jnp.concatenate([wg, jnp.zeros((d, LANE - wg.shape[1]), F32)], -1)
    w_in_p = jnp.concatenate([wz, _pad_q_cols(wq), wkv, wg], -1).astype(BF16)
    w_out_p = jnp.concatenate([w_out[:RWKV_DIM], _pad_o_rows(w_out[RWKV_DIM:])], 0).astype(BF16)
    qg = jnp.tile(q_norm, 2 * N_ATT_HEADS).reshape(1, N_ATT_HEADS * LANE)
    ksg = jnp.tile(k_norm[1], 2).reshape(1, LANE)
    kwg = jnp.tile(k_norm[2], 2).reshape(1, LANE)
    return w_in_p, w_out_p, qg, ksg, kwg


def _inproj_odd(x, mods, g, wts, gm, tile, tpb):
    w_in_p, _, qg, ksg, kwg = wts
    gone, gtwo, _ = gm
    outs = [(P_C, F32), (N_ATT_HEADS * LANE, BF16), (2 * LANE, F32), (2 * LANE, F32), (2 * LANE, BF16),
            (2 * LANE, F32), (2 * LANE, BF16), (LANE, F32)]
    return _row_call(_inproj_odd_kernel, x, mods, [g, w_in_p, qg, ksg, kwg, gone, gtwo], outs, tile, tpb,
                     "inproj_odd", outs_bt=[(2 * LANE, F32)] * 3)


def _rwkv_pre_kernel(z_ref, zp_ref, *rest):
    _rwkv_pre_body(z_ref[...], zp_ref[...], *rest)


def _rwkv_pre_seq_kernel(z_ref, zh_ref, *rest, tpb):
    z = z_ref[...]
    _rwkv_pre_body(z, _shifted(z, zh_ref[...], 1, tpb), *rest)


def _rwkv_pre_body(z, zp, mu_ref, w0_ref, a0_ref, kk_ref, ka_ref, wup_ref, aup_ref, gup_ref, gsum_ref,
                   r_o, w_o, k_o, v_o, kk_o, kka_o, g_o):
    zm = z + (zp - z) * mu_ref[...]
    r = zm[:, 0:RWKV_DIM]
    k = zm[:, RWKV_DIM:2 * RWKV_DIM]
    v = zm[:, 2 * RWKV_DIM:3 * RWKV_DIM]
    t12 = zm[:, 3 * RWKV_DIM:3 * RWKV_DIM + LANE]
    gd = zm[:, 3 * RWKV_DIM + LANE:P_C]
    xw = w0_ref[...] + _dot(jnp.tanh(t12), wup_ref[...])
    sp = jnp.maximum(-xw, 0.0) + jnp.log(1.0 + jnp.exp(-jnp.abs(xw)))
    w_o[...] = jnp.exp(-jnp.exp(-sp - 0.5))
    a = _sigmoid(a0_ref[...] + _dot(t12, aup_ref[...]))
    g_o[...] = _dot(_sigmoid(gd), gup_ref[...])
    kk = k * kk_ref[...]
    gsum = gsum_ref[...]
    for t in range(RWKV_DIM // LANE):
        sl = slice(t * LANE, (t + 1) * LANE)
        kt = kk[:, sl]
        nrm = jnp.maximum(jnp.sqrt(_dot_split(kt * kt, gsum)), 1e-12)
        kn = kt / nrm
        kk_o[:, sl] = kn
        kka_o[:, sl] = kn * a[:, sl]
    r_o[...] = r
    v_o[...] = v
    k_o[...] = k * (1.0 + (a - 1.0) * ka_ref[...])


def _rwkv_pre(zc, zprev, cpar, gsum, tile, tpb=None):
    mu, w0, w_up, a0, a_up, g_up, k_k, k_a = cpar
    z64 = jnp.zeros((LORA_W, RWKV_DIM), F32)
    consts = [mu.reshape(1, P_C), w0.reshape(1, -1), a0.reshape(1, -1), k_k.reshape(1, -1), k_a.reshape(1, -1),
              jnp.concatenate([w_up, z64], 0).astype(BF16), jnp.concatenate([z64, a_up], 0).astype(BF16),
              g_up.astype(BF16), gsum]
    outs = [(RWKV_DIM, F32)] * 7
    if zprev is None:
        return _row_call(functools.partial(_rwkv_pre_seq_kernel, tpb=tpb), [zc], [], consts, outs, tile, 1,
                         "rwkv_pre", halos=[0])
    return _row_call(_rwkv_pre_kernel, [zc, zprev], [], consts, outs, tile, 1, "rwkv_pre_s")


SCAN_P = 64
SCAN_VH = HEAD_DIM // 2
SCAN_SEQS = SCAN_P // RWKV_HEADS


def _scan_kernel(kk_ref, w_ref, kka_ref, k_ref, r_ref, v_ref, s0_ref, y_ref, so_ref, st, *, tc):
    ti = pl.program_id(1)

    @pl.when(ti == 0)
    def _():
        st[...] = s0_ref[...]

    lo_half = lax.broadcasted_iota(I32, (1, LANE), 1) < SCAN_P

    def lanes_of(ref, t):
        x = ref[:, t]
        xs = [x[:, h * HEAD_DIM:(h + 1) * HEAD_DIM] for h in range(RWKV_HEADS)]
        return jnp.concatenate(xs + xs, axis=0).T

    def step(t, c):
        kk, w, kka, kt, rt, vf = (lanes_of(ref, t) for ref in (kk_ref, w_ref, kka_ref, k_ref, r_ref, v_ref))
        ys = []
        for vi in range(SCAN_VH):
            s = st[vi]
            sa = -jnp.sum(s * kk, axis=0, keepdims=True)
            vrow = jnp.where(lo_half, vf[vi:vi + 1, :], vf[vi + SCAN_VH:vi + SCAN_VH + 1, :])
            sn = s * w + sa * kka + vrow * kt
            st[vi] = sn
            ys.append(jnp.sum(sn * rt, axis=0, keepdims=True))
        y_ref[t] = jnp.concatenate(ys, axis=0)
        return c

    lax.fori_loop(0, tc, step, 0)

    @pl.when(ti == pl.num_programs(1) - 1)
    def _():
        so_ref[...] = st[...]


def _scan_unlayout_y(y, b, t):
    nc = y.shape[0]
    a = jnp.concatenate([y[..., :SCAN_P], y[..., SCAN_P:]], axis=2)
    a = a.reshape(nc, t, HEAD_DIM, RWKV_HEADS, SCAN_SEQS).transpose(0, 4, 1, 3, 2)
    return a.reshape(nc * SCAN_SEQS, t, RWKV_DIM)[:b].reshape(b * t, RWKV_DIM)


def _scan_layout_state(s):
    b = s.shape[0]
    nc = -(-b // SCAN_SEQS)
    a = jnp.pad(s, ((0, nc * SCAN_SEQS - b), (0, 0), (0, 0), (0, 0)))
    a = a.reshape(nc, SCAN_SEQS, RWKV_HEADS, HEAD_DIM, HEAD_DIM).transpose(0, 3, 4, 2, 1)
    a = a.reshape(nc, HEAD_DIM, HEAD_DIM, SCAN_P)
    return jnp.concatenate([a[:, :SCAN_VH], a[:, SCAN_VH:]], -1)


def _scan_unlayout_state(st, b):
    nc = st.shape[0]
    a = jnp.concatenate([st[..., :SCAN_P], st[..., SCAN_P:]], axis=1)
    a = a.reshape(nc, HEAD_DIM, HEAD_DIM, RWKV_HEADS, SCAN_SEQS).transpose(0, 4, 3, 1, 2)
    return a.reshape(nc * SCAN_SEQS, RWKV_HEADS, HEAD_DIM, HEAD_DIM)[:b]


def _rwkv_scan(pre, s0, b, t, tc):
    r, w, k, v, kk, kka, _ = pre
    bp = -(-b // SCAN_SEQS) * SCAN_SEQS
    ops = [jnp.pad(x.reshape(b, t, RWKV_DIM), ((0, bp - b), (0, 0), (0, 0))) for x in (kk, w, kka, k, r, v)]
    s0l = _scan_layout_state(s0)
    nc = s0l.shape[0]
    kspec = pl.BlockSpec((SCAN_SEQS, tc, RWKV_DIM), lambda c, i: (c, i, 0))
    vspec = pl.BlockSpec((None, tc, SCAN_VH, LANE), lambda c, i: (c, i, 0, 0))
    sspec = pl.BlockSpec((None, SCAN_VH, HEAD_DIM, LANE), lambda c, i: (c, 0, 0, 0))
    y, so = pl.pallas_call(
        functools.partial(_scan_kernel, tc=tc),
        grid=(nc, t // tc),
        in_specs=[kspec] * 6 + [sspec],
        out_specs=[vspec, sspec],
        out_shape=[jax.ShapeDtypeStruct((nc, t, SCAN_VH, LANE), F32),
                   jax.ShapeDtypeStruct((nc, SCAN_VH, HEAD_DIM, LANE), F32)],
        scratch_shapes=[pltpu.VMEM((SCAN_VH, HEAD_DIM, LANE), F32)],
        compiler_params=_cparams("parallel", "arbitrary"),
        name="rwkv_scan",
    )(*ops, s0l)
    return _scan_unlayout_y(y, b, t), _scan_unlayout_state(so, b)


def _compress_kernel(x_ref, pe_ref, w_ref, kg_ref, gtwo_ref, o_ref):
    z = jnp.dot((x_ref[...] + pe_ref[...]).astype(BF16), w_ref[...], preferred_element_type=F32)
    o_ref[:, 0:LANE] = _group_rms(z[:, 0:LANE], gtwo_ref[...], kg_ref[...])
    o_ref[:, LANE:2 * LANE] = z[:, LANE:2 * LANE]


def _compress_weights(cmp_pe, cmp_w, k_norm_c):
    wk = cmp_w[0].reshape(CMP_BLOCK, HEAD_DIM, HEAD_DIM)
    wv = cmp_w[1].reshape(CMP_BLOCK, HEAD_DIM, HEAD_DIM)
    full = jnp.einsum("srde,st->rsdte", jnp.stack([wk, wk, wv, wv]), jnp.eye(4, dtype=F32))
    pe = jnp.stack([cmp_pe[0], cmp_pe[0], cmp_pe[1], cmp_pe[1]], axis=1)
    return (full.reshape(CMP_BLOCK * 4 * HEAD_DIM, 4 * HEAD_DIM).astype(BF16), pe.reshape(1, -1),
            jnp.tile(k_norm_c, 2).reshape(1, LANE))


def _compress(rows, cw, gtwo, tile):
    wfull, pe, kg = cw
    return _row_call(_compress_kernel, rows, [], [pe, wfull, kg, gtwo], [(2 * LANE, F32)], tile, 1, "nsa_compress")[0]


def _compress_paged_kernel(pt_ref, ident_ref, pe_ref, w_ref, kg_ref, gtwo_ref, *refs, grp, n_pages):
    del pt_ref
    page_refs, o_ref, xs = refs[:grp * n_pages], refs[-2], refs[-1]
    ident = ident_ref[...]
    for i in range(grp * n_pages):
        for half in range(2):
            xt = (page_refs[i][half * LANE:(half + 1) * LANE, :] + pe_ref[half]).astype(BF16)
            xs[half, i * PAGE_SIZE:(i + 1) * PAGE_SIZE, :] = _dot_nt(ident, xt)
    n_blk = grp * n_pages * (PAGE_SIZE // CMP_BLOCK)
    acc = [jnp.zeros((n_blk, LANE), F32) for _ in range(2)]
    for r in range(CMP_BLOCK):
        for half in range(2):
            rows = xs[half, pl.ds(r, n_blk, stride=CMP_BLOCK), :]
            acc[half] = acc[half] + jnp.dot(rows.astype(BF16), w_ref[half, r], preferred_element_type=F32)
    kc = _group_rms(acc[0], gtwo_ref[...], kg_ref[...])
    per_seq = n_blk // grp
    for g in range(grp):
        o_ref[g, 0:per_seq, 0:LANE] = kc[g * per_seq:(g + 1) * per_seq].astype(BF16)
        o_ref[g, 0:per_seq, LANE:2 * LANE] = acc[1][g * per_seq:(g + 1) * per_seq].astype(BF16)
        o_ref[g, per_seq:LANE, :] = jnp.zeros((LANE - per_seq, 2 * LANE), BF16)


def _compress_paged(cmp_t, page_table, cmp_pe, cmp_w, k_norm_c, gtwo, grp):
    b, n_pages = page_table.shape
    wk = cmp_w[0].reshape(CMP_BLOCK, HEAD_DIM, HEAD_DIM)
    wv = cmp_w[1].reshape(CMP_BLOCK, HEAD_DIM, HEAD_DIM)
    wbd = jnp.einsum("krde,ht->krhdte", jnp.stack([wk, wv]), jnp.eye(N_KV_HEADS, dtype=F32))
    wbd = wbd.reshape(2, CMP_BLOCK, LANE, LANE).astype(BF16)
    pe = jnp.tile(cmp_pe.transpose(0, 2, 1), (1, N_KV_HEADS, PAGE_SIZE // CMP_BLOCK))
    ident = jnp.eye(LANE, dtype=BF16)
    kg = jnp.tile(k_norm_c, 2).reshape(1, LANE)
    cspec = lambda a: pl.BlockSpec(a.shape, lambda i, pt: (0,) * a.ndim)
    kern = functools.partial(_compress_paged_kernel, grp=grp, n_pages=n_pages)
    return pl.pallas_call(
        kern,
        grid_spec=pltpu.PrefetchScalarGridSpec(
            num_scalar_prefetch=1, grid=(b // grp,),
            in_specs=[cspec(ident), cspec(pe), cspec(wbd), cspec(kg), cspec(gtwo)]
            + _page_specs(cmp_t, n_pages, grp),
            out_specs=pl.BlockSpec((grp, LANE, 2 * LANE), lambda i, pt: (i, 0, 0)),
            scratch_shapes=[pltpu.VMEM((2, grp * n_pages * PAGE_SIZE, LANE), F32)]),
        out_shape=jax.ShapeDtypeStruct((b, LANE, 2 * LANE), BF16),
        compiler_params=_cparams("parallel"),
        name="nsa_compress_paged",
    )(page_table, ident, pe, wbd, kg, gtwo, *([cmp_t] * (grp * n_pages)))


def _bias_cmp(rel_bias, q_starts, qb):
    q0 = jnp.asarray(q_starts, I32)[:, None, None]
    r = jnp.arange(qb)[None, :, None]
    n = jnp.arange(LANE)[None, None, :]
    return _bias_lookup(rel_bias, q0 + r - (n * CMP_BLOCK + CMP_BLOCK - 1)).transpose(0, 3, 1, 2)


def _nsa_sample_kernel(pt_ref, q_ref, gates_ref, kcv_ref, snew_ref, wnew_ref, win_ref, bias_ref, biasc_ref,
                       biasw_ref, pair_ref, *refs, grp, n_pages, t_pos, n_cmp, n_sel, w_eff):
    del pt_ref
    slc_refs, o_ref = refs[:grp * n_pages], refs[-1]
    lane1 = lax.broadcasted_iota(I32, (1, LANE), 1)
    lane8 = lax.broadcasted_iota(I32, (N_ATT_HEADS, LANE), 1)
    row_all = lax.broadcasted_iota(I32, (N_ATT_HEADS, (n_pages + 1) * LANE), 0)
    lanew = lax.broadcasted_iota(I32, (N_ATT_HEADS, w_eff), 1)
    lane1_f = lane1.astype(F32)
    bias = bias_ref[...]
    bias_now = bias[:, n_pages * LANE:n_pages * LANE + 1]
    mask_c = ((lane8 * CMP_BLOCK + CMP_BLOCK - 1) <= t_pos) & (lane8 < n_cmp)
    cur = t_pos // SEL_BLOCK
    forced = (lane1 == 0) | (lane1 == cur) | (lane1 == cur - 1)
    sel_causal = lane1 * SEL_BLOCK <= t_pos
    tail_valid = jnp.concatenate([lane1] * n_pages + [lane1 + LANE], axis=1) <= LANE
    for g in range(grp):
        q = q_ref[g]
        qf = q.astype(F32)
        s = jnp.where(mask_c, _dot_nt(q, kcv_ref[g, :, 0:LANE]) * ATT_SCALE + biasc_ref[...], NEG)
        e = jnp.exp(s - jnp.max(s, axis=1, keepdims=True))
        pc = jnp.where(mask_c, e / jnp.sum(e, axis=1, keepdims=True), 0.0)
        o_c = jnp.dot(pc.astype(BF16), kcv_ref[g, :, LANE:2 * LANE], preferred_element_type=F32)
        masks = []
        for hk in range(N_KV_HEADS):
            ps = jnp.sum(pc[hk * GQA:(hk + 1) * GQA], axis=0, keepdims=True)
            score = _dot_split(ps, pair_ref[...])
            score = jnp.where(sel_causal, jnp.where(forced, FORCE, score), NEG)
            picked = jnp.zeros((1, LANE), F32)
            for _ in range(n_sel):
                mx = jnp.max(score, axis=1, keepdims=True)
                first = jnp.min(jnp.where(score == mx, lane1_f, float(LANE)), axis=1, keepdims=True)
                hit = lane1_f == first
                picked = jnp.where(hit, 1.0, picked)
                score = jnp.where(hit, TAKEN, score)
            per_page = PAGE_SIZE // SEL_BLOCK
            tiles = []
            for p in range(n_pages + 1):
                t = jnp.zeros((1, LANE), F32)
                for a in range(per_page):
                    blk = picked[:, p * per_page + a:p * per_page + a + 1]
                    t = jnp.where(lane1 // SEL_BLOCK == a, blk, t)
                tiles.append(t)
            masks.append(jnp.concatenate(tiles, axis=1))
        valid = (jnp.where(row_all < GQA, masks[0], masks[1]) > 0.5) & tail_valid
        snew = _bf16_round(snew_ref[g])
        tiles = [_dot(q, slc_refs[g * n_pages + p][0:LANE, :]) for p in range(n_pages)]
        tiles.append(jnp.where(lane1 == 0, jnp.sum(qf * snew[:, 0:LANE], axis=1, keepdims=True), 0.0))
        s = jnp.where(valid, jnp.concatenate(tiles, axis=1) * ATT_SCALE + bias, NEG)
        e = jnp.exp(s - jnp.max(s, axis=1, keepdims=True))
        p_all = jnp.where(valid, e / jnp.sum(e, axis=1, keepdims=True), 0.0)
        o_s = p_all[:, n_pages * LANE:n_pages * LANE + 1] * snew[:, LANE:2 * LANE]
        for p in range(n_pages):
            o_s = o_s + _dot_nt(p_all[:, p * LANE:(p + 1) * LANE], slc_refs[g * n_pages + p][LANE:2 * LANE, :])
        wnew = _bf16_round(wnew_ref[g])
        valid_w = (w_eff - lanew) < WINDOW
        s_w = jnp.where(valid_w, _dot(q, win_ref[g, 0:LANE, :]) * ATT_SCALE + biasw_ref[...], NEG)
        s_n = jnp.sum(qf * wnew[:, 0:LANE], axis=1, keepdims=True) * ATT_SCALE + bias_now
        m = jnp.maximum(jnp.max(s_w, axis=1, keepdims=True), s_n)
        e_w = jnp.where(valid_w, jnp.exp(s_w - m), 0.0)
        e_n = jnp.exp(s_n - m)
        den = jnp.sum(e_w, axis=1, keepdims=True) + e_n
        o_w = _dot_nt(e_w / den, win_ref[g, LANE:2 * LANE, :]) + (e_n / den) * wnew[:, LANE:2 * LANE]
        gates = gates_ref[g]
        _head_rows_out(o_ref, g, gates[:, 0:1] * o_c + gates[:, 1:2] * o_s + gates[:, 2:3] * o_w)


def _nsa_sample(q8, gates8, kcv, snew, wnew, win_t, bias, bias_c, bias_w, pair, slc_t, page_table,
                *, t_pos, n_cmp, n_sel, grp):
    b, n_pages = page_table.shape
    w_eff = win_t.shape[2]
    gspec = lambda a: pl.BlockSpec((grp,) + a.shape[1:], lambda i, pt: (i,) + (0,) * (a.ndim - 1))
    cspec = lambda a: pl.BlockSpec(a.shape, lambda i, pt: (0,) * a.ndim)
    kern = functools.partial(_nsa_sample_kernel, grp=grp, n_pages=n_pages, t_pos=t_pos, n_cmp=n_cmp, n_sel=n_sel,
                             w_eff=w_eff)
    return pl.pallas_call(
        kern,
        grid_spec=pltpu.PrefetchScalarGridSpec(
            num_scalar_prefetch=1, grid=(b // grp,),
            in_specs=[gspec(q8), gspec(gates8), gspec(kcv), gspec(snew), gspec(wnew), gspec(win_t), cspec(bias),
                      cspec(bias_c), cspec(bias_w), cspec(pair)] + _page_specs(slc_t, n_pages, grp),
            out_specs=pl.BlockSpec((grp, N_ATT_HEADS, LANE), lambda i, pt: (i, 0, 0))),
        out_shape=jax.ShapeDtypeStruct((b, N_ATT_HEADS, LANE), BF16),
        compiler_params=_cparams("parallel"),
        name="nsa_sample",
    )(page_table, q8, gates8, kcv, snew, wnew, win_t, bias, bias_c, bias_w, pair, *([slc_t] * (grp * n_pages)))


def _post_odd_kernel(x_ref, y_ref, r_ref, k_ref, v_ref, g_ref, od_ref, gate_ref, shf_ref, scf_ref,
                     gf_ref, lnw_ref, lnb_ref, rk_ref, gtwo_ref, wo_ref, wrt_ref, br_ref, x2_ref, hf_ref, cwt_ref):
    gtwo = gtwo_ref[...]
    mix = jnp.dot(od_ref[...], wo_ref[RWKV_DIM:RWKV_DIM + N_ATT_HEADS * LANE, :], preferred_element_type=F32)
    for t in range(RWKV_DIM // LANE):
        sl = slice(t * LANE, (t + 1) * LANE)
        y = y_ref[:, sl]
        dlt = y - _dot_split(y, gtwo)
        yn = (dlt * lax.rsqrt(_dot_split(dlt * dlt, gtwo) + GN_EPS)) * lnw_ref[:, sl] + lnb_ref[:, sl]
        dot_rk = _dot_split(r_ref[:, sl] * k_ref[:, sl] * rk_ref[:, sl], gtwo) * float(HEAD_DIM)
        oc = (yn + dot_rk * v_ref[:, sl]) * g_ref[:, sl]
        mix = mix + jnp.dot(oc.astype(BF16), wo_ref[sl, :], preferred_element_type=F32)
    _post_tail(x_ref[...], mix, gate_ref[...], gf_ref[...], shf_ref[...], scf_ref[...], wrt_ref, br_ref,
               x2_ref, hf_ref, cwt_ref)


def _odd_layer(xp, xs, mp, ms, page_table, wkv0, shift0, cmp_pool, slc_pool, win_buf, rel_bias, router, experts,
               g_mix, g_ffn, w_in, w_out, cpar, r_k, ln_w, ln_b, q_norm, k_norm, cmp_pe, cmp_w):
    B, S, D = xp.shape
    Bs = xs.shape[0]
    n_pages = page_table.shape[1]
    past = n_pages * PAGE_SIZE
    wts = _odd_weights(w_in, w_out, q_norm, k_norm)
    gm = _gmats()
    gone, gtwo, _ = gm
    gsum = (gtwo.astype(F32) * HEAD_DIM).astype(BF16)
    i = jnp.arange(LANE)
    pair = jnp.where(i[:, None] // 2 == i[None, :], 1.0, 0.0).astype(BF16)
    cw = _compress_weights(cmp_pe, cmp_w, k_norm[0])
    wrt, br = router
    wg, wu, wd, lyr = experts
    g_mix = g_mix.reshape(1, D)
    post_consts = [g_ffn.reshape(1, D), ln_w.reshape(1, -1), ln_b.reshape(1, -1), r_k.reshape(1, -1), gtwo, wts[1],
                   wrt, br]
    w_eff = win_buf.shape[1]

    tp = ROW_TILE
    zc, q_p, kvc, _, kvs_b, _, kvw_b, gates, kvc_t, kvs_t, kvw_t = _inproj_odd(xp.reshape(B * S, D), [mp[0], mp[1]],
                                                                               g_mix, wts, gm, tp, S // tp)
    r3 = lambda a: a.reshape(B, S, a.shape[-1])
    pre = _rwkv_pre(zc, None, cpar, gsum, tp, S // tp)
    y, wkv_p = _rwkv_scan(pre, jnp.zeros((B, RWKV_HEADS, HEAD_DIM, HEAD_DIM), F32), B, S, SCAN_TIME_CHUNK)
    n_cmp = S // CMP_BLOCK
    kcv = _compress(kvc.reshape(B * n_cmp, CMP_BLOCK * 2 * LANE), cw, gtwo, _pick_tile(B * n_cmp, 256))
    kcv = jnp.pad(kcv.reshape(B, n_cmp, 2 * LANE), ((0, 0), (0, LANE - n_cmp), (0, 0))).astype(BF16)
    n_slc = -(-S // SEL_BLOCK)
    od = _nsa_t(r3(q_p), r3(gates), kcv, r3(kvs_b), r3(kvw_b), _bias_tiles_t(rel_bias), _bias_cmp_t(rel_bias, S),
                pair.T, n_cmp=n_cmp, n_sel=min(N_SEL_BLOCKS, n_slc))
    x2, hf, cwt = _row_call(_post_odd_kernel,
                            [xp.reshape(B * S, D), y, pre[0], pre[2], pre[3], pre[6], od.reshape(B * S, -1)],
                            [mp[2], mp[3], mp[4]], post_consts, POST_OUTS, tp, S // tp, "post_odd",
                            outs_t=[(N_EXPERTS, F32)])
    tm = min(MOE_TILE, S)
    xp3 = _moe_sorted(hf, cwt, x2, mp[5], wg, wu, wd, lyr, tm, S // tm).reshape(B, S, D)
    kv5 = lambda a, n: a.reshape(-1, n, 2, N_KV_HEADS, HEAD_DIM)
    outs_p = (wkv_p, r3(zc)[:, S - 1], _cache_rows(kvc_t), _cache_rows(kvs_t),
              _cache_rows(kvw_t[:, :, S - min(WINDOW, S):]))

    zc_s, q_s, kvc_s, kvs_s, _, kvw_s, _, gates_s, _, _, _ = _inproj_odd(xs, [ms[0], ms[1]], g_mix, wts, gm, Bs, 1)
    pre_s = _rwkv_pre(zc_s, shift0, cpar, gsum, Bs)
    y_s, wkv_s = _rwkv_scan(pre_s, wkv0, Bs, 1, 1)
    n_pool = cmp_pool.shape[0]
    cmp_t = cmp_pool.transpose(0, 2, 3, 4, 1).reshape(n_pool, 2 * LANE, PAGE_SIZE)
    kcv_s = _compress_paged(cmp_t, page_table, cmp_pe, cmp_w, k_norm[0], gtwo, _pick_group(Bs, SAMPLE_GROUP))
    n_cmp_s = (past + 1) // CMP_BLOCK
    slc_t = slc_pool.transpose(0, 2, 3, 4, 1).reshape(n_pool, 2 * LANE, PAGE_SIZE)
    win_t = win_buf.transpose(0, 2, 3, 4, 1).reshape(Bs, 2 * LANE, w_eff)
    gates8 = jnp.pad(gates_s[:, :3 * N_ATT_HEADS].reshape(Bs, 3, N_ATT_HEADS).transpose(0, 2, 1),
                     ((0, 0), (0, 0), (0, LANE - 3)))
    n_slc_s = -(-(past + 1) // SEL_BLOCK)
    od_s = _nsa_sample(q_s.reshape(Bs, N_ATT_HEADS, LANE), gates8, kcv_s, kvs_s[:, None, :], kvw_s[:, None, :],
                       win_t, _bias_row(rel_bias, past), _bias_cmp(rel_bias, [past], 1)[0, :, 0, :],
                       _bias_lookup(rel_bias, w_eff - jnp.arange(w_eff)).T, pair, slc_t, page_table,
                       t_pos=past, n_cmp=n_cmp_s, n_sel=min(N_SEL_BLOCKS, n_slc_s),
                       grp=_pick_group(Bs, SAMPLE_GROUP)).reshape(Bs, -1)
    x2s, hfs, cwts = _row_call(_post_odd_kernel, [xs, y_s, pre_s[0], pre_s[2], pre_s[3], pre_s[6], od_s],
                               [ms[2], ms[3], ms[4]], post_consts, POST_OUTS, Bs, 1, "post_odd_s",
                               outs_t=[(N_EXPERTS, F32)])
    xs3 = _moe(hfs, cwts.T, x2s, ms[5], wg, wu, wd, lyr, Bs, 1)
    win_new = jnp.concatenate([win_buf[:, 1:], kv5(kvw_s, 1)], axis=1)
    outs_s = (wkv_s, zc_s, kv5(kvc_s, 1), kv5(kvs_s, 1), win_new)
    return xp3, xs3, outs_p, outs_s


def _mods(c_p, c_s, w_all, b, layer):
    nb = c_p.shape[0]
    m = _ada(jnp.concatenate([c_p, c_s], 0), w_all, b, layer)
    parts = jnp.split(m, 6, axis=-1)
    return [p[:nb, None, :] for p in parts], [p[None, nb:, :] for p in parts]


def _forward(x_prompt, x_sample, c_prompt, c_sample, page_table, cache_a_kv, cache_a_kidx, state_b_conv,
             state_c_wkv, state_c_shift, cache_d_cmp, cache_d_slc, cache_d_win, rel_bias, w_router, b_router,
             w_ada, b_ada, g_norm_mix, g_norm_ffn, w_expert_gate, w_expert_up, w_expert_down, e_w_in, e_w_out,
             a_q_norm, a_k_norm, b_conv_w, b_conv_b, o_w_in, o_w_out, c_mu, c_w0, c_w_up, c_a0, c_a_up,
             c_g_up, c_k_k, c_k_a, c_r_k, c_ln_w, c_ln_b, d_q_norm, d_k_norm, d_cmp_pe, d_cmp_w):
    assert w_ada.shape[0] == 2 and e_w_in.shape[0] == 1 and o_w_in.shape[0] == 1
    B, S, D = x_prompt.shape
    Bs = x_sample.shape[0]
    assert x_sample.shape[1] == 1
    xp, xs = x_prompt, x_sample.reshape(Bs, D)
    router = (w_router.T, b_router.reshape(N_EXPERTS, 1))
    n_pool = cache_a_kv.shape[1]
    wbf = tuple(w.astype(BF16) for w in (w_expert_gate, w_expert_up, w_expert_down))
    experts = lambda l: wbf + (l,)

    mp, ms = _mods(c_prompt, c_sample, w_ada, b_ada[0], 0)
    xp, xs, ep, es = _even_layer(xp, xs, mp, ms, page_table, cache_a_kv[0], cache_a_kidx[0], state_b_conv[0],
                                 rel_bias, router, experts(0), g_norm_mix[0], g_norm_ffn[0], e_w_in[0], e_w_out[0],
                                 a_q_norm[0], a_k_norm[0], b_conv_w[0], b_conv_b[0])
    mp, ms = _mods(c_prompt, c_sample, w_ada, b_ada[1], 1)
    cpar = (c_mu[0], c_w0[0], c_w_up[0], c_a0[0], c_a_up[0], c_g_up[0], c_k_k[0], c_k_a[0])
    xp, xs, op, os_ = _odd_layer(xp, xs, mp, ms, page_table, state_c_wkv[0], state_c_shift[0], cache_d_cmp[0],
                                 cache_d_slc[0], cache_d_win[0], rel_bias, router, experts(1), g_norm_mix[1],
                                 g_norm_ffn[1], o_w_in[0], o_w_out[0], cpar, c_r_k[0].reshape(-1), c_ln_w[0],
                                 c_ln_b[0], d_q_norm[0], d_k_norm[0], d_cmp_pe[0], d_cmp_w[0])
    stack = lambda ts: tuple(a[None] for a in ts)
    return (xp, xs.reshape(Bs, 1, D)) + stack(ep) + stack(op) + stack(es) + stack(os_)


def kernel(x_prompt, x_sample, c_prompt, c_sample, page_table, cache_a_kv, cache_a_kidx, state_b_conv, state_c_wkv, state_c_shift, cache_d_cmp, cache_d_slc, cache_d_win, rel_bias, w_router, b_router, w_ada, b_ada, g_norm_mix, g_norm_ffn, w_expert_gate, w_expert_up, w_expert_down, e_w_in, e_w_out, a_q_norm, a_k_norm, b_conv_w, b_conv_b, o_w_in, o_w_out, c_mu, c_w0, c_w_up, c_a0, c_a_up, c_g_up, c_k_k, c_k_a, c_r_k, c_ln_w, c_ln_b, d_q_norm, d_k_norm, d_cmp_pe, d_cmp_w):
    return _forward(x_prompt, x_sample, c_prompt, c_sample, page_table, cache_a_kv, cache_a_kidx, state_b_conv,
                    state_c_wkv, state_c_shift, cache_d_cmp, cache_d_slc, cache_d_win, rel_bias, w_router, b_router,
                    w_ada, b_ada, g_norm_mix, g_norm_ffn, w_expert_gate, w_expert_up, w_expert_down, e_w_in, e_w_out,
                    a_q_norm, a_k_norm, b_conv_w, b_conv_b, o_w_in, o_w_out, c_mu, c_w0, c_w_up, c_a0, c_a_up,
                    c_g_up, c_k_k, c_k_a, c_r_k, c_ln_w, c_ln_b, d_q_norm, d_k_norm, d_cmp_pe, d_cmp_w)
```

```python
import functools
import math

import jax
import jax.numpy as jnp
from jax import lax
from jax.experimental import pallas as pl
from jax.experimental.pallas import tpu as pltpu

F32 = jnp.float32
BF16 = jnp.bfloat16
I32 = jnp.int32

LANE = 128
HEAD_DIM = 64
N_ATT_HEADS = 8
N_KV_HEADS = 2
GQA = N_ATT_HEADS // N_KV_HEADS
IDX_HEADS = 4
IDX_DIM = 64
TOPK_MAX = 256
CONV_CH = 512
RWKV_HEADS = 8
RWKV_DIM = RWKV_HEADS * HEAD_DIM
LORA_W = 64
LORA_A = 64
LORA_G = 128
GN_EPS = 64e-5
CMP_BLOCK = 32
SEL_BLOCK = 64
N_SEL_BLOCKS = 8
WINDOW = 512
N_BUCKETS = 32
MAX_DISTANCE = 128
N_EXPERTS = 16
N_GROUPS = 4
EXPERTS_PER_GROUP = N_EXPERTS // N_GROUPS
PAGE_SIZE = 128
RMS_EPS = 1e-6
NEG = -1e30
FORCE = 1e9
TAKEN = -3e38
ATT_SCALE = HEAD_DIM ** -0.5
VMEM_LIMIT = 56 * 1024 * 1024
ROW_TILE = 512
MOE_TILE = 1024
SCAN_TIME_CHUNK = 64
SAMPLE_GROUP = 4


def _cparams(*sem):
    return pltpu.CompilerParams(dimension_semantics=sem, vmem_limit_bytes=VMEM_LIMIT)


def _pick_tile(rows, pref):
    t = min(pref, rows)
    while rows % t or (t % 8 and t != rows):
        t -= 1
    return t


def _pick_group(n, pref):
    g = min(pref, n)
    while n % g:
        g -= 1
    return g


def _const_spec(a):
    nd = a.ndim
    return pl.BlockSpec(a.shape, lambda *_: (0,) * nd)


def _dot(a, b):
    return jnp.dot(a.astype(BF16), b.astype(BF16), preferred_element_type=F32)


def _dot_nt(a, b):
    return lax.dot_general(a.astype(BF16), b.astype(BF16), (((1,), (1,)), ((), ())),
                           preferred_element_type=F32)


def _dot_split(x, m):
    hi = x.astype(BF16)
    r1 = x - hi.astype(F32)
    mid = r1.astype(BF16)
    lo = (r1 - mid.astype(F32)).astype(BF16)
    return (jnp.dot(hi, m, preferred_element_type=F32) + jnp.dot(mid, m, preferred_element_type=F32)
            + jnp.dot(lo, m, preferred_element_type=F32))


def _bf16_round(x):
    return x.astype(BF16).astype(F32)


def _sigmoid(x):
    return 1.0 / (1.0 + jnp.exp(-x))


def _silu(x):
    return x * _sigmoid(x)


def _modulate(x, g, shift, scale):
    y = x * lax.rsqrt(jnp.mean(x * x, axis=-1, keepdims=True) + RMS_EPS)
    return (y * g) * (1.0 + scale) + shift


def _group_rms(t, gmat, gain):
    ms = _dot_split(t * t, gmat)
    return (t * lax.rsqrt(ms + RMS_EPS)) * gain


def _ada_kernel(c_ref, w_ref, b_ref, o_ref):
    o_ref[...] = _dot(_silu(c_ref[...]), w_ref[...]) + b_ref[...]


def _ada(c, w_all, b, layer):
    r, d = c.shape
    n = w_all.shape[2]
    tn = 512
    return pl.pallas_call(
        _ada_kernel,
        grid=(n // tn,),
        in_specs=[pl.BlockSpec((r, d), lambda j: (0, 0)),
                  pl.BlockSpec((None, d, tn), lambda j: (layer, 0, j)),
                  pl.BlockSpec((1, tn), lambda j: (0, j))],
        out_specs=pl.BlockSpec((r, tn), lambda j: (0, j)),
        out_shape=jax.ShapeDtypeStruct((r, n), F32),
        compiler_params=_cparams("parallel"),
        name="ada_mod",
    )(c, w_all, b.reshape(1, n))


E_Q0, E_KV0, E_QI0, E_MISC0, E_BG0, E_CG0, E_XIN0, E_END = 0, 1024, 1280, 1792, 1920, 2432, 2944, 3456


def _inproj_even_kernel(x_ref, shift_ref, scale_ref, g_ref, w_ref, qg_ref, kg_ref, gone_ref, gtwo_ref,
                        q_ref, kv_ref, kvb_ref, qi_ref, misc_ref, miscb_ref, bg_ref, u_ref, kvt_ref, misct_ref):
    h = _modulate(x_ref[...], g_ref[...], shift_ref[...], scale_ref[...])
    z = jnp.dot(h.astype(BF16), w_ref[...], preferred_element_type=F32)
    gone = gone_ref[...]
    for t in range(N_ATT_HEADS):
        sl = slice(t * LANE, (t + 1) * LANE)
        q_ref[:, sl] = _group_rms(z[:, E_Q0 + t * LANE:E_Q0 + (t + 1) * LANE], gone, qg_ref[:, sl]).astype(BF16)
    k = _group_rms(z[:, E_KV0:E_KV0 + LANE], gtwo_ref[...], kg_ref[...])
    v = z[:, E_KV0 + LANE:E_KV0 + 2 * LANE]
    kv_ref[:, 0:LANE] = k
    kv_ref[:, LANE:2 * LANE] = v
    kvb_ref[:, 0:LANE] = k.astype(BF16)
    kvb_ref[:, LANE:2 * LANE] = v.astype(BF16)
    qi_ref[...] = z[:, E_QI0:E_MISC0].astype(BF16)
    misc = z[:, E_MISC0:E_BG0]
    misc_ref[...] = misc
    miscb_ref[...] = misc.astype(BF16)
    bg_ref[...] = z[:, E_BG0:E_CG0]
    u_ref[...] = z[:, E_CG0:E_XIN0] * z[:, E_XIN0:E_END]
    kvt_ref[0:LANE, :] = k.T
    kvt_ref[LANE:2 * LANE, :] = v.T
    misct_ref[...] = misc.T


HALO = 8


def _shifted(z, halo, k, tpb):
    first = (pl.program_id(0) % tpb) == 0
    row = lax.broadcasted_iota(I32, z.shape, 0)
    out = pltpu.roll(z, k, axis=0)
    for j in range(k):
        prev = jnp.where(first, 0.0, halo[HALO - k + j:HALO - k + j + 1, :])
        out = jnp.where(row == j, prev, out)
    return out


def _row_call(kernel, xs, mods, consts, outs, tile, tpb, name, outs_t=(), outs_bt=(), halos=()):
    if not isinstance(xs, (list, tuple)):
        xs = [xs]
    rows = xs[0].shape[0]
    n_tiles = rows // tile
    in_specs = [pl.BlockSpec((tile, x.shape[1]), lambda t: (t, 0)) for x in xs]
    in_specs += [pl.BlockSpec((HALO, xs[i].shape[1]), lambda t: (jnp.maximum(t * (tile // HALO) - 1, 0), 0))
                 for i in halos]
    xs = list(xs) + [xs[i] for i in halos]
    for m in mods:
        in_specs.append(pl.BlockSpec((None,) + m.shape[1:], lambda t: (t // tpb, 0, 0)))
    in_specs += [_const_spec(c) for c in consts]
    out_specs = [pl.BlockSpec((tile, w), lambda t: (t, 0)) for (w, _) in outs]
    out_shape = [jax.ShapeDtypeStruct((rows, w), dt) for (w, dt) in outs]
    out_specs += [pl.BlockSpec((hh, tile), lambda t: (0, t)) for (hh, _) in outs_t]
    out_shape += [jax.ShapeDtypeStruct((hh, rows), dt) for (hh, dt) in outs_t]
    out_specs += [pl.BlockSpec((None, hh, tile), lambda t: (t // tpb, 0, t % tpb)) for (hh, _) in outs_bt]
    out_shape += [jax.ShapeDtypeStruct((n_tiles // tpb, hh, tile * tpb), dt) for (hh, dt) in outs_bt]
    return pl.pallas_call(kernel, grid=(n_tiles,), in_specs=in_specs, out_specs=out_specs, out_shape=out_shape,
                          compiler_params=_cparams("parallel"), name=name)(*xs, *mods, *consts)


def _pad_q_cols(wq):
    d = wq.shape[0]
    w = wq.reshape(d, N_ATT_HEADS, HEAD_DIM)
    z = jnp.zeros_like(w)
    lo = jnp.concatenate([w, z], -1)
    hi = jnp.concatenate([z, w], -1)
    sel = (jnp.arange(N_ATT_HEADS) >= GQA)[None, :, None]
    return jnp.where(sel, hi, lo).reshape(d, N_ATT_HEADS * LANE)


def _pad_o_rows(wo):
    d = wo.shape[1]
    w = wo.reshape(N_ATT_HEADS, HEAD_DIM, d)
    z = jnp.zeros_like(w)
    lo = jnp.concatenate([w, z], 1)
    hi = jnp.concatenate([z, w], 1)
    sel = (jnp.arange(N_ATT_HEADS) >= GQA)[:, None, None]
    return jnp.where(sel, hi, lo).reshape(N_ATT_HEADS * LANE, d)


def _gmats():
    i = jnp.arange(LANE)
    gone = jnp.full((LANE, LANE), 1.0 / HEAD_DIM, F32).astype(BF16)
    gtwo = jnp.where((i[:, None] // HEAD_DIM) == (i[None, :] // HEAD_DIM), 1.0 / HEAD_DIM, 0.0).astype(BF16)
    tri = jnp.where(i[:, None] <= i[None, :], 1.0, 0.0).astype(BF16)
    return gone, gtwo, tri


def _even_weights(w_in, w_out, q_norm, k_norm):
    d = w_in.shape[0]
    a_q, a_kv = N_ATT_HEADS * HEAD_DIM, 2 * N_KV_HEADS * HEAD_DIM
    o = 0
    wq = w_in[:, o:o + a_q]; o += a_q
    wkv = w_in[:, o:o + a_kv]; o += a_kv
    wqi = w_in[:, o:o + IDX_HEADS * IDX_DIM]; o += IDX_HEADS * IDX_DIM
    wki = w_in[:, o:o + IDX_DIM]; o += IDX_DIM
    wwi = w_in[:, o:o + IDX_HEADS]; o += IDX_HEADS
    wrest = w_in[:, o:]
    wqi = jnp.concatenate([wqi.reshape(d, IDX_HEADS, IDX_DIM), jnp.zeros((d, IDX_HEADS, LANE - IDX_DIM), F32)],
                          -1).reshape(d, IDX_HEADS * LANE)
    wmisc = jnp.concatenate([wki, wwi, jnp.zeros((d, LANE - IDX_DIM - IDX_HEADS), F32)], -1)
    w_in_p = jnp.concatenate([_pad_q_cols(wq), wkv, wqi, wmisc, wrest], -1).astype(BF16)
    w_out_p = jnp.concatenate([_pad_o_rows(w_out[:a_q]), w_out[a_q:]], 0).astype(BF16)
    qg = jnp.tile(q_norm, 2 * N_ATT_HEADS).reshape(1, N_ATT_HEADS * LANE)
    kg = jnp.tile(k_norm, 2).reshape(1, LANE)
    return w_in_p, w_out_p, qg, kg


def _inproj_even(x, mods, g, wts, gm, tile, tpb):
    w_in_p, _, qg, kg = wts
    gone, gtwo, _ = gm
    outs = [(N_ATT_HEADS * LANE, BF16), (2 * LANE, F32), (2 * LANE, BF16), (IDX_HEADS * LANE, BF16),
            (LANE, F32), (LANE, BF16), (CONV_CH, F32), (CONV_CH, F32)]
    return _row_call(_inproj_even_kernel, x, mods, [g, w_in_p, qg, kg, gone, gtwo], outs, tile, tpb, "inproj_even",
                     outs_bt=[(2 * LANE, F32), (LANE, F32)])


def _t5_bucket(dist):
    dist = jnp.maximum(dist, 0)
    exact = N_BUCKETS // 2
    far = exact + (jnp.log(jnp.maximum(dist, 1).astype(F32) / exact)
                   / math.log(MAX_DISTANCE / exact) * (N_BUCKETS - exact)).astype(I32)
    return jnp.where(dist < exact, dist, jnp.minimum(far, N_BUCKETS - 1))


def _bias_lookup(rel_bias, dist):
    onehot = (_t5_bucket(dist)[..., None] == jnp.arange(N_BUCKETS)).astype(F32)
    return jnp.einsum("...k,kh->...h", onehot, rel_bias, precision=lax.Precision.HIGHEST)


def _bias_row(rel_bias, t_pos):
    return _bias_lookup(rel_bias, t_pos - jnp.arange(t_pos + LANE)).T


def _bias_tiles(rel_bias, qb):
    r = jnp.arange(qb)[:, None]
    c = jnp.arange(LANE)[None, :]
    tiles = [_bias_lookup(rel_bias, d * LANE + r - c) for d in range(3)]
    return jnp.stack(tiles).transpose(0, 3, 1, 2)


def _stack_heads(q_ref, hk):
    return jnp.concatenate([q_ref[:, (hk * GQA + g) * LANE:(hk * GQA + g + 1) * LANE] for g in range(GQA)], axis=0)


QB = LANE
QW = GQA * QB


def _sub_sum(x):
    return jnp.sum(x, axis=0, keepdims=True)


def _flash_t_pair(blocks, qs, bias_ref, carry, acc_ref):
    logits = [[jnp.where(mk[hk], _dot_nt(kb, qs[hk]) + bias_ref[dsel, hk], NEG) for (kb, _, mk, dsel) in blocks]
              for hk in range(N_KV_HEADS)]
    out = []
    for hk in range(N_KV_HEADS):
        m, l = carry[hk]
        m_new = m
        for s in logits[hk]:
            m_new = jnp.maximum(m_new, jnp.max(s, axis=0, keepdims=True))
        alpha = jnp.exp(m - m_new)
        l = alpha * l
        pv = None
        for s, (_, vt, _, _) in zip(logits[hk], blocks):
            p = jnp.exp(s - m_new)
            l = l + _sub_sum(p)
            d = jnp.dot(vt, p.astype(BF16), preferred_element_type=F32)
            pv = d if pv is None else pv + d
        acc_ref[hk] = alpha * acc_ref[hk] + pv
        out.append((m_new, l))
    return tuple(out)


def _scaled_queries(q_ref, hk):
    return (_stack_heads(q_ref, hk).astype(F32) * ATT_SCALE).astype(BF16)


def _pair_loop(nblk, body, init):
    def body2(jj, c):
        return body(2 * jj + 1, body(2 * jj, c))
    return lax.fori_loop(0, (nblk + 1) // 2, body2, init)


def _flash_t_init():
    return (jnp.full((1, QW), NEG, F32), jnp.zeros((1, QW), F32))


def _tile_lanes(x):
    return jnp.concatenate([x] * GQA, axis=1)


def _write_heads_t(o_ref, o_ts):
    lane = lax.broadcasted_iota(I32, (QB, LANE), 1)
    for hk in range(N_KV_HEADS):
        valid = (lane // HEAD_DIM) == hk
        for g in range(GQA):
            h = hk * GQA + g
            o = o_ts[hk][:, g * QB:(g + 1) * QB].T
            o_ref[:, h * LANE:(h + 1) * LANE] = jnp.where(valid, o, 0.0).astype(BF16)


def _dsa_t_kernel(q_ref, qi_ref, misc_ref, kidx_ref, k_ref, vt_ref, bias_ref, trit_ref, o_ref, key_s, acc_s,
                  *, n_keep):
    i = pl.program_id(1)
    q0 = i * QB
    nblk = i + 1
    krow = lax.broadcasted_iota(I32, (LANE, QB), 0)
    qcol = lax.broadcasted_iota(I32, (LANE, QB), 1)
    misc_t = misc_ref[...].T
    wis = [_bf16_round(misc_t[IDX_DIM + h:IDX_DIM + h + 1, :]) for h in range(IDX_HEADS)]
    qi = qi_ref[...]
    idx_scale = (IDX_HEADS * IDX_DIM) ** -0.5

    def causal(j):
        return j * LANE + krow <= q0 + qcol

    def pass_a(j, c):
        kb = kidx_ref[pl.ds(pl.multiple_of(j * LANE, LANE), LANE), :]
        acc = jnp.zeros((LANE, QB), F32)
        for h in range(IDX_HEADS):
            acc = acc + _bf16_round(jnp.maximum(_dot_nt(kb, qi[:, h * LANE:(h + 1) * LANE]), 0.0)) * wis[h]
        key_s[j] = _order_keys(jnp.where(causal(j), acc * idx_scale, NEG))
        return c

    _pair_loop(nblk, pass_a, 0)

    def count(pred):
        def body(j, a):
            return a + jnp.where(pred(key_s[j]), 1.0, 0.0)
        return _sub_sum(_pair_loop(nblk, body, jnp.zeros((LANE, QB), F32)))

    keep = float(n_keep)
    thr = jnp.where(count(lambda k: k >= 0) >= keep, jnp.int32(0), jnp.int32(-2 ** 31))

    def search(it, thr):
        cand = thr | lax.shift_left(jnp.int32(1), jnp.int32(30) - it)
        return jnp.where(count(lambda k: k >= cand) >= keep, cand, thr)

    thr = lax.fori_loop(0, 31, search, thr)
    need = keep - count(lambda k: k > thr)
    trit = trit_ref[...]

    def pass_c(j, run):
        key = key_s[j]
        eq = key == thr
        eqf = jnp.where(eq, 1.0, 0.0)
        cum = jnp.dot(trit, eqf.astype(BF16), preferred_element_type=F32) + run
        sel = ((key > thr) | (eq & (cum <= need))) & causal(j)
        key_s[j] = jnp.where(sel, 1, 0)
        return run + _sub_sum(eqf)

    _pair_loop(nblk, pass_c, jnp.zeros((1, QB), F32))

    qs = [_scaled_queries(q_ref, hk) for hk in range(N_KV_HEADS)]
    acc_s[...] = jnp.zeros_like(acc_s)

    def pass_d(jj, carry):
        blocks = []
        for j in (2 * jj, 2 * jj + 1):
            kb = k_ref[pl.ds(pl.multiple_of(j * LANE, LANE), LANE), :]
            blocks.append((kb, vt_ref[j], [_tile_lanes(key_s[j] > 0)] * N_KV_HEADS, jnp.clip(i - j, 0, 2)))
        return _flash_t_pair(blocks, qs, bias_ref, carry, acc_s)

    res = lax.fori_loop(0, (nblk + 1) // 2, pass_d, tuple(_flash_t_init() for _ in range(N_KV_HEADS)))
    _write_heads_t(o_ref, [acc_s[hk] / res[hk][1] for hk in range(N_KV_HEADS)])


def _bias_tiles_t(rel_bias):
    t = _bias_tiles(rel_bias, QB)
    t = t.reshape(3, N_KV_HEADS, GQA, QB, LANE).transpose(0, 1, 4, 2, 3)
    return t.reshape(3, N_KV_HEADS, LANE, QW)


def _blocks_t(x):
    b, s, w = x.shape
    return x.reshape(b, s // LANE, LANE, w).transpose(0, 1, 3, 2)


def _dsa_t(q, qi, misc, kidx_b, kv_b, bias_t, trit, *, n_keep):
    b, s, _ = q.shape
    vt = _blocks_t(kv_b[:, :, LANE:])
    qspec = lambda w: pl.BlockSpec((None, QB, w), lambda bi, i: (bi, i, 0))
    kspec = pl.BlockSpec((None, s, LANE), lambda bi, i: (bi, 0, 0))
    return pl.pallas_call(
        functools.partial(_dsa_t_kernel, n_keep=n_keep),
        grid=(b, s // QB),
        in_specs=[qspec(N_ATT_HEADS * LANE), qspec(IDX_HEADS * LANE), qspec(LANE), kspec, kspec,
                  pl.BlockSpec((None, s // LANE, LANE, LANE), lambda bi, i: (bi, 0, 0, 0)),
                  _const_spec(bias_t), _const_spec(trit)],
        out_specs=qspec(N_ATT_HEADS * LANE),
        out_shape=jax.ShapeDtypeStruct((b, s, N_ATT_HEADS * LANE), BF16),
        scratch_shapes=[pltpu.VMEM((s // LANE, LANE, QB), I32), pltpu.VMEM((N_KV_HEADS, LANE, QW), F32)],
        compiler_params=_cparams("parallel", "parallel"),
        name="dsa_attention_t",
    )(q, qi, misc, kidx_b, kv_b, vt, bias_t, trit)


def _dot_split_rhs(m, x):
    hi = x.astype(BF16)
    r1 = x - hi.astype(F32)
    mid = r1.astype(BF16)
    lo = (r1 - mid.astype(F32)).astype(BF16)
    return (jnp.dot(m, hi, preferred_element_type=F32) + jnp.dot(m, mid, preferred_element_type=F32)
            + jnp.dot(m, lo, preferred_element_type=F32))


def _nsa_t_kernel(q_ref, gates_ref, kc_ref, vct_ref, ks_ref, vst_ref, kw_ref, vwt_ref, bias_ref, biasc_ref,
                  pairt_ref, o_ref, acc_s, *, n_cmp, n_sel):
    i = pl.program_id(1)
    q0 = i * QB
    nblk = i + 1
    heads = range(N_KV_HEADS)
    krow = lax.broadcasted_iota(I32, (LANE, QB), 0)
    qcol = lax.broadcasted_iota(I32, (LANE, QB), 1)
    t_pos = q0 + qcol
    krow_f = krow.astype(F32)
    qs = [_scaled_queries(q_ref, hk) for hk in heads]
    gates_t = gates_ref[...].T

    def gate_row(br, hk):
        return jnp.concatenate([gates_t[br * N_ATT_HEADS + hk * GQA + g:br * N_ATT_HEADS + hk * GQA + g + 1, :]
                                for g in range(GQA)], axis=1)

    mask_c = _tile_lanes(((krow * CMP_BLOCK + CMP_BLOCK - 1) <= t_pos) & (krow < n_cmp))
    cur = t_pos // SEL_BLOCK
    forced = (krow == 0) | (krow == cur) | (krow == cur - 1)
    sel_causal = krow * SEL_BLOCK <= t_pos
    o_cmp, picked = [], []
    for hk in heads:
        s = jnp.where(mask_c, _dot_nt(kc_ref[...], qs[hk]) + biasc_ref[hk], NEG)
        e = jnp.exp(s - jnp.max(s, axis=0, keepdims=True))
        p = jnp.where(mask_c, e / _sub_sum(e), 0.0)
        o_cmp.append(jnp.dot(vct_ref[...], p.astype(BF16), preferred_element_type=F32))
        ps = p[:, 0:QB]
        for g in range(1, GQA):
            ps = ps + p[:, g * QB:(g + 1) * QB]
        score = _dot_split_rhs(pairt_ref[...], ps)
        score = jnp.where(sel_causal, jnp.where(forced, FORCE, score), NEG)
        pk = jnp.zeros((LANE, QB), F32)
        for _ in range(n_sel):
            mx = jnp.max(score, axis=0, keepdims=True)
            first = jnp.min(jnp.where(score == mx, krow_f, float(LANE)), axis=0, keepdims=True)
            hit = krow_f == first
            pk = jnp.where(hit, 1.0, pk)
            score = jnp.where(hit, TAKEN, score)
        picked.append(pk.astype(BF16))

    def key_block(k_ref, vt_ref, j):
        return k_ref[pl.ds(pl.multiple_of(j * LANE, LANE), LANE), :], vt_ref[j]

    erow = lax.broadcasted_iota(I32, (LANE, LANE), 0)
    ecol = lax.broadcasted_iota(I32, (LANE, LANE), 1)
    acc_s[...] = jnp.zeros_like(acc_s)

    def slc_body(jj, carry):
        blocks = []
        for j in (2 * jj, 2 * jj + 1):
            kb, vt = key_block(ks_ref, vst_ref, j)
            expand = jnp.where(ecol == 2 * j + erow // SEL_BLOCK, 1.0, 0.0).astype(BF16)
            causal = j * LANE + krow <= t_pos
            masks = [_tile_lanes((jnp.dot(expand, picked[hk], preferred_element_type=F32) > 0.5) & causal)
                     for hk in heads]
            blocks.append((kb, vt, masks, jnp.clip(i - j, 0, 2)))
        return _flash_t_pair(blocks, qs, bias_ref, carry, acc_s.at[0])

    res_s = lax.fori_loop(0, (nblk + 1) // 2, slc_body, tuple(_flash_t_init() for _ in heads))

    lo = jnp.maximum(i - WINDOW // LANE - 1, 0)

    def win_body(jj, carry):
        blocks = []
        for j in (lo + 2 * jj, lo + 2 * jj + 1):
            kb, vt = key_block(kw_ref, vwt_ref, j)
            dist = t_pos - (j * LANE + krow)
            mask = _tile_lanes((dist >= 0) & (dist < WINDOW))
            blocks.append((kb, vt, [mask] * N_KV_HEADS, jnp.clip(i - j, 0, 2)))
        return _flash_t_pair(blocks, qs, bias_ref, carry, acc_s.at[1])

    res_w = lax.fori_loop(0, (i - lo + 2) // 2, win_body, tuple(_flash_t_init() for _ in heads))

    _write_heads_t(o_ref, [gate_row(0, hk) * o_cmp[hk] + gate_row(1, hk) * (acc_s[0, hk] / res_s[hk][1])
                           + gate_row(2, hk) * (acc_s[1, hk] / res_w[hk][1]) for hk in heads])


def _bias_cmp_t(rel_bias, s):
    t = _bias_cmp(rel_bias, [j * QB for j in range(s // QB)], QB)
    t = t.reshape(s // QB, N_KV_HEADS, GQA, QB, LANE).transpose(0, 1, 4, 2, 3)
    return t.reshape(s // QB, N_KV_HEADS, LANE, QW)


def _nsa_t(q, gates, kcv, kvs_b, kvw_b, bias_t, bias_c, pair_t, *, n_cmp, n_sel):
    b, s, _ = q.shape
    vct = kcv[:, :, LANE:].transpose(0, 2, 1)
    qspec = lambda w: pl.BlockSpec((None, QB, w), lambda bi, i: (bi, i, 0))
    kspec = pl.BlockSpec((None, s, LANE), lambda bi, i: (bi, 0, 0))
    vspec = pl.BlockSpec((None, s // LANE, LANE, LANE), lambda bi, i: (bi, 0, 0, 0))
    cspec = pl.BlockSpec((None, LANE, LANE), lambda bi, i: (bi, 0, 0))
    return pl.pallas_call(
        functools.partial(_nsa_t_kernel, n_cmp=n_cmp, n_sel=n_sel),
        grid=(b, s // QB),
        in_specs=[qspec(N_ATT_HEADS * LANE), qspec(LANE), cspec, cspec, kspec, vspec, kspec, vspec,
                  _const_spec(bias_t), pl.BlockSpec((None,) + bias_c.shape[1:], lambda bi, i: (i, 0, 0, 0)),
                  _const_spec(pair_t)],
        out_specs=qspec(N_ATT_HEADS * LANE),
        out_shape=jax.ShapeDtypeStruct((b, s, N_ATT_HEADS * LANE), BF16),
        scratch_shapes=[pltpu.VMEM((2, N_KV_HEADS, LANE, QW), F32)],
        compiler_params=_cparams("parallel", "parallel"),
        name="nsa_attention_t",
    )(q, gates, kcv, vct, kvs_b, _blocks_t(kvs_b[:, :, LANE:]), kvw_b, _blocks_t(kvw_b[:, :, LANE:]),
      bias_t, bias_c, pair_t)


def _select_top(keys, n_keep, tri):
    keep = float(n_keep)
    n = keys[0].shape[1]

    def count(pred):
        return [jnp.sum(jnp.where(pred(g, k), 1.0, 0.0), axis=1, keepdims=True) for g, k in enumerate(keys)]

    int_min = jnp.int32(-2 ** 31)
    thr = tuple(jnp.where(c >= keep, jnp.int32(0), int_min) for c in count(lambda g, k: k >= 0))

    def search(it, thr):
        bit = lax.shift_left(jnp.int32(1), jnp.int32(30) - it)
        cand = [t | bit for t in thr]
        cnt = count(lambda g, k: k >= cand[g])
        return tuple(jnp.where(c >= keep, cd, t) for c, cd, t in zip(cnt, cand, thr))

    thr = lax.fori_loop(0, 31, search, thr)
    need = [keep - c for c in count(lambda g, k: k > thr[g])]
    sels = []
    for g, k in enumerate(keys):
        run = jnp.zeros((1, 1), F32)
        parts = []
        for t in range(n // LANE):
            kt = k[:, t * LANE:(t + 1) * LANE]
            eq = kt == thr[g]
            eqf = jnp.where(eq, 1.0, 0.0)
            cum = jnp.dot(eqf.astype(BF16), tri, preferred_element_type=F32) + run
            parts.append((kt > thr[g]) | (eq & (cum <= need[g])))
            run = run + jnp.sum(eqf, axis=1, keepdims=True)
        sels.append(jnp.concatenate(parts, axis=1))
    return sels


def _order_keys(score):
    score = jnp.where(score == 0.0, 0.0, score)
    bits = lax.bitcast_convert_type(score, I32)
    return jnp.where(bits < 0, bits ^ jnp.int32(0x7FFFFFFF), bits)


def _head_rows_out(o_ref, g, acc):
    rowh = lax.broadcasted_iota(I32, (N_ATT_HEADS, LANE), 0)
    laneh = lax.broadcasted_iota(I32, (N_ATT_HEADS, LANE), 1)
    o_ref[g] = jnp.where((laneh // HEAD_DIM) == (rowh // GQA), acc, 0.0).astype(BF16)


def _dsa_sample_kernel(pt_ref, q_ref, qi_ref, wi_ref, knew_ref, kvnew_ref, bias_ref, tri_ref, *refs,
                       grp, n_pages, n_keep):
    del pt_ref
    ki_refs, kv_refs, o_ref = refs[:grp * n_pages], refs[grp * n_pages:2 * grp * n_pages], refs[-1]
    lane1 = lax.broadcasted_iota(I32, (1, LANE), 1)
    idx_scale = (IDX_HEADS * IDX_DIM) ** -0.5
    keys = []
    for g in range(grp):
        qi = qi_ref[g]
        wi = _bf16_round(wi_ref[g])
        tiles = []
        for p in range(n_pages):
            rel = _bf16_round(jnp.maximum(_dot(qi, ki_refs[g * n_pages + p][...]), 0.0))
            tiles.append(jnp.sum(rel * wi, axis=0, keepdims=True) * idx_scale)
        rel_new = _bf16_round(jnp.maximum(jnp.sum(qi.astype(F32) * _bf16_round(knew_ref[g]), axis=1, keepdims=True),
                                          0.0))
        sc_new = jnp.sum(rel_new * wi, axis=0, keepdims=True) * idx_scale
        tiles.append(jnp.where(lane1 == 0, sc_new, NEG))
        keys.append(_order_keys(jnp.concatenate(tiles, axis=1)))
    sels = _select_top(keys, n_keep, tri_ref[...])
    bias = bias_ref[...]
    for g in range(grp):
        q = q_ref[g]
        kvnew = _bf16_round(kvnew_ref[g])
        tiles = [_dot(q, kv_refs[g * n_pages + p][0:LANE, :]) for p in range(n_pages)]
        s_new = jnp.sum(q.astype(F32) * kvnew[:, 0:LANE], axis=1, keepdims=True)
        tiles.append(jnp.where(lane1 == 0, s_new, 0.0))
        valid = sels[g] & (jnp.concatenate([lane1] * n_pages + [lane1 + LANE], axis=1) <= LANE)
        s = jnp.where(valid, jnp.concatenate(tiles, axis=1) * ATT_SCALE + bias, NEG)
        e = jnp.exp(s - jnp.max(s, axis=1, keepdims=True))
        p_all = jnp.where(valid, e / jnp.sum(e, axis=1, keepdims=True), 0.0)
        acc = p_all[:, n_pages * LANE:n_pages * LANE + 1] * kvnew[:, LANE:2 * LANE]
        for p in range(n_pages):
            acc = acc + _dot_nt(p_all[:, p * LANE:(p + 1) * LANE], kv_refs[g * n_pages + p][LANE:2 * LANE, :])
        _head_rows_out(o_ref, g, acc)


def _page_specs(pool_t, n_pages, grp):
    r, c = pool_t.shape[1:]
    return [pl.BlockSpec((None, r, c), lambda i, pt, g=g, p=p: (pt[i * grp + g, p], 0, 0))
            for g in range(grp) for p in range(n_pages)]


def _dsa_sample(q8, qi8, wi8, knew, kvnew, bias, tri, ki_t, kv_t, page_table, *, n_keep, grp):
    b, n_pages = page_table.shape
    gspec = lambda a: pl.BlockSpec((grp,) + a.shape[1:], lambda i, pt: (i,) + (0,) * (a.ndim - 1))
    cspec = lambda a: pl.BlockSpec(a.shape, lambda i, pt: (0,) * a.ndim)
    kern = functools.partial(_dsa_sample_kernel, grp=grp, n_pages=n_pages, n_keep=n_keep)
    return pl.pallas_call(
        kern,
        grid_spec=pltpu.PrefetchScalarGridSpec(
            num_scalar_prefetch=1, grid=(b // grp,),
            in_specs=[gspec(q8), gspec(qi8), gspec(wi8), gspec(knew), gspec(kvnew), cspec(bias), cspec(tri)]
            + _page_specs(ki_t, n_pages, grp) + _page_specs(kv_t, n_pages, grp),
            out_specs=pl.BlockSpec((grp, N_ATT_HEADS, LANE), lambda i, pt: (i, 0, 0))),
        out_shape=jax.ShapeDtypeStruct((b, N_ATT_HEADS, LANE), BF16),
        compiler_params=_cparams("parallel"),
        name="dsa_sample",
    )(page_table, q8, qi8, wi8, knew, kvnew, bias, tri, *([ki_t] * (grp * n_pages)), *([kv_t] * (grp * n_pages)))


def _route(hf, wrt, br):
    logits = _dot_nt(wrt, hf)
    s = _sigmoid(logits)
    sel = s + br
    rows = [sel[e:e + 1, :] for e in range(N_EXPERTS)]
    grp = []
    for g in range(N_GROUPS):
        a = rows[g * EXPERTS_PER_GROUP:(g + 1) * EXPERTS_PER_GROUP]
        best = None
        for i in range(EXPERTS_PER_GROUP):
            for j in range(i + 1, EXPERTS_PER_GROUP):
                v = a[i] + a[j]
                best = v if best is None else jnp.maximum(best, v)
        grp.append(best)
    gbest = jnp.zeros_like(grp[0], dtype=I32)
    cur = grp[0]
    for g in range(1, N_GROUPS):
        better = grp[g] > cur
        gbest = jnp.where(better, g, gbest)
        cur = jnp.where(better, grp[g], cur)
    picked = []
    for g in range(N_GROUPS):
        a = rows[g * EXPERTS_PER_GROUP:(g + 1) * EXPERTS_PER_GROUP]
        for j in range(EXPERTS_PER_GROUP):
            rank = jnp.zeros_like(a[j])
            for jj in range(EXPERTS_PER_GROUP):
                if jj != j:
                    ahead = (a[jj] > a[j]) | (a[jj] == a[j]) if jj < j else (a[jj] > a[j])
                    rank = rank + jnp.where(ahead, 1.0, 0.0)
            e = g * EXPERTS_PER_GROUP + j
            picked.append(jnp.where((gbest == g) & (rank < 2.0), s[e:e + 1, :], 0.0))
    den = picked[0]
    for p in picked[1:]:
        den = den + p
    return jnp.concatenate([p / den for p in picked], axis=0)


def _post_tail(x, mix, gate, gf, shf, scf, wrt_ref, br_ref, x2_ref, hf_ref, cwt_ref):
    x2 = x + gate * mix
    x2_ref[...] = x2
    hf = _modulate(x2, gf, shf, scf)
    hf_ref[...] = hf.astype(BF16)
    cwt_ref[...] = _route(hf, wrt_ref[...], br_ref[...])


def _post_even_kernel(x_ref, oa_ref, bg_ref, u_ref, um1_ref, um2_ref, *rest):
    _post_even_body(x_ref, oa_ref, bg_ref, u_ref[...], um1_ref[...], um2_ref[...], *rest)


def _post_even_seq_kernel(x_ref, oa_ref, bg_ref, u_ref, uh_ref, *rest, tpb):
    u = u_ref[...]
    _post_even_body(x_ref, oa_ref, bg_ref, u, _shifted(u, uh_ref[...], 1, tpb), _shifted(u, uh_ref[...], 2, tpb),
                    *rest)


def _post_even_body(x_ref, oa_ref, bg_ref, u, um1, um2, gate_ref, shf_ref, scf_ref,
                    gf_ref, cw_ref, cb_ref, wo_ref, wrt_ref, br_ref, x2_ref, hf_ref, cwt_ref):
    cw = cw_ref[...]
    y = cb_ref[...] + cw[0:1, :] * um2
    y = y + cw[1:2, :] * um1
    y = y + cw[2:3, :] * u
    n_a = N_ATT_HEADS * LANE
    mix = (jnp.dot(oa_ref[...], wo_ref[0:n_a, :], preferred_element_type=F32)
           + jnp.dot((bg_ref[...] * y).astype(BF16), wo_ref[n_a:n_a + CONV_CH, :], preferred_element_type=F32))
    _post_tail(x_ref[...], mix, gate_ref[...], gf_ref[...], shf_ref[...], scf_ref[...], wrt_ref, br_ref,
               x2_ref, hf_ref, cwt_ref)


POST_OUTS = [(1024, F32), (1024, BF16)]


def _moe_kernel(hf_ref, cw_ref, x2_ref, gate_ref, wg_ref, wu_ref, wd_ref, o_ref, acc_ref):
    e = pl.program_id(1)

    @pl.when(e == 0)
    def _():
        acc_ref[...] = jnp.zeros_like(acc_ref)

    hf = hf_ref[...]
    hmid = _silu(_dot(hf, wg_ref[...])) * _dot(hf, wu_ref[...])
    cw = cw_ref[...]
    lane = lax.broadcasted_iota(I32, cw.shape, 1)
    wcol = jnp.sum(jnp.where(lane == e, cw, 0.0), axis=1, keepdims=True)
    acc_ref[...] += _dot(hmid, wd_ref[...]) * wcol

    @pl.when(e == N_EXPERTS - 1)
    def _():
        o_ref[...] = x2_ref[...] + gate_ref[...] * acc_ref[...]


def _moe(hf, cw, x2, gate, wg, wu, wd, layer, tile, tpb):
    rows, d = x2.shape
    de = wg.shape[3]
    return pl.pallas_call(
        _moe_kernel,
        grid=(rows // tile, N_EXPERTS),
        in_specs=[pl.BlockSpec((tile, d), lambda t, e: (t, 0)),
                  pl.BlockSpec((tile, N_EXPERTS), lambda t, e: (t, 0)),
                  pl.BlockSpec((tile, d), lambda t, e: (t, 0)),
                  pl.BlockSpec((None,) + gate.shape[1:], lambda t, e: (t // tpb, 0, 0)),
                  pl.BlockSpec((None, None, d, de), lambda t, e: (layer, e, 0, 0)),
                  pl.BlockSpec((None, None, d, de), lambda t, e: (layer, e, 0, 0)),
                  pl.BlockSpec((None, None, de, d), lambda t, e: (layer, e, 0, 0))],
        out_specs=pl.BlockSpec((tile, d), lambda t, e: (t, 0)),
        out_shape=jax.ShapeDtypeStruct((rows, d), F32),
        scratch_shapes=[pltpu.VMEM((tile, d), F32)],
        compiler_params=_cparams("parallel", "arbitrary"),
        name="moe_dense",
    )(hf, cw, x2, gate, wg, wu, wd)


MOE_WIN = LANE


def _moe_sorted_kernel(plan_ref, hf_ref, cw_ref, cwt_ref, x2_ref, gate_ref, tril_ref, g16_ref, g16t_ref,
                       wg_ref, wu_ref, wd_ref, o_ref, p_s, pt_s, xs_s, cws_s, ys_s):
    ti = pl.program_id(0)
    e = pl.program_id(1)
    t = hf_ref.shape[0]
    g = e // EXPERTS_PER_GROUP

    @pl.when(e == 0)
    def _():
        tril = tril_ref[...]
        memb_col = jnp.dot(jnp.where(cw_ref[...] > 0.0, 1.0, 0.0).astype(BF16), g16_ref[...],
                           preferred_element_type=F32) > 0.5
        memb_row = jnp.dot(g16t_ref[...], jnp.where(cwt_ref[...] > 0.0, 1.0, 0.0).astype(BF16),
                           preferred_element_type=F32) > 0.5
        mcf = jnp.where(memb_col, 1.0, 0.0)
        mrf = jnp.where(memb_row, 1.0, 0.0)
        rank_col = jnp.dot(tril, mcf.astype(BF16), preferred_element_type=F32)
        rank_row = _dot_nt(mrf, tril)
        lane = lax.broadcasted_iota(I32, (t, LANE), 1)
        row8 = lax.broadcasted_iota(I32, (8, t), 0)
        base_col = jnp.zeros((t, LANE), F32)
        base_row = jnp.zeros((8, t), F32)
        for gg in range(N_GROUPS):
            start = plan_ref[ti, gg].astype(F32)
            base_col = jnp.where(lane == gg, start, base_col)
            base_row = jnp.where(row8 == gg, start, base_row)
        slot_col = jnp.sum(mcf * (base_col + rank_col - 1.0), axis=1, keepdims=True)
        slot_row = jnp.sum(mrf * (base_row + rank_row - 1.0), axis=0, keepdims=True)
        col_iota = lax.broadcasted_iota(I32, (MOE_WIN, t), 1).astype(F32)
        row_iota = lax.broadcasted_iota(I32, (MOE_WIN, t), 0).astype(F32)
        for c in range(t // MOE_WIN):
            rows = slice(c * MOE_WIN, (c + 1) * MOE_WIN)
            p_s[rows, :] = jnp.where(slot_row == row_iota + float(c * MOE_WIN), 1.0, 0.0).astype(BF16)
            pt_s[rows, :] = jnp.where(slot_col[rows] == col_iota, 1.0, 0.0).astype(BF16)
        p = p_s[...]
        xs_s[...] = jnp.dot(p, hf_ref[...], preferred_element_type=F32).astype(BF16)
        cws_s[...] = _dot_split_rhs(p, cw_ref[...])
        ys_s[...] = jnp.zeros_like(ys_s)

    lane16 = lax.broadcasted_iota(I32, (MOE_WIN, N_EXPERTS), 1)

    def window(c, carry):
        rows = pl.ds(pl.multiple_of(c * MOE_WIN, MOE_WIN), MOE_WIN)
        xw = xs_s[rows, :]
        hmid = _silu(_dot(xw, wg_ref[...])) * _dot(xw, wu_ref[...])
        wcol = jnp.sum(jnp.where(lane16 == e, cws_s[rows, :], 0.0), axis=1, keepdims=True)
        ys_s[rows, :] = ys_s[rows, :] + _dot(hmid, wd_ref[...]) * wcol
        return carry

    lax.fori_loop(plan_ref[ti, N_GROUPS + g], plan_ref[ti, 2 * N_GROUPS + g], window, 0)

    @pl.when(e == N_EXPERTS - 1)
    def _():
        ys = ys_s[...]
        hi = ys.astype(BF16)
        lo = (ys - hi.astype(F32)).astype(BF16)
        pt = pt_s[...]
        back = jnp.dot(pt, hi, preferred_element_type=F32) + jnp.dot(pt, lo, preferred_element_type=F32)
        o_ref[...] = x2_ref[...] + gate_ref[...] * back


def _moe_plan(cwt, tile):
    n = cwt.shape[1]
    member = (cwt.reshape(N_GROUPS, EXPERTS_PER_GROUP, n // tile, tile) > 0.0).any(axis=1)
    cnt = member.sum(axis=-1).astype(I32).T
    start = jnp.cumsum(cnt, axis=1) - cnt
    lo = start // MOE_WIN
    hi = jnp.where(cnt > 0, (start + cnt + MOE_WIN - 1) // MOE_WIN, lo)
    return jnp.concatenate([start, lo, hi], axis=1)


def _moe_sorted(hf, cwt, x2, gate, wg, wu, wd, layer, tile, tpb):
    rows, d = x2.shape
    de = wg.shape[3]
    i = jnp.arange(tile)
    tril = jnp.where(i[None, :] <= i[:, None], 1.0, 0.0).astype(BF16)
    e16 = jnp.arange(N_EXPERTS)
    g16 = jnp.where(e16[:, None] // EXPERTS_PER_GROUP == jnp.arange(LANE)[None, :], 1.0, 0.0).astype(BF16)
    g16t = jnp.where(jnp.arange(8)[:, None] == e16[None, :] // EXPERTS_PER_GROUP, 1.0, 0.0).astype(BF16)
    cspec = lambda a: pl.BlockSpec(a.shape, lambda t, e, plan: (0,) * a.ndim)
    return pl.pallas_call(
        _moe_sorted_kernel,
        grid_spec=pltpu.PrefetchScalarGridSpec(
            num_scalar_prefetch=1, grid=(rows // tile, N_EXPERTS),
            in_specs=[pl.BlockSpec((tile, d), lambda t, e, plan: (t, 0)),
                      pl.BlockSpec((tile, N_EXPERTS), lambda t, e, plan: (t, 0)),
                      pl.BlockSpec((N_EXPERTS, tile), lambda t, e, plan: (0, t)),
                      pl.BlockSpec((tile, d), lambda t, e, plan: (t, 0)),
                      pl.BlockSpec((None,) + gate.shape[1:], lambda t, e, plan: (t // tpb, 0, 0)),
                      cspec(tril), cspec(g16), cspec(g16t),
                      pl.BlockSpec((None, None, d, de), lambda t, e, plan: (layer, e, 0, 0)),
                      pl.BlockSpec((None, None, d, de), lambda t, e, plan: (layer, e, 0, 0)),
                      pl.BlockSpec((None, None, de, d), lambda t, e, plan: (layer, e, 0, 0))],
            out_specs=pl.BlockSpec((tile, d), lambda t, e, plan: (t, 0)),
            scratch_shapes=[pltpu.VMEM((tile, tile), BF16), pltpu.VMEM((tile, tile), BF16),
                            pltpu.VMEM((tile, d), BF16), pltpu.VMEM((tile, N_EXPERTS), F32),
                            pltpu.VMEM((tile, d), F32)]),
        out_shape=jax.ShapeDtypeStruct((rows, d), F32),
        compiler_params=_cparams("parallel", "arbitrary"),
        name="moe_sorted",
    )(_moe_plan(cwt, tile), hf, cwt.T, cwt, x2, gate, tril, g16, g16t, wg, wu, wd)


def _cache_rows(kv_t):
    b, _, s = kv_t.shape
    return kv_t.reshape(b, 2, N_KV_HEADS, HEAD_DIM, s).transpose(0, 4, 1, 2, 3)


def _even_layer(xp, xs, mp, ms, page_table, kv_pool, kidx_pool, conv_buf, rel_bias, router, experts,
                g_mix, g_ffn, w_in, w_out, q_norm, k_norm, conv_w, conv_b):
    B, S, D = xp.shape
    Bs = xs.shape[0]
    past = page_table.shape[1] * PAGE_SIZE
    wts = _even_weights(w_in, w_out, q_norm, k_norm)
    gm = _gmats()
    wrt, br = router
    wg, wu, wd, lyr = experts
    g_mix = g_mix.reshape(1, D)
    g_ffn = g_ffn.reshape(1, D)
    post_consts = [g_ffn, conv_w, conv_b.reshape(1, CONV_CH), wts[1], wrt, br]

    tp = ROW_TILE
    q_p, _, kv_b, qi_p, misc, misc_b, bg, u, kv_t, misc_t = _inproj_even(xp.reshape(B * S, D), [mp[0], mp[1]],
                                                                        g_mix, wts, gm, tp, S // tp)
    r3 = lambda a: a.reshape(B, S, a.shape[-1])
    oa = _dsa_t(r3(q_p), r3(qi_p), r3(misc), r3(misc_b), r3(kv_b), _bias_tiles_t(rel_bias), gm[2].T,
                n_keep=min(TOPK_MAX, S // 4))
    u3 = r3(u)
    x2, hf, cwt = _row_call(functools.partial(_post_even_seq_kernel, tpb=S // tp),
                            [xp.reshape(B * S, D), oa.reshape(B * S, -1), bg, u],
                            [mp[2], mp[3], mp[4]], post_consts, POST_OUTS, tp, S // tp, "post_even",
                            outs_t=[(N_EXPERTS, F32)], halos=[3])
    tm = min(MOE_TILE, S)
    xp3 = _moe_sorted(hf, cwt, x2, mp[5], wg, wu, wd, lyr, tm, S // tm).reshape(B, S, D)
    outs_p = (_cache_rows(kv_t), misc_t[:, :IDX_DIM].transpose(0, 2, 1), u3[:, S - 2:])

    q_s, kv_fs, _, qi_s, misc_s, _, bg_s, u_s, _, _ = _inproj_even(xs, [ms[0], ms[1]], g_mix, wts, gm, Bs, 1)
    n_pool = kv_pool.shape[0]
    kv_t = kv_pool.transpose(0, 2, 3, 4, 1).reshape(n_pool, 2 * LANE, PAGE_SIZE)
    ki_t = kidx_pool.transpose(0, 2, 1)
    qi8 = jnp.pad(qi_s.reshape(Bs, IDX_HEADS, LANE)[:, :, :IDX_DIM], ((0, 0), (0, N_ATT_HEADS - IDX_HEADS), (0, 0)))
    wi8 = jnp.pad(misc_s[:, IDX_DIM:IDX_DIM + IDX_HEADS], ((0, 0), (0, N_ATT_HEADS - IDX_HEADS)))[:, :, None]
    oa_s = _dsa_sample(q_s.reshape(Bs, N_ATT_HEADS, LANE), qi8, wi8, misc_s[:, None, :IDX_DIM], kv_fs[:, None, :],
                       _bias_row(rel_bias, past), gm[2], ki_t, kv_t, page_table,
                       n_keep=min(TOPK_MAX, (past + 1) // 4), grp=_pick_group(Bs, SAMPLE_GROUP)).reshape(Bs, -1)
    x2s, hfs, cwts = _row_call(_post_even_kernel, [xs, oa_s, bg_s, u_s, conv_buf[:, 1], conv_buf[:, 0]],
                               [ms[2], ms[3], ms[4]], post_consts, POST_OUTS, Bs, 1, "post_even_s",
                               outs_t=[(N_EXPERTS, F32)])
    xs3 = _moe(hfs, cwts.T, x2s, ms[5], wg, wu, wd, lyr, Bs, 1)
    outs_s = (kv_fs.reshape(Bs, 1, 2, N_KV_HEADS, HEAD_DIM), misc_s[:, None, :IDX_DIM],
              jnp.concatenate([conv_buf[:, 1:], u_s[:, None, :]], axis=1))
    return xp3, xs3, outs_p, outs_s


O_Z0, O_Q0, O_KVC0, O_KVS0, O_KVW0, O_G0, O_END = 0, 1792, 2816, 3072, 3328, 3584, 3712
P_C = 3 * RWKV_DIM + LORA_W + LORA_A + LORA_G


def _inproj_odd_kernel(x_ref, shift_ref, scale_ref, g_ref, w_ref, qg_ref, ksg_ref, kwg_ref, gone_ref, gtwo_ref,
                       zc_ref, q_ref, kvc_ref, kvs_ref, kvsb_ref, kvw_ref, kvwb_ref, gates_ref,
                       kvct_ref, kvst_ref, kvwt_ref):
    h = _modulate(x_ref[...], g_ref[...], shift_ref[...], scale_ref[...])
    z = jnp.dot(h.astype(BF16), w_ref[...], preferred_element_type=F32)
    zc_ref[...] = z[:, O_Z0:O_Q0]
    gone = gone_ref[...]
    gtwo = gtwo_ref[...]
    for t in range(N_ATT_HEADS):
        sl = slice(t * LANE, (t + 1) * LANE)
        q_ref[:, sl] = _group_rms(z[:, O_Q0 + t * LANE:O_Q0 + (t + 1) * LANE], gone, qg_ref[:, sl]).astype(BF16)
    kvc_ref[...] = z[:, O_KVC0:O_KVS0]
    kvct_ref[0:LANE, :] = z[:, O_KVC0:O_KVC0 + LANE].T
    kvct_ref[LANE:2 * LANE, :] = z[:, O_KVC0 + LANE:O_KVS0].T
    for base, gain_ref, f_ref, b_ref, t_ref in ((O_KVS0, ksg_ref, kvs_ref, kvsb_ref, kvst_ref),
                                                (O_KVW0, kwg_ref, kvw_ref, kvwb_ref, kvwt_ref)):
        k = _group_rms(z[:, base:base + LANE], gtwo, gain_ref[...])
        v = z[:, base + LANE:base + 2 * LANE]
        t_ref[0:LANE, :] = k.T
        t_ref[LANE:2 * LANE, :] = v.T
        f_ref[:, 0:LANE] = k
        f_ref[:, LANE:2 * LANE] = v
        b_ref[:, 0:LANE] = k.astype(BF16)
        b_ref[:, LANE:2 * LANE] = v.astype(BF16)
    gates_ref[...] = _sigmoid(z[:, O_G0:O_END])


def _odd_weights(w_in, w_out, q_norm, k_norm):
    d = w_in.shape[0]
    a_q, a_kv = N_ATT_HEADS * HEAD_DIM, 2 * N_KV_HEADS * HEAD_DIM
    o = P_C
    wz = w_in[:, :o]
    wq = w_in[:, o:o + a_q]; o += a_q
    wkv = w_in[:, o:o + 3 * a_kv]; o += 3 * a_kv
    wg = w_in[:, o:]
    wg = jnp.concatenate([wg, jnp.zeros((d, LANE - wg.shape[1]), F32)], -1)
    w_in_p = jnp.concatenate([wz, _pad_q_cols(wq), wkv, wg], -1).astype(BF16)
    w_out_p = jnp.concatenate([w_out[:RWKV_DIM], _pad_o_rows(w_out[RWKV_DIM:])], 0).astype(BF16)
    qg = jnp.tile(q_norm, 2 * N_ATT_HEADS).reshape(1, N_ATT_HEADS * LANE)
    ksg = jnp.tile(k_norm[1], 2).reshape(1, LANE)
    kwg = jnp.tile(k_norm[2], 2).reshape(1, LANE)
    return w_in_p, w_out_p, qg, ksg, kwg


def _inproj_odd(x, mods, g, wts, gm, tile, tpb):
    w_in_p, _, qg, ksg, kwg = wts
    gone, gtwo, _ = gm
    outs = [(P_C, F32), (N_ATT_HEADS * LANE, BF16), (2 * LANE, F32), (2 * LANE, F32), (2 * LANE, BF16),
            (2 * LANE, F32), (2 * LANE, BF16), (LANE, F32)]
    return _row_call(_inproj_odd_kernel, x, mods, [g, w_in_p, qg, ksg, kwg, gone, gtwo], outs, tile, tpb,
                     "inproj_odd", outs_bt=[(2 * LANE, F32)] * 3)


def _rwkv_pre_kernel(z_ref, zp_ref, *rest):
    _rwkv_pre_body(z_ref[...], zp_ref[...], *rest)


def _rwkv_pre_seq_kernel(z_ref, zh_ref, *rest, tpb):
    z = z_ref[...]
    _rwkv_pre_body(z, _shifted(z, zh_ref[...], 1, tpb), *rest)


def _rwkv_pre_body(z, zp, mu_ref, w0_ref, a0_ref, kk_ref, ka_ref, wup_ref, aup_ref, gup_ref, gsum_ref,
                   r_o, w_o, k_o, v_o, kk_o, kka_o, g_o):
    zm = z + (zp - z) * mu_ref[...]
    r = zm[:, 0:RWKV_DIM]
    k = zm[:, RWKV_DIM:2 * RWKV_DIM]
    v = zm[:, 2 * RWKV_DIM:3 * RWKV_DIM]
    t12 = zm[:, 3 * RWKV_DIM:3 * RWKV_DIM + LANE]
    gd = zm[:, 3 * RWKV_DIM + LANE:P_C]
    xw = w0_ref[...] + _dot(jnp.tanh(t12), wup_ref[...])
    sp = jnp.maximum(-xw, 0.0) + jnp.log(1.0 + jnp.exp(-jnp.abs(xw)))
    w_o[...] = jnp.exp(-jnp.exp(-sp - 0.5))
    a = _sigmoid(a0_ref[...] + _dot(t12, aup_ref[...]))
    g_o[...] = _dot(_sigmoid(gd), gup_ref[...])
    kk = k * kk_ref[...]
    gsum = gsum_ref[...]
    for t in range(RWKV_DIM // LANE):
        sl = slice(t * LANE, (t + 1) * LANE)
        kt = kk[:, sl]
        nrm = jnp.maximum(jnp.sqrt(_dot_split(kt * kt, gsum)), 1e-12)
        kn = kt / nrm
        kk_o[:, sl] = kn
        kka_o[:, sl] = kn * a[:, sl]
    r_o[...] = r
    v_o[...] = v
    k_o[...] = k * (1.0 + (a - 1.0) * ka_ref[...])


def _rwkv_pre(zc, zprev, cpar, gsum, tile, tpb=None):
    mu, w0, w_up, a0, a_up, g_up, k_k, k_a = cpar
    z64 = jnp.zeros((LORA_W, RWKV_DIM), F32)
    consts = [mu.reshape(1, P_C), w0.reshape(1, -1), a0.reshape(1, -1), k_k.reshape(1, -1), k_a.reshape(1, -1),
              jnp.concatenate([w_up, z64], 0).astype(BF16), jnp.concatenate([z64, a_up], 0).astype(BF16),
              g_up.astype(BF16), gsum]
    outs = [(RWKV_DIM, F32)] * 7
    if zprev is None:
        return _row_call(functools.partial(_rwkv_pre_seq_kernel, tpb=tpb), [zc], [], consts, outs, tile, 1,
                         "rwkv_pre", halos=[0])
    return _row_call(_rwkv_pre_kernel, [zc, zprev], [], consts, outs, tile, 1, "rwkv_pre_s")


SCAN_P = 64
SCAN_VH = HEAD_DIM // 2
SCAN_SEQS = SCAN_P // RWKV_HEADS


def _scan_kernel(kk_ref, w_ref, kka_ref, k_ref, r_ref, v_ref, s0_ref, y_ref, so_ref, st, *, tc):
    ti = pl.program_id(1)

    @pl.when(ti == 0)
    def _():
        st[...] = s0_ref[...]

    lo_half = lax.broadcasted_iota(I32, (1, LANE), 1) < SCAN_P

    def lanes_of(ref, t):
        x = ref[:, t]
        xs = [x[:, h * HEAD_DIM:(h + 1) * HEAD_DIM] for h in range(RWKV_HEADS)]
        return jnp.concatenate(xs + xs, axis=0).T

    def step(t, c):
        kk, w, kka, kt, rt, vf = (lanes_of(ref, t) for ref in (kk_ref, w_ref, kka_ref, k_ref, r_ref, v_ref))
        ys = []
        for vi in range(SCAN_VH):
            s = st[vi]
            sa = -jnp.sum(s * kk, axis=0, keepdims=True)
            vrow = jnp.where(lo_half, vf[vi:vi + 1, :], vf[vi + SCAN_VH:vi + SCAN_VH + 1, :])
            sn = s * w + sa * kka + vrow * kt
            st[vi] = sn
            ys.append(jnp.sum(sn * rt, axis=0, keepdims=True))
        y_ref[t] = jnp.concatenate(ys, axis=0)
        return c

    lax.fori_loop(0, tc, step, 0)

    @pl.when(ti == pl.num_programs(1) - 1)
    def _():
        so_ref[...] = st[...]


def _scan_unlayout_y(y, b, t):
    nc = y.shape[0]
    a = jnp.concatenate([y[..., :SCAN_P], y[..., SCAN_P:]], axis=2)
    a = a.reshape(nc, t, HEAD_DIM, RWKV_HEADS, SCAN_SEQS).transpose(0, 4, 1, 3, 2)
    return a.reshape(nc * SCAN_SEQS, t, RWKV_DIM)[:b].reshape(b * t, RWKV_DIM)


def _scan_layout_state(s):
    b = s.shape[0]
    nc = -(-b // SCAN_SEQS)
    a = jnp.pad(s, ((0, nc * SCAN_SEQS - b), (0, 0), (0, 0), (0, 0)))
    a = a.reshape(nc, SCAN_SEQS, RWKV_HEADS, HEAD_DIM, HEAD_DIM).transpose(0, 3, 4, 2, 1)
    a = a.reshape(nc, HEAD_DIM, HEAD_DIM, SCAN_P)
    return jnp.concatenate([a[:, :SCAN_VH], a[:, SCAN_VH:]], -1)


def _scan_unlayout_state(st, b):
    nc = st.shape[0]
    a = jnp.concatenate([st[..., :SCAN_P], st[..., SCAN_P:]], axis=1)
    a = a.reshape(nc, HEAD_DIM, HEAD_DIM, RWKV_HEADS, SCAN_SEQS).transpose(0, 4, 3, 1, 2)
    return a.reshape(nc * SCAN_SEQS, RWKV_HEADS, HEAD_DIM, HEAD_DIM)[:b]


def _rwkv_scan(pre, s0, b, t, tc):
    r, w, k, v, kk, kka, _ = pre
    bp = -(-b // SCAN_SEQS) * SCAN_SEQS
    ops = [jnp.pad(x.reshape(b, t, RWKV_DIM), ((0, bp - b), (0, 0), (0, 0))) for x in (kk, w, kka, k, r, v)]
    s0l = _scan_layout_state(s0)
    nc = s0l.shape[0]
    kspec = pl.BlockSpec((SCAN_SEQS, tc, RWKV_DIM), lambda c, i: (c, i, 0))
    vspec = pl.BlockSpec((None, tc, SCAN_VH, LANE), lambda c, i: (c, i, 0, 0))
    sspec = pl.BlockSpec((None, SCAN_VH, HEAD_DIM, LANE), lambda c, i: (c, 0, 0, 0))
    y, so = pl.pallas_call(
        functools.partial(_scan_kernel, tc=tc),
        grid=(nc, t // tc),
        in_specs=[kspec] * 6 + [sspec],
        out_specs=[vspec, sspec],
        out_shape=[jax.ShapeDtypeStruct((nc, t, SCAN_VH, LANE), F32),
                   jax.ShapeDtypeStruct((nc, SCAN_VH, HEAD_DIM, LANE), F32)],
        scratch_shapes=[pltpu.VMEM((SCAN_VH, HEAD_DIM, LANE), F32)],
        compiler_params=_cparams("parallel", "arbitrary"),
        name="rwkv_scan",
    )(*ops, s0l)
    return _scan_unlayout_y(y, b, t), _scan_unlayout_state(so, b)


def _compress_kernel(x_ref, pe_ref, w_ref, kg_ref, gtwo_ref, o_ref):
    z = jnp.dot((x_ref[...] + pe_ref[...]).astype(BF16), w_ref[...], preferred_element_type=F32)
    o_ref[:, 0:LANE] = _group_rms(z[:, 0:LANE], gtwo_ref[...], kg_ref[...])
    o_ref[:, LANE:2 * LANE] = z[:, LANE:2 * LANE]


def _compress_weights(cmp_pe, cmp_w, k_norm_c):
    wk = cmp_w[0].reshape(CMP_BLOCK, HEAD_DIM, HEAD_DIM)
    wv = cmp_w[1].reshape(CMP_BLOCK, HEAD_DIM, HEAD_DIM)
    full = jnp.einsum("srde,st->rsdte", jnp.stack([wk, wk, wv, wv]), jnp.eye(4, dtype=F32))
    pe = jnp.stack([cmp_pe[0], cmp_pe[0], cmp_pe[1], cmp_pe[1]], axis=1)
    return (full.reshape(CMP_BLOCK * 4 * HEAD_DIM, 4 * HEAD_DIM).astype(BF16), pe.reshape(1, -1),
            jnp.tile(k_norm_c, 2).reshape(1, LANE))


def _compress(rows, cw, gtwo, tile):
    wfull, pe, kg = cw
    return _row_call(_compress_kernel, rows, [], [pe, wfull, kg, gtwo], [(2 * LANE, F32)], tile, 1, "nsa_compress")[0]


def _compress_paged_kernel(pt_ref, ident_ref, pe_ref, w_ref, kg_ref, gtwo_ref, *refs, grp, n_pages):
    del pt_ref
    page_refs, o_ref, xs = refs[:grp * n_pages], refs[-2], refs[-1]
    ident = ident_ref[...]
    for i in range(grp * n_pages):
        for half in range(2):
            xt = (page_refs[i][half * LANE:(half + 1) * LANE, :] + pe_ref[half]).astype(BF16)
            xs[half, i * PAGE_SIZE:(i + 1) * PAGE_SIZE, :] = _dot_nt(ident, xt)
    n_blk = grp * n_pages * (PAGE_SIZE // CMP_BLOCK)
    acc = [jnp.zeros((n_blk, LANE), F32) for _ in range(2)]
    for r in range(CMP_BLOCK):
        for half in range(2):
            rows = xs[half, pl.ds(r, n_blk, stride=CMP_BLOCK), :]
            acc[half] = acc[half] + jnp.dot(rows.astype(BF16), w_ref[half, r], preferred_element_type=F32)
    kc = _group_rms(acc[0], gtwo_ref[...], kg_ref[...])
    per_seq = n_blk // grp
    for g in range(grp):
        o_ref[g, 0:per_seq, 0:LANE] = kc[g * per_seq:(g + 1) * per_seq].astype(BF16)
        o_ref[g, 0:per_seq, LANE:2 * LANE] = acc[1][g * per_seq:(g + 1) * per_seq].astype(BF16)
        o_ref[g, per_seq:LANE, :] = jnp.zeros((LANE - per_seq, 2 * LANE), BF16)


def _compress_paged(cmp_t, page_table, cmp_pe, cmp_w, k_norm_c, gtwo, grp):
    b, n_pages = page_table.shape
    wk = cmp_w[0].reshape(CMP_BLOCK, HEAD_DIM, HEAD_DIM)
    wv = cmp_w[1].reshape(CMP_BLOCK, HEAD_DIM, HEAD_DIM)
    wbd = jnp.einsum("krde,ht->krhdte", jnp.stack([wk, wv]), jnp.eye(N_KV_HEADS, dtype=F32))
    wbd = wbd.reshape(2, CMP_BLOCK, LANE, LANE).astype(BF16)
    pe = jnp.tile(cmp_pe.transpose(0, 2, 1), (1, N_KV_HEADS, PAGE_SIZE // CMP_BLOCK))
    ident = jnp.eye(LANE, dtype=BF16)
    kg = jnp.tile(k_norm_c, 2).reshape(1, LANE)
    cspec = lambda a: pl.BlockSpec(a.shape, lambda i, pt: (0,) * a.ndim)
    kern = functools.partial(_compress_paged_kernel, grp=grp, n_pages=n_pages)
    return pl.pallas_call(
        kern,
        grid_spec=pltpu.PrefetchScalarGridSpec(
            num_scalar_prefetch=1, grid=(b // grp,),
            in_specs=[cspec(ident), cspec(pe), cspec(wbd), cspec(kg), cspec(gtwo)]
            + _page_specs(cmp_t, n_pages, grp),
            out_specs=pl.BlockSpec((grp, LANE, 2 * LANE), lambda i, pt: (i, 0, 0)),
            scratch_shapes=[pltpu.VMEM((2, grp * n_pages * PAGE_SIZE, LANE), F32)]),
        out_shape=jax.ShapeDtypeStruct((b, LANE, 2 * LANE), BF16),
        compiler_params=_cparams("parallel"),
        name="nsa_compress_paged",
    )(page_table, ident, pe, wbd, kg, gtwo, *([cmp_t] * (grp * n_pages)))


def _bias_cmp(rel_bias, q_starts, qb):
    q0 = jnp.asarray(q_starts, I32)[:, None, None]
    r = jnp.arange(qb)[None, :, None]
    n = jnp.arange(LANE)[None, None, :]
    return _bias_lookup(rel_bias, q0 + r - (n * CMP_BLOCK + CMP_BLOCK - 1)).transpose(0, 3, 1, 2)


def _nsa_sample_kernel(pt_ref, q_ref, gates_ref, kcv_ref, snew_ref, wnew_ref, win_ref, bias_ref, biasc_ref,
                       biasw_ref, pair_ref, *refs, grp, n_pages, t_pos, n_cmp, n_sel, w_eff):
    del pt_ref
    slc_refs, o_ref = refs[:grp * n_pages], refs[-1]
    lane1 = lax.broadcasted_iota(I32, (1, LANE), 1)
    lane8 = lax.broadcasted_iota(I32, (N_ATT_HEADS, LANE), 1)
    row_all = lax.broadcasted_iota(I32, (N_ATT_HEADS, (n_pages + 1) * LANE), 0)
    lanew = lax.broadcasted_iota(I32, (N_ATT_HEADS, w_eff), 1)
    lane1_f = lane1.astype(F32)
    bias = bias_ref[...]
    bias_now = bias[:, n_pages * LANE:n_pages * LANE + 1]
    mask_c = ((lane8 * CMP_BLOCK + CMP_BLOCK - 1) <= t_pos) & (lane8 < n_cmp)
    cur = t_pos // SEL_BLOCK
    forced = (lane1 == 0) | (lane1 == cur) | (lane1 == cur - 1)
    sel_causal = lane1 * SEL_BLOCK <= t_pos
    tail_valid = jnp.concatenate([lane1] * n_pages + [lane1 + LANE], axis=1) <= LANE
    for g in range(grp):
        q = q_ref[g]
        qf = q.astype(F32)
        s = jnp.where(mask_c, _dot_nt(q, kcv_ref[g, :, 0:LANE]) * ATT_SCALE + biasc_ref[...], NEG)
        e = jnp.exp(s - jnp.max(s, axis=1, keepdims=True))
        pc = jnp.where(mask_c, e / jnp.sum(e, axis=1, keepdims=True), 0.0)
        o_c = jnp.dot(pc.astype(BF16), kcv_ref[g, :, LANE:2 * LANE], preferred_element_type=F32)
        masks = []
        for hk in range(N_KV_HEADS):
            ps = jnp.sum(pc[hk * GQA:(hk + 1) * GQA], axis=0, keepdims=True)
            score = _dot_split(ps, pair_ref[...])
            score = jnp.where(sel_causal, jnp.where(forced, FORCE, score), NEG)
            picked = jnp.zeros((1, LANE), F32)
            for _ in range(n_sel):
                mx = jnp.max(score, axis=1, keepdims=True)
                first = jnp.min(jnp.where(score == mx, lane1_f, float(LANE)), axis=1, keepdims=True)
                hit = lane1_f == first
                picked = jnp.where(hit, 1.0, picked)
                score = jnp.where(hit, TAKEN, score)
            per_page = PAGE_SIZE // SEL_BLOCK
            tiles = []
            for p in range(n_pages + 1):
                t = jnp.zeros((1, LANE), F32)
                for a in range(per_page):
                    blk = picked[:, p * per_page + a:p * per_page + a + 1]
                    t = jnp.where(lane1 // SEL_BLOCK == a, blk, t)
                tiles.append(t)
            masks.append(jnp.concatenate(tiles, axis=1))
        valid = (jnp.where(row_all < GQA, masks[0], masks[1]) > 0.5) & tail_valid
        snew = _bf16_round(snew_ref[g])
        tiles = [_dot(q, slc_refs[g * n_pages + p][0:LANE, :]) for p in range(n_pages)]
        tiles.append(jnp.where(lane1 == 0, jnp.sum(qf * snew[:, 0:LANE], axis=1, keepdims=True), 0.0))
        s = jnp.where(valid, jnp.concatenate(tiles, axis=1) * ATT_SCALE + bias, NEG)
        e = jnp.exp(s - jnp.max(s, axis=1, keepdims=True))
        p_all = jnp.where(valid, e / jnp.sum(e, axis=1, keepdims=True), 0.0)
        o_s = p_all[:, n_pages * LANE:n_pages * LANE + 1] * snew[:, LANE:2 * LANE]
        for p in range(n_pages):
            o_s = o_s + _dot_nt(p_all[:, p * LANE:(p + 1) * LANE], slc_refs[g * n_pages + p][LANE:2 * LANE, :])
        wnew = _bf16_round(wnew_ref[g])
        valid_w = (w_eff - lanew) < WINDOW
        s_w = jnp.where(valid_w, _dot(q, win_ref[g, 0:LANE, :]) * ATT_SCALE + biasw_ref[...], NEG)
        s_n = jnp.sum(qf * wnew[:, 0:LANE], axis=1, keepdims=True) * ATT_SCALE + bias_now
        m = jnp.maximum(jnp.max(s_w, axis=1, keepdims=True), s_n)
        e_w = jnp.where(valid_w, jnp.exp(s_w - m), 0.0)
        e_n = jnp.exp(s_n - m)
        den = jnp.sum(e_w, axis=1, keepdims=True) + e_n
        o_w = _dot_nt(e_w / den, win_ref[g, LANE:2 * LANE, :]) + (e_n / den) * wnew[:, LANE:2 * LANE]
        gates = gates_ref[g]
        _head_rows_out(o_ref, g, gates[:, 0:1] * o_c + gates[:, 1:2] * o_s + gates[:, 2:3] * o_w)


def _nsa_sample(q8, gates8, kcv, snew, wnew, win_t, bias, bias_c, bias_w, pair, slc_t, page_table,
                *, t_pos, n_cmp, n_sel, grp):
    b, n_pages = page_table.shape
    w_eff = win_t.shape[2]
    gspec = lambda a: pl.BlockSpec((grp,) + a.shape[1:], lambda i, pt: (i,) + (0,) * (a.ndim - 1))
    cspec = lambda a: pl.BlockSpec(a.shape, lambda i, pt: (0,) * a.ndim)
    kern = functools.partial(_nsa_sample_kernel, grp=grp, n_pages=n_pages, t_pos=t_pos, n_cmp=n_cmp, n_sel=n_sel,
                             w_eff=w_eff)
    return pl.pallas_call(
        kern,
        grid_spec=pltpu.PrefetchScalarGridSpec(
            num_scalar_prefetch=1, grid=(b // grp,),
            in_specs=[gspec(q8), gspec(gates8), gspec(kcv), gspec(snew), gspec(wnew), gspec(win_t), cspec(bias),
                      cspec(bias_c), cspec(bias_w), cspec(pair)] + _page_specs(slc_t, n_pages, grp),
            out_specs=pl.BlockSpec((grp, N_ATT_HEADS, LANE), lambda i, pt: (i, 0, 0))),
        out_shape=jax.ShapeDtypeStruct((b, N_ATT_HEADS, LANE), BF16),
        compiler_params=_cparams("parallel"),
        name="nsa_sample",
    )(page_table, q8, gates8, kcv, snew, wnew, win_t, bias, bias_c, bias_w, pair, *([slc_t] * (grp * n_pages)))


def _post_odd_kernel(x_ref, y_ref, r_ref, k_ref, v_ref, g_ref, od_ref, gate_ref, shf_ref, scf_ref,
                     gf_ref, lnw_ref, lnb_ref, rk_ref, gtwo_ref, wo_ref, wrt_ref, br_ref, x2_ref, hf_ref, cwt_ref):
    gtwo = gtwo_ref[...]
    mix = jnp.dot(od_ref[...], wo_ref[RWKV_DIM:RWKV_DIM + N_ATT_HEADS * LANE, :], preferred_element_type=F32)
    for t in range(RWKV_DIM // LANE):
        sl = slice(t * LANE, (t + 1) * LANE)
        y = y_ref[:, sl]
        dlt = y - _dot_split(y, gtwo)
        yn = (dlt * lax.rsqrt(_dot_split(dlt * dlt, gtwo) + GN_EPS)) * lnw_ref[:, sl] + lnb_ref[:, sl]
        dot_rk = _dot_split(r_ref[:, sl] * k_ref[:, sl] * rk_ref[:, sl], gtwo) * float(HEAD_DIM)
        oc = (yn + dot_rk * v_ref[:, sl]) * g_ref[:, sl]
        mix = mix + jnp.dot(oc.astype(BF16), wo_ref[sl, :], preferred_element_type=F32)
    _post_tail(x_ref[...], mix, gate_ref[...], gf_ref[...], shf_ref[...], scf_ref[...], wrt_ref, br_ref,
               x2_ref, hf_ref, cwt_ref)


def _odd_layer(xp, xs, mp, ms, page_table, wkv0, shift0, cmp_pool, slc_pool, win_buf, rel_bias, router, experts,
               g_mix, g_ffn, w_in, w_out, cpar, r_k, ln_w, ln_b, q_norm, k_norm, cmp_pe, cmp_w):
    B, S, D = xp.shape
    Bs = xs.shape[0]
    n_pages = page_table.shape[1]
    past = n_pages * PAGE_SIZE
    wts = _odd_weights(w_in, w_out, q_norm, k_norm)
    gm = _gmats()
    gone, gtwo, _ = gm
    gsum = (gtwo.astype(F32) * HEAD_DIM).astype(BF16)
    i = jnp.arange(LANE)
    pair = jnp.where(i[:, None] // 2 == i[None, :], 1.0, 0.0).astype(BF16)
    cw = _compress_weights(cmp_pe, cmp_w, k_norm[0])
    wrt, br = router
    wg, wu, wd, lyr = experts
    g_mix = g_mix.reshape(1, D)
    post_consts = [g_ffn.reshape(1, D), ln_w.reshape(1, -1), ln_b.reshape(1, -1), r_k.reshape(1, -1), gtwo, wts[1],
                   wrt, br]
    w_eff = win_buf.shape[1]

    tp = ROW_TILE
    zc, q_p, kvc, _, kvs_b, _, kvw_b, gates, kvc_t, kvs_t, kvw_t = _inproj_odd(xp.reshape(B * S, D), [mp[0], mp[1]],
                                                                               g_mix, wts, gm, tp, S // tp)
    r3 = lambda a: a.reshape(B, S, a.shape[-1])
    pre = _rwkv_pre(zc, None, cpar, gsum, tp, S // tp)
    y, wkv_p = _rwkv_scan(pre, jnp.zeros((B, RWKV_HEADS, HEAD_DIM, HEAD_DIM), F32), B, S, SCAN_TIME_CHUNK)
    n_cmp = S // CMP_BLOCK
    kcv = _compress(kvc.reshape(B * n_cmp, CMP_BLOCK * 2 * LANE), cw, gtwo, _pick_tile(B * n_cmp, 256))
    kcv = jnp.pad(kcv.reshape(B, n_cmp, 2 * LANE), ((0, 0), (0, LANE - n_cmp), (0, 0))).astype(BF16)
    n_slc = -(-S // SEL_BLOCK)
    od = _nsa_t(r3(q_p), r3(gates), kcv, r3(kvs_b), r3(kvw_b), _bias_tiles_t(rel_bias), _bias_cmp_t(rel_bias, S),
                pair.T, n_cmp=n_cmp, n_sel=min(N_SEL_BLOCKS, n_slc))
    x2, hf, cwt = _row_call(_post_odd_kernel,
                            [xp.reshape(B * S, D), y, pre[0], pre[2], pre[3], pre[6], od.reshape(B * S, -1)],
                            [mp[2], mp[3], mp[4]], post_consts, POST_OUTS, tp, S // tp, "post_odd",
                            outs_t=[(N_EXPERTS, F32)])
    tm = min(MOE_TILE, S)
    xp3 = _moe_sorted(hf, cwt, x2, mp[5], wg, wu, wd, lyr, tm, S // tm).reshape(B, S, D)
    kv5 = lambda a, n: a.reshape(-1, n, 2, N_KV_HEADS, HEAD_DIM)
    outs_p = (wkv_p, r3(zc)[:, S - 1], _cache_rows(kvc_t), _cache_rows(kvs_t),
              _cache_rows(kvw_t[:, :, S - min(WINDOW, S):]))

    zc_s, q_s, kvc_s, kvs_s, _, kvw_s, _, gates_s, _, _, _ = _inproj_odd(xs, [ms[0], ms[1]], g_mix, wts, gm, Bs, 1)
    pre_s = _rwkv_pre(zc_s, shift0, cpar, gsum, Bs)
    y_s, wkv_s = _rwkv_scan(pre_s, wkv0, Bs, 1, 1)
    n_pool = cmp_pool.shape[0]
    cmp_t = cmp_pool.transpose(0, 2, 3, 4, 1).reshape(n_pool, 2 * LANE, PAGE_SIZE)
    kcv_s = _compress_paged(cmp_t, page_table, cmp_pe, cmp_w, k_norm[0], gtwo, _pick_group(Bs, SAMPLE_GROUP))
    n_cmp_s = (past + 1) // CMP_BLOCK
    slc_t = slc_pool.transpose(0, 2, 3, 4, 1).reshape(n_pool, 2 * LANE, PAGE_SIZE)
    win_t = win_buf.transpose(0, 2, 3, 4, 1).reshape(Bs, 2 * LANE, w_eff)
    gates8 = jnp.pad(gates_s[:, :3 * N_ATT_HEADS].reshape(Bs, 3, N_ATT_HEADS).transpose(0, 2, 1),
                     ((0, 0), (0, 0), (0, LANE - 3)))
    n_slc_s = -(-(past + 1) // SEL_BLOCK)
    od_s = _nsa_sample(q_s.reshape(Bs, N_ATT_HEADS, LANE), gates8, kcv_s, kvs_s[:, None, :], kvw_s[:, None, :],
                       win_t, _bias_row(rel_bias, past), _bias_cmp(rel_bias, [past], 1)[0, :, 0, :],
                       _bias_lookup(rel_bias, w_eff - jnp.arange(w_eff)).T, pair, slc_t, page_table,
                       t_pos=past, n_cmp=n_cmp_s, n_sel=min(N_SEL_BLOCKS, n_slc_s),
                       grp=_pick_group(Bs, SAMPLE_GROUP)).reshape(Bs, -1)
    x2s, hfs, cwts = _row_call(_post_odd_kernel, [xs, y_s, pre_s[0], pre_s[2], pre_s[3], pre_s[6], od_s],
                               [ms[2], ms[3], ms[4]], post_consts, POST_OUTS, Bs, 1, "post_odd_s",
                               outs_t=[(N_EXPERTS, F32)])
    xs3 = _moe(hfs, cwts.T, x2s, ms[5], wg, wu, wd, lyr, Bs, 1)
    win_new = jnp.concatenate([win_buf[:, 1:], kv5(kvw_s, 1)], axis=1)
    outs_s = (wkv_s, zc_s, kv5(kvc_s, 1), kv5(kvs_s, 1), win_new)
    return xp3, xs3, outs_p, outs_s


def _mods(c_p, c_s, w_all, b, layer):
    nb = c_p.shape[0]
    m = _ada(jnp.concatenate([c_p, c_s], 0), w_all, b, layer)
    parts = jnp.split(m, 6, axis=-1)
    return [p[:nb, None, :] for p in parts], [p[None, nb:, :] for p in parts]


def _forward(x_prompt, x_sample, c_prompt, c_sample, page_table, cache_a_kv, cache_a_kidx, state_b_conv,
             state_c_wkv, state_c_shift, cache_d_cmp, cache_d_slc, cache_d_win, rel_bias, w_router, b_router,
             w_ada, b_ada, g_norm_mix, g_norm_ffn, w_expert_gate, w_expert_up, w_expert_down, e_w_in, e_w_out,
             a_q_norm, a_k_norm, b_conv_w, b_conv_b, o_w_in, o_w_out, c_mu, c_w0, c_w_up, c_a0, c_a_up,
             c_g_up, c_k_k, c_k_a, c_r_k, c_ln_w, c_ln_b, d_q_norm, d_k_norm, d_cmp_pe, d_cmp_w):
    assert w_ada.shape[0] == 2 and e_w_in.shape[0] == 1 and o_w_in.shape[0] == 1
    B, S, D = x_prompt.shape
    Bs = x_sample.shape[0]
    assert x_sample.shape[1] == 1
    xp, xs = x_prompt, x_sample.reshape(Bs, D)
    router = (w_router.T, b_router.reshape(N_EXPERTS, 1))
    n_pool = cache_a_kv.shape[1]
    wbf = tuple(w.astype(BF16) for w in (w_expert_gate, w_expert_up, w_expert_down))
    experts = lambda l: wbf + (l,)

    mp, ms = _mods(c_prompt, c_sample, w_ada, b_ada[0], 0)
    xp, xs, ep, es = _even_layer(xp, xs, mp, ms, page_table, cache_a_kv[0], cache_a_kidx[0], state_b_conv[0],
                                 rel_bias, router, experts(0), g_norm_mix[0], g_norm_ffn[0], e_w_in[0], e_w_out[0],
                                 a_q_norm[0], a_k_norm[0], b_conv_w[0], b_conv_b[0])
    mp, ms = _mods(c_prompt, c_sample, w_ada, b_ada[1], 1)
    cpar = (c_mu[0], c_w0[0], c_w_up[0], c_a0[0], c_a_up[0], c_g_up[0], c_k_k[0], c_k_a[0])
    xp, xs, op, os_ = _odd_layer(xp, xs, mp, ms, page_table, state_c_wkv[0], state_c_shift[0], cache_d_cmp[0],
                                 cache_d_slc[0], cache_d_win[0], rel_bias, router, experts(1), g_norm_mix[1],
                                 g_norm_ffn[1], o_w_in[0], o_w_out[0], cpar, c_r_k[0].reshape(-1), c_ln_w[0],
                                 c_ln_b[0], d_q_norm[0], d_k_norm[0], d_cmp_pe[0], d_cmp_w[0])
    stack = lambda ts: tuple(a[None] for a in ts)
    return (xp, xs.reshape(Bs, 1, D)) + stack(ep) + stack(op) + stack(es) + stack(os_)


def kernel(x_prompt, x_sample, c_prompt, c_sample, page_table, cache_a_kv, cache_a_kidx, state_b_conv, state_c_wkv, state_c_shift, cache_d_cmp, cache_d_slc, cache_d_win, rel_bias, w_router, b_router, w_ada, b_ada, g_norm_mix, g_norm_ffn, w_expert_gate, w_expert_up, w_expert_down, e_w_in, e_w_out, a_q_norm, a_k_norm, b_conv_w, b_conv_b, o_w_in, o_w_out, c_mu, c_w0, c_w_up, c_a0, c_a_up, c_g_up, c_k_k, c_k_a, c_r_k, c_ln_w, c_ln_b, d_q_norm, d_k_norm, d_cmp_pe, d_cmp_w):
    return _forward(x_prompt, x_sample, c_prompt, c_sample, page_table, cache_a_kv, cache_a_kidx, state_b_conv,
                    state_c_wkv, state_c_shift, cache_d_cmp, cache_d_slc, cache_d_win, rel_bias, w_router, b_router,
                    w_ada, b_ada, g_norm_mix, g_norm_ffn, w_expert_gate, w_expert_up, w_expert_down, e_w_in, e_w_out,
                    a_q_norm, a_k_norm, b_conv_w, b_conv_b, o_w_in, o_w_out, c_mu, c_w0, c_w_up, c_a0, c_a_up,
                    c_g_up, c_k_k, c_k_a, c_r_k, c_ln_w, c_ln_b, d_q_norm, d_k_norm, d_cmp_pe, d_cmp_w)
```

```python
import functools
import math

import jax
import jax.numpy as jnp
from jax import lax
from jax.experimental import pallas as pl
from jax.experimental.pallas import tpu as pltpu

F32 = jnp.float32
BF16 = jnp.bfloat16
I32 = jnp.int32

LANE = 128
HEAD_DIM = 64
N_ATT_HEADS = 8
N_KV_HEADS = 2
GQA = N_ATT_HEADS // N_KV_HEADS
IDX_HEADS = 4
IDX_DIM = 64
TOPK_MAX = 256
CONV_CH = 512
RWKV_HEADS = 8
RWKV_DIM = RWKV_HEADS * HEAD_DIM
LORA_W = 64
LORA_A = 64
LORA_G = 128
GN_EPS = 64e-5
CMP_BLOCK = 32
SEL_BLOCK = 64
N_SEL_BLOCKS = 8
WINDOW = 512
N_BUCKETS = 32
MAX_DISTANCE = 128
N_EXPERTS = 16
N_GROUPS = 4
EXPERTS_PER_GROUP = N_EXPERTS // N_GROUPS
PAGE_SIZE = 128
RMS_EPS = 1e-6
NEG = -1e30
FORCE = 1e9
TAKEN = -3e38
ATT_SCALE = HEAD_DIM ** -0.5
VMEM_LIMIT = 56 * 1024 * 1024
ROW_TILE = 512
MOE_TILE = 1024
SCAN_TIME_CHUNK = 64
SAMPLE_GROUP = 4


def _cparams(*sem):
    return pltpu.CompilerParams(dimension_semantics=sem, vmem_limit_bytes=VMEM_LIMIT)


def _pick_tile(rows, pref):
    t = min(pref, rows)
    while rows % t or (t % 8 and t != rows):
        t -= 1
    return t


def _pick_group(n, pref):
    g = min(pref, n)
    while n % g:
        g -= 1
    return g


def _const_spec(a):
    nd = a.ndim
    return pl.BlockSpec(a.shape, lambda *_: (0,) * nd)


def _dot(a, b):
    return jnp.dot(a.astype(BF16), b.astype(BF16), preferred_element_type=F32)


def _dot_nt(a, b):
    return lax.dot_general(a.astype(BF16), b.astype(BF16), (((1,), (1,)), ((), ())),
                           preferred_element_type=F32)


def _dot_split(x, m):
    hi = x.astype(BF16)
    r1 = x - hi.astype(F32)
    mid = r1.astype(BF16)
    lo = (r1 - mid.astype(F32)).astype(BF16)
    return (jnp.dot(hi, m, preferred_element_type=F32) + jnp.dot(mid, m, preferred_element_type=F32)
            + jnp.dot(lo, m, preferred_element_type=F32))


def _bf16_round(x):
    return x.astype(BF16).astype(F32)


def _sigmoid(x):
    return 1.0 / (1.0 + jnp.exp(-x))


def _silu(x):
    return x * _sigmoid(x)


def _modulate(x, g, shift, scale):
    y = x * lax.rsqrt(jnp.mean(x * x, axis=-1, keepdims=True) + RMS_EPS)
    return (y * g) * (1.0 + scale) + shift


def _group_rms(t, gmat, gain):
    ms = _dot_split(t * t, gmat)
    return (t * lax.rsqrt(ms + RMS_EPS)) * gain


def _ada_kernel(c_ref, w_ref, b_ref, o_ref):
    o_ref[...] = _dot(_silu(c_ref[...]), w_ref[...]) + b_ref[...]


def _ada(c, w_all, b, layer):
    r, d = c.shape
    n = w_all.shape[2]
    tn = 512
    return pl.pallas_call(
        _ada_kernel,
        grid=(n // tn,),
        in_specs=[pl.BlockSpec((r, d), lambda j: (0, 0)),
                  pl.BlockSpec((None, d, tn), lambda j: (layer, 0, j)),
                  pl.BlockSpec((1, tn), lambda j: (0, j))],
        out_specs=pl.BlockSpec((r, tn), lambda j: (0, j)),
        out_shape=jax.ShapeDtypeStruct((r, n), F32),
        compiler_params=_cparams("parallel"),
        name="ada_mod",
    )(c, w_all, b.reshape(1, n))


E_Q0, E_KV0, E_QI0, E_MISC0, E_BG0, E_CG0, E_XIN0, E_END = 0, 1024, 1280, 1792, 1920, 2432, 2944, 3456


def _inproj_even_kernel(x_ref, shift_ref, scale_ref, g_ref, w_ref, qg_ref, kg_ref, gone_ref, gtwo_ref,
                        q_ref, kv_ref, kvb_ref, qi_ref, misc_ref, miscb_ref, bg_ref, u_ref, kvt_ref, misct_ref):
    h = _modulate(x_ref[...], g_ref[...], shift_ref[...], scale_ref[...])
    z = jnp.dot(h.astype(BF16), w_ref[...], preferred_element_type=F32)
    gone = gone_ref[...]
    for t in range(N_ATT_HEADS):
        sl = slice(t * LANE, (t + 1) * LANE)
        q_ref[:, sl] = _group_rms(z[:, E_Q0 + t * LANE:E_Q0 + (t + 1) * LANE], gone, qg_ref[:, sl]).astype(BF16)
    k = _group_rms(z[:, E_KV0:E_KV0 + LANE], gtwo_ref[...], kg_ref[...])
    v = z[:, E_KV0 + LANE:E_KV0 + 2 * LANE]
    kv_ref[:, 0:LANE] = k
    kv_ref[:, LANE:2 * LANE] = v
    kvb_ref[:, 0:LANE] = k.astype(BF16)
    kvb_ref[:, LANE:2 * LANE] = v.astype(BF16)
    qi_ref[...] = z[:, E_QI0:E_MISC0].astype(BF16)
    misc = z[:, E_MISC0:E_BG0]
    misc_ref[...] = misc
    miscb_ref[...] = misc.astype(BF16)
    bg_ref[...] = z[:, E_BG0:E_CG0]
    u_ref[...] = z[:, E_CG0:E_XIN0] * z[:, E_XIN0:E_END]
    kvt_ref[0:LANE, :] = k.T
    kvt_ref[LANE:2 * LANE, :] = v.T
    misct_ref[...] = misc.T


HALO = 8


def _shifted(z, halo, k, tpb):
    first = (pl.program_id(0) % tpb) == 0
    row = lax.broadcasted_iota(I32, z.shape, 0)
    out = pltpu.roll(z, k, axis=0)
    for j in range(k):
        prev = jnp.where(first, 0.0, halo[HALO - k + j:HALO - k + j + 1, :])
        out = jnp.where(row == j, prev, out)
    return out


def _row_call(kernel, xs, mods, consts, outs, tile, tpb, name, outs_t=(), outs_bt=(), halos=()):
    if not isinstance(xs, (list, tuple)):
        xs = [xs]
    rows = xs[0].shape[0]
    n_tiles = rows // tile
    in_specs = [pl.BlockSpec((tile, x.shape[1]), lambda t: (t, 0)) for x in xs]
    in_specs += [pl.BlockSpec((HALO, xs[i].shape[1]), lambda t: (jnp.maximum(t * (tile // HALO) - 1, 0), 0))
                 for i in halos]
    xs = list(xs) + [xs[i] for i in halos]
    for m in mods:
        in_specs.append(pl.BlockSpec((None,) + m.shape[1:], lambda t: (t // tpb, 0, 0)))
    in_specs += [_const_spec(c) for c in consts]
    out_specs = [pl.BlockSpec((tile, w), lambda t: (t, 0)) for (w, _) in outs]
    out_shape = [jax.ShapeDtypeStruct((rows, w), dt) for (w, dt) in outs]
    out_specs += [pl.BlockSpec((hh, tile), lambda t: (0, t)) for (hh, _) in outs_t]
    out_shape += [jax.ShapeDtypeStruct((hh, rows), dt) for (hh, dt) in outs_t]
    out_specs += [pl.BlockSpec((None, hh, tile), lambda t: (t // tpb, 0, t % tpb)) for (hh, _) in outs_bt]
    out_shape += [jax.ShapeDtypeStruct((n_tiles // tpb, hh, tile * tpb), dt) for (hh, dt) in outs_bt]
    return pl.pallas_call(kernel, grid=(n_tiles,), in_specs=in_specs, out_specs=out_specs, out_shape=out_shape,
                          compiler_params=_cparams("parallel"), name=name)(*xs, *mods, *consts)


def _pad_q_cols(wq):
    d = wq.shape[0]
    w = wq.reshape(d, N_ATT_HEADS, HEAD_DIM)
    z = jnp.zeros_like(w)
    lo = jnp.concatenate([w, z], -1)
    hi = jnp.concatenate([z, w], -1)
    sel = (jnp.arange(N_ATT_HEADS) >= GQA)[None, :, None]
    return jnp.where(sel, hi, lo).reshape(d, N_ATT_HEADS * LANE)


def _pad_o_rows(wo):
    d = wo.shape[1]
    w = wo.reshape(N_ATT_HEADS, HEAD_DIM, d)
    z = jnp.zeros_like(w)
    lo = jnp.concatenate([w, z], 1)
    hi = jnp.concatenate([z, w], 1)
    sel = (jnp.arange(N_ATT_HEADS) >= GQA)[:, None, None]
    return jnp.where(sel, hi, lo).reshape(N_ATT_HEADS * LANE, d)


def _gmats():
    i = jnp.arange(LANE)
    gone = jnp.full((LANE, LANE), 1.0 / HEAD_DIM, F32).astype(BF16)
    gtwo = jnp.where((i[:, None] // HEAD_DIM) == (i[None, :] // HEAD_DIM), 1.0 / HEAD_DIM, 0.0).astype(BF16)
    tri = jnp.where(i[:, None] <= i[None, :], 1.0, 0.0).astype(BF16)
    return gone, gtwo, tri


def _even_weights(w_in, w_out, q_norm, k_norm):
    d = w_in.shape[0]
    a_q, a_kv = N_ATT_HEADS * HEAD_DIM, 2 * N_KV_HEADS * HEAD_DIM
    o = 0
    wq = w_in[:, o:o + a_q]; o += a_q
    wkv = w_in[:, o:o + a_kv]; o += a_kv
    wqi = w_in[:, o:o + IDX_HEADS * IDX_DIM]; o += IDX_HEADS * IDX_DIM
    wki = w_in[:, o:o + IDX_DIM]; o += IDX_DIM
    wwi = w_in[:, o:o + IDX_HEADS]; o += IDX_HEADS
    wrest = w_in[:, o:]
    wqi = jnp.concatenate([wqi.reshape(d, IDX_HEADS, IDX_DIM), jnp.zeros((d, IDX_HEADS, LANE - IDX_DIM), F32)],
                          -1).reshape(d, IDX_HEADS * LANE)
    wmisc = jnp.concatenate([wki, wwi, jnp.zeros((d, LANE - IDX_DIM - IDX_HEADS), F32)], -1)
    w_in_p = jnp.concatenate([_pad_q_cols(wq), wkv, wqi, wmisc, wrest], -1).astype(BF16)
    w_out_p = jnp.concatenate([_pad_o_rows(w_out[:a_q]), w_out[a_q:]], 0).astype(BF16)
    qg = jnp.tile(q_norm, 2 * N_ATT_HEADS).reshape(1, N_ATT_HEADS * LANE)
    kg = jnp.tile(k_norm, 2).reshape(1, LANE)
    return w_in_p, w_out_p, qg, kg


def _inproj_even(x, mods, g, wts, gm, tile, tpb):
    w_in_p, _, qg, kg = wts
    gone, gtwo, _ = gm
    outs = [(N_ATT_HEADS * LANE, BF16), (2 * LANE, F32), (2 * LANE, BF16), (IDX_HEADS * LANE, BF16),
            (LANE, F32), (LANE, BF16), (CONV_CH, F32), (CONV_CH, F32)]
    return _row_call(_inproj_even_kernel, x, mods, [g, w_in_p, qg, kg, gone, gtwo], outs, tile, tpb, "inproj_even",
                     outs_bt=[(2 * LANE, F32), (LANE, F32)])


def _t5_bucket(dist):
    dist = jnp.maximum(dist, 0)
    exact = N_BUCKETS // 2
    far = exact + (jnp.log(jnp.maximum(dist, 1).astype(F32) / exact)
                   / math.log(MAX_DISTANCE / exact) * (N_BUCKETS - exact)).astype(I32)
    return jnp.where(dist < exact, dist, jnp.minimum(far, N_BUCKETS - 1))


def _bias_lookup(rel_bias, dist):
    onehot = (_t5_bucket(dist)[..., None] == jnp.arange(N_BUCKETS)).astype(F32)
    return jnp.einsum("...k,kh->...h", onehot, rel_bias, precision=lax.Precision.HIGHEST)


def _bias_row(rel_bias, t_pos):
    return _bias_lookup(rel_bias, t_pos - jnp.arange(t_pos + LANE)).T


def _bias_tiles(rel_bias, qb):
    r = jnp.arange(qb)[:, None]
    c = jnp.arange(LANE)[None, :]
    tiles = [_bias_lookup(rel_bias, d * LANE + r - c) for d in range(3)]
    return jnp.stack(tiles).transpose(0, 3, 1, 2)


def _stack_heads(q_ref, hk):
    return jnp.concatenate([q_ref[:, (hk * GQA + g) * LANE:(hk * GQA + g + 1) * LANE] for g in range(GQA)], axis=0)


QB = LANE
QW = GQA * QB


def _sub_sum(x):
    return jnp.sum(x, axis=0, keepdims=True)


def _flash_t_pair(blocks, qs, bias_ref, carry, acc_ref):
    logits = [[jnp.where(mk[hk], _dot_nt(kb, qs[hk]) + bias_ref[dsel, hk], NEG) for (kb, _, mk, dsel) in blocks]
              for hk in range(N_KV_HEADS)]
    out = []
    for hk in range(N_KV_HEADS):
        m, l = carry[hk]
        m_new = m
        for s in logits[hk]:
            m_new = jnp.maximum(m_new, jnp.max(s, axis=0, keepdims=True))
        alpha = jnp.exp(m - m_new)
        l = alpha * l
        pv = None
        for s, (_, vt, _, _) in zip(logits[hk], blocks):
            p = jnp.exp(s - m_new)
            l = l + _sub_sum(p)
            d = jnp.dot(vt, p.astype(BF16), preferred_element_type=F32)
            pv = d if pv is None else pv + d
        acc_ref[hk] = alpha * acc_ref[hk] + pv
        out.append((m_new, l))
    return tuple(out)


def _scaled_queries(q_ref, hk):
    return (_stack_heads(q_ref, hk).astype(F32) * ATT_SCALE).astype(BF16)


def _pair_loop(nblk, body, init):
    def body2(jj, c):
        return body(2 * jj + 1, body(2 * jj, c))
    return lax.fori_loop(0, (nblk + 1) // 2, body2, init)


def _flash_t_init():
    return (jnp.full((1, QW), NEG, F32), jnp.zeros((1, QW), F32))


def _tile_lanes(x):
    return jnp.concatenate([x] * GQA, axis=1)


def _write_heads_t(o_ref, o_ts):
    lane = lax.broadcasted_iota(I32, (QB, LANE), 1)
    for hk in range(N_KV_HEADS):
        valid = (lane // HEAD_DIM) == hk
        for g in range(GQA):
            h = hk * GQA + g
            o = o_ts[hk][:, g * QB:(g + 1) * QB].T
            o_ref[:, h * LANE:(h + 1) * LANE] = jnp.where(valid, o, 0.0).astype(BF16)


def _dsa_t_kernel(q_ref, qi_ref, misc_ref, kidx_ref, k_ref, vt_ref, bias_ref, trit_ref, o_ref, key_s, acc_s,
                  *, n_keep):
    i = pl.program_id(1)
    q0 = i * QB
    nblk = i + 1
    krow = lax.broadcasted_iota(I32, (LANE, QB), 0)
    qcol = lax.broadcasted_iota(I32, (LANE, QB), 1)
    misc_t = misc_ref[...].T
    wis = [_bf16_round(misc_t[IDX_DIM + h:IDX_DIM + h + 1, :]) for h in range(IDX_HEADS)]
    qi = jnp.concatenate([qi_ref[:, h * LANE:(h + 1) * LANE] for h in range(IDX_HEADS)], axis=0)
    idx_scale = (IDX_HEADS * IDX_DIM) ** -0.5

    def causal(j):
        return j * LANE + krow <= q0 + qcol

    def pass_a(j, c):
        kb = kidx_ref[pl.ds(pl.multiple_of(j * LANE, LANE), LANE), :]
        rel = _bf16_round(jnp.maximum(_dot_nt(kb, qi), 0.0))
        acc = jnp.zeros((LANE, QB), F32)
        for h in range(IDX_HEADS):
            acc = acc + rel[:, h * QB:(h + 1) * QB] * wis[h]
        key_s[j] = _order_keys(jnp.where(causal(j), acc * idx_scale, NEG))
        return c

    _pair_loop(nblk, pass_a, 0)

    def count(pred):
        def body(j, a):
            return a + jnp.where(pred(key_s[j]), 1.0, 0.0)
        return _sub_sum(_pair_loop(nblk, body, jnp.zeros((LANE, QB), F32)))

    keep = float(n_keep)
    thr = jnp.where(count(lambda k: k >= 0) >= keep, jnp.int32(0), jnp.int32(-2 ** 31))

    def search(it, thr):
        cand = thr | lax.shift_left(jnp.int32(1), jnp.int32(30) - it)
        return jnp.where(count(lambda k: k >= cand) >= keep, cand, thr)

    thr = lax.fori_loop(0, 31, search, thr)
    need = keep - count(lambda k: k > thr)
    trit = trit_ref[...]

    def pass_c(j, run):
        key = key_s[j]
        eq = key == thr
        eqf = jnp.where(eq, 1.0, 0.0)
        cum = jnp.dot(trit, eqf.astype(BF16), preferred_element_type=F32) + run
        sel = ((key > thr) | (eq & (cum <= need))) & causal(j)
        key_s[j] = jnp.where(sel, 1, 0)
        return run + _sub_sum(eqf)

    _pair_loop(nblk, pass_c, jnp.zeros((1, QB), F32))

    qs = [_scaled_queries(q_ref, hk) for hk in range(N_KV_HEADS)]
    acc_s[...] = jnp.zeros_like(acc_s)

    def pass_d(jj, carry):
        blocks = []
        for j in (2 * jj, 2 * jj + 1):
            kb = k_ref[pl.ds(pl.multiple_of(j * LANE, LANE), LANE), :]
            blocks.append((kb, vt_ref[j], [_tile_lanes(key_s[j] > 0)] * N_KV_HEADS, jnp.clip(i - j, 0, 2)))
        return _flash_t_pair(blocks, qs, bias_ref, carry, acc_s)

    res = lax.fori_loop(0, (nblk + 1) // 2, pass_d, tuple(_flash_t_init() for _ in range(N_KV_HEADS)))
    _write_heads_t(o_ref, [acc_s[hk] / res[hk][1] for hk in range(N_KV_HEADS)])


def _bias_tiles_t(rel_bias):
    t = _bias_tiles(rel_bias, QB)
    t = t.reshape(3, N_KV_HEADS, GQA, QB, LANE).transpose(0, 1, 4, 2, 3)
    return t.reshape(3, N_KV_HEADS, LANE, QW)


def _blocks_t(x):
    b, s, w = x.shape
    return x.reshape(b, s // LANE, LANE, w).transpose(0, 1, 3, 2)


def _dsa_t(q, qi, misc, kidx_b, kv_b, bias_t, trit, *, n_keep):
    b, s, _ = q.shape
    vt = _blocks_t(kv_b[:, :, LANE:])
    qspec = lambda w: pl.BlockSpec((None, QB, w), lambda bi, i: (bi, i, 0))
    kspec = pl.BlockSpec((None, s, LANE), lambda bi, i: (bi, 0, 0))
    return pl.pallas_call(
        functools.partial(_dsa_t_kernel, n_keep=n_keep),
        grid=(b, s // QB),
        in_specs=[qspec(N_ATT_HEADS * LANE), qspec(IDX_HEADS * LANE), qspec(LANE), kspec, kspec,
                  pl.BlockSpec((None, s // LANE, LANE, LANE), lambda bi, i: (bi, 0, 0, 0)),
                  _const_spec(bias_t), _const_spec(trit)],
        out_specs=qspec(N_ATT_HEADS * LANE),
        out_shape=jax.ShapeDtypeStruct((b, s, N_ATT_HEADS * LANE), BF16),
        scratch_shapes=[pltpu.VMEM((s // LANE, LANE, QB), I32), pltpu.VMEM((N_KV_HEADS, LANE, QW), F32)],
        compiler_params=_cparams("parallel", "parallel"),
        name="dsa_attention_t",
    )(q, qi, misc, kidx_b, kv_b, vt, bias_t, trit)


def _dot_split_rhs(m, x):
    hi = x.astype(BF16)
    r1 = x - hi.astype(F32)
    mid = r1.astype(BF16)
    lo = (r1 - mid.astype(F32)).astype(BF16)
    return (jnp.dot(m, hi, preferred_element_type=F32) + jnp.dot(m, mid, preferred_element_type=F32)
            + jnp.dot(m, lo, preferred_element_type=F32))


def _nsa_t_kernel(q_ref, gates_ref, kc_ref, vct_ref, ks_ref, vst_ref, kw_ref, vwt_ref, bias_ref, biasc_ref,
                  pairt_ref, o_ref, acc_s, *, n_cmp, n_sel):
    i = pl.program_id(1)
    q0 = i * QB
    nblk = i + 1
    heads = range(N_KV_HEADS)
    krow = lax.broadcasted_iota(I32, (LANE, QB), 0)
    qcol = lax.broadcasted_iota(I32, (LANE, QB), 1)
    t_pos = q0 + qcol
    krow_f = krow.astype(F32)
    qs = [_scaled_queries(q_ref, hk) for hk in heads]
    gates_t = gates_ref[...].T

    def gate_row(br, hk):
        return jnp.concatenate([gates_t[br * N_ATT_HEADS + hk * GQA + g:br * N_ATT_HEADS + hk * GQA + g + 1, :]
                                for g in range(GQA)], axis=1)

    mask_c = _tile_lanes(((krow * CMP_BLOCK + CMP_BLOCK - 1) <= t_pos) & (krow < n_cmp))
    cur = t_pos // SEL_BLOCK
    forced = (krow == 0) | (krow == cur) | (krow == cur - 1)
    sel_causal = krow * SEL_BLOCK <= t_pos
    o_cmp, picked = [], []
    for hk in heads:
        s = jnp.where(mask_c, _dot_nt(kc_ref[...], qs[hk]) + biasc_ref[hk], NEG)
        e = jnp.exp(s - jnp.max(s, axis=0, keepdims=True))
        p = jnp.where(mask_c, e / _sub_sum(e), 0.0)
        o_cmp.append(jnp.dot(vct_ref[...], p.astype(BF16), preferred_element_type=F32))
        ps = p[:, 0:QB]
        for g in range(1, GQA):
            ps = ps + p[:, g * QB:(g + 1) * QB]
        score = _dot_split_rhs(pairt_ref[...], ps)
        score = jnp.where(sel_causal, jnp.where(forced, FORCE, score), NEG)
        pk = jnp.zeros((LANE, QB), F32)
        for _ in range(n_sel):
            mx = jnp.max(score, axis=0, keepdims=True)
            first = jnp.min(jnp.where(score == mx, krow_f, float(LANE)), axis=0, keepdims=True)
            hit = krow_f == first
            pk = jnp.where(hit, 1.0, pk)
            score = jnp.where(hit, TAKEN, score)
        picked.append(pk.astype(BF16))

    def key_block(k_ref, vt_ref, j):
        return k_ref[pl.ds(pl.multiple_of(j * LANE, LANE), LANE), :], vt_ref[j]

    erow = lax.broadcasted_iota(I32, (LANE, LANE), 0)
    ecol = lax.broadcasted_iota(I32, (LANE, LANE), 1)
    acc_s[...] = jnp.zeros_like(acc_s)

    def slc_body(jj, carry):
        blocks = []
        for j in (2 * jj, 2 * jj + 1):
            kb, vt = key_block(ks_ref, vst_ref, j)
            expand = jnp.where(ecol == 2 * j + erow // SEL_BLOCK, 1.0, 0.0).astype(BF16)
            causal = j * LANE + krow <= t_pos
            masks = [_tile_lanes((jnp.dot(expand, picked[hk], preferred_element_type=F32) > 0.5) & causal)
                     for hk in heads]
            blocks.append((kb, vt, masks, jnp.clip(i - j, 0, 2)))
        return _flash_t_pair(blocks, qs, bias_ref, carry, acc_s.at[0])

    res_s = lax.fori_loop(0, (nblk + 1) // 2, slc_body, tuple(_flash_t_init() for _ in heads))

    lo = jnp.maximum(i - WINDOW // LANE - 1, 0)

    def win_body(jj, carry):
        blocks = []
        for j in (lo + 2 * jj, lo + 2 * jj + 1):
            kb, vt = key_block(kw_ref, vwt_ref, j)
            dist = t_pos - (j * LANE + krow)
            mask = _tile_lanes((dist >= 0) & (dist < WINDOW))
            blocks.append((kb, vt, [mask] * N_KV_HEADS, jnp.clip(i - j, 0, 2)))
        return _flash_t_pair(blocks, qs, bias_ref, carry, acc_s.at[1])

    res_w = lax.fori_loop(0, (i - lo + 2) // 2, win_body, tuple(_flash_t_init() for _ in heads))

    _write_heads_t(o_ref, [gate_row(0, hk) * o_cmp[hk] + gate_row(1, hk) * (acc_s[0, hk] / res_s[hk][1])
                           + gate_row(2, hk) * (acc_s[1, hk] / res_w[hk][1]) for hk in heads])


def _bias_cmp_t(rel_bias, s):
    t = _bias_cmp(rel_bias, [j * QB for j in range(s // QB)], QB)
    t = t.reshape(s // QB, N_KV_HEADS, GQA, QB, LANE).transpose(0, 1, 4, 2, 3)
    return t.reshape(s // QB, N_KV_HEADS, LANE, QW)


def _nsa_t(q, gates, kcv, kvs_b, kvw_b, bias_t, bias_c, pair_t, *, n_cmp, n_sel):
    b, s, _ = q.shape
    vct = kcv[:, :, LANE:].transpose(0, 2, 1)
    qspec = lambda w: pl.BlockSpec((None, QB, w), lambda bi, i: (bi, i, 0))
    kspec = pl.BlockSpec((None, s, LANE), lambda bi, i: (bi, 0, 0))
    vspec = pl.BlockSpec((None, s // LANE, LANE, LANE), lambda bi, i: (bi, 0, 0, 0))
    cspec = pl.BlockSpec((None, LANE, LANE), lambda bi, i: (bi, 0, 0))
    return pl.pallas_call(
        functools.partial(_nsa_t_kernel, n_cmp=n_cmp, n_sel=n_sel),
        grid=(b, s // QB),
        in_specs=[qspec(N_ATT_HEADS * LANE), qspec(LANE), cspec, cspec, kspec, vspec, kspec, vspec,
                  _const_spec(bias_t), pl.BlockSpec((None,) + bias_c.shape[1:], lambda bi, i: (i, 0, 0, 0)),
                  _const_spec(pair_t)],
        out_specs=qspec(N_ATT_HEADS * LANE),
        out_shape=jax.ShapeDtypeStruct((b, s, N_ATT_HEADS * LANE), BF16),
        scratch_shapes=[pltpu.VMEM((2, N_KV_HEADS, LANE, QW), F32)],
        compiler_params=_cparams("parallel", "parallel"),
        name="nsa_attention_t",
    )(q, gates, kcv, vct, kvs_b, _blocks_t(kvs_b[:, :, LANE:]), kvw_b, _blocks_t(kvw_b[:, :, LANE:]),
      bias_t, bias_c, pair_t)


def _select_top(keys, n_keep, tri):
    keep = float(n_keep)
    n = keys[0].shape[1]

    def count(pred):
        return [jnp.sum(jnp.where(pred(g, k), 1.0, 0.0), axis=1, keepdims=True) for g, k in enumerate(keys)]

    int_min = jnp.int32(-2 ** 31)
    thr = tuple(jnp.where(c >= keep, jnp.int32(0), int_min) for c in count(lambda g, k: k >= 0))

    def search(it, thr):
        bit = lax.shift_left(jnp.int32(1), jnp.int32(30) - it)
        cand = [t | bit for t in thr]
        cnt = count(lambda g, k: k >= cand[g])
        return tuple(jnp.where(c >= keep, cd, t) for c, cd, t in zip(cnt, cand, thr))

    thr = lax.fori_loop(0, 31, search, thr)
    need = [keep - c for c in count(lambda g, k: k > thr[g])]
    sels = []
    for g, k in enumerate(keys):
        run = jnp.zeros((1, 1), F32)
        parts = []
        for t in range(n // LANE):
            kt = k[:, t * LANE:(t + 1) * LANE]
            eq = kt == thr[g]
            eqf = jnp.where(eq, 1.0, 0.0)
            cum = jnp.dot(eqf.astype(BF16), tri, preferred_element_type=F32) + run
            parts.append((kt > thr[g]) | (eq & (cum <= need[g])))
            run = run + jnp.sum(eqf, axis=1, keepdims=True)
        sels.append(jnp.concatenate(parts, axis=1))
    return sels


def _order_keys(score):
    score = jnp.where(score == 0.0, 0.0, score)
    bits = lax.bitcast_convert_type(score, I32)
    return jnp.where(bits < 0, bits ^ jnp.int32(0x7FFFFFFF), bits)


def _head_rows_out(o_ref, g, acc):
    rowh = lax.broadcasted_iota(I32, (N_ATT_HEADS, LANE), 0)
    laneh = lax.broadcasted_iota(I32, (N_ATT_HEADS, LANE), 1)
    o_ref[g] = jnp.where((laneh // HEAD_DIM) == (rowh // GQA), acc, 0.0).astype(BF16)


def _dsa_sample_kernel(pt_ref, q_ref, qi_ref, wi_ref, knew_ref, kvnew_ref, bias_ref, tri_ref, *refs,
                       grp, n_pages, n_keep):
    del pt_ref
    ki_refs, kv_refs, o_ref = refs[:grp * n_pages], refs[grp * n_pages:2 * grp * n_pages], refs[-1]
    lane1 = lax.broadcasted_iota(I32, (1, LANE), 1)
    idx_scale = (IDX_HEADS * IDX_DIM) ** -0.5
    keys = []
    for g in range(grp):
        qi = qi_ref[g]
        wi = _bf16_round(wi_ref[g])
        tiles = []
        for p in range(n_pages):
            rel = _bf16_round(jnp.maximum(_dot(qi, ki_refs[g * n_pages + p][...]), 0.0))
            tiles.append(jnp.sum(rel * wi, axis=0, keepdims=True) * idx_scale)
        rel_new = _bf16_round(jnp.maximum(jnp.sum(qi.astype(F32) * _bf16_round(knew_ref[g]), axis=1, keepdims=True),
                                          0.0))
        sc_new = jnp.sum(rel_new * wi, axis=0, keepdims=True) * idx_scale
        tiles.append(jnp.where(lane1 == 0, sc_new, NEG))
        keys.append(_order_keys(jnp.concatenate(tiles, axis=1)))
    sels = _select_top(keys, n_keep, tri_ref[...])
    bias = bias_ref[...]
    for g in range(grp):
        q = q_ref[g]
        kvnew = _bf16_round(kvnew_ref[g])
        tiles = [_dot(q, kv_refs[g * n_pages + p][0:LANE, :]) for p in range(n_pages)]
        s_new = jnp.sum(q.astype(F32) * kvnew[:, 0:LANE], axis=1, keepdims=True)
        tiles.append(jnp.where(lane1 == 0, s_new, 0.0))
        valid = sels[g] & (jnp.concatenate([lane1] * n_pages + [lane1 + LANE], axis=1) <= LANE)
        s = jnp.where(valid, jnp.concatenate(tiles, axis=1) * ATT_SCALE + bias, NEG)
        e = jnp.exp(s - jnp.max(s, axis=1, keepdims=True))
        p_all = jnp.where(valid, e / jnp.sum(e, axis=1, keepdims=True), 0.0)
        acc = p_all[:, n_pages * LANE:n_pages * LANE + 1] * kvnew[:, LANE:2 * LANE]
        for p in range(n_pages):
            acc = acc + _dot_nt(p_all[:, p * LANE:(p + 1) * LANE], kv_refs[g * n_pages + p][LANE:2 * LANE, :])
        _head_rows_out(o_ref, g, acc)


def _page_specs(pool_t, n_pages, grp):
    r, c = pool_t.shape[1:]
    return [pl.BlockSpec((None, r, c), lambda i, pt, g=g, p=p: (pt[i * grp + g, p], 0, 0))
            for g in range(grp) for p in range(n_pages)]


def _dsa_sample(q8, qi8, wi8, knew, kvnew, bias, tri, ki_t, kv_t, page_table, *, n_keep, grp):
    b, n_pages = page_table.shape
    gspec = lambda a: pl.BlockSpec((grp,) + a.shape[1:], lambda i, pt: (i,) + (0,) * (a.ndim - 1))
    cspec = lambda a: pl.BlockSpec(a.shape, lambda i, pt: (0,) * a.ndim)
    kern = functools.partial(_dsa_sample_kernel, grp=grp, n_pages=n_pages, n_keep=n_keep)
    return pl.pallas_call(
        kern,
        grid_spec=pltpu.PrefetchScalarGridSpec(
            num_scalar_prefetch=1, grid=(b // grp,),
            in_specs=[gspec(q8), gspec(qi8), gspec(wi8), gspec(knew), gspec(kvnew), cspec(bias), cspec(tri)]
            + _page_specs(ki_t, n_pages, grp) + _page_specs(kv_t, n_pages, grp),
            out_specs=pl.BlockSpec((grp, N_ATT_HEADS, LANE), lambda i, pt: (i, 0, 0))),
        out_shape=jax.ShapeDtypeStruct((b, N_ATT_HEADS, LANE), BF16),
        compiler_params=_cparams("parallel"),
        name="dsa_sample",
    )(page_table, q8, qi8, wi8, knew, kvnew, bias, tri, *([ki_t] * (grp * n_pages)), *([kv_t] * (grp * n_pages)))


def _route(hf, wrt, br):
    logits = _dot_nt(wrt, hf)
    s = _sigmoid(logits)
    sel = s + br
    rows = [sel[e:e + 1, :] for e in range(N_EXPERTS)]
    grp = []
    for g in range(N_GROUPS):
        a = rows[g * EXPERTS_PER_GROUP:(g + 1) * EXPERTS_PER_GROUP]
        best = None
        for i in range(EXPERTS_PER_GROUP):
            for j in range(i + 1, EXPERTS_PER_GROUP):
                v = a[i] + a[j]
                best = v if best is None else jnp.maximum(best, v)
        grp.append(best)
    gbest = jnp.zeros_like(grp[0], dtype=I32)
    cur = grp[0]
    for g in range(1, N_GROUPS):
        better = grp[g] > cur
        gbest = jnp.where(better, g, gbest)
        cur = jnp.where(better, grp[g], cur)
    picked = []
    for g in range(N_GROUPS):
        a = rows[g * EXPERTS_PER_GROUP:(g + 1) * EXPERTS_PER_GROUP]
        for j in range(EXPERTS_PER_GROUP):
            rank = jnp.zeros_like(a[j])
            for jj in range(EXPERTS_PER_GROUP):
                if jj != j:
                    ahead = (a[jj] > a[j]) | (a[jj] == a[j]) if jj < j else (a[jj] > a[j])
                    rank = rank + jnp.where(ahead, 1.0, 0.0)
            e = g * EXPERTS_PER_GROUP + j
            picked.append(jnp.where((gbest == g) & (rank < 2.0), s[e:e + 1, :], 0.0))
    den = picked[0]
    for p in picked[1:]:
        den = den + p
    return jnp.concatenate([p / den for p in picked], axis=0)


def _post_tail(x, mix, gate, gf, shf, scf, wrt_ref, br_ref, x2_ref, hf_ref, cwt_ref):
    x2 = x + gate * mix
    x2_ref[...] = x2
    hf = _modulate(x2, gf, shf, scf)
    hf_ref[...] = hf.astype(BF16)
    cwt_ref[...] = _route(hf, wrt_ref[...], br_ref[...])


def _post_even_kernel(x_ref, oa_ref, bg_ref, u_ref, um1_ref, um2_ref, *rest):
    _post_even_body(x_ref, oa_ref, bg_ref, u_ref[...], um1_ref[...], um2_ref[...], *rest)


def _post_even_seq_kernel(x_ref, oa_ref, bg_ref, u_ref, uh_ref, *rest, tpb):
    u = u_ref[...]
    _post_even_body(x_ref, oa_ref, bg_ref, u, _shifted(u, uh_ref[...], 1, tpb), _shifted(u, uh_ref[...], 2, tpb),
                    *rest)


def _post_even_body(x_ref, oa_ref, bg_ref, u, um1, um2, gate_ref, shf_ref, scf_ref,
                    gf_ref, cw_ref, cb_ref, wo_ref, wrt_ref, br_ref, x2_ref, hf_ref, cwt_ref):
    cw = cw_ref[...]
    y = cb_ref[...] + cw[0:1, :] * um2
    y = y + cw[1:2, :] * um1
    y = y + cw[2:3, :] * u
    n_a = N_ATT_HEADS * LANE
    mix = (jnp.dot(oa_ref[...], wo_ref[0:n_a, :], preferred_element_type=F32)
           + jnp.dot((bg_ref[...] * y).astype(BF16), wo_ref[n_a:n_a + CONV_CH, :], preferred_element_type=F32))
    _post_tail(x_ref[...], mix, gate_ref[...], gf_ref[...], shf_ref[...], scf_ref[...], wrt_ref, br_ref,
               x2_ref, hf_ref, cwt_ref)


POST_OUTS = [(1024, F32), (1024, BF16)]


def _moe_kernel(hf_ref, cw_ref, x2_ref, gate_ref, wg_ref, wu_ref, wd_ref, o_ref, acc_ref):
    e = pl.program_id(1)

    @pl.when(e == 0)
    def _():
        acc_ref[...] = jnp.zeros_like(acc_ref)

    hf = hf_ref[...]
    hmid = _silu(_dot(hf, wg_ref[...])) * _dot(hf, wu_ref[...])
    cw = cw_ref[...]
    lane = lax.broadcasted_iota(I32, cw.shape, 1)
    wcol = jnp.sum(jnp.where(lane == e, cw, 0.0), axis=1, keepdims=True)
    acc_ref[...] += _dot(hmid, wd_ref[...]) * wcol

    @pl.when(e == N_EXPERTS - 1)
    def _():
        o_ref[...] = x2_ref[...] + gate_ref[...] * acc_ref[...]


def _moe(hf, cw, x2, gate, wg, wu, wd, layer, tile, tpb):
    rows, d = x2.shape
    de = wg.shape[3]
    return pl.pallas_call(
        _moe_kernel,
        grid=(rows // tile, N_EXPERTS),
        in_specs=[pl.BlockSpec((tile, d), lambda t, e: (t, 0)),
                  pl.BlockSpec((tile, N_EXPERTS), lambda t, e: (t, 0)),
                  pl.BlockSpec((tile, d), lambda t, e: (t, 0)),
                  pl.BlockSpec((None,) + gate.shape[1:], lambda t, e: (t // tpb, 0, 0)),
                  pl.BlockSpec((None, None, d, de), lambda t, e: (layer, e, 0, 0)),
                  pl.BlockSpec((None, None, d, de), lambda t, e: (layer, e, 0, 0)),
                  pl.BlockSpec((None, None, de, d), lambda t, e: (layer, e, 0, 0))],
        out_specs=pl.BlockSpec((tile, d), lambda t, e: (t, 0)),
        out_shape=jax.ShapeDtypeStruct((rows, d), F32),
        scratch_shapes=[pltpu.VMEM((tile, d), F32)],
        compiler_params=_cparams("parallel", "arbitrary"),
        name="moe_dense",
    )(hf, cw, x2, gate, wg, wu, wd)


MOE_WIN = LANE


def _moe_sorted_kernel(plan_ref, hf_ref, cw_ref, cwt_ref, x2_ref, gate_ref, tril_ref, g16_ref, g16t_ref,
                       wg_ref, wu_ref, wd_ref, o_ref, p_s, pt_s, xs_s, cws_s, ys_s):
    ti = pl.program_id(0)
    e = pl.program_id(1)
    t = hf_ref.shape[0]
    g = e // EXPERTS_PER_GROUP

    @pl.when(e == 0)
    def _():
        tril = tril_ref[...]
        memb_col = jnp.dot(jnp.where(cw_ref[...] > 0.0, 1.0, 0.0).astype(BF16), g16_ref[...],
                           preferred_element_type=F32) > 0.5
        memb_row = jnp.dot(g16t_ref[...], jnp.where(cwt_ref[...] > 0.0, 1.0, 0.0).astype(BF16),
                           preferred_element_type=F32) > 0.5
        mcf = jnp.where(memb_col, 1.0, 0.0)
        mrf = jnp.where(memb_row, 1.0, 0.0)
        rank_col = jnp.dot(tril, mcf.astype(BF16), preferred_element_type=F32)
        rank_row = _dot_nt(mrf, tril)
        lane = lax.broadcasted_iota(I32, (t, LANE), 1)
        row8 = lax.broadcasted_iota(I32, (8, t), 0)
        base_col = jnp.zeros((t, LANE), F32)
        base_row = jnp.zeros((8, t), F32)
        for gg in range(N_GROUPS):
            start = plan_ref[ti, gg].astype(F32)
            base_col = jnp.where(lane == gg, start, base_col)
            base_row = jnp.where(row8 == gg, start, base_row)
        slot_col = jnp.sum(mcf * (base_col + rank_col - 1.0), axis=1, keepdims=True)
        slot_row = jnp.sum(mrf * (base_row + rank_row - 1.0), axis=0, keepdims=True)
        col_iota = lax.broadcasted_iota(I32, (MOE_WIN, t), 1).astype(F32)
        row_iota = lax.broadcasted_iota(I32, (MOE_WIN, t), 0).astype(F32)
        for c in range(t // MOE_WIN):
            rows = slice(c * MOE_WIN, (c + 1) * MOE_WIN)
            p_s[rows, :] = jnp.where(slot_row == row_iota + float(c * MOE_WIN), 1.0, 0.0).astype(BF16)
            pt_s[rows, :] = jnp.where(slot_col[rows] == col_iota, 1.0, 0.0).astype(BF16)
        p = p_s[...]
        xs_s[...] = jnp.dot(p, hf_ref[...], preferred_element_type=F32).astype(BF16)
        cws_s[...] = _dot_split_rhs(p, cw_ref[...])
        ys_s[...] = jnp.zeros_like(ys_s)

    lane16 = lax.broadcasted_iota(I32, (MOE_WIN, N_EXPERTS), 1)

    def window(c, carry):
        rows = pl.ds(pl.multiple_of(c * MOE_WIN, MOE_WIN), MOE_WIN)
        xw = xs_s[rows, :]
        hmid = _silu(_dot(xw, wg_ref[...])) * _dot(xw, wu_ref[...])
        wcol = jnp.sum(jnp.where(lane16 == e, cws_s[rows, :], 0.0), axis=1, keepdims=True)
        ys_s[rows, :] = ys_s[rows, :] + _dot(hmid, wd_ref[...]) * wcol
        return carry

    lax.fori_loop(plan_ref[ti, N_GROUPS + g], plan_ref[ti, 2 * N_GROUPS + g], window, 0)

    @pl.when(e == N_EXPERTS - 1)
    def _():
        ys = ys_s[...]
        hi = ys.astype(BF16)
        lo = (ys - hi.astype(F32)).astype(BF16)
        pt = pt_s[...]
        back = jnp.dot(pt, hi, preferred_element_type=F32) + jnp.dot(pt, lo, preferred_element_type=F32)
        o_ref[...] = x2_ref[...] + gate_ref[...] * back


def _moe_plan(cwt, tile):
    n = cwt.shape[1]
    member = (cwt.reshape(N_GROUPS, EXPERTS_PER_GROUP, n // tile, tile) > 0.0).any(axis=1)
    cnt = member.sum(axis=-1).astype(I32).T
    start = jnp.cumsum(cnt, axis=1) - cnt
    lo = start // MOE_WIN
    hi = jnp.where(cnt > 0, (start + cnt + MOE_WIN - 1) // MOE_WIN, lo)
    return jnp.concatenate([start, lo, hi], axis=1)


def _moe_sorted(hf, cwt, x2, gate, wg, wu, wd, layer, tile, tpb):
    rows, d = x2.shape
    de = wg.shape[3]
    i = jnp.arange(tile)
    tril = jnp.where(i[None, :] <= i[:, None], 1.0, 0.0).astype(BF16)
    e16 = jnp.arange(N_EXPERTS)
    g16 = jnp.where(e16[:, None] // EXPERTS_PER_GROUP == jnp.arange(LANE)[None, :], 1.0, 0.0).astype(BF16)
    g16t = jnp.where(jnp.arange(8)[:, None] == e16[None, :] // EXPERTS_PER_GROUP, 1.0, 0.0).astype(BF16)
    cspec = lambda a: pl.BlockSpec(a.shape, lambda t, e, plan: (0,) * a.ndim)
    return pl.pallas_call(
        _moe_sorted_kernel,
        grid_spec=pltpu.PrefetchScalarGridSpec(
            num_scalar_prefetch=1, grid=(rows // tile, N_EXPERTS),
            in_specs=[pl.BlockSpec((tile, d), lambda t, e, plan: (t, 0)),
                      pl.BlockSpec((tile, N_EXPERTS), lambda t, e, plan: (t, 0)),
                      pl.BlockSpec((N_EXPERTS, tile), lambda t, e, plan: (0, t)),
                      pl.BlockSpec((tile, d), lambda t, e, plan: (t, 0)),
                      pl.BlockSpec((None,) + gate.shape[1:], lambda t, e, plan: (t // tpb, 0, 0)),
                      cspec(tril), cspec(g16), cspec(g16t),
                      pl.BlockSpec((None, None, d, de), lambda t, e, plan: (layer, e, 0, 0)),
                      pl.BlockSpec((None, None, d, de), lambda t, e, plan: (layer, e, 0, 0)),
                      pl.BlockSpec((None, None, de, d), lambda t, e, plan: (layer, e, 0, 0))],
            out_specs=pl.BlockSpec((tile, d), lambda t, e, plan: (t, 0)),
            scratch_shapes=[pltpu.VMEM((tile, tile), BF16), pltpu.VMEM((tile, tile), BF16),
                            pltpu.VMEM((tile, d), BF16), pltpu.VMEM((tile, N_EXPERTS), F32),
                            pltpu.VMEM((tile, d), F32)]),
        out_shape=jax.ShapeDtypeStruct((rows, d), F32),
        compiler_params=_cparams("parallel", "arbitrary"),
        name="moe_sorted",
    )(_moe_plan(cwt, tile), hf, cwt.T, cwt, x2, gate, tril, g16, g16t, wg, wu, wd)


def _cache_rows(kv_t):
    b, _, s = kv_t.shape
    return kv_t.reshape(b, 2, N_KV_HEADS, HEAD_DIM, s).transpose(0, 4, 1, 2, 3)


def _even_layer(xp, xs, mp, ms, page_table, kv_pool, kidx_pool, conv_buf, rel_bias, router, experts,
                g_mix, g_ffn, w_in, w_out, q_norm, k_norm, conv_w, conv_b):
    B, S, D = xp.shape
    Bs = xs.shape[0]
    past = page_table.shape[1] * PAGE_SIZE
    wts = _even_weights(w_in, w_out, q_norm, k_norm)
    gm = _gmats()
    wrt, br = router
    wg, wu, wd, lyr = experts
    g_mix = g_mix.reshape(1, D)
    g_ffn = g_ffn.reshape(1, D)
    post_consts = [g_ffn, conv_w, conv_b.reshape(1, CONV_CH), wts[1], wrt, br]

    tp = ROW_TILE
    q_p, _, kv_b, qi_p, misc, misc_b, bg, u, kv_t, misc_t = _inproj_even(xp.reshape(B * S, D), [mp[0], mp[1]],
                                                                        g_mix, wts, gm, tp, S // tp)
    r3 = lambda a: a.reshape(B, S, a.shape[-1])
    oa = _dsa_t(r3(q_p), r3(qi_p), r3(misc), r3(misc_b), r3(kv_b), _bias_tiles_t(rel_bias), gm[2].T,
                n_keep=min(TOPK_MAX, S // 4))
    u3 = r3(u)
    x2, hf, cwt = _row_call(functools.partial(_post_even_seq_kernel, tpb=S // tp),
                            [xp.reshape(B * S, D), oa.reshape(B * S, -1), bg, u],
                            [mp[2], mp[3], mp[4]], post_consts, POST_OUTS, tp, S // tp, "post_even",
                            outs_t=[(N_EXPERTS, F32)], halos=[3])
    tm = min(MOE_TILE, S)
    xp3 = _moe_sorted(hf, cwt, x2, mp[5], wg, wu, wd, lyr, tm, S // tm).reshape(B, S, D)
    outs_p = (_cache_rows(kv_t), misc_t[:, :IDX_DIM].transpose(0, 2, 1), u3[:, S - 2:])

    q_s, kv_fs, _, qi_s, misc_s, _, bg_s, u_s, _, _ = _inproj_even(xs, [ms[0], ms[1]], g_mix, wts, gm, Bs, 1)
    n_pool = kv_pool.shape[0]
    kv_t = kv_pool.transpose(0, 2, 3, 4, 1).reshape(n_pool, 2 * LANE, PAGE_SIZE)
    ki_t = kidx_pool.transpose(0, 2, 1)
    qi8 = jnp.pad(qi_s.reshape(Bs, IDX_HEADS, LANE)[:, :, :IDX_DIM], ((0, 0), (0, N_ATT_HEADS - IDX_HEADS), (0, 0)))
    wi8 = jnp.pad(misc_s[:, IDX_DIM:IDX_DIM + IDX_HEADS], ((0, 0), (0, N_ATT_HEADS - IDX_HEADS)))[:, :, None]
    oa_s = _dsa_sample(q_s.reshape(Bs, N_ATT_HEADS, LANE), qi8, wi8, misc_s[:, None, :IDX_DIM], kv_fs[:, None, :],
                       _bias_row(rel_bias, past), gm[2], ki_t, kv_t, page_table,
                       n_keep=min(TOPK_MAX, (past + 1) // 4), grp=_pick_group(Bs, SAMPLE_GROUP)).reshape(Bs, -1)
    x2s, hfs, cwts = _row_call(_post_even_kernel, [xs, oa_s, bg_s, u_s, conv_buf[:, 1], conv_buf[:, 0]],
                               [ms[2], ms[3], ms[4]], post_consts, POST_OUTS, Bs, 1, "post_even_s",
                               outs_t=[(N_EXPERTS, F32)])
    xs3 = _moe(hfs, cwts.T, x2s, ms[5], wg, wu, wd, lyr, Bs, 1)
    outs_s = (kv_fs.reshape(Bs, 1, 2, N_KV_HEADS, HEAD_DIM), misc_s[:, None, :IDX_DIM],
              jnp.concatenate([conv_buf[:, 1:], u_s[:, None, :]], axis=1))
    return xp3, xs3, outs_p, outs_s


O_Z0, O_Q0, O_KVC0, O_KVS0, O_KVW0, O_G0, O_END = 0, 1792, 2816, 3072, 3328, 3584, 3712
P_C = 3 * RWKV_DIM + LORA_W + LORA_A + LORA_G


def _inproj_odd_kernel(x_ref, shift_ref, scale_ref, g_ref, w_ref, qg_ref, ksg_ref, kwg_ref, gone_ref, gtwo_ref,
                       zc_ref, q_ref, kvc_ref, kvs_ref, kvsb_ref, kvw_ref, kvwb_ref, gates_ref,
                       kvct_ref, kvst_ref, kvwt_ref):
    h = _modulate(x_ref[...], g_ref[...], shift_ref[...], scale_ref[...])
    z = jnp.dot(h.astype(BF16), w_ref[...], preferred_element_type=F32)
    zc_ref[...] = z[:, O_Z0:O_Q0]
    gone = gone_ref[...]
    gtwo = gtwo_ref[...]
    for t in range(N_ATT_HEADS):
        sl = slice(t * LANE, (t + 1) * LANE)
        q_ref[:, sl] = _group_rms(z[:, O_Q0 + t * LANE:O_Q0 + (t + 1) * LANE], gone, qg_ref[:, sl]).astype(BF16)
    kvc_ref[...] = z[:, O_KVC0:O_KVS0]
    kvct_ref[0:LANE, :] = z[:, O_KVC0:O_KVC0 + LANE].T
    kvct_ref[LANE:2 * LANE, :] = z[:, O_KVC0 + LANE:O_KVS0].T
    for base, gain_ref, f_ref, b_ref, t_ref in ((O_KVS0, ksg_ref, kvs_ref, kvsb_ref, kvst_ref),
                                                (O_KVW0, kwg_ref, kvw_ref, kvwb_ref, kvwt_ref)):
        k = _group_rms(z[:, base:base + LANE], gtwo, gain_ref[...])
        v = z[:, base + LANE:base + 2 * LANE]
        t_ref[0:LANE, :] = k.T
        t_ref[LANE:2 * LANE, :] = v.T
        f_ref[:, 0:LANE] = k
        f_ref[:, LANE:2 * LANE] = v
        b_ref[:, 0:LANE] = k.astype(BF16)
        b_ref[:, LANE:2 * LANE] = v.astype(BF16)
    gates_ref[...] = _sigmoid(z[:, O_G0:O_END])


def _odd_weights(w_in, w_out, q_norm, k_norm):
    d = w_in.shape[0]
    a_q, a_kv = N_ATT_HEADS * HEAD_DIM, 2 * N_KV_HEADS * HEAD_DIM
    o = P_C
    wz = w_in[:, :o]
    wq = w_in[:, o:o + a_q]; o += a_q
    wkv = w_in[:, o:o + 3 * a_kv]; o += 3 * a_kv
    wg = w_in[:, o:]
    wg = jnp.concatenate([wg, jnp.zeros((d, LANE - wg.shape[1]), F32)], -1)
    w_in_p = jnp.concatenate([wz, _pad_q_cols(wq), wkv, wg], -1).astype(BF16)
    w_out_p = jnp.concatenate([w_out[:RWKV_DIM], _pad_o_rows(w_out[RWKV_DIM:])], 0).astype(BF16)
    qg = jnp.tile(q_norm, 2 * N_ATT_HEADS).reshape(1, N_ATT_HEADS * LANE)
    ksg = jnp.tile(k_norm[1], 2).reshape(1, LANE)
    kwg = jnp.tile(k_norm[2], 2).reshape(1, LANE)
    return w_in_p, w_out_p, qg, ksg, kwg


def _inproj_odd(x, mods, g, wts, gm, tile, tpb):
    w_in_p, _, qg, ksg, kwg = wts
    gone, gtwo, _ = gm
    outs = [(P_C, F32), (N_ATT_HEADS * LANE, BF16), (2 * LANE, F32), (2 * LANE, F32), (2 * LANE, BF16),
            (2 * LANE, F32), (2 * LANE, BF16), (LANE, F32)]
    return _row_call(_inproj_odd_kernel, x, mods, [g, w_in_p, qg, ksg, kwg, gone, gtwo], outs, tile, tpb,
                     "inproj_odd", outs_bt=[(2 * LANE, F32)] * 3)


def _rwkv_pre_kernel(z_ref, zp_ref, *rest):
    _rwkv_pre_body(z_ref[...], zp_ref[...], *rest)


def _rwkv_pre_seq_kernel(z_ref, zh_ref, *rest, tpb):
    z = z_ref[...]
    _rwkv_pre_body(z, _shifted(z, zh_ref[...], 1, tpb), *rest)


def _rwkv_pre_body(z, zp, mu_ref, w0_ref, a0_ref, kk_ref, ka_ref, wup_ref, aup_ref, gup_ref, gsum_ref,
                   r_o, w_o, k_o, v_o, kk_o, kka_o, g_o):
    zm = z + (zp - z) * mu_ref[...]
    r = zm[:, 0:RWKV_DIM]
    k = zm[:, RWKV_DIM:2 * RWKV_DIM]
    v = zm[:, 2 * RWKV_DIM:3 * RWKV_DIM]
    t12 = zm[:, 3 * RWKV_DIM:3 * RWKV_DIM + LANE]
    gd = zm[:, 3 * RWKV_DIM + LANE:P_C]
    xw = w0_ref[...] + _dot(jnp.tanh(t12), wup_ref[...])
    sp = jnp.maximum(-xw, 0.0) + jnp.log(1.0 + jnp.exp(-jnp.abs(xw)))
    w_o[...] = jnp.exp(-jnp.exp(-sp - 0.5))
    a = _sigmoid(a0_ref[...] + _dot(t12, aup_ref[...]))
    g_o[...] = _dot(_sigmoid(gd), gup_ref[...])
    kk = k * kk_ref[...]
    gsum = gsum_ref[...]
    for t in range(RWKV_DIM // LANE):
        sl = slice(t * LANE, (t + 1) * LANE)
        kt = kk[:, sl]
        nrm = jnp.maximum(jnp.sqrt(_dot_split(kt * kt, gsum)), 1e-12)
        kn = kt / nrm
        kk_o[:, sl] = kn
        kka_o[:, sl] = kn * a[:, sl]
    r_o[...] = r
    v_o[...] = v
    k_o[...] = k * (1.0 + (a - 1.0) * ka_ref[...])


def _rwkv_pre(zc, zprev, cpar, gsum, tile, tpb=None):
    mu, w0, w_up, a0, a_up, g_up, k_k, k_a = cpar
    z64 = jnp.zeros((LORA_W, RWKV_DIM), F32)
    consts = [mu.reshape(1, P_C), w0.reshape(1, -1), a0.reshape(1, -1), k_k.reshape(1, -1), k_a.reshape(1, -1),
              jnp.concatenate([w_up, z64], 0).astype(BF16), jnp.concatenate([z64, a_up], 0).astype(BF16),
              g_up.astype(BF16), gsum]
    outs = [(RWKV_DIM, F32)] * 7
    if zprev is None:
        return _row_call(functools.partial(_rwkv_pre_seq_kernel, tpb=tpb), [zc], [], consts, outs, tile, 1,
                         "rwkv_pre", halos=[0])
    return _row_call(_rwkv_pre_kernel, [zc, zprev], [], consts, outs, tile, 1, "rwkv_pre_s")


SCAN_P = 64
SCAN_VH = HEAD_DIM // 2
SCAN_SEQS = SCAN_P // RWKV_HEADS


def _scan_kernel(kk_ref, w_ref, kka_ref, k_ref, r_ref, v_ref, s0_ref, y_ref, so_ref, st, *, tc):
    ti = pl.program_id(1)

    @pl.when(ti == 0)
    def _():
        st[...] = s0_ref[...]

    lo_half = lax.broadcasted_iota(I32, (1, LANE), 1) < SCAN_P

    def lanes_of(ref, t):
        x = ref[:, t]
        xs = [x[:, h * HEAD_DIM:(h + 1) * HEAD_DIM] for h in range(RWKV_HEADS)]
        return jnp.concatenate(xs + xs, axis=0).T

    def step(t, c):
        kk, w, kka, kt, rt, vf = (lanes_of(ref, t) for ref in (kk_ref, w_ref, kka_ref, k_ref, r_ref, v_ref))
        ys = []
        for vi in range(SCAN_VH):
            s = st[vi]
            sa = -jnp.sum(s * kk, axis=0, keepdims=True)
            vrow = jnp.where(lo_half, vf[vi:vi + 1, :], vf[vi + SCAN_VH:vi + SCAN_VH + 1, :])
            sn = s * w + sa * kka + vrow * kt
            st[vi] = sn
            ys.append(jnp.sum(sn * rt, axis=0, keepdims=True))
        y_ref[t] = jnp.concatenate(ys, axis=0)
        return c

    lax.fori_loop(0, tc, step, 0)

    @pl.when(ti == pl.num_programs(1) - 1)
    def _():
        so_ref[...] = st[...]


def _scan_unlayout_y(y, b, t):
    nc = y.shape[0]
    a = jnp.concatenate([y[..., :SCAN_P], y[..., SCAN_P:]], axis=2)
    a = a.reshape(nc, t, HEAD_DIM, RWKV_HEADS, SCAN_SEQS).transpose(0, 4, 1, 3, 2)
    return a.reshape(nc * SCAN_SEQS, t, RWKV_DIM)[:b].reshape(b * t, RWKV_DIM)


def _scan_layout_state(s):
    b = s.shape[0]
    nc = -(-b // SCAN_SEQS)
    a = jnp.pad(s, ((0, nc * SCAN_SEQS - b), (0, 0), (0, 0), (0, 0)))
    a = a.reshape(nc, SCAN_SEQS, RWKV_HEADS, HEAD_DIM, HEAD_DIM).transpose(0, 3, 4, 2, 1)
    a = a.reshape(nc, HEAD_DIM, HEAD_DIM, SCAN_P)
    return jnp.concatenate([a[:, :SCAN_VH], a[:, SCAN_VH:]], -1)


def _scan_unlayout_state(st, b):
    nc = st.shape[0]
    a = jnp.concatenate([st[..., :SCAN_P], st[..., SCAN_P:]], axis=1)
    a = a.reshape(nc, HEAD_DIM, HEAD_DIM, RWKV_HEADS, SCAN_SEQS).transpose(0, 4, 3, 1, 2)
    return a.reshape(nc * SCAN_SEQS, RWKV_HEADS, HEAD_DIM, HEAD_DIM)[:b]


def _rwkv_scan(pre, s0, b, t, tc):
    r, w, k, v, kk, kka, _ = pre
    bp = -(-b // SCAN_SEQS) * SCAN_SEQS
    ops = [jnp.pad(x.reshape(b, t, RWKV_DIM), ((0, bp - b), (0, 0), (0, 0))) for x in (kk, w, kka, k, r, v)]
    s0l = _scan_layout_state(s0)
    nc = s0l.shape[0]
    kspec = pl.BlockSpec((SCAN_SEQS, tc, RWKV_DIM), lambda c, i: (c, i, 0))
    vspec = pl.BlockSpec((None, tc, SCAN_VH, LANE), lambda c, i: (c, i, 0, 0))
    sspec = pl.BlockSpec((None, SCAN_VH, HEAD_DIM, LANE), lambda c, i: (c, 0, 0, 0))
    y, so = pl.pallas_call(
        functools.partial(_scan_kernel, tc=tc),
        grid=(nc, t // tc),
        in_specs=[kspec] * 6 + [sspec],
        out_specs=[vspec, sspec],
        out_shape=[jax.ShapeDtypeStruct((nc, t, SCAN_VH, LANE), F32),
                   jax.ShapeDtypeStruct((nc, SCAN_VH, HEAD_DIM, LANE), F32)],
        scratch_shapes=[pltpu.VMEM((SCAN_VH, HEAD_DIM, LANE), F32)],
        compiler_params=_cparams("parallel", "arbitrary"),
        name="rwkv_scan",
    )(*ops, s0l)
    return _scan_unlayout_y(y, b, t), _scan_unlayout_state(so, b)


def _compress_kernel(x_ref, pe_ref, w_ref, kg_ref, gtwo_ref, o_ref):
    z = jnp.dot((x_ref[...] + pe_ref[...]).astype(BF16), w_ref[...], preferred_element_type=F32)
    o_ref[:, 0:LANE] = _group_rms(z[:, 0:LANE], gtwo_ref[...], kg_ref[...])
    o_ref[:, LANE:2 * LANE] = z[:, LANE:2 * LANE]


def _compress_weights(cmp_pe, cmp_w, k_norm_c):
    wk = cmp_w[0].reshape(CMP_BLOCK, HEAD_DIM, HEAD_DIM)
    wv = cmp_w[1].reshape(CMP_BLOCK, HEAD_DIM, HEAD_DIM)
    full = jnp.einsum("srde,st->rsdte", jnp.stack([wk, wk, wv, wv]), jnp.eye(4, dtype=F32))
    pe = jnp.stack([cmp_pe[0], cmp_pe[0], cmp_pe[1], cmp_pe[1]], axis=1)
    return (full.reshape(CMP_BLOCK * 4 * HEAD_DIM, 4 * HEAD_DIM).astype(BF16), pe.reshape(1, -1),
            jnp.tile(k_norm_c, 2).reshape(1, LANE))


def _compress(rows, cw, gtwo, tile):
    wfull, pe, kg = cw
    return _row_call(_compress_kernel, rows, [], [pe, wfull, kg, gtwo], [(2 * LANE, F32)], tile, 1, "nsa_compress")[0]


def _compress_paged_kernel(pt_ref, ident_ref, pe_ref, w_ref, kg_ref, gtwo_ref, *refs, grp, n_pages):
    del pt_ref
    page_refs, o_ref, xs = refs[:grp * n_pages], refs[-2], refs[-1]
    ident = ident_ref[...]
    for i in range(grp * n_pages):
        for half in range(2):
            xt = (page_refs[i][half * LANE:(half + 1) * LANE, :] + pe_ref[half]).astype(BF16)
            xs[half, i * PAGE_SIZE:(i + 1) * PAGE_SIZE, :] = _dot_nt(ident, xt)
    n_blk = grp * n_pages * (PAGE_SIZE // CMP_BLOCK)
    acc = [jnp.zeros((n_blk, LANE), F32) for _ in range(2)]
    for r in range(CMP_BLOCK):
        for half in range(2):
            rows = xs[half, pl.ds(r, n_blk, stride=CMP_BLOCK), :]
            acc[half] = acc[half] + jnp.dot(rows.astype(BF16), w_ref[half, r], preferred_element_type=F32)
    kc = _group_rms(acc[0], gtwo_ref[...], kg_ref[...])
    per_seq = n_blk // grp
    for g in range(grp):
        o_ref[g, 0:per_seq, 0:LANE] = kc[g * per_seq:(g + 1) * per_seq].astype(BF16)
        o_ref[g, 0:per_seq, LANE:2 * LANE] = acc[1][g * per_seq:(g + 1) * per_seq].astype(BF16)
        o_ref[g, per_seq:LANE, :] = jnp.zeros((LANE - per_seq, 2 * LANE), BF16)


def _compress_paged(cmp_t, page_table, cmp_pe, cmp_w, k_norm_c, gtwo, grp):
    b, n_pages = page_table.shape
    wk = cmp_w[0].reshape(CMP_BLOCK, HEAD_DIM, HEAD_DIM)
    wv = cmp_w[1].reshape(CMP_BLOCK, HEAD_DIM, HEAD_DIM)
    wbd = jnp.einsum("krde,ht->krhdte", jnp.stack([wk, wv]), jnp.eye(N_KV_HEADS, dtype=F32))
    wbd = wbd.reshape(2, CMP_BLOCK, LANE, LANE).astype(BF16)
    pe = jnp.tile(cmp_pe.transpose(0, 2, 1), (1, N_KV_HEADS, PAGE_SIZE // CMP_BLOCK))
    ident = jnp.eye(LANE, dtype=BF16)
    kg = jnp.tile(k_norm_c, 2).reshape(1, LANE)
    cspec = lambda a: pl.BlockSpec(a.shape, lambda i, pt: (0,) * a.ndim)
    kern = functools.partial(_compress_paged_kernel, grp=grp, n_pages=n_pages)
    return pl.pallas_call(
        kern,
        grid_spec=pltpu.PrefetchScalarGridSpec(
            num_scalar_prefetch=1, grid=(b // grp,),
            in_specs=[cspec(ident), cspec(pe), cspec(wbd), cspec(kg), cspec(gtwo)]
            + _page_specs(cmp_t, n_pages, grp),
            out_specs=pl.BlockSpec((grp, LANE, 2 * LANE), lambda i, pt: (i, 0, 0)),
            scratch_shapes=[pltpu.VMEM((2, grp * n_pages * PAGE_SIZE, LANE), F32)]),
        out_shape=jax.ShapeDtypeStruct((b, LANE, 2 * LANE), BF16),
        compiler_params=_cparams("parallel"),
        name="nsa_compress_paged",
    )(page_table, ident, pe, wbd, kg, gtwo, *([cmp_t] * (grp * n_pages)))


def _bias_cmp(rel_bias, q_starts, qb):
    q0 = jnp.asarray(q_starts, I32)[:, None, None]
    r = jnp.arange(qb)[None, :, None]
    n = jnp.arange(LANE)[None, None, :]
    return _bias_lookup(rel_bias, q0 + r - (n * CMP_BLOCK + CMP_BLOCK - 1)).transpose(0, 3, 1, 2)


def _nsa_sample_kernel(pt_ref, q_ref, gates_ref, kcv_ref, snew_ref, wnew_ref, win_ref, bias_ref, biasc_ref,
                       biasw_ref, pair_ref, *refs, grp, n_pages, t_pos, n_cmp, n_sel, w_eff):
    del pt_ref
    slc_refs, o_ref = refs[:grp * n_pages], refs[-1]
    lane1 = lax.broadcasted_iota(I32, (1, LANE), 1)
    lane8 = lax.broadcasted_iota(I32, (N_ATT_HEADS, LANE), 1)
    row_all = lax.broadcasted_iota(I32, (N_ATT_HEADS, (n_pages + 1) * LANE), 0)
    lanew = lax.broadcasted_iota(I32, (N_ATT_HEADS, w_eff), 1)
    lane1_f = lane1.astype(F32)
    bias = bias_ref[...]
    bias_now = bias[:, n_pages * LANE:n_pages * LANE + 1]
    mask_c = ((lane8 * CMP_BLOCK + CMP_BLOCK - 1) <= t_pos) & (lane8 < n_cmp)
    cur = t_pos // SEL_BLOCK
    forced = (lane1 == 0) | (lane1 == cur) | (lane1 == cur - 1)
    sel_causal = lane1 * SEL_BLOCK <= t_pos
    tail_valid = jnp.concatenate([lane1] * n_pages + [lane1 + LANE], axis=1) <= LANE
    for g in range(grp):
        q = q_ref[g]
        qf = q.astype(F32)
        s = jnp.where(mask_c, _dot_nt(q, kcv_ref[g, :, 0:LANE]) * ATT_SCALE + biasc_ref[...], NEG)
        e = jnp.exp(s - jnp.max(s, axis=1, keepdims=True))
        pc = jnp.where(mask_c, e / jnp.sum(e, axis=1, keepdims=True), 0.0)
        o_c = jnp.dot(pc.astype(BF16), kcv_ref[g, :, LANE:2 * LANE], preferred_element_type=F32)
        masks = []
        for hk in range(N_KV_HEADS):
            ps = jnp.sum(pc[hk * GQA:(hk + 1) * GQA], axis=0, keepdims=True)
            score = _dot_split(ps, pair_ref[...])
            score = jnp.where(sel_causal, jnp.where(forced, FORCE, score), NEG)
            picked = jnp.zeros((1, LANE), F32)
            for _ in range(n_sel):
                mx = jnp.max(score, axis=1, keepdims=True)
                first = jnp.min(jnp.where(score == mx, lane1_f, float(LANE)), axis=1, keepdims=True)
                hit = lane1_f == first
                picked = jnp.where(hit, 1.0, picked)
                score = jnp.where(hit, TAKEN, score)
            per_page = PAGE_SIZE // SEL_BLOCK
            tiles = []
            for p in range(n_pages + 1):
                t = jnp.zeros((1, LANE), F32)
                for a in range(per_page):
                    blk = picked[:, p * per_page + a:p * per_page + a + 1]
                    t = jnp.where(lane1 // SEL_BLOCK == a, blk, t)
                tiles.append(t)
            masks.append(jnp.concatenate(tiles, axis=1))
        valid = (jnp.where(row_all < GQA, masks[0], masks[1]) > 0.5) & tail_valid
        snew = _bf16_round(snew_ref[g])
        tiles = [_dot(q, slc_refs[g * n_pages + p][0:LANE, :]) for p in range(n_pages)]
        tiles.append(jnp.where(lane1 == 0, jnp.sum(qf * snew[:, 0:LANE], axis=1, keepdims=True), 0.0))
        s = jnp.where(valid, jnp.concatenate(tiles, axis=1) * ATT_SCALE + bias, NEG)
        e = jnp.exp(s - jnp.max(s, axis=1, keepdims=True))
        p_all = jnp.where(valid, e / jnp.sum(e, axis=1, keepdims=True), 0.0)
        o_s = p_all[:, n_pages * LANE:n_pages * LANE + 1] * snew[:, LANE:2 * LANE]
        for p in range(n_pages):
            o_s = o_s + _dot_nt(p_all[:, p * LANE:(p + 1) * LANE], slc_refs[g * n_pages + p][LANE:2 * LANE, :])
        wnew = _bf16_round(wnew_ref[g])
        valid_w = (w_eff - lanew) < WINDOW
        s_w = jnp.where(valid_w, _dot(q, win_ref[g, 0:LANE, :]) * ATT_SCALE + biasw_ref[...], NEG)
        s_n = jnp.sum(qf * wnew[:, 0:LANE], axis=1, keepdims=True) * ATT_SCALE + bias_now
        m = jnp.maximum(jnp.max(s_w, axis=1, keepdims=True), s_n)
        e_w = jnp.where(valid_w, jnp.exp(s_w - m), 0.0)
        e_n = jnp.exp(s_n - m)
        den = jnp.sum(e_w, axis=1, keepdims=True) + e_n
        o_w = _dot_nt(e_w / den, win_ref[g, LANE:2 * LANE, :]) + (e_n / den) * wnew[:, LANE:2 * LANE]
        gates = gates_ref[g]
        _head_rows_out(o_ref, g, gates[:, 0:1] * o_c + gates[:, 1:2] * o_s + gates[:, 2:3] * o_w)


def _nsa_sample(q8, gates8, kcv, snew, wnew, win_t, bias, bias_c, bias_w, pair, slc_t, page_table,
                *, t_pos, n_cmp, n_sel, grp):
    b, n_pages = page_table.shape
    w_eff = win_t.shape[2]
    gspec = lambda a: pl.BlockSpec((grp,) + a.shape[1:], lambda i, pt: (i,) + (0,) * (a.ndim - 1))
    cspec = lambda a: pl.BlockSpec(a.shape, lambda i, pt: (0,) * a.ndim)
    kern = functools.partial(_nsa_sample_kernel, grp=grp, n_pages=n_pages, t_pos=t_pos, n_cmp=n_cmp, n_sel=n_sel,
                             w_eff=w_eff)
    return pl.pallas_call(
        kern,
        grid_spec=pltpu.PrefetchScalarGridSpec(
            num_scalar_prefetch=1, grid=(b // grp,),
            in_specs=[gspec(q8), gspec(gates8), gspec(kcv), gspec(snew), gspec(wnew), gspec(win_t), cspec(bias),
                      cspec(bias_c), cspec(bias_w), cspec(pair)] + _page_specs(slc_t, n_pages, grp),
            out_specs=pl.BlockSpec((grp, N_ATT_HEADS, LANE), lambda i, pt: (i, 0, 0))),
        out_shape=jax.ShapeDtypeStruct((b, N_ATT_HEADS, LANE), BF16),
        compiler_params=_cparams("parallel"),
        name="nsa_sample",
    )(page_table, q8, gates8, kcv, snew, wnew, win_t, bias, bias_c, bias_w, pair, *([slc_t] * (grp * n_pages)))


def _post_odd_kernel(x_ref, y_ref, r_ref, k_ref, v_ref, g_ref, od_ref, gate_ref, shf_ref, scf_ref,
                     gf_ref, lnw_ref, lnb_ref, rk_ref, gtwo_ref, wo_ref, wrt_ref, br_ref, x2_ref, hf_ref, cwt_ref):
    gtwo = gtwo_ref[...]
    mix = jnp.dot(od_ref[...], wo_ref[RWKV_DIM:RWKV_DIM + N_ATT_HEADS * LANE, :], preferred_element_type=F32)
    for t in range(RWKV_DIM // LANE):
        sl = slice(t * LANE, (t + 1) * LANE)
        y = y_ref[:, sl]
        dlt = y - _dot_split(y, gtwo)
        yn = (dlt * lax.rsqrt(_dot_split(dlt * dlt, gtwo) + GN_EPS)) * lnw_ref[:, sl] + lnb_ref[:, sl]
        dot_rk = _dot_split(r_ref[:, sl] * k_ref[:, sl] * rk_ref[:, sl], gtwo) * float(HEAD_DIM)
        oc = (yn + dot_rk * v_ref[:, sl]) * g_ref[:, sl]
        mix = mix + jnp.dot(oc.astype(BF16), wo_ref[sl, :], preferred_element_type=F32)
    _post_tail(x_ref[...], mix, gate_ref[...], gf_ref[...], shf_ref[...], scf_ref[...], wrt_ref, br_ref,
               x2_ref, hf_ref, cwt_ref)


def _odd_layer(xp, xs, mp, ms, page_table, wkv0, shift0, cmp_pool, slc_pool, win_buf, rel_bias, router, experts,
               g_mix, g_ffn, w_in, w_out, cpar, r_k, ln_w, ln_b, q_norm, k_norm, cmp_pe, cmp_w):
    B, S, D = xp.shape
    Bs = xs.shape[0]
    n_pages = page_table.shape[1]
    past = n_pages * PAGE_SIZE
    wts = _odd_weights(w_in, w_out, q_norm, k_norm)
    gm = _gmats()
    gone, gtwo, _ = gm
    gsum = (gtwo.astype(F32) * HEAD_DIM).astype(BF16)
    i = jnp.arange(LANE)
    pair = jnp.where(i[:, None] // 2 == i[None, :], 1.0, 0.0).astype(BF16)
    cw = _compress_weights(cmp_pe, cmp_w, k_norm[0])
    wrt, br = router
    wg, wu, wd, lyr = experts
    g_mix = g_mix.reshape(1, D)
    post_consts = [g_ffn.reshape(1, D), ln_w.reshape(1, -1), ln_b.reshape(1, -1), r_k.reshape(1, -1), gtwo, wts[1],
                   wrt, br]
    w_eff = win_buf.shape[1]

    tp = ROW_TILE
    zc, q_p, kvc, _, kvs_b, _, kvw_b, gates, kvc_t, kvs_t, kvw_t = _inproj_odd(xp.reshape(B * S, D), [mp[0], mp[1]],
                                                                               g_mix, wts, gm, tp, S // tp)
    r3 = lambda a: a.reshape(B, S, a.shape[-1])
    pre = _rwkv_pre(zc, None, cpar, gsum, tp, S // tp)
    y, wkv_p = _rwkv_scan(pre, jnp.zeros((B, RWKV_HEADS, HEAD_DIM, HEAD_DIM), F32), B, S, SCAN_TIME_CHUNK)
    n_cmp = S // CMP_BLOCK
    kcv = _compress(kvc.reshape(B * n_cmp, CMP_BLOCK * 2 * LANE), cw, gtwo, _pick_tile(B * n_cmp, 256))
    kcv = jnp.pad(kcv.reshape(B, n_cmp, 2 * LANE), ((0, 0), (0, LANE - n_cmp), (0, 0))).astype(BF16)
    n_slc = -(-S // SEL_BLOCK)
    od = _nsa_t(r3(q_p), r3(gates), kcv, r3(kvs_b), r3(kvw_b), _bias_tiles_t(rel_bias), _bias_cmp_t(rel_bias, S),
                pair.T, n_cmp=n_cmp, n_sel=min(N_SEL_BLOCKS, n_slc))
    x2, hf, cwt = _row_call(_post_odd_kernel,
                            [xp.reshape(B * S, D), y, pre[0], pre[2], pre[3], pre[6], od.reshape(B * S, -1)],
                            [mp[2], mp[3], mp[4]], post_consts, POST_OUTS, tp, S // tp, "post_odd",
                            outs_t=[(N_EXPERTS, F32)])
    tm = min(MOE_TILE, S)
    xp3 = _moe_sorted(hf, cwt, x2, mp[5], wg, wu, wd, lyr, tm, S // tm).reshape(B, S, D)
    kv5 = lambda a, n: a.reshape(-1, n, 2, N_KV_HEADS, HEAD_DIM)
    outs_p = (wkv_p, r3(zc)[:, S - 1], _cache_rows(kvc_t), _cache_rows(kvs_t),
              _cache_rows(kvw_t[:, :, S - min(WINDOW, S):]))

    zc_s, q_s, kvc_s, kvs_s, _, kvw_s, _, gates_s, _, _, _ = _inproj_odd(xs, [ms[0], ms[1]], g_mix, wts, gm, Bs, 1)
    pre_s = _rwkv_pre(zc_s, shift0, cpar, gsum, Bs)
    y_s, wkv_s = _rwkv_scan(pre_s, wkv0, Bs, 1, 1)
    n_pool = cmp_pool.shape[0]
    cmp_t = cmp_pool.transpose(0, 2, 3, 4, 1).reshape(n_pool, 2 * LANE, PAGE_SIZE)
    kcv_s = _compress_paged(cmp_t, page_table, cmp_pe, cmp_w, k_norm[0], gtwo, _pick_group(Bs, SAMPLE_GROUP))
    n_cmp_s = (past + 1) // CMP_BLOCK
    slc_t = slc_pool.transpose(0, 2, 3, 4, 1).reshape(n_pool, 2 * LANE, PAGE_SIZE)
    win_t = win_buf.transpose(0, 2, 3, 4, 1).reshape(Bs, 2 * LANE, w_eff)
    gates8 = jnp.pad(gates_s[:, :3 * N_ATT_HEADS].reshape(Bs, 3, N_ATT_HEADS).transpose(0, 2, 1),
                     ((0, 0), (0, 0), (0, LANE - 3)))
    n_slc_s = -(-(past + 1) // SEL_BLOCK)
    od_s = _nsa_sample(q_s.reshape(Bs, N_ATT_HEADS, LANE), gates8, kcv_s, kvs_s[:, None, :], kvw_s[:, None, :],
                       win_t, _bias_row(rel_bias, past), _bias_cmp(rel_bias, [past], 1)[0, :, 0, :],
                       _bias_lookup(rel_bias, w_eff - jnp.arange(w_eff)).T, pair, slc_t, page_table,
                       t_pos=past, n_cmp=n_cmp_s, n_sel=min(N_SEL_BLOCKS, n_slc_s),
                       grp=_pick_group(Bs, SAMPLE_GROUP)).reshape(Bs, -1)
    x2s, hfs, cwts = _row_call(_post_odd_kernel, [xs, y_s, pre_s[0], pre_s[2], pre_s[3], pre_s[6], od_s],
                               [ms[2], ms[3], ms[4]], post_consts, POST_OUTS, Bs, 1, "post_odd_s",
                               outs_t=[(N_EXPERTS, F32)])
    xs3 = _moe(hfs, cwts.T, x2s, ms[5], wg, wu, wd, lyr, Bs, 1)
    win_new = jnp.concatenate([win_buf[:, 1:], kv5(kvw_s, 1)], axis=1)
    outs_s = (wkv_s, zc_s, kv5(kvc_s, 1), kv5(kvs_s, 1), win_new)
    return xp3, xs3, outs_p, outs_s


def _mods(c_p, c_s, w_all, b, layer):
    nb = c_p.shape[0]
    m = _ada(jnp.concatenate([c_p, c_s], 0), w_all, b, layer)
    parts = jnp.split(m, 6, axis=-1)
    return [p[:nb, None, :] for p in parts], [p[None, nb:, :] for p in parts]


def _forward(x_prompt, x_sample, c_prompt, c_sample, page_table, cache_a_kv, cache_a_kidx, state_b_conv,
             state_c_wkv, state_c_shift, cache_d_cmp, cache_d_slc, cache_d_win, rel_bias, w_router, b_router,
             w_ada, b_ada, g_norm_mix, g_norm_ffn, w_expert_gate, w_expert_up, w_expert_down, e_w_in, e_w_out,
             a_q_norm, a_k_norm, b_conv_w, b_conv_b, o_w_in, o_w_out, c_mu, c_w0, c_w_up, c_a0, c_a_up,
             c_g_up, c_k_k, c_k_a, c_r_k, c_ln_w, c_ln_b, d_q_norm, d_k_norm, d_cmp_pe, d_cmp_w):
    assert w_ada.shape[0] == 2 and e_w_in.shape[0] == 1 and o_w_in.shape[0] == 1
    B, S, D = x_prompt.shape
    Bs = x_sample.shape[0]
    assert x_sample.shape[1] == 1
    xp, xs = x_prompt, x_sample.reshape(Bs, D)
    router = (w_router.T, b_router.reshape(N_EXPERTS, 1))
    n_pool = cache_a_kv.shape[1]
    wbf = tuple(w.astype(BF16) for w in (w_expert_gate, w_expert_up, w_expert_down))
    experts = lambda l: wbf + (l,)

    mp, ms = _mods(c_prompt, c_sample, w_ada, b_ada[0], 0)
    xp, xs, ep, es = _even_layer(xp, xs, mp, ms, page_table, cache_a_kv[0], cache_a_kidx[0], state_b_conv[0],
                                 rel_bias, router, experts(0), g_norm_mix[0], g_norm_ffn[0], e_w_in[0], e_w_out[0],
                                 a_q_norm[0], a_k_norm[0], b_conv_w[0], b_conv_b[0])
    mp, ms = _mods(c_prompt, c_sample, w_ada, b_ada[1], 1)
    cpar = (c_mu[0], c_w0[0], c_w_up[0], c_a0[0], c_a_up[0], c_g_up[0], c_k_k[0], c_k_a[0])
    xp, xs, op, os_ = _odd_layer(xp, xs, mp, ms, page_table, state_c_wkv[0], state_c_shift[0], cache_d_cmp[0],
                                 cache_d_slc[0], cache_d_win[0], rel_bias, router, experts(1), g_norm_mix[1],
                                 g_norm_ffn[1], o_w_in[0], o_w_out[0], cpar, c_r_k[0].reshape(-1), c_ln_w[0],
                                 c_ln_b[0], d_q_norm[0], d_k_norm[0], d_cmp_pe[0], d_cmp_w[0])
    stack = lambda ts: tuple(a[None] for a in ts)
    return (xp, xs.reshape(Bs, 1, D)) + stack(ep) + stack(op) + stack(es) + stack(os_)


def kernel(x_prompt, x_sample, c_prompt, c_sample, page_table, cache_a_kv, cache_a_kidx, state_b_conv, state_c_wkv, state_c_shift, cache_d_cmp, cache_d_slc, cache_d_win, rel_bias, w_router, b_router, w_ada, b_ada, g_norm_mix, g_norm_ffn, w_expert_gate, w_expert_up, w_expert_down, e_w_in, e_w_out, a_q_norm, a_k_norm, b_conv_w, b_conv_b, o_w_in, o_w_out, c_mu, c_w0, c_w_up, c_a0, c_a_up, c_g_up, c_k_k, c_k_a, c_r_k, c_ln_w, c_ln_b, d_q_norm, d_k_norm, d_cmp_pe, d_cmp_w):
    return _forward(x_prompt, x_sample, c_prompt, c_sample, page_table, cache_a_kv, cache_a_kidx, state_b_conv,
                    state_c_wkv, state_c_shift, cache_d_cmp, cache_d_slc, cache_d_win, rel_bias, w_router, b_router,
                    w_ada, b_ada, g_norm_mix, g_norm_ffn, w_expert_gate, w_expert_up, w_expert_down, e_w_in, e_w_out,
                    a_q_norm, a_k_norm, b_conv_w, b_conv_b, o_w_in, o_w_out, c_mu, c_w0, c_w_up, c_a0, c_a_up,
                    c_g_up, c_k_k, c_k_a, c_r_k, c_ln_w, c_ln_b, d_q_norm, d_k_norm, d_cmp_pe, d_cmp_w)
```

```python
import functools
import math

import jax
import jax.numpy as jnp
from jax import lax
from jax.experimental import pallas as pl
from jax.experimental.pallas import tpu as pltpu

F32 = jnp.float32
BF16 = jnp.bfloat16
I32 = jnp.int32

LANE = 128
HEAD_DIM = 64
N_ATT_HEADS = 8
N_KV_HEADS = 2
GQA = N_ATT_HEADS // N_KV_HEADS
IDX_HEADS = 4
IDX_DIM = 64
TOPK_MAX = 256
CONV_CH = 512
RWKV_HEADS = 8
RWKV_DIM = RWKV_HEADS * HEAD_DIM
LORA_W = 64
LORA_A = 64
LORA_G = 128
GN_EPS = 64e-5
CMP_BLOCK = 32
SEL_BLOCK = 64
N_SEL_BLOCKS = 8
WINDOW = 512
N_BUCKETS = 32
MAX_DISTANCE = 128
N_EXPERTS = 16
N_GROUPS = 4
EXPERTS_PER_GROUP = N_EXPERTS // N_GROUPS
PAGE_SIZE = 128
RMS_EPS = 1e-6
NEG = -1e30
FORCE = 1e9
TAKEN = -3e38
ATT_SCALE = HEAD_DIM ** -0.5
VMEM_LIMIT = 56 * 1024 * 1024
ROW_TILE = 512
MOE_TILE = 1024
SCAN_TIME_CHUNK = 64
SAMPLE_GROUP = 8


def _cparams(*sem):
    return pltpu.CompilerParams(dimension_semantics=sem, vmem_limit_bytes=VMEM_LIMIT)


def _pick_tile(rows, pref):
    t = min(pref, rows)
    while rows % t or (t % 8 and t != rows):
        t -= 1
    return t


def _pick_group(n, pref):
    g = min(pref, n)
    while n % g:
        g -= 1
    return g


def _const_spec(a):
    nd = a.ndim
    return pl.BlockSpec(a.shape, lambda *_: (0,) * nd)


def _dot(a, b):
    return jnp.dot(a.astype(BF16), b.astype(BF16), preferred_element_type=F32)


def _dot_nt(a, b):
    return lax.dot_general(a.astype(BF16), b.astype(BF16), (((1,), (1,)), ((), ())),
                           preferred_element_type=F32)


def _dot_split(x, m):
    hi = x.astype(BF16)
    r1 = x - hi.astype(F32)
    mid = r1.astype(BF16)
    lo = (r1 - mid.astype(F32)).astype(BF16)
    return (jnp.dot(hi, m, preferred_element_type=F32) + jnp.dot(mid, m, preferred_element_type=F32)
            + jnp.dot(lo, m, preferred_element_type=F32))


def _bf16_round(x):
    return x.astype(BF16).astype(F32)


def _sigmoid(x):
    return 1.0 / (1.0 + jnp.exp(-x))


def _silu(x):
    return x * _sigmoid(x)


def _modulate(x, g, shift, scale):
    y = x * lax.rsqrt(jnp.mean(x * x, axis=-1, keepdims=True) + RMS_EPS)
    return (y * g) * (1.0 + scale) + shift


def _group_rms(t, gmat, gain):
    ms = _dot_split(t * t, gmat)
    return (t * lax.rsqrt(ms + RMS_EPS)) * gain


def _ada_kernel(c_ref, w_ref, b_ref, o_ref):
    o_ref[...] = _dot(_silu(c_ref[...]), w_ref[...]) + b_ref[...]


def _ada(c, w_all, b, layer):
    r, d = c.shape
    n = w_all.shape[2]
    tn = 512
    return pl.pallas_call(
        _ada_kernel,
        grid=(n // tn,),
        in_specs=[pl.BlockSpec((r, d), lambda j: (0, 0)),
                  pl.BlockSpec((None, d, tn), lambda j: (layer, 0, j)),
                  pl.BlockSpec((1, tn), lambda j: (0, j))],
        out_specs=pl.BlockSpec((r, tn), lambda j: (0, j)),
        out_shape=jax.ShapeDtypeStruct((r, n), F32),
        compiler_params=_cparams("parallel"),
        name="ada_mod",
    )(c, w_all, b.reshape(1, n))


E_Q0, E_KV0, E_QI0, E_MISC0, E_BG0, E_CG0, E_XIN0, E_END = 0, 1024, 1280, 1792, 1920, 2432, 2944, 3456


def _inproj_even_kernel(x_ref, shift_ref, scale_ref, g_ref, w_ref, qg_ref, kg_ref, gone_ref, gtwo_ref,
                        q_ref, kv_ref, kvb_ref, qi_ref, misc_ref, miscb_ref, bg_ref, u_ref, kvt_ref, misct_ref):
    h = _modulate(x_ref[...], g_ref[...], shift_ref[...], scale_ref[...])
    z = jnp.dot(h.astype(BF16), w_ref[...], preferred_element_type=F32)
    gone = gone_ref[...]
    for t in range(N_ATT_HEADS):
        sl = slice(t * LANE, (t + 1) * LANE)
        q_ref[:, sl] = _group_rms(z[:, E_Q0 + t * LANE:E_Q0 + (t + 1) * LANE], gone, qg_ref[:, sl]).astype(BF16)
    k = _group_rms(z[:, E_KV0:E_KV0 + LANE], gtwo_ref[...], kg_ref[...])
    v = z[:, E_KV0 + LANE:E_KV0 + 2 * LANE]
    kv_ref[:, 0:LANE] = k
    kv_ref[:, LANE:2 * LANE] = v
    kvb_ref[:, 0:LANE] = k.astype(BF16)
    kvb_ref[:, LANE:2 * LANE] = v.astype(BF16)
    qi_ref[...] = z[:, E_QI0:E_MISC0].astype(BF16)
    misc = z[:, E_MISC0:E_BG0]
    misc_ref[...] = misc
    miscb_ref[...] = misc.astype(BF16)
    bg_ref[...] = z[:, E_BG0:E_CG0]
    u_ref[...] = z[:, E_CG0:E_XIN0] * z[:, E_XIN0:E_END]
    kvt_ref[0:LANE, :] = k.T
    kvt_ref[LANE:2 * LANE, :] = v.T
    misct_ref[...] = misc.T


HALO = 8


def _shifted(z, halo, k, tpb):
    first = (pl.program_id(0) % tpb) == 0
    row = lax.broadcasted_iota(I32, z.shape, 0)
    out = pltpu.roll(z, k, axis=0)
    for j in range(k):
        prev = jnp.where(first, 0.0, halo[HALO - k + j:HALO - k + j + 1, :])
        out = jnp.where(row == j, prev, out)
    return out


def _row_call(kernel, xs, mods, consts, outs, tile, tpb, name, outs_t=(), outs_bt=(), halos=()):
    if not isinstance(xs, (list, tuple)):
        xs = [xs]
    rows = xs[0].shape[0]
    n_tiles = rows // tile
    in_specs = [pl.BlockSpec((tile, x.shape[1]), lambda t: (t, 0)) for x in xs]
    in_specs += [pl.BlockSpec((HALO, xs[i].shape[1]), lambda t: (jnp.maximum(t * (tile // HALO) - 1, 0), 0))
                 for i in halos]
    xs = list(xs) + [xs[i] for i in halos]
    for m in mods:
        in_specs.append(pl.BlockSpec((None,) + m.shape[1:], lambda t: (t // tpb, 0, 0)))
    in_specs += [_const_spec(c) for c in consts]
    out_specs = [pl.BlockSpec((tile, w), lambda t: (t, 0)) for (w, _) in outs]
    out_shape = [jax.ShapeDtypeStruct((rows, w), dt) for (w, dt) in outs]
    out_specs += [pl.BlockSpec((hh, tile), lambda t: (0, t)) for (hh, _) in outs_t]
    out_shape += [jax.ShapeDtypeStruct((hh, rows), dt) for (hh, dt) in outs_t]
    out_specs += [pl.BlockSpec((None, hh, tile), lambda t: (t // tpb, 0, t % tpb)) for (hh, _) in outs_bt]
    out_shape += [jax.ShapeDtypeStruct((n_tiles // tpb, hh, tile * tpb), dt) for (hh, dt) in outs_bt]
    return pl.pallas_call(kernel, grid=(n_tiles,), in_specs=in_specs, out_specs=out_specs, out_shape=out_shape,
                          compiler_params=_cparams("parallel"), name=name)(*xs, *mods, *consts)


def _pad_q_cols(wq):
    d = wq.shape[0]
    w = wq.reshape(d, N_ATT_HEADS, HEAD_DIM)
    z = jnp.zeros_like(w)
    lo = jnp.concatenate([w, z], -1)
    hi = jnp.concatenate([z, w], -1)
    sel = (jnp.arange(N_ATT_HEADS) >= GQA)[None, :, None]
    return jnp.where(sel, hi, lo).reshape(d, N_ATT_HEADS * LANE)


def _pad_o_rows(wo):
    d = wo.shape[1]
    w = wo.reshape(N_ATT_HEADS, HEAD_DIM, d)
    z = jnp.zeros_like(w)
    lo = jnp.concatenate([w, z], 1)
    hi = jnp.concatenate([z, w], 1)
    sel = (jnp.arange(N_ATT_HEADS) >= GQA)[:, None, None]
    return jnp.where(sel, hi, lo).reshape(N_ATT_HEADS * LANE, d)


def _gmats():
    i = jnp.arange(LANE)
    gone = jnp.full((LANE, LANE), 1.0 / HEAD_DIM, F32).astype(BF16)
    gtwo = jnp.where((i[:, None] // HEAD_DIM) == (i[None, :] // HEAD_DIM), 1.0 / HEAD_DIM, 0.0).astype(BF16)
    tri = jnp.where(i[:, None] <= i[None, :], 1.0, 0.0).astype(BF16)
    return gone, gtwo, tri


def _even_weights(w_in, w_out, q_norm, k_norm):
    d = w_in.shape[0]
    a_q, a_kv = N_ATT_HEADS * HEAD_DIM, 2 * N_KV_HEADS * HEAD_DIM
    o = 0
    wq = w_in[:, o:o + a_q]; o += a_q
    wkv = w_in[:, o:o + a_kv]; o += a_kv
    wqi = w_in[:, o:o + IDX_HEADS * IDX_DIM]; o += IDX_HEADS * IDX_DIM
    wki = w_in[:, o:o + IDX_DIM]; o += IDX_DIM
    wwi = w_in[:, o:o + IDX_HEADS]; o += IDX_HEADS
    wrest = w_in[:, o:]
    wqi = jnp.concatenate([wqi.reshape(d, IDX_HEADS, IDX_DIM), jnp.zeros((d, IDX_HEADS, LANE - IDX_DIM), F32)],
                          -1).reshape(d, IDX_HEADS * LANE)
    wmisc = jnp.concatenate([wki, wwi, jnp.zeros((d, LANE - IDX_DIM - IDX_HEADS), F32)], -1)
    w_in_p = jnp.concatenate([_pad_q_cols(wq), wkv, wqi, wmisc, wrest], -1).astype(BF16)
    w_out_p = jnp.concatenate([_pad_o_rows(w_out[:a_q]), w_out[a_q:]], 0).astype(BF16)
    qg = jnp.tile(q_norm, 2 * N_ATT_HEADS).reshape(1, N_ATT_HEADS * LANE)
    kg = jnp.tile(k_norm, 2).reshape(1, LANE)
    return w_in_p, w_out_p, qg, kg


def _inproj_even(x, mods, g, wts, gm, tile, tpb):
    w_in_p, _, qg, kg = wts
    gone, gtwo, _ = gm
    outs = [(N_ATT_HEADS * LANE, BF16), (2 * LANE, F32), (2 * LANE, BF16), (IDX_HEADS * LANE, BF16),
            (LANE, F32), (LANE, BF16), (CONV_CH, F32), (CONV_CH, F32)]
    return _row_call(_inproj_even_kernel, x, mods, [g, w_in_p, qg, kg, gone, gtwo], outs, tile, tpb, "inproj_even",
                     outs_bt=[(2 * LANE, F32), (LANE, F32)])


def _t5_bucket(dist):
    dist = jnp.maximum(dist, 0)
    exact = N_BUCKETS // 2
    far = exact + (jnp.log(jnp.maximum(dist, 1).astype(F32) / exact)
                   / math.log(MAX_DISTANCE / exact) * (N_BUCKETS - exact)).astype(I32)
    return jnp.where(dist < exact, dist, jnp.minimum(far, N_BUCKETS - 1))


def _bias_lookup(rel_bias, dist):
    onehot = (_t5_bucket(dist)[..., None] == jnp.arange(N_BUCKETS)).astype(F32)
    return jnp.einsum("...k,kh->...h", onehot, rel_bias, precision=lax.Precision.HIGHEST)


def _bias_row(rel_bias, t_pos):
    return _bias_lookup(rel_bias, t_pos - jnp.arange(t_pos + LANE)).T


def _bias_tiles(rel_bias, qb):
    r = jnp.arange(qb)[:, None]
    c = jnp.arange(LANE)[None, :]
    tiles = [_bias_lookup(rel_bias, d * LANE + r - c) for d in range(3)]
    return jnp.stack(tiles).transpose(0, 3, 1, 2)


def _stack_heads(q_ref, hk):
    return jnp.concatenate([q_ref[:, (hk * GQA + g) * LANE:(hk * GQA + g + 1) * LANE] for g in range(GQA)], axis=0)


QB = LANE
QW = GQA * QB


def _sub_sum(x):
    return jnp.sum(x, axis=0, keepdims=True)


def _flash_t_pair(blocks, qs, bias_ref, carry, acc_ref):
    logits = [[jnp.where(mk[hk], _dot_nt(kb, qs[hk]) + bias_ref[dsel, hk], NEG) for (kb, _, mk, dsel) in blocks]
              for hk in range(N_KV_HEADS)]
    out = []
    for hk in range(N_KV_HEADS):
        m, l = carry[hk]
        m_new = m
        for s in logits[hk]:
            m_new = jnp.maximum(m_new, jnp.max(s, axis=0, keepdims=True))
        alpha = jnp.exp(m - m_new)
        l = alpha * l
        pv = None
        for s, (_, vt, _, _) in zip(logits[hk], blocks):
            p = jnp.exp(s - m_new)
            l = l + _sub_sum(p)
            d = jnp.dot(vt, p.astype(BF16), preferred_element_type=F32)
            pv = d if pv is None else pv + d
        acc_ref[hk] = alpha * acc_ref[hk] + pv
        out.append((m_new, l))
    return tuple(out)


def _scaled_queries(q_ref, hk):
    return (_stack_heads(q_ref, hk).astype(F32) * ATT_SCALE).astype(BF16)


def _pair_loop(nblk, body, init):
    def body2(jj, c):
        return body(2 * jj + 1, body(2 * jj, c))
    return lax.fori_loop(0, (nblk + 1) // 2, body2, init)


def _flash_t_init():
    return (jnp.full((1, QW), NEG, F32), jnp.zeros((1, QW), F32))


def _tile_lanes(x):
    return jnp.concatenate([x] * GQA, axis=1)


def _write_heads_t(o_ref, o_ts):
    lane = lax.broadcasted_iota(I32, (QB, LANE), 1)
    for hk in range(N_KV_HEADS):
        valid = (lane // HEAD_DIM) == hk
        for g in range(GQA):
            h = hk * GQA + g
            o = o_ts[hk][:, g * QB:(g + 1) * QB].T
            o_ref[:, h * LANE:(h + 1) * LANE] = jnp.where(valid, o, 0.0).astype(BF16)


def _dsa_t_kernel(q_ref, qi_ref, misc_ref, kidx_ref, k_ref, vt_ref, bias_ref, trit_ref, o_ref, key_s, acc_s,
                  *, n_keep):
    i = pl.program_id(1)
    q0 = i * QB
    nblk = i + 1
    krow = lax.broadcasted_iota(I32, (LANE, QB), 0)
    qcol = lax.broadcasted_iota(I32, (LANE, QB), 1)
    misc_t = misc_ref[...].T
    wis = [_bf16_round(misc_t[IDX_DIM + h:IDX_DIM + h + 1, :]) for h in range(IDX_HEADS)]
    qi = jnp.concatenate([qi_ref[:, h * LANE:(h + 1) * LANE] for h in range(IDX_HEADS)], axis=0)
    idx_scale = (IDX_HEADS * IDX_DIM) ** -0.5

    def causal(j):
        return j * LANE + krow <= q0 + qcol

    def pass_a(j, c):
        kb = kidx_ref[pl.ds(pl.multiple_of(j * LANE, LANE), LANE), :]
        rel = _bf16_round(jnp.maximum(_dot_nt(kb, qi), 0.0))
        acc = jnp.zeros((LANE, QB), F32)
        for h in range(IDX_HEADS):
            acc = acc + rel[:, h * QB:(h + 1) * QB] * wis[h]
        key_s[j] = _order_keys(jnp.where(causal(j), acc * idx_scale, NEG))
        return c

    _pair_loop(nblk, pass_a, 0)

    def count(pred):
        def body(j, a):
            return a + jnp.where(pred(key_s[j]), 1.0, 0.0)
        return _sub_sum(_pair_loop(nblk, body, jnp.zeros((LANE, QB), F32)))

    keep = float(n_keep)
    thr = jnp.where(count(lambda k: k >= 0) >= keep, jnp.int32(0), jnp.int32(-2 ** 31))

    def search(it, thr):
        cand = thr | lax.shift_left(jnp.int32(1), jnp.int32(30) - it)
        return jnp.where(count(lambda k: k >= cand) >= keep, cand, thr)

    thr = lax.fori_loop(0, 31, search, thr)
    need = keep - count(lambda k: k > thr)
    trit = trit_ref[...]

    def pass_c(j, run):
        key = key_s[j]
        eq = key == thr
        eqf = jnp.where(eq, 1.0, 0.0)
        cum = jnp.dot(trit, eqf.astype(BF16), preferred_element_type=F32) + run
        sel = ((key > thr) | (eq & (cum <= need))) & causal(j)
        key_s[j] = jnp.where(sel, 1, 0)
        return run + _sub_sum(eqf)

    _pair_loop(nblk, pass_c, jnp.zeros((1, QB), F32))

    qs = [_scaled_queries(q_ref, hk) for hk in range(N_KV_HEADS)]
    acc_s[...] = jnp.zeros_like(acc_s)

    def pass_d(jj, carry):
        blocks = []
        for j in (2 * jj, 2 * jj + 1):
            kb = k_ref[pl.ds(pl.multiple_of(j * LANE, LANE), LANE), :]
            blocks.append((kb, vt_ref[j], [_tile_lanes(key_s[j] > 0)] * N_KV_HEADS, jnp.clip(i - j, 0, 2)))
        return _flash_t_pair(blocks, qs, bias_ref, carry, acc_s)

    res = lax.fori_loop(0, (nblk + 1) // 2, pass_d, tuple(_flash_t_init() for _ in range(N_KV_HEADS)))
    _write_heads_t(o_ref, [acc_s[hk] / res[hk][1] for hk in range(N_KV_HEADS)])


def _bias_tiles_t(rel_bias):
    t = _bias_tiles(rel_bias, QB)
    t = t.reshape(3, N_KV_HEADS, GQA, QB, LANE).transpose(0, 1, 4, 2, 3)
    return t.reshape(3, N_KV_HEADS, LANE, QW)


def _blocks_t(x):
    b, s, w = x.shape
    return x.reshape(b, s // LANE, LANE, w).transpose(0, 1, 3, 2)


def _dsa_t(q, qi, misc, kidx_b, kv_b, bias_t, trit, *, n_keep):
    b, s, _ = q.shape
    vt = _blocks_t(kv_b[:, :, LANE:])
    qspec = lambda w: pl.BlockSpec((None, QB, w), lambda bi, i: (bi, i, 0))
    kspec = pl.BlockSpec((None, s, LANE), lambda bi, i: (bi, 0, 0))
    return pl.pallas_call(
        functools.partial(_dsa_t_kernel, n_keep=n_keep),
        grid=(b, s // QB),
        in_specs=[qspec(N_ATT_HEADS * LANE), qspec(IDX_HEADS * LANE), qspec(LANE), kspec, kspec,
                  pl.BlockSpec((None, s // LANE, LANE, LANE), lambda bi, i: (bi, 0, 0, 0)),
                  _const_spec(bias_t), _const_spec(trit)],
        out_specs=qspec(N_ATT_HEADS * LANE),
        out_shape=jax.ShapeDtypeStruct((b, s, N_ATT_HEADS * LANE), BF16),
        scratch_shapes=[pltpu.VMEM((s // LANE, LANE, QB), I32), pltpu.VMEM((N_KV_HEADS, LANE, QW), F32)],
        compiler_params=_cparams("parallel", "parallel"),
        name="dsa_attention_t",
    )(q, qi, misc, kidx_b, kv_b, vt, bias_t, trit)


def _dot_split_rhs(m, x):
    hi = x.astype(BF16)
    r1 = x - hi.astype(F32)
    mid = r1.astype(BF16)
    lo = (r1 - mid.astype(F32)).astype(BF16)
    return (jnp.dot(m, hi, preferred_element_type=F32) + jnp.dot(m, mid, preferred_element_type=F32)
            + jnp.dot(m, lo, preferred_element_type=F32))


def _nsa_t_kernel(q_ref, gates_ref, kc_ref, vct_ref, ks_ref, vst_ref, kw_ref, vwt_ref, bias_ref, biasc_ref,
                  pairt_ref, o_ref, acc_s, *, n_cmp, n_sel):
    i = pl.program_id(1)
    q0 = i * QB
    nblk = i + 1
    heads = range(N_KV_HEADS)
    krow = lax.broadcasted_iota(I32, (LANE, QB), 0)
    qcol = lax.broadcasted_iota(I32, (LANE, QB), 1)
    t_pos = q0 + qcol
    krow_f = krow.astype(F32)
    qs = [_scaled_queries(q_ref, hk) for hk in heads]
    gates_t = gates_ref[...].T

    def gate_row(br, hk):
        return jnp.concatenate([gates_t[br * N_ATT_HEADS + hk * GQA + g:br * N_ATT_HEADS + hk * GQA + g + 1, :]
                                for g in range(GQA)], axis=1)

    mask_c = _tile_lanes(((krow * CMP_BLOCK + CMP_BLOCK - 1) <= t_pos) & (krow < n_cmp))
    cur = t_pos // SEL_BLOCK
    forced = (krow == 0) | (krow == cur) | (krow == cur - 1)
    sel_causal = krow * SEL_BLOCK <= t_pos
    o_cmp, picked = [], []
    for hk in heads:
        s = jnp.where(mask_c, _dot_nt(kc_ref[...], qs[hk]) + biasc_ref[hk], NEG)
        e = jnp.exp(s - jnp.max(s, axis=0, keepdims=True))
        p = jnp.where(mask_c, e / _sub_sum(e), 0.0)
        o_cmp.append(jnp.dot(vct_ref[...], p.astype(BF16), preferred_element_type=F32))
        ps = p[:, 0:QB]
        for g in range(1, GQA):
            ps = ps + p[:, g * QB:(g + 1) * QB]
        score = _dot_split_rhs(pairt_ref[...], ps)
        score = jnp.where(sel_causal, jnp.where(forced, FORCE, score), NEG)
        pk = jnp.zeros((LANE, QB), F32)
        for _ in range(n_sel):
            mx = jnp.max(score, axis=0, keepdims=True)
            first = jnp.min(jnp.where(score == mx, krow_f, float(LANE)), axis=0, keepdims=True)
            hit = krow_f == first
            pk = jnp.where(hit, 1.0, pk)
            score = jnp.where(hit, TAKEN, score)
        picked.append(pk.astype(BF16))

    def key_block(k_ref, vt_ref, j):
        return k_ref[pl.ds(pl.multiple_of(j * LANE, LANE), LANE), :], vt_ref[j]

    erow = lax.broadcasted_iota(I32, (LANE, LANE), 0)
    ecol = lax.broadcasted_iota(I32, (LANE, LANE), 1)
    acc_s[...] = jnp.zeros_like(acc_s)

    def slc_body(jj, carry):
        blocks = []
        for j in (2 * jj, 2 * jj + 1):
            kb, vt = key_block(ks_ref, vst_ref, j)
            expand = jnp.where(ecol == 2 * j + erow // SEL_BLOCK, 1.0, 0.0).astype(BF16)
            causal = j * LANE + krow <= t_pos
            masks = [_tile_lanes((jnp.dot(expand, picked[hk], preferred_element_type=F32) > 0.5) & causal)
                     for hk in heads]
            blocks.append((kb, vt, masks, jnp.clip(i - j, 0, 2)))
        return _flash_t_pair(blocks, qs, bias_ref, carry, acc_s.at[0])

    res_s = lax.fori_loop(0, (nblk + 1) // 2, slc_body, tuple(_flash_t_init() for _ in heads))

    lo = jnp.maximum(i - WINDOW // LANE - 1, 0)

    def win_body(jj, carry):
        blocks = []
        for j in (lo + 2 * jj, lo + 2 * jj + 1):
            kb, vt = key_block(kw_ref, vwt_ref, j)
            dist = t_pos - (j * LANE + krow)
            mask = _tile_lanes((dist >= 0) & (dist < WINDOW))
            blocks.append((kb, vt, [mask] * N_KV_HEADS, jnp.clip(i - j, 0, 2)))
        return _flash_t_pair(blocks, qs, bias_ref, carry, acc_s.at[1])

    res_w = lax.fori_loop(0, (i - lo + 2) // 2, win_body, tuple(_flash_t_init() for _ in heads))

    _write_heads_t(o_ref, [gate_row(0, hk) * o_cmp[hk] + gate_row(1, hk) * (acc_s[0, hk] / res_s[hk][1])
                           + gate_row(2, hk) * (acc_s[1, hk] / res_w[hk][1]) for hk in heads])


def _bias_cmp_t(rel_bias, s):
    t = _bias_cmp(rel_bias, [j * QB for j in range(s // QB)], QB)
    t = t.reshape(s // QB, N_KV_HEADS, GQA, QB, LANE).transpose(0, 1, 4, 2, 3)
    return t.reshape(s // QB, N_KV_HEADS, LANE, QW)


def _nsa_t(q, gates, kcv, kvs_b, kvw_b, bias_t, bias_c, pair_t, *, n_cmp, n_sel):
    b, s, _ = q.shape
    vct = kcv[:, :, LANE:].transpose(0, 2, 1)
    qspec = lambda w: pl.BlockSpec((None, QB, w), lambda bi, i: (bi, i, 0))
    kspec = pl.BlockSpec((None, s, LANE), lambda bi, i: (bi, 0, 0))
    vspec = pl.BlockSpec((None, s // LANE, LANE, LANE), lambda bi, i: (bi, 0, 0, 0))
    cspec = pl.BlockSpec((None, LANE, LANE), lambda bi, i: (bi, 0, 0))
    return pl.pallas_call(
        functools.partial(_nsa_t_kernel, n_cmp=n_cmp, n_sel=n_sel),
        grid=(b, s // QB),
        in_specs=[qspec(N_ATT_HEADS * LANE), qspec(LANE), cspec, cspec, kspec, vspec, kspec, vspec,
                  _const_spec(bias_t), pl.BlockSpec((None,) + bias_c.shape[1:], lambda bi, i: (i, 0, 0, 0)),
                  _const_spec(pair_t)],
        out_specs=qspec(N_ATT_HEADS * LANE),
        out_shape=jax.ShapeDtypeStruct((b, s, N_ATT_HEADS * LANE), BF16),
        scratch_shapes=[pltpu.VMEM((2, N_KV_HEADS, LANE, QW), F32)],
        compiler_params=_cparams("parallel", "parallel"),
        name="nsa_attention_t",
    )(q, gates, kcv, vct, kvs_b, _blocks_t(kvs_b[:, :, LANE:]), kvw_b, _blocks_t(kvw_b[:, :, LANE:]),
      bias_t, bias_c, pair_t)


def _select_top(keys, n_keep, tri):
    keep = float(n_keep)
    n = keys[0].shape[1]

    def count(pred):
        return [jnp.sum(jnp.where(pred(g, k), 1.0, 0.0), axis=1, keepdims=True) for g, k in enumerate(keys)]

    int_min = jnp.int32(-2 ** 31)
    thr = tuple(jnp.where(c >= keep, jnp.int32(0), int_min) for c in count(lambda g, k: k >= 0))

    def search(it, thr):
        bit = lax.shift_left(jnp.int32(1), jnp.int32(30) - it)
        cand = [t | bit for t in thr]
        cnt = count(lambda g, k: k >= cand[g])
        return tuple(jnp.where(c >= keep, cd, t) for c, cd, t in zip(cnt, cand, thr))

    thr = lax.fori_loop(0, 31, search, thr)
    need = [keep - c for c in count(lambda g, k: k > thr[g])]
    sels = []
    for g, k in enumerate(keys):
        run = jnp.zeros((1, 1), F32)
        parts = []
        for t in range(n // LANE):
            kt = k[:, t * LANE:(t + 1) * LANE]
            eq = kt == thr[g]
            eqf = jnp.where(eq, 1.0, 0.0)
            cum = jnp.dot(eqf.astype(BF16), tri, preferred_element_type=F32) + run
            parts.append((kt > thr[g]) | (eq & (cum <= need[g])))
            run = run + jnp.sum(eqf, axis=1, keepdims=True)
        sels.append(jnp.concatenate(parts, axis=1))
    return sels


def _order_keys(score):
    score = jnp.where(score == 0.0, 0.0, score)
    bits = lax.bitcast_convert_type(score, I32)
    return jnp.where(bits < 0, bits ^ jnp.int32(0x7FFFFFFF), bits)


def _head_rows_out(o_ref, g, acc):
    rowh = lax.broadcasted_iota(I32, (N_ATT_HEADS, LANE), 0)
    laneh = lax.broadcasted_iota(I32, (N_ATT_HEADS, LANE), 1)
    o_ref[g] = jnp.where((laneh // HEAD_DIM) == (rowh // GQA), acc, 0.0).astype(BF16)


def _dsa_sample_kernel(pt_ref, q_ref, qi_ref, wi_ref, knew_ref, kvnew_ref, bias_ref, tri_ref, *refs,
                       grp, n_pages, n_keep):
    del pt_ref
    ki_refs, kv_refs, o_ref = refs[:grp * n_pages], refs[grp * n_pages:2 * grp * n_pages], refs[-1]
    lane1 = lax.broadcasted_iota(I32, (1, LANE), 1)
    idx_scale = (IDX_HEADS * IDX_DIM) ** -0.5
    keys = []
    for g in range(grp):
        qi = qi_ref[g]
        wi = _bf16_round(wi_ref[g])
        tiles = []
        for p in range(n_pages):
            rel = _bf16_round(jnp.maximum(_dot(qi, ki_refs[g * n_pages + p][...]), 0.0))
            tiles.append(jnp.sum(rel * wi, axis=0, keepdims=True) * idx_scale)
        rel_new = _bf16_round(jnp.maximum(jnp.sum(qi.astype(F32) * _bf16_round(knew_ref[g]), axis=1, keepdims=True),
                                          0.0))
        sc_new = jnp.sum(rel_new * wi, axis=0, keepdims=True) * idx_scale
        tiles.append(jnp.where(lane1 == 0, sc_new, NEG))
        keys.append(_order_keys(jnp.concatenate(tiles, axis=1)))
    sels = _select_top(keys, n_keep, tri_ref[...])
    bias = bias_ref[...]
    for g in range(grp):
        q = q_ref[g]
        kvnew = _bf16_round(kvnew_ref[g])
        tiles = [_dot(q, kv_refs[g * n_pages + p][0:LANE, :]) for p in range(n_pages)]
        s_new = jnp.sum(q.astype(F32) * kvnew[:, 0:LANE], axis=1, keepdims=True)
        tiles.append(jnp.where(lane1 == 0, s_new, 0.0))
        valid = sels[g] & (jnp.concatenate([lane1] * n_pages + [lane1 + LANE], axis=1) <= LANE)
        s = jnp.where(valid, jnp.concatenate(tiles, axis=1) * ATT_SCALE + bias, NEG)
        e = jnp.exp(s - jnp.max(s, axis=1, keepdims=True))
        p_all = jnp.where(valid, e / jnp.sum(e, axis=1, keepdims=True), 0.0)
        acc = p_all[:, n_pages * LANE:n_pages * LANE + 1] * kvnew[:, LANE:2 * LANE]
        for p in range(n_pages):
            acc = acc + _dot_nt(p_all[:, p * LANE:(p + 1) * LANE], kv_refs[g * n_pages + p][LANE:2 * LANE, :])
        _head_rows_out(o_ref, g, acc)


def _page_specs(pool_t, n_pages, grp):
    r, c = pool_t.shape[1:]
    return [pl.BlockSpec((None, r, c), lambda i, pt, g=g, p=p: (pt[i * grp + g, p], 0, 0))
            for g in range(grp) for p in range(n_pages)]


def _dsa_sample(q8, qi8, wi8, knew, kvnew, bias, tri, ki_t, kv_t, page_table, *, n_keep, grp):
    b, n_pages = page_table.shape
    gspec = lambda a: pl.BlockSpec((grp,) + a.shape[1:], lambda i, pt: (i,) + (0,) * (a.ndim - 1))
    cspec = lambda a: pl.BlockSpec(a.shape, lambda i, pt: (0,) * a.ndim)
    kern = functools.partial(_dsa_sample_kernel, grp=grp, n_pages=n_pages, n_keep=n_keep)
    return pl.pallas_call(
        kern,
        grid_spec=pltpu.PrefetchScalarGridSpec(
            num_scalar_prefetch=1, grid=(b // grp,),
            in_specs=[gspec(q8), gspec(qi8), gspec(wi8), gspec(knew), gspec(kvnew), cspec(bias), cspec(tri)]
            + _page_specs(ki_t, n_pages, grp) + _page_specs(kv_t, n_pages, grp),
            out_specs=pl.BlockSpec((grp, N_ATT_HEADS, LANE), lambda i, pt: (i, 0, 0))),
        out_shape=jax.ShapeDtypeStruct((b, N_ATT_HEADS, LANE), BF16),
        compiler_params=_cparams("parallel"),
        name="dsa_sample",
    )(page_table, q8, qi8, wi8, knew, kvnew, bias, tri, *([ki_t] * (grp * n_pages)), *([kv_t] * (grp * n_pages)))


def _route(hf, wrt, br):
    logits = _dot_nt(wrt, hf)
    s = _sigmoid(logits)
    sel = s + br
    rows = [sel[e:e + 1, :] for e in range(N_EXPERTS)]
    grp = []
    for g in range(N_GROUPS):
        a = rows[g * EXPERTS_PER_GROUP:(g + 1) * EXPERTS_PER_GROUP]
        best = None
        for i in range(EXPERTS_PER_GROUP):
            for j in range(i + 1, EXPERTS_PER_GROUP):
                v = a[i] + a[j]
                best = v if best is None else jnp.maximum(best, v)
        grp.append(best)
    gbest = jnp.zeros_like(grp[0], dtype=I32)
    cur = grp[0]
    for g in range(1, N_GROUPS):
        better = grp[g] > cur
        gbest = jnp.where(better, g, gbest)
        cur = jnp.where(better, grp[g], cur)
    picked = []
    for g in range(N_GROUPS):
        a = rows[g * EXPERTS_PER_GROUP:(g + 1) * EXPERTS_PER_GROUP]
        for j in range(EXPERTS_PER_GROUP):
            rank = jnp.zeros_like(a[j])
            for jj in range(EXPERTS_PER_GROUP):
                if jj != j:
                    ahead = (a[jj] > a[j]) | (a[jj] == a[j]) if jj < j else (a[jj] > a[j])
                    rank = rank + jnp.where(ahead, 1.0, 0.0)
            e = g * EXPERTS_PER_GROUP + j
            picked.append(jnp.where((gbest == g) & (rank < 2.0), s[e:e + 1, :], 0.0))
    den = picked[0]
    for p in picked[1:]:
        den = den + p
    return jnp.concatenate([p / den for p in picked], axis=0)


def _post_tail(x, mix, gate, gf, shf, scf, wrt_ref, br_ref, x2_ref, hf_ref, cwt_ref):
    x2 = x + gate * mix
    x2_ref[...] = x2
    hf = _modulate(x2, gf, shf, scf)
    hf_ref[...] = hf.astype(BF16)
    cwt_ref[...] = _route(hf, wrt_ref[...], br_ref[...])


def _post_even_kernel(x_ref, oa_ref, bg_ref, u_ref, um1_ref, um2_ref, *rest):
    _post_even_body(x_ref, oa_ref, bg_ref, u_ref[...], um1_ref[...], um2_ref[...], *rest)


def _post_even_seq_kernel(x_ref, oa_ref, bg_ref, u_ref, uh_ref, *rest, tpb):
    u = u_ref[...]
    _post_even_body(x_ref, oa_ref, bg_ref, u, _shifted(u, uh_ref[...], 1, tpb), _shifted(u, uh_ref[...], 2, tpb),
                    *rest)


def _post_even_body(x_ref, oa_ref, bg_ref, u, um1, um2, gate_ref, shf_ref, scf_ref,
                    gf_ref, cw_ref, cb_ref, wo_ref, wrt_ref, br_ref, x2_ref, hf_ref, cwt_ref):
    cw = cw_ref[...]
    y = cb_ref[...] + cw[0:1, :] * um2
    y = y + cw[1:2, :] * um1
    y = y + cw[2:3, :] * u
    n_a = N_ATT_HEADS * LANE
    mix = (jnp.dot(oa_ref[...], wo_ref[0:n_a, :], preferred_element_type=F32)
           + jnp.dot((bg_ref[...] * y).astype(BF16), wo_ref[n_a:n_a + CONV_CH, :], preferred_element_type=F32))
    _post_tail(x_ref[...], mix, gate_ref[...], gf_ref[...], shf_ref[...], scf_ref[...], wrt_ref, br_ref,
               x2_ref, hf_ref, cwt_ref)


POST_OUTS = [(1024, F32), (1024, BF16)]


def _moe_kernel(hf_ref, cw_ref, x2_ref, gate_ref, wg_ref, wu_ref, wd_ref, o_ref, acc_ref):
    e = pl.program_id(1)

    @pl.when(e == 0)
    def _():
        acc_ref[...] = jnp.zeros_like(acc_ref)

    hf = hf_ref[...]
    hmid = _silu(_dot(hf, wg_ref[...])) * _dot(hf, wu_ref[...])
    cw = cw_ref[...]
    lane = lax.broadcasted_iota(I32, cw.shape, 1)
    wcol = jnp.sum(jnp.where(lane == e, cw, 0.0), axis=1, keepdims=True)
    acc_ref[...] += _dot(hmid, wd_ref[...]) * wcol

    @pl.when(e == N_EXPERTS - 1)
    def _():
        o_ref[...] = x2_ref[...] + gate_ref[...] * acc_ref[...]


def _moe(hf, cw, x2, gate, wg, wu, wd, layer, tile, tpb):
    rows, d = x2.shape
    de = wg.shape[3]
    return pl.pallas_call(
        _moe_kernel,
        grid=(rows // tile, N_EXPERTS),
        in_specs=[pl.BlockSpec((tile, d), lambda t, e: (t, 0)),
                  pl.BlockSpec((tile, N_EXPERTS), lambda t, e: (t, 0)),
                  pl.BlockSpec((tile, d), lambda t, e: (t, 0)),
                  pl.BlockSpec((None,) + gate.shape[1:], lambda t, e: (t // tpb, 0, 0)),
                  pl.BlockSpec((None, None, d, de), lambda t, e: (layer, e, 0, 0)),
                  pl.BlockSpec((None, None, d, de), lambda t, e: (layer, e, 0, 0)),
                  pl.BlockSpec((None, None, de, d), lambda t, e: (layer, e, 0, 0))],
        out_specs=pl.BlockSpec((tile, d), lambda t, e: (t, 0)),
        out_shape=jax.ShapeDtypeStruct((rows, d), F32),
        scratch_shapes=[pltpu.VMEM((tile, d), F32)],
        compiler_params=_cparams("parallel", "arbitrary"),
        name="moe_dense",
    )(hf, cw, x2, gate, wg, wu, wd)


MOE_WIN = LANE


def _moe_sorted_kernel(plan_ref, hf_ref, cw_ref, cwt_ref, x2_ref, gate_ref, tril_ref, g16_ref, g16t_ref,
                       wg_ref, wu_ref, wd_ref, o_ref, p_s, pt_s, xs_s, cws_s, ys_s):
    ti = pl.program_id(0)
    e = pl.program_id(1)
    t = hf_ref.shape[0]
    g = e // EXPERTS_PER_GROUP

    @pl.when(e == 0)
    def _():
        tril = tril_ref[...]
        memb_col = jnp.dot(jnp.where(cw_ref[...] > 0.0, 1.0, 0.0).astype(BF16), g16_ref[...],
                           preferred_element_type=F32) > 0.5
        memb_row = jnp.dot(g16t_ref[...], jnp.where(cwt_ref[...] > 0.0, 1.0, 0.0).astype(BF16),
                           preferred_element_type=F32) > 0.5
        mcf = jnp.where(memb_col, 1.0, 0.0)
        mrf = jnp.where(memb_row, 1.0, 0.0)
        rank_col = jnp.dot(tril, mcf.astype(BF16), preferred_element_type=F32)
        rank_row = _dot_nt(mrf, tril)
        lane = lax.broadcasted_iota(I32, (t, LANE), 1)
        row8 = lax.broadcasted_iota(I32, (8, t), 0)
        base_col = jnp.zeros((t, LANE), F32)
        base_row = jnp.zeros((8, t), F32)
        for gg in range(N_GROUPS):
            start = plan_ref[ti, gg].astype(F32)
            base_col = jnp.where(lane == gg, start, base_col)
            base_row = jnp.where(row8 == gg, start, base_row)
        slot_col = jnp.sum(mcf * (base_col + rank_col - 1.0), axis=1, keepdims=True)
        slot_row = jnp.sum(mrf * (base_row + rank_row - 1.0), axis=0, keepdims=True)
        col_iota = lax.broadcasted_iota(I32, (MOE_WIN, t), 1).astype(F32)
        row_iota = lax.broadcasted_iota(I32, (MOE_WIN, t), 0).astype(F32)
        for c in range(t // MOE_WIN):
            rows = slice(c * MOE_WIN, (c + 1) * MOE_WIN)
            p_s[rows, :] = jnp.where(slot_row == row_iota + float(c * MOE_WIN), 1.0, 0.0).astype(BF16)
            pt_s[rows, :] = jnp.where(slot_col[rows] == col_iota, 1.0, 0.0).astype(BF16)
        p = p_s[...]
        xs_s[...] = jnp.dot(p, hf_ref[...], preferred_element_type=F32).astype(BF16)
        cws_s[...] = _dot_split_rhs(p, cw_ref[...])
        ys_s[...] = jnp.zeros_like(ys_s)

    lane16 = lax.broadcasted_iota(I32, (MOE_WIN, N_EXPERTS), 1)

    def window(c, carry):
        rows = pl.ds(pl.multiple_of(c * MOE_WIN, MOE_WIN), MOE_WIN)
        xw = xs_s[rows, :]
        hmid = _silu(_dot(xw, wg_ref[...])) * _dot(xw, wu_ref[...])
        wcol = jnp.sum(jnp.where(lane16 == e, cws_s[rows, :], 0.0), axis=1, keepdims=True)
        ys_s[rows, :] = ys_s[rows, :] + _dot(hmid, wd_ref[...]) * wcol
        return carry

    lax.fori_loop(plan_ref[ti, N_GROUPS + g], plan_ref[ti, 2 * N_GROUPS + g], window, 0)

    @pl.when(e == N_EXPERTS - 1)
    def _():
        ys = ys_s[...]
        hi = ys.astype(BF16)
        lo = (ys - hi.astype(F32)).astype(BF16)
        pt = pt_s[...]
        back = jnp.dot(pt, hi, preferred_element_type=F32) + jnp.dot(pt, lo, preferred_element_type=F32)
        o_ref[...] = x2_ref[...] + gate_ref[...] * back


def _moe_plan(cwt, tile):
    n = cwt.shape[1]
    member = (cwt.reshape(N_GROUPS, EXPERTS_PER_GROUP, n // tile, tile) > 0.0).any(axis=1)
    cnt = member.sum(axis=-1).astype(I32).T
    start = jnp.cumsum(cnt, axis=1) - cnt
    lo = start // MOE_WIN
    hi = jnp.where(cnt > 0, (start + cnt + MOE_WIN - 1) // MOE_WIN, lo)
    return jnp.concatenate([start, lo, hi], axis=1)


def _moe_sorted(hf, cwt, x2, gate, wg, wu, wd, layer, tile, tpb):
    rows, d = x2.shape
    de = wg.shape[3]
    i = jnp.arange(tile)
    tril = jnp.where(i[None, :] <= i[:, None], 1.0, 0.0).astype(BF16)
    e16 = jnp.arange(N_EXPERTS)
    g16 = jnp.where(e16[:, None] // EXPERTS_PER_GROUP == jnp.arange(LANE)[None, :], 1.0, 0.0).astype(BF16)
    g16t = jnp.where(jnp.arange(8)[:, None] == e16[None, :] // EXPERTS_PER_GROUP, 1.0, 0.0).astype(BF16)
    cspec = lambda a: pl.BlockSpec(a.shape, lambda t, e, plan: (0,) * a.ndim)
    return pl.pallas_call(
        _moe_sorted_kernel,
        grid_spec=pltpu.PrefetchScalarGridSpec(
            num_scalar_prefetch=1, grid=(rows // tile, N_EXPERTS),
            in_specs=[pl.BlockSpec((tile, d), lambda t, e, plan: (t, 0)),
                      pl.BlockSpec((tile, N_EXPERTS), lambda t, e, plan: (t, 0)),
                      pl.BlockSpec((N_EXPERTS, tile), lambda t, e, plan: (0, t)),
                      pl.BlockSpec((tile, d), lambda t, e, plan: (t, 0)),
                      pl.BlockSpec((None,) + gate.shape[1:], lambda t, e, plan: (t // tpb, 0, 0)),
                      cspec(tril), cspec(g16), cspec(g16t),
                      pl.BlockSpec((None, None, d, de), lambda t, e, plan: (layer, e, 0, 0)),
                      pl.BlockSpec((None, None, d, de), lambda t, e, plan: (layer, e, 0, 0)),
                      pl.BlockSpec((None, None, de, d), lambda t, e, plan: (layer, e, 0, 0))],
            out_specs=pl.BlockSpec((tile, d), lambda t, e, plan: (t, 0)),
            scratch_shapes=[pltpu.VMEM((tile, tile), BF16), pltpu.VMEM((tile, tile), BF16),
                            pltpu.VMEM((tile, d), BF16), pltpu.VMEM((tile, N_EXPERTS), F32),
                            pltpu.VMEM((tile, d), F32)]),
        out_shape=jax.ShapeDtypeStruct((rows, d), F32),
        compiler_params=_cparams("parallel", "arbitrary"),
        name="moe_sorted",
    )(_moe_plan(cwt, tile), hf, cwt.T, cwt, x2, gate, tril, g16, g16t, wg, wu, wd)


def _cache_rows(kv_t):
    b, _, s = kv_t.shape
    return kv_t.reshape(b, 2, N_KV_HEADS, HEAD_DIM, s).transpose(0, 4, 1, 2, 3)


def _even_layer(xp, xs, mp, ms, page_table, kv_pool, kidx_pool, conv_buf, rel_bias, router, experts,
                g_mix, g_ffn, w_in, w_out, q_norm, k_norm, conv_w, conv_b):
    B, S, D = xp.shape
    Bs = xs.shape[0]
    past = page_table.shape[1] * PAGE_SIZE
    wts = _even_weights(w_in, w_out, q_norm, k_norm)
    gm = _gmats()
    wrt, br = router
    wg, wu, wd, lyr = experts
    g_mix = g_mix.reshape(1, D)
    g_ffn = g_ffn.reshape(1, D)
    post_consts = [g_ffn, conv_w, conv_b.reshape(1, CONV_CH), wts[1], wrt, br]

    tp = ROW_TILE
    q_p, _, kv_b, qi_p, misc, misc_b, bg, u, kv_t, misc_t = _inproj_even(xp.reshape(B * S, D), [mp[0], mp[1]],
                                                                        g_mix, wts, gm, tp, S // tp)
    r3 = lambda a: a.reshape(B, S, a.shape[-1])
    oa = _dsa_t(r3(q_p), r3(qi_p), r3(misc), r3(misc_b), r3(kv_b), _bias_tiles_t(rel_bias), gm[2].T,
                n_keep=min(TOPK_MAX, S // 4))
    u3 = r3(u)
    x2, hf, cwt = _row_call(functools.partial(_post_even_seq_kernel, tpb=S // tp),
                            [xp.reshape(B * S, D), oa.reshape(B * S, -1), bg, u],
                            [mp[2], mp[3], mp[4]], post_consts, POST_OUTS, tp, S // tp, "post_even",
                            outs_t=[(N_EXPERTS, F32)], halos=[3])
    tm = min(MOE_TILE, S)
    xp3 = _moe_sorted(hf, cwt, x2, mp[5], wg, wu, wd, lyr, tm, S // tm).reshape(B, S, D)
    outs_p = (_cache_rows(kv_t), misc_t[:, :IDX_DIM].transpose(0, 2, 1), u3[:, S - 2:])

    q_s, kv_fs, _, qi_s, misc_s, _, bg_s, u_s, _, _ = _inproj_even(xs, [ms[0], ms[1]], g_mix, wts, gm, Bs, 1)
    n_pool = kv_pool.shape[0]
    kv_t = kv_pool.transpose(0, 2, 3, 4, 1).reshape(n_pool, 2 * LANE, PAGE_SIZE)
    ki_t = kidx_pool.transpose(0, 2, 1)
    qi8 = jnp.pad(qi_s.reshape(Bs, IDX_HEADS, LANE)[:, :, :IDX_DIM], ((0, 0), (0, N_ATT_HEADS - IDX_HEADS), (0, 0)))
    wi8 = jnp.pad(misc_s[:, IDX_DIM:IDX_DIM + IDX_HEADS], ((0, 0), (0, N_ATT_HEADS - IDX_HEADS)))[:, :, None]
    oa_s = _dsa_sample(q_s.reshape(Bs, N_ATT_HEADS, LANE), qi8, wi8, misc_s[:, None, :IDX_DIM], kv_fs[:, None, :],
                       _bias_row(rel_bias, past), gm[2], ki_t, kv_t, page_table,
                       n_keep=min(TOPK_MAX, (past + 1) // 4), grp=_pick_group(Bs, SAMPLE_GROUP)).reshape(Bs, -1)
    x2s, hfs, cwts = _row_call(_post_even_kernel, [xs, oa_s, bg_s, u_s, conv_buf[:, 1], conv_buf[:, 0]],
                               [ms[2], ms[3], ms[4]], post_consts, POST_OUTS, Bs, 1, "post_even_s",
                               outs_t=[(N_EXPERTS, F32)])
    xs3 = _moe(hfs, cwts.T, x2s, ms[5], wg, wu, wd, lyr, Bs, 1)
    outs_s = (kv_fs.reshape(Bs, 1, 2, N_KV_HEADS, HEAD_DIM), misc_s[:, None, :IDX_DIM],
              jnp.concatenate([conv_buf[:, 1:], u_s[:, None, :]], axis=1))
    return xp3, xs3, outs_p, outs_s


O_Z0, O_Q0, O_KVC0, O_KVS0, O_KVW0, O_G0, O_END = 0, 1792, 2816, 3072, 3328, 3584, 3712
P_C = 3 * RWKV_DIM + LORA_W + LORA_A + LORA_G


def _inproj_odd_kernel(x_ref, shift_ref, scale_ref, g_ref, w_ref, qg_ref, ksg_ref, kwg_ref, gone_ref, gtwo_ref,
                       zc_ref, q_ref, kvc_ref, kvs_ref, kvsb_ref, kvw_ref, kvwb_ref, gates_ref,
                       kvct_ref, kvst_ref, kvwt_ref):
    h = _modulate(x_ref[...], g_ref[...], shift_ref[...], scale_ref[...])
    z = jnp.dot(h.astype(BF16), w_ref[...], preferred_element_type=F32)
    zc_ref[...] = z[:, O_Z0:O_Q0]
    gone = gone_ref[...]
    gtwo = gtwo_ref[...]
    for t in range(N_ATT_HEADS):
        sl = slice(t * LANE, (t + 1) * LANE)
        q_ref[:, sl] = _group_rms(z[:, O_Q0 + t * LANE:O_Q0 + (t + 1) * LANE], gone, qg_ref[:, sl]).astype(BF16)
    kvc_ref[...] = z[:, O_KVC0:O_KVS0]
    kvct_ref[0:LANE, :] = z[:, O_KVC0:O_KVC0 + LANE].T
    kvct_ref[LANE:2 * LANE, :] = z[:, O_KVC0 + LANE:O_KVS0].T
    for base, gain_ref, f_ref, b_ref, t_ref in ((O_KVS0, ksg_ref, kvs_ref, kvsb_ref, kvst_ref),
                                                (O_KVW0, kwg_ref, kvw_ref, kvwb_ref, kvwt_ref)):
        k = _group_rms(z[:, base:base + LANE], gtwo, gain_ref[...])
        v = z[:, base + LANE:base + 2 * LANE]
        t_ref[0:LANE, :] = k.T
        t_ref[LANE:2 * LANE, :] = v.T
        f_ref[:, 0:LANE] = k
        f_ref[:, LANE:2 * LANE] = v
        b_ref[:, 0:LANE] = k.astype(BF16)
        b_ref[:, LANE:2 * LANE] = v.astype(BF16)
    gates_ref[...] = _sigmoid(z[:, O_G0:O_END])


def _odd_weights(w_in, w_out, q_norm, k_norm):
    d = w_in.shape[0]
    a_q, a_kv = N_ATT_HEADS * HEAD_DIM, 2 * N_KV_HEADS * HEAD_DIM
    o = P_C
    wz = w_in[:, :o]
    wq = w_in[:, o:o + a_q]; o += a_q
    wkv = w_in[:, o:o + 3 * a_kv]; o += 3 * a_kv
    wg = w_in[:, o:]
    wg = jnp.concatenate([wg, jnp.zeros((d, LANE - wg.shape[1]), F32)], -1)
    w_in_p = jnp.concatenate([wz, _pad_q_cols(wq), wkv, wg], -1).astype(BF16)
    w_out_p = jnp.concatenate([w_out[:RWKV_DIM], _pad_o_rows(w_out[RWKV_DIM:])], 0).astype(BF16)
    qg = jnp.tile(q_norm, 2 * N_ATT_HEADS).reshape(1, N_ATT_HEADS * LANE)
    ksg = jnp.tile(k_norm[1], 2).reshape(1, LANE)
    kwg = jnp.tile(k_norm[2], 2).reshape(1, LANE)
    return w_in_p, w_out_p, qg, ksg, kwg


def _inproj_odd(x, mods, g, wts, gm, tile, tpb):
    w_in_p, _, qg, ksg, kwg = wts
    gone, gtwo, _ = gm
    outs = [(P_C, F32), (N_ATT_HEADS * LANE, BF16), (2 * LANE, F32), (2 * LANE, F32), (2 * LANE, BF16),
            (2 * LANE, F32), (2 * LANE, BF16), (LANE, F32)]
    return _row_call(_inproj_odd_kernel, x, mods, [g, w_in_p, qg, ksg, kwg, gone, gtwo], outs, tile, tpb,
                     "inproj_odd", outs_bt=[(2 * LANE, F32)] * 3)


def _rwkv_pre_kernel(z_ref, zp_ref, *rest):
    _rwkv_pre_body(z_ref[...], zp_ref[...], *rest)


def _rwkv_pre_seq_kernel(z_ref, zh_ref, *rest, tpb):
    z = z_ref[...]
    _rwkv_pre_body(z, _shifted(z, zh_ref[...], 1, tpb), *rest)


def _rwkv_pre_body(z, zp, mu_ref, w0_ref, a0_ref, kk_ref, ka_ref, wup_ref, aup_ref, gup_ref, gsum_ref,
                   r_o, w_o, k_o, v_o, kk_o, kka_o, g_o):
    zm = z + (zp - z) * mu_ref[...]
    r = zm[:, 0:RWKV_DIM]
    k = zm[:, RWKV_DIM:2 * RWKV_DIM]
    v = zm[:, 2 * RWKV_DIM:3 * RWKV_DIM]
    t12 = zm[:, 3 * RWKV_DIM:3 * RWKV_DIM + LANE]
    gd = zm[:, 3 * RWKV_DIM + LANE:P_C]
    xw = w0_ref[...] + _dot(jnp.tanh(t12), wup_ref[...])
    sp = jnp.maximum(-xw, 0.0) + jnp.log(1.0 + jnp.exp(-jnp.abs(xw)))
    w_o[...] = jnp.exp(-jnp.exp(-sp - 0.5))
    a = _sigmoid(a0_ref[...] + _dot(t12, aup_ref[...]))
    g_o[...] = _dot(_sigmoid(gd), gup_ref[...])
    kk = k * kk_ref[...]
    gsum = gsum_ref[...]
    for t in range(RWKV_DIM // LANE):
        sl = slice(t * LANE, (t + 1) * LANE)
        kt = kk[:, sl]
        nrm = jnp.maximum(jnp.sqrt(_dot_split(kt * kt, gsum)), 1e-12)
        kn = kt / nrm
        kk_o[:, sl] = kn
        kka_o[:, sl] = kn * a[:, sl]
    r_o[...] = r
    v_o[...] = v
    k_o[...] = k * (1.0 + (a - 1.0) * ka_ref[...])


def _rwkv_pre(zc, zprev, cpar, gsum, tile, tpb=None):
    mu, w0, w_up, a0, a_up, g_up, k_k, k_a = cpar
    z64 = jnp.zeros((LORA_W, RWKV_DIM), F32)
    consts = [mu.reshape(1, P_C), w0.reshape(1, -1), a0.reshape(1, -1), k_k.reshape(1, -1), k_a.reshape(1, -1),
              jnp.concatenate([w_up, z64], 0).astype(BF16), jnp.concatenate([z64, a_up], 0).astype(BF16),
              g_up.astype(BF16), gsum]
    outs = [(RWKV_DIM, F32)] * 7
    if zprev is None:
        return _row_call(functools.partial(_rwkv_pre_seq_kernel, tpb=tpb), [zc], [], consts, outs, tile, 1,
                         "rwkv_pre", halos=[0])
    return _row_call(_rwkv_pre_kernel, [zc, zprev], [], consts, outs, tile, 1, "rwkv_pre_s")


SCAN_P = 64
SCAN_VH = HEAD_DIM // 2
SCAN_SEQS = SCAN_P // RWKV_HEADS


def _scan_kernel(kk_ref, w_ref, kka_ref, k_ref, r_ref, v_ref, s0_ref, y_ref, so_ref, st, *, tc):
    ti = pl.program_id(1)

    @pl.when(ti == 0)
    def _():
        st[...] = s0_ref[...]

    lo_half = lax.broadcasted_iota(I32, (1, LANE), 1) < SCAN_P

    def lanes_of(ref, t):
        x = ref[:, t]
        xs = [x[:, h * HEAD_DIM:(h + 1) * HEAD_DIM] for h in range(RWKV_HEADS)]
        return jnp.concatenate(xs + xs, axis=0).T

    def step(t, c):
        kk, w, kka, kt, rt, vf = (lanes_of(ref, t) for ref in (kk_ref, w_ref, kka_ref, k_ref, r_ref, v_ref))
        ys = []
        for vi in range(SCAN_VH):
            s = st[vi]
            sa = -jnp.sum(s * kk, axis=0, keepdims=True)
            vrow = jnp.where(lo_half, vf[vi:vi + 1, :], vf[vi + SCAN_VH:vi + SCAN_VH + 1, :])
            sn = s * w + sa * kka + vrow * kt
            st[vi] = sn
            ys.append(jnp.sum(sn * rt, axis=0, keepdims=True))
        y_ref[t] = jnp.concatenate(ys, axis=0)
        return c

    lax.fori_loop(0, tc, step, 0)

    @pl.when(ti == pl.num_programs(1) - 1)
    def _():
        so_ref[...] = st[...]


def _scan_unlayout_y(y, b, t):
    nc = y.shape[0]
    a = jnp.concatenate([y[..., :SCAN_P], y[..., SCAN_P:]], axis=2)
    a = a.reshape(nc, t, HEAD_DIM, RWKV_HEADS, SCAN_SEQS).transpose(0, 4, 1, 3, 2)
    return a.reshape(nc * SCAN_SEQS, t, RWKV_DIM)[:b].reshape(b * t, RWKV_DIM)


def _scan_layout_state(s):
    b = s.shape[0]
    nc = -(-b // SCAN_SEQS)
    a = jnp.pad(s, ((0, nc * SCAN_SEQS - b), (0, 0), (0, 0), (0, 0)))
    a = a.reshape(nc, SCAN_SEQS, RWKV_HEADS, HEAD_DIM, HEAD_DIM).transpose(0, 3, 4, 2, 1)
    a = a.reshape(nc, HEAD_DIM, HEAD_DIM, SCAN_P)
    return jnp.concatenate([a[:, :SCAN_VH], a[:, SCAN_VH:]], -1)


def _scan_unlayout_state(st, b):
    nc = st.shape[0]
    a = jnp.concatenate([st[..., :SCAN_P], st[..., SCAN_P:]], axis=1)
    a = a.reshape(nc, HEAD_DIM, HEAD_DIM, RWKV_HEADS, SCAN_SEQS).transpose(0, 4, 3, 1, 2)
    return a.reshape(nc * SCAN_SEQS, RWKV_HEADS, HEAD_DIM, HEAD_DIM)[:b]


def _rwkv_scan(pre, s0, b, t, tc):
    r, w, k, v, kk, kka, _ = pre
    bp = -(-b // SCAN_SEQS) * SCAN_SEQS
    ops = [jnp.pad(x.reshape(b, t, RWKV_DIM), ((0, bp - b), (0, 0), (0, 0))) for x in (kk, w, kka, k, r, v)]
    s0l = _scan_layout_state(s0)
    nc = s0l.shape[0]
    kspec = pl.BlockSpec((SCAN_SEQS, tc, RWKV_DIM), lambda c, i: (c, i, 0))
    vspec = pl.BlockSpec((None, tc, SCAN_VH, LANE), lambda c, i: (c, i, 0, 0))
    sspec = pl.BlockSpec((None, SCAN_VH, HEAD_DIM, LANE), lambda c, i: (c, 0, 0, 0))
    y, so = pl.pallas_call(
        functools.partial(_scan_kernel, tc=tc),
        grid=(nc, t // tc),
        in_specs=[kspec] * 6 + [sspec],
        out_specs=[vspec, sspec],
        out_shape=[jax.ShapeDtypeStruct((nc, t, SCAN_VH, LANE), F32),
                   jax.ShapeDtypeStruct((nc, SCAN_VH, HEAD_DIM, LANE), F32)],
        scratch_shapes=[pltpu.VMEM((SCAN_VH, HEAD_DIM, LANE), F32)],
        compiler_params=_cparams("parallel", "arbitrary"),
        name="rwkv_scan",
    )(*ops, s0l)
    return _scan_unlayout_y(y, b, t), _scan_unlayout_state(so, b)


def _compress_kernel(x_ref, pe_ref, w_ref, kg_ref, gtwo_ref, o_ref):
    z = jnp.dot((x_ref[...] + pe_ref[...]).astype(BF16), w_ref[...], preferred_element_type=F32)
    o_ref[:, 0:LANE] = _group_rms(z[:, 0:LANE], gtwo_ref[...], kg_ref[...])
    o_ref[:, LANE:2 * LANE] = z[:, LANE:2 * LANE]


def _compress_weights(cmp_pe, cmp_w, k_norm_c):
    wk = cmp_w[0].reshape(CMP_BLOCK, HEAD_DIM, HEAD_DIM)
    wv = cmp_w[1].reshape(CMP_BLOCK, HEAD_DIM, HEAD_DIM)
    full = jnp.einsum("srde,st->rsdte", jnp.stack([wk, wk, wv, wv]), jnp.eye(4, dtype=F32))
    pe = jnp.stack([cmp_pe[0], cmp_pe[0], cmp_pe[1], cmp_pe[1]], axis=1)
    return (full.reshape(CMP_BLOCK * 4 * HEAD_DIM, 4 * HEAD_DIM).astype(BF16), pe.reshape(1, -1),
            jnp.tile(k_norm_c, 2).reshape(1, LANE))


def _compress(rows, cw, gtwo, tile):
    wfull, pe, kg = cw
    return _row_call(_compress_kernel, rows, [], [pe, wfull, kg, gtwo], [(2 * LANE, F32)], tile, 1, "nsa_compress")[0]


def _compress_paged_kernel(pt_ref, ident_ref, pe_ref, w_ref, kg_ref, gtwo_ref, *refs, grp, n_pages):
    del pt_ref
    page_refs, o_ref, xs = refs[:grp * n_pages], refs[-2], refs[-1]
    ident = ident_ref[...]
    for i in range(grp * n_pages):
        for half in range(2):
            xt = (page_refs[i][half * LANE:(half + 1) * LANE, :] + pe_ref[half]).astype(BF16)
            xs[half, i * PAGE_SIZE:(i + 1) * PAGE_SIZE, :] = _dot_nt(ident, xt)
    n_blk = grp * n_pages * (PAGE_SIZE // CMP_BLOCK)
    acc = [jnp.zeros((n_blk, LANE), F32) for _ in range(2)]
    for r in range(CMP_BLOCK):
        for half in range(2):
            rows = xs[half, pl.ds(r, n_blk, stride=CMP_BLOCK), :]
            acc[half] = acc[half] + jnp.dot(rows.astype(BF16), w_ref[half, r], preferred_element_type=F32)
    kc = _group_rms(acc[0], gtwo_ref[...], kg_ref[...])
    per_seq = n_blk // grp
    for g in range(grp):
        o_ref[g, 0:per_seq, 0:LANE] = kc[g * per_seq:(g + 1) * per_seq].astype(BF16)
        o_ref[g, 0:per_seq, LANE:2 * LANE] = acc[1][g * per_seq:(g + 1) * per_seq].astype(BF16)
        o_ref[g, per_seq:LANE, :] = jnp.zeros((LANE - per_seq, 2 * LANE), BF16)


def _compress_paged(cmp_t, page_table, cmp_pe, cmp_w, k_norm_c, gtwo, grp):
    b, n_pages = page_table.shape
    wk = cmp_w[0].reshape(CMP_BLOCK, HEAD_DIM, HEAD_DIM)
    wv = cmp_w[1].reshape(CMP_BLOCK, HEAD_DIM, HEAD_DIM)
    wbd = jnp.einsum("krde,ht->krhdte", jnp.stack([wk, wv]), jnp.eye(N_KV_HEADS, dtype=F32))
    wbd = wbd.reshape(2, CMP_BLOCK, LANE, LANE).astype(BF16)
    pe = jnp.tile(cmp_pe.transpose(0, 2, 1), (1, N_KV_HEADS, PAGE_SIZE // CMP_BLOCK))
    ident = jnp.eye(LANE, dtype=BF16)
    kg = jnp.tile(k_norm_c, 2).reshape(1, LANE)
    cspec = lambda a: pl.BlockSpec(a.shape, lambda i, pt: (0,) * a.ndim)
    kern = functools.partial(_compress_paged_kernel, grp=grp, n_pages=n_pages)
    return pl.pallas_call(
        kern,
        grid_spec=pltpu.PrefetchScalarGridSpec(
            num_scalar_prefetch=1, grid=(b // grp,),
            in_specs=[cspec(ident), cspec(pe), cspec(wbd), cspec(kg), cspec(gtwo)]
            + _page_specs(cmp_t, n_pages, grp),
            out_specs=pl.BlockSpec((grp, LANE, 2 * LANE), lambda i, pt: (i, 0, 0)),
            scratch_shapes=[pltpu.VMEM((2, grp * n_pages * PAGE_SIZE, LANE), F32)]),
        out_shape=jax.ShapeDtypeStruct((b, LANE, 2 * LANE), BF16),
        compiler_params=_cparams("parallel"),
        name="nsa_compress_paged",
    )(page_table, ident, pe, wbd, kg, gtwo, *([cmp_t] * (grp * n_pages)))


def _bias_cmp(rel_bias, q_starts, qb):
    q0 = jnp.asarray(q_starts, I32)[:, None, None]
    r = jnp.arange(qb)[None, :, None]
    n = jnp.arange(LANE)[None, None, :]
    return _bias_lookup(rel_bias, q0 + r - (n * CMP_BLOCK + CMP_BLOCK - 1)).transpose(0, 3, 1, 2)


def _nsa_sample_kernel(pt_ref, q_ref, gates_ref, kcv_ref, snew_ref, wnew_ref, win_ref, bias_ref, biasc_ref,
                       biasw_ref, pair_ref, *refs, grp, n_pages, t_pos, n_cmp, n_sel, w_eff):
    del pt_ref
    slc_refs, o_ref = refs[:grp * n_pages], refs[-1]
    lane1 = lax.broadcasted_iota(I32, (1, LANE), 1)
    lane8 = lax.broadcasted_iota(I32, (N_ATT_HEADS, LANE), 1)
    row_all = lax.broadcasted_iota(I32, (N_ATT_HEADS, (n_pages + 1) * LANE), 0)
    lanew = lax.broadcasted_iota(I32, (N_ATT_HEADS, w_eff), 1)
    lane1_f = lane1.astype(F32)
    bias = bias_ref[...]
    bias_now = bias[:, n_pages * LANE:n_pages * LANE + 1]
    mask_c = ((lane8 * CMP_BLOCK + CMP_BLOCK - 1) <= t_pos) & (lane8 < n_cmp)
    cur = t_pos // SEL_BLOCK
    forced = (lane1 == 0) | (lane1 == cur) | (lane1 == cur - 1)
    sel_causal = lane1 * SEL_BLOCK <= t_pos
    tail_valid = jnp.concatenate([lane1] * n_pages + [lane1 + LANE], axis=1) <= LANE
    for g in range(grp):
        q = q_ref[g]
        qf = q.astype(F32)
        s = jnp.where(mask_c, _dot_nt(q, kcv_ref[g, :, 0:LANE]) * ATT_SCALE + biasc_ref[...], NEG)
        e = jnp.exp(s - jnp.max(s, axis=1, keepdims=True))
        pc = jnp.where(mask_c, e / jnp.sum(e, axis=1, keepdims=True), 0.0)
        o_c = jnp.dot(pc.astype(BF16), kcv_ref[g, :, LANE:2 * LANE], preferred_element_type=F32)
        masks = []
        for hk in range(N_KV_HEADS):
            ps = jnp.sum(pc[hk * GQA:(hk + 1) * GQA], axis=0, keepdims=True)
            score = _dot_split(ps, pair_ref[...])
            score = jnp.where(sel_causal, jnp.where(forced, FORCE, score), NEG)
            picked = jnp.zeros((1, LANE), F32)
            for _ in range(n_sel):
                mx = jnp.max(score, axis=1, keepdims=True)
                first = jnp.min(jnp.where(score == mx, lane1_f, float(LANE)), axis=1, keepdims=True)
                hit = lane1_f == first
                picked = jnp.where(hit, 1.0, picked)
                score = jnp.where(hit, TAKEN, score)
            per_page = PAGE_SIZE // SEL_BLOCK
            tiles = []
            for p in range(n_pages + 1):
                t = jnp.zeros((1, LANE), F32)
                for a in range(per_page):
                    blk = picked[:, p * per_page + a:p * per_page + a + 1]
                    t = jnp.where(lane1 // SEL_BLOCK == a, blk, t)
                tiles.append(t)
            masks.append(jnp.concatenate(tiles, axis=1))
        valid = (jnp.where(row_all < GQA, masks[0], masks[1]) > 0.5) & tail_valid
        snew = _bf16_round(snew_ref[g])
        tiles = [_dot(q, slc_refs[g * n_pages + p][0:LANE, :]) for p in range(n_pages)]
        tiles.append(jnp.where(lane1 == 0, jnp.sum(qf * snew[:, 0:LANE], axis=1, keepdims=True), 0.0))
        s = jnp.where(valid, jnp.concatenate(tiles, axis=1) * ATT_SCALE + bias, NEG)
        e = jnp.exp(s - jnp.max(s, axis=1, keepdims=True))
        p_all = jnp.where(valid, e / jnp.sum(e, axis=1, keepdims=True), 0.0)
        o_s = p_all[:, n_pages * LANE:n_pages * LANE + 1] * snew[:, LANE:2 * LANE]
        for p in range(n_pages):
            o_s = o_s + _dot_nt(p_all[:, p * LANE:(p + 1) * LANE], slc_refs[g * n_pages + p][LANE:2 * LANE, :])
        wnew = _bf16_round(wnew_ref[g])
        valid_w = (w_eff - lanew) < WINDOW
        s_w = jnp.where(valid_w, _dot(q, win_ref[g, 0:LANE, :]) * ATT_SCALE + biasw_ref[...], NEG)
        s_n = jnp.sum(qf * wnew[:, 0:LANE], axis=1, keepdims=True) * ATT_SCALE + bias_now
        m = jnp.maximum(jnp.max(s_w, axis=1, keepdims=True), s_n)
        e_w = jnp.where(valid_w, jnp.exp(s_w - m), 0.0)
        e_n = jnp.exp(s_n - m)
        den = jnp.sum(e_w, axis=1, keepdims=True) + e_n
        o_w = _dot_nt(e_w / den, win_ref[g, LANE:2 * LANE, :]) + (e_n / den) * wnew[:, LANE:2 * LANE]
        gates = gates_ref[g]
        _head_rows_out(o_ref, g, gates[:, 0:1] * o_c + gates[:, 1:2] * o_s + gates[:, 2:3] * o_w)


def _nsa_sample(q8, gates8, kcv, snew, wnew, win_t, bias, bias_c, bias_w, pair, slc_t, page_table,
                *, t_pos, n_cmp, n_sel, grp):
    b, n_pages = page_table.shape
    w_eff = win_t.shape[2]
    gspec = lambda a: pl.BlockSpec((grp,) + a.shape[1:], lambda i, pt: (i,) + (0,) * (a.ndim - 1))
    cspec = lambda a: pl.BlockSpec(a.shape, lambda i, pt: (0,) * a.ndim)
    kern = functools.partial(_nsa_sample_kernel, grp=grp, n_pages=n_pages, t_pos=t_pos, n_cmp=n_cmp, n_sel=n_sel,
                             w_eff=w_eff)
    return pl.pallas_call(
        kern,
        grid_spec=pltpu.PrefetchScalarGridSpec(
            num_scalar_prefetch=1, grid=(b // grp,),
            in_specs=[gspec(q8), gspec(gates8), gspec(kcv), gspec(snew), gspec(wnew), gspec(win_t), cspec(bias),
                      cspec(bias_c), cspec(bias_w), cspec(pair)] + _page_specs(slc_t, n_pages, grp),
            out_specs=pl.BlockSpec((grp, N_ATT_HEADS, LANE), lambda i, pt: (i, 0, 0))),
        out_shape=jax.ShapeDtypeStruct((b, N_ATT_HEADS, LANE), BF16),
        compiler_params=_cparams("parallel"),
        name="nsa_sample",
    )(page_table, q8, gates8, kcv, snew, wnew, win_t, bias, bias_c, bias_w, pair, *([slc_t] * (grp * n_pages)))


def _post_odd_kernel(x_ref, y_ref, r_ref, k_ref, v_ref, g_ref, od_ref, gate_ref, shf_ref, scf_ref,
                     gf_ref, lnw_ref, lnb_ref, rk_ref, gtwo_ref, wo_ref, wrt_ref, br_ref, x2_ref, hf_ref, cwt_ref):
    gtwo = gtwo_ref[...]
    mix = jnp.dot(od_ref[...], wo_ref[RWKV_DIM:RWKV_DIM + N_ATT_HEADS * LANE, :], preferred_element_type=F32)
    for t in range(RWKV_DIM // LANE):
        sl = slice(t * LANE, (t + 1) * LANE)
        y = y_ref[:, sl]
        dlt = y - _dot_split(y, gtwo)
        yn = (dlt * lax.rsqrt(_dot_split(dlt * dlt, gtwo) + GN_EPS)) * lnw_ref[:, sl] + lnb_ref[:, sl]
        dot_rk = _dot_split(r_ref[:, sl] * k_ref[:, sl] * rk_ref[:, sl], gtwo) * float(HEAD_DIM)
        oc = (yn + dot_rk * v_ref[:, sl]) * g_ref[:, sl]
        mix = mix + jnp.dot(oc.astype(BF16), wo_ref[sl, :], preferred_element_type=F32)
    _post_tail(x_ref[...], mix, gate_ref[...], gf_ref[...], shf_ref[...], scf_ref[...], wrt_ref, br_ref,
               x2_ref, hf_ref, cwt_ref)


def _odd_layer(xp, xs, mp, ms, page_table, wkv0, shift0, cmp_pool, slc_pool, win_buf, rel_bias, router, experts,
               g_mix, g_ffn, w_in, w_out, cpar, r_k, ln_w, ln_b, q_norm, k_norm, cmp_pe, cmp_w):
    B, S, D = xp.shape
    Bs = xs.shape[0]
    n_pages = page_table.shape[1]
    past = n_pages * PAGE_SIZE
    wts = _odd_weights(w_in, w_out, q_norm, k_norm)
    gm = _gmats()
    gone, gtwo, _ = gm
    gsum = (gtwo.astype(F32) * HEAD_DIM).astype(BF16)
    i = jnp.arange(LANE)
    pair = jnp.where(i[:, None] // 2 == i[None, :], 1.0, 0.0).astype(BF16)
    cw = _compress_weights(cmp_pe, cmp_w, k_norm[0])
    wrt, br = router
    wg, wu, wd, lyr = experts
    g_mix = g_mix.reshape(1, D)
    post_consts = [g_ffn.reshape(1, D), ln_w.reshape(1, -1), ln_b.reshape(1, -1), r_k.reshape(1, -1), gtwo, wts[1],
                   wrt, br]
    w_eff = win_buf.shape[1]

    tp = ROW_TILE
    zc, q_p, kvc, _, kvs_b, _, kvw_b, gates, kvc_t, kvs_t, kvw_t = _inproj_odd(xp.reshape(B * S, D), [mp[0], mp[1]],
                                                                               g_mix, wts, gm, tp, S // tp)
    r3 = lambda a: a.reshape(B, S, a.shape[-1])
    pre = _rwkv_pre(zc, None, cpar, gsum, tp, S // tp)
    y, wkv_p = _rwkv_scan(pre, jnp.zeros((B, RWKV_HEADS, HEAD_DIM, HEAD_DIM), F32), B, S, SCAN_TIME_CHUNK)
    n_cmp = S // CMP_BLOCK
    kcv = _compress(kvc.reshape(B * n_cmp, CMP_BLOCK * 2 * LANE), cw, gtwo, _pick_tile(B * n_cmp, 256))
    kcv = jnp.pad(kcv.reshape(B, n_cmp, 2 * LANE), ((0, 0), (0, LANE - n_cmp), (0, 0))).astype(BF16)
    n_slc = -(-S // SEL_BLOCK)
    od = _nsa_t(r3(q_p), r3(gates), kcv, r3(kvs_b), r3(kvw_b), _bias_tiles_t(rel_bias), _bias_cmp_t(rel_bias, S),
                pair.T, n_cmp=n_cmp, n_sel=min(N_SEL_BLOCKS, n_slc))
    x2, hf, cwt = _row_call(_post_odd_kernel,
                            [xp.reshape(B * S, D), y, pre[0], pre[2], pre[3], pre[6], od.reshape(B * S, -1)],
                            [mp[2], mp[3], mp[4]], post_consts, POST_OUTS, tp, S // tp, "post_odd",
                            outs_t=[(N_EXPERTS, F32)])
    tm = min(MOE_TILE, S)
    xp3 = _moe_sorted(hf, cwt, x2, mp[5], wg, wu, wd, lyr, tm, S // tm).reshape(B, S, D)
    kv5 = lambda a, n: a.reshape(-1, n, 2, N_KV_HEADS, HEAD_DIM)
    outs_p = (wkv_p, r3(zc)[:, S - 1], _cache_rows(kvc_t), _cache_rows(kvs_t),
              _cache_rows(kvw_t[:, :, S - min(WINDOW, S):]))

    zc_s, q_s, kvc_s, kvs_s, _, kvw_s, _, gates_s, _, _, _ = _inproj_odd(xs, [ms[0], ms[1]], g_mix, wts, gm, Bs, 1)
    pre_s = _rwkv_pre(zc_s, shift0, cpar, gsum, Bs)
    y_s, wkv_s = _rwkv_scan(pre_s, wkv0, Bs, 1, 1)
    n_pool = cmp_pool.shape[0]
    cmp_t = cmp_pool.transpose(0, 2, 3, 4, 1).reshape(n_pool, 2 * LANE, PAGE_SIZE)
    kcv_s = _compress_paged(cmp_t, page_table, cmp_pe, cmp_w, k_norm[0], gtwo, _pick_group(Bs, SAMPLE_GROUP))
    n_cmp_s = (past + 1) // CMP_BLOCK
    slc_t = slc_pool.transpose(0, 2, 3, 4, 1).reshape(n_pool, 2 * LANE, PAGE_SIZE)
    win_t = win_buf.transpose(0, 2, 3, 4, 1).reshape(Bs, 2 * LANE, w_eff)
    gates8 = jnp.pad(gates_s[:, :3 * N_ATT_HEADS].reshape(Bs, 3, N_ATT_HEADS).transpose(0, 2, 1),
                     ((0, 0), (0, 0), (0, LANE - 3)))
    n_slc_s = -(-(past + 1) // SEL_BLOCK)
    od_s = _nsa_sample(q_s.reshape(Bs, N_ATT_HEADS, LANE), gates8, kcv_s, kvs_s[:, None, :], kvw_s[:, None, :],
                       win_t, _bias_row(rel_bias, past), _bias_cmp(rel_bias, [past], 1)[0, :, 0, :],
                       _bias_lookup(rel_bias, w_eff - jnp.arange(w_eff)).T, pair, slc_t, page_table,
                       t_pos=past, n_cmp=n_cmp_s, n_sel=min(N_SEL_BLOCKS, n_slc_s),
                       grp=_pick_group(Bs, SAMPLE_GROUP)).reshape(Bs, -1)
    x2s, hfs, cwts = _row_call(_post_odd_kernel, [xs, y_s, pre_s[0], pre_s[2], pre_s[3], pre_s[6], od_s],
                               [ms[2], ms[3], ms[4]], post_consts, POST_OUTS, Bs, 1, "post_odd_s",
                               outs_t=[(N_EXPERTS, F32)])
    xs3 = _moe(hfs, cwts.T, x2s, ms[5], wg, wu, wd, lyr, Bs, 1)
    win_new = jnp.concatenate([win_buf[:, 1:], kv5(kvw_s, 1)], axis=1)
    outs_s = (wkv_s, zc_s, kv5(kvc_s, 1), kv5(kvs_s, 1), win_new)
    return xp3, xs3, outs_p, outs_s


def _mods(c_p, c_s, w_all, b, layer):
    nb = c_p.shape[0]
    m = _ada(jnp.concatenate([c_p, c_s], 0), w_all, b, layer)
    parts = jnp.split(m, 6, axis=-1)
    return [p[:nb, None, :] for p in parts], [p[None, nb:, :] for p in parts]


def _forward(x_prompt, x_sample, c_prompt, c_sample, page_table, cache_a_kv, cache_a_kidx, state_b_conv,
             state_c_wkv, state_c_shift, cache_d_cmp, cache_d_slc, cache_d_win, rel_bias, w_router, b_router,
             w_ada, b_ada, g_norm_mix, g_norm_ffn, w_expert_gate, w_expert_up, w_expert_down, e_w_in, e_w_out,
             a_q_norm, a_k_norm, b_conv_w, b_conv_b, o_w_in, o_w_out, c_mu, c_w0, c_w_up, c_a0, c_a_up,
             c_g_up, c_k_k, c_k_a, c_r_k, c_ln_w, c_ln_b, d_q_norm, d_k_norm, d_cmp_pe, d_cmp_w):
    assert w_ada.shape[0] == 2 and e_w_in.shape[0] == 1 and o_w_in.shape[0] == 1
    B, S, D = x_prompt.shape
    Bs = x_sample.shape[0]
    assert x_sample.shape[1] == 1
    xp, xs = x_prompt, x_sample.reshape(Bs, D)
    router = (w_router.T, b_router.reshape(N_EXPERTS, 1))
    n_pool = cache_a_kv.shape[1]
    wbf = tuple(w.astype(BF16) for w in (w_expert_gate, w_expert_up, w_expert_down))
    experts = lambda l: wbf + (l,)

    mp, ms = _mods(c_prompt, c_sample, w_ada, b_ada[0], 0)
    xp, xs, ep, es = _even_layer(xp, xs, mp, ms, page_table, cache_a_kv[0], cache_a_kidx[0], state_b_conv[0],
                                 rel_bias, router, experts(0), g_norm_mix[0], g_norm_ffn[0], e_w_in[0], e_w_out[0],
                                 a_q_norm[0], a_k_norm[0], b_conv_w[0], b_conv_b[0])
    mp, ms = _mods(c_prompt, c_sample, w_ada, b_ada[1], 1)
    cpar = (c_mu[0], c_w0[0], c_w_up[0], c_a0[0], c_a_up[0], c_g_up[0], c_k_k[0], c_k_a[0])
    xp, xs, op, os_ = _odd_layer(xp, xs, mp, ms, page_table, state_c_wkv[0], state_c_shift[0], cache_d_cmp[0],
                                 cache_d_slc[0], cache_d_win[0], rel_bias, router, experts(1), g_norm_mix[1],
                                 g_norm_ffn[1], o_w_in[0], o_w_out[0], cpar, c_r_k[0].reshape(-1), c_ln_w[0],
                                 c_ln_b[0], d_q_norm[0], d_k_norm[0], d_cmp_pe[0], d_cmp_w[0])
    stack = lambda ts: tuple(a[None] for a in ts)
    return (xp, xs.reshape(Bs, 1, D)) + stack(ep) + stack(op) + stack(es) + stack(os_)


def kernel(x_prompt, x_sample, c_prompt, c_sample, page_table, cache_a_kv, cache_a_kidx, state_b_conv, state_c_wkv, state_c_shift, cache_d_cmp, cache_d_slc, cache_d_win, rel_bias, w_router, b_router, w_ada, b_ada, g_norm_mix, g_norm_ffn, w_expert_gate, w_expert_up, w_expert_down, e_w_in, e_w_out, a_q_norm, a_k_norm, b_conv_w, b_conv_b, o_w_in, o_w_out, c_mu, c_w0, c_w_up, c_a0, c_a_up, c_g_up, c_k_k, c_k_a, c_r_k, c_ln_w, c_ln_b, d_q_norm, d_k_norm, d_cmp_pe, d_cmp_w):
    return _forward(x_prompt, x_sample, c_prompt, c_sample, page_table, cache_a_kv, cache_a_kidx, state_b_conv,
                    state_c_wkv, state_c_shift, cache_d_cmp, cache_d_slc, cache_d_win, rel_bias, w_router, b_router,
                    w_ada, b_ada, g_norm_mix, g_norm_ffn, w_expert_gate, w_expert_up, w_expert_down, e_w_in, e_w_out,
                    a_q_norm, a_k_norm, b_conv_w, b_conv_b, o_w_in, o_w_out, c_mu, c_w0, c_w_up, c_a0, c_a_up,
                    c_g_up, c_k_k, c_k_a, c_r_k, c_ln_w, c_ln_b, d_q_norm, d_k_norm, d_cmp_pe, d_cmp_w)
```
